```python
import jax
import jax.numpy as jnp
from jax import lax
import numpy as np

D_MODEL = 2048
BATCH = 2
SEQ = 8192
DEPTH = 1

PLE_DIM = 256
NORM_EPS = 1e-6

RWKV_HEADS = 16
RWKV_HEAD = 64
RWKV_WIDTH = RWKV_HEADS * RWKV_HEAD
DECAY_LORA = 64
AAA_LORA = 64
GATE_LORA = 160
GN_EPS = 64e-5

MLA_HEADS = 8
Q_LORA = 512
KV_LORA = 512
QK_NOPE = 128
QK_ROPE = 64
QK_HEAD = QK_NOPE + QK_ROPE
V_HEAD = 128
MLA_WIDTH = MLA_HEADS * V_HEAD
ROPE_THETA = 10000.0
Q_BLOCK = 128

N_EXPERTS = 64
TOP_K = 8
N_GROUPS = 8
TOPK_GROUPS = 4
MOE_INTER = 512
SHARED_INTER = 512
ROUTED_SCALE = 2.5
EXPERT_BLOCK = 128

RWKV_COLS = 3 * RWKV_WIDTH + DECAY_LORA + AAA_LORA + GATE_LORA
MLA_COLS = Q_LORA + KV_LORA + QK_ROPE
GATE_COLS = 2 * D_MODEL
IN_COLS = RWKV_COLS + MLA_COLS + GATE_COLS

kernel_name = 'hybrid_rwkv7_mla_moe_ple'


def rms_norm(x, g, eps=NORM_EPS):
    xf = x.astype(jnp.float32)
    y = xf * lax.rsqrt(jnp.mean(xf * xf, axis=-1, keepdims=True) + eps)
    return (y * g.astype(jnp.float32)).astype(x.dtype)


def token_shift_mix(u, mu):
    prev = jnp.pad(u, ((0, 0), (1, 0), (0, 0)))[:, :-1]
    return u + (prev - u) * mu


def rwkv7_time_mix(u, w0, w2, a0, a2, g2, k_k, k_a, r_k, gn_w, gn_b):
    B, S, _ = u.shape
    H, N = RWKV_HEADS, RWKV_HEAD
    f32 = jnp.float32
    cuts = [RWKV_WIDTH, 2 * RWKV_WIDTH, 3 * RWKV_WIDTH,
            3 * RWKV_WIDTH + DECAY_LORA, 3 * RWKV_WIDTH + DECAY_LORA + AAA_LORA]
    r, k, v, dw, da, dg = jnp.split(u.astype(f32), cuts, axis=-1)
    w_log = -jax.nn.softplus(-(w0 + jnp.tanh(dw) @ w2)) - 0.5
    decay = jnp.exp(-jnp.exp(w_log))
    a = jax.nn.sigmoid(a0 + da @ a2)
    g = jax.nn.sigmoid(dg) @ g2
    heads = lambda t: t.reshape(B, S, H, N)
    r, k, v, a, decay = heads(r), heads(k), heads(v), heads(a), heads(decay)
    kk = k * k_k
    kk = kk / jnp.maximum(jnp.linalg.norm(kk, axis=-1, keepdims=True), 1e-12)
    k = k * (1.0 + (a - 1.0) * k_a)

    def step(state, inp):
        r_t, w_t, k_t, v_t, kk_t, a_t = inp
        sk = jnp.einsum('bhij,bhj->bhi', state, kk_t)
        state = (state * w_t[:, :, None, :]
                 - sk[..., None] * (kk_t * a_t)[:, :, None, :]
                 + v_t[..., None] * k_t[:, :, None, :])
        return state, jnp.einsum('bhij,bhj->bhi', state, r_t)

    xs = tuple(jnp.moveaxis(t, 1, 0) for t in (r, decay, k, v, kk, a))
    _, ys = lax.scan(step, jnp.zeros((B, H, N, N), f32), xs)
    y = jnp.moveaxis(ys, 0, 1)
    mean = jnp.mean(y, axis=-1, keepdims=True)
    var = jnp.mean(jnp.square(y - mean), axis=-1, keepdims=True)
    y = (y - mean) * lax.rsqrt(var + GN_EPS) * gn_w.reshape(H, N) + gn_b.reshape(H, N)
    y = y + jnp.sum(r * k * r_k, axis=-1, keepdims=True) * v
    return (y.reshape(B, S, RWKV_WIDTH) * g).astype(u.dtype)


def rope(t, positions):
    half = QK_ROPE // 2
    inv_freq = ROPE_THETA ** (-jnp.arange(half, dtype=jnp.float32) / half)
    ang = positions.astype(jnp.float32)[..., None] * inv_freq
    cos = jnp.cos(ang)[:, :, None, :]
    sin = jnp.sin(ang)[:, :, None, :]
    tf = t.astype(jnp.float32)
    t1, t2 = tf[..., :half], tf[..., half:]
    return jnp.concatenate([t1 * cos - t2 * sin, t1 * sin + t2 * cos], axis=-1).astype(t.dtype)


def causal_attention(q, k, v):
    B, S, H, Dh = q.shape
    nb = S // Q_BLOCK
    scale = Dh ** -0.5
    kf = k.astype(jnp.float32)
    vf = v.astype(jnp.float32)
    qb = jnp.moveaxis(q.astype(jnp.float32).reshape(B, nb, Q_BLOCK, H, Dh), 1, 0)
    key_pos = jnp.arange(S)

    def block(args):
        q_blk, start = args
        s = jnp.einsum('bqhd,bkhd->bhqk', q_blk, kf) * scale
        causal = key_pos[None, :] <= (start + jnp.arange(Q_BLOCK))[:, None]
        s = jnp.where(causal, s, -jnp.inf)
        return jnp.einsum('bhqk,bkhd->bqhd', jax.nn.softmax(s, axis=-1), vf)

    out = lax.map(block, (qb, jnp.arange(nb) * Q_BLOCK))
    return jnp.moveaxis(out, 0, 1).reshape(B, S, H, v.shape[-1]).astype(v.dtype)


def mla(cq, ckv, k_rope, positions, g_qa, g_kva, w_uq, w_ukv, g_qn, g_kn):
    B, S, _ = cq.shape
    H = MLA_HEADS
    q = (rms_norm(cq, g_qa) @ w_uq).reshape(B, S, H, QK_HEAD)
    kv = (rms_norm(ckv, g_kva) @ w_ukv).reshape(B, S, H, QK_NOPE + V_HEAD)
    k_nope, v = kv[..., :QK_NOPE], kv[..., QK_NOPE:]
    k = jnp.concatenate([k_nope, jnp.broadcast_to(k_rope[:, :, None, :], (B, S, H, QK_ROPE))], axis=-1)
    q = rms_norm(q, g_qn)
    k = rms_norm(k, g_kn)
    q = jnp.concatenate([q[..., :QK_NOPE], rope(q[..., QK_NOPE:], positions)], axis=-1)
    k = jnp.concatenate([k[..., :QK_NOPE], rope(k[..., QK_NOPE:], positions)], axis=-1)
    return causal_attention(q, k, v).reshape(B, S, MLA_WIDTH)


def swiglu(h, w_gate, w_up, w_down):
    return (jax.nn.silu(h @ w_gate) * (h @ w_up)) @ w_down


def moe_route(h, w_router, router_bias):
    T = h.shape[0]
    scores = jax.nn.sigmoid(h.astype(jnp.float32) @ w_router.astype(jnp.float32))
    choice = scores + router_bias.astype(jnp.float32)
    grp = choice.reshape(T, N_GROUPS, N_EXPERTS // N_GROUPS)
    grp_score = jnp.sum(lax.top_k(grp, 2)[0], axis=-1)
    _, gidx = lax.top_k(grp_score, TOPK_GROUPS)
    gmask = jnp.any(gidx[..., None] == jnp.arange(N_GROUPS), axis=-2)
    emask = jnp.repeat(gmask, N_EXPERTS // N_GROUPS, axis=-1)
    _, eidx = lax.top_k(jnp.where(emask, choice, -jnp.inf), TOP_K)
    wts = jnp.take_along_axis(scores, eidx, axis=-1)
    wts = wts / (jnp.sum(wts, axis=-1, keepdims=True) + 1e-20) * ROUTED_SCALE
    return eidx, wts


def moe_routed(h, eidx, wts, w_gate, w_up, w_down):
    T, Dm = h.shape
    E, BLK = N_EXPERTS, EXPERT_BLOCK
    A = T * TOP_K
    flat_e = eidx.reshape(-1).astype(jnp.int32)
    flat_tok = jnp.repeat(jnp.arange(T, dtype=jnp.int32), TOP_K)
    flat_w = wts.reshape(-1)
    order = jnp.argsort(flat_e)
    se, stok, sw = flat_e[order], flat_tok[order], flat_w[order]
    counts = jnp.bincount(flat_e, length=E).astype(jnp.int32)
    starts = jnp.cumsum(counts) - counts
    pcounts = (counts + BLK - 1) // BLK * BLK
    pends = jnp.cumsum(pcounts)
    pstarts = pends - pcounts
    dest = pstarts[se] + jnp.arange(A, dtype=jnp.int32) - starts[se]
    P = ((A + BLK - 1) // BLK + E) * BLK
    nblk = P // BLK
    row_tok = jnp.full((P,), T, jnp.int32).at[dest].set(stok)
    row_w = jnp.zeros((P,), jnp.float32).at[dest].set(sw)
    blk_e = jnp.clip(jnp.searchsorted(pends, jnp.arange(nblk, dtype=jnp.int32) * BLK, side='right'), 0, E - 1)
    h_pad = jnp.concatenate([h, jnp.zeros((1, Dm), h.dtype)], axis=0)

    def body(acc, blk):
        tok, w, e = blk
        xb = h_pad[tok]
        y = swiglu(xb, w_gate[e], w_up[e], w_down[e])
        return acc.at[tok].add((y * w[:, None]).astype(acc.dtype)), None

    acc, _ = lax.scan(body, jnp.zeros((T + 1, Dm), h.dtype),
                      (row_tok.reshape(nblk, BLK), row_w.reshape(nblk, BLK), blk_e))
    return acc[:T]


def setup_inputs(seed: int = 0) -> dict:
    key = jax.random.key(seed)
    counter = [0]
    f32 = jnp.float32

    def nk():
        counter[0] += 1
        return jax.random.fold_in(key, counter[0])

    def nrm(shape, scale=1.0):
        return jax.random.normal(nk(), shape, f32) * scale

    def gain(shape):
        return 1.0 + 0.02 * jax.random.normal(nk(), shape, f32)

    L, D = DEPTH, D_MODEL
    x = nrm((BATCH, SEQ, D))
    p = nrm((L, BATCH, SEQ, PLE_DIM))
    offsets = jax.random.randint(nk(), (BATCH, 1), 0, 1024, dtype=jnp.int32)
    positions = offsets + jnp.arange(SEQ, dtype=jnp.int32)[None, :]
    return {
        'x': x,
        'p': p,
        'positions': positions,
        'g_mix': gain((L, D)),
        'w_in': nrm((L, D, IN_COLS), D ** -0.5),
        'mu_rwkv': jax.random.uniform(nk(), (L, RWKV_COLS), f32),
        'w0': jax.random.uniform(nk(), (L, RWKV_WIDTH), f32, -6.0, 1.0),
        'w2': nrm((L, DECAY_LORA, RWKV_WIDTH), 0.5 * DECAY_LORA ** -0.5),
        'a0': nrm((L, RWKV_WIDTH), 0.5),
        'a2': nrm((L, AAA_LORA, RWKV_WIDTH), AAA_LORA ** -0.5),
        'g2': nrm((L, GATE_LORA, RWKV_WIDTH), GATE_LORA ** -0.5),
        'k_k': 1.0 + 0.1 * nrm((L, RWKV_HEADS, RWKV_HEAD)),
        'k_a': 1.0 + 0.1 * nrm((L, RWKV_HEADS, RWKV_HEAD)),
        'r_k': nrm((L, RWKV_HEADS, RWKV_HEAD), 0.1),
        'gn_w': gain((L, RWKV_WIDTH)),
        'gn_b': nrm((L, RWKV_WIDTH), 0.02),
        'w_a_up': nrm((L, RWKV_WIDTH, D), RWKV_WIDTH ** -0.5),
        'g_qa': gain((L, Q_LORA)),
        'g_kva': gain((L, KV_LORA)),
        'w_uq': nrm((L, Q_LORA, MLA_HEADS * QK_HEAD), Q_LORA ** -0.5),
        'w_ukv': nrm((L, KV_LORA, MLA_HEADS * (QK_NOPE + V_HEAD)), KV_LORA ** -0.5),
        'g_qn': gain((L, QK_HEAD)),
        'g_kn': gain((L, QK_HEAD)),
        'w_b_up': nrm((L, MLA_WIDTH, D), MLA_WIDTH ** -0.5),
        'w_o': nrm((L, D, D), D ** -0.5),
        'g_ffn': gain((L, D)),
        'w_router': nrm((L, D, N_EXPERTS), D ** -0.5),
        'router_bias': nrm((L, N_EXPERTS), 0.01),
        'w_exp_gate': nrm((L, N_EXPERTS, D, MOE_INTER), D ** -0.5),
        'w_exp_up': nrm((L, N_EXPERTS, D, MOE_INTER), D ** -0.5),
        'w_exp_down': nrm((L, N_EXPERTS, MOE_INTER, D), MOE_INTER ** -0.5),
        'w_sh_gate': nrm((L, D, SHARED_INTER), D ** -0.5),
        'w_sh_up': nrm((L, D, SHARED_INTER), D ** -0.5),
        'w_sh_down': nrm((L, SHARED_INTER, D), SHARED_INTER ** -0.5),
        'w_ple': nrm((L, PLE_DIM, D), PLE_DIM ** -0.5),
        'g_ple_post': gain((L, D)),
        'g_ple_in': gain((L, D)),
        'w_ple_gate': nrm((L, D, D), D ** -0.5),
    }


def reference(x, p, positions, g_mix, w_in, mu_rwkv, w0, w2, a0, a2, g2, k_k, k_a, r_k,
              gn_w, gn_b, w_a_up, g_qa, g_kva, w_uq, w_ukv, g_qn, g_kn, w_b_up, w_o,
              g_ffn, w_router, router_bias, w_exp_gate, w_exp_up, w_exp_down,
              w_sh_gate, w_sh_up, w_sh_down, w_ple, g_ple_post, g_ple_in, w_ple_gate):
    B, S, Dm = x.shape
    for i in range(DEPTH):
        h = rms_norm(x, g_mix[i])
        proj = h @ w_in[i]
        u_rwkv, u_mla, gate_a, gate_b = jnp.split(
            proj, [RWKV_COLS, RWKV_COLS + MLA_COLS, RWKV_COLS + MLA_COLS + D_MODEL], axis=-1)
        u_rwkv = token_shift_mix(u_rwkv, mu_rwkv[i])
        o_a = rwkv7_time_mix(u_rwkv, w0[i], w2[i], a0[i], a2[i], g2[i], k_k[i], k_a[i],
                             r_k[i], gn_w[i], gn_b[i])
        cq, ckv, k_rope = jnp.split(u_mla, [Q_LORA, Q_LORA + KV_LORA], axis=-1)
        o_b = mla(cq, ckv, k_rope, positions, g_qa[i], g_kva[i], w_uq[i], w_ukv[i],
                  g_qn[i], g_kn[i])
        merged = (jax.nn.sigmoid(gate_a) * (o_a @ w_a_up[i])
                  + jax.nn.sigmoid(gate_b) * (o_b @ w_b_up[i]))
        x = x + merged @ w_o[i]
        h2 = rms_norm(x, g_ffn[i]).reshape(B * S, Dm)
        eidx, wts = moe_route(h2, w_router[i], router_bias[i])
        moe = (swiglu(h2, w_sh_gate[i], w_sh_up[i], w_sh_down[i])
               + moe_routed(h2, eidx, wts, w_exp_gate[i], w_exp_up[i], w_exp_down[i]))
        x = x + moe.reshape(B, S, Dm)
        ple = rms_norm(p[i] @ w_ple[i], g_ple_post[i])
        ple_gate = jax.nn.sigmoid(rms_norm(x, g_ple_in[i]) @ w_ple_gate[i])
        x = x + ple_gate * ple
    return x
```

```python
import functools

import jax
import jax.numpy as jnp
from jax import lax
from jax.experimental import pallas as pl
from jax.experimental.pallas import tpu as pltpu

F32 = jnp.float32
BF16 = jnp.bfloat16
I32 = jnp.int32
U32 = jnp.uint32

D_MODEL = 2048
PLE_DIM = 256
NORM_EPS = 1e-6
RWKV_HEADS = 16
RWKV_HEAD = 64
RWKV_WIDTH = RWKV_HEADS * RWKV_HEAD
DECAY_LORA = 64
AAA_LORA = 64
GATE_LORA = 160
GN_EPS = 64e-5
MLA_HEADS = 8
Q_LORA = 512
KV_LORA = 512
QK_NOPE = 128
QK_ROPE = 64
QK_HEAD = QK_NOPE + QK_ROPE
V_HEAD = 128
MLA_WIDTH = MLA_HEADS * V_HEAD
ROPE_THETA = 10000.0
N_EXPERTS = 64
TOP_K = 8
N_GROUPS = 8
GROUP_SIZE = N_EXPERTS // N_GROUPS
TOPK_GROUPS = 4
MOE_INTER = 512
ROUTED_SCALE = 2.5

LANES = 128
QK_PAD = 2 * LANES
PACK_ROWS = 8
PACK_W = D_MODEL // 2 // PACK_ROWS
assert PACK_W == LANES

LORA_PAD = 128
GATE_LORA_PAD = 256
RWKV_PAD = 3 * RWKV_WIDTH + 2 * LORA_PAD + GATE_LORA_PAD
MLA_PAD = 1536
COL_RWKV = 0
COL_MLA = RWKV_PAD
COL_GA = COL_MLA + MLA_PAD
COL_GB = COL_GA + D_MODEL
IN_PAD = COL_GB + D_MODEL

VMEM_LIMIT = 56 * 1024 * 1024
CHUNK = 64


def _cparams(sem):
    return pltpu.CompilerParams(dimension_semantics=sem, vmem_limit_bytes=VMEM_LIMIT)


def _dot(a, b):
    return jnp.dot(a, b, preferred_element_type=F32)


def _dot_nt(a, b):
    return lax.dot_general(a, b, (((1,), (1,)), ((), ())), preferred_element_type=F32)


def _dot_tn(a, b):
    return lax.dot_general(a, b, (((0,), (0,)), ((), ())), preferred_element_type=F32)


def _split_bf16(x):
    hi = x.astype(BF16)
    lo = (x - hi.astype(F32)).astype(BF16)
    return hi, lo


def _rms(x, g):
    ms = jnp.mean(x * x, axis=-1, keepdims=True)
    return x * lax.rsqrt(ms + NORM_EPS) * g


def _sigmoid(x):
    return 1.0 / (1.0 + jnp.exp(-x))


def _inproj_kernel(x_ref, g_ref, w_ref, o_ref, h_scr):
    @pl.when(pl.program_id(1) == 0)
    def _():
        h_scr[...] = _rms(x_ref[...], g_ref[...]).astype(BF16)

    o_ref[...] = _dot(h_scr[...], w_ref[...]).astype(o_ref.dtype)


def _inproj(x2, g, w, tm, tn):
    T, D = x2.shape
    N = w.shape[1]
    return pl.pallas_call(
        _inproj_kernel,
        grid=(T // tm, N // tn),
        in_specs=[
            pl.BlockSpec((tm, D), lambda i, j: (i, 0)),
            pl.BlockSpec((1, D), lambda i, j: (0, 0)),
            pl.BlockSpec((D, tn), lambda i, j: (0, j)),
        ],
        out_specs=pl.BlockSpec((tm, tn), lambda i, j: (i, j)),
        out_shape=jax.ShapeDtypeStruct((T, N), BF16),
        scratch_shapes=[pltpu.VMEM((tm, D), BF16)],
        compiler_params=_cparams(("parallel", "arbitrary")),
        name="inproj",
    )(x2, g, w)


HALO = 16


def _rwkv_prep_kernel(cur_ref, halo_ref, mu_ref, w0_ref, w2_ref, a0_ref, a2_ref, g2_ref,
                      kk_ref, ka_ref, bd_ref,
                      r_out, k_out, v_out, kk_out, a_out, lw_out, g_out, *, tm, seq):
    i = pl.program_id(0)
    cur = cur_ref[...].astype(F32)
    first = (i * tm) % seq == 0
    last_prev = halo_ref[HALO - 1:HALO, :].astype(F32)
    last_prev = jnp.where(first, 0.0, last_prev)
    row = lax.broadcasted_iota(I32, cur.shape, 0)
    prev = jnp.where(row == 0, last_prev, pltpu.roll(cur, 1, 0))
    u = cur + (prev - cur) * mu_ref[...]
    W = RWKV_WIDTH
    r = u[:, 0:W]
    k = u[:, W:2 * W]
    v = u[:, 2 * W:3 * W]
    dw = u[:, 3 * W:3 * W + LORA_PAD]
    da = u[:, 3 * W + LORA_PAD:3 * W + 2 * LORA_PAD]
    dg = u[:, 3 * W + 2 * LORA_PAD:3 * W + 2 * LORA_PAD + GATE_LORA_PAD]
    z = w0_ref[...] + _dot(jnp.tanh(dw).astype(BF16), w2_ref[...])
    w_log = -(jnp.maximum(-z, 0.0) + jnp.log(1.0 + jnp.exp(-jnp.abs(z)))) - 0.5
    lw_out[...] = -jnp.exp(w_log)
    a = _sigmoid(a0_ref[...] + _dot(da.astype(BF16), a2_ref[...]))
    g_out[...] = _dot(_sigmoid(dg).astype(BF16), g2_ref[...]).astype(g_out.dtype)
    kk = k * kk_ref[...]
    hi, lo = _split_bf16(kk * kk)
    ssq = _dot(hi, bd_ref[...]) + _dot(lo, bd_ref[...])
    kk = kk / jnp.maximum(jnp.sqrt(ssq), 1e-12)
    r_out[...] = r.astype(r_out.dtype)
    k_out[...] = (k * (1.0 + (a - 1.0) * ka_ref[...])).astype(k_out.dtype)
    v_out[...] = v.astype(v_out.dtype)
    kk_out[...] = kk.astype(kk_out.dtype)
    a_out[...] = a.astype(a_out.dtype)


def _rwkv_prep(proj, mu, w0, w2, a0, a2, g2, k_k, k_a, bd, tm, seq):
    T = proj.shape[0]
    W = RWKV_WIDTH
    full = lambda shape: pl.BlockSpec(shape, lambda i: (0, 0))
    out = lambda: pl.BlockSpec((tm, W), lambda i: (i, 0))
    return pl.pallas_call(
        functools.partial(_rwkv_prep_kernel, tm=tm, seq=seq),
        grid=(T // tm,),
        in_specs=[
            pl.BlockSpec((tm, RWKV_PAD), lambda i: (i, 0)),
            pl.BlockSpec((HALO, RWKV_PAD), lambda i: (jnp.maximum(i * (tm // HALO) - 1, 0), 0)),
            full((1, RWKV_PAD)), full((1, W)), full((LORA_PAD, W)), full((1, W)),
            full((LORA_PAD, W)), full((GATE_LORA_PAD, W)), full((1, W)), full((1, W)),
            full((W, W)),
        ],
        out_specs=[out() for _ in range(7)],
        out_shape=[jax.ShapeDtypeStruct((T, W), dt)
                   for dt in (BF16, BF16, BF16, BF16, BF16, F32, BF16)],
        compiler_params=_cparams(("parallel",)),
        name="rwkv_prep",
    )(proj, proj, mu, w0, w2, a0, a2, g2, k_k, k_a, bd)


def _rwkv_scan_kernel(r_ref, k_ref, v_ref, kk_ref, a_ref, lw_ref, g_ref,
                      gnw_ref, gnb_ref, rk_ref, o_ref, s_scr, *, rows):
    L = CHUNK
    L2 = 2 * L

    @pl.when(pl.program_id(2) == 0)
    def _():
        s_scr[...] = jnp.zeros_like(s_scr)

    lane = lax.broadcasted_iota(I32, (L, LANES), 1)
    m_a = (lane < RWKV_HEAD).astype(F32)
    m_b = 1.0 - m_a
    ri = lax.broadcasted_iota(I32, (L2, L2), 0)
    ci = lax.broadcasted_iota(I32, (L2, L2), 1)
    strict = ri > ci
    incl = ri >= ci
    eye = (ri == ci).astype(F32)
    tri = (lax.broadcasted_iota(I32, (L, L), 0) >= lax.broadcasted_iota(I32, (L, L), 1)).astype(BF16)
    hr = lax.broadcasted_iota(I32, (LANES, LANES), 0) // RWKV_HEAD
    hc = lax.broadcasted_iota(I32, (LANES, LANES), 1) // RWKV_HEAD
    head_ones = (hr == hc).astype(BF16)
    gnw = gnw_ref[...]
    gnb = gnb_ref[...]
    rk = rk_ref[...]

    def stack(x):
        return jnp.concatenate([x * m_a, x * m_b], axis=0)

    def chunk(c, carry):
        sl = pl.ds(pl.multiple_of(c * L, L), L)
        r = r_ref[0, sl, :].astype(F32)
        k = k_ref[0, sl, :].astype(F32)
        v = v_ref[0, sl, :].astype(F32)
        kk = kk_ref[0, sl, :].astype(F32)
        a = a_ref[0, sl, :].astype(F32)
        lw = lw_ref[0, sl, :]
        g = g_ref[0, sl, :].astype(F32)
        hi, lo = _split_bf16(lw)
        cum = _dot(tri, hi) + _dot(tri, lo)
        c_end = cum[L - 1:L, :]
        p_in = jnp.exp(cum)
        p_inv = jnp.exp(-cum)
        p_ex = jnp.exp(cum - lw)
        p_end = jnp.exp(c_end - cum)
        b = kk * a
        x_a = stack(-(kk * p_ex)).astype(BF16)
        x_r = stack(r * p_in).astype(BF16)
        x_b = stack(b * p_inv).astype(BF16)
        x_k = stack(k * p_inv).astype(BF16)
        v_st = stack(v).astype(BF16)
        z_hat = jnp.concatenate([stack(b * p_end), stack(k * p_end)], axis=0).astype(BF16)

        big = _dot_nt(jnp.concatenate([x_a, x_r], axis=0),
                      jnp.concatenate([x_b, x_k], axis=0))
        n_ab = jnp.where(strict, big[0:L2, 0:L2], 0.0)
        a_ak = jnp.where(strict, big[0:L2, L2:2 * L2], 0.0)
        a_rb = jnp.where(incl, big[L2:2 * L2, 0:L2], 0.0)
        a_rk = jnp.where(incl, big[L2:2 * L2, L2:2 * L2], 0.0)

        t_inv = eye + n_ab
        n_pow = n_ab
        for _ in range(5):
            n_b = n_pow.astype(BF16)
            n_pow = _dot(n_b, n_b)
            t_inv = t_inv + _dot(t_inv.astype(BF16), n_pow.astype(BF16))

        s0 = s_scr[...]
        s0_b = s0.astype(BF16)
        rhs = _dot_nt(x_a, s0_b) + _dot(a_ak.astype(BF16), v_st)
        u_st = _dot(t_inv.astype(BF16), rhs.astype(BF16)).astype(BF16)
        uv = jnp.concatenate([u_st, v_st], axis=0)
        y_st = _dot_nt(x_r, s0_b) + _dot(
            jnp.concatenate([a_rb.astype(BF16), a_rk.astype(BF16)], axis=1), uv)
        s_scr[...] = s0 * jnp.exp(c_end) + _dot_tn(uv, z_hat)

        y = y_st[0:L, :] + y_st[L:L2, :]
        mean = _dot(y.astype(BF16), head_ones) * (1.0 / RWKV_HEAD)
        yc = y - mean
        var = _dot((yc * yc).astype(BF16), head_ones) * (1.0 / RWKV_HEAD)
        yn = yc * lax.rsqrt(var + GN_EPS) * gnw + gnb
        bonus = _dot((r * k * rk).astype(BF16), head_ones)
        o_ref[0, sl, :] = ((yn + bonus * v) * g).astype(o_ref.dtype)
        return carry

    lax.fori_loop(0, rows // L, chunk, 0)


def _rwkv_scan(r, k, v, kk, a, lw, g, gn_w, gn_b, r_k, rows):
    B, S, W = r.shape
    npair = W // LANES
    seq = lambda: pl.BlockSpec((1, rows, LANES), lambda b, p, c: (b, c, p))
    par = lambda: pl.BlockSpec((1, LANES), lambda b, p, c: (0, p))
    return pl.pallas_call(
        functools.partial(_rwkv_scan_kernel, rows=rows),
        grid=(B, npair, S // rows),
        in_specs=[seq() for _ in range(7)] + [par(), par(), par()],
        out_specs=seq(),
        out_shape=jax.ShapeDtypeStruct((B, S, W), BF16),
        scratch_shapes=[pltpu.VMEM((LANES, LANES), F32)],
        compiler_params=_cparams(("parallel", "parallel", "arbitrary")),
        name="rwkv_scan",
    )(r, k, v, kk, a, lw, g, gn_w, gn_b, r_k)


def _rope128(x, cos, sin_signed):
    lane = lax.broadcasted_iota(I32, x.shape, 1)
    half = QK_ROPE // 2
    partner = jnp.where(lane < half, pltpu.roll(x, LANES - half, 1), pltpu.roll(x, half, 1))
    return x * cos + partner * sin_signed


def _mla_prep_kernel(cq_ref, ckv_ref, kr_ref, pos_ref, invf_ref, gqa_ref, gkva_ref,
                     wuq_ref, wukv_ref, gqn_ref, gkn_ref,
                     q_out, k_out, v_out, cqn_scr, ckvn_scr, cos_scr, sin_scr):
    @pl.when(pl.program_id(1) == 0)
    def _():
        cqn_scr[...] = _rms(cq_ref[...].astype(F32), gqa_ref[...]).astype(BF16)
        ckvn_scr[...] = _rms(ckv_ref[...].astype(F32), gkva_ref[...]).astype(BF16)
        ang = pos_ref[...].astype(F32) * invf_ref[...]
        lane = lax.broadcasted_iota(I32, ang.shape, 1)
        cos_scr[...] = jnp.cos(ang)
        sin_scr[...] = jnp.where(lane < QK_ROPE // 2, -1.0, 1.0) * jnp.sin(ang)

    cos = cos_scr[...]
    sin = sin_scr[...]
    inv_n = 1.0 / QK_HEAD
    q = _dot(cqn_scr[...], wuq_ref[...])
    rs = lax.rsqrt(jnp.sum(q * q, axis=-1, keepdims=True) * inv_n + NORM_EPS)
    q = q * rs * gqn_ref[...] * (QK_HEAD ** -0.5)
    q_out[...] = jnp.concatenate(
        [q[:, 0:LANES], _rope128(q[:, LANES:], cos, sin)], axis=1).astype(q_out.dtype)

    kv = _dot(ckvn_scr[...], wukv_ref[...])
    k_nope = kv[:, 0:QK_NOPE]
    kr = kr_ref[...].astype(F32)
    ssq = (jnp.sum(k_nope * k_nope, axis=-1, keepdims=True)
           + jnp.sum(kr * kr, axis=-1, keepdims=True))
    rs = lax.rsqrt(ssq * inv_n + NORM_EPS)
    gkn = gkn_ref[...]
    k_out[...] = jnp.concatenate(
        [k_nope * rs * gkn[:, 0:LANES], _rope128(kr * rs * gkn[:, LANES:], cos, sin)],
        axis=1).astype(k_out.dtype)
    v_out[...] = kv[:, QK_NOPE:].astype(v_out.dtype)


def _mla_prep(proj, pos, invf, g_qa, g_kva, wuq, wukv, gqn, gkn, tm):
    T = proj.shape[0]
    H = MLA_HEADS
    c0 = COL_MLA // Q_LORA
    ckr = (COL_MLA + Q_LORA + KV_LORA) // LANES
    full = lambda shape: pl.BlockSpec(shape, lambda i, h: (0, 0))
    return pl.pallas_call(
        _mla_prep_kernel,
        grid=(T // tm, H),
        in_specs=[
            pl.BlockSpec((tm, Q_LORA), lambda i, h: (i, c0)),
            pl.BlockSpec((tm, KV_LORA), lambda i, h: (i, c0 + 1)),
            pl.BlockSpec((tm, LANES), lambda i, h: (i, ckr)),
            pl.BlockSpec((tm, 1), lambda i, h: (i, 0)),
            full((1, LANES)), full((1, Q_LORA)), full((1, KV_LORA)),
            pl.BlockSpec((Q_LORA, QK_PAD), lambda i, h: (0, h)),
            pl.BlockSpec((KV_LORA, QK_NOPE + V_HEAD), lambda i, h: (0, h)),
            full((1, QK_PAD)), full((1, QK_PAD)),
        ],
        out_specs=[
            pl.BlockSpec((tm, QK_PAD), lambda i, h: (i, h)),
            pl.BlockSpec((tm, QK_PAD), lambda i, h: (i, h)),
            pl.BlockSpec((tm, V_HEAD), lambda i, h: (i, h)),
        ],
        out_shape=[jax.ShapeDtypeStruct((T, H * QK_PAD), BF16),
                   jax.ShapeDtypeStruct((T, H * QK_PAD), BF16),
                   jax.ShapeDtypeStruct((T, H * V_HEAD), BF16)],
        scratch_shapes=[pltpu.VMEM((tm, Q_LORA), BF16), pltpu.VMEM((tm, KV_LORA), BF16),
                        pltpu.VMEM((tm, LANES), F32), pltpu.VMEM((tm, LANES), F32)],
        compiler_params=_cparams(("parallel", "arbitrary")),
        name="mla_prep",
    )(proj, proj, proj, pos, invf, g_qa, g_kva, wuq, wukv, gqn, gkn)


MASK_VALUE = -1e30


def _flash_kernel(q_ref, k_ref, v_ref, o_ref, m_scr, l_scr, acc_scr, *, tq):
    i = pl.program_id(2)
    q = q_ref[0]
    m_scr[...] = jnp.full_like(m_scr, MASK_VALUE)
    l_scr[...] = jnp.zeros_like(l_scr)
    acc_scr[...] = jnp.zeros_like(acc_scr)

    def step(j, masked):
        ks = pl.ds(pl.multiple_of(j * tq, tq), tq)
        s = _dot_nt(q, k_ref[0, ks, :])
        if masked:
            rr = lax.broadcasted_iota(I32, s.shape, 0)
            cc = lax.broadcasted_iota(I32, s.shape, 1)
            s = jnp.where(cc <= rr, s, MASK_VALUE)
        m_old = m_scr[...]
        m_new = jnp.maximum(m_old, jnp.max(s, axis=-1, keepdims=True))
        p = jnp.exp(s - m_new)
        alpha = jnp.exp(m_old - m_new)
        l_scr[...] = alpha * l_scr[...] + jnp.sum(p, axis=-1, keepdims=True)
        acc_scr[...] = alpha * acc_scr[...] + _dot(p.astype(BF16), v_ref[0, ks, :])
        m_scr[...] = m_new

    def body(j, carry):
        step(j, False)
        return carry

    lax.fori_loop(0, i, body, 0)
    step(i, True)
    o_ref[0] = (acc_scr[...] / l_scr[...]).astype(o_ref.dtype)


def _flash(q, k, v, tq):
    B, S, _ = q.shape
    H = MLA_HEADS
    return pl.pallas_call(
        functools.partial(_flash_kernel, tq=tq),
        grid=(B, H, S // tq),
        in_specs=[
            pl.BlockSpec((1, tq, QK_PAD), lambda b, h, i: (b, i, h)),
            pl.BlockSpec((1, S, QK_PAD), lambda b, h, i: (b, 0, h)),
            pl.BlockSpec((1, S, V_HEAD), lambda b, h, i: (b, 0, h)),
        ],
        out_specs=pl.BlockSpec((1, tq, V_HEAD), lambda b, h, i: (b, i, h)),
        out_shape=jax.ShapeDtypeStruct((B, S, H * V_HEAD), BF16),
        scratch_shapes=[pltpu.VMEM((tq, 1), F32), pltpu.VMEM((tq, 1), F32),
                        pltpu.VMEM((tq, V_HEAD), F32)],
        compiler_params=_cparams(("parallel", "parallel", "arbitrary")),
        name="flash_attn",
    )(q, k, v)


def _merge_kernel(oa_ref, ob_ref, wa_ref, wb_ref, ga_ref, gb_ref, o_ref):
    ya = _dot(oa_ref[...], wa_ref[...])
    yb = _dot(ob_ref[...], wb_ref[...])
    ga = _sigmoid(ga_ref[...].astype(F32))
    gb = _sigmoid(gb_ref[...].astype(F32))
    o_ref[...] = (ga * ya + gb * yb).astype(o_ref.dtype)


def _merge(oa, ob, wa, wb, proj, tm, tn):
    T, K = oa.shape
    N = wa.shape[1]
    ca = COL_GA // tn
    cb = COL_GB // tn
    return pl.pallas_call(
        _merge_kernel,
        grid=(T // tm, N // tn),
        in_specs=[
            pl.BlockSpec((tm, K), lambda i, j: (i, 0)),
            pl.BlockSpec((tm, K), lambda i, j: (i, 0)),
            pl.BlockSpec((K, tn), lambda i, j: (0, j)),
            pl.BlockSpec((K, tn), lambda i, j: (0, j)),
            pl.BlockSpec((tm, tn), lambda i, j: (i, ca + j)),
            pl.BlockSpec((tm, tn), lambda i, j: (i, cb + j)),
        ],
        out_specs=pl.BlockSpec((tm, tn), lambda i, j: (i, j)),
        out_shape=jax.ShapeDtypeStruct((T, N), BF16),
        compiler_params=_cparams(("parallel", "parallel")),
        name="merge",
    )(oa, ob, wa, wb, proj, proj)


def _pack_rows(x, out_ref, tm):
    for c in range(PACK_ROWS):
        lo = x[:, c * 2 * LANES:c * 2 * LANES + LANES]
        hi = x[:, c * 2 * LANES + LANES:(c + 1) * 2 * LANES]
        lo_b = pltpu.bitcast(lo.astype(BF16).astype(F32), U32)
        hi_b = pltpu.bitcast(hi.astype(BF16).astype(F32), U32)
        out_ref[pl.ds(c, tm, stride=PACK_ROWS), :] = hi_b | (lo_b >> 16)


def _unpack_rows(ref, c, tm, lead=None):
    idx = (pl.ds(c, tm, stride=PACK_ROWS), slice(None))
    if lead is not None:
        idx = (lead,) + idx
    u = ref[idx]
    lo = pltpu.bitcast(u << 16, F32)
    hi = pltpu.bitcast(u & jnp.uint32(0xFFFF0000), F32)
    return lo, hi


def _wo_kernel(m_ref, x_ref, wo_ref, g_ref, wr_ref, x1_ref, h2_ref, h2p_ref, lg_ref, *, tm):
    x1 = x_ref[...] + _dot(m_ref[...], wo_ref[...])
    x1_ref[...] = x1
    h2 = _rms(x1, g_ref[...])
    h2_ref[...] = h2.astype(BF16)
    _pack_rows(h2, h2p_ref, tm)
    h_hi, h_lo = _split_bf16(h2)
    w_hi, w_lo = _split_bf16(wr_ref[...])
    lg_ref[...] = _dot_nt(w_hi, h_hi) + _dot_nt(w_hi, h_lo) + _dot_nt(w_lo, h_hi)


def _wo(merged, x2, wo, g, wr_t, tm):
    T, D = x2.shape
    E = wr_t.shape[0]
    return pl.pallas_call(
        functools.partial(_wo_kernel, tm=tm),
        grid=(T // tm,),
        in_specs=[
            pl.BlockSpec((tm, D), lambda i: (i, 0)),
            pl.BlockSpec((tm, D), lambda i: (i, 0)),
            pl.BlockSpec((D, D), lambda i: (0, 0)),
            pl.BlockSpec((1, D), lambda i: (0, 0)),
            pl.BlockSpec((E, D), lambda i: (0, 0)),
        ],
        out_specs=[
            pl.BlockSpec((tm, D), lambda i: (i, 0)),
            pl.BlockSpec((tm, D), lambda i: (i, 0)),
            pl.BlockSpec((tm * PACK_ROWS, PACK_W), lambda i: (i, 0)),
            pl.BlockSpec((E, tm), lambda i: (0, i)),
        ],
        out_shape=[jax.ShapeDtypeStruct((T, D), F32),
                   jax.ShapeDtypeStruct((T, D), BF16),
                   jax.ShapeDtypeStruct((T * PACK_ROWS, PACK_W), U32),
                   jax.ShapeDtypeStruct((E, T), F32)],
        compiler_params=_cparams(("parallel",)),
        name="wo_norm_router",
    )(merged, x2, wo, g, wr_t)


def _route_kernel(lg_ref, bias_ref, eidx_ref, wts_ref, rnk_ref, cnt_ref, carry_scr, *, tm):
    E = N_EXPERTS
    NEG = -jnp.inf

    @pl.when(pl.program_id(0) == 0)
    def _():
        carry_scr[...] = jnp.zeros_like(carry_scr)

    scores = _sigmoid(lg_ref[...])
    choice = scores + bias_ref[...]
    c3 = choice.reshape(N_GROUPS, GROUP_SIZE, tm)
    sub = lax.broadcasted_iota(I32, c3.shape, 1)
    m1 = jnp.max(c3, axis=1, keepdims=True)
    i1 = jnp.min(jnp.where(c3 == m1, sub, GROUP_SIZE), axis=1, keepdims=True)
    m2 = jnp.max(jnp.where(sub == i1, NEG, c3), axis=1, keepdims=True)
    gs = (m1 + m2).reshape(N_GROUPS, tm)
    gi = lax.broadcasted_iota(I32, gs.shape, 0)
    gsel = jnp.zeros(gs.shape, F32)
    for _ in range(TOPK_GROUPS):
        mx = jnp.max(gs, axis=0, keepdims=True)
        ix = jnp.min(jnp.where(gs == mx, gi, N_GROUPS), axis=0, keepdims=True)
        hit = gi == ix
        gsel = jnp.where(hit, 1.0, gsel)
        gs = jnp.where(hit, NEG, gs)
    emask = jnp.broadcast_to(gsel.reshape(N_GROUPS, 1, tm), (N_GROUPS, GROUP_SIZE, tm)).reshape(E, tm)
    x = jnp.where(emask > 0.5, choice, NEG)
    ei = lax.broadcasted_iota(I32, x.shape, 0)
    sel = jnp.zeros(x.shape, F32)
    idx_rows, w_rows = [], []
    for _ in range(TOP_K):
        mx = jnp.max(x, axis=0, keepdims=True)
        ix = jnp.min(jnp.where(x == mx, ei, E), axis=0, keepdims=True)
        hit = ei == ix
        w_rows.append(jnp.sum(jnp.where(hit, scores, 0.0), axis=0, keepdims=True))
        idx_rows.append(ix)
        sel = jnp.where(hit, 1.0, sel)
        x = jnp.where(hit, NEG, x)
    w = jnp.concatenate(w_rows, axis=0)
    w = w / (jnp.sum(w, axis=0, keepdims=True) + 1e-20) * ROUTED_SCALE
    eidx_ref[...] = jnp.concatenate(idx_rows, axis=0)
    wts_ref[...] = w

    upper = (lax.broadcasted_iota(I32, (tm, tm), 0) < lax.broadcasted_iota(I32, (tm, tm), 1)).astype(BF16)
    base = carry_scr[...][:, 0:1]
    excl = _dot(sel.astype(BF16), upper) + base
    rnk_ref[...] = jnp.concatenate(
        [jnp.sum(jnp.where(ei == ix, excl, 0.0), axis=0, keepdims=True) for ix in idx_rows],
        axis=0).astype(I32)
    carry_scr[...] = carry_scr[...] + jnp.sum(sel, axis=1, keepdims=True)
    cnt_ref[...] = carry_scr[...]


def _route(logits_t, bias, tm):
    E, T = logits_t.shape
    K = TOP_K
    tok = lambda: pl.BlockSpec((K, tm), lambda i: (0, i))
    return pl.pallas_call(
        functools.partial(_route_kernel, tm=tm),
        grid=(T // tm,),
        in_specs=[pl.BlockSpec((E, tm), lambda i: (0, i)),
                  pl.BlockSpec((E, 1), lambda i: (0, 0))],
        out_specs=[tok(), tok(), tok(), pl.BlockSpec((E, LANES), lambda i: (0, 0))],
        out_shape=[jax.ShapeDtypeStruct((K, T), I32), jax.ShapeDtypeStruct((K, T), F32),
                   jax.ShapeDtypeStruct((K, T), I32), jax.ShapeDtypeStruct((E, LANES), F32)],
        scratch_shapes=[pltpu.VMEM((E, LANES), F32)],
        compiler_params=_cparams(("arbitrary",)),
        name="route",
    )(logits_t, bias)


def _row_copy(src_hbm, src_row, dst_hbm, dst_row, sem):
    return pltpu.make_async_copy(
        src_hbm.at[pl.ds(pl.multiple_of(src_row * PACK_ROWS, PACK_ROWS), PACK_ROWS), :],
        dst_hbm.at[pl.ds(pl.multiple_of(dst_row * PACK_ROWS, PACK_ROWS), PACK_ROWS), :],
        sem)


def _dispatch_kernel(eidx_ref, rnk_ref, pst_ref, h_hbm, xs_in, xs_out, sem, *, tm):
    del xs_in
    base = pl.program_id(0) * tm

    def issue(t, carry):
        for k in range(TOP_K):
            dest = pst_ref[eidx_ref[k, t]] + rnk_ref[k, t]
            _row_copy(h_hbm, base + t, xs_out, dest, sem).start()
        return carry

    lax.fori_loop(0, tm, issue, 0)

    def drain(t, carry):
        for k in range(TOP_K):
            _row_copy(h_hbm, 0, xs_out, 0, sem).wait()
        return carry

    lax.fori_loop(0, tm, drain, 0)


def _dispatch(eidx, rnk, pstart, h2p, xs_zero, tm):
    K, T = eidx.shape
    smem_tok = lambda: pl.BlockSpec((K, tm), lambda i: (0, i), memory_space=pltpu.SMEM)
    return pl.pallas_call(
        functools.partial(_dispatch_kernel, tm=tm),
        grid=(T // tm,),
        in_specs=[smem_tok(), smem_tok(),
                  pl.BlockSpec(memory_space=pltpu.SMEM),
                  pl.BlockSpec(memory_space=pl.ANY),
                  pl.BlockSpec(memory_space=pl.ANY)],
        out_specs=pl.BlockSpec(memory_space=pl.ANY),
        out_shape=jax.ShapeDtypeStruct(xs_zero.shape, xs_zero.dtype),
        scratch_shapes=[pltpu.SemaphoreType.DMA(())],
        input_output_aliases={4: 0},
        compiler_params=_cparams(("arbitrary",)),
        name="dispatch",
    )(eidx, rnk, pstart, h2p, xs_zero)


def _ffn_kernel(te_ref, xs_ref, wg_ref, wu_ref, wd_ref, ys_ref, *, tm):
    del te_ref
    parts = []
    for c in range(PACK_ROWS):
        lo, hi = _unpack_rows(xs_ref, c, tm)
        parts += [lo.astype(BF16), hi.astype(BF16)]
    a = jnp.concatenate(parts, axis=1)
    hg = _dot(a, wg_ref[0])
    hu = _dot(a, wu_ref[0])
    hid = (hg * _sigmoid(hg) * hu).astype(BF16)
    _pack_rows(_dot(hid, wd_ref[0]), ys_ref, tm)


def _ffn(tile_e, xs, wg, wu, wd, tm):
    P8 = xs.shape[0]
    nt = P8 // (tm * PACK_ROWS)
    D, I = wg.shape[1], wg.shape[2]
    grid_spec = pltpu.PrefetchScalarGridSpec(
        num_scalar_prefetch=1,
        grid=(nt,),
        in_specs=[
            pl.BlockSpec((tm * PACK_ROWS, PACK_W), lambda i, te: (i, 0)),
            pl.BlockSpec((1, D, I), lambda i, te: (te[i], 0, 0)),
            pl.BlockSpec((1, D, I), lambda i, te: (te[i], 0, 0)),
            pl.BlockSpec((1, I, D), lambda i, te: (te[i], 0, 0)),
        ],
        out_specs=pl.BlockSpec((tm * PACK_ROWS, PACK_W), lambda i, te: (i, 0)),
    )
    return pl.pallas_call(
        functools.partial(_ffn_kernel, tm=tm),
        grid_spec=grid_spec,
        out_shape=jax.ShapeDtypeStruct(xs.shape, U32),
        compiler_params=_cparams(("parallel",)),
        name="expert_ffn",
    )(tile_e, xs, wg, wu, wd)


def _combine_kernel(eidx_ref, rnk_ref, pst_ref, ys_hbm, wt_ref, x1_ref, h2_ref,
                    wsg_ref, wsu_ref, wsd_ref, o_ref, buf, sem, *, tm):
    def issue(t, carry):
        for k in range(TOP_K):
            dest = pst_ref[eidx_ref[k, t]] + rnk_ref[k, t]
            pltpu.make_async_copy(
                ys_hbm.at[pl.ds(pl.multiple_of(dest * PACK_ROWS, PACK_ROWS), PACK_ROWS), :],
                buf.at[k, pl.ds(pl.multiple_of(t * PACK_ROWS, PACK_ROWS), PACK_ROWS), :],
                sem).start()
        return carry

    lax.fori_loop(0, tm, issue, 0)

    h2 = h2_ref[...]
    hg = _dot(h2, wsg_ref[...])
    hu = _dot(h2, wsu_ref[...])
    o_ref[...] = x1_ref[...] + _dot((hg * _sigmoid(hg) * hu).astype(BF16), wsd_ref[...])

    def drain(t, carry):
        for k in range(TOP_K):
            pltpu.make_async_copy(ys_hbm.at[pl.ds(0, PACK_ROWS), :],
                                  buf.at[0, pl.ds(0, PACK_ROWS), :], sem).wait()
        return carry

    lax.fori_loop(0, tm, drain, 0)

    wt = wt_ref[...]
    for c in range(PACK_ROWS):
        acc_lo = jnp.zeros((tm, LANES), F32)
        acc_hi = jnp.zeros((tm, LANES), F32)
        for k in range(TOP_K):
            lo, hi = _unpack_rows(buf, c, tm, lead=k)
            wk = wt[:, k:k + 1]
            acc_lo = acc_lo + wk * lo
            acc_hi = acc_hi + wk * hi
        s0 = c * 2 * LANES
        o_ref[:, s0:s0 + LANES] = o_ref[:, s0:s0 + LANES] + acc_lo
        o_ref[:, s0 + LANES:s0 + 2 * LANES] = o_ref[:, s0 + LANES:s0 + 2 * LANES] + acc_hi


def _combine(eidx, rnk, pstart, ys, wts_t, x1, h2, wsg, wsu, wsd, tm):
    K, T = eidx.shape
    D = x1.shape[1]
    I = wsg.shape[1]
    smem_tok = lambda: pl.BlockSpec((K, tm), lambda i: (0, i), memory_space=pltpu.SMEM)
    return pl.pallas_call(
        functools.partial(_combine_kernel, tm=tm),
        grid=(T // tm,),
        in_specs=[smem_tok(), smem_tok(),
                  pl.BlockSpec(memory_space=pltpu.SMEM),
                  pl.BlockSpec(memory_space=pl.ANY),
                  pl.BlockSpec((tm, K), lambda i: (i, 0)),
                  pl.BlockSpec((tm, D), lambda i: (i, 0)),
                  pl.BlockSpec((tm, D), lambda i: (i, 0)),
                  pl.BlockSpec((D, I), lambda i: (0, 0)),
                  pl.BlockSpec((D, I), lambda i: (0, 0)),
                  pl.BlockSpec((I, D), lambda i: (0, 0))],
        out_specs=pl.BlockSpec((tm, D), lambda i: (i, 0)),
        out_shape=jax.ShapeDtypeStruct((T, D), F32),
        scratch_shapes=[pltpu.VMEM((K, tm * PACK_ROWS, PACK_W), U32),
                        pltpu.SemaphoreType.DMA(())],
        compiler_params=_cparams(("arbitrary",)),
        name="combine",
    )(eidx, rnk, pstart, ys, wts_t, x1, h2, wsg, wsu, wsd)


def _ple_kernel(xf_ref, xt_ref, p_ref, wp_ref, gpost_ref, gin_ref, wg_ref, o_ref,
                hn_scr, ple_scr, *, tn):
    j = pl.program_id(1)

    @pl.when(j == 0)
    def _():
        hn_scr[...] = _rms(xf_ref[...], gin_ref[...]).astype(BF16)
        ple_scr[...] = _rms(_dot(p_ref[...].astype(BF16), wp_ref[...]), gpost_ref[...])

    cols = pl.ds(pl.multiple_of(j * tn, tn), tn)
    gate = _sigmoid(_dot(hn_scr[...], wg_ref[...]))
    o_ref[...] = xt_ref[...] + gate * ple_scr[:, cols]


def _ple(x2, p2, wp, gpost, gin, wg, tm, tn):
    T, D = x2.shape
    Pd = p2.shape[1]
    return pl.pallas_call(
        functools.partial(_ple_kernel, tn=tn),
        grid=(T // tm, D // tn),
        in_specs=[
            pl.BlockSpec((tm, D), lambda i, j: (i, 0)),
            pl.BlockSpec((tm, tn), lambda i, j: (i, j)),
            pl.BlockSpec((tm, Pd), lambda i, j: (i, 0)),
            pl.BlockSpec((Pd, D), lambda i, j: (0, 0)),
            pl.BlockSpec((1, D), lambda i, j: (0, 0)),
            pl.BlockSpec((1, D), lambda i, j: (0, 0)),
            pl.BlockSpec((D, tn), lambda i, j: (0, j)),
        ],
        out_specs=pl.BlockSpec((tm, tn), lambda i, j: (i, j)),
        out_shape=jax.ShapeDtypeStruct((T, D), F32),
        scratch_shapes=[pltpu.VMEM((tm, D), BF16), pltpu.VMEM((tm, D), F32)],
        compiler_params=_cparams(("parallel", "arbitrary")),
        name="ple",
    )(x2, x2, p2, wp, gpost, gin, wg)


def _pad_cols(w, n):
    return jnp.pad(w, ((0, 0), (0, n - w.shape[1])))


def _pad_rows(w, n):
    return jnp.pad(w, ((0, n - w.shape[0]), (0, 0)))


def _layout_w_in(w):
    W = RWKV_WIDTH
    c = 3 * W
    segs = [w[:, 0:c],
            _pad_cols(w[:, c:c + DECAY_LORA], LORA_PAD),
            _pad_cols(w[:, c + DECAY_LORA:c + DECAY_LORA + AAA_LORA], LORA_PAD),
            _pad_cols(w[:, c + DECAY_LORA + AAA_LORA:c + DECAY_LORA + AAA_LORA + GATE_LORA], GATE_LORA_PAD)]
    c += DECAY_LORA + AAA_LORA + GATE_LORA
    mla_cols = Q_LORA + KV_LORA + QK_ROPE
    segs.append(_pad_cols(w[:, c:c + mla_cols], MLA_PAD))
    c += mla_cols
    segs.append(w[:, c:])
    out = jnp.concatenate(segs, axis=1).astype(BF16)
    assert out.shape[1] == IN_PAD
    return out


def _layout_mu(mu):
    W = RWKV_WIDTH
    c = 3 * W
    segs = [mu[0:c],
            jnp.pad(mu[c:c + DECAY_LORA], (0, LORA_PAD - DECAY_LORA)),
            jnp.pad(mu[c + DECAY_LORA:c + DECAY_LORA + AAA_LORA], (0, LORA_PAD - AAA_LORA)),
            jnp.pad(mu[c + DECAY_LORA + AAA_LORA:], (0, GATE_LORA_PAD - GATE_LORA))]
    return jnp.concatenate(segs).reshape(1, RWKV_PAD)


def _layer(x, p, positions, g_mix, w_in, mu_rwkv, w0, w2, a0, a2, g2, k_k, k_a, r_k,
           gn_w, gn_b, w_a_up, g_qa, g_kva, w_uq, w_ukv, g_qn, g_kn, w_b_up, w_o,
           g_ffn, w_router, router_bias, w_exp_gate, w_exp_up, w_exp_down,
           w_sh_gate, w_sh_up, w_sh_down, w_ple, g_ple_post, g_ple_in, w_ple_gate,
           *, tiles):
    B, S, D = x.shape
    T = B * S
    W = RWKV_WIDTH
    x2 = x.reshape(T, D)
    row = lambda v: v.reshape(1, -1).astype(F32)

    proj = _inproj(x2, row(g_mix), _layout_w_in(w_in), tiles["tm_in"], tiles["tn_in"])

    head_blk = (jnp.arange(W)[:, None] // RWKV_HEAD == jnp.arange(W)[None, :] // RWKV_HEAD).astype(BF16)
    r, k, v, kk, a, lw, g = _rwkv_prep(
        proj, _layout_mu(mu_rwkv), row(w0), _pad_rows(w2, LORA_PAD).astype(BF16), row(a0),
        _pad_rows(a2, LORA_PAD).astype(BF16), _pad_rows(g2, GATE_LORA_PAD).astype(BF16),
        row(k_k), row(k_a), head_blk, tiles["tm_prep"], S)
    b3 = lambda t: t.reshape(B, S, W)
    o_a = _rwkv_scan(b3(r), b3(k), b3(v), b3(kk), b3(a), b3(lw), b3(g),
                     row(gn_w), row(gn_b), row(r_k), tiles["rows_scan"]).reshape(T, W)

    half = QK_ROPE // 2
    inv_freq = ROPE_THETA ** (-jnp.arange(half, dtype=F32) / half)
    invf = jnp.concatenate([inv_freq, inv_freq, jnp.zeros((LANES - QK_ROPE,), F32)]).reshape(1, LANES)
    wuq = jnp.pad(w_uq.reshape(Q_LORA, MLA_HEADS, QK_HEAD),
                  ((0, 0), (0, 0), (0, QK_PAD - QK_HEAD))).reshape(Q_LORA, MLA_HEADS * QK_PAD)
    pad_g = lambda gv: jnp.pad(gv, (0, QK_PAD - QK_HEAD)).reshape(1, QK_PAD)
    q, kx, vx = _mla_prep(proj, positions.reshape(T, 1).astype(I32), invf, row(g_qa), row(g_kva),
                          wuq.astype(BF16), w_ukv.astype(BF16), pad_g(g_qn), pad_g(g_kn),
                          tiles["tm_mla"])
    o_b = _flash(q.reshape(B, S, -1), kx.reshape(B, S, -1), vx.reshape(B, S, -1),
                 tiles["tq"]).reshape(T, MLA_WIDTH)

    merged = _merge(o_a, o_b, w_a_up.astype(BF16), w_b_up.astype(BF16), proj,
                    tiles["tm_merge"], tiles["tn_merge"])

    x1, h2, h2p, logits_t = _wo(merged, x2, w_o.astype(BF16), row(g_ffn),
                                jnp.transpose(w_router).astype(F32), tiles["tm_wo"])
    eidx, wts, rnk, cnt = _route(logits_t, router_bias.reshape(N_EXPERTS, 1).astype(F32),
                                 tiles["tm_route"])
    tmf = tiles["tm_ffn"]
    counts = cnt[:, 0].astype(I32)
    pcounts = (counts + tmf - 1) // tmf * tmf
    pends = jnp.cumsum(pcounts)
    pstart = (pends - pcounts).astype(I32)
    n_tiles = (T * TOP_K) // tmf + N_EXPERTS
    tile_e = jnp.clip(jnp.searchsorted(pends, jnp.arange(n_tiles, dtype=I32) * tmf, side="right"),
                      0, N_EXPERTS - 1).astype(I32)
    xs = _dispatch(eidx, rnk, pstart, h2p,
                   jnp.zeros((n_tiles * tmf * PACK_ROWS, PACK_W), U32), tiles["tm_disp"])
    ys = _ffn(tile_e, xs, w_exp_gate.astype(BF16), w_exp_up.astype(BF16),
              w_exp_down.astype(BF16), tmf)
    x3 = _combine(eidx, rnk, pstart, ys, jnp.transpose(wts), x1, h2,
                  w_sh_gate.astype(BF16), w_sh_up.astype(BF16), w_sh_down.astype(BF16),
                  tiles["tm_comb"])

    out = _ple(x3, p.reshape(T, PLE_DIM), w_ple.astype(BF16), row(g_ple_post), row(g_ple_in),
               w_ple_gate.astype(BF16), tiles["tm_ple"], tiles["tn_ple"])
    return out.reshape(B, S, D)


TILES = dict(tm_in=512, tn_in=512, tm_prep=256, rows_scan=512, tm_mla=512, tq=512,
             tm_merge=512, tn_merge=512, tm_wo=256, tm_route=512, tm_ffn=256,
             tm_disp=512, tm_comb=256, tm_ple=512, tn_ple=512)


def kernel(x, p, positions, g_mix, w_in, mu_rwkv, w0, w2, a0, a2, g2, k_k, k_a, r_k, gn_w, gn_b, w_a_up, g_qa, g_kva, w_uq, w_ukv, g_qn, g_kn, w_b_up, w_o, g_ffn, w_router, router_bias, w_exp_gate, w_exp_up, w_exp_down, w_sh_gate, w_sh_up, w_sh_down, w_ple, g_ple_post, g_ple_in, w_ple_gate):
    args = (g_mix, w_in, mu_rwkv, w0, w2, a0, a2, g2, k_k, k_a, r_k, gn_w, gn_b, w_a_up,
            g_qa, g_kva, w_uq, w_ukv, g_qn, g_kn, w_b_up, w_o, g_ffn, w_router, router_bias,
            w_exp_gate, w_exp_up, w_exp_down, w_sh_gate, w_sh_up, w_sh_down, w_ple,
            g_ple_post, g_ple_in, w_ple_gate)
    assert all(t.shape[0] == 1 for t in args), "single-layer stack expected"
    return _layer(x, p[0], positions, *[t[0] for t in args], tiles=TILES)
```

```python
import functools

import jax
import jax.numpy as jnp
from jax import lax
from jax.experimental import pallas as pl
from jax.experimental.pallas import tpu as pltpu

F32 = jnp.float32
BF16 = jnp.bfloat16
I32 = jnp.int32
U32 = jnp.uint32

D_MODEL = 2048
PLE_DIM = 256
NORM_EPS = 1e-6
RWKV_HEADS = 16
RWKV_HEAD = 64
RWKV_WIDTH = RWKV_HEADS * RWKV_HEAD
DECAY_LORA = 64
AAA_LORA = 64
GATE_LORA = 160
GN_EPS = 64e-5
MLA_HEADS = 8
Q_LORA = 512
KV_LORA = 512
QK_NOPE = 128
QK_ROPE = 64
QK_HEAD = QK_NOPE + QK_ROPE
V_HEAD = 128
MLA_WIDTH = MLA_HEADS * V_HEAD
ROPE_THETA = 10000.0
N_EXPERTS = 64
TOP_K = 8
N_GROUPS = 8
GROUP_SIZE = N_EXPERTS // N_GROUPS
TOPK_GROUPS = 4
MOE_INTER = 512
ROUTED_SCALE = 2.5

LANES = 128
QK_PAD = 2 * LANES
PACK_ROWS = 8
PACK_W = D_MODEL // 2 // PACK_ROWS
assert PACK_W == LANES

LORA_PAD = 128
GATE_LORA_PAD = 256
RWKV_PAD = 3 * RWKV_WIDTH + 2 * LORA_PAD + GATE_LORA_PAD
MLA_PAD = 1536
COL_RWKV = 0
COL_MLA = RWKV_PAD
COL_GA = COL_MLA + MLA_PAD
COL_GB = COL_GA + D_MODEL
IN_PAD = COL_GB + D_MODEL

VMEM_LIMIT = 56 * 1024 * 1024
CHUNK = 64


def _cparams(sem):
    return pltpu.CompilerParams(dimension_semantics=sem, vmem_limit_bytes=VMEM_LIMIT)


def _dot(a, b):
    return jnp.dot(a, b, preferred_element_type=F32)


def _dot_nt(a, b):
    return lax.dot_general(a, b, (((1,), (1,)), ((), ())), preferred_element_type=F32)


def _dot_tn(a, b):
    return lax.dot_general(a, b, (((0,), (0,)), ((), ())), preferred_element_type=F32)


def _split_bf16(x):
    hi = x.astype(BF16)
    lo = (x - hi.astype(F32)).astype(BF16)
    return hi, lo


def _rms(x, g):
    ms = jnp.mean(x * x, axis=-1, keepdims=True)
    return x * lax.rsqrt(ms + NORM_EPS) * g


def _sigmoid(x):
    return 1.0 / (1.0 + jnp.exp(-x))


def _inproj_kernel(x_ref, g_ref, w_ref, o_ref, h_scr):
    @pl.when(pl.program_id(1) == 0)
    def _():
        h_scr[...] = _rms(x_ref[...], g_ref[...]).astype(BF16)

    o_ref[...] = _dot(h_scr[...], w_ref[...]).astype(o_ref.dtype)


def _inproj(x2, g, w, tm, tn):
    T, D = x2.shape
    N = w.shape[1]
    return pl.pallas_call(
        _inproj_kernel,
        grid=(T // tm, N // tn),
        in_specs=[
            pl.BlockSpec((tm, D), lambda i, j: (i, 0)),
            pl.BlockSpec((1, D), lambda i, j: (0, 0)),
            pl.BlockSpec((D, tn), lambda i, j: (0, j)),
        ],
        out_specs=pl.BlockSpec((tm, tn), lambda i, j: (i, j)),
        out_shape=jax.ShapeDtypeStruct((T, N), BF16),
        scratch_shapes=[pltpu.VMEM((tm, D), BF16)],
        compiler_params=_cparams(("parallel", "arbitrary")),
        name="inproj",
    )(x2, g, w)


HALO = 16


def _rwkv_prep_kernel(cur_ref, halo_ref, mu_ref, w0_ref, w2_ref, a0_ref, a2_ref, g2_ref,
                      kk_ref, ka_ref, bd_ref,
                      r_out, k_out, v_out, kk_out, a_out, lw_out, g_out, *, tm, seq):
    i = pl.program_id(0)
    cur = cur_ref[...].astype(F32)
    first = (i * tm) % seq == 0
    last_prev = halo_ref[HALO - 1:HALO, :].astype(F32)
    last_prev = jnp.where(first, 0.0, last_prev)
    row = lax.broadcasted_iota(I32, cur.shape, 0)
    prev = jnp.where(row == 0, last_prev, pltpu.roll(cur, 1, 0))
    u = cur + (prev - cur) * mu_ref[...]
    W = RWKV_WIDTH
    r = u[:, 0:W]
    k = u[:, W:2 * W]
    v = u[:, 2 * W:3 * W]
    dw = u[:, 3 * W:3 * W + LORA_PAD]
    da = u[:, 3 * W + LORA_PAD:3 * W + 2 * LORA_PAD]
    dg = u[:, 3 * W + 2 * LORA_PAD:3 * W + 2 * LORA_PAD + GATE_LORA_PAD]
    z = w0_ref[...] + _dot(jnp.tanh(dw).astype(BF16), w2_ref[...])
    w_log = -(jnp.maximum(-z, 0.0) + jnp.log(1.0 + jnp.exp(-jnp.abs(z)))) - 0.5
    lw_out[...] = -jnp.exp(w_log)
    a = _sigmoid(a0_ref[...] + _dot(da.astype(BF16), a2_ref[...]))
    g_out[...] = _dot(_sigmoid(dg).astype(BF16), g2_ref[...]).astype(g_out.dtype)
    kk = k * kk_ref[...]
    hi, lo = _split_bf16(kk * kk)
    ssq = _dot(hi, bd_ref[...]) + _dot(lo, bd_ref[...])
    kk = kk / jnp.maximum(jnp.sqrt(ssq), 1e-12)
    r_out[...] = r.astype(r_out.dtype)
    k_out[...] = (k * (1.0 + (a - 1.0) * ka_ref[...])).astype(k_out.dtype)
    v_out[...] = v.astype(v_out.dtype)
    kk_out[...] = kk.astype(kk_out.dtype)
    a_out[...] = a.astype(a_out.dtype)


def _rwkv_prep(proj, mu, w0, w2, a0, a2, g2, k_k, k_a, bd, tm, seq):
    T = proj.shape[0]
    W = RWKV_WIDTH
    full = lambda shape: pl.BlockSpec(shape, lambda i: (0, 0))
    out = lambda: pl.BlockSpec((tm, W), lambda i: (i, 0))
    return pl.pallas_call(
        functools.partial(_rwkv_prep_kernel, tm=tm, seq=seq),
        grid=(T // tm,),
        in_specs=[
            pl.BlockSpec((tm, RWKV_PAD), lambda i: (i, 0)),
            pl.BlockSpec((HALO, RWKV_PAD), lambda i: (jnp.maximum(i * (tm // HALO) - 1, 0), 0)),
            full((1, RWKV_PAD)), full((1, W)), full((LORA_PAD, W)), full((1, W)),
            full((LORA_PAD, W)), full((GATE_LORA_PAD, W)), full((1, W)), full((1, W)),
            full((W, W)),
        ],
        out_specs=[out() for _ in range(7)],
        out_shape=[jax.ShapeDtypeStruct((T, W), dt)
                   for dt in (BF16, BF16, BF16, BF16, BF16, F32, BF16)],
        compiler_params=_cparams(("parallel",)),
        name="rwkv_prep",
    )(proj, proj, mu, w0, w2, a0, a2, g2, k_k, k_a, bd)


def _rwkv_scan_kernel(r_ref, k_ref, v_ref, kk_ref, a_ref, lw_ref, g_ref,
                      gnw_ref, gnb_ref, rk_ref, o_ref, s_scr, *, rows):
    L = CHUNK
    L2 = 2 * L

    @pl.when(pl.program_id(2) == 0)
    def _():
        s_scr[...] = jnp.zeros_like(s_scr)

    lane = lax.broadcasted_iota(I32, (L, LANES), 1)
    m_a = (lane < RWKV_HEAD).astype(F32)
    m_b = 1.0 - m_a
    ri = lax.broadcasted_iota(I32, (L2, L2), 0)
    ci = lax.broadcasted_iota(I32, (L2, L2), 1)
    strict = ri > ci
    incl = ri >= ci
    eye = (ri == ci).astype(F32)
    tri = (lax.broadcasted_iota(I32, (L, L), 0) >= lax.broadcasted_iota(I32, (L, L), 1)).astype(BF16)
    hr = lax.broadcasted_iota(I32, (LANES, LANES), 0) // RWKV_HEAD
    hc = lax.broadcasted_iota(I32, (LANES, LANES), 1) // RWKV_HEAD
    head_ones = (hr == hc).astype(BF16)
    gnw = gnw_ref[...]
    gnb = gnb_ref[...]
    rk = rk_ref[...]

    def stack(x):
        return jnp.concatenate([x * m_a, x * m_b], axis=0)

    def chunk(c, carry):
        sl = pl.ds(pl.multiple_of(c * L, L), L)
        r = r_ref[0, sl, :].astype(F32)
        k = k_ref[0, sl, :].astype(F32)
        v = v_ref[0, sl, :].astype(F32)
        kk = kk_ref[0, sl, :].astype(F32)
        a = a_ref[0, sl, :].astype(F32)
        lw = lw_ref[0, sl, :]
        g = g_ref[0, sl, :].astype(F32)
        hi, lo = _split_bf16(lw)
        cum = _dot(tri, hi) + _dot(tri, lo)
        c_end = cum[L - 1:L, :]
        p_in = jnp.exp(cum)
        p_inv = jnp.exp(-cum)
        p_ex = jnp.exp(cum - lw)
        p_end = jnp.exp(c_end - cum)
        b = kk * a
        x_a = stack(-(kk * p_ex)).astype(BF16)
        x_r = stack(r * p_in).astype(BF16)
        x_b = stack(b * p_inv).astype(BF16)
        x_k = stack(k * p_inv).astype(BF16)
        v_st = stack(v).astype(BF16)
        z_hat = jnp.concatenate([stack(b * p_end), stack(k * p_end)], axis=0).astype(BF16)

        big = _dot_nt(jnp.concatenate([x_a, x_r], axis=0),
                      jnp.concatenate([x_b, x_k], axis=0))
        n_ab = jnp.where(strict, big[0:L2, 0:L2], 0.0)
        a_ak = jnp.where(strict, big[0:L2, L2:2 * L2], 0.0)
        a_rb = jnp.where(incl, big[L2:2 * L2, 0:L2], 0.0)
        a_rk = jnp.where(incl, big[L2:2 * L2, L2:2 * L2], 0.0)

        t_inv = eye + n_ab
        n_pow = n_ab
        for _ in range(5):
            n_b = n_pow.astype(BF16)
            n_pow = _dot(n_b, n_b)
            t_inv = t_inv + _dot(t_inv.astype(BF16), n_pow.astype(BF16))

        s0 = s_scr[...]
        s0_b = s0.astype(BF16)
        rhs = _dot_nt(x_a, s0_b) + _dot(a_ak.astype(BF16), v_st)
        u_st = _dot(t_inv.astype(BF16), rhs.astype(BF16)).astype(BF16)
        uv = jnp.concatenate([u_st, v_st], axis=0)
        y_st = _dot_nt(x_r, s0_b) + _dot(
            jnp.concatenate([a_rb.astype(BF16), a_rk.astype(BF16)], axis=1), uv)
        s_scr[...] = s0 * jnp.exp(c_end) + _dot_tn(uv, z_hat)

        y = y_st[0:L, :] + y_st[L:L2, :]
        mean = _dot(y.astype(BF16), head_ones) * (1.0 / RWKV_HEAD)
        yc = y - mean
        var = _dot((yc * yc).astype(BF16), head_ones) * (1.0 / RWKV_HEAD)
        yn = yc * lax.rsqrt(var + GN_EPS) * gnw + gnb
        bonus = _dot((r * k * rk).astype(BF16), head_ones)
        o_ref[0, sl, :] = ((yn + bonus * v) * g).astype(o_ref.dtype)
        return carry

    lax.fori_loop(0, rows // L, chunk, 0)


def _rwkv_scan(r, k, v, kk, a, lw, g, gn_w, gn_b, r_k, rows):
    B, S, W = r.shape
    npair = W // LANES
    seq = lambda: pl.BlockSpec((1, rows, LANES), lambda b, p, c: (b, c, p))
    par = lambda: pl.BlockSpec((1, LANES), lambda b, p, c: (0, p))
    return pl.pallas_call(
        functools.partial(_rwkv_scan_kernel, rows=rows),
        grid=(B, npair, S // rows),
        in_specs=[seq() for _ in range(7)] + [par(), par(), par()],
        out_specs=seq(),
        out_shape=jax.ShapeDtypeStruct((B, S, W), BF16),
        scratch_shapes=[pltpu.VMEM((LANES, LANES), F32)],
        compiler_params=_cparams(("parallel", "parallel", "arbitrary")),
        name="rwkv_scan",
    )(r, k, v, kk, a, lw, g, gn_w, gn_b, r_k)


def _rope128(x, cos, sin_signed):
    lane = lax.broadcasted_iota(I32, x.shape, 1)
    half = QK_ROPE // 2
    partner = jnp.where(lane < half, pltpu.roll(x, LANES - half, 1), pltpu.roll(x, half, 1))
    return x * cos + partner * sin_signed


def _mla_prep_kernel(cq_ref, ckv_ref, kr_ref, pos_ref, invf_ref, gqa_ref, gkva_ref,
                     wuq_ref, wukv_ref, gqn_ref, gkn_ref,
                     q_out, k_out, v_out, cqn_scr, ckvn_scr, cos_scr, sin_scr):
    @pl.when(pl.program_id(1) == 0)
    def _():
        cqn_scr[...] = _rms(cq_ref[...].astype(F32), gqa_ref[...]).astype(BF16)
        ckvn_scr[...] = _rms(ckv_ref[...].astype(F32), gkva_ref[...]).astype(BF16)
        ang = pos_ref[...].astype(F32) * invf_ref[...]
        lane = lax.broadcasted_iota(I32, ang.shape, 1)
        cos_scr[...] = jnp.cos(ang)
        sin_scr[...] = jnp.where(lane < QK_ROPE // 2, -1.0, 1.0) * jnp.sin(ang)

    cos = cos_scr[...]
    sin = sin_scr[...]
    inv_n = 1.0 / QK_HEAD
    q = _dot(cqn_scr[...], wuq_ref[...])
    rs = lax.rsqrt(jnp.sum(q * q, axis=-1, keepdims=True) * inv_n + NORM_EPS)
    q = q * rs * gqn_ref[...] * (QK_HEAD ** -0.5)
    q_out[...] = jnp.concatenate(
        [q[:, 0:LANES], _rope128(q[:, LANES:], cos, sin)], axis=1).astype(q_out.dtype)

    kv = _dot(ckvn_scr[...], wukv_ref[...])
    k_nope = kv[:, 0:QK_NOPE]
    kr = kr_ref[...].astype(F32)
    ssq = (jnp.sum(k_nope * k_nope, axis=-1, keepdims=True)
           + jnp.sum(kr * kr, axis=-1, keepdims=True))
    rs = lax.rsqrt(ssq * inv_n + NORM_EPS)
    gkn = gkn_ref[...]
    k_out[...] = jnp.concatenate(
        [k_nope * rs * gkn[:, 0:LANES], _rope128(kr * rs * gkn[:, LANES:], cos, sin)],
        axis=1).astype(k_out.dtype)
    v_out[...] = kv[:, QK_NOPE:].astype(v_out.dtype)


def _mla_prep(proj, pos, invf, g_qa, g_kva, wuq, wukv, gqn, gkn, tm):
    T = proj.shape[0]
    H = MLA_HEADS
    c0 = COL_MLA // Q_LORA
    ckr = (COL_MLA + Q_LORA + KV_LORA) // LANES
    full = lambda shape: pl.BlockSpec(shape, lambda i, h: (0, 0))
    return pl.pallas_call(
        _mla_prep_kernel,
        grid=(T // tm, H),
        in_specs=[
            pl.BlockSpec((tm, Q_LORA), lambda i, h: (i, c0)),
            pl.BlockSpec((tm, KV_LORA), lambda i, h: (i, c0 + 1)),
            pl.BlockSpec((tm, LANES), lambda i, h: (i, ckr)),
            pl.BlockSpec((tm, 1), lambda i, h: (i, 0)),
            full((1, LANES)), full((1, Q_LORA)), full((1, KV_LORA)),
            pl.BlockSpec((Q_LORA, QK_PAD), lambda i, h: (0, h)),
            pl.BlockSpec((KV_LORA, QK_NOPE + V_HEAD), lambda i, h: (0, h)),
            full((1, QK_PAD)), full((1, QK_PAD)),
        ],
        out_specs=[
            pl.BlockSpec((tm, QK_PAD), lambda i, h: (i, h)),
            pl.BlockSpec((tm, QK_PAD), lambda i, h: (i, h)),
            pl.BlockSpec((tm, V_HEAD), lambda i, h: (i, h)),
        ],
        out_shape=[jax.ShapeDtypeStruct((T, H * QK_PAD), BF16),
                   jax.ShapeDtypeStruct((T, H * QK_PAD), BF16),
                   jax.ShapeDtypeStruct((T, H * V_HEAD), BF16)],
        scratch_shapes=[pltpu.VMEM((tm, Q_LORA), BF16), pltpu.VMEM((tm, KV_LORA), BF16),
                        pltpu.VMEM((tm, LANES), F32), pltpu.VMEM((tm, LANES), F32)],
        compiler_params=_cparams(("parallel", "arbitrary")),
        name="mla_prep",
    )(proj, proj, proj, pos, invf, g_qa, g_kva, wuq, wukv, gqn, gkn)


MASK_VALUE = -1e30


def _flash_kernel(q_ref, k_ref, v_ref, o_ref, m_scr, l_scr, acc_scr, *, tq):
    i = pl.program_id(2)
    q = q_ref[0]
    m_scr[...] = jnp.full_like(m_scr, MASK_VALUE)
    l_scr[...] = jnp.zeros_like(l_scr)
    acc_scr[...] = jnp.zeros_like(acc_scr)

    def step(j, masked):
        ks = pl.ds(pl.multiple_of(j * tq, tq), tq)
        s = _dot_nt(q, k_ref[0, ks, :])
        if masked:
            rr = lax.broadcasted_iota(I32, s.shape, 0)
            cc = lax.broadcasted_iota(I32, s.shape, 1)
            s = jnp.where(cc <= rr, s, MASK_VALUE)
        m_old = m_scr[...]
        m_new = jnp.maximum(m_old, jnp.max(s, axis=-1, keepdims=True))
        p = jnp.exp(s - m_new)
        alpha = jnp.exp(m_old - m_new)
        l_scr[...] = alpha * l_scr[...] + jnp.sum(p, axis=-1, keepdims=True)
        acc_scr[...] = alpha * acc_scr[...] + _dot(p.astype(BF16), v_ref[0, ks, :])
        m_scr[...] = m_new

    def body(j, carry):
        step(j, False)
        return carry

    lax.fori_loop(0, i, body, 0)
    step(i, True)
    o_ref[0] = (acc_scr[...] / l_scr[...]).astype(o_ref.dtype)


def _flash(q, k, v, tq):
    B, S, _ = q.shape
    H = MLA_HEADS
    return pl.pallas_call(
        functools.partial(_flash_kernel, tq=tq),
        grid=(B, H, S // tq),
        in_specs=[
            pl.BlockSpec((1, tq, QK_PAD), lambda b, h, i: (b, i, h)),
            pl.BlockSpec((1, S, QK_PAD), lambda b, h, i: (b, 0, h)),
            pl.BlockSpec((1, S, V_HEAD), lambda b, h, i: (b, 0, h)),
        ],
        out_specs=pl.BlockSpec((1, tq, V_HEAD), lambda b, h, i: (b, i, h)),
        out_shape=jax.ShapeDtypeStruct((B, S, H * V_HEAD), BF16),
        scratch_shapes=[pltpu.VMEM((tq, 1), F32), pltpu.VMEM((tq, 1), F32),
                        pltpu.VMEM((tq, V_HEAD), F32)],
        compiler_params=_cparams(("parallel", "parallel", "arbitrary")),
        name="flash_attn",
    )(q, k, v)


def _merge_kernel(oa_ref, ob_ref, wa_ref, wb_ref, ga_ref, gb_ref, o_ref):
    ya = _dot(oa_ref[...], wa_ref[...])
    yb = _dot(ob_ref[...], wb_ref[...])
    ga = _sigmoid(ga_ref[...].astype(F32))
    gb = _sigmoid(gb_ref[...].astype(F32))
    o_ref[...] = (ga * ya + gb * yb).astype(o_ref.dtype)


def _merge(oa, ob, wa, wb, proj, tm, tn):
    T, K = oa.shape
    N = wa.shape[1]
    ca = COL_GA // tn
    cb = COL_GB // tn
    return pl.pallas_call(
        _merge_kernel,
        grid=(T // tm, N // tn),
        in_specs=[
            pl.BlockSpec((tm, K), lambda i, j: (i, 0)),
            pl.BlockSpec((tm, K), lambda i, j: (i, 0)),
            pl.BlockSpec((K, tn), lambda i, j: (0, j)),
            pl.BlockSpec((K, tn), lambda i, j: (0, j)),
            pl.BlockSpec((tm, tn), lambda i, j: (i, ca + j)),
            pl.BlockSpec((tm, tn), lambda i, j: (i, cb + j)),
        ],
        out_specs=pl.BlockSpec((tm, tn), lambda i, j: (i, j)),
        out_shape=jax.ShapeDtypeStruct((T, N), BF16),
        compiler_params=_cparams(("parallel", "parallel")),
        name="merge",
    )(oa, ob, wa, wb, proj, proj)


def _pack_rows(x, out_ref, tm):
    for c in range(PACK_ROWS):
        lo = x[:, c * 2 * LANES:c * 2 * LANES + LANES]
        hi = x[:, c * 2 * LANES + LANES:(c + 1) * 2 * LANES]
        lo_b = pltpu.bitcast(lo.astype(BF16).astype(F32), U32)
        hi_b = pltpu.bitcast(hi.astype(BF16).astype(F32), U32)
        out_ref[pl.ds(c, tm, stride=PACK_ROWS), :] = hi_b | (lo_b >> 16)


def _unpack_rows(ref, c, tm, lead=None):
    idx = (pl.ds(c, tm, stride=PACK_ROWS), slice(None))
    if lead is not None:
        idx = (lead,) + idx
    u = ref[idx]
    lo = pltpu.bitcast(u << 16, F32)
    hi = pltpu.bitcast(u & jnp.uint32(0xFFFF0000), F32)
    return lo, hi


def _wo_kernel(m_ref, x_ref, wo_ref, g_ref, wr_ref, x1_ref, h2_ref, h2p_ref, lg_ref, *, tm):
    x1 = x_ref[...] + _dot(m_ref[...], wo_ref[...])
    x1_ref[...] = x1
    h2 = _rms(x1, g_ref[...])
    h2_ref[...] = h2.astype(BF16)
    _pack_rows(h2, h2p_ref, tm)
    h_hi, h_lo = _split_bf16(h2)
    w_hi, w_lo = _split_bf16(wr_ref[...])
    lg_ref[...] = _dot_nt(w_hi, h_hi) + _dot_nt(w_hi, h_lo) + _dot_nt(w_lo, h_hi)


def _wo(merged, x2, wo, g, wr_t, tm):
    T, D = x2.shape
    E = wr_t.shape[0]
    return pl.pallas_call(
        functools.partial(_wo_kernel, tm=tm),
        grid=(T // tm,),
        in_specs=[
            pl.BlockSpec((tm, D), lambda i: (i, 0)),
            pl.BlockSpec((tm, D), lambda i: (i, 0)),
            pl.BlockSpec((D, D), lambda i: (0, 0)),
            pl.BlockSpec((1, D), lambda i: (0, 0)),
            pl.BlockSpec((E, D), lambda i: (0, 0)),
        ],
        out_specs=[
            pl.BlockSpec((tm, D), lambda i: (i, 0)),
            pl.BlockSpec((tm, D), lambda i: (i, 0)),
            pl.BlockSpec((tm * PACK_ROWS, PACK_W), lambda i: (i, 0)),
            pl.BlockSpec((E, tm), lambda i: (0, i)),
        ],
        out_shape=[jax.ShapeDtypeStruct((T, D), F32),
                   jax.ShapeDtypeStruct((T, D), BF16),
                   jax.ShapeDtypeStruct((T * PACK_ROWS, PACK_W), U32),
                   jax.ShapeDtypeStruct((E, T), F32)],
        compiler_params=_cparams(("parallel",)),
        name="wo_norm_router",
    )(merged, x2, wo, g, wr_t)


def _route_kernel(lg_ref, bias_ref, eidx_ref, wts_ref, rnk_ref, cnt_ref, carry_scr, *, tm):
    E = N_EXPERTS
    NEG = -jnp.inf

    @pl.when(pl.program_id(0) == 0)
    def _():
        carry_scr[...] = jnp.zeros_like(carry_scr)

    scores = _sigmoid(lg_ref[...])
    choice = scores + bias_ref[...]
    c3 = choice.reshape(N_GROUPS, GROUP_SIZE, tm)
    sub = lax.broadcasted_iota(I32, c3.shape, 1)
    m1 = jnp.max(c3, axis=1, keepdims=True)
    i1 = jnp.min(jnp.where(c3 == m1, sub, GROUP_SIZE), axis=1, keepdims=True)
    m2 = jnp.max(jnp.where(sub == i1, NEG, c3), axis=1, keepdims=True)
    gs = (m1 + m2).reshape(N_GROUPS, tm)
    gi = lax.broadcasted_iota(I32, gs.shape, 0)
    gsel = jnp.zeros(gs.shape, F32)
    for _ in range(TOPK_GROUPS):
        mx = jnp.max(gs, axis=0, keepdims=True)
        ix = jnp.min(jnp.where(gs == mx, gi, N_GROUPS), axis=0, keepdims=True)
        hit = gi == ix
        gsel = jnp.where(hit, 1.0, gsel)
        gs = jnp.where(hit, NEG, gs)
    emask = jnp.broadcast_to(gsel.reshape(N_GROUPS, 1, tm), (N_GROUPS, GROUP_SIZE, tm)).reshape(E, tm)
    x = jnp.where(emask > 0.5, choice, NEG)
    ei = lax.broadcasted_iota(I32, x.shape, 0)
    sel = jnp.zeros(x.shape, F32)
    idx_rows, w_rows = [], []
    for _ in range(TOP_K):
        mx = jnp.max(x, axis=0, keepdims=True)
        ix = jnp.min(jnp.where(x == mx, ei, E), axis=0, keepdims=True)
        hit = ei == ix
        w_rows.append(jnp.sum(jnp.where(hit, scores, 0.0), axis=0, keepdims=True))
        idx_rows.append(ix)
        sel = jnp.where(hit, 1.0, sel)
        x = jnp.where(hit, NEG, x)
    w = jnp.concatenate(w_rows, axis=0)
    w = w / (jnp.sum(w, axis=0, keepdims=True) + 1e-20) * ROUTED_SCALE
    eidx_ref[...] = jnp.concatenate(idx_rows, axis=0)
    wts_ref[...] = w

    upper = (lax.broadcasted_iota(I32, (tm, tm), 0) < lax.broadcasted_iota(I32, (tm, tm), 1)).astype(BF16)
    base = carry_scr[...][:, 0:1]
    excl = _dot(sel.astype(BF16), upper) + base
    rnk_ref[...] = jnp.concatenate(
        [jnp.sum(jnp.where(ei == ix, excl, 0.0), axis=0, keepdims=True) for ix in idx_rows],
        axis=0).astype(I32)
    carry_scr[...] = carry_scr[...] + jnp.sum(sel, axis=1, keepdims=True)
    cnt_ref[...] = carry_scr[...]


def _route(logits_t, bias, tm):
    E, T = logits_t.shape
    K = TOP_K
    tok = lambda: pl.BlockSpec((K, tm), lambda i: (0, i))
    return pl.pallas_call(
        functools.partial(_route_kernel, tm=tm),
        grid=(T // tm,),
        in_specs=[pl.BlockSpec((E, tm), lambda i: (0, i)),
                  pl.BlockSpec((E, 1), lambda i: (0, 0))],
        out_specs=[tok(), tok(), tok(), pl.BlockSpec((E, LANES), lambda i: (0, 0))],
        out_shape=[jax.ShapeDtypeStruct((K, T), I32), jax.ShapeDtypeStruct((K, T), F32),
                   jax.ShapeDtypeStruct((K, T), I32), jax.ShapeDtypeStruct((E, LANES), F32)],
        scratch_shapes=[pltpu.VMEM((E, LANES), F32)],
        compiler_params=_cparams(("arbitrary",)),
        name="route",
    )(logits_t, bias)


def _dispatch_kernel(eidx_ref, rnk_ref, pst_ref, h_ref, xs_in, xs_out, sem, *, tm):
    del xs_in

    def row_copy(t, dest):
        return pltpu.make_async_copy(
            h_ref.at[pl.ds(pl.multiple_of(t * PACK_ROWS, PACK_ROWS), PACK_ROWS), :],
            xs_out.at[pl.ds(pl.multiple_of(dest * PACK_ROWS, PACK_ROWS), PACK_ROWS), :],
            sem)

    def issue(t, carry):
        for k in range(TOP_K):
            row_copy(t, pst_ref[eidx_ref[k, t]] + rnk_ref[k, t]).start()
        return carry

    lax.fori_loop(0, tm, issue, 0)

    def drain(t, carry):
        for k in range(TOP_K):
            row_copy(0, 0).wait()
        return carry

    lax.fori_loop(0, tm, drain, 0)


def _dispatch(eidx, rnk, pstart, h2p, xs_zero, tm):
    K, T = eidx.shape
    smem_tok = lambda: pl.BlockSpec((K, tm), lambda i: (0, i), memory_space=pltpu.SMEM)
    return pl.pallas_call(
        functools.partial(_dispatch_kernel, tm=tm),
        grid=(T // tm,),
        in_specs=[smem_tok(), smem_tok(),
                  pl.BlockSpec(memory_space=pltpu.SMEM),
                  pl.BlockSpec((tm * PACK_ROWS, PACK_W), lambda i: (i, 0)),
                  pl.BlockSpec(memory_space=pl.ANY)],
        out_specs=pl.BlockSpec(memory_space=pl.ANY),
        out_shape=jax.ShapeDtypeStruct(xs_zero.shape, xs_zero.dtype),
        scratch_shapes=[pltpu.SemaphoreType.DMA(())],
        input_output_aliases={4: 0},
        compiler_params=_cparams(("arbitrary",)),
        name="dispatch",
    )(eidx, rnk, pstart, h2p, xs_zero)


def _ffn_kernel(te_ref, xs_ref, wg_ref, wu_ref, wd_ref, ys_ref, *, tm):
    del te_ref
    parts = []
    for c in range(PACK_ROWS):
        lo, hi = _unpack_rows(xs_ref, c, tm)
        parts += [lo.astype(BF16), hi.astype(BF16)]
    a = jnp.concatenate(parts, axis=1)
    hg = _dot(a, wg_ref[0])
    hu = _dot(a, wu_ref[0])
    hid = (hg * _sigmoid(hg) * hu).astype(BF16)
    _pack_rows(_dot(hid, wd_ref[0]), ys_ref, tm)


def _ffn(tile_e, xs, wg, wu, wd, tm):
    P8 = xs.shape[0]
    nt = P8 // (tm * PACK_ROWS)
    D, I = wg.shape[1], wg.shape[2]
    grid_spec = pltpu.PrefetchScalarGridSpec(
        num_scalar_prefetch=1,
        grid=(nt,),
        in_specs=[
            pl.BlockSpec((tm * PACK_ROWS, PACK_W), lambda i, te: (i, 0)),
            pl.BlockSpec((1, D, I), lambda i, te: (te[i], 0, 0)),
            pl.BlockSpec((1, D, I), lambda i, te: (te[i], 0, 0)),
            pl.BlockSpec((1, I, D), lambda i, te: (te[i], 0, 0)),
        ],
        out_specs=pl.BlockSpec((tm * PACK_ROWS, PACK_W), lambda i, te: (i, 0)),
    )
    return pl.pallas_call(
        functools.partial(_ffn_kernel, tm=tm),
        grid_spec=grid_spec,
        out_shape=jax.ShapeDtypeStruct(xs.shape, U32),
        compiler_params=_cparams(("parallel",)),
        name="expert_ffn",
    )(tile_e, xs, wg, wu, wd)


def _combine_kernel(eidx_ref, rnk_ref, pst_ref, ys_hbm, wt_ref, x1_ref, h2_ref,
                    wsg_ref, wsu_ref, wsd_ref, o_ref, buf, sem, *, tm):
    def issue(t, carry):
        for k in range(TOP_K):
            dest = pst_ref[eidx_ref[k, t]] + rnk_ref[k, t]
            pltpu.make_async_copy(
                ys_hbm.at[pl.ds(pl.multiple_of(dest * PACK_ROWS, PACK_ROWS), PACK_ROWS), :],
                buf.at[k, pl.ds(pl.multiple_of(t * PACK_ROWS, PACK_ROWS), PACK_ROWS), :],
                sem).start()
        return carry

    lax.fori_loop(0, tm, issue, 0)

    h2 = h2_ref[...]
    hg = _dot(h2, wsg_ref[...])
    hu = _dot(h2, wsu_ref[...])
    o_ref[...] = x1_ref[...] + _dot((hg * _sigmoid(hg) * hu).astype(BF16), wsd_ref[...])

    def drain(t, carry):
        for k in range(TOP_K):
            pltpu.make_async_copy(ys_hbm.at[pl.ds(0, PACK_ROWS), :],
                                  buf.at[0, pl.ds(0, PACK_ROWS), :], sem).wait()
        return carry

    lax.fori_loop(0, tm, drain, 0)

    wt = wt_ref[...]
    for c in range(PACK_ROWS):
        acc_lo = jnp.zeros((tm, LANES), F32)
        acc_hi = jnp.zeros((tm, LANES), F32)
        for k in range(TOP_K):
            lo, hi = _unpack_rows(buf, c, tm, lead=k)
            wk = wt[:, k:k + 1]
            acc_lo = acc_lo + wk * lo
            acc_hi = acc_hi + wk * hi
        s0 = c * 2 * LANES
        o_ref[:, s0:s0 + LANES] = o_ref[:, s0:s0 + LANES] + acc_lo
        o_ref[:, s0 + LANES:s0 + 2 * LANES] = o_ref[:, s0 + LANES:s0 + 2 * LANES] + acc_hi


def _combine(eidx, rnk, pstart, ys, wts_t, x1, h2, wsg, wsu, wsd, tm):
    K, T = eidx.shape
    D = x1.shape[1]
    I = wsg.shape[1]
    smem_tok = lambda: pl.BlockSpec((K, tm), lambda i: (0, i), memory_space=pltpu.SMEM)
    return pl.pallas_call(
        functools.partial(_combine_kernel, tm=tm),
        grid=(T // tm,),
        in_specs=[smem_tok(), smem_tok(),
                  pl.BlockSpec(memory_space=pltpu.SMEM),
                  pl.BlockSpec(memory_space=pl.ANY),
                  pl.BlockSpec((tm, K), lambda i: (i, 0)),
                  pl.BlockSpec((tm, D), lambda i: (i, 0)),
                  pl.BlockSpec((tm, D), lambda i: (i, 0)),
                  pl.BlockSpec((D, I), lambda i: (0, 0)),
                  pl.BlockSpec((D, I), lambda i: (0, 0)),
                  pl.BlockSpec((I, D), lambda i: (0, 0))],
        out_specs=pl.BlockSpec((tm, D), lambda i: (i, 0)),
        out_shape=jax.ShapeDtypeStruct((T, D), F32),
        scratch_shapes=[pltpu.VMEM((K, tm * PACK_ROWS, PACK_W), U32),
                        pltpu.SemaphoreType.DMA(())],
        compiler_params=_cparams(("arbitrary",)),
        name="combine",
    )(eidx, rnk, pstart, ys, wts_t, x1, h2, wsg, wsu, wsd)


def _ple_kernel(xf_ref, xt_ref, p_ref, wp_ref, gpost_ref, gin_ref, wg_ref, o_ref,
                hn_scr, ple_scr, *, tn):
    j = pl.program_id(1)

    @pl.when(j == 0)
    def _():
        hn_scr[...] = _rms(xf_ref[...], gin_ref[...]).astype(BF16)
        ple_scr[...] = _rms(_dot(p_ref[...].astype(BF16), wp_ref[...]), gpost_ref[...])

    cols = pl.ds(pl.multiple_of(j * tn, tn), tn)
    gate = _sigmoid(_dot(hn_scr[...], wg_ref[...]))
    o_ref[...] = xt_ref[...] + gate * ple_scr[:, cols]


def _ple(x2, p2, wp, gpost, gin, wg, tm, tn):
    T, D = x2.shape
    Pd = p2.shape[1]
    return pl.pallas_call(
        functools.partial(_ple_kernel, tn=tn),
        grid=(T // tm, D // tn),
        in_specs=[
            pl.BlockSpec((tm, D), lambda i, j: (i, 0)),
            pl.BlockSpec((tm, tn), lambda i, j: (i, j)),
            pl.BlockSpec((tm, Pd), lambda i, j: (i, 0)),
            pl.BlockSpec((Pd, D), lambda i, j: (0, 0)),
            pl.BlockSpec((1, D), lambda i, j: (0, 0)),
            pl.BlockSpec((1, D), lambda i, j: (0, 0)),
            pl.BlockSpec((D, tn), lambda i, j: (0, j)),
        ],
        out_specs=pl.BlockSpec((tm, tn), lambda i, j: (i, j)),
        out_shape=jax.ShapeDtypeStruct((T, D), F32),
        scratch_shapes=[pltpu.VMEM((tm, D), BF16), pltpu.VMEM((tm, D), F32)],
        compiler_params=_cparams(("parallel", "arbitrary")),
        name="ple",
    )(x2, x2, p2, wp, gpost, gin, wg)


def _pad_cols(w, n):
    return jnp.pad(w, ((0, 0), (0, n - w.shape[1])))


def _pad_rows(w, n):
    return jnp.pad(w, ((0, n - w.shape[0]), (0, 0)))


def _layout_w_in(w):
    W = RWKV_WIDTH
    c = 3 * W
    segs = [w[:, 0:c],
            _pad_cols(w[:, c:c + DECAY_LORA], LORA_PAD),
            _pad_cols(w[:, c + DECAY_LORA:c + DECAY_LORA + AAA_LORA], LORA_PAD),
            _pad_cols(w[:, c + DECAY_LORA + AAA_LORA:c + DECAY_LORA + AAA_LORA + GATE_LORA], GATE_LORA_PAD)]
    c += DECAY_LORA + AAA_LORA + GATE_LORA
    mla_cols = Q_LORA + KV_LORA + QK_ROPE
    segs.append(_pad_cols(w[:, c:c + mla_cols], MLA_PAD))
    c += mla_cols
    segs.append(w[:, c:])
    out = jnp.concatenate(segs, axis=1).astype(BF16)
    assert out.shape[1] == IN_PAD
    return out


def _layout_mu(mu):
    W = RWKV_WIDTH
    c = 3 * W
    segs = [mu[0:c],
            jnp.pad(mu[c:c + DECAY_LORA], (0, LORA_PAD - DECAY_LORA)),
            jnp.pad(mu[c + DECAY_LORA:c + DECAY_LORA + AAA_LORA], (0, LORA_PAD - AAA_LORA)),
            jnp.pad(mu[c + DECAY_LORA + AAA_LORA:], (0, GATE_LORA_PAD - GATE_LORA))]
    return jnp.concatenate(segs).reshape(1, RWKV_PAD)


def _layer(x, p, positions, g_mix, w_in, mu_rwkv, w0, w2, a0, a2, g2, k_k, k_a, r_k,
           gn_w, gn_b, w_a_up, g_qa, g_kva, w_uq, w_ukv, g_qn, g_kn, w_b_up, w_o,
           g_ffn, w_router, router_bias, w_exp_gate, w_exp_up, w_exp_down,
           w_sh_gate, w_sh_up, w_sh_down, w_ple, g_ple_post, g_ple_in, w_ple_gate,
           *, tiles):
    B, S, D = x.shape
    T = B * S
    W = RWKV_WIDTH
    x2 = x.reshape(T, D)
    row = lambda v: v.reshape(1, -1).astype(F32)

    proj = _inproj(x2, row(g_mix), _layout_w_in(w_in), tiles["tm_in"], tiles["tn_in"])

    head_blk = (jnp.arange(W)[:, None] // RWKV_HEAD == jnp.arange(W)[None, :] // RWKV_HEAD).astype(BF16)
    r, k, v, kk, a, lw, g = _rwkv_prep(
        proj, _layout_mu(mu_rwkv), row(w0), _pad_rows(w2, LORA_PAD).astype(BF16), row(a0),
        _pad_rows(a2, LORA_PAD).astype(BF16), _pad_rows(g2, GATE_LORA_PAD).astype(BF16),
        row(k_k), row(k_a), head_blk, tiles["tm_prep"], S)
    b3 = lambda t: t.reshape(B, S, W)
    o_a = _rwkv_scan(b3(r), b3(k), b3(v), b3(kk), b3(a), b3(lw), b3(g),
                     row(gn_w), row(gn_b), row(r_k), tiles["rows_scan"]).reshape(T, W)

    half = QK_ROPE // 2
    inv_freq = ROPE_THETA ** (-jnp.arange(half, dtype=F32) / half)
    invf = jnp.concatenate([inv_freq, inv_freq, jnp.zeros((LANES - QK_ROPE,), F32)]).reshape(1, LANES)
    wuq = jnp.pad(w_uq.reshape(Q_LORA, MLA_HEADS, QK_HEAD),
                  ((0, 0), (0, 0), (0, QK_PAD - QK_HEAD))).reshape(Q_LORA, MLA_HEADS * QK_PAD)
    pad_g = lambda gv: jnp.pad(gv, (0, QK_PAD - QK_HEAD)).reshape(1, QK_PAD)
    q, kx, vx = _mla_prep(proj, positions.reshape(T, 1).astype(I32), invf, row(g_qa), row(g_kva),
                          wuq.astype(BF16), w_ukv.astype(BF16), pad_g(g_qn), pad_g(g_kn),
                          tiles["tm_mla"])
    o_b = _flash(q.reshape(B, S, -1), kx.reshape(B, S, -1), vx.reshape(B, S, -1),
                 tiles["tq"]).reshape(T, MLA_WIDTH)

    merged = _merge(o_a, o_b, w_a_up.astype(BF16), w_b_up.astype(BF16), proj,
                    tiles["tm_merge"], tiles["tn_merge"])

    x1, h2, h2p, logits_t = _wo(merged, x2, w_o.astype(BF16), row(g_ffn),
                                jnp.transpose(w_router).astype(F32), tiles["tm_wo"])
    eidx, wts, rnk, cnt = _route(logits_t, router_bias.reshape(N_EXPERTS, 1).astype(F32),
                                 tiles["tm_route"])
    tmf = tiles["tm_ffn"]
    counts = cnt[:, 0].astype(I32)
    pcounts = (counts + tmf - 1) // tmf * tmf
    pends = jnp.cumsum(pcounts)
    pstart = (pends - pcounts).astype(I32)
    n_tiles = (T * TOP_K) // tmf + N_EXPERTS
    tile_start = jnp.arange(n_tiles, dtype=I32) * tmf
    tile_e = jnp.minimum(jnp.sum((pends[None, :] <= tile_start[:, None]).astype(I32), axis=1),
                         N_EXPERTS - 1)
    xs = _dispatch(eidx, rnk, pstart, h2p,
                   jnp.zeros((n_tiles * tmf * PACK_ROWS, PACK_W), U32), tiles["tm_disp"])
    ys = _ffn(tile_e, xs, w_exp_gate.astype(BF16), w_exp_up.astype(BF16),
              w_exp_down.astype(BF16), tmf)
    x3 = _combine(eidx, rnk, pstart, ys, jnp.transpose(wts), x1, h2,
                  w_sh_gate.astype(BF16), w_sh_up.astype(BF16), w_sh_down.astype(BF16),
                  tiles["tm_comb"])

    out = _ple(x3, p.reshape(T, PLE_DIM), w_ple.astype(BF16), row(g_ple_post), row(g_ple_in),
               w_ple_gate.astype(BF16), tiles["tm_ple"], tiles["tn_ple"])
    return out.reshape(B, S, D)


TILES = dict(tm_in=512, tn_in=512, tm_prep=256, rows_scan=512, tm_mla=512, tq=512,
             tm_merge=512, tn_merge=512, tm_wo=256, tm_route=512, tm_ffn=256,
             tm_disp=512, tm_comb=256, tm_ple=512, tn_ple=512)


def kernel(x, p, positions, g_mix, w_in, mu_rwkv, w0, w2, a0, a2, g2, k_k, k_a, r_k, gn_w, gn_b, w_a_up, g_qa, g_kva, w_uq, w_ukv, g_qn, g_kn, w_b_up, w_o, g_ffn, w_router, router_bias, w_exp_gate, w_exp_up, w_exp_down, w_sh_gate, w_sh_up, w_sh_down, w_ple, g_ple_post, g_ple_in, w_ple_gate):
    args = (g_mix, w_in, mu_rwkv, w0, w2, a0, a2, g2, k_k, k_a, r_k, gn_w, gn_b, w_a_up,
            g_qa, g_kva, w_uq, w_ukv, g_qn, g_kn, w_b_up, w_o, g_ffn, w_router, router_bias,
            w_exp_gate, w_exp_up, w_exp_down, w_sh_gate, w_sh_up, w_sh_down, w_ple,
            g_ple_post, g_ple_in, w_ple_gate)
    assert all(t.shape[0] == 1 for t in args), "single-layer stack expected"
    return _layer(x, p[0], positions, *[t[0] for t in args], tiles=TILES)
```

```python
import functools

import jax
import jax.numpy as jnp
from jax import lax
from jax.experimental import pallas as pl
from jax.experimental.pallas import tpu as pltpu

F32 = jnp.float32
BF16 = jnp.bfloat16
I32 = jnp.int32
U32 = jnp.uint32

D_MODEL = 2048
PLE_DIM = 256
NORM_EPS = 1e-6
RWKV_HEADS = 16
RWKV_HEAD = 64
RWKV_WIDTH = RWKV_HEADS * RWKV_HEAD
DECAY_LORA = 64
AAA_LORA = 64
GATE_LORA = 160
GN_EPS = 64e-5
MLA_HEADS = 8
Q_LORA = 512
KV_LORA = 512
QK_NOPE = 128
QK_ROPE = 64
QK_HEAD = QK_NOPE + QK_ROPE
V_HEAD = 128
MLA_WIDTH = MLA_HEADS * V_HEAD
ROPE_THETA = 10000.0
N_EXPERTS = 64
TOP_K = 8
N_GROUPS = 8
GROUP_SIZE = N_EXPERTS // N_GROUPS
TOPK_GROUPS = 4
MOE_INTER = 512
ROUTED_SCALE = 2.5

LANES = 128
QK_PAD = 2 * LANES
PACK_ROWS = 8
PACK_W = D_MODEL // 2 // PACK_ROWS
assert PACK_W == LANES

LORA_PAD = 128
GATE_LORA_PAD = 256
RWKV_PAD = 3 * RWKV_WIDTH + 2 * LORA_PAD + GATE_LORA_PAD
MLA_PAD = 1536
COL_RWKV = 0
COL_MLA = RWKV_PAD
COL_GA = COL_MLA + MLA_PAD
COL_GB = COL_GA + D_MODEL
IN_PAD = COL_GB + D_MODEL

VMEM_LIMIT = 56 * 1024 * 1024
CHUNK = 64


def _cparams(sem):
    return pltpu.CompilerParams(dimension_semantics=sem, vmem_limit_bytes=VMEM_LIMIT)


def _dot(a, b):
    return jnp.dot(a, b, preferred_element_type=F32)


def _dot_nt(a, b):
    return lax.dot_general(a, b, (((1,), (1,)), ((), ())), preferred_element_type=F32)


def _dot_tn(a, b):
    return lax.dot_general(a, b, (((0,), (0,)), ((), ())), preferred_element_type=F32)


def _split_bf16(x):
    hi = x.astype(BF16)
    lo = (x - hi.astype(F32)).astype(BF16)
    return hi, lo


def _rms(x, g):
    ms = jnp.mean(x * x, axis=-1, keepdims=True)
    return x * lax.rsqrt(ms + NORM_EPS) * g


def _sigmoid(x):
    return 1.0 / (1.0 + jnp.exp(-x))


def _inproj_kernel(x_ref, g_ref, w_ref, o_ref, h_scr):
    @pl.when(pl.program_id(1) == 0)
    def _():
        h_scr[...] = _rms(x_ref[...], g_ref[...]).astype(BF16)

    o_ref[...] = _dot(h_scr[...], w_ref[...]).astype(o_ref.dtype)


def _inproj(x2, g, w, tm, tn):
    T, D = x2.shape
    N = w.shape[1]
    return pl.pallas_call(
        _inproj_kernel,
        grid=(T // tm, N // tn),
        in_specs=[
            pl.BlockSpec((tm, D), lambda i, j: (i, 0)),
            pl.BlockSpec((1, D), lambda i, j: (0, 0)),
            pl.BlockSpec((D, tn), lambda i, j: (0, j)),
        ],
        out_specs=pl.BlockSpec((tm, tn), lambda i, j: (i, j)),
        out_shape=jax.ShapeDtypeStruct((T, N), BF16),
        scratch_shapes=[pltpu.VMEM((tm, D), BF16)],
        compiler_params=_cparams(("parallel", "arbitrary")),
        name="inproj",
    )(x2, g, w)


HALO = 16


def _rwkv_prep_kernel(cur_ref, halo_ref, mu_ref, w0_ref, w2_ref, a0_ref, a2_ref, g2_ref,
                      kk_ref, ka_ref, bd_ref,
                      r_out, k_out, v_out, kk_out, a_out, lw_out, g_out, *, tm, seq):
    i = pl.program_id(0)
    cur = cur_ref[...].astype(F32)
    first = (i * tm) % seq == 0
    last_prev = halo_ref[HALO - 1:HALO, :].astype(F32)
    last_prev = jnp.where(first, 0.0, last_prev)
    row = lax.broadcasted_iota(I32, cur.shape, 0)
    prev = jnp.where(row == 0, last_prev, pltpu.roll(cur, 1, 0))
    u = cur + (prev - cur) * mu_ref[...]
    W = RWKV_WIDTH
    r = u[:, 0:W]
    k = u[:, W:2 * W]
    v = u[:, 2 * W:3 * W]
    dw = u[:, 3 * W:3 * W + LORA_PAD]
    da = u[:, 3 * W + LORA_PAD:3 * W + 2 * LORA_PAD]
    dg = u[:, 3 * W + 2 * LORA_PAD:3 * W + 2 * LORA_PAD + GATE_LORA_PAD]
    z = w0_ref[...] + _dot(jnp.tanh(dw).astype(BF16), w2_ref[...])
    w_log = -(jnp.maximum(-z, 0.0) + jnp.log(1.0 + jnp.exp(-jnp.abs(z)))) - 0.5
    lw_out[...] = -jnp.exp(w_log)
    a = _sigmoid(a0_ref[...] + _dot(da.astype(BF16), a2_ref[...]))
    g_out[...] = _dot(_sigmoid(dg).astype(BF16), g2_ref[...]).astype(g_out.dtype)
    kk = k * kk_ref[...]
    hi, lo = _split_bf16(kk * kk)
    ssq = _dot(hi, bd_ref[...]) + _dot(lo, bd_ref[...])
    kk = kk / jnp.maximum(jnp.sqrt(ssq), 1e-12)
    r_out[...] = r.astype(r_out.dtype)
    k_out[...] = (k * (1.0 + (a - 1.0) * ka_ref[...])).astype(k_out.dtype)
    v_out[...] = v.astype(v_out.dtype)
    kk_out[...] = kk.astype(kk_out.dtype)
    a_out[...] = a.astype(a_out.dtype)


def _rwkv_prep(proj, mu, w0, w2, a0, a2, g2, k_k, k_a, bd, tm, seq):
    T = proj.shape[0]
    W = RWKV_WIDTH
    full = lambda shape: pl.BlockSpec(shape, lambda i: (0, 0))
    out = lambda: pl.BlockSpec((tm, W), lambda i: (i, 0))
    return pl.pallas_call(
        functools.partial(_rwkv_prep_kernel, tm=tm, seq=seq),
        grid=(T // tm,),
        in_specs=[
            pl.BlockSpec((tm, RWKV_PAD), lambda i: (i, 0)),
            pl.BlockSpec((HALO, RWKV_PAD), lambda i: (jnp.maximum(i * (tm // HALO) - 1, 0), 0)),
            full((1, RWKV_PAD)), full((1, W)), full((LORA_PAD, W)), full((1, W)),
            full((LORA_PAD, W)), full((GATE_LORA_PAD, W)), full((1, W)), full((1, W)),
            full((W, W)),
        ],
        out_specs=[out() for _ in range(7)],
        out_shape=[jax.ShapeDtypeStruct((T, W), dt)
                   for dt in (BF16, BF16, BF16, BF16, BF16, F32, BF16)],
        compiler_params=_cparams(("parallel",)),
        name="rwkv_prep",
    )(proj, proj, mu, w0, w2, a0, a2, g2, k_k, k_a, bd)


def _rwkv_scan_kernel(r_ref, k_ref, v_ref, kk_ref, a_ref, lw_ref, g_ref,
                      gnw_ref, gnb_ref, rk_ref, o_ref, s_scr, *, rows):
    L = CHUNK
    L2 = 2 * L

    @pl.when(pl.program_id(2) == 0)
    def _():
        s_scr[...] = jnp.zeros_like(s_scr)

    lane = lax.broadcasted_iota(I32, (L, LANES), 1)
    m_a = (lane < RWKV_HEAD).astype(F32)
    m_b = 1.0 - m_a
    ri = lax.broadcasted_iota(I32, (L2, L2), 0)
    ci = lax.broadcasted_iota(I32, (L2, L2), 1)
    strict = ri > ci
    incl = ri >= ci
    eye = (ri == ci).astype(F32)
    tri = (lax.broadcasted_iota(I32, (L, L), 0) >= lax.broadcasted_iota(I32, (L, L), 1)).astype(BF16)
    hr = lax.broadcasted_iota(I32, (LANES, LANES), 0) // RWKV_HEAD
    hc = lax.broadcasted_iota(I32, (LANES, LANES), 1) // RWKV_HEAD
    head_ones = (hr == hc).astype(BF16)
    gnw = gnw_ref[...]
    gnb = gnb_ref[...]
    rk = rk_ref[...]

    def stack(x):
        return jnp.concatenate([x * m_a, x * m_b], axis=0)

    chunks = range(rows // L)
    rows_of = lambda c: slice(c * L, (c + 1) * L)
    each = lambda f, *lists: [f(*vals) for vals in zip(*lists)]

    def operands(c):
        sl = rows_of(c)
        r = r_ref[0, sl, :].astype(F32)
        k = k_ref[0, sl, :].astype(F32)
        v = v_ref[0, sl, :].astype(F32)
        kk = kk_ref[0, sl, :].astype(F32)
        a = a_ref[0, sl, :].astype(F32)
        lw = lw_ref[0, sl, :]
        hi, lo = _split_bf16(lw)
        cum = _dot(tri, hi) + _dot(tri, lo)
        c_end = cum[L - 1:L, :]
        p_inv = jnp.exp(-cum)
        p_end = jnp.exp(c_end - cum)
        b = kk * a
        return dict(
            x_a=stack(-(kk * jnp.exp(cum - lw))).astype(BF16),
            x_r=stack(r * jnp.exp(cum)),
            x_bk=jnp.concatenate([stack(b * p_inv), stack(k * p_inv)], axis=0).astype(BF16),
            v_st=stack(v).astype(BF16),
            z_hat=jnp.concatenate([stack(b * p_end), stack(k * p_end)], axis=0).astype(BF16),
            p_row=jnp.exp(c_end),
            rkr=(r * k * rk).astype(BF16), v=v)

    ops = [operands(c) for c in chunks]
    big = [_dot_nt(jnp.concatenate([o["x_a"], o["x_r"].astype(BF16)], axis=0), o["x_bk"])
           for o in ops]
    n_pow = [jnp.where(strict, m[0:L2, 0:L2], 0.0) for m in big]
    a_ak = [jnp.where(strict, m[0:L2, L2:2 * L2], 0.0).astype(BF16) for m in big]
    a_r = [jnp.concatenate([jnp.where(incl, m[L2:2 * L2, 0:L2], 0.0),
                            jnp.where(incl, m[L2:2 * L2, L2:2 * L2], 0.0)], axis=1).astype(BF16)
           for m in big]

    t_inv = [eye + n for n in n_pow]
    for _ in range(5):
        n_pow = each(lambda n: _dot(n.astype(BF16), n.astype(BF16)), n_pow)
        t_inv = each(lambda t, n: t + _dot(t.astype(BF16), n.astype(BF16)), t_inv, n_pow)

    akv = each(lambda m, o: _dot(m, o["v_st"]).astype(BF16), a_ak, ops)
    w = each(lambda t, o, x: _dot(t.astype(BF16), jnp.concatenate([o["x_a"], x], axis=1)),
             t_inv, ops, akv)
    w1 = [m[:, 0:LANES].astype(BF16) for m in w]
    w2v = each(lambda m, o: jnp.concatenate([m[:, LANES:].astype(BF16), o["v_st"]], axis=0), w, ops)
    g_mat = each(lambda o, m, x: (o["x_r"] + _dot(m[:, 0:L2], x)).astype(BF16), ops, a_r, w1)
    y0 = each(_dot, a_r, w2v)
    m_mat = each(lambda x, o: _dot_tn(x, o["z_hat"][0:L2, :]).astype(BF16), w1, ops)
    c2 = each(lambda x, o: _dot_tn(x, o["z_hat"]), w2v, ops)
    bonus_v = [_dot(o["rkr"], head_ones) * o["v"] for o in ops]

    s = s_scr[...]
    y = []
    for c in chunks:
        s_b = s.astype(BF16)
        y_st = _dot_nt(g_mat[c], s_b) + y0[c]
        s = s * ops[c]["p_row"] + _dot(s_b, m_mat[c]) + c2[c]
        y.append(y_st[0:L, :] + y_st[L:L2, :])
    s_scr[...] = s

    mean = [_dot(t.astype(BF16), head_ones) * (1.0 / RWKV_HEAD) for t in y]
    yc = each(lambda t, m: t - m, y, mean)
    var = [_dot((t * t).astype(BF16), head_ones) * (1.0 / RWKV_HEAD) for t in yc]
    for c in chunks:
        yn = yc[c] * lax.rsqrt(var[c] + GN_EPS) * gnw + gnb
        sl = rows_of(c)
        o_ref[0, sl, :] = ((yn + bonus_v[c]) * g_ref[0, sl, :].astype(F32)).astype(o_ref.dtype)


def _rwkv_scan(r, k, v, kk, a, lw, g, gn_w, gn_b, r_k, rows):
    B, S, W = r.shape
    npair = W // LANES
    seq = lambda: pl.BlockSpec((1, rows, LANES), lambda b, p, c: (b, c, p))
    par = lambda: pl.BlockSpec((1, LANES), lambda b, p, c: (0, p))
    return pl.pallas_call(
        functools.partial(_rwkv_scan_kernel, rows=rows),
        grid=(B, npair, S // rows),
        in_specs=[seq() for _ in range(7)] + [par(), par(), par()],
        out_specs=seq(),
        out_shape=jax.ShapeDtypeStruct((B, S, W), BF16),
        scratch_shapes=[pltpu.VMEM((LANES, LANES), F32)],
        compiler_params=_cparams(("parallel", "parallel", "arbitrary")),
        name="rwkv_scan",
    )(r, k, v, kk, a, lw, g, gn_w, gn_b, r_k)


def _rope128(x, cos, sin_signed):
    lane = lax.broadcasted_iota(I32, x.shape, 1)
    half = QK_ROPE // 2
    partner = jnp.where(lane < half, pltpu.roll(x, LANES - half, 1), pltpu.roll(x, half, 1))
    return x * cos + partner * sin_signed


def _mla_prep_kernel(cq_ref, ckv_ref, kr_ref, pos_ref, invf_ref, gqa_ref, gkva_ref,
                     wuq_ref, wukv_ref, gqn_ref, gkn_ref,
                     q_out, k_out, v_out, cqn_scr, ckvn_scr, cos_scr, sin_scr):
    @pl.when(pl.program_id(1) == 0)
    def _():
        cqn_scr[...] = _rms(cq_ref[...].astype(F32), gqa_ref[...]).astype(BF16)
        ckvn_scr[...] = _rms(ckv_ref[...].astype(F32), gkva_ref[...]).astype(BF16)
        ang = pos_ref[...].astype(F32) * invf_ref[...]
        lane = lax.broadcasted_iota(I32, ang.shape, 1)
        cos_scr[...] = jnp.cos(ang)
        sin_scr[...] = jnp.where(lane < QK_ROPE // 2, -1.0, 1.0) * jnp.sin(ang)

    cos = cos_scr[...]
    sin = sin_scr[...]
    inv_n = 1.0 / QK_HEAD
    q = _dot(cqn_scr[...], wuq_ref[...])
    rs = lax.rsqrt(jnp.sum(q * q, axis=-1, keepdims=True) * inv_n + NORM_EPS)
    q = q * rs * gqn_ref[...] * (QK_HEAD ** -0.5 * LOG2E)
    q_out[...] = jnp.concatenate(
        [q[:, 0:LANES], _rope128(q[:, LANES:], cos, sin)], axis=1).astype(q_out.dtype)

    kv = _dot(ckvn_scr[...], wukv_ref[...])
    k_nope = kv[:, 0:QK_NOPE]
    kr = kr_ref[...].astype(F32)
    ssq = (jnp.sum(k_nope * k_nope, axis=-1, keepdims=True)
           + jnp.sum(kr * kr, axis=-1, keepdims=True))
    rs = lax.rsqrt(ssq * inv_n + NORM_EPS)
    gkn = gkn_ref[...]
    k_out[...] = jnp.concatenate(
        [k_nope * rs * gkn[:, 0:LANES], _rope128(kr * rs * gkn[:, LANES:], cos, sin)],
        axis=1).astype(k_out.dtype)
    v_out[...] = jnp.transpose(kv[:, QK_NOPE:]).astype(v_out.dtype)


def _mla_prep(proj, pos, invf, g_qa, g_kva, wuq, wukv, gqn, gkn, tm):
    T = proj.shape[0]
    H = MLA_HEADS
    c0 = COL_MLA // Q_LORA
    ckr = (COL_MLA + Q_LORA + KV_LORA) // LANES
    full = lambda shape: pl.BlockSpec(shape, lambda i, h: (0, 0))
    return pl.pallas_call(
        _mla_prep_kernel,
        grid=(T // tm, H),
        in_specs=[
            pl.BlockSpec((tm, Q_LORA), lambda i, h: (i, c0)),
            pl.BlockSpec((tm, KV_LORA), lambda i, h: (i, c0 + 1)),
            pl.BlockSpec((tm, LANES), lambda i, h: (i, ckr)),
            pl.BlockSpec((tm, 1), lambda i, h: (i, 0)),
            full((1, LANES)), full((1, Q_LORA)), full((1, KV_LORA)),
            pl.BlockSpec((Q_LORA, QK_PAD), lambda i, h: (0, h)),
            pl.BlockSpec((KV_LORA, QK_NOPE + V_HEAD), lambda i, h: (0, h)),
            full((1, QK_PAD)), full((1, QK_PAD)),
        ],
        out_specs=[
            pl.BlockSpec((tm, QK_PAD), lambda i, h: (i, h)),
            pl.BlockSpec((tm, QK_PAD), lambda i, h: (i, h)),
            pl.BlockSpec((V_HEAD, tm), lambda i, h: (h, i)),
        ],
        out_shape=[jax.ShapeDtypeStruct((T, H * QK_PAD), BF16),
                   jax.ShapeDtypeStruct((T, H * QK_PAD), BF16),
                   jax.ShapeDtypeStruct((H * V_HEAD, T), BF16)],
        scratch_shapes=[pltpu.VMEM((tm, Q_LORA), BF16), pltpu.VMEM((tm, KV_LORA), BF16),
                        pltpu.VMEM((tm, LANES), F32), pltpu.VMEM((tm, LANES), F32)],
        compiler_params=_cparams(("parallel", "arbitrary")),
        name="mla_prep",
    )(proj, proj, proj, pos, invf, g_qa, g_kva, wuq, wukv, gqn, gkn)


MASK_VALUE = -1e30
LOG2E = 1.4426950408889634


def _flash_kernel(q_ref, k_ref, vt_ref, o_ref, s_scr, p_scr, alpha_scr, m_scr, l_scr, acc_scr,
                  *, tq):
    i = pl.program_id(2)
    q = q_ref[0]
    m_scr[...] = jnp.full_like(m_scr, MASK_VALUE)
    l_scr[...] = jnp.zeros_like(l_scr)
    acc_scr[...] = jnp.zeros_like(acc_scr)

    def scores(j, slot):
        ks = pl.ds(pl.multiple_of(j * tq, tq), tq)
        s_scr[slot] = _dot_nt(k_ref[0, ks, :], q)

    def softmax(slot, masked):
        for c in range(tq // LANES):
            cols = slice(c * LANES, (c + 1) * LANES)
            s = s_scr[slot, :, cols]
            if masked:
                kpos = lax.broadcasted_iota(I32, s.shape, 0)
                qpos = lax.broadcasted_iota(I32, s.shape, 1) + c * LANES
                s = jnp.where(kpos <= qpos, s, MASK_VALUE)
            m_old = m_scr[:, cols]
            m_new = jnp.maximum(m_old, jnp.max(s, axis=0, keepdims=True))
            p = jnp.exp2(s - m_new)
            alpha = jnp.exp2(m_old - m_new)
            l_scr[:, cols] = alpha * l_scr[:, cols] + jnp.sum(p, axis=0, keepdims=True)
            m_scr[:, cols] = m_new
            p_scr[slot, :, cols] = p.astype(BF16)
            alpha_scr[slot, :, cols] = alpha

    def weighted_values(j, slot):
        vt = vt_ref[:, pl.ds(pl.multiple_of(j * tq, tq), tq)]
        acc_scr[...] = alpha_scr[slot] * acc_scr[...] + _dot(vt, p_scr[slot])

    scores(0, 0)
    p_scr[1] = jnp.zeros((tq, tq), BF16)
    alpha_scr[1] = jnp.ones((1, tq), F32)

    def body(j, carry):
        slot = j % 2
        weighted_values(jnp.maximum(j - 1, 0), 1 - slot)
        softmax(slot, False)
        scores(j + 1, 1 - slot)
        return carry

    lax.fori_loop(0, i, body, 0)
    last = i % 2
    weighted_values(jnp.maximum(i - 1, 0), 1 - last)
    softmax(last, True)
    weighted_values(i, last)
    o_ref[0] = jnp.transpose(acc_scr[...] / l_scr[...]).astype(o_ref.dtype)


def _flash(q, k, vt, tq):
    B, S, _ = q.shape
    H = MLA_HEADS
    return pl.pallas_call(
        functools.partial(_flash_kernel, tq=tq),
        grid=(B, H, S // tq),
        in_specs=[
            pl.BlockSpec((1, tq, QK_PAD), lambda b, h, i: (b, i, h)),
            pl.BlockSpec((1, S, QK_PAD), lambda b, h, i: (b, 0, h)),
            pl.BlockSpec((V_HEAD, S), lambda b, h, i: (h, b)),
        ],
        out_specs=pl.BlockSpec((1, tq, V_HEAD), lambda b, h, i: (b, i, h)),
        out_shape=jax.ShapeDtypeStruct((B, S, H * V_HEAD), BF16),
        scratch_shapes=[pltpu.VMEM((2, tq, tq), F32), pltpu.VMEM((2, tq, tq), BF16),
                        pltpu.VMEM((2, 1, tq), F32),
                        pltpu.VMEM((1, tq), F32), pltpu.VMEM((1, tq), F32),
                        pltpu.VMEM((V_HEAD, tq), F32)],
        compiler_params=_cparams(("parallel", "parallel", "arbitrary")),
        name="flash_attn",
    )(q, k, vt)


def _merge_kernel(oa_ref, ob_ref, wa_ref, wb_ref, ga_ref, gb_ref, o_ref):
    ya = _dot(oa_ref[...], wa_ref[...])
    yb = _dot(ob_ref[...], wb_ref[...])
    ga = _sigmoid(ga_ref[...].astype(F32))
    gb = _sigmoid(gb_ref[...].astype(F32))
    o_ref[...] = (ga * ya + gb * yb).astype(o_ref.dtype)


def _merge(oa, ob, wa, wb, proj, tm, tn):
    T, K = oa.shape
    N = wa.shape[1]
    ca = COL_GA // tn
    cb = COL_GB // tn
    return pl.pallas_call(
        _merge_kernel,
        grid=(T // tm, N // tn),
        in_specs=[
            pl.BlockSpec((tm, K), lambda i, j: (i, 0)),
            pl.BlockSpec((tm, K), lambda i, j: (i, 0)),
            pl.BlockSpec((K, tn), lambda i, j: (0, j)),
            pl.BlockSpec((K, tn), lambda i, j: (0, j)),
            pl.BlockSpec((tm, tn), lambda i, j: (i, ca + j)),
            pl.BlockSpec((tm, tn), lambda i, j: (i, cb + j)),
        ],
        out_specs=pl.BlockSpec((tm, tn), lambda i, j: (i, j)),
        out_shape=jax.ShapeDtypeStruct((T, N), BF16),
        compiler_params=_cparams(("parallel", "parallel")),
        name="merge",
    )(oa, ob, wa, wb, proj, proj)


def _pack_rows(x, out_ref, tm):
    for c in range(PACK_ROWS):
        lo = x[:, c * 2 * LANES:c * 2 * LANES + LANES]
        hi = x[:, c * 2 * LANES + LANES:(c + 1) * 2 * LANES]
        lo_b = pltpu.bitcast(lo.astype(BF16).astype(F32), U32)
        hi_b = pltpu.bitcast(hi.astype(BF16).astype(F32), U32)
        out_ref[pl.ds(c, tm, stride=PACK_ROWS), :] = hi_b | (lo_b >> 16)


def _unpack_rows(ref, c, tm, lead=None):
    idx = (pl.ds(c, tm, stride=PACK_ROWS), slice(None))
    if lead is not None:
        idx = (lead,) + idx
    u = ref[idx]
    lo = pltpu.bitcast(u << 16, F32)
    hi = pltpu.bitcast(u & jnp.uint32(0xFFFF0000), F32)
    return lo, hi


def _wo_kernel(m_ref, x_ref, wo_ref, g_ref, wr_ref, x1_ref, h2_ref, h2p_ref, lg_ref, *, tm):
    x1 = x_ref[...] + _dot(m_ref[...], wo_ref[...])
    x1_ref[...] = x1
    h2 = _rms(x1, g_ref[...])
    h2_ref[...] = h2.astype(BF16)
    _pack_rows(h2, h2p_ref, tm)
    h_hi, h_lo = _split_bf16(h2)
    w_hi, w_lo = _split_bf16(wr_ref[...])
    lg_ref[...] = _dot_nt(w_hi, h_hi) + _dot_nt(w_hi, h_lo) + _dot_nt(w_lo, h_hi)


def _wo(merged, x2, wo, g, wr_t, tm):
    T, D = x2.shape
    E = wr_t.shape[0]
    return pl.pallas_call(
        functools.partial(_wo_kernel, tm=tm),
        grid=(T // tm,),
        in_specs=[
            pl.BlockSpec((tm, D), lambda i: (i, 0)),
            pl.BlockSpec((tm, D), lambda i: (i, 0)),
            pl.BlockSpec((D, D), lambda i: (0, 0)),
            pl.BlockSpec((1, D), lambda i: (0, 0)),
            pl.BlockSpec((E, D), lambda i: (0, 0)),
        ],
        out_specs=[
            pl.BlockSpec((tm, D), lambda i: (i, 0)),
            pl.BlockSpec((tm, D), lambda i: (i, 0)),
            pl.BlockSpec((tm * PACK_ROWS, PACK_W), lambda i: (i, 0)),
            pl.BlockSpec((E, tm), lambda i: (0, i)),
        ],
        out_shape=[jax.ShapeDtypeStruct((T, D), F32),
                   jax.ShapeDtypeStruct((T, D), BF16),
                   jax.ShapeDtypeStruct((T * PACK_ROWS, PACK_W), U32),
                   jax.ShapeDtypeStruct((E, T), F32)],
        compiler_params=_cparams(("parallel",)),
        name="wo_norm_router",
    )(merged, x2, wo, g, wr_t)


def _route_kernel(lg_ref, bias_ref, eidx_ref, wts_ref, rnk_ref, cnt_ref, carry_scr, *, tm):
    E = N_EXPERTS
    NEG = -jnp.inf

    @pl.when(pl.program_id(0) == 0)
    def _():
        carry_scr[...] = jnp.zeros_like(carry_scr)

    scores = _sigmoid(lg_ref[...])
    choice = scores + bias_ref[...]
    c3 = choice.reshape(N_GROUPS, GROUP_SIZE, tm)
    sub = lax.broadcasted_iota(I32, c3.shape, 1)
    m1 = jnp.max(c3, axis=1, keepdims=True)
    i1 = jnp.min(jnp.where(c3 == m1, sub, GROUP_SIZE), axis=1, keepdims=True)
    m2 = jnp.max(jnp.where(sub == i1, NEG, c3), axis=1, keepdims=True)
    gs = (m1 + m2).reshape(N_GROUPS, tm)
    gi = lax.broadcasted_iota(I32, gs.shape, 0)
    gsel = jnp.zeros(gs.shape, F32)
    for _ in range(TOPK_GROUPS):
        mx = jnp.max(gs, axis=0, keepdims=True)
        ix = jnp.min(jnp.where(gs == mx, gi, N_GROUPS), axis=0, keepdims=True)
        hit = gi == ix
        gsel = jnp.where(hit, 1.0, gsel)
        gs = jnp.where(hit, NEG, gs)
    emask = jnp.broadcast_to(gsel.reshape(N_GROUPS, 1, tm), (N_GROUPS, GROUP_SIZE, tm)).reshape(E, tm)
    x = jnp.where(emask > 0.5, choice, NEG)
    ei = lax.broadcasted_iota(I32, x.shape, 0)
    sel = jnp.zeros(x.shape, F32)
    idx_rows, w_rows = [], []
    for _ in range(TOP_K):
        mx = jnp.max(x, axis=0, keepdims=True)
        ix = jnp.min(jnp.where(x == mx, ei, E), axis=0, keepdims=True)
        hit = ei == ix
        w_rows.append(jnp.sum(jnp.where(hit, scores, 0.0), axis=0, keepdims=True))
        idx_rows.append(ix)
        sel = jnp.where(hit, 1.0, sel)
        x = jnp.where(hit, NEG, x)
    w = jnp.concatenate(w_rows, axis=0)
    w = w / (jnp.sum(w, axis=0, keepdims=True) + 1e-20) * ROUTED_SCALE
    eidx_ref[...] = jnp.concatenate(idx_rows, axis=0)
    wts_ref[...] = w

    upper = (lax.broadcasted_iota(I32, (tm, tm), 0) < lax.broadcasted_iota(I32, (tm, tm), 1)).astype(BF16)
    base = carry_scr[...][:, 0:1]
    excl = _dot(sel.astype(BF16), upper) + base
    rnk_ref[...] = jnp.concatenate(
        [jnp.sum(jnp.where(ei == ix, excl, 0.0), axis=0, keepdims=True) for ix in idx_rows],
        axis=0).astype(I32)
    carry_scr[...] = carry_scr[...] + jnp.sum(sel, axis=1, keepdims=True)
    cnt_ref[...] = carry_scr[...]


def _route(logits_t, bias, tm):
    E, T = logits_t.shape
    K = TOP_K
    tok = lambda: pl.BlockSpec((K, tm), lambda i: (0, i))
    return pl.pallas_call(
        functools.partial(_route_kernel, tm=tm),
        grid=(T // tm,),
        in_specs=[pl.BlockSpec((E, tm), lambda i: (0, i)),
                  pl.BlockSpec((E, 1), lambda i: (0, 0))],
        out_specs=[tok(), tok(), tok(), pl.BlockSpec((E, LANES), lambda i: (0, 0))],
        out_shape=[jax.ShapeDtypeStruct((K, T), I32), jax.ShapeDtypeStruct((K, T), F32),
                   jax.ShapeDtypeStruct((K, T), I32), jax.ShapeDtypeStruct((E, LANES), F32)],
        scratch_shapes=[pltpu.VMEM((E, LANES), F32)],
        compiler_params=_cparams(("arbitrary",)),
        name="route",
    )(logits_t, bias)


def _dispatch_kernel(eidx_ref, rnk_ref, pst_ref, h_ref, xs_in, xs_out, sem, *, tm):
    del xs_in

    def row_copy(t, dest):
        return pltpu.make_async_copy(
            h_ref.at[pl.ds(pl.multiple_of(t * PACK_ROWS, PACK_ROWS), PACK_ROWS), :],
            xs_out.at[pl.ds(pl.multiple_of(dest * PACK_ROWS, PACK_ROWS), PACK_ROWS), :],
            sem)

    def issue(t, carry):
        for k in range(TOP_K):
            row_copy(t, pst_ref[eidx_ref[k, t]] + rnk_ref[k, t]).start()
        return carry

    lax.fori_loop(0, tm, issue, 0)

    def drain(t, carry):
        for k in range(TOP_K):
            row_copy(0, 0).wait()
        return carry

    lax.fori_loop(0, tm, drain, 0)


def _dispatch(eidx, rnk, pstart, h2p, xs_zero, tm):
    K, T = eidx.shape
    smem_tok = lambda: pl.BlockSpec((K, tm), lambda i: (0, i), memory_space=pltpu.SMEM)
    return pl.pallas_call(
        functools.partial(_dispatch_kernel, tm=tm),
        grid=(T // tm,),
        in_specs=[smem_tok(), smem_tok(),
                  pl.BlockSpec(memory_space=pltpu.SMEM),
                  pl.BlockSpec((tm * PACK_ROWS, PACK_W), lambda i: (i, 0)),
                  pl.BlockSpec(memory_space=pl.ANY)],
        out_specs=pl.BlockSpec(memory_space=pl.ANY),
        out_shape=jax.ShapeDtypeStruct(xs_zero.shape, xs_zero.dtype),
        scratch_shapes=[pltpu.SemaphoreType.DMA(())],
        input_output_aliases={4: 0},
        compiler_params=_cparams(("arbitrary",)),
        name="dispatch",
    )(eidx, rnk, pstart, h2p, xs_zero)


def _ffn_kernel(te_ref, xs_ref, wg_ref, wu_ref, wd_ref, ys_ref, *, tm):
    del te_ref
    parts = []
    for c in range(PACK_ROWS):
        lo, hi = _unpack_rows(xs_ref, c, tm)
        parts += [lo.astype(BF16), hi.astype(BF16)]
    a = jnp.concatenate(parts, axis=1)
    hg = _dot(a, wg_ref[0])
    hu = _dot(a, wu_ref[0])
    hid = (hg * _sigmoid(hg) * hu).astype(BF16)
    _pack_rows(_dot(hid, wd_ref[0]), ys_ref, tm)


def _ffn(tile_e, xs, wg, wu, wd, tm):
    P8 = xs.shape[0]
    nt = P8 // (tm * PACK_ROWS)
    D, I = wg.shape[1], wg.shape[2]
    grid_spec = pltpu.PrefetchScalarGridSpec(
        num_scalar_prefetch=1,
        grid=(nt,),
        in_specs=[
            pl.BlockSpec((tm * PACK_ROWS, PACK_W), lambda i, te: (i, 0)),
            pl.BlockSpec((1, D, I), lambda i, te: (te[i], 0, 0)),
            pl.BlockSpec((1, D, I), lambda i, te: (te[i], 0, 0)),
            pl.BlockSpec((1, I, D), lambda i, te: (te[i], 0, 0)),
        ],
        out_specs=pl.BlockSpec((tm * PACK_ROWS, PACK_W), lambda i, te: (i, 0)),
    )
    return pl.pallas_call(
        functools.partial(_ffn_kernel, tm=tm),
        grid_spec=grid_spec,
        out_shape=jax.ShapeDtypeStruct(xs.shape, U32),
        compiler_params=_cparams(("parallel",)),
        name="expert_ffn",
    )(tile_e, xs, wg, wu, wd)


def _combine_kernel(eidx_ref, rnk_ref, pst_ref, ys_hbm, wt_ref, x1_ref, h2_ref,
                    wsg_ref, wsu_ref, wsd_ref, o_ref, buf, sem, *, tm):
    def issue(t, carry):
        for k in range(TOP_K):
            dest = pst_ref[eidx_ref[k, t]] + rnk_ref[k, t]
            pltpu.make_async_copy(
                ys_hbm.at[pl.ds(pl.multiple_of(dest * PACK_ROWS, PACK_ROWS), PACK_ROWS), :],
                buf.at[k, pl.ds(pl.multiple_of(t * PACK_ROWS, PACK_ROWS), PACK_ROWS), :],
                sem).start()
        return carry

    lax.fori_loop(0, tm, issue, 0)

    h2 = h2_ref[...]
    hg = _dot(h2, wsg_ref[...])
    hu = _dot(h2, wsu_ref[...])
    o_ref[...] = x1_ref[...] + _dot((hg * _sigmoid(hg) * hu).astype(BF16), wsd_ref[...])

    def drain(t, carry):
        for k in range(TOP_K):
            pltpu.make_async_copy(ys_hbm.at[pl.ds(0, PACK_ROWS), :],
                                  buf.at[0, pl.ds(0, PACK_ROWS), :], sem).wait()
        return carry

    lax.fori_loop(0, tm, drain, 0)

    wt = wt_ref[...]
    for c in range(PACK_ROWS):
        acc_lo = jnp.zeros((tm, LANES), F32)
        acc_hi = jnp.zeros((tm, LANES), F32)
        for k in range(TOP_K):
            lo, hi = _unpack_rows(buf, c, tm, lead=k)
            wk = wt[:, k:k + 1]
            acc_lo = acc_lo + wk * lo
            acc_hi = acc_hi + wk * hi
        s0 = c * 2 * LANES
        o_ref[:, s0:s0 + LANES] = o_ref[:, s0:s0 + LANES] + acc_lo
        o_ref[:, s0 + LANES:s0 + 2 * LANES] = o_ref[:, s0 + LANES:s0 + 2 * LANES] + acc_hi


def _combine(eidx, rnk, pstart, ys, wts_t, x1, h2, wsg, wsu, wsd, tm):
    K, T = eidx.shape
    D = x1.shape[1]
    I = wsg.shape[1]
    smem_tok = lambda: pl.BlockSpec((K, tm), lambda i: (0, i), memory_space=pltpu.SMEM)
    return pl.pallas_call(
        functools.partial(_combine_kernel, tm=tm),
        grid=(T // tm,),
        in_specs=[smem_tok(), smem_tok(),
                  pl.BlockSpec(memory_space=pltpu.SMEM),
                  pl.BlockSpec(memory_space=pl.ANY),
                  pl.BlockSpec((tm, K), lambda i: (i, 0)),
                  pl.BlockSpec((tm, D), lambda i: (i, 0)),
                  pl.BlockSpec((tm, D), lambda i: (i, 0)),
                  pl.BlockSpec((D, I), lambda i: (0, 0)),
                  pl.BlockSpec((D, I), lambda i: (0, 0)),
                  pl.BlockSpec((I, D), lambda i: (0, 0))],
        out_specs=pl.BlockSpec((tm, D), lambda i: (i, 0)),
        out_shape=jax.ShapeDtypeStruct((T, D), F32),
        scratch_shapes=[pltpu.VMEM((K, tm * PACK_ROWS, PACK_W), U32),
                        pltpu.SemaphoreType.DMA(())],
        compiler_params=_cparams(("arbitrary",)),
        name="combine",
    )(eidx, rnk, pstart, ys, wts_t, x1, h2, wsg, wsu, wsd)


def _ple_kernel(xf_ref, xt_ref, p_ref, wp_ref, gpost_ref, gin_ref, wg_ref, o_ref,
                hn_scr, ple_scr, *, tn):
    j = pl.program_id(1)

    @pl.when(j == 0)
    def _():
        hn_scr[...] = _rms(xf_ref[...], gin_ref[...]).astype(BF16)
        ple_scr[...] = _rms(_dot(p_ref[...].astype(BF16), wp_ref[...]), gpost_ref[...])

    cols = pl.ds(pl.multiple_of(j * tn, tn), tn)
    gate = _sigmoid(_dot(hn_scr[...], wg_ref[...]))
    o_ref[...] = xt_ref[...] + gate * ple_scr[:, cols]


def _ple(x2, p2, wp, gpost, gin, wg, tm, tn):
    T, D = x2.shape
    Pd = p2.shape[1]
    return pl.pallas_call(
        functools.partial(_ple_kernel, tn=tn),
        grid=(T // tm, D // tn),
        in_specs=[
            pl.BlockSpec((tm, D), lambda i, j: (i, 0)),
            pl.BlockSpec((tm, tn), lambda i, j: (i, j)),
            pl.BlockSpec((tm, Pd), lambda i, j: (i, 0)),
            pl.BlockSpec((Pd, D), lambda i, j: (0, 0)),
            pl.BlockSpec((1, D), lambda i, j: (0, 0)),
            pl.BlockSpec((1, D), lambda i, j: (0, 0)),
            pl.BlockSpec((D, tn), lambda i, j: (0, j)),
        ],
        out_specs=pl.BlockSpec((tm, tn), lambda i, j: (i, j)),
        out_shape=jax.ShapeDtypeStruct((T, D), F32),
        scratch_shapes=[pltpu.VMEM((tm, D), BF16), pltpu.VMEM((tm, D), F32)],
        compiler_params=_cparams(("parallel", "arbitrary")),
        name="ple",
    )(x2, x2, p2, wp, gpost, gin, wg)


def _pad_cols(w, n):
    return jnp.pad(w, ((0, 0), (0, n - w.shape[1])))


def _pad_rows(w, n):
    return jnp.pad(w, ((0, n - w.shape[0]), (0, 0)))


def _layout_w_in(w):
    W = RWKV_WIDTH
    c = 3 * W
    segs = [w[:, 0:c],
            _pad_cols(w[:, c:c + DECAY_LORA], LORA_PAD),
            _pad_cols(w[:, c + DECAY_LORA:c + DECAY_LORA + AAA_LORA], LORA_PAD),
            _pad_cols(w[:, c + DECAY_LORA + AAA_LORA:c + DECAY_LORA + AAA_LORA + GATE_LORA], GATE_LORA_PAD)]
    c += DECAY_LORA + AAA_LORA + GATE_LORA
    mla_cols = Q_LORA + KV_LORA + QK_ROPE
    segs.append(_pad_cols(w[:, c:c + mla_cols], MLA_PAD))
    c += mla_cols
    segs.append(w[:, c:])
    out = jnp.concatenate(segs, axis=1).astype(BF16)
    assert out.shape[1] == IN_PAD
    return out


def _layout_mu(mu):
    W = RWKV_WIDTH
    c = 3 * W
    segs = [mu[0:c],
            jnp.pad(mu[c:c + DECAY_LORA], (0, LORA_PAD - DECAY_LORA)),
            jnp.pad(mu[c + DECAY_LORA:c + DECAY_LORA + AAA_LORA], (0, LORA_PAD - AAA_LORA)),
            jnp.pad(mu[c + DECAY_LORA + AAA_LORA:], (0, GATE_LORA_PAD - GATE_LORA))]
    return jnp.concatenate(segs).reshape(1, RWKV_PAD)


def _layer(x, p, positions, g_mix, w_in, mu_rwkv, w0, w2, a0, a2, g2, k_k, k_a, r_k,
           gn_w, gn_b, w_a_up, g_qa, g_kva, w_uq, w_ukv, g_qn, g_kn, w_b_up, w_o,
           g_ffn, w_router, router_bias, w_exp_gate, w_exp_up, w_exp_down,
           w_sh_gate, w_sh_up, w_sh_down, w_ple, g_ple_post, g_ple_in, w_ple_gate,
           *, tiles):
    B, S, D = x.shape
    T = B * S
    W = RWKV_WIDTH
    x2 = x.reshape(T, D)
    row = lambda v: v.reshape(1, -1).astype(F32)

    proj = _inproj(x2, row(g_mix), _layout_w_in(w_in), tiles["tm_in"], tiles["tn_in"])

    head_blk = (jnp.arange(W)[:, None] // RWKV_HEAD == jnp.arange(W)[None, :] // RWKV_HEAD).astype(BF16)
    r, k, v, kk, a, lw, g = _rwkv_prep(
        proj, _layout_mu(mu_rwkv), row(w0), _pad_rows(w2, LORA_PAD).astype(BF16), row(a0),
        _pad_rows(a2, LORA_PAD).astype(BF16), _pad_rows(g2, GATE_LORA_PAD).astype(BF16),
        row(k_k), row(k_a), head_blk, tiles["tm_prep"], S)
    b3 = lambda t: t.reshape(B, S, W)
    o_a = _rwkv_scan(b3(r), b3(k), b3(v), b3(kk), b3(a), b3(lw), b3(g),
                     row(gn_w), row(gn_b), row(r_k), tiles["rows_scan"]).reshape(T, W)

    half = QK_ROPE // 2
    inv_freq = ROPE_THETA ** (-jnp.arange(half, dtype=F32) / half)
    invf = jnp.concatenate([inv_freq, inv_freq, jnp.zeros((LANES - QK_ROPE,), F32)]).reshape(1, LANES)
    wuq = jnp.pad(w_uq.reshape(Q_LORA, MLA_HEADS, QK_HEAD),
                  ((0, 0), (0, 0), (0, QK_PAD - QK_HEAD))).reshape(Q_LORA, MLA_HEADS * QK_PAD)
    pad_g = lambda gv: jnp.pad(gv, (0, QK_PAD - QK_HEAD)).reshape(1, QK_PAD)
    q, kx, vx = _mla_prep(proj, positions.reshape(T, 1).astype(I32), invf, row(g_qa), row(g_kva),
                          wuq.astype(BF16), w_ukv.astype(BF16), pad_g(g_qn), pad_g(g_kn),
                          tiles["tm_mla"])
    o_b = _flash(q.reshape(B, S, -1), kx.reshape(B, S, -1), vx, tiles["tq"]).reshape(T, MLA_WIDTH)

    merged = _merge(o_a, o_b, w_a_up.astype(BF16), w_b_up.astype(BF16), proj,
                    tiles["tm_merge"], tiles["tn_merge"])

    x1, h2, h2p, logits_t = _wo(merged, x2, w_o.astype(BF16), row(g_ffn),
                                jnp.transpose(w_router).astype(F32), tiles["tm_wo"])
    eidx, wts, rnk, cnt = _route(logits_t, router_bias.reshape(N_EXPERTS, 1).astype(F32),
                                 tiles["tm_route"])
    tmf = tiles["tm_ffn"]
    counts = cnt[:, 0].astype(I32)
    pcounts = (counts + tmf - 1) // tmf * tmf
    pends = jnp.cumsum(pcounts)
    pstart = (pends - pcounts).astype(I32)
    n_tiles = (T * TOP_K) // tmf + N_EXPERTS
    tile_start = jnp.arange(n_tiles, dtype=I32) * tmf
    tile_e = jnp.minimum(jnp.sum((pends[None, :] <= tile_start[:, None]).astype(I32), axis=1),
                         N_EXPERTS - 1)
    xs = _dispatch(eidx, rnk, pstart, h2p,
                   jnp.zeros((n_tiles * tmf * PACK_ROWS, PACK_W), U32), tiles["tm_disp"])
    ys = _ffn(tile_e, xs, w_exp_gate.astype(BF16), w_exp_up.astype(BF16),
              w_exp_down.astype(BF16), tmf)
    x3 = _combine(eidx, rnk, pstart, ys, jnp.transpose(wts), x1, h2,
                  w_sh_gate.astype(BF16), w_sh_up.astype(BF16), w_sh_down.astype(BF16),
                  tiles["tm_comb"])

    out = _ple(x3, p.reshape(T, PLE_DIM), w_ple.astype(BF16), row(g_ple_post), row(g_ple_in),
               w_ple_gate.astype(BF16), tiles["tm_ple"], tiles["tn_ple"])
    return out.reshape(B, S, D)


TILES = dict(tm_in=512, tn_in=512, tm_prep=256, rows_scan=512, tm_mla=512, tq=512,
             tm_merge=512, tn_merge=512, tm_wo=256, tm_route=512, tm_ffn=256,
             tm_disp=512, tm_comb=256, tm_ple=512, tn_ple=512)


def kernel(x, p, positions, g_mix, w_in, mu_rwkv, w0, w2, a0, a2, g2, k_k, k_a, r_k, gn_w, gn_b, w_a_up, g_qa, g_kva, w_uq, w_ukv, g_qn, g_kn, w_b_up, w_o, g_ffn, w_router, router_bias, w_exp_gate, w_exp_up, w_exp_down, w_sh_gate, w_sh_up, w_sh_down, w_ple, g_ple_post, g_ple_in, w_ple_gate):
    args = (g_mix, w_in, mu_rwkv, w0, w2, a0, a2, g2, k_k, k_a, r_k, gn_w, gn_b, w_a_up,
            g_qa, g_kva, w_uq, w_ukv, g_qn, g_kn, w_b_up, w_o, g_ffn, w_router, router_bias,
            w_exp_gate, w_exp_up, w_exp_down, w_sh_gate, w_sh_up, w_sh_down, w_ple,
            g_ple_post, g_ple_in, w_ple_gate)
    assert all(t.shape[0] == 1 for t in args), "single-layer stack expected"
    return _layer(x, p[0], positions, *[t[0] for t in args], tiles=TILES)
```

```python
import functools

import jax
import jax.numpy as jnp
from jax import lax
from jax.experimental import pallas as pl
from jax.experimental.pallas import tpu as pltpu

F32 = jnp.float32
BF16 = jnp.bfloat16
I32 = jnp.int32
U32 = jnp.uint32

D_MODEL = 2048
PLE_DIM = 256
NORM_EPS = 1e-6
RWKV_HEADS = 16
RWKV_HEAD = 64
RWKV_WIDTH = RWKV_HEADS * RWKV_HEAD
DECAY_LORA = 64
AAA_LORA = 64
GATE_LORA = 160
GN_EPS = 64e-5
MLA_HEADS = 8
Q_LORA = 512
KV_LORA = 512
QK_NOPE = 128
QK_ROPE = 64
QK_HEAD = QK_NOPE + QK_ROPE
V_HEAD = 128
MLA_WIDTH = MLA_HEADS * V_HEAD
ROPE_THETA = 10000.0
N_EXPERTS = 64
TOP_K = 8
N_GROUPS = 8
GROUP_SIZE = N_EXPERTS // N_GROUPS
TOPK_GROUPS = 4
MOE_INTER = 512
ROUTED_SCALE = 2.5

LANES = 128
QK_PAD = 2 * LANES
PACK_ROWS = 8
PACK_W = D_MODEL // 2 // PACK_ROWS
assert PACK_W == LANES

LORA_PAD = 128
GATE_LORA_PAD = 256
RWKV_PAD = 3 * RWKV_WIDTH + 2 * LORA_PAD + GATE_LORA_PAD
MLA_PAD = 1536
COL_RWKV = 0
COL_MLA = RWKV_PAD
COL_GA = COL_MLA + MLA_PAD
COL_GB = COL_GA + D_MODEL
IN_PAD = COL_GB + D_MODEL

VMEM_LIMIT = 56 * 1024 * 1024
CHUNK = 64


def _cparams(sem):
    return pltpu.CompilerParams(dimension_semantics=sem, vmem_limit_bytes=VMEM_LIMIT)


def _dot(a, b):
    return jnp.dot(a, b, preferred_element_type=F32)


def _dot_nt(a, b):
    return lax.dot_general(a, b, (((1,), (1,)), ((), ())), preferred_element_type=F32)


def _dot_tn(a, b):
    return lax.dot_general(a, b, (((0,), (0,)), ((), ())), preferred_element_type=F32)


def _split_bf16(x):
    hi = x.astype(BF16)
    lo = (x - hi.astype(F32)).astype(BF16)
    return hi, lo


def _rms(x, g):
    ms = jnp.mean(x * x, axis=-1, keepdims=True)
    return x * lax.rsqrt(ms + NORM_EPS) * g


def _sigmoid(x):
    return 1.0 / (1.0 + jnp.exp(-x))


def _inproj_kernel(x_ref, g_ref, w_ref, o_ref, h_scr):
    @pl.when(pl.program_id(1) == 0)
    def _():
        h_scr[...] = _rms(x_ref[...], g_ref[...]).astype(BF16)

    o_ref[...] = _dot(h_scr[...], w_ref[...]).astype(o_ref.dtype)


def _inproj(x2, g, w, tm, tn):
    T, D = x2.shape
    N = w.shape[1]
    return pl.pallas_call(
        _inproj_kernel,
        grid=(T // tm, N // tn),
        in_specs=[
            pl.BlockSpec((tm, D), lambda i, j: (i, 0)),
            pl.BlockSpec((1, D), lambda i, j: (0, 0)),
            pl.BlockSpec((D, tn), lambda i, j: (0, j)),
        ],
        out_specs=pl.BlockSpec((tm, tn), lambda i, j: (i, j)),
        out_shape=jax.ShapeDtypeStruct((T, N), BF16),
        scratch_shapes=[pltpu.VMEM((tm, D), BF16)],
        compiler_params=_cparams(("parallel", "arbitrary")),
        name="inproj",
    )(x2, g, w)


HALO = 16


def _rwkv_prep_kernel(cur_ref, halo_ref, mu_ref, w0_ref, w2_ref, a0_ref, a2_ref, g2_ref,
                      kk_ref, ka_ref, bd_ref,
                      r_out, k_out, v_out, kk_out, a_out, lw_out, g_out, *, tm, seq):
    i = pl.program_id(0)
    cur = cur_ref[...].astype(F32)
    first = (i * tm) % seq == 0
    last_prev = halo_ref[HALO - 1:HALO, :].astype(F32)
    last_prev = jnp.where(first, 0.0, last_prev)
    row = lax.broadcasted_iota(I32, cur.shape, 0)
    prev = jnp.where(row == 0, last_prev, pltpu.roll(cur, 1, 0))
    u = cur + (prev - cur) * mu_ref[...]
    W = RWKV_WIDTH
    r = u[:, 0:W]
    k = u[:, W:2 * W]
    v = u[:, 2 * W:3 * W]
    dw = u[:, 3 * W:3 * W + LORA_PAD]
    da = u[:, 3 * W + LORA_PAD:3 * W + 2 * LORA_PAD]
    dg = u[:, 3 * W + 2 * LORA_PAD:3 * W + 2 * LORA_PAD + GATE_LORA_PAD]
    z = w0_ref[...] + _dot(jnp.tanh(dw).astype(BF16), w2_ref[...])
    w_log = -(jnp.maximum(-z, 0.0) + jnp.log(1.0 + jnp.exp(-jnp.abs(z)))) - 0.5
    lw_out[...] = -jnp.exp(w_log)
    a = _sigmoid(a0_ref[...] + _dot(da.astype(BF16), a2_ref[...]))
    g_out[...] = _dot(_sigmoid(dg).astype(BF16), g2_ref[...]).astype(g_out.dtype)
    kk = k * kk_ref[...]
    hi, lo = _split_bf16(kk * kk)
    ssq = _dot(hi, bd_ref[...]) + _dot(lo, bd_ref[...])
    kk = kk / jnp.maximum(jnp.sqrt(ssq), 1e-12)
    r_out[...] = r.astype(r_out.dtype)
    k_out[...] = (k * (1.0 + (a - 1.0) * ka_ref[...])).astype(k_out.dtype)
    v_out[...] = v.astype(v_out.dtype)
    kk_out[...] = kk.astype(kk_out.dtype)
    a_out[...] = a.astype(a_out.dtype)


def _rwkv_prep(proj, mu, w0, w2, a0, a2, g2, k_k, k_a, bd, tm, seq):
    T = proj.shape[0]
    W = RWKV_WIDTH
    full = lambda shape: pl.BlockSpec(shape, lambda i: (0, 0))
    out = lambda: pl.BlockSpec((tm, W), lambda i: (i, 0))
    return pl.pallas_call(
        functools.partial(_rwkv_prep_kernel, tm=tm, seq=seq),
        grid=(T // tm,),
        in_specs=[
            pl.BlockSpec((tm, RWKV_PAD), lambda i: (i, 0)),
            pl.BlockSpec((HALO, RWKV_PAD), lambda i: (jnp.maximum(i * (tm // HALO) - 1, 0), 0)),
            full((1, RWKV_PAD)), full((1, W)), full((LORA_PAD, W)), full((1, W)),
            full((LORA_PAD, W)), full((GATE_LORA_PAD, W)), full((1, W)), full((1, W)),
            full((W, W)),
        ],
        out_specs=[out() for _ in range(7)],
        out_shape=[jax.ShapeDtypeStruct((T, W), dt)
                   for dt in (BF16, BF16, BF16, BF16, BF16, F32, BF16)],
        compiler_params=_cparams(("parallel",)),
        name="rwkv_prep",
    )(proj, proj, mu, w0, w2, a0, a2, g2, k_k, k_a, bd)


def _rwkv_scan_kernel(r_ref, k_ref, v_ref, kk_ref, a_ref, lw_ref, g_ref,
                      gnw_ref, gnb_ref, rk_ref, o_ref, s_scr, *, rows):
    L = CHUNK
    L2 = 2 * L

    @pl.when(pl.program_id(2) == 0)
    def _():
        s_scr[...] = jnp.zeros_like(s_scr)

    lane = lax.broadcasted_iota(I32, (L, LANES), 1)
    m_a = (lane < RWKV_HEAD).astype(F32)
    m_b = 1.0 - m_a
    ri = lax.broadcasted_iota(I32, (L2, L2), 0)
    ci = lax.broadcasted_iota(I32, (L2, L2), 1)
    strict = ri > ci
    incl = ri >= ci
    eye = (ri == ci).astype(F32)
    tri = (lax.broadcasted_iota(I32, (L, L), 0) >= lax.broadcasted_iota(I32, (L, L), 1)).astype(BF16)
    hr = lax.broadcasted_iota(I32, (LANES, LANES), 0) // RWKV_HEAD
    hc = lax.broadcasted_iota(I32, (LANES, LANES), 1) // RWKV_HEAD
    head_ones = (hr == hc).astype(BF16)
    gnw = gnw_ref[...]
    gnb = gnb_ref[...]
    rk = rk_ref[...]

    def stack(x):
        return jnp.concatenate([x * m_a, x * m_b], axis=0)

    chunks = range(rows // L)
    rows_of = lambda c: slice(c * L, (c + 1) * L)
    each = lambda f, *lists: [f(*vals) for vals in zip(*lists)]

    def operands(c):
        sl = rows_of(c)
        r = r_ref[0, sl, :].astype(F32)
        k = k_ref[0, sl, :].astype(F32)
        v = v_ref[0, sl, :].astype(F32)
        kk = kk_ref[0, sl, :].astype(F32)
        a = a_ref[0, sl, :].astype(F32)
        lw = lw_ref[0, sl, :]
        hi, lo = _split_bf16(lw)
        cum = _dot(tri, hi) + _dot(tri, lo)
        c_end = cum[L - 1:L, :]
        p_inv = jnp.exp(-cum)
        p_end = jnp.exp(c_end - cum)
        b = kk * a
        return dict(
            x_a=stack(-(kk * jnp.exp(cum - lw))).astype(BF16),
            x_r=stack(r * jnp.exp(cum)),
            x_bk=jnp.concatenate([stack(b * p_inv), stack(k * p_inv)], axis=0).astype(BF16),
            v_st=stack(v).astype(BF16),
            z_hat=jnp.concatenate([stack(b * p_end), stack(k * p_end)], axis=0).astype(BF16),
            p_row=jnp.exp(c_end),
            rkr=(r * k * rk).astype(BF16), v=v)

    ops = [operands(c) for c in chunks]
    big = [_dot_nt(jnp.concatenate([o["x_a"], o["x_r"].astype(BF16)], axis=0), o["x_bk"])
           for o in ops]
    n_pow = [jnp.where(strict, m[0:L2, 0:L2], 0.0) for m in big]
    a_ak = [jnp.where(strict, m[0:L2, L2:2 * L2], 0.0).astype(BF16) for m in big]
    a_r = [jnp.concatenate([jnp.where(incl, m[L2:2 * L2, 0:L2], 0.0),
                            jnp.where(incl, m[L2:2 * L2, L2:2 * L2], 0.0)], axis=1).astype(BF16)
           for m in big]

    t_inv = [eye + n for n in n_pow]
    for _ in range(5):
        n_pow = each(lambda n: _dot(n.astype(BF16), n.astype(BF16)), n_pow)
        t_inv = each(lambda t, n: t + _dot(t.astype(BF16), n.astype(BF16)), t_inv, n_pow)

    akv = each(lambda m, o: _dot(m, o["v_st"]).astype(BF16), a_ak, ops)
    w = each(lambda t, o, x: _dot(t.astype(BF16), jnp.concatenate([o["x_a"], x], axis=1)),
             t_inv, ops, akv)
    w1 = [m[:, 0:LANES].astype(BF16) for m in w]
    w2v = each(lambda m, o: jnp.concatenate([m[:, LANES:].astype(BF16), o["v_st"]], axis=0), w, ops)
    g_mat = each(lambda o, m, x: (o["x_r"] + _dot(m[:, 0:L2], x)).astype(BF16), ops, a_r, w1)
    y0 = each(_dot, a_r, w2v)
    m_mat = each(lambda x, o: _dot_tn(x, o["z_hat"][0:L2, :]).astype(BF16), w1, ops)
    c2 = each(lambda x, o: _dot_tn(x, o["z_hat"]), w2v, ops)
    bonus_v = [_dot(o["rkr"], head_ones) * o["v"] for o in ops]

    s = s_scr[...]
    y = []
    for c in chunks:
        s_b = s.astype(BF16)
        y_st = _dot_nt(g_mat[c], s_b) + y0[c]
        s = s * ops[c]["p_row"] + _dot(s_b, m_mat[c]) + c2[c]
        y.append(y_st[0:L, :] + y_st[L:L2, :])
    s_scr[...] = s

    mean = [_dot(t.astype(BF16), head_ones) * (1.0 / RWKV_HEAD) for t in y]
    yc = each(lambda t, m: t - m, y, mean)
    var = [_dot((t * t).astype(BF16), head_ones) * (1.0 / RWKV_HEAD) for t in yc]
    for c in chunks:
        yn = yc[c] * lax.rsqrt(var[c] + GN_EPS) * gnw + gnb
        sl = rows_of(c)
        o_ref[0, sl, :] = ((yn + bonus_v[c]) * g_ref[0, sl, :].astype(F32)).astype(o_ref.dtype)


def _rwkv_scan(r, k, v, kk, a, lw, g, gn_w, gn_b, r_k, rows):
    B, S, W = r.shape
    npair = W // LANES
    seq = lambda: pl.BlockSpec((1, rows, LANES), lambda b, p, c: (b, c, p))
    par = lambda: pl.BlockSpec((1, LANES), lambda b, p, c: (0, p))
    return pl.pallas_call(
        functools.partial(_rwkv_scan_kernel, rows=rows),
        grid=(B, npair, S // rows),
        in_specs=[seq() for _ in range(7)] + [par(), par(), par()],
        out_specs=seq(),
        out_shape=jax.ShapeDtypeStruct((B, S, W), BF16),
        scratch_shapes=[pltpu.VMEM((LANES, LANES), F32)],
        compiler_params=_cparams(("parallel", "parallel", "arbitrary")),
        name="rwkv_scan",
    )(r, k, v, kk, a, lw, g, gn_w, gn_b, r_k)


def _rope128(x, cos, sin_signed):
    lane = lax.broadcasted_iota(I32, x.shape, 1)
    half = QK_ROPE // 2
    partner = jnp.where(lane < half, pltpu.roll(x, LANES - half, 1), pltpu.roll(x, half, 1))
    return x * cos + partner * sin_signed


def _mla_prep_kernel(cq_ref, ckv_ref, kr_ref, pos_ref, invf_ref, gqa_ref, gkva_ref,
                     wuq_ref, wukv_ref, gqn_ref, gkn_ref,
                     q_out, k_out, v_out, cqn_scr, ckvn_scr, cos_scr, sin_scr):
    @pl.when(pl.program_id(1) == 0)
    def _():
        cqn_scr[...] = _rms(cq_ref[...].astype(F32), gqa_ref[...]).astype(BF16)
        ckvn_scr[...] = _rms(ckv_ref[...].astype(F32), gkva_ref[...]).astype(BF16)
        ang = pos_ref[...].astype(F32) * invf_ref[...]
        lane = lax.broadcasted_iota(I32, ang.shape, 1)
        cos_scr[...] = jnp.cos(ang)
        sin_scr[...] = jnp.where(lane < QK_ROPE // 2, -1.0, 1.0) * jnp.sin(ang)

    cos = cos_scr[...]
    sin = sin_scr[...]
    inv_n = 1.0 / QK_HEAD
    q = _dot(cqn_scr[...], wuq_ref[...])
    rs = lax.rsqrt(jnp.sum(q * q, axis=-1, keepdims=True) * inv_n + NORM_EPS)
    q = q * rs * gqn_ref[...] * (QK_HEAD ** -0.5 * LOG2E)
    q_out[...] = jnp.concatenate(
        [q[:, 0:LANES], _rope128(q[:, LANES:], cos, sin)], axis=1).astype(q_out.dtype)

    kv = _dot(ckvn_scr[...], wukv_ref[...])
    k_nope = kv[:, 0:QK_NOPE]
    kr = kr_ref[...].astype(F32)
    ssq = (jnp.sum(k_nope * k_nope, axis=-1, keepdims=True)
           + jnp.sum(kr * kr, axis=-1, keepdims=True))
    rs = lax.rsqrt(ssq * inv_n + NORM_EPS)
    gkn = gkn_ref[...]
    k_out[...] = jnp.concatenate(
        [k_nope * rs * gkn[:, 0:LANES], _rope128(kr * rs * gkn[:, LANES:], cos, sin)],
        axis=1).astype(k_out.dtype)
    v_out[...] = jnp.transpose(kv[:, QK_NOPE:]).astype(v_out.dtype)


def _mla_prep(proj, pos, invf, g_qa, g_kva, wuq, wukv, gqn, gkn, tm):
    T = proj.shape[0]
    H = MLA_HEADS
    c0 = COL_MLA // Q_LORA
    ckr = (COL_MLA + Q_LORA + KV_LORA) // LANES
    full = lambda shape: pl.BlockSpec(shape, lambda i, h: (0, 0))
    return pl.pallas_call(
        _mla_prep_kernel,
        grid=(T // tm, H),
        in_specs=[
            pl.BlockSpec((tm, Q_LORA), lambda i, h: (i, c0)),
            pl.BlockSpec((tm, KV_LORA), lambda i, h: (i, c0 + 1)),
            pl.BlockSpec((tm, LANES), lambda i, h: (i, ckr)),
            pl.BlockSpec((tm, 1), lambda i, h: (i, 0)),
            full((1, LANES)), full((1, Q_LORA)), full((1, KV_LORA)),
            pl.BlockSpec((Q_LORA, QK_PAD), lambda i, h: (0, h)),
            pl.BlockSpec((KV_LORA, QK_NOPE + V_HEAD), lambda i, h: (0, h)),
            full((1, QK_PAD)), full((1, QK_PAD)),
        ],
        out_specs=[
            pl.BlockSpec((tm, QK_PAD), lambda i, h: (i, h)),
            pl.BlockSpec((tm, QK_PAD), lambda i, h: (i, h)),
            pl.BlockSpec((V_HEAD, tm), lambda i, h: (h, i)),
        ],
        out_shape=[jax.ShapeDtypeStruct((T, H * QK_PAD), BF16),
                   jax.ShapeDtypeStruct((T, H * QK_PAD), BF16),
                   jax.ShapeDtypeStruct((H * V_HEAD, T), BF16)],
        scratch_shapes=[pltpu.VMEM((tm, Q_LORA), BF16), pltpu.VMEM((tm, KV_LORA), BF16),
                        pltpu.VMEM((tm, LANES), F32), pltpu.VMEM((tm, LANES), F32)],
        compiler_params=_cparams(("parallel", "arbitrary")),
        name="mla_prep",
    )(proj, proj, proj, pos, invf, g_qa, g_kva, wuq, wukv, gqn, gkn)


MASK_VALUE = -1e30
LOG2E = 1.4426950408889634


def _flash_kernel(q_ref, k_ref, vt_ref, o_ref, s_scr, p_scr, alpha_scr, m_scr, l_scr, acc_scr,
                  *, tq):
    i = pl.program_id(2)
    q = q_ref[0]
    m_scr[...] = jnp.full_like(m_scr, MASK_VALUE)
    l_scr[...] = jnp.zeros_like(l_scr)
    acc_scr[...] = jnp.zeros_like(acc_scr)

    def scores(j, slot):
        ks = pl.ds(pl.multiple_of(j * tq, tq), tq)
        s_scr[slot] = _dot_nt(k_ref[0, ks, :], q)

    def softmax(slot, masked):
        for c in range(tq // LANES):
            cols = slice(c * LANES, (c + 1) * LANES)
            s = s_scr[slot, :, cols]
            if masked:
                kpos = lax.broadcasted_iota(I32, s.shape, 0)
                qpos = lax.broadcasted_iota(I32, s.shape, 1) + c * LANES
                s = jnp.where(kpos <= qpos, s, MASK_VALUE)
            m_old = m_scr[:, cols]
            m_new = jnp.maximum(m_old, jnp.max(s, axis=0, keepdims=True))
            p = jnp.exp2(s - m_new)
            alpha = jnp.exp2(m_old - m_new)
            l_scr[:, cols] = alpha * l_scr[:, cols] + jnp.sum(p, axis=0, keepdims=True)
            m_scr[:, cols] = m_new
            p_scr[slot, :, cols] = p.astype(BF16)
            alpha_scr[slot, :, cols] = alpha

    def weighted_values(j, slot):
        vt = vt_ref[:, pl.ds(pl.multiple_of(j * tq, tq), tq)]
        acc_scr[...] = alpha_scr[slot] * acc_scr[...] + _dot(vt, p_scr[slot])

    scores(0, 0)
    p_scr[1] = jnp.zeros((tq, tq), BF16)
    alpha_scr[1] = jnp.ones((1, tq), F32)

    def step(j, slot):
        scores(j + 1, 1 - slot)
        weighted_values(jnp.maximum(j - 1, 0), 1 - slot)
        softmax(slot, False)

    def pair(jp, carry):
        step(2 * jp, 0)
        step(2 * jp + 1, 1)
        return carry

    def finish(slot):
        weighted_values(jnp.maximum(i - 1, 0), 1 - slot)
        softmax(slot, True)
        weighted_values(i, slot)
        o_ref[0] = jnp.transpose(acc_scr[...] / l_scr[...]).astype(o_ref.dtype)

    lax.fori_loop(0, i // 2, pair, 0)

    @pl.when(i % 2 == 0)
    def _():
        finish(0)

    @pl.when(i % 2 == 1)
    def _():
        step(i - 1, 0)
        finish(1)


def _flash(q, k, vt, tq):
    B, S, _ = q.shape
    H = MLA_HEADS
    return pl.pallas_call(
        functools.partial(_flash_kernel, tq=tq),
        grid=(B, H, S // tq),
        in_specs=[
            pl.BlockSpec((1, tq, QK_PAD), lambda b, h, i: (b, i, h)),
            pl.BlockSpec((1, S, QK_PAD), lambda b, h, i: (b, 0, h)),
            pl.BlockSpec((V_HEAD, S), lambda b, h, i: (h, b)),
        ],
        out_specs=pl.BlockSpec((1, tq, V_HEAD), lambda b, h, i: (b, i, h)),
        out_shape=jax.ShapeDtypeStruct((B, S, H * V_HEAD), BF16),
        scratch_shapes=[pltpu.VMEM((2, tq, tq), F32), pltpu.VMEM((2, tq, tq), BF16),
                        pltpu.VMEM((2, 1, tq), F32),
                        pltpu.VMEM((1, tq), F32), pltpu.VMEM((1, tq), F32),
                        pltpu.VMEM((V_HEAD, tq), F32)],
        compiler_params=_cparams(("parallel", "parallel", "arbitrary")),
        name="flash_attn",
    )(q, k, vt)


def _merge_kernel(oa_ref, ob_ref, wa_ref, wb_ref, ga_ref, gb_ref, o_ref):
    ya = _dot(oa_ref[...], wa_ref[...])
    yb = _dot(ob_ref[...], wb_ref[...])
    ga = _sigmoid(ga_ref[...].astype(F32))
    gb = _sigmoid(gb_ref[...].astype(F32))
    o_ref[...] = (ga * ya + gb * yb).astype(o_ref.dtype)


def _merge(oa, ob, wa, wb, proj, tm, tn):
    T, K = oa.shape
    N = wa.shape[1]
    ca = COL_GA // tn
    cb = COL_GB // tn
    return pl.pallas_call(
        _merge_kernel,
        grid=(T // tm, N // tn),
        in_specs=[
            pl.BlockSpec((tm, K), lambda i, j: (i, 0)),
            pl.BlockSpec((tm, K), lambda i, j: (i, 0)),
            pl.BlockSpec((K, tn), lambda i, j: (0, j)),
            pl.BlockSpec((K, tn), lambda i, j: (0, j)),
            pl.BlockSpec((tm, tn), lambda i, j: (i, ca + j)),
            pl.BlockSpec((tm, tn), lambda i, j: (i, cb + j)),
        ],
        out_specs=pl.BlockSpec((tm, tn), lambda i, j: (i, j)),
        out_shape=jax.ShapeDtypeStruct((T, N), BF16),
        compiler_params=_cparams(("parallel", "parallel")),
        name="merge",
    )(oa, ob, wa, wb, proj, proj)


def _pack_rows(x, out_ref, tm):
    for c in range(PACK_ROWS):
        lo = x[:, c * 2 * LANES:c * 2 * LANES + LANES]
        hi = x[:, c * 2 * LANES + LANES:(c + 1) * 2 * LANES]
        lo_b = pltpu.bitcast(lo.astype(BF16).astype(F32), U32)
        hi_b = pltpu.bitcast(hi.astype(BF16).astype(F32), U32)
        out_ref[pl.ds(c, tm, stride=PACK_ROWS), :] = hi_b | (lo_b >> 16)


def _unpack_rows(ref, c, tm, lead=None):
    idx = (pl.ds(c, tm, stride=PACK_ROWS), slice(None))
    if lead is not None:
        idx = (lead,) + idx
    u = ref[idx]
    lo = pltpu.bitcast(u << 16, F32)
    hi = pltpu.bitcast(u & jnp.uint32(0xFFFF0000), F32)
    return lo, hi


def _wo_kernel(m_ref, x_ref, wo_ref, g_ref, wr_ref, x1_ref, h2_ref, h2p_ref, lg_ref, *, tm):
    x1 = x_ref[...] + _dot(m_ref[...], wo_ref[...])
    x1_ref[...] = x1
    h2 = _rms(x1, g_ref[...])
    h2_ref[...] = h2.astype(BF16)
    _pack_rows(h2, h2p_ref, tm)
    h_hi, h_lo = _split_bf16(h2)
    w_hi, w_lo = _split_bf16(wr_ref[...])
    lg_ref[...] = _dot_nt(w_hi, h_hi) + _dot_nt(w_hi, h_lo) + _dot_nt(w_lo, h_hi)


def _wo(merged, x2, wo, g, wr_t, tm):
    T, D = x2.shape
    E = wr_t.shape[0]
    return pl.pallas_call(
        functools.partial(_wo_kernel, tm=tm),
        grid=(T // tm,),
        in_specs=[
            pl.BlockSpec((tm, D), lambda i: (i, 0)),
            pl.BlockSpec((tm, D), lambda i: (i, 0)),
            pl.BlockSpec((D, D), lambda i: (0, 0)),
            pl.BlockSpec((1, D), lambda i: (0, 0)),
            pl.BlockSpec((E, D), lambda i: (0, 0)),
        ],
        out_specs=[
            pl.BlockSpec((tm, D), lambda i: (i, 0)),
            pl.BlockSpec((tm, D), lambda i: (i, 0)),
            pl.BlockSpec((tm * PACK_ROWS, PACK_W), lambda i: (i, 0)),
            pl.BlockSpec((E, tm), lambda i: (0, i)),
        ],
        out_shape=[jax.ShapeDtypeStruct((T, D), F32),
                   jax.ShapeDtypeStruct((T, D), BF16),
                   jax.ShapeDtypeStruct((T * PACK_ROWS, PACK_W), U32),
                   jax.ShapeDtypeStruct((E, T), F32)],
        compiler_params=_cparams(("parallel",)),
        name="wo_norm_router",
    )(merged, x2, wo, g, wr_t)


def _route_kernel(lg_ref, bias_ref, eidx_ref, wts_ref, rnk_ref, cnt_ref, carry_scr, *, tm):
    E = N_EXPERTS
    NEG = -jnp.inf

    @pl.when(pl.program_id(0) == 0)
    def _():
        carry_scr[...] = jnp.zeros_like(carry_scr)

    scores = _sigmoid(lg_ref[...])
    choice = scores + bias_ref[...]
    c3 = choice.reshape(N_GROUPS, GROUP_SIZE, tm)
    sub = lax.broadcasted_iota(I32, c3.shape, 1)
    m1 = jnp.max(c3, axis=1, keepdims=True)
    i1 = jnp.min(jnp.where(c3 == m1, sub, GROUP_SIZE), axis=1, keepdims=True)
    m2 = jnp.max(jnp.where(sub == i1, NEG, c3), axis=1, keepdims=True)
    gs = (m1 + m2).reshape(N_GROUPS, tm)
    gi = lax.broadcasted_iota(I32, gs.shape, 0)
    gsel = jnp.zeros(gs.shape, F32)
    for _ in range(TOPK_GROUPS):
        mx = jnp.max(gs, axis=0, keepdims=True)
        ix = jnp.min(jnp.where(gs == mx, gi, N_GROUPS), axis=0, keepdims=True)
        hit = gi == ix
        gsel = jnp.where(hit, 1.0, gsel)
        gs = jnp.where(hit, NEG, gs)
    emask = jnp.broadcast_to(gsel.reshape(N_GROUPS, 1, tm), (N_GROUPS, GROUP_SIZE, tm)).reshape(E, tm)
    x = jnp.where(emask > 0.5, choice, NEG)
    ei = lax.broadcasted_iota(I32, x.shape, 0)
    sel = jnp.zeros(x.shape, F32)
    idx_rows, w_rows = [], []
    for _ in range(TOP_K):
        mx = jnp.max(x, axis=0, keepdims=True)
        ix = jnp.min(jnp.where(x == mx, ei, E), axis=0, keepdims=True)
        hit = ei == ix
        w_rows.append(jnp.sum(jnp.where(hit, scores, 0.0), axis=0, keepdims=True))
        idx_rows.append(ix)
        sel = jnp.where(hit, 1.0, sel)
        x = jnp.where(hit, NEG, x)
    w = jnp.concatenate(w_rows, axis=0)
    w = w / (jnp.sum(w, axis=0, keepdims=True) + 1e-20) * ROUTED_SCALE
    eidx_ref[...] = jnp.concatenate(idx_rows, axis=0)
    wts_ref[...] = w

    upper = (lax.broadcasted_iota(I32, (tm, tm), 0) < lax.broadcasted_iota(I32, (tm, tm), 1)).astype(BF16)
    base = carry_scr[...][:, 0:1]
    excl = _dot(sel.astype(BF16), upper) + base
    rnk_ref[...] = jnp.concatenate(
        [jnp.sum(jnp.where(ei == ix, excl, 0.0), axis=0, keepdims=True) for ix in idx_rows],
        axis=0).astype(I32)
    carry_scr[...] = carry_scr[...] + jnp.sum(sel, axis=1, keepdims=True)
    cnt_ref[...] = carry_scr[...]


def _route(logits_t, bias, tm):
    E, T = logits_t.shape
    K = TOP_K
    tok = lambda: pl.BlockSpec((K, tm), lambda i: (0, i))
    return pl.pallas_call(
        functools.partial(_route_kernel, tm=tm),
        grid=(T // tm,),
        in_specs=[pl.BlockSpec((E, tm), lambda i: (0, i)),
                  pl.BlockSpec((E, 1), lambda i: (0, 0))],
        out_specs=[tok(), tok(), tok(), pl.BlockSpec((E, LANES), lambda i: (0, 0))],
        out_shape=[jax.ShapeDtypeStruct((K, T), I32), jax.ShapeDtypeStruct((K, T), F32),
                   jax.ShapeDtypeStruct((K, T), I32), jax.ShapeDtypeStruct((E, LANES), F32)],
        scratch_shapes=[pltpu.VMEM((E, LANES), F32)],
        compiler_params=_cparams(("arbitrary",)),
        name="route",
    )(logits_t, bias)


def _dispatch_kernel(eidx_ref, rnk_ref, pst_ref, h_ref, xs_in, xs_out, sem, *, tm):
    del xs_in

    def row_copy(t, dest):
        return pltpu.make_async_copy(
            h_ref.at[pl.ds(pl.multiple_of(t * PACK_ROWS, PACK_ROWS), PACK_ROWS), :],
            xs_out.at[pl.ds(pl.multiple_of(dest * PACK_ROWS, PACK_ROWS), PACK_ROWS), :],
            sem)

    def issue(t, carry):
        for k in range(TOP_K):
            row_copy(t, pst_ref[eidx_ref[k, t]] + rnk_ref[k, t]).start()
        return carry

    lax.fori_loop(0, tm, issue, 0)

    def drain(t, carry):
        for k in range(TOP_K):
            row_copy(0, 0).wait()
        return carry

    lax.fori_loop(0, tm, drain, 0)


def _dispatch(eidx, rnk, pstart, h2p, xs_zero, tm):
    K, T = eidx.shape
    smem_tok = lambda: pl.BlockSpec((K, tm), lambda i: (0, i), memory_space=pltpu.SMEM)
    return pl.pallas_call(
        functools.partial(_dispatch_kernel, tm=tm),
        grid=(T // tm,),
        in_specs=[smem_tok(), smem_tok(),
                  pl.BlockSpec(memory_space=pltpu.SMEM),
                  pl.BlockSpec((tm * PACK_ROWS, PACK_W), lambda i: (i, 0)),
                  pl.BlockSpec(memory_space=pl.ANY)],
        out_specs=pl.BlockSpec(memory_space=pl.ANY),
        out_shape=jax.ShapeDtypeStruct(xs_zero.shape, xs_zero.dtype),
        scratch_shapes=[pltpu.SemaphoreType.DMA(())],
        input_output_aliases={4: 0},
        compiler_params=_cparams(("arbitrary",)),
        name="dispatch",
    )(eidx, rnk, pstart, h2p, xs_zero)


def _ffn_kernel(te_ref, nu_ref, xs_ref, wg_ref, wu_ref, wd_ref, ys_ref, wg_b, wu_b, wd_b, *, tm):
    i = pl.program_id(0)
    used = i < nu_ref[0]

    @pl.when(used & ((i == 0) | (te_ref[i] != te_ref[jnp.maximum(i - 1, 0)])))
    def _():
        wg_b[...] = wg_ref[0].astype(BF16)
        wu_b[...] = wu_ref[0].astype(BF16)
        wd_b[...] = wd_ref[0].astype(BF16)

    @pl.when(used)
    def _():
        parts = []
        for c in range(PACK_ROWS):
            lo, hi = _unpack_rows(xs_ref, c, tm)
            parts += [lo.astype(BF16), hi.astype(BF16)]
        a = jnp.concatenate(parts, axis=1)
        hg = _dot(a, wg_b[...])
        hu = _dot(a, wu_b[...])
        hid = (hg * _sigmoid(hg) * hu).astype(BF16)
        _pack_rows(_dot(hid, wd_b[...]), ys_ref, tm)

    @pl.when(jnp.logical_not(used))
    def _():
        ys_ref[...] = jnp.zeros_like(ys_ref)


def _ffn(tile_e, n_used, xs, wg, wu, wd, tm):
    P8 = xs.shape[0]
    nt = P8 // (tm * PACK_ROWS)
    D, I = wg.shape[1], wg.shape[2]
    grid_spec = pltpu.PrefetchScalarGridSpec(
        num_scalar_prefetch=2,
        grid=(nt,),
        in_specs=[
            pl.BlockSpec((tm * PACK_ROWS, PACK_W), lambda i, te, nu: (i, 0)),
            pl.BlockSpec((1, D, I), lambda i, te, nu: (te[i], 0, 0)),
            pl.BlockSpec((1, D, I), lambda i, te, nu: (te[i], 0, 0)),
            pl.BlockSpec((1, I, D), lambda i, te, nu: (te[i], 0, 0)),
        ],
        out_specs=pl.BlockSpec((tm * PACK_ROWS, PACK_W), lambda i, te, nu: (i, 0)),
        scratch_shapes=[pltpu.VMEM((D, I), BF16), pltpu.VMEM((D, I), BF16),
                        pltpu.VMEM((I, D), BF16)],
    )
    return pl.pallas_call(
        functools.partial(_ffn_kernel, tm=tm),
        grid_spec=grid_spec,
        out_shape=jax.ShapeDtypeStruct(xs.shape, U32),
        compiler_params=_cparams(("arbitrary",)),
        name="expert_ffn",
    )(tile_e, n_used, xs, wg, wu, wd)


def _combine_kernel(eidx_ref, rnk_ref, pst_ref, ys_hbm, wt_ref, x1_ref, h2_ref,
                    wsg_ref, wsu_ref, wsd_ref, o_ref, buf, sem, *, tm):
    def issue(t, carry):
        for k in range(TOP_K):
            dest = pst_ref[eidx_ref[k, t]] + rnk_ref[k, t]
            pltpu.make_async_copy(
                ys_hbm.at[pl.ds(pl.multiple_of(dest * PACK_ROWS, PACK_ROWS), PACK_ROWS), :],
                buf.at[k, pl.ds(pl.multiple_of(t * PACK_ROWS, PACK_ROWS), PACK_ROWS), :],
                sem).start()
        return carry

    lax.fori_loop(0, tm, issue, 0)

    h2 = h2_ref[...]
    hg = _dot(h2, wsg_ref[...])
    hu = _dot(h2, wsu_ref[...])
    o_ref[...] = x1_ref[...] + _dot((hg * _sigmoid(hg) * hu).astype(BF16), wsd_ref[...])

    def drain(t, carry):
        for k in range(TOP_K):
            pltpu.make_async_copy(ys_hbm.at[pl.ds(0, PACK_ROWS), :],
                                  buf.at[0, pl.ds(0, PACK_ROWS), :], sem).wait()
        return carry

    lax.fori_loop(0, tm, drain, 0)

    wt = wt_ref[...]
    for c in range(PACK_ROWS):
        acc_lo = jnp.zeros((tm, LANES), F32)
        acc_hi = jnp.zeros((tm, LANES), F32)
        for k in range(TOP_K):
            lo, hi = _unpack_rows(buf, c, tm, lead=k)
            wk = wt[:, k:k + 1]
            acc_lo = acc_lo + wk * lo
            acc_hi = acc_hi + wk * hi
        s0 = c * 2 * LANES
        o_ref[:, s0:s0 + LANES] = o_ref[:, s0:s0 + LANES] + acc_lo
        o_ref[:, s0 + LANES:s0 + 2 * LANES] = o_ref[:, s0 + LANES:s0 + 2 * LANES] + acc_hi


def _combine(eidx, rnk, pstart, ys, wts_t, x1, h2, wsg, wsu, wsd, tm):
    K, T = eidx.shape
    D = x1.shape[1]
    I = wsg.shape[1]
    smem_tok = lambda: pl.BlockSpec((K, tm), lambda i: (0, i), memory_space=pltpu.SMEM)
    return pl.pallas_call(
        functools.partial(_combine_kernel, tm=tm),
        grid=(T // tm,),
        in_specs=[smem_tok(), smem_tok(),
                  pl.BlockSpec(memory_space=pltpu.SMEM),
                  pl.BlockSpec(memory_space=pl.ANY),
                  pl.BlockSpec((tm, K), lambda i: (i, 0)),
                  pl.BlockSpec((tm, D), lambda i: (i, 0)),
                  pl.BlockSpec((tm, D), lambda i: (i, 0)),
                  pl.BlockSpec((D, I), lambda i: (0, 0)),
                  pl.BlockSpec((D, I), lambda i: (0, 0)),
                  pl.BlockSpec((I, D), lambda i: (0, 0))],
        out_specs=pl.BlockSpec((tm, D), lambda i: (i, 0)),
        out_shape=jax.ShapeDtypeStruct((T, D), F32),
        scratch_shapes=[pltpu.VMEM((K, tm * PACK_ROWS, PACK_W), U32),
                        pltpu.SemaphoreType.DMA(())],
        compiler_params=_cparams(("arbitrary",)),
        name="combine",
    )(eidx, rnk, pstart, ys, wts_t, x1, h2, wsg, wsu, wsd)


def _ple_kernel(xf_ref, xt_ref, p_ref, wp_ref, gpost_ref, gin_ref, wg_ref, o_ref,
                hn_scr, ple_scr, *, tn):
    j = pl.program_id(1)

    @pl.when(j == 0)
    def _():
        hn_scr[...] = _rms(xf_ref[...], gin_ref[...]).astype(BF16)
        ple_scr[...] = _rms(_dot(p_ref[...].astype(BF16), wp_ref[...]), gpost_ref[...])

    cols = pl.ds(pl.multiple_of(j * tn, tn), tn)
    gate = _sigmoid(_dot(hn_scr[...], wg_ref[...]))
    o_ref[...] = xt_ref[...] + gate * ple_scr[:, cols]


def _ple(x2, p2, wp, gpost, gin, wg, tm, tn):
    T, D = x2.shape
    Pd = p2.shape[1]
    return pl.pallas_call(
        functools.partial(_ple_kernel, tn=tn),
        grid=(T // tm, D // tn),
        in_specs=[
            pl.BlockSpec((tm, D), lambda i, j: (i, 0)),
            pl.BlockSpec((tm, tn), lambda i, j: (i, j)),
            pl.BlockSpec((tm, Pd), lambda i, j: (i, 0)),
            pl.BlockSpec((Pd, D), lambda i, j: (0, 0)),
            pl.BlockSpec((1, D), lambda i, j: (0, 0)),
            pl.BlockSpec((1, D), lambda i, j: (0, 0)),
            pl.BlockSpec((D, tn), lambda i, j: (0, j)),
        ],
        out_specs=pl.BlockSpec((tm, tn), lambda i, j: (i, j)),
        out_shape=jax.ShapeDtypeStruct((T, D), F32),
        scratch_shapes=[pltpu.VMEM((tm, D), BF16), pltpu.VMEM((tm, D), F32)],
        compiler_params=_cparams(("parallel", "arbitrary")),
        name="ple",
    )(x2, x2, p2, wp, gpost, gin, wg)


def _pad_cols(w, n):
    return jnp.pad(w, ((0, 0), (0, n - w.shape[1])))


def _pad_rows(w, n):
    return jnp.pad(w, ((0, n - w.shape[0]), (0, 0)))


def _layout_w_in(w):
    W = RWKV_WIDTH
    c = 3 * W
    segs = [w[:, 0:c],
            _pad_cols(w[:, c:c + DECAY_LORA], LORA_PAD),
            _pad_cols(w[:, c + DECAY_LORA:c + DECAY_LORA + AAA_LORA], LORA_PAD),
            _pad_cols(w[:, c + DECAY_LORA + AAA_LORA:c + DECAY_LORA + AAA_LORA + GATE_LORA], GATE_LORA_PAD)]
    c += DECAY_LORA + AAA_LORA + GATE_LORA
    mla_cols = Q_LORA + KV_LORA + QK_ROPE
    segs.append(_pad_cols(w[:, c:c + mla_cols], MLA_PAD))
    c += mla_cols
    segs.append(w[:, c:])
    out = jnp.concatenate(segs, axis=1).astype(BF16)
    assert out.shape[1] == IN_PAD
    return out


def _layout_mu(mu):
    W = RWKV_WIDTH
    c = 3 * W
    segs = [mu[0:c],
            jnp.pad(mu[c:c + DECAY_LORA], (0, LORA_PAD - DECAY_LORA)),
            jnp.pad(mu[c + DECAY_LORA:c + DECAY_LORA + AAA_LORA], (0, LORA_PAD - AAA_LORA)),
            jnp.pad(mu[c + DECAY_LORA + AAA_LORA:], (0, GATE_LORA_PAD - GATE_LORA))]
    return jnp.concatenate(segs).reshape(1, RWKV_PAD)


def _layer(x, p, positions, g_mix, w_in, mu_rwkv, w0, w2, a0, a2, g2, k_k, k_a, r_k,
           gn_w, gn_b, w_a_up, g_qa, g_kva, w_uq, w_ukv, g_qn, g_kn, w_b_up, w_o,
           g_ffn, w_router, router_bias, w_exp_gate, w_exp_up, w_exp_down,
           w_sh_gate, w_sh_up, w_sh_down, w_ple, g_ple_post, g_ple_in, w_ple_gate,
           *, tiles):
    B, S, D = x.shape
    T = B * S
    W = RWKV_WIDTH
    x2 = x.reshape(T, D)
    row = lambda v: v.reshape(1, -1).astype(F32)

    proj = _inproj(x2, row(g_mix), _layout_w_in(w_in), tiles["tm_in"], tiles["tn_in"])

    head_blk = (jnp.arange(W)[:, None] // RWKV_HEAD == jnp.arange(W)[None, :] // RWKV_HEAD).astype(BF16)
    r, k, v, kk, a, lw, g = _rwkv_prep(
        proj, _layout_mu(mu_rwkv), row(w0), _pad_rows(w2, LORA_PAD).astype(BF16), row(a0),
        _pad_rows(a2, LORA_PAD).astype(BF16), _pad_rows(g2, GATE_LORA_PAD).astype(BF16),
        row(k_k), row(k_a), head_blk, tiles["tm_prep"], S)
    b3 = lambda t: t.reshape(B, S, W)
    o_a = _rwkv_scan(b3(r), b3(k), b3(v), b3(kk), b3(a), b3(lw), b3(g),
                     row(gn_w), row(gn_b), row(r_k), tiles["rows_scan"]).reshape(T, W)

    half = QK_ROPE // 2
    inv_freq = ROPE_THETA ** (-jnp.arange(half, dtype=F32) / half)
    invf = jnp.concatenate([inv_freq, inv_freq, jnp.zeros((LANES - QK_ROPE,), F32)]).reshape(1, LANES)
    wuq = jnp.pad(w_uq.reshape(Q_LORA, MLA_HEADS, QK_HEAD),
                  ((0, 0), (0, 0), (0, QK_PAD - QK_HEAD))).reshape(Q_LORA, MLA_HEADS * QK_PAD)
    pad_g = lambda gv: jnp.pad(gv, (0, QK_PAD - QK_HEAD)).reshape(1, QK_PAD)
    q, kx, vx = _mla_prep(proj, positions.reshape(T, 1).astype(I32), invf, row(g_qa), row(g_kva),
                          wuq.astype(BF16), w_ukv.astype(BF16), pad_g(g_qn), pad_g(g_kn),
                          tiles["tm_mla"])
    o_b = _flash(q.reshape(B, S, -1), kx.reshape(B, S, -1), vx, tiles["tq"]).reshape(T, MLA_WIDTH)

    merged = _merge(o_a, o_b, w_a_up.astype(BF16), w_b_up.astype(BF16), proj,
                    tiles["tm_merge"], tiles["tn_merge"])

    x1, h2, h2p, logits_t = _wo(merged, x2, w_o.astype(BF16), row(g_ffn),
                                jnp.transpose(w_router).astype(F32), tiles["tm_wo"])
    eidx, wts, rnk, cnt = _route(logits_t, router_bias.reshape(N_EXPERTS, 1).astype(F32),
                                 tiles["tm_route"])
    tmf = tiles["tm_ffn"]
    counts = cnt[:, 0].astype(I32)
    pcounts = (counts + tmf - 1) // tmf * tmf
    pends = jnp.cumsum(pcounts)
    pstart = (pends - pcounts).astype(I32)
    n_tiles = (T * TOP_K) // tmf + N_EXPERTS
    tile_start = jnp.arange(n_tiles, dtype=I32) * tmf
    tile_e = jnp.minimum(jnp.sum((pends[None, :] <= tile_start[:, None]).astype(I32), axis=1),
                         N_EXPERTS - 1)
    xs = _dispatch(eidx, rnk, pstart, h2p,
                   jnp.zeros((n_tiles * tmf * PACK_ROWS, PACK_W), U32), tiles["tm_disp"])
    n_used = (pends[N_EXPERTS - 1:] // tmf).astype(I32)
    ys = _ffn(tile_e, n_used, xs, w_exp_gate, w_exp_up, w_exp_down, tmf)
    x3 = _combine(eidx, rnk, pstart, ys, jnp.transpose(wts), x1, h2,
                  w_sh_gate.astype(BF16), w_sh_up.astype(BF16), w_sh_down.astype(BF16),
                  tiles["tm_comb"])

    out = _ple(x3, p.reshape(T, PLE_DIM), w_ple.astype(BF16), row(g_ple_post), row(g_ple_in),
               w_ple_gate.astype(BF16), tiles["tm_ple"], tiles["tn_ple"])
    return out.reshape(B, S, D)


TILES = dict(tm_in=1024, tn_in=512, tm_prep=256, rows_scan=512, tm_mla=512, tq=512,
             tm_merge=512, tn_merge=512, tm_wo=256, tm_route=512, tm_ffn=256,
             tm_disp=512, tm_comb=256, tm_ple=512, tn_ple=512)


def kernel(x, p, positions, g_mix, w_in, mu_rwkv, w0, w2, a0, a2, g2, k_k, k_a, r_k, gn_w, gn_b, w_a_up, g_qa, g_kva, w_uq, w_ukv, g_qn, g_kn, w_b_up, w_o, g_ffn, w_router, router_bias, w_exp_gate, w_exp_up, w_exp_down, w_sh_gate, w_sh_up, w_sh_down, w_ple, g_ple_post, g_ple_in, w_ple_gate):
    args = (g_mix, w_in, mu_rwkv, w0, w2, a0, a2, g2, k_k, k_a, r_k, gn_w, gn_b, w_a_up,
            g_qa, g_kva, w_uq, w_ukv, g_qn, g_kn, w_b_up, w_o, g_ffn, w_router, router_bias,
            w_exp_gate, w_exp_up, w_exp_down, w_sh_gate, w_sh_up, w_sh_down, w_ple,
            g_ple_post, g_ple_in, w_ple_gate)
    assert all(t.shape[0] == 1 for t in args), "single-layer stack expected"
    return _layer(x, p[0], positions, *[t[0] for t in args], tiles=TILES)
```

```python
import functools

import jax
import jax.numpy as jnp
from jax import lax
from jax.experimental import pallas as pl
from jax.experimental.pallas import tpu as pltpu

F32 = jnp.float32
BF16 = jnp.bfloat16
I32 = jnp.int32
U32 = jnp.uint32

D_MODEL = 2048
PLE_DIM = 256
NORM_EPS = 1e-6
RWKV_HEADS = 16
RWKV_HEAD = 64
RWKV_WIDTH = RWKV_HEADS * RWKV_HEAD
DECAY_LORA = 64
AAA_LORA = 64
GATE_LORA = 160
GN_EPS = 64e-5
MLA_HEADS = 8
Q_LORA = 512
KV_LORA = 512
QK_NOPE = 128
QK_ROPE = 64
QK_HEAD = QK_NOPE + QK_ROPE
V_HEAD = 128
MLA_WIDTH = MLA_HEADS * V_HEAD
ROPE_THETA = 10000.0
N_EXPERTS = 64
TOP_K = 8
N_GROUPS = 8
GROUP_SIZE = N_EXPERTS // N_GROUPS
TOPK_GROUPS = 4
MOE_INTER = 512
ROUTED_SCALE = 2.5

LANES = 128
QK_PAD = 2 * LANES
PACK_ROWS = 8
PACK_W = D_MODEL // 2 // PACK_ROWS
assert PACK_W == LANES

LORA_PAD = 128
GATE_LORA_PAD = 256
RWKV_PAD = 3 * RWKV_WIDTH + 2 * LORA_PAD + GATE_LORA_PAD
MLA_PAD = 1536
COL_RWKV = 0
COL_MLA = RWKV_PAD
COL_GA = COL_MLA + MLA_PAD
COL_GB = COL_GA + D_MODEL
IN_PAD = COL_GB + D_MODEL

VMEM_LIMIT = 56 * 1024 * 1024
CHUNK = 64
WO_ROWS = 128


def _cparams(sem):
    return pltpu.CompilerParams(dimension_semantics=sem, vmem_limit_bytes=VMEM_LIMIT)


def _dot(a, b):
    return jnp.dot(a, b, preferred_element_type=F32)


def _dot_nt(a, b):
    return lax.dot_general(a, b, (((1,), (1,)), ((), ())), preferred_element_type=F32)


def _dot_tn(a, b):
    return lax.dot_general(a, b, (((0,), (0,)), ((), ())), preferred_element_type=F32)


def _split_bf16(x):
    hi = x.astype(BF16)
    lo = (x - hi.astype(F32)).astype(BF16)
    return hi, lo


def _rms(x, g):
    ms = jnp.mean(x * x, axis=-1, keepdims=True)
    return x * lax.rsqrt(ms + NORM_EPS) * g


def _sigmoid(x):
    return 1.0 / (1.0 + jnp.exp(-x))


def _inproj_kernel(x_ref, g_ref, w_ref, o_ref, h_scr):
    @pl.when(pl.program_id(1) == 0)
    def _():
        h_scr[...] = _rms(x_ref[...], g_ref[...]).astype(BF16)

    o_ref[...] = _dot(h_scr[...], w_ref[...]).astype(o_ref.dtype)


def _inproj(x2, g, w, tm, tn):
    T, D = x2.shape
    N = w.shape[1]
    return pl.pallas_call(
        _inproj_kernel,
        grid=(T // tm, N // tn),
        in_specs=[
            pl.BlockSpec((tm, D), lambda i, j: (i, 0)),
            pl.BlockSpec((1, D), lambda i, j: (0, 0)),
            pl.BlockSpec((D, tn), lambda i, j: (0, j)),
        ],
        out_specs=pl.BlockSpec((tm, tn), lambda i, j: (i, j)),
        out_shape=jax.ShapeDtypeStruct((T, N), BF16),
        scratch_shapes=[pltpu.VMEM((tm, D), BF16)],
        compiler_params=_cparams(("parallel", "arbitrary")),
        name="inproj",
    )(x2, g, w)


HALO = 16


def _rwkv_prep_kernel(cur_ref, halo_ref, mu_ref, w0_ref, w2_ref, a0_ref, a2_ref, g2_ref,
                      kk_ref, ka_ref, bd_ref,
                      r_out, k_out, v_out, kk_out, a_out, lw_out, g_out, *, tm, seq):
    i = pl.program_id(0)
    cur = cur_ref[...].astype(F32)
    first = (i * tm) % seq == 0
    last_prev = halo_ref[HALO - 1:HALO, :].astype(F32)
    last_prev = jnp.where(first, 0.0, last_prev)
    row = lax.broadcasted_iota(I32, cur.shape, 0)
    prev = jnp.where(row == 0, last_prev, pltpu.roll(cur, 1, 0))
    u = cur + (prev - cur) * mu_ref[...]
    W = RWKV_WIDTH
    r = u[:, 0:W]
    k = u[:, W:2 * W]
    v = u[:, 2 * W:3 * W]
    dw = u[:, 3 * W:3 * W + LORA_PAD]
    da = u[:, 3 * W + LORA_PAD:3 * W + 2 * LORA_PAD]
    dg = u[:, 3 * W + 2 * LORA_PAD:3 * W + 2 * LORA_PAD + GATE_LORA_PAD]
    z = w0_ref[...] + _dot(jnp.tanh(dw).astype(BF16), w2_ref[...])
    w_log = -(jnp.maximum(-z, 0.0) + jnp.log(1.0 + jnp.exp(-jnp.abs(z)))) - 0.5
    lw_out[...] = -jnp.exp(w_log)
    a = _sigmoid(a0_ref[...] + _dot(da.astype(BF16), a2_ref[...]))
    g_out[...] = _dot(_sigmoid(dg).astype(BF16), g2_ref[...]).astype(g_out.dtype)
    kk = k * kk_ref[...]
    hi, lo = _split_bf16(kk * kk)
    ssq = _dot(hi, bd_ref[...]) + _dot(lo, bd_ref[...])
    kk = kk / jnp.maximum(jnp.sqrt(ssq), 1e-12)
    r_out[...] = r.astype(r_out.dtype)
    k_out[...] = (k * (1.0 + (a - 1.0) * ka_ref[...])).astype(k_out.dtype)
    v_out[...] = v.astype(v_out.dtype)
    kk_out[...] = kk.astype(kk_out.dtype)
    a_out[...] = a.astype(a_out.dtype)


def _rwkv_prep(proj, mu, w0, w2, a0, a2, g2, k_k, k_a, bd, tm, seq):
    T = proj.shape[0]
    W = RWKV_WIDTH
    full = lambda shape: pl.BlockSpec(shape, lambda i: (0, 0))
    out = lambda: pl.BlockSpec((tm, W), lambda i: (i, 0))
    return pl.pallas_call(
        functools.partial(_rwkv_prep_kernel, tm=tm, seq=seq),
        grid=(T // tm,),
        in_specs=[
            pl.BlockSpec((tm, RWKV_PAD), lambda i: (i, 0)),
            pl.BlockSpec((HALO, RWKV_PAD), lambda i: (jnp.maximum(i * (tm // HALO) - 1, 0), 0)),
            full((1, RWKV_PAD)), full((1, W)), full((LORA_PAD, W)), full((1, W)),
            full((LORA_PAD, W)), full((GATE_LORA_PAD, W)), full((1, W)), full((1, W)),
            full((W, W)),
        ],
        out_specs=[out() for _ in range(7)],
        out_shape=[jax.ShapeDtypeStruct((T, W), dt)
                   for dt in (BF16, BF16, BF16, BF16, BF16, F32, BF16)],
        compiler_params=_cparams(("parallel",)),
        name="rwkv_prep",
    )(proj, proj, mu, w0, w2, a0, a2, g2, k_k, k_a, bd)


def _rwkv_scan_kernel(r_ref, k_ref, v_ref, kk_ref, a_ref, lw_ref, g_ref,
                      gnw_ref, gnb_ref, rk_ref, o_ref, s_scr, *, rows):
    L = CHUNK
    L2 = 2 * L

    @pl.when(pl.program_id(2) == 0)
    def _():
        s_scr[...] = jnp.zeros_like(s_scr)

    lane = lax.broadcasted_iota(I32, (L, LANES), 1)
    m_a = (lane < RWKV_HEAD).astype(F32)
    m_b = 1.0 - m_a
    ri = lax.broadcasted_iota(I32, (L2, L2), 0)
    ci = lax.broadcasted_iota(I32, (L2, L2), 1)
    strict = ri > ci
    incl = ri >= ci
    eye = (ri == ci).astype(F32)
    tri = (lax.broadcasted_iota(I32, (L, L), 0) >= lax.broadcasted_iota(I32, (L, L), 1)).astype(BF16)
    hr = lax.broadcasted_iota(I32, (LANES, LANES), 0) // RWKV_HEAD
    hc = lax.broadcasted_iota(I32, (LANES, LANES), 1) // RWKV_HEAD
    head_ones = (hr == hc).astype(BF16)
    gnw = gnw_ref[...]
    gnb = gnb_ref[...]
    rk = rk_ref[...]

    def stack(x):
        return jnp.concatenate([x * m_a, x * m_b], axis=0)

    chunks = range(rows // L)
    rows_of = lambda c: slice(c * L, (c + 1) * L)
    each = lambda f, *lists: [f(*vals) for vals in zip(*lists)]

    def operands(c):
        sl = rows_of(c)
        r = r_ref[0, sl, :].astype(F32)
        k = k_ref[0, sl, :].astype(F32)
        v = v_ref[0, sl, :].astype(F32)
        kk = kk_ref[0, sl, :].astype(F32)
        a = a_ref[0, sl, :].astype(F32)
        lw = lw_ref[0, sl, :]
        hi, lo = _split_bf16(lw)
        cum = _dot(tri, hi) + _dot(tri, lo)
        c_end = cum[L - 1:L, :]
        p_inv = jnp.exp(-cum)
        p_end = jnp.exp(c_end - cum)
        b = kk * a
        return dict(
            x_a=stack(-(kk * jnp.exp(cum - lw))).astype(BF16),
            x_r=stack(r * jnp.exp(cum)),
            x_bk=jnp.concatenate([stack(b * p_inv), stack(k * p_inv)], axis=0).astype(BF16),
            v_st=stack(v).astype(BF16),
            z_hat=jnp.concatenate([stack(b * p_end), stack(k * p_end)], axis=0).astype(BF16),
            p_row=jnp.exp(c_end),
            rkr=(r * k * rk).astype(BF16), v=v)

    ops = [operands(c) for c in chunks]
    big = [_dot_nt(jnp.concatenate([o["x_a"], o["x_r"].astype(BF16)], axis=0), o["x_bk"])
           for o in ops]
    n_pow = [jnp.where(strict, m[0:L2, 0:L2], 0.0) for m in big]
    a_ak = [jnp.where(strict, m[0:L2, L2:2 * L2], 0.0).astype(BF16) for m in big]
    a_r = [jnp.concatenate([jnp.where(incl, m[L2:2 * L2, 0:L2], 0.0),
                            jnp.where(incl, m[L2:2 * L2, L2:2 * L2], 0.0)], axis=1).astype(BF16)
           for m in big]

    t_inv = [eye + n for n in n_pow]
    for _ in range(5):
        n_pow = each(lambda n: _dot(n.astype(BF16), n.astype(BF16)), n_pow)
        t_inv = each(lambda t, n: t + _dot(t.astype(BF16), n.astype(BF16)), t_inv, n_pow)

    akv = each(lambda m, o: _dot(m, o["v_st"]).astype(BF16), a_ak, ops)
    w = each(lambda t, o, x: _dot(t.astype(BF16), jnp.concatenate([o["x_a"], x], axis=1)),
             t_inv, ops, akv)
    w1 = [m[:, 0:LANES].astype(BF16) for m in w]
    w2v = each(lambda m, o: jnp.concatenate([m[:, LANES:].astype(BF16), o["v_st"]], axis=0), w, ops)
    g_mat = each(lambda o, m, x: (o["x_r"] + _dot(m[:, 0:L2], x)).astype(BF16), ops, a_r, w1)
    y0 = each(_dot, a_r, w2v)
    m_mat = each(lambda x, o: _dot_tn(x, o["z_hat"][0:L2, :]).astype(BF16), w1, ops)
    c2 = each(lambda x, o: _dot_tn(x, o["z_hat"]), w2v, ops)
    bonus_v = [_dot(o["rkr"], head_ones) * o["v"] for o in ops]

    s = s_scr[...]
    y = []
    for c in chunks:
        s_b = s.astype(BF16)
        y_st = _dot_nt(g_mat[c], s_b) + y0[c]
        s = s * ops[c]["p_row"] + _dot(s_b, m_mat[c]) + c2[c]
        y.append(y_st[0:L, :] + y_st[L:L2, :])
    s_scr[...] = s

    mean = [_dot(t.astype(BF16), head_ones) * (1.0 / RWKV_HEAD) for t in y]
    yc = each(lambda t, m: t - m, y, mean)
    var = [_dot((t * t).astype(BF16), head_ones) * (1.0 / RWKV_HEAD) for t in yc]
    for c in chunks:
        yn = yc[c] * lax.rsqrt(var[c] + GN_EPS) * gnw + gnb
        sl = rows_of(c)
        o_ref[0, sl, :] = ((yn + bonus_v[c]) * g_ref[0, sl, :].astype(F32)).astype(o_ref.dtype)


def _rwkv_scan(r, k, v, kk, a, lw, g, gn_w, gn_b, r_k, rows):
    B, S, W = r.shape
    npair = W // LANES
    seq = lambda: pl.BlockSpec((1, rows, LANES), lambda b, p, c: (b, c, p))
    par = lambda: pl.BlockSpec((1, LANES), lambda b, p, c: (0, p))
    return pl.pallas_call(
        functools.partial(_rwkv_scan_kernel, rows=rows),
        grid=(B, npair, S // rows),
        in_specs=[seq() for _ in range(7)] + [par(), par(), par()],
        out_specs=seq(),
        out_shape=jax.ShapeDtypeStruct((B, S, W), BF16),
        scratch_shapes=[pltpu.VMEM((LANES, LANES), F32)],
        compiler_params=_cparams(("parallel", "parallel", "arbitrary")),
        name="rwkv_scan",
    )(r, k, v, kk, a, lw, g, gn_w, gn_b, r_k)


def _rope128(x, cos, sin_signed):
    lane = lax.broadcasted_iota(I32, x.shape, 1)
    half = QK_ROPE // 2
    partner = jnp.where(lane < half, pltpu.roll(x, LANES - half, 1), pltpu.roll(x, half, 1))
    return x * cos + partner * sin_signed


def _mla_prep_kernel(cq_ref, ckv_ref, kr_ref, pos_ref, invf_ref, gqa_ref, gkva_ref,
                     wuq_ref, wukv_ref, gqn_ref, gkn_ref,
                     q_out, k_out, v_out, cqn_scr, ckvn_scr, cos_scr, sin_scr):
    @pl.when(pl.program_id(1) == 0)
    def _():
        cqn_scr[...] = _rms(cq_ref[...].astype(F32), gqa_ref[...]).astype(BF16)
        ckvn_scr[...] = _rms(ckv_ref[...].astype(F32), gkva_ref[...]).astype(BF16)
        ang = pos_ref[...].astype(F32) * invf_ref[...]
        lane = lax.broadcasted_iota(I32, ang.shape, 1)
        cos_scr[...] = jnp.cos(ang)
        sin_scr[...] = jnp.where(lane < QK_ROPE // 2, -1.0, 1.0) * jnp.sin(ang)

    cos = cos_scr[...]
    sin = sin_scr[...]
    inv_n = 1.0 / QK_HEAD
    q = _dot(cqn_scr[...], wuq_ref[...])
    rs = lax.rsqrt(jnp.sum(q * q, axis=-1, keepdims=True) * inv_n + NORM_EPS)
    q = q * rs * gqn_ref[...] * (QK_HEAD ** -0.5 * LOG2E)
    q_out[...] = jnp.concatenate(
        [q[:, 0:LANES], _rope128(q[:, LANES:], cos, sin)], axis=1).astype(q_out.dtype)

    kv = _dot(ckvn_scr[...], wukv_ref[...])
    k_nope = kv[:, 0:QK_NOPE]
    kr = kr_ref[...].astype(F32)
    ssq = (jnp.sum(k_nope * k_nope, axis=-1, keepdims=True)
           + jnp.sum(kr * kr, axis=-1, keepdims=True))
    rs = lax.rsqrt(ssq * inv_n + NORM_EPS)
    gkn = gkn_ref[...]
    k_out[...] = jnp.concatenate(
        [k_nope * rs * gkn[:, 0:LANES], _rope128(kr * rs * gkn[:, LANES:], cos, sin)],
        axis=1).astype(k_out.dtype)
    v_out[...] = jnp.transpose(kv[:, QK_NOPE:]).astype(v_out.dtype)


def _mla_prep(proj, pos, invf, g_qa, g_kva, wuq, wukv, gqn, gkn, tm):
    T = proj.shape[0]
    H = MLA_HEADS
    c0 = COL_MLA // Q_LORA
    ckr = (COL_MLA + Q_LORA + KV_LORA) // LANES
    full = lambda shape: pl.BlockSpec(shape, lambda i, h: (0, 0))
    return pl.pallas_call(
        _mla_prep_kernel,
        grid=(T // tm, H),
        in_specs=[
            pl.BlockSpec((tm, Q_LORA), lambda i, h: (i, c0)),
            pl.BlockSpec((tm, KV_LORA), lambda i, h: (i, c0 + 1)),
            pl.BlockSpec((tm, LANES), lambda i, h: (i, ckr)),
            pl.BlockSpec((tm, 1), lambda i, h: (i, 0)),
            full((1, LANES)), full((1, Q_LORA)), full((1, KV_LORA)),
            pl.BlockSpec((Q_LORA, QK_PAD), lambda i, h: (0, h)),
            pl.BlockSpec((KV_LORA, QK_NOPE + V_HEAD), lambda i, h: (0, h)),
            full((1, QK_PAD)), full((1, QK_PAD)),
        ],
        out_specs=[
            pl.BlockSpec((tm, QK_PAD), lambda i, h: (i, h)),
            pl.BlockSpec((tm, QK_PAD), lambda i, h: (i, h)),
            pl.BlockSpec((V_HEAD, tm), lambda i, h: (h, i)),
        ],
        out_shape=[jax.ShapeDtypeStruct((T, H * QK_PAD), BF16),
                   jax.ShapeDtypeStruct((T, H * QK_PAD), BF16),
                   jax.ShapeDtypeStruct((H * V_HEAD, T), BF16)],
        scratch_shapes=[pltpu.VMEM((tm, Q_LORA), BF16), pltpu.VMEM((tm, KV_LORA), BF16),
                        pltpu.VMEM((tm, LANES), F32), pltpu.VMEM((tm, LANES), F32)],
        compiler_params=_cparams(("parallel", "arbitrary")),
        name="mla_prep",
    )(proj, proj, proj, pos, invf, g_qa, g_kva, wuq, wukv, gqn, gkn)


MASK_VALUE = -1e30
LOG2E = 1.4426950408889634


def _flash_kernel(q_ref, k_ref, vt_ref, o_ref, s_scr, p_scr, alpha_scr, m_scr, l_scr, acc_scr,
                  *, tq):
    i = pl.program_id(2)
    q = q_ref[0]
    m_scr[...] = jnp.full_like(m_scr, MASK_VALUE)
    l_scr[...] = jnp.zeros_like(l_scr)
    acc_scr[...] = jnp.zeros_like(acc_scr)

    def scores(j, slot):
        ks = pl.ds(pl.multiple_of(j * tq, tq), tq)
        s_scr[slot] = _dot_nt(k_ref[0, ks, :], q)

    def softmax(slot, masked):
        for c in range(tq // LANES):
            cols = slice(c * LANES, (c + 1) * LANES)
            s = s_scr[slot, :, cols]
            if masked:
                kpos = lax.broadcasted_iota(I32, s.shape, 0)
                qpos = lax.broadcasted_iota(I32, s.shape, 1) + c * LANES
                s = jnp.where(kpos <= qpos, s, MASK_VALUE)
            m_old = m_scr[:, cols]
            m_new = jnp.maximum(m_old, jnp.max(s, axis=0, keepdims=True))
            p = jnp.exp2(s - m_new)
            alpha = jnp.exp2(m_old - m_new)
            l_scr[:, cols] = alpha * l_scr[:, cols] + jnp.sum(p, axis=0, keepdims=True)
            m_scr[:, cols] = m_new
            p_scr[slot, :, cols] = p.astype(BF16)
            alpha_scr[slot, :, cols] = alpha

    def weighted_values(j, slot):
        vt = vt_ref[:, pl.ds(pl.multiple_of(j * tq, tq), tq)]
        acc_scr[...] = alpha_scr[slot] * acc_scr[...] + _dot(vt, p_scr[slot])

    scores(0, 0)
    p_scr[1] = jnp.zeros((tq, tq), BF16)
    alpha_scr[1] = jnp.ones((1, tq), F32)

    def step(j, slot):
        scores(j + 1, 1 - slot)
        weighted_values(jnp.maximum(j - 1, 0), 1 - slot)
        softmax(slot, False)

    def pair(jp, carry):
        step(2 * jp, 0)
        step(2 * jp + 1, 1)
        return carry

    def finish(slot):
        weighted_values(jnp.maximum(i - 1, 0), 1 - slot)
        softmax(slot, True)
        weighted_values(i, slot)
        o_ref[0] = jnp.transpose(acc_scr[...] / l_scr[...]).astype(o_ref.dtype)

    lax.fori_loop(0, i // 2, pair, 0)

    @pl.when(i % 2 == 0)
    def _():
        finish(0)

    @pl.when(i % 2 == 1)
    def _():
        step(i - 1, 0)
        finish(1)


def _flash(q, k, vt, tq):
    B, S, _ = q.shape
    H = MLA_HEADS
    return pl.pallas_call(
        functools.partial(_flash_kernel, tq=tq),
        grid=(B, H, S // tq),
        in_specs=[
            pl.BlockSpec((1, tq, QK_PAD), lambda b, h, i: (b, i, h)),
            pl.BlockSpec((1, S, QK_PAD), lambda b, h, i: (b, 0, h)),
            pl.BlockSpec((V_HEAD, S), lambda b, h, i: (h, b)),
        ],
        out_specs=pl.BlockSpec((1, tq, V_HEAD), lambda b, h, i: (b, i, h)),
        out_shape=jax.ShapeDtypeStruct((B, S, H * V_HEAD), BF16),
        scratch_shapes=[pltpu.VMEM((2, tq, tq), F32), pltpu.VMEM((2, tq, tq), BF16),
                        pltpu.VMEM((2, 1, tq), F32),
                        pltpu.VMEM((1, tq), F32), pltpu.VMEM((1, tq), F32),
                        pltpu.VMEM((V_HEAD, tq), F32)],
        compiler_params=_cparams(("parallel", "parallel", "arbitrary")),
        name="flash_attn",
    )(q, k, vt)


def _merge_kernel(oa_ref, ob_ref, wa_ref, wb_ref, ga_ref, gb_ref, o_ref):
    ya = _dot(oa_ref[...], wa_ref[...])
    yb = _dot(ob_ref[...], wb_ref[...])
    ga = _sigmoid(ga_ref[...].astype(F32))
    gb = _sigmoid(gb_ref[...].astype(F32))
    o_ref[...] = (ga * ya + gb * yb).astype(o_ref.dtype)


def _merge(oa, ob, wa, wb, proj, tm, tn):
    T, K = oa.shape
    N = wa.shape[1]
    ca = COL_GA // tn
    cb = COL_GB // tn
    return pl.pallas_call(
        _merge_kernel,
        grid=(T // tm, N // tn),
        in_specs=[
            pl.BlockSpec((tm, K), lambda i, j: (i, 0)),
            pl.BlockSpec((tm, K), lambda i, j: (i, 0)),
            pl.BlockSpec((K, tn), lambda i, j: (0, j)),
            pl.BlockSpec((K, tn), lambda i, j: (0, j)),
            pl.BlockSpec((tm, tn), lambda i, j: (i, ca + j)),
            pl.BlockSpec((tm, tn), lambda i, j: (i, cb + j)),
        ],
        out_specs=pl.BlockSpec((tm, tn), lambda i, j: (i, j)),
        out_shape=jax.ShapeDtypeStruct((T, N), BF16),
        compiler_params=_cparams(("parallel", "parallel")),
        name="merge",
    )(oa, ob, wa, wb, proj, proj)


def _pack_rows(x, out_ref, n, row0=0):
    for c in range(PACK_ROWS):
        lo = x[:, c * 2 * LANES:c * 2 * LANES + LANES]
        hi = x[:, c * 2 * LANES + LANES:(c + 1) * 2 * LANES]
        lo_b = pltpu.bitcast(lo.astype(BF16).astype(F32), U32)
        hi_b = pltpu.bitcast(hi.astype(BF16).astype(F32), U32)
        out_ref[pl.ds(row0 * PACK_ROWS + c, n, stride=PACK_ROWS), :] = hi_b | (lo_b >> 16)


def _unpack_rows(ref, c, tm, lead=None):
    idx = (pl.ds(c, tm, stride=PACK_ROWS), slice(None))
    if lead is not None:
        idx = (lead,) + idx
    u = ref[idx]
    lo = pltpu.bitcast(u << 16, F32)
    hi = pltpu.bitcast(u & jnp.uint32(0xFFFF0000), F32)
    return lo, hi


def _wo_kernel(m_ref, x_ref, wo_ref, g_ref, wr_ref, x1_ref, h2_ref, h2p_ref, lg_ref, *, tm):
    hw = WO_ROWS
    groups = [slice(h * hw, (h + 1) * hw) for h in range(tm // hw)]
    wo = wo_ref[...]
    x1 = [x_ref[r, :] + _dot(m_ref[r, :], wo) for r in groups]
    h2 = [_rms(v, g_ref[...]) for v in x1]
    w_hi, w_lo = _split_bf16(wr_ref[...])
    for h, r in enumerate(groups):
        x1_ref[r, :] = x1[h]
        h2_ref[r, :] = h2[h].astype(BF16)
        _pack_rows(h2[h], h2p_ref, hw, row0=h * hw)
        h_hi, h_lo = _split_bf16(h2[h])
        lg_ref[:, r] = _dot_nt(w_hi, h_hi) + _dot_nt(w_hi, h_lo) + _dot_nt(w_lo, h_hi)


def _wo(merged, x2, wo, g, wr_t, tm):
    T, D = x2.shape
    E = wr_t.shape[0]
    return pl.pallas_call(
        functools.partial(_wo_kernel, tm=tm),
        grid=(T // tm,),
        in_specs=[
            pl.BlockSpec((tm, D), lambda i: (i, 0)),
            pl.BlockSpec((tm, D), lambda i: (i, 0)),
            pl.BlockSpec((D, D), lambda i: (0, 0), pipeline_mode=pl.Buffered(1)),
            pl.BlockSpec((1, D), lambda i: (0, 0)),
            pl.BlockSpec((E, D), lambda i: (0, 0)),
        ],
        out_specs=[
            pl.BlockSpec((tm, D), lambda i: (i, 0)),
            pl.BlockSpec((tm, D), lambda i: (i, 0)),
            pl.BlockSpec((tm * PACK_ROWS, PACK_W), lambda i: (i, 0)),
            pl.BlockSpec((E, tm), lambda i: (0, i)),
        ],
        out_shape=[jax.ShapeDtypeStruct((T, D), F32),
                   jax.ShapeDtypeStruct((T, D), BF16),
                   jax.ShapeDtypeStruct((T * PACK_ROWS, PACK_W), U32),
                   jax.ShapeDtypeStruct((E, T), F32)],
        compiler_params=_cparams(("parallel",)),
        name="wo_norm_router",
    )(merged, x2, wo, g, wr_t)


def _route_kernel(lg_ref, bias_ref, eidx_ref, wts_ref, rnk_ref, cnt_ref, carry_scr, *, tm):
    E = N_EXPERTS
    NEG = -jnp.inf

    @pl.when(pl.program_id(0) == 0)
    def _():
        carry_scr[...] = jnp.zeros_like(carry_scr)

    scores = _sigmoid(lg_ref[...])
    choice = scores + bias_ref[...]
    c3 = choice.reshape(N_GROUPS, GROUP_SIZE, tm)
    sub = lax.broadcasted_iota(I32, c3.shape, 1)
    m1 = jnp.max(c3, axis=1, keepdims=True)
    i1 = jnp.min(jnp.where(c3 == m1, sub, GROUP_SIZE), axis=1, keepdims=True)
    m2 = jnp.max(jnp.where(sub == i1, NEG, c3), axis=1, keepdims=True)
    gs = (m1 + m2).reshape(N_GROUPS, tm)
    gi = lax.broadcasted_iota(I32, gs.shape, 0)
    gsel = jnp.zeros(gs.shape, F32)
    for _ in range(TOPK_GROUPS):
        mx = jnp.max(gs, axis=0, keepdims=True)
        ix = jnp.min(jnp.where(gs == mx, gi, N_GROUPS), axis=0, keepdims=True)
        hit = gi == ix
        gsel = jnp.where(hit, 1.0, gsel)
        gs = jnp.where(hit, NEG, gs)
    emask = jnp.broadcast_to(gsel.reshape(N_GROUPS, 1, tm), (N_GROUPS, GROUP_SIZE, tm)).reshape(E, tm)
    x = jnp.where(emask > 0.5, choice, NEG)
    ei = lax.broadcasted_iota(I32, x.shape, 0)
    sel = jnp.zeros(x.shape, F32)
    idx_rows, w_rows = [], []
    for _ in range(TOP_K):
        mx = jnp.max(x, axis=0, keepdims=True)
        ix = jnp.min(jnp.where(x == mx, ei, E), axis=0, keepdims=True)
        hit = ei == ix
        w_rows.append(jnp.sum(jnp.where(hit, scores, 0.0), axis=0, keepdims=True))
        idx_rows.append(ix)
        sel = jnp.where(hit, 1.0, sel)
        x = jnp.where(hit, NEG, x)
    w = jnp.concatenate(w_rows, axis=0)
    w = w / (jnp.sum(w, axis=0, keepdims=True) + 1e-20) * ROUTED_SCALE
    eidx_ref[...] = jnp.concatenate(idx_rows, axis=0)
    wts_ref[...] = w

    upper = (lax.broadcasted_iota(I32, (tm, tm), 0) < lax.broadcasted_iota(I32, (tm, tm), 1)).astype(BF16)
    base = carry_scr[...][:, 0:1]
    excl = _dot(sel.astype(BF16), upper) + base
    rnk_ref[...] = jnp.concatenate(
        [jnp.sum(jnp.where(ei == ix, excl, 0.0), axis=0, keepdims=True) for ix in idx_rows],
        axis=0).astype(I32)
    carry_scr[...] = carry_scr[...] + jnp.sum(sel, axis=1, keepdims=True)
    cnt_ref[...] = carry_scr[...]


def _route(logits_t, bias, tm):
    E, T = logits_t.shape
    K = TOP_K
    tok = lambda: pl.BlockSpec((K, tm), lambda i: (0, i))
    return pl.pallas_call(
        functools.partial(_route_kernel, tm=tm),
        grid=(T // tm,),
        in_specs=[pl.BlockSpec((E, tm), lambda i: (0, i)),
                  pl.BlockSpec((E, 1), lambda i: (0, 0))],
        out_specs=[tok(), tok(), tok(), pl.BlockSpec((E, LANES), lambda i: (0, 0))],
        out_shape=[jax.ShapeDtypeStruct((K, T), I32), jax.ShapeDtypeStruct((K, T), F32),
                   jax.ShapeDtypeStruct((K, T), I32), jax.ShapeDtypeStruct((E, LANES), F32)],
        scratch_shapes=[pltpu.VMEM((E, LANES), F32)],
        compiler_params=_cparams(("arbitrary",)),
        name="route",
    )(logits_t, bias)


def _dispatch_kernel(eidx_ref, rnk_ref, pst_ref, h_ref, xs_in, xs_out, sem, *, tm):
    del xs_in

    def row_copy(t, dest):
        return pltpu.make_async_copy(
            h_ref.at[pl.ds(pl.multiple_of(t * PACK_ROWS, PACK_ROWS), PACK_ROWS), :],
            xs_out.at[pl.ds(pl.multiple_of(dest * PACK_ROWS, PACK_ROWS), PACK_ROWS), :],
            sem)

    def issue(t, carry):
        for k in range(TOP_K):
            row_copy(t, pst_ref[eidx_ref[k, t]] + rnk_ref[k, t]).start()
        return carry

    lax.fori_loop(0, tm, issue, 0)

    def drain(t, carry):
        for k in range(TOP_K):
            row_copy(0, 0).wait()
        return carry

    lax.fori_loop(0, tm, drain, 0)


def _dispatch(eidx, rnk, pstart, h2p, xs_zero, tm):
    K, T = eidx.shape
    smem_tok = lambda: pl.BlockSpec((K, tm), lambda i: (0, i), memory_space=pltpu.SMEM)
    return pl.pallas_call(
        functools.partial(_dispatch_kernel, tm=tm),
        grid=(T // tm,),
        in_specs=[smem_tok(), smem_tok(),
                  pl.BlockSpec(memory_space=pltpu.SMEM),
                  pl.BlockSpec((tm * PACK_ROWS, PACK_W), lambda i: (i, 0)),
                  pl.BlockSpec(memory_space=pl.ANY)],
        out_specs=pl.BlockSpec(memory_space=pl.ANY),
        out_shape=jax.ShapeDtypeStruct(xs_zero.shape, xs_zero.dtype),
        scratch_shapes=[pltpu.SemaphoreType.DMA(())],
        input_output_aliases={4: 0},
        compiler_params=_cparams(("arbitrary",)),
        name="dispatch",
    )(eidx, rnk, pstart, h2p, xs_zero)


def _ffn_kernel(te_ref, nu_ref, xs_ref, wg_ref, wu_ref, wd_ref, ys_ref, wg_b, wu_b, wd_b, *, tm):
    i = pl.program_id(0)
    used = i < nu_ref[0]

    @pl.when(used & ((i == 0) | (te_ref[i] != te_ref[jnp.maximum(i - 1, 0)])))
    def _():
        wg_b[...] = wg_ref[0].astype(BF16)
        wu_b[...] = wu_ref[0].astype(BF16)
        wd_b[...] = wd_ref[0].astype(BF16)

    @pl.when(used)
    def _():
        parts = []
        for c in range(PACK_ROWS):
            lo, hi = _unpack_rows(xs_ref, c, tm)
            parts += [lo.astype(BF16), hi.astype(BF16)]
        a = jnp.concatenate(parts, axis=1)
        hg = _dot(a, wg_b[...])
        hu = _dot(a, wu_b[...])
        hid = (hg * _sigmoid(hg) * hu).astype(BF16)
        _pack_rows(_dot(hid, wd_b[...]), ys_ref, tm)

    @pl.when(jnp.logical_not(used))
    def _():
        ys_ref[...] = jnp.zeros_like(ys_ref)


def _ffn(tile_e, n_used, xs, wg, wu, wd, tm):
    P8 = xs.shape[0]
    nt = P8 // (tm * PACK_ROWS)
    D, I = wg.shape[1], wg.shape[2]
    grid_spec = pltpu.PrefetchScalarGridSpec(
        num_scalar_prefetch=2,
        grid=(nt,),
        in_specs=[
            pl.BlockSpec((tm * PACK_ROWS, PACK_W), lambda i, te, nu: (i, 0)),
            pl.BlockSpec((1, D, I), lambda i, te, nu: (te[i], 0, 0)),
            pl.BlockSpec((1, D, I), lambda i, te, nu: (te[i], 0, 0)),
            pl.BlockSpec((1, I, D), lambda i, te, nu: (te[i], 0, 0)),
        ],
        out_specs=pl.BlockSpec((tm * PACK_ROWS, PACK_W), lambda i, te, nu: (i, 0)),
        scratch_shapes=[pltpu.VMEM((D, I), BF16), pltpu.VMEM((D, I), BF16),
                        pltpu.VMEM((I, D), BF16)],
    )
    return pl.pallas_call(
        functools.partial(_ffn_kernel, tm=tm),
        grid_spec=grid_spec,
        out_shape=jax.ShapeDtypeStruct(xs.shape, U32),
        compiler_params=_cparams(("arbitrary",)),
        name="expert_ffn",
    )(tile_e, n_used, xs, wg, wu, wd)


def _combine_kernel(eidx_ref, rnk_ref, pst_ref, ys_hbm, wt_ref, x1_ref, h2_ref,
                    wsg_ref, wsu_ref, wsd_ref, o_ref, buf, sem, *, tm):
    def issue(t, carry):
        for k in range(TOP_K):
            dest = pst_ref[eidx_ref[k, t]] + rnk_ref[k, t]
            pltpu.make_async_copy(
                ys_hbm.at[pl.ds(pl.multiple_of(dest * PACK_ROWS, PACK_ROWS), PACK_ROWS), :],
                buf.at[k, pl.ds(pl.multiple_of(t * PACK_ROWS, PACK_ROWS), PACK_ROWS), :],
                sem).start()
        return carry

    lax.fori_loop(0, tm, issue, 0)

    h2 = h2_ref[...]
    hg = _dot(h2, wsg_ref[...])
    hu = _dot(h2, wsu_ref[...])
    o_ref[...] = x1_ref[...] + _dot((hg * _sigmoid(hg) * hu).astype(BF16), wsd_ref[...])

    def drain(t, carry):
        for k in range(TOP_K):
            pltpu.make_async_copy(ys_hbm.at[pl.ds(0, PACK_ROWS), :],
                                  buf.at[0, pl.ds(0, PACK_ROWS), :], sem).wait()
        return carry

    lax.fori_loop(0, tm, drain, 0)

    wt = wt_ref[...]
    for c in range(PACK_ROWS):
        acc_lo = jnp.zeros((tm, LANES), F32)
        acc_hi = jnp.zeros((tm, LANES), F32)
        for k in range(TOP_K):
            lo, hi = _unpack_rows(buf, c, tm, lead=k)
            wk = wt[:, k:k + 1]
            acc_lo = acc_lo + wk * lo
            acc_hi = acc_hi + wk * hi
        s0 = c * 2 * LANES
        o_ref[:, s0:s0 + LANES] = o_ref[:, s0:s0 + LANES] + acc_lo
        o_ref[:, s0 + LANES:s0 + 2 * LANES] = o_ref[:, s0 + LANES:s0 + 2 * LANES] + acc_hi


def _combine(eidx, rnk, pstart, ys, wts_t, x1, h2, wsg, wsu, wsd, tm):
    K, T = eidx.shape
    D = x1.shape[1]
    I = wsg.shape[1]
    smem_tok = lambda: pl.BlockSpec((K, tm), lambda i: (0, i), memory_space=pltpu.SMEM)
    return pl.pallas_call(
        functools.partial(_combine_kernel, tm=tm),
        grid=(T // tm,),
        in_specs=[smem_tok(), smem_tok(),
                  pl.BlockSpec(memory_space=pltpu.SMEM),
                  pl.BlockSpec(memory_space=pl.ANY),
                  pl.BlockSpec((tm, K), lambda i: (i, 0)),
                  pl.BlockSpec((tm, D), lambda i: (i, 0)),
                  pl.BlockSpec((tm, D), lambda i: (i, 0)),
                  pl.BlockSpec((D, I), lambda i: (0, 0)),
                  pl.BlockSpec((D, I), lambda i: (0, 0)),
                  pl.BlockSpec((I, D), lambda i: (0, 0))],
        out_specs=pl.BlockSpec((tm, D), lambda i: (i, 0)),
        out_shape=jax.ShapeDtypeStruct((T, D), F32),
        scratch_shapes=[pltpu.VMEM((K, tm * PACK_ROWS, PACK_W), U32),
                        pltpu.SemaphoreType.DMA(())],
        compiler_params=_cparams(("arbitrary",)),
        name="combine",
    )(eidx, rnk, pstart, ys, wts_t, x1, h2, wsg, wsu, wsd)


def _ple_kernel(xf_ref, xt_ref, p_ref, wp_ref, gpost_ref, gin_ref, wg_ref, o_ref,
                hn_scr, ple_scr, *, tn):
    j = pl.program_id(1)

    @pl.when(j == 0)
    def _():
        hn_scr[...] = _rms(xf_ref[...], gin_ref[...]).astype(BF16)
        ple_scr[...] = _rms(_dot(p_ref[...].astype(BF16), wp_ref[...]), gpost_ref[...])

    cols = pl.ds(pl.multiple_of(j * tn, tn), tn)
    gate = _sigmoid(_dot(hn_scr[...], wg_ref[...]))
    o_ref[...] = xt_ref[...] + gate * ple_scr[:, cols]


def _ple(x2, p2, wp, gpost, gin, wg, tm, tn):
    T, D = x2.shape
    Pd = p2.shape[1]
    return pl.pallas_call(
        functools.partial(_ple_kernel, tn=tn),
        grid=(T // tm, D // tn),
        in_specs=[
            pl.BlockSpec((tm, D), lambda i, j: (i, 0)),
            pl.BlockSpec((tm, tn), lambda i, j: (i, j)),
            pl.BlockSpec((tm, Pd), lambda i, j: (i, 0)),
            pl.BlockSpec((Pd, D), lambda i, j: (0, 0)),
            pl.BlockSpec((1, D), lambda i, j: (0, 0)),
            pl.BlockSpec((1, D), lambda i, j: (0, 0)),
            pl.BlockSpec((D, tn), lambda i, j: (0, j)),
        ],
        out_specs=pl.BlockSpec((tm, tn), lambda i, j: (i, j)),
        out_shape=jax.ShapeDtypeStruct((T, D), F32),
        scratch_shapes=[pltpu.VMEM((tm, D), BF16), pltpu.VMEM((tm, D), F32)],
        compiler_params=_cparams(("parallel", "arbitrary")),
        name="ple",
    )(x2, x2, p2, wp, gpost, gin, wg)


def _pad_cols(w, n):
    return jnp.pad(w, ((0, 0), (0, n - w.shape[1])))


def _pad_rows(w, n):
    return jnp.pad(w, ((0, n - w.shape[0]), (0, 0)))


def _layout_w_in(w):
    W = RWKV_WIDTH
    c = 3 * W
    segs = [w[:, 0:c],
            _pad_cols(w[:, c:c + DECAY_LORA], LORA_PAD),
            _pad_cols(w[:, c + DECAY_LORA:c + DECAY_LORA + AAA_LORA], LORA_PAD),
            _pad_cols(w[:, c + DECAY_LORA + AAA_LORA:c + DECAY_LORA + AAA_LORA + GATE_LORA], GATE_LORA_PAD)]
    c += DECAY_LORA + AAA_LORA + GATE_LORA
    mla_cols = Q_LORA + KV_LORA + QK_ROPE
    segs.append(_pad_cols(w[:, c:c + mla_cols], MLA_PAD))
    c += mla_cols
    segs.append(w[:, c:])
    out = jnp.concatenate(segs, axis=1).astype(BF16)
    assert out.shape[1] == IN_PAD
    return out


def _layout_mu(mu):
    W = RWKV_WIDTH
    c = 3 * W
    segs = [mu[0:c],
            jnp.pad(mu[c:c + DECAY_LORA], (0, LORA_PAD - DECAY_LORA)),
            jnp.pad(mu[c + DECAY_LORA:c + DECAY_LORA + AAA_LORA], (0, LORA_PAD - AAA_LORA)),
            jnp.pad(mu[c + DECAY_LORA + AAA_LORA:], (0, GATE_LORA_PAD - GATE_LORA))]
    return jnp.concatenate(segs).reshape(1, RWKV_PAD)


def _layer(x, p, positions, g_mix, w_in, mu_rwkv, w0, w2, a0, a2, g2, k_k, k_a, r_k,
           gn_w, gn_b, w_a_up, g_qa, g_kva, w_uq, w_ukv, g_qn, g_kn, w_b_up, w_o,
           g_ffn, w_router, router_bias, w_exp_gate, w_exp_up, w_exp_down,
           w_sh_gate, w_sh_up, w_sh_down, w_ple, g_ple_post, g_ple_in, w_ple_gate,
           *, tiles):
    B, S, D = x.shape
    T = B * S
    W = RWKV_WIDTH
    x2 = x.reshape(T, D)
    row = lambda v: v.reshape(1, -1).astype(F32)

    proj = _inproj(x2, row(g_mix), _layout_w_in(w_in), tiles["tm_in"], tiles["tn_in"])

    head_blk = (jnp.arange(W)[:, None] // RWKV_HEAD == jnp.arange(W)[None, :] // RWKV_HEAD).astype(BF16)
    r, k, v, kk, a, lw, g = _rwkv_prep(
        proj, _layout_mu(mu_rwkv), row(w0), _pad_rows(w2, LORA_PAD).astype(BF16), row(a0),
        _pad_rows(a2, LORA_PAD).astype(BF16), _pad_rows(g2, GATE_LORA_PAD).astype(BF16),
        row(k_k), row(k_a), head_blk, tiles["tm_prep"], S)
    b3 = lambda t: t.reshape(B, S, W)
    o_a = _rwkv_scan(b3(r), b3(k), b3(v), b3(kk), b3(a), b3(lw), b3(g),
                     row(gn_w), row(gn_b), row(r_k), tiles["rows_scan"]).reshape(T, W)

    half = QK_ROPE // 2
    inv_freq = ROPE_THETA ** (-jnp.arange(half, dtype=F32) / half)
    invf = jnp.concatenate([inv_freq, inv_freq, jnp.zeros((LANES - QK_ROPE,), F32)]).reshape(1, LANES)
    wuq = jnp.pad(w_uq.reshape(Q_LORA, MLA_HEADS, QK_HEAD),
                  ((0, 0), (0, 0), (0, QK_PAD - QK_HEAD))).reshape(Q_LORA, MLA_HEADS * QK_PAD)
    pad_g = lambda gv: jnp.pad(gv, (0, QK_PAD - QK_HEAD)).reshape(1, QK_PAD)
    q, kx, vx = _mla_prep(proj, positions.reshape(T, 1).astype(I32), invf, row(g_qa), row(g_kva),
                          wuq.astype(BF16), w_ukv.astype(BF16), pad_g(g_qn), pad_g(g_kn),
                          tiles["tm_mla"])
    o_b = _flash(q.reshape(B, S, -1), kx.reshape(B, S, -1), vx, tiles["tq"]).reshape(T, MLA_WIDTH)

    merged = _merge(o_a, o_b, w_a_up.astype(BF16), w_b_up.astype(BF16), proj,
                    tiles["tm_merge"], tiles["tn_merge"])

    x1, h2, h2p, logits_t = _wo(merged, x2, w_o.astype(BF16), row(g_ffn),
                                jnp.transpose(w_router).astype(F32), tiles["tm_wo"])
    eidx, wts, rnk, cnt = _route(logits_t, router_bias.reshape(N_EXPERTS, 1).astype(F32),
                                 tiles["tm_route"])
    tmf = tiles["tm_ffn"]
    counts = cnt[:, 0].astype(I32)
    pcounts = (counts + tmf - 1) // tmf * tmf
    pends = jnp.cumsum(pcounts)
    pstart = (pends - pcounts).astype(I32)
    n_tiles = (T * TOP_K) // tmf + N_EXPERTS
    tile_start = jnp.arange(n_tiles, dtype=I32) * tmf
    tile_e = jnp.minimum(jnp.sum((pends[None, :] <= tile_start[:, None]).astype(I32), axis=1),
                         N_EXPERTS - 1)
    xs = _dispatch(eidx, rnk, pstart, h2p,
                   jnp.zeros((n_tiles * tmf * PACK_ROWS, PACK_W), U32), tiles["tm_disp"])
    n_used = (pends[N_EXPERTS - 1:] // tmf).astype(I32)
    ys = _ffn(tile_e, n_used, xs, w_exp_gate, w_exp_up, w_exp_down, tmf)
    x3 = _combine(eidx, rnk, pstart, ys, jnp.transpose(wts), x1, h2,
                  w_sh_gate.astype(BF16), w_sh_up.astype(BF16), w_sh_down.astype(BF16),
                  tiles["tm_comb"])

    out = _ple(x3, p.reshape(T, PLE_DIM), w_ple.astype(BF16), row(g_ple_post), row(g_ple_in),
               w_ple_gate.astype(BF16), tiles["tm_ple"], tiles["tn_ple"])
    return out.reshape(B, S, D)


TILES = dict(tm_in=1024, tn_in=512, tm_prep=256, rows_scan=512, tm_mla=512, tq=512,
             tm_merge=512, tn_merge=512, tm_wo=512, tm_route=512, tm_ffn=256,
             tm_disp=512, tm_comb=256, tm_ple=512, tn_ple=1024)


def kernel(x, p, positions, g_mix, w_in, mu_rwkv, w0, w2, a0, a2, g2, k_k, k_a, r_k, gn_w, gn_b, w_a_up, g_qa, g_kva, w_uq, w_ukv, g_qn, g_kn, w_b_up, w_o, g_ffn, w_router, router_bias, w_exp_gate, w_exp_up, w_exp_down, w_sh_gate, w_sh_up, w_sh_down, w_ple, g_ple_post, g_ple_in, w_ple_gate):
    args = (g_mix, w_in, mu_rwkv, w0, w2, a0, a2, g2, k_k, k_a, r_k, gn_w, gn_b, w_a_up,
            g_qa, g_kva, w_uq, w_ukv, g_qn, g_kn, w_b_up, w_o, g_ffn, w_router, router_bias,
            w_exp_gate, w_exp_up, w_exp_down, w_sh_gate, w_sh_up, w_sh_down, w_ple,
            g_ple_post, g_ple_in, w_ple_gate)
    assert all(t.shape[0] == 1 for t in args), "single-layer stack expected"
    return _layer(x, p[0], positions, *[t[0] for t in args], tiles=TILES)
```

```python
import functools

import jax
import jax.numpy as jnp
from jax import lax
from jax.experimental import pallas as pl
from jax.experimental.pallas import tpu as pltpu

F32 = jnp.float32
BF16 = jnp.bfloat16
I32 = jnp.int32
U32 = jnp.uint32

D_MODEL = 2048
PLE_DIM = 256
NORM_EPS = 1e-6
RWKV_HEADS = 16
RWKV_HEAD = 64
RWKV_WIDTH = RWKV_HEADS * RWKV_HEAD
DECAY_LORA = 64
AAA_LORA = 64
GATE_LORA = 160
GN_EPS = 64e-5
MLA_HEADS = 8
Q_LORA = 512
KV_LORA = 512
QK_NOPE = 128
QK_ROPE = 64
QK_HEAD = QK_NOPE + QK_ROPE
V_HEAD = 128
MLA_WIDTH = MLA_HEADS * V_HEAD
ROPE_THETA = 10000.0
N_EXPERTS = 64
TOP_K = 8
N_GROUPS = 8
GROUP_SIZE = N_EXPERTS // N_GROUPS
TOPK_GROUPS = 4
MOE_INTER = 512
ROUTED_SCALE = 2.5

LANES = 128
QK_PAD = 2 * LANES
PACK_ROWS = 8
PACK_W = D_MODEL // 2 // PACK_ROWS
assert PACK_W == LANES

LORA_PAD = 128
GATE_LORA_PAD = 256
RWKV_PAD = 3 * RWKV_WIDTH + 2 * LORA_PAD + GATE_LORA_PAD
MLA_PAD = 1536
COL_RWKV = 0
COL_MLA = RWKV_PAD
COL_GA = COL_MLA + MLA_PAD
COL_GB = COL_GA + D_MODEL
IN_PAD = COL_GB + D_MODEL

VMEM_LIMIT = 56 * 1024 * 1024
CHUNK = 64
WO_ROWS = 128


def _cparams(sem):
    return pltpu.CompilerParams(dimension_semantics=sem, vmem_limit_bytes=VMEM_LIMIT)


def _dot(a, b):
    return jnp.dot(a, b, preferred_element_type=F32)


def _dot_nt(a, b):
    return lax.dot_general(a, b, (((1,), (1,)), ((), ())), preferred_element_type=F32)


def _dot_tn(a, b):
    return lax.dot_general(a, b, (((0,), (0,)), ((), ())), preferred_element_type=F32)


def _split_bf16(x):
    hi = x.astype(BF16)
    lo = (x - hi.astype(F32)).astype(BF16)
    return hi, lo


def _rms(x, g):
    ms = jnp.mean(x * x, axis=-1, keepdims=True)
    return x * lax.rsqrt(ms + NORM_EPS) * g


def _sigmoid(x):
    return 1.0 / (1.0 + jnp.exp(-x))


def _inproj_kernel(x_ref, g_ref, w_ref, o_ref, h_scr):
    @pl.when(pl.program_id(1) == 0)
    def _():
        h_scr[...] = _rms(x_ref[...], g_ref[...]).astype(BF16)

    o_ref[...] = _dot(h_scr[...], w_ref[...]).astype(o_ref.dtype)


def _inproj(x2, g, w, tm, tn):
    T, D = x2.shape
    N = w.shape[1]
    return pl.pallas_call(
        _inproj_kernel,
        grid=(T // tm, N // tn),
        in_specs=[
            pl.BlockSpec((tm, D), lambda i, j: (i, 0)),
            pl.BlockSpec((1, D), lambda i, j: (0, 0)),
            pl.BlockSpec((D, tn), lambda i, j: (0, j)),
        ],
        out_specs=pl.BlockSpec((tm, tn), lambda i, j: (i, j)),
        out_shape=jax.ShapeDtypeStruct((T, N), BF16),
        scratch_shapes=[pltpu.VMEM((tm, D), BF16)],
        compiler_params=_cparams(("parallel", "arbitrary")),
        name="inproj",
    )(x2, g, w)


HALO = 16


def _rwkv_prep_kernel(cur_ref, halo_ref, mu_ref, w0_ref, w2_ref, a0_ref, a2_ref, g2_ref,
                      kk_ref, ka_ref, bd_ref,
                      r_out, k_out, v_out, kk_out, a_out, lw_out, g_out, *, tm, seq):
    i = pl.program_id(0)
    cur = cur_ref[...].astype(F32)
    first = (i * tm) % seq == 0
    last_prev = halo_ref[HALO - 1:HALO, :].astype(F32)
    last_prev = jnp.where(first, 0.0, last_prev)
    row = lax.broadcasted_iota(I32, cur.shape, 0)
    prev = jnp.where(row == 0, last_prev, pltpu.roll(cur, 1, 0))
    u = cur + (prev - cur) * mu_ref[...]
    W = RWKV_WIDTH
    r = u[:, 0:W]
    k = u[:, W:2 * W]
    v = u[:, 2 * W:3 * W]
    dw = u[:, 3 * W:3 * W + LORA_PAD]
    da = u[:, 3 * W + LORA_PAD:3 * W + 2 * LORA_PAD]
    dg = u[:, 3 * W + 2 * LORA_PAD:3 * W + 2 * LORA_PAD + GATE_LORA_PAD]
    z = w0_ref[...] + _dot(jnp.tanh(dw).astype(BF16), w2_ref[...])
    w_log = -(jnp.maximum(-z, 0.0) + jnp.log(1.0 + jnp.exp(-jnp.abs(z)))) - 0.5
    lw_out[...] = -jnp.exp(w_log)
    a = _sigmoid(a0_ref[...] + _dot(da.astype(BF16), a2_ref[...]))
    g_out[...] = _dot(_sigmoid(dg).astype(BF16), g2_ref[...]).astype(g_out.dtype)
    kk = k * kk_ref[...]
    hi, lo = _split_bf16(kk * kk)
    ssq = _dot(hi, bd_ref[...]) + _dot(lo, bd_ref[...])
    kk = kk / jnp.maximum(jnp.sqrt(ssq), 1e-12)
    r_out[...] = r.astype(r_out.dtype)
    k_out[...] = (k * (1.0 + (a - 1.0) * ka_ref[...])).astype(k_out.dtype)
    v_out[...] = v.astype(v_out.dtype)
    kk_out[...] = kk.astype(kk_out.dtype)
    a_out[...] = a.astype(a_out.dtype)


def _rwkv_prep(proj, mu, w0, w2, a0, a2, g2, k_k, k_a, bd, tm, seq):
    T = proj.shape[0]
    W = RWKV_WIDTH
    full = lambda shape: pl.BlockSpec(shape, lambda i: (0, 0))
    out = lambda: pl.BlockSpec((tm, W), lambda i: (i, 0))
    return pl.pallas_call(
        functools.partial(_rwkv_prep_kernel, tm=tm, seq=seq),
        grid=(T // tm,),
        in_specs=[
            pl.BlockSpec((tm, RWKV_PAD), lambda i: (i, 0)),
            pl.BlockSpec((HALO, RWKV_PAD), lambda i: (jnp.maximum(i * (tm // HALO) - 1, 0), 0)),
            full((1, RWKV_PAD)), full((1, W)), full((LORA_PAD, W)), full((1, W)),
            full((LORA_PAD, W)), full((GATE_LORA_PAD, W)), full((1, W)), full((1, W)),
            full((W, W)),
        ],
        out_specs=[out() for _ in range(7)],
        out_shape=[jax.ShapeDtypeStruct((T, W), dt)
                   for dt in (BF16, BF16, BF16, BF16, BF16, F32, BF16)],
        compiler_params=_cparams(("parallel",)),
        name="rwkv_prep",
    )(proj, proj, mu, w0, w2, a0, a2, g2, k_k, k_a, bd)


def _rwkv_scan_kernel(r_ref, k_ref, v_ref, kk_ref, a_ref, lw_ref, g_ref,
                      gnw_ref, gnb_ref, rk_ref, o_ref, s_scr, *, rows):
    L = CHUNK
    L2 = 2 * L

    @pl.when(pl.program_id(2) == 0)
    def _():
        s_scr[...] = jnp.zeros_like(s_scr)

    lane = lax.broadcasted_iota(I32, (L, LANES), 1)
    m_a = (lane < RWKV_HEAD).astype(F32)
    m_b = 1.0 - m_a
    ri = lax.broadcasted_iota(I32, (L2, L2), 0)
    ci = lax.broadcasted_iota(I32, (L2, L2), 1)
    strict = ri > ci
    incl = ri >= ci
    eye = (ri == ci).astype(F32)
    tri = (lax.broadcasted_iota(I32, (L, L), 0) >= lax.broadcasted_iota(I32, (L, L), 1)).astype(BF16)
    hr = lax.broadcasted_iota(I32, (LANES, LANES), 0) // RWKV_HEAD
    hc = lax.broadcasted_iota(I32, (LANES, LANES), 1) // RWKV_HEAD
    head_ones = (hr == hc).astype(BF16)
    gnw = gnw_ref[...]
    gnb = gnb_ref[...]
    rk = rk_ref[...]

    def stack(x):
        return jnp.concatenate([x * m_a, x * m_b], axis=0)

    chunks = range(rows // L)
    rows_of = lambda c: slice(c * L, (c + 1) * L)
    each = lambda f, *lists: [f(*vals) for vals in zip(*lists)]

    def operands(c):
        sl = rows_of(c)
        r = r_ref[0, sl, :].astype(F32)
        k = k_ref[0, sl, :].astype(F32)
        v = v_ref[0, sl, :].astype(F32)
        kk = kk_ref[0, sl, :].astype(F32)
        a = a_ref[0, sl, :].astype(F32)
        lw = lw_ref[0, sl, :]
        hi, lo = _split_bf16(lw)
        cum = _dot(tri, hi) + _dot(tri, lo)
        c_end = cum[L - 1:L, :]
        p_inv = jnp.exp(-cum)
        p_end = jnp.exp(c_end - cum)
        b = kk * a
        return dict(
            x_a=stack(-(kk * jnp.exp(cum - lw))).astype(BF16),
            x_r=stack(r * jnp.exp(cum)),
            x_bk=jnp.concatenate([stack(b * p_inv), stack(k * p_inv)], axis=0).astype(BF16),
            v_st=stack(v).astype(BF16),
            z_hat=jnp.concatenate([stack(b * p_end), stack(k * p_end)], axis=0).astype(BF16),
            p_row=jnp.exp(c_end),
            rkr=(r * k * rk).astype(BF16), v=v)

    ops = [operands(c) for c in chunks]
    big = [_dot_nt(jnp.concatenate([o["x_a"], o["x_r"].astype(BF16)], axis=0), o["x_bk"])
           for o in ops]
    n_pow = [jnp.where(strict, m[0:L2, 0:L2], 0.0) for m in big]
    a_ak = [jnp.where(strict, m[0:L2, L2:2 * L2], 0.0).astype(BF16) for m in big]
    a_r = [jnp.concatenate([jnp.where(incl, m[L2:2 * L2, 0:L2], 0.0),
                            jnp.where(incl, m[L2:2 * L2, L2:2 * L2], 0.0)], axis=1).astype(BF16)
           for m in big]

    t_inv = [eye + n for n in n_pow]
    for _ in range(5):
        n_pow = each(lambda n: _dot(n.astype(BF16), n.astype(BF16)), n_pow)
        t_inv = each(lambda t, n: t + _dot(t.astype(BF16), n.astype(BF16)), t_inv, n_pow)

    akv = each(lambda m, o: _dot(m, o["v_st"]).astype(BF16), a_ak, ops)
    w = each(lambda t, o, x: _dot(t.astype(BF16), jnp.concatenate([o["x_a"], x], axis=1)),
             t_inv, ops, akv)
    w1 = [m[:, 0:LANES].astype(BF16) for m in w]
    w2v = each(lambda m, o: jnp.concatenate([m[:, LANES:].astype(BF16), o["v_st"]], axis=0), w, ops)
    g_mat = each(lambda o, m, x: (o["x_r"] + _dot(m[:, 0:L2], x)).astype(BF16), ops, a_r, w1)
    y0 = each(_dot, a_r, w2v)
    m_mat = each(lambda x, o: _dot_tn(x, o["z_hat"][0:L2, :]).astype(BF16), w1, ops)
    c2 = each(lambda x, o: _dot_tn(x, o["z_hat"]), w2v, ops)
    bonus_v = [_dot(o["rkr"], head_ones) * o["v"] for o in ops]

    s = s_scr[...]
    y = []
    for c in chunks:
        s_b = s.astype(BF16)
        y_st = _dot_nt(g_mat[c], s_b) + y0[c]
        s = s * ops[c]["p_row"] + _dot(s_b, m_mat[c]) + c2[c]
        y.append(y_st[0:L, :] + y_st[L:L2, :])
    s_scr[...] = s

    mean = [_dot(t.astype(BF16), head_ones) * (1.0 / RWKV_HEAD) for t in y]
    yc = each(lambda t, m: t - m, y, mean)
    var = [_dot((t * t).astype(BF16), head_ones) * (1.0 / RWKV_HEAD) for t in yc]
    for c in chunks:
        yn = yc[c] * lax.rsqrt(var[c] + GN_EPS) * gnw + gnb
        sl = rows_of(c)
        o_ref[0, sl, :] = ((yn + bonus_v[c]) * g_ref[0, sl, :].astype(F32)).astype(o_ref.dtype)


def _rwkv_scan(r, k, v, kk, a, lw, g, gn_w, gn_b, r_k, rows):
    B, S, W = r.shape
    npair = W // LANES
    seq = lambda: pl.BlockSpec((1, rows, LANES), lambda b, p, c: (b, c, p))
    par = lambda: pl.BlockSpec((1, LANES), lambda b, p, c: (0, p))
    return pl.pallas_call(
        functools.partial(_rwkv_scan_kernel, rows=rows),
        grid=(B, npair, S // rows),
        in_specs=[seq() for _ in range(7)] + [par(), par(), par()],
        out_specs=seq(),
        out_shape=jax.ShapeDtypeStruct((B, S, W), BF16),
        scratch_shapes=[pltpu.VMEM((LANES, LANES), F32)],
        compiler_params=_cparams(("parallel", "parallel", "arbitrary")),
        name="rwkv_scan",
    )(r, k, v, kk, a, lw, g, gn_w, gn_b, r_k)


def _rope128(x, cos, sin_signed):
    lane = lax.broadcasted_iota(I32, x.shape, 1)
    half = QK_ROPE // 2
    partner = jnp.where(lane < half, pltpu.roll(x, LANES - half, 1), pltpu.roll(x, half, 1))
    return x * cos + partner * sin_signed


def _mla_prep_kernel(cq_ref, ckv_ref, kr_ref, pos_ref, invf_ref, gqa_ref, gkva_ref,
                     wuq_ref, wukv_ref, gqn_ref, gkn_ref,
                     q_out, k_out, v_out, cqn_scr, ckvn_scr, cos_scr, sin_scr):
    @pl.when(pl.program_id(1) == 0)
    def _():
        cqn_scr[...] = _rms(cq_ref[...].astype(F32), gqa_ref[...]).astype(BF16)
        ckvn_scr[...] = _rms(ckv_ref[...].astype(F32), gkva_ref[...]).astype(BF16)
        ang = pos_ref[...].astype(F32) * invf_ref[...]
        lane = lax.broadcasted_iota(I32, ang.shape, 1)
        cos_scr[...] = jnp.cos(ang)
        sin_scr[...] = jnp.where(lane < QK_ROPE // 2, -1.0, 1.0) * jnp.sin(ang)

    cos = cos_scr[...]
    sin = sin_scr[...]
    inv_n = 1.0 / QK_HEAD
    q = _dot(cqn_scr[...], wuq_ref[...])
    rs = lax.rsqrt(jnp.sum(q * q, axis=-1, keepdims=True) * inv_n + NORM_EPS)
    q = q * rs * gqn_ref[...] * (QK_HEAD ** -0.5 * LOG2E)
    q_out[...] = jnp.concatenate(
        [q[:, 0:LANES], _rope128(q[:, LANES:], cos, sin)], axis=1).astype(q_out.dtype)

    kv = _dot(ckvn_scr[...], wukv_ref[...])
    k_nope = kv[:, 0:QK_NOPE]
    kr = kr_ref[...].astype(F32)
    ssq = (jnp.sum(k_nope * k_nope, axis=-1, keepdims=True)
           + jnp.sum(kr * kr, axis=-1, keepdims=True))
    rs = lax.rsqrt(ssq * inv_n + NORM_EPS)
    gkn = gkn_ref[...]
    k_out[...] = jnp.concatenate(
        [k_nope * rs * gkn[:, 0:LANES], _rope128(kr * rs * gkn[:, LANES:], cos, sin)],
        axis=1).astype(k_out.dtype)
    v_out[...] = jnp.transpose(kv[:, QK_NOPE:]).astype(v_out.dtype)


def _mla_prep(proj, pos, invf, g_qa, g_kva, wuq, wukv, gqn, gkn, tm):
    T = proj.shape[0]
    H = MLA_HEADS
    c0 = COL_MLA // Q_LORA
    ckr = (COL_MLA + Q_LORA + KV_LORA) // LANES
    full = lambda shape: pl.BlockSpec(shape, lambda i, h: (0, 0))
    return pl.pallas_call(
        _mla_prep_kernel,
        grid=(T // tm, H),
        in_specs=[
            pl.BlockSpec((tm, Q_LORA), lambda i, h: (i, c0)),
            pl.BlockSpec((tm, KV_LORA), lambda i, h: (i, c0 + 1)),
            pl.BlockSpec((tm, LANES), lambda i, h: (i, ckr)),
            pl.BlockSpec((tm, 1), lambda i, h: (i, 0)),
            full((1, LANES)), full((1, Q_LORA)), full((1, KV_LORA)),
            pl.BlockSpec((Q_LORA, QK_PAD), lambda i, h: (0, h)),
            pl.BlockSpec((KV_LORA, QK_NOPE + V_HEAD), lambda i, h: (0, h)),
            full((1, QK_PAD)), full((1, QK_PAD)),
        ],
        out_specs=[
            pl.BlockSpec((tm, QK_PAD), lambda i, h: (i, h)),
            pl.BlockSpec((tm, QK_PAD), lambda i, h: (i, h)),
            pl.BlockSpec((V_HEAD, tm), lambda i, h: (h, i)),
        ],
        out_shape=[jax.ShapeDtypeStruct((T, H * QK_PAD), BF16),
                   jax.ShapeDtypeStruct((T, H * QK_PAD), BF16),
                   jax.ShapeDtypeStruct((H * V_HEAD, T), BF16)],
        scratch_shapes=[pltpu.VMEM((tm, Q_LORA), BF16), pltpu.VMEM((tm, KV_LORA), BF16),
                        pltpu.VMEM((tm, LANES), F32), pltpu.VMEM((tm, LANES), F32)],
        compiler_params=_cparams(("parallel", "arbitrary")),
        name="mla_prep",
    )(proj, proj, proj, pos, invf, g_qa, g_kva, wuq, wukv, gqn, gkn)


MASK_VALUE = -1e30
LOG2E = 1.4426950408889634


def _flash_kernel(q_ref, k_ref, vt_ref, o_ref, s_scr, p_scr, alpha_scr, m_scr, l_scr, acc_scr,
                  *, tq):
    i = pl.program_id(2)
    q = q_ref[0]
    m_scr[...] = jnp.full_like(m_scr, MASK_VALUE)
    l_scr[...] = jnp.zeros_like(l_scr)
    acc_scr[...] = jnp.zeros_like(acc_scr)

    def scores(j, slot):
        ks = pl.ds(pl.multiple_of(j * tq, tq), tq)
        s_scr[slot] = _dot_nt(k_ref[0, ks, :], q)

    def softmax(slot, masked):
        for c in range(tq // LANES):
            cols = slice(c * LANES, (c + 1) * LANES)
            s = s_scr[slot, :, cols]
            if masked:
                kpos = lax.broadcasted_iota(I32, s.shape, 0)
                qpos = lax.broadcasted_iota(I32, s.shape, 1) + c * LANES
                s = jnp.where(kpos <= qpos, s, MASK_VALUE)
            m_old = m_scr[:, cols]
            m_new = jnp.maximum(m_old, jnp.max(s, axis=0, keepdims=True))
            p = jnp.exp2(s - m_new)
            alpha = jnp.exp2(m_old - m_new)
            l_scr[:, cols] = alpha * l_scr[:, cols] + jnp.sum(p, axis=0, keepdims=True)
            m_scr[:, cols] = m_new
            p_scr[slot, :, cols] = p.astype(BF16)
            alpha_scr[slot, :, cols] = alpha

    def weighted_values(j, slot):
        vt = vt_ref[:, pl.ds(pl.multiple_of(j * tq, tq), tq)]
        acc_scr[...] = alpha_scr[slot] * acc_scr[...] + _dot(vt, p_scr[slot])

    scores(0, 0)
    p_scr[1] = jnp.zeros((tq, tq), BF16)
    alpha_scr[1] = jnp.ones((1, tq), F32)

    def step(j, slot):
        scores(j + 1, 1 - slot)
        weighted_values(jnp.maximum(j - 1, 0), 1 - slot)
        softmax(slot, False)

    def pair(jp, carry):
        step(2 * jp, 0)
        step(2 * jp + 1, 1)
        return carry

    def finish(slot):
        weighted_values(jnp.maximum(i - 1, 0), 1 - slot)
        softmax(slot, True)
        weighted_values(i, slot)
        o_ref[0] = jnp.transpose(acc_scr[...] / l_scr[...]).astype(o_ref.dtype)

    lax.fori_loop(0, i // 2, pair, 0)

    @pl.when(i % 2 == 0)
    def _():
        finish(0)

    @pl.when(i % 2 == 1)
    def _():
        step(i - 1, 0)
        finish(1)


def _flash(q, k, vt, tq):
    B, S, _ = q.shape
    H = MLA_HEADS
    return pl.pallas_call(
        functools.partial(_flash_kernel, tq=tq),
        grid=(B, H, S // tq),
        in_specs=[
            pl.BlockSpec((1, tq, QK_PAD), lambda b, h, i: (b, i, h)),
            pl.BlockSpec((1, S, QK_PAD), lambda b, h, i: (b, 0, h)),
            pl.BlockSpec((V_HEAD, S), lambda b, h, i: (h, b)),
        ],
        out_specs=pl.BlockSpec((1, tq, V_HEAD), lambda b, h, i: (b, i, h)),
        out_shape=jax.ShapeDtypeStruct((B, S, H * V_HEAD), BF16),
        scratch_shapes=[pltpu.VMEM((2, tq, tq), F32), pltpu.VMEM((2, tq, tq), BF16),
                        pltpu.VMEM((2, 1, tq), F32),
                        pltpu.VMEM((1, tq), F32), pltpu.VMEM((1, tq), F32),
                        pltpu.VMEM((V_HEAD, tq), F32)],
        compiler_params=_cparams(("parallel", "parallel", "arbitrary")),
        name="flash_attn",
    )(q, k, vt)


def _merge_kernel(oa_ref, ob_ref, wa_ref, wb_ref, ga_ref, gb_ref, o_ref):
    ya = _dot(oa_ref[...], wa_ref[...])
    yb = _dot(ob_ref[...], wb_ref[...])
    ga = _sigmoid(ga_ref[...].astype(F32))
    gb = _sigmoid(gb_ref[...].astype(F32))
    o_ref[...] = (ga * ya + gb * yb).astype(o_ref.dtype)


def _merge(oa, ob, wa, wb, proj, tm, tn):
    T, K = oa.shape
    N = wa.shape[1]
    ca = COL_GA // tn
    cb = COL_GB // tn
    return pl.pallas_call(
        _merge_kernel,
        grid=(T // tm, N // tn),
        in_specs=[
            pl.BlockSpec((tm, K), lambda i, j: (i, 0)),
            pl.BlockSpec((tm, K), lambda i, j: (i, 0)),
            pl.BlockSpec((K, tn), lambda i, j: (0, j)),
            pl.BlockSpec((K, tn), lambda i, j: (0, j)),
            pl.BlockSpec((tm, tn), lambda i, j: (i, ca + j)),
            pl.BlockSpec((tm, tn), lambda i, j: (i, cb + j)),
        ],
        out_specs=pl.BlockSpec((tm, tn), lambda i, j: (i, j)),
        out_shape=jax.ShapeDtypeStruct((T, N), BF16),
        compiler_params=_cparams(("parallel", "parallel")),
        name="merge",
    )(oa, ob, wa, wb, proj, proj)


def _pack_rows(x, out_ref, n, row0=0):
    for c in range(PACK_ROWS):
        lo = x[:, c * 2 * LANES:c * 2 * LANES + LANES]
        hi = x[:, c * 2 * LANES + LANES:(c + 1) * 2 * LANES]
        lo_b = pltpu.bitcast(lo.astype(BF16).astype(F32), U32)
        hi_b = pltpu.bitcast(hi.astype(BF16).astype(F32), U32)
        out_ref[pl.ds(row0 * PACK_ROWS + c, n, stride=PACK_ROWS), :] = hi_b | (lo_b >> 16)


def _unpack_rows(ref, c, tm, lead=None):
    idx = (pl.ds(c, tm, stride=PACK_ROWS), slice(None))
    if lead is not None:
        idx = (lead,) + idx
    u = ref[idx]
    lo = pltpu.bitcast(u << 16, F32)
    hi = pltpu.bitcast(u & jnp.uint32(0xFFFF0000), F32)
    return lo, hi


def _wo_kernel(m_ref, x_ref, wo_ref, g_ref, wr_ref, x1_ref, h2_ref, h2p_ref, lg_ref, *, tm):
    hw = WO_ROWS
    groups = [slice(h * hw, (h + 1) * hw) for h in range(tm // hw)]
    wo = wo_ref[...]
    x1 = [x_ref[r, :] + _dot(m_ref[r, :], wo) for r in groups]
    h2 = [_rms(v, g_ref[...]) for v in x1]
    w_hi, w_lo = _split_bf16(wr_ref[...])
    for h, r in enumerate(groups):
        x1_ref[r, :] = x1[h]
        h2_ref[r, :] = h2[h].astype(BF16)
        _pack_rows(h2[h], h2p_ref, hw, row0=h * hw)
        h_hi, h_lo = _split_bf16(h2[h])
        lg_ref[:, r] = _dot_nt(w_hi, h_hi) + _dot_nt(w_hi, h_lo) + _dot_nt(w_lo, h_hi)


def _wo(merged, x2, wo, g, wr_t, tm):
    T, D = x2.shape
    E = wr_t.shape[0]
    return pl.pallas_call(
        functools.partial(_wo_kernel, tm=tm),
        grid=(T // tm,),
        in_specs=[
            pl.BlockSpec((tm, D), lambda i: (i, 0)),
            pl.BlockSpec((tm, D), lambda i: (i, 0)),
            pl.BlockSpec((D, D), lambda i: (0, 0), pipeline_mode=pl.Buffered(1)),
            pl.BlockSpec((1, D), lambda i: (0, 0)),
            pl.BlockSpec((E, D), lambda i: (0, 0)),
        ],
        out_specs=[
            pl.BlockSpec((tm, D), lambda i: (i, 0)),
            pl.BlockSpec((tm, D), lambda i: (i, 0)),
            pl.BlockSpec((tm * PACK_ROWS, PACK_W), lambda i: (i, 0)),
            pl.BlockSpec((E, tm), lambda i: (0, i)),
        ],
        out_shape=[jax.ShapeDtypeStruct((T, D), F32),
                   jax.ShapeDtypeStruct((T, D), BF16),
                   jax.ShapeDtypeStruct((T * PACK_ROWS, PACK_W), U32),
                   jax.ShapeDtypeStruct((E, T), F32)],
        compiler_params=_cparams(("parallel",)),
        name="wo_norm_router",
    )(merged, x2, wo, g, wr_t)


def _route_kernel(lg_ref, bias_ref, eidx_ref, wts_ref, rnk_ref, cnt_ref, carry_scr, *, tm):
    E = N_EXPERTS
    NEG = -jnp.inf

    @pl.when(pl.program_id(0) == 0)
    def _():
        carry_scr[...] = jnp.zeros_like(carry_scr)

    scores = _sigmoid(lg_ref[...])
    choice = scores + bias_ref[...]
    c3 = choice.reshape(N_GROUPS, GROUP_SIZE, tm)
    sub = lax.broadcasted_iota(I32, c3.shape, 1)
    m1 = jnp.max(c3, axis=1, keepdims=True)
    i1 = jnp.min(jnp.where(c3 == m1, sub, GROUP_SIZE), axis=1, keepdims=True)
    m2 = jnp.max(jnp.where(sub == i1, NEG, c3), axis=1, keepdims=True)
    gs = (m1 + m2).reshape(N_GROUPS, tm)
    gi = lax.broadcasted_iota(I32, gs.shape, 0)
    gsel = jnp.zeros(gs.shape, F32)
    for _ in range(TOPK_GROUPS):
        mx = jnp.max(gs, axis=0, keepdims=True)
        ix = jnp.min(jnp.where(gs == mx, gi, N_GROUPS), axis=0, keepdims=True)
        hit = gi == ix
        gsel = jnp.where(hit, 1.0, gsel)
        gs = jnp.where(hit, NEG, gs)
    emask = jnp.broadcast_to(gsel.reshape(N_GROUPS, 1, tm), (N_GROUPS, GROUP_SIZE, tm)).reshape(E, tm)
    x = jnp.where(emask > 0.5, choice, NEG)
    ei = lax.broadcasted_iota(I32, x.shape, 0)
    sel = jnp.zeros(x.shape, F32)
    idx_rows, w_rows = [], []
    for _ in range(TOP_K):
        mx = jnp.max(x, axis=0, keepdims=True)
        ix = jnp.min(jnp.where(x == mx, ei, E), axis=0, keepdims=True)
        hit = ei == ix
        w_rows.append(jnp.sum(jnp.where(hit, scores, 0.0), axis=0, keepdims=True))
        idx_rows.append(ix)
        sel = jnp.where(hit, 1.0, sel)
        x = jnp.where(hit, NEG, x)
    w = jnp.concatenate(w_rows, axis=0)
    w = w / (jnp.sum(w, axis=0, keepdims=True) + 1e-20) * ROUTED_SCALE
    eidx_ref[...] = jnp.concatenate(idx_rows, axis=0)
    wts_ref[...] = w

    upper = (lax.broadcasted_iota(I32, (tm, tm), 0) < lax.broadcasted_iota(I32, (tm, tm), 1)).astype(BF16)
    base = carry_scr[...][:, 0:1]
    excl = _dot(sel.astype(BF16), upper) + base
    rnk_ref[...] = jnp.concatenate(
        [jnp.sum(jnp.where(ei == ix, excl, 0.0), axis=0, keepdims=True) for ix in idx_rows],
        axis=0).astype(I32)
    carry_scr[...] = carry_scr[...] + jnp.sum(sel, axis=1, keepdims=True)
    cnt_ref[...] = carry_scr[...]


def _route(logits_t, bias, tm):
    E, T = logits_t.shape
    K = TOP_K
    tok = lambda: pl.BlockSpec((K, tm), lambda i: (0, i))
    return pl.pallas_call(
        functools.partial(_route_kernel, tm=tm),
        grid=(T // tm,),
        in_specs=[pl.BlockSpec((E, tm), lambda i: (0, i)),
                  pl.BlockSpec((E, 1), lambda i: (0, 0))],
        out_specs=[tok(), tok(), tok(), pl.BlockSpec((E, LANES), lambda i: (0, 0))],
        out_shape=[jax.ShapeDtypeStruct((K, T), I32), jax.ShapeDtypeStruct((K, T), F32),
                   jax.ShapeDtypeStruct((K, T), I32), jax.ShapeDtypeStruct((E, LANES), F32)],
        scratch_shapes=[pltpu.VMEM((E, LANES), F32)],
        compiler_params=_cparams(("arbitrary",)),
        name="route",
    )(logits_t, bias)


def _slots_kernel(pst_ref, eidx_ref, rnk_ref, dest_ref):
    eidx = eidx_ref[...]
    dest = rnk_ref[...]
    for e in range(N_EXPERTS):
        dest = dest + jnp.where(eidx == e, pst_ref[e], 0)
    dest_ref[...] = dest


def _slots(eidx, rnk, pstart, tm):
    K, T = eidx.shape
    tok = lambda: pl.BlockSpec((K, tm), lambda i: (0, i))
    return pl.pallas_call(
        _slots_kernel,
        grid=(T // tm,),
        in_specs=[pl.BlockSpec(memory_space=pltpu.SMEM), tok(), tok()],
        out_specs=tok(),
        out_shape=jax.ShapeDtypeStruct((K, T), I32),
        compiler_params=_cparams(("parallel",)),
        name="slots",
    )(pstart, eidx, rnk)


def _dispatch_kernel(dest_ref, h_ref, xs_in, xs_out, sem, *, tm):
    del xs_in

    def row_copy(t, dest):
        return pltpu.make_async_copy(
            h_ref.at[pl.ds(pl.multiple_of(t * PACK_ROWS, PACK_ROWS), PACK_ROWS), :],
            xs_out.at[pl.ds(pl.multiple_of(dest * PACK_ROWS, PACK_ROWS), PACK_ROWS), :],
            sem)

    def issue(t, carry):
        for k in range(TOP_K):
            row_copy(t, dest_ref[k, t]).start()
        return carry

    lax.fori_loop(0, tm, issue, 0)

    for k in range(TOP_K):
        pltpu.make_async_copy(h_ref, xs_out.at[pl.ds(0, tm * PACK_ROWS), :], sem).wait()


def _dispatch(dest, h2p, xs_zero, tm):
    K, T = dest.shape
    return pl.pallas_call(
        functools.partial(_dispatch_kernel, tm=tm),
        grid=(T // tm,),
        in_specs=[pl.BlockSpec((K, tm), lambda i: (0, i), memory_space=pltpu.SMEM),
                  pl.BlockSpec((tm * PACK_ROWS, PACK_W), lambda i: (i, 0)),
                  pl.BlockSpec(memory_space=pl.ANY)],
        out_specs=pl.BlockSpec(memory_space=pl.ANY),
        out_shape=jax.ShapeDtypeStruct(xs_zero.shape, xs_zero.dtype),
        scratch_shapes=[pltpu.SemaphoreType.DMA(())],
        input_output_aliases={2: 0},
        compiler_params=_cparams(("arbitrary",)),
        name="dispatch",
    )(dest, h2p, xs_zero)


def _ffn_kernel(te_ref, nu_ref, xs_ref, wg_ref, wu_ref, wd_ref, ys_ref, wg_b, wu_b, wd_b, *, tm):
    i = pl.program_id(0)
    used = i < nu_ref[0]

    @pl.when(used & ((i == 0) | (te_ref[i] != te_ref[jnp.maximum(i - 1, 0)])))
    def _():
        wg_b[...] = wg_ref[0].astype(BF16)
        wu_b[...] = wu_ref[0].astype(BF16)
        wd_b[...] = wd_ref[0].astype(BF16)

    @pl.when(used)
    def _():
        parts = []
        for c in range(PACK_ROWS):
            lo, hi = _unpack_rows(xs_ref, c, tm)
            parts += [lo.astype(BF16), hi.astype(BF16)]
        a = jnp.concatenate(parts, axis=1)
        hg = _dot(a, wg_b[...])
        hu = _dot(a, wu_b[...])
        hid = (hg * _sigmoid(hg) * hu).astype(BF16)
        _pack_rows(_dot(hid, wd_b[...]), ys_ref, tm)

    @pl.when(jnp.logical_not(used))
    def _():
        ys_ref[...] = jnp.zeros_like(ys_ref)


def _ffn(tile_e, n_used, xs, wg, wu, wd, tm):
    P8 = xs.shape[0]
    nt = P8 // (tm * PACK_ROWS)
    D, I = wg.shape[1], wg.shape[2]
    grid_spec = pltpu.PrefetchScalarGridSpec(
        num_scalar_prefetch=2,
        grid=(nt,),
        in_specs=[
            pl.BlockSpec((tm * PACK_ROWS, PACK_W), lambda i, te, nu: (i, 0)),
            pl.BlockSpec((1, D, I), lambda i, te, nu: (te[i], 0, 0)),
            pl.BlockSpec((1, D, I), lambda i, te, nu: (te[i], 0, 0)),
            pl.BlockSpec((1, I, D), lambda i, te, nu: (te[i], 0, 0)),
        ],
        out_specs=pl.BlockSpec((tm * PACK_ROWS, PACK_W), lambda i, te, nu: (i, 0)),
        scratch_shapes=[pltpu.VMEM((D, I), BF16), pltpu.VMEM((D, I), BF16),
                        pltpu.VMEM((I, D), BF16)],
    )
    return pl.pallas_call(
        functools.partial(_ffn_kernel, tm=tm),
        grid_spec=grid_spec,
        out_shape=jax.ShapeDtypeStruct(xs.shape, U32),
        compiler_params=_cparams(("arbitrary",)),
        name="expert_ffn",
    )(tile_e, n_used, xs, wg, wu, wd)


def _combine_kernel(dest_ref, ys_hbm, wt_ref, x1_ref, h2_ref,
                    wsg_ref, wsu_ref, wsd_ref, o_ref, buf, sem, *, tm, n_tiles):
    s = pl.program_id(0)

    def row_gather(slot, k, t, dest):
        return pltpu.make_async_copy(
            ys_hbm.at[pl.ds(pl.multiple_of(dest * PACK_ROWS, PACK_ROWS), PACK_ROWS), :],
            buf.at[slot, k, pl.ds(t * PACK_ROWS, PACK_ROWS), :],
            sem.at[slot])

    def issue(slot):
        for t in range(tm):
            for k in range(TOP_K):
                row_gather(slot, k, t, dest_ref[k, t]).start()

    def drain(slot):
        for k in range(TOP_K):
            pltpu.make_async_copy(ys_hbm.at[pl.ds(0, tm * PACK_ROWS), :], buf.at[slot, k],
                                  sem.at[slot]).wait()

    def reduce(slot):
        h2 = h2_ref[...]
        hg = _dot(h2, wsg_ref[...])
        hu = _dot(h2, wsu_ref[...])
        base = x1_ref[...] + _dot((hg * _sigmoid(hg) * hu).astype(BF16), wsd_ref[...])
        wt = wt_ref[...]
        for c in range(PACK_ROWS):
            acc_lo = base[:, c * 2 * LANES:c * 2 * LANES + LANES]
            acc_hi = base[:, c * 2 * LANES + LANES:(c + 1) * 2 * LANES]
            for k in range(TOP_K):
                lo, hi = _unpack_rows(buf.at[slot], c, tm, lead=k)
                wk = wt[:, k:k + 1]
                acc_lo = acc_lo + wk * lo
                acc_hi = acc_hi + wk * hi
            o_ref[:, c * 2 * LANES:c * 2 * LANES + LANES] = acc_lo
            o_ref[:, c * 2 * LANES + LANES:(c + 1) * 2 * LANES] = acc_hi

    for slot in range(2):
        parity = s % 2 == slot

        @pl.when(parity & (s == 0))
        def _():
            issue(slot)

        @pl.when(parity & (s > 0) & (s < n_tiles))
        def _():
            drain(1 - slot)
            issue(slot)
            reduce(1 - slot)

        @pl.when(parity & (s == n_tiles))
        def _():
            drain(1 - slot)
            reduce(1 - slot)


def _combine(dest, ys, wts_t, x1, h2, wsg, wsu, wsd, tm):
    K, T = dest.shape
    D = x1.shape[1]
    I = wsg.shape[1]
    n = T // tm
    nxt = lambda s: jnp.minimum(s, n - 1)
    cur = lambda s: jnp.maximum(s - 1, 0)
    return pl.pallas_call(
        functools.partial(_combine_kernel, tm=tm, n_tiles=n),
        grid=(n + 1,),
        in_specs=[pl.BlockSpec((K, tm), lambda s: (0, nxt(s)), memory_space=pltpu.SMEM),
                  pl.BlockSpec(memory_space=pl.ANY),
                  pl.BlockSpec((tm, K), lambda s: (cur(s), 0)),
                  pl.BlockSpec((tm, D), lambda s: (cur(s), 0)),
                  pl.BlockSpec((tm, D), lambda s: (cur(s), 0)),
                  pl.BlockSpec((D, I), lambda s: (0, 0)),
                  pl.BlockSpec((D, I), lambda s: (0, 0)),
                  pl.BlockSpec((I, D), lambda s: (0, 0))],
        out_specs=pl.BlockSpec((tm, D), lambda s: (cur(s), 0)),
        out_shape=jax.ShapeDtypeStruct((T, D), F32),
        scratch_shapes=[pltpu.VMEM((2, K, tm * PACK_ROWS, PACK_W), U32),
                        pltpu.SemaphoreType.DMA((2,))],
        compiler_params=_cparams(("arbitrary",)),
        name="combine",
    )(dest, ys, wts_t, x1, h2, wsg, wsu, wsd)


def _ple_kernel(xf_ref, xt_ref, p_ref, wp_ref, gpost_ref, gin_ref, wg_ref, o_ref,
                hn_scr, ple_scr, *, tn):
    j = pl.program_id(1)

    @pl.when(j == 0)
    def _():
        hn_scr[...] = _rms(xf_ref[...], gin_ref[...]).astype(BF16)
        ple_scr[...] = _rms(_dot(p_ref[...].astype(BF16), wp_ref[...]), gpost_ref[...])

    cols = pl.ds(pl.multiple_of(j * tn, tn), tn)
    gate = _sigmoid(_dot(hn_scr[...], wg_ref[...]))
    o_ref[...] = xt_ref[...] + gate * ple_scr[:, cols]


def _ple(x2, p2, wp, gpost, gin, wg, tm, tn):
    T, D = x2.shape
    Pd = p2.shape[1]
    return pl.pallas_call(
        functools.partial(_ple_kernel, tn=tn),
        grid=(T // tm, D // tn),
        in_specs=[
            pl.BlockSpec((tm, D), lambda i, j: (i, 0)),
            pl.BlockSpec((tm, tn), lambda i, j: (i, j)),
            pl.BlockSpec((tm, Pd), lambda i, j: (i, 0)),
            pl.BlockSpec((Pd, D), lambda i, j: (0, 0)),
            pl.BlockSpec((1, D), lambda i, j: (0, 0)),
            pl.BlockSpec((1, D), lambda i, j: (0, 0)),
            pl.BlockSpec((D, tn), lambda i, j: (0, j)),
        ],
        out_specs=pl.BlockSpec((tm, tn), lambda i, j: (i, j)),
        out_shape=jax.ShapeDtypeStruct((T, D), F32),
        scratch_shapes=[pltpu.VMEM((tm, D), BF16), pltpu.VMEM((tm, D), F32)],
        compiler_params=_cparams(("parallel", "arbitrary")),
        name="ple",
    )(x2, x2, p2, wp, gpost, gin, wg)


def _pad_cols(w, n):
    return jnp.pad(w, ((0, 0), (0, n - w.shape[1])))


def _pad_rows(w, n):
    return jnp.pad(w, ((0, n - w.shape[0]), (0, 0)))


def _layout_w_in(w):
    W = RWKV_WIDTH
    c = 3 * W
    segs = [w[:, 0:c],
            _pad_cols(w[:, c:c + DECAY_LORA], LORA_PAD),
            _pad_cols(w[:, c + DECAY_LORA:c + DECAY_LORA + AAA_LORA], LORA_PAD),
            _pad_cols(w[:, c + DECAY_LORA + AAA_LORA:c + DECAY_LORA + AAA_LORA + GATE_LORA], GATE_LORA_PAD)]
    c += DECAY_LORA + AAA_LORA + GATE_LORA
    mla_cols = Q_LORA + KV_LORA + QK_ROPE
    segs.append(_pad_cols(w[:, c:c + mla_cols], MLA_PAD))
    c += mla_cols
    segs.append(w[:, c:])
    out = jnp.concatenate(segs, axis=1).astype(BF16)
    assert out.shape[1] == IN_PAD
    return out


def _layout_mu(mu):
    W = RWKV_WIDTH
    c = 3 * W
    segs = [mu[0:c],
            jnp.pad(mu[c:c + DECAY_LORA], (0, LORA_PAD - DECAY_LORA)),
            jnp.pad(mu[c + DECAY_LORA:c + DECAY_LORA + AAA_LORA], (0, LORA_PAD - AAA_LORA)),
            jnp.pad(mu[c + DECAY_LORA + AAA_LORA:], (0, GATE_LORA_PAD - GATE_LORA))]
    return jnp.concatenate(segs).reshape(1, RWKV_PAD)


def _layer(x, p, positions, g_mix, w_in, mu_rwkv, w0, w2, a0, a2, g2, k_k, k_a, r_k,
           gn_w, gn_b, w_a_up, g_qa, g_kva, w_uq, w_ukv, g_qn, g_kn, w_b_up, w_o,
           g_ffn, w_router, router_bias, w_exp_gate, w_exp_up, w_exp_down,
           w_sh_gate, w_sh_up, w_sh_down, w_ple, g_ple_post, g_ple_in, w_ple_gate,
           *, tiles):
    B, S, D = x.shape
    T = B * S
    W = RWKV_WIDTH
    x2 = x.reshape(T, D)
    row = lambda v: v.reshape(1, -1).astype(F32)

    proj = _inproj(x2, row(g_mix), _layout_w_in(w_in), tiles["tm_in"], tiles["tn_in"])

    head_blk = (jnp.arange(W)[:, None] // RWKV_HEAD == jnp.arange(W)[None, :] // RWKV_HEAD).astype(BF16)
    r, k, v, kk, a, lw, g = _rwkv_prep(
        proj, _layout_mu(mu_rwkv), row(w0), _pad_rows(w2, LORA_PAD).astype(BF16), row(a0),
        _pad_rows(a2, LORA_PAD).astype(BF16), _pad_rows(g2, GATE_LORA_PAD).astype(BF16),
        row(k_k), row(k_a), head_blk, tiles["tm_prep"], S)
    b3 = lambda t: t.reshape(B, S, W)
    o_a = _rwkv_scan(b3(r), b3(k), b3(v), b3(kk), b3(a), b3(lw), b3(g),
                     row(gn_w), row(gn_b), row(r_k), tiles["rows_scan"]).reshape(T, W)

    half = QK_ROPE // 2
    inv_freq = ROPE_THETA ** (-jnp.arange(half, dtype=F32) / half)
    invf = jnp.concatenate([inv_freq, inv_freq, jnp.zeros((LANES - QK_ROPE,), F32)]).reshape(1, LANES)
    wuq = jnp.pad(w_uq.reshape(Q_LORA, MLA_HEADS, QK_HEAD),
                  ((0, 0), (0, 0), (0, QK_PAD - QK_HEAD))).reshape(Q_LORA, MLA_HEADS * QK_PAD)
    pad_g = lambda gv: jnp.pad(gv, (0, QK_PAD - QK_HEAD)).reshape(1, QK_PAD)
    q, kx, vx = _mla_prep(proj, positions.reshape(T, 1).astype(I32), invf, row(g_qa), row(g_kva),
                          wuq.astype(BF16), w_ukv.astype(BF16), pad_g(g_qn), pad_g(g_kn),
                          tiles["tm_mla"])
    o_b = _flash(q.reshape(B, S, -1), kx.reshape(B, S, -1), vx, tiles["tq"]).reshape(T, MLA_WIDTH)

    merged = _merge(o_a, o_b, w_a_up.astype(BF16), w_b_up.astype(BF16), proj,
                    tiles["tm_merge"], tiles["tn_merge"])

    x1, h2, h2p, logits_t = _wo(merged, x2, w_o.astype(BF16), row(g_ffn),
                                jnp.transpose(w_router).astype(F32), tiles["tm_wo"])
    eidx, wts, rnk, cnt = _route(logits_t, router_bias.reshape(N_EXPERTS, 1).astype(F32),
                                 tiles["tm_route"])
    tmf = tiles["tm_ffn"]
    counts = cnt[:, 0].astype(I32)
    pcounts = (counts + tmf - 1) // tmf * tmf
    pends = jnp.cumsum(pcounts)
    pstart = (pends - pcounts).astype(I32)
    n_tiles = (T * TOP_K) // tmf + N_EXPERTS
    tile_start = jnp.arange(n_tiles, dtype=I32) * tmf
    tile_e = jnp.minimum(jnp.sum((pends[None, :] <= tile_start[:, None]).astype(I32), axis=1),
                         N_EXPERTS - 1)
    dest = _slots(eidx, rnk, pstart, tiles["tm_slots"])
    xs = _dispatch(dest, h2p, jnp.zeros((n_tiles * tmf * PACK_ROWS, PACK_W), U32), tiles["tm_disp"])
    n_used = (pends[N_EXPERTS - 1:] // tmf).astype(I32)
    ys = _ffn(tile_e, n_used, xs, w_exp_gate, w_exp_up, w_exp_down, tmf)
    x3 = _combine(dest, ys, jnp.transpose(wts), x1, h2,
                  w_sh_gate.astype(BF16), w_sh_up.astype(BF16), w_sh_down.astype(BF16),
                  tiles["tm_comb"])

    out = _ple(x3, p.reshape(T, PLE_DIM), w_ple.astype(BF16), row(g_ple_post), row(g_ple_in),
               w_ple_gate.astype(BF16), tiles["tm_ple"], tiles["tn_ple"])
    return out.reshape(B, S, D)


TILES = dict(tm_in=1024, tn_in=512, tm_prep=256, rows_scan=512, tm_mla=512, tq=512,
             tm_merge=512, tn_merge=512, tm_wo=512, tm_route=512, tm_ffn=256,
             tm_slots=2048, tm_disp=512, tm_comb=128, tm_ple=512, tn_ple=1024)


def kernel(x, p, positions, g_mix, w_in, mu_rwkv, w0, w2, a0, a2, g2, k_k, k_a, r_k, gn_w, gn_b, w_a_up, g_qa, g_kva, w_uq, w_ukv, g_qn, g_kn, w_b_up, w_o, g_ffn, w_router, router_bias, w_exp_gate, w_exp_up, w_exp_down, w_sh_gate, w_sh_up, w_sh_down, w_ple, g_ple_post, g_ple_in, w_ple_gate):
    args = (g_mix, w_in, mu_rwkv, w0, w2, a0, a2, g2, k_k, k_a, r_k, gn_w, gn_b, w_a_up,
            g_qa, g_kva, w_uq, w_ukv, g_qn, g_kn, w_b_up, w_o, g_ffn, w_router, router_bias,
            w_exp_gate, w_exp_up, w_exp_down, w_sh_gate, w_sh_up, w_sh_down, w_ple,
            g_ple_post, g_ple_in, w_ple_gate)
    assert all(t.shape[0] == 1 for t in args), "single-layer stack expected"
    return _layer(x, p[0], positions, *[t[0] for t in args], tiles=TILES)
```

```python
import functools

import jax
import jax.numpy as jnp
from jax import lax
from jax.experimental import pallas as pl
from jax.experimental.pallas import tpu as pltpu

F32 = jnp.float32
BF16 = jnp.bfloat16
I32 = jnp.int32
U32 = jnp.uint32

D_MODEL = 2048
PLE_DIM = 256
NORM_EPS = 1e-6
RWKV_HEADS = 16
RWKV_HEAD = 64
RWKV_WIDTH = RWKV_HEADS * RWKV_HEAD
DECAY_LORA = 64
AAA_LORA = 64
GATE_LORA = 160
GN_EPS = 64e-5
MLA_HEADS = 8
Q_LORA = 512
KV_LORA = 512
QK_NOPE = 128
QK_ROPE = 64
QK_HEAD = QK_NOPE + QK_ROPE
V_HEAD = 128
MLA_WIDTH = MLA_HEADS * V_HEAD
ROPE_THETA = 10000.0
N_EXPERTS = 64
TOP_K = 8
N_GROUPS = 8
GROUP_SIZE = N_EXPERTS // N_GROUPS
TOPK_GROUPS = 4
MOE_INTER = 512
ROUTED_SCALE = 2.5

LANES = 128
QK_PAD = 2 * LANES
PACK_ROWS = 8
PACK_W = D_MODEL // 2 // PACK_ROWS
assert PACK_W == LANES

LORA_PAD = 128
GATE_LORA_PAD = 256
RWKV_PAD = 3 * RWKV_WIDTH + 2 * LORA_PAD + GATE_LORA_PAD
MLA_PAD = 1536
COL_RWKV = 0
COL_MLA = RWKV_PAD
COL_GA = COL_MLA + MLA_PAD
COL_GB = COL_GA + D_MODEL
IN_PAD = COL_GB + D_MODEL

VMEM_LIMIT = 56 * 1024 * 1024
CHUNK = 64
WO_ROWS = 128


def _cparams(sem):
    return pltpu.CompilerParams(dimension_semantics=sem, vmem_limit_bytes=VMEM_LIMIT)


def _dot(a, b):
    return jnp.dot(a, b, preferred_element_type=F32)


def _dot_nt(a, b):
    return lax.dot_general(a, b, (((1,), (1,)), ((), ())), preferred_element_type=F32)


def _dot_tn(a, b):
    return lax.dot_general(a, b, (((0,), (0,)), ((), ())), preferred_element_type=F32)


def _split_bf16(x):
    hi = x.astype(BF16)
    lo = (x - hi.astype(F32)).astype(BF16)
    return hi, lo


def _rms(x, g):
    ms = jnp.mean(x * x, axis=-1, keepdims=True)
    return x * lax.rsqrt(ms + NORM_EPS) * g


def _sigmoid(x):
    return 1.0 / (1.0 + jnp.exp(-x))


def _inproj_kernel(x_ref, g_ref, w_ref, o_ref, h_scr):
    @pl.when(pl.program_id(1) == 0)
    def _():
        h_scr[...] = _rms(x_ref[...], g_ref[...]).astype(BF16)

    o_ref[...] = _dot(h_scr[...], w_ref[...]).astype(o_ref.dtype)


def _inproj(x2, g, w, tm, tn):
    T, D = x2.shape
    N = w.shape[1]
    return pl.pallas_call(
        _inproj_kernel,
        grid=(T // tm, N // tn),
        in_specs=[
            pl.BlockSpec((tm, D), lambda i, j: (i, 0)),
            pl.BlockSpec((1, D), lambda i, j: (0, 0)),
            pl.BlockSpec((D, tn), lambda i, j: (0, j)),
        ],
        out_specs=pl.BlockSpec((tm, tn), lambda i, j: (i, j)),
        out_shape=jax.ShapeDtypeStruct((T, N), BF16),
        scratch_shapes=[pltpu.VMEM((tm, D), BF16)],
        compiler_params=_cparams(("parallel", "arbitrary")),
        name="inproj",
    )(x2, g, w)


HALO = 16


def _rwkv_prep_kernel(cur_ref, halo_ref, mu_ref, w0_ref, w2_ref, a0_ref, a2_ref, g2_ref,
                      kk_ref, ka_ref, bd_ref,
                      r_out, k_out, v_out, kk_out, a_out, lw_out, g_out, *, tm, seq):
    i = pl.program_id(0)
    cur = cur_ref[...].astype(F32)
    first = (i * tm) % seq == 0
    last_prev = halo_ref[HALO - 1:HALO, :].astype(F32)
    last_prev = jnp.where(first, 0.0, last_prev)
    row = lax.broadcasted_iota(I32, cur.shape, 0)
    prev = jnp.where(row == 0, last_prev, pltpu.roll(cur, 1, 0))
    u = cur + (prev - cur) * mu_ref[...]
    W = RWKV_WIDTH
    r = u[:, 0:W]
    k = u[:, W:2 * W]
    v = u[:, 2 * W:3 * W]
    dw = u[:, 3 * W:3 * W + LORA_PAD]
    da = u[:, 3 * W + LORA_PAD:3 * W + 2 * LORA_PAD]
    dg = u[:, 3 * W + 2 * LORA_PAD:3 * W + 2 * LORA_PAD + GATE_LORA_PAD]
    z = w0_ref[...] + _dot(jnp.tanh(dw).astype(BF16), w2_ref[...])
    w_log = -(jnp.maximum(-z, 0.0) + jnp.log(1.0 + jnp.exp(-jnp.abs(z)))) - 0.5
    lw_out[...] = -jnp.exp(w_log)
    a = _sigmoid(a0_ref[...] + _dot(da.astype(BF16), a2_ref[...]))
    g_out[...] = _dot(_sigmoid(dg).astype(BF16), g2_ref[...]).astype(g_out.dtype)
    kk = k * kk_ref[...]
    hi, lo = _split_bf16(kk * kk)
    ssq = _dot(hi, bd_ref[...]) + _dot(lo, bd_ref[...])
    kk = kk / jnp.maximum(jnp.sqrt(ssq), 1e-12)
    r_out[...] = r.astype(r_out.dtype)
    k_out[...] = (k * (1.0 + (a - 1.0) * ka_ref[...])).astype(k_out.dtype)
    v_out[...] = v.astype(v_out.dtype)
    kk_out[...] = kk.astype(kk_out.dtype)
    a_out[...] = a.astype(a_out.dtype)


def _rwkv_prep(proj, mu, w0, w2, a0, a2, g2, k_k, k_a, bd, tm, seq):
    T = proj.shape[0]
    W = RWKV_WIDTH
    full = lambda shape: pl.BlockSpec(shape, lambda i: (0, 0))
    out = lambda: pl.BlockSpec((tm, W), lambda i: (i, 0))
    return pl.pallas_call(
        functools.partial(_rwkv_prep_kernel, tm=tm, seq=seq),
        grid=(T // tm,),
        in_specs=[
            pl.BlockSpec((tm, RWKV_PAD), lambda i: (i, 0)),
            pl.BlockSpec((HALO, RWKV_PAD), lambda i: (jnp.maximum(i * (tm // HALO) - 1, 0), 0)),
            full((1, RWKV_PAD)), full((1, W)), full((LORA_PAD, W)), full((1, W)),
            full((LORA_PAD, W)), full((GATE_LORA_PAD, W)), full((1, W)), full((1, W)),
            full((W, W)),
        ],
        out_specs=[out() for _ in range(7)],
        out_shape=[jax.ShapeDtypeStruct((T, W), dt)
                   for dt in (BF16, BF16, BF16, BF16, BF16, F32, BF16)],
        compiler_params=_cparams(("parallel",)),
        name="rwkv_prep",
    )(proj, proj, mu, w0, w2, a0, a2, g2, k_k, k_a, bd)


def _rwkv_scan_kernel(r_ref, k_ref, v_ref, kk_ref, a_ref, lw_ref, g_ref,
                      gnw_ref, gnb_ref, rk_ref, o_ref, s_scr, *, rows):
    L = CHUNK
    L2 = 2 * L

    @pl.when(pl.program_id(2) == 0)
    def _():
        s_scr[...] = jnp.zeros_like(s_scr)

    lane = lax.broadcasted_iota(I32, (L, LANES), 1)
    m_a = (lane < RWKV_HEAD).astype(F32)
    m_b = 1.0 - m_a
    ri = lax.broadcasted_iota(I32, (L2, L2), 0)
    ci = lax.broadcasted_iota(I32, (L2, L2), 1)
    strict = ri > ci
    incl = ri >= ci
    eye = (ri == ci).astype(F32)
    tri = (lax.broadcasted_iota(I32, (L, L), 0) >= lax.broadcasted_iota(I32, (L, L), 1)).astype(BF16)
    hr = lax.broadcasted_iota(I32, (LANES, LANES), 0) // RWKV_HEAD
    hc = lax.broadcasted_iota(I32, (LANES, LANES), 1) // RWKV_HEAD
    head_ones = (hr == hc).astype(BF16)
    gnw = gnw_ref[...]
    gnb = gnb_ref[...]
    rk = rk_ref[...]

    def stack(x):
        return jnp.concatenate([x * m_a, x * m_b], axis=0)

    chunks = range(rows // L)
    rows_of = lambda c: slice(c * L, (c + 1) * L)
    each = lambda f, *lists: [f(*vals) for vals in zip(*lists)]

    def operands(c):
        sl = rows_of(c)
        r = r_ref[0, sl, :].astype(F32)
        k = k_ref[0, sl, :].astype(F32)
        v = v_ref[0, sl, :].astype(F32)
        kk = kk_ref[0, sl, :].astype(F32)
        a = a_ref[0, sl, :].astype(F32)
        lw = lw_ref[0, sl, :]
        hi, lo = _split_bf16(lw)
        cum = _dot(tri, hi) + _dot(tri, lo)
        c_end = cum[L - 1:L, :]
        p_inv = jnp.exp(-cum)
        p_end = jnp.exp(c_end - cum)
        b = kk * a
        return dict(
            x_a=stack(-(kk * jnp.exp(cum - lw))).astype(BF16),
            x_r=stack(r * jnp.exp(cum)),
            x_bk=jnp.concatenate([stack(b * p_inv), stack(k * p_inv)], axis=0).astype(BF16),
            v_st=stack(v).astype(BF16),
            z_hat=jnp.concatenate([stack(b * p_end), stack(k * p_end)], axis=0).astype(BF16),
            p_row=jnp.exp(c_end),
            rkr=(r * k * rk).astype(BF16), v=v)

    ops = [operands(c) for c in chunks]
    big = [_dot_nt(jnp.concatenate([o["x_a"], o["x_r"].astype(BF16)], axis=0), o["x_bk"])
           for o in ops]
    n_pow = [jnp.where(strict, m[0:L2, 0:L2], 0.0) for m in big]
    a_ak = [jnp.where(strict, m[0:L2, L2:2 * L2], 0.0).astype(BF16) for m in big]
    a_r = [jnp.concatenate([jnp.where(incl, m[L2:2 * L2, 0:L2], 0.0),
                            jnp.where(incl, m[L2:2 * L2, L2:2 * L2], 0.0)], axis=1).astype(BF16)
           for m in big]

    t_inv = [eye + n for n in n_pow]
    for _ in range(5):
        n_pow = each(lambda n: _dot(n.astype(BF16), n.astype(BF16)), n_pow)
        t_inv = each(lambda t, n: t + _dot(t.astype(BF16), n.astype(BF16)), t_inv, n_pow)

    akv = each(lambda m, o: _dot(m, o["v_st"]).astype(BF16), a_ak, ops)
    w = each(lambda t, o, x: _dot(t.astype(BF16), jnp.concatenate([o["x_a"], x], axis=1)),
             t_inv, ops, akv)
    w1 = [m[:, 0:LANES].astype(BF16) for m in w]
    w2v = each(lambda m, o: jnp.concatenate([m[:, LANES:].astype(BF16), o["v_st"]], axis=0), w, ops)
    g_mat = each(lambda o, m, x: (o["x_r"] + _dot(m[:, 0:L2], x)).astype(BF16), ops, a_r, w1)
    y0 = each(_dot, a_r, w2v)
    m_mat = each(lambda x, o: _dot_tn(x, o["z_hat"][0:L2, :]).astype(BF16), w1, ops)
    c2 = each(lambda x, o: _dot_tn(x, o["z_hat"]), w2v, ops)
    bonus_v = [_dot(o["rkr"], head_ones) * o["v"] for o in ops]

    s = s_scr[...]
    y = []
    for c in chunks:
        s_b = s.astype(BF16)
        y_st = _dot_nt(g_mat[c], s_b) + y0[c]
        s = s * ops[c]["p_row"] + _dot(s_b, m_mat[c]) + c2[c]
        y.append(y_st[0:L, :] + y_st[L:L2, :])
    s_scr[...] = s

    mean = [_dot(t.astype(BF16), head_ones) * (1.0 / RWKV_HEAD) for t in y]
    yc = each(lambda t, m: t - m, y, mean)
    var = [_dot((t * t).astype(BF16), head_ones) * (1.0 / RWKV_HEAD) for t in yc]
    for c in chunks:
        yn = yc[c] * lax.rsqrt(var[c] + GN_EPS) * gnw + gnb
        sl = rows_of(c)
        o_ref[0, sl, :] = ((yn + bonus_v[c]) * g_ref[0, sl, :].astype(F32)).astype(o_ref.dtype)


def _rwkv_scan(r, k, v, kk, a, lw, g, gn_w, gn_b, r_k, rows):
    B, S, W = r.shape
    npair = W // LANES
    seq = lambda: pl.BlockSpec((1, rows, LANES), lambda b, p, c: (b, c, p))
    par = lambda: pl.BlockSpec((1, LANES), lambda b, p, c: (0, p))
    return pl.pallas_call(
        functools.partial(_rwkv_scan_kernel, rows=rows),
        grid=(B, npair, S // rows),
        in_specs=[seq() for _ in range(7)] + [par(), par(), par()],
        out_specs=seq(),
        out_shape=jax.ShapeDtypeStruct((B, S, W), BF16),
        scratch_shapes=[pltpu.VMEM((LANES, LANES), F32)],
        compiler_params=_cparams(("parallel", "parallel", "arbitrary")),
        name="rwkv_scan",
    )(r, k, v, kk, a, lw, g, gn_w, gn_b, r_k)


def _rope128(x, cos, sin_signed):
    lane = lax.broadcasted_iota(I32, x.shape, 1)
    half = QK_ROPE // 2
    partner = jnp.where(lane < half, pltpu.roll(x, LANES - half, 1), pltpu.roll(x, half, 1))
    return x * cos + partner * sin_signed


def _mla_prep_kernel(cq_ref, ckv_ref, kr_ref, pos_ref, invf_ref, gqa_ref, gkva_ref,
                     wuq_ref, wukv_ref, gqn_ref, gkn_ref,
                     q_out, k_out, v_out, cqn_scr, ckvn_scr, cos_scr, sin_scr):
    @pl.when(pl.program_id(1) == 0)
    def _():
        cqn_scr[...] = _rms(cq_ref[...].astype(F32), gqa_ref[...]).astype(BF16)
        ckvn_scr[...] = _rms(ckv_ref[...].astype(F32), gkva_ref[...]).astype(BF16)
        ang = pos_ref[...].astype(F32) * invf_ref[...]
        lane = lax.broadcasted_iota(I32, ang.shape, 1)
        cos_scr[...] = jnp.cos(ang)
        sin_scr[...] = jnp.where(lane < QK_ROPE // 2, -1.0, 1.0) * jnp.sin(ang)

    cos = cos_scr[...]
    sin = sin_scr[...]
    inv_n = 1.0 / QK_HEAD
    q = _dot(cqn_scr[...], wuq_ref[...])
    rs = lax.rsqrt(jnp.sum(q * q, axis=-1, keepdims=True) * inv_n + NORM_EPS)
    q = q * rs * gqn_ref[...] * (QK_HEAD ** -0.5 * LOG2E)
    q_out[...] = jnp.concatenate(
        [q[:, 0:LANES], _rope128(q[:, LANES:], cos, sin)], axis=1).astype(q_out.dtype)

    kv = _dot(ckvn_scr[...], wukv_ref[...])
    k_nope = kv[:, 0:QK_NOPE]
    kr = kr_ref[...].astype(F32)
    ssq = (jnp.sum(k_nope * k_nope, axis=-1, keepdims=True)
           + jnp.sum(kr * kr, axis=-1, keepdims=True))
    rs = lax.rsqrt(ssq * inv_n + NORM_EPS)
    gkn = gkn_ref[...]
    k_out[...] = jnp.concatenate(
        [k_nope * rs * gkn[:, 0:LANES], _rope128(kr * rs * gkn[:, LANES:], cos, sin)],
        axis=1).astype(k_out.dtype)
    v_out[...] = jnp.transpose(kv[:, QK_NOPE:]).astype(v_out.dtype)


def _mla_prep(proj, pos, invf, g_qa, g_kva, wuq, wukv, gqn, gkn, tm):
    T = proj.shape[0]
    H = MLA_HEADS
    c0 = COL_MLA // Q_LORA
    ckr = (COL_MLA + Q_LORA + KV_LORA) // LANES
    full = lambda shape: pl.BlockSpec(shape, lambda i, h: (0, 0))
    return pl.pallas_call(
        _mla_prep_kernel,
        grid=(T // tm, H),
        in_specs=[
            pl.BlockSpec((tm, Q_LORA), lambda i, h: (i, c0)),
            pl.BlockSpec((tm, KV_LORA), lambda i, h: (i, c0 + 1)),
            pl.BlockSpec((tm, LANES), lambda i, h: (i, ckr)),
            pl.BlockSpec((tm, 1), lambda i, h: (i, 0)),
            full((1, LANES)), full((1, Q_LORA)), full((1, KV_LORA)),
            pl.BlockSpec((Q_LORA, QK_PAD), lambda i, h: (0, h)),
            pl.BlockSpec((KV_LORA, QK_NOPE + V_HEAD), lambda i, h: (0, h)),
            full((1, QK_PAD)), full((1, QK_PAD)),
        ],
        out_specs=[
            pl.BlockSpec((tm, QK_PAD), lambda i, h: (i, h)),
            pl.BlockSpec((tm, QK_PAD), lambda i, h: (i, h)),
            pl.BlockSpec((V_HEAD, tm), lambda i, h: (h, i)),
        ],
        out_shape=[jax.ShapeDtypeStruct((T, H * QK_PAD), BF16),
                   jax.ShapeDtypeStruct((T, H * QK_PAD), BF16),
                   jax.ShapeDtypeStruct((H * V_HEAD, T), BF16)],
        scratch_shapes=[pltpu.VMEM((tm, Q_LORA), BF16), pltpu.VMEM((tm, KV_LORA), BF16),
                        pltpu.VMEM((tm, LANES), F32), pltpu.VMEM((tm, LANES), F32)],
        compiler_params=_cparams(("parallel", "arbitrary")),
        name="mla_prep",
    )(proj, proj, proj, pos, invf, g_qa, g_kva, wuq, wukv, gqn, gkn)


MASK_VALUE = -1e30
LOG2E = 1.4426950408889634


def _flash_kernel(q_ref, k_ref, vt_ref, o_ref, s_scr, p_scr, alpha_scr, m_scr, l_scr, acc_scr,
                  *, tq):
    i = pl.program_id(2)
    q = q_ref[0]
    m_scr[...] = jnp.full_like(m_scr, MASK_VALUE)
    l_scr[...] = jnp.zeros_like(l_scr)
    acc_scr[...] = jnp.zeros_like(acc_scr)

    def scores(j, slot):
        ks = pl.ds(pl.multiple_of(j * tq, tq), tq)
        s_scr[slot] = _dot_nt(k_ref[0, ks, :], q)

    def softmax(slot, masked):
        for c in range(tq // LANES):
            cols = slice(c * LANES, (c + 1) * LANES)
            s = s_scr[slot, :, cols]
            if masked:
                kpos = lax.broadcasted_iota(I32, s.shape, 0)
                qpos = lax.broadcasted_iota(I32, s.shape, 1) + c * LANES
                s = jnp.where(kpos <= qpos, s, MASK_VALUE)
            m_old = m_scr[:, cols]
            m_new = jnp.maximum(m_old, jnp.max(s, axis=0, keepdims=True))
            p = jnp.exp2(s - m_new)
            alpha = jnp.exp2(m_old - m_new)
            l_scr[:, cols] = alpha * l_scr[:, cols] + jnp.sum(p, axis=0, keepdims=True)
            m_scr[:, cols] = m_new
            p_scr[slot, :, cols] = p.astype(BF16)
            alpha_scr[slot, :, cols] = alpha

    def weighted_values(j, slot):
        vt = vt_ref[:, pl.ds(pl.multiple_of(j * tq, tq), tq)]
        acc_scr[...] = alpha_scr[slot] * acc_scr[...] + _dot(vt, p_scr[slot])

    scores(0, 0)
    p_scr[1] = jnp.zeros((tq, tq), BF16)
    alpha_scr[1] = jnp.ones((1, tq), F32)

    def step(j, slot):
        scores(j + 1, 1 - slot)
        weighted_values(jnp.maximum(j - 1, 0), 1 - slot)
        softmax(slot, False)

    def pair(jp, carry):
        step(2 * jp, 0)
        step(2 * jp + 1, 1)
        return carry

    def finish(slot):
        weighted_values(jnp.maximum(i - 1, 0), 1 - slot)
        softmax(slot, True)
        weighted_values(i, slot)
        o_ref[0] = jnp.transpose(acc_scr[...] / l_scr[...]).astype(o_ref.dtype)

    lax.fori_loop(0, i // 2, pair, 0)

    @pl.when(i % 2 == 0)
    def _():
        finish(0)

    @pl.when(i % 2 == 1)
    def _():
        step(i - 1, 0)
        finish(1)


def _flash(q, k, vt, tq):
    B, S, _ = q.shape
    H = MLA_HEADS
    return pl.pallas_call(
        functools.partial(_flash_kernel, tq=tq),
        grid=(B, H, S // tq),
        in_specs=[
            pl.BlockSpec((1, tq, QK_PAD), lambda b, h, i: (b, i, h)),
            pl.BlockSpec((1, S, QK_PAD), lambda b, h, i: (b, 0, h)),
            pl.BlockSpec((V_HEAD, S), lambda b, h, i: (h, b)),
        ],
        out_specs=pl.BlockSpec((1, tq, V_HEAD), lambda b, h, i: (b, i, h)),
        out_shape=jax.ShapeDtypeStruct((B, S, H * V_HEAD), BF16),
        scratch_shapes=[pltpu.VMEM((2, tq, tq), F32), pltpu.VMEM((2, tq, tq), BF16),
                        pltpu.VMEM((2, 1, tq), F32),
                        pltpu.VMEM((1, tq), F32), pltpu.VMEM((1, tq), F32),
                        pltpu.VMEM((V_HEAD, tq), F32)],
        compiler_params=_cparams(("parallel", "parallel", "arbitrary")),
        name="flash_attn",
    )(q, k, vt)


def _merge_kernel(oa_ref, ob_ref, wa_ref, wb_ref, ga_ref, gb_ref, o_ref):
    ya = _dot(oa_ref[...], wa_ref[...])
    yb = _dot(ob_ref[...], wb_ref[...])
    ga = _sigmoid(ga_ref[...].astype(F32))
    gb = _sigmoid(gb_ref[...].astype(F32))
    o_ref[...] = (ga * ya + gb * yb).astype(o_ref.dtype)


def _merge(oa, ob, wa, wb, proj, tm, tn):
    T, K = oa.shape
    N = wa.shape[1]
    ca = COL_GA // tn
    cb = COL_GB // tn
    return pl.pallas_call(
        _merge_kernel,
        grid=(T // tm, N // tn),
        in_specs=[
            pl.BlockSpec((tm, K), lambda i, j: (i, 0)),
            pl.BlockSpec((tm, K), lambda i, j: (i, 0)),
            pl.BlockSpec((K, tn), lambda i, j: (0, j)),
            pl.BlockSpec((K, tn), lambda i, j: (0, j)),
            pl.BlockSpec((tm, tn), lambda i, j: (i, ca + j)),
            pl.BlockSpec((tm, tn), lambda i, j: (i, cb + j)),
        ],
        out_specs=pl.BlockSpec((tm, tn), lambda i, j: (i, j)),
        out_shape=jax.ShapeDtypeStruct((T, N), BF16),
        compiler_params=_cparams(("parallel", "parallel")),
        name="merge",
    )(oa, ob, wa, wb, proj, proj)


def _pack_rows(x, out_ref, n, row0=0):
    for c in range(PACK_ROWS):
        lo = x[:, c * 2 * LANES:c * 2 * LANES + LANES]
        hi = x[:, c * 2 * LANES + LANES:(c + 1) * 2 * LANES]
        lo_b = pltpu.bitcast(lo.astype(BF16).astype(F32), U32)
        hi_b = pltpu.bitcast(hi.astype(BF16).astype(F32), U32)
        out_ref[pl.ds(row0 * PACK_ROWS + c, n, stride=PACK_ROWS), :] = hi_b | (lo_b >> 16)


def _unpack_rows(ref, c, tm, lead=None):
    idx = (pl.ds(c, tm, stride=PACK_ROWS), slice(None))
    if lead is not None:
        idx = (lead,) + idx
    u = ref[idx]
    lo = pltpu.bitcast(u << 16, F32)
    hi = pltpu.bitcast(u & jnp.uint32(0xFFFF0000), F32)
    return lo, hi


def _wo_kernel(m_ref, x_ref, wo_ref, g_ref, wr_ref, x1_ref, h2_ref, h2p_ref, lg_ref, *, tm):
    hw = WO_ROWS
    groups = [slice(h * hw, (h + 1) * hw) for h in range(tm // hw)]
    wo = wo_ref[...]
    x1 = [x_ref[r, :] + _dot(m_ref[r, :], wo) for r in groups]
    h2 = [_rms(v, g_ref[...]) for v in x1]
    w_hi, w_lo = _split_bf16(wr_ref[...])
    for h, r in enumerate(groups):
        x1_ref[r, :] = x1[h]
        h2_ref[r, :] = h2[h].astype(BF16)
        _pack_rows(h2[h], h2p_ref, hw, row0=h * hw)
        h_hi, h_lo = _split_bf16(h2[h])
        lg_ref[:, r] = _dot_nt(w_hi, h_hi) + _dot_nt(w_hi, h_lo) + _dot_nt(w_lo, h_hi)


def _wo(merged, x2, wo, g, wr_t, tm):
    T, D = x2.shape
    E = wr_t.shape[0]
    return pl.pallas_call(
        functools.partial(_wo_kernel, tm=tm),
        grid=(T // tm,),
        in_specs=[
            pl.BlockSpec((tm, D), lambda i: (i, 0)),
            pl.BlockSpec((tm, D), lambda i: (i, 0)),
            pl.BlockSpec((D, D), lambda i: (0, 0), pipeline_mode=pl.Buffered(1)),
            pl.BlockSpec((1, D), lambda i: (0, 0)),
            pl.BlockSpec((E, D), lambda i: (0, 0)),
        ],
        out_specs=[
            pl.BlockSpec((tm, D), lambda i: (i, 0)),
            pl.BlockSpec((tm, D), lambda i: (i, 0)),
            pl.BlockSpec((tm * PACK_ROWS, PACK_W), lambda i: (i, 0)),
            pl.BlockSpec((E, tm), lambda i: (0, i)),
        ],
        out_shape=[jax.ShapeDtypeStruct((T, D), F32),
                   jax.ShapeDtypeStruct((T, D), BF16),
                   jax.ShapeDtypeStruct((T * PACK_ROWS, PACK_W), U32),
                   jax.ShapeDtypeStruct((E, T), F32)],
        compiler_params=_cparams(("parallel",)),
        name="wo_norm_router",
    )(merged, x2, wo, g, wr_t)


def _route_kernel(lg_ref, bias_ref, eidx_ref, wts_ref, rnk_ref, cnt_ref, carry_scr, *, tm):
    E = N_EXPERTS
    NEG = -jnp.inf

    @pl.when(pl.program_id(0) == 0)
    def _():
        carry_scr[...] = jnp.zeros_like(carry_scr)

    scores = _sigmoid(lg_ref[...])
    choice = scores + bias_ref[...]
    c3 = choice.reshape(N_GROUPS, GROUP_SIZE, tm)
    sub = lax.broadcasted_iota(I32, c3.shape, 1)
    m1 = jnp.max(c3, axis=1, keepdims=True)
    i1 = jnp.min(jnp.where(c3 == m1, sub, GROUP_SIZE), axis=1, keepdims=True)
    m2 = jnp.max(jnp.where(sub == i1, NEG, c3), axis=1, keepdims=True)
    gs = (m1 + m2).reshape(N_GROUPS, tm)
    gi = lax.broadcasted_iota(I32, gs.shape, 0)
    gsel = jnp.zeros(gs.shape, F32)
    for _ in range(TOPK_GROUPS):
        mx = jnp.max(gs, axis=0, keepdims=True)
        ix = jnp.min(jnp.where(gs == mx, gi, N_GROUPS), axis=0, keepdims=True)
        hit = gi == ix
        gsel = jnp.where(hit, 1.0, gsel)
        gs = jnp.where(hit, NEG, gs)
    emask = jnp.broadcast_to(gsel.reshape(N_GROUPS, 1, tm), (N_GROUPS, GROUP_SIZE, tm)).reshape(E, tm)
    x = jnp.where(emask > 0.5, choice, NEG)
    ei = lax.broadcasted_iota(I32, x.shape, 0)
    sel = jnp.zeros(x.shape, F32)
    idx_rows, w_rows = [], []
    for _ in range(TOP_K):
        mx = jnp.max(x, axis=0, keepdims=True)
        ix = jnp.min(jnp.where(x == mx, ei, E), axis=0, keepdims=True)
        hit = ei == ix
        w_rows.append(jnp.sum(jnp.where(hit, scores, 0.0), axis=0, keepdims=True))
        idx_rows.append(ix)
        sel = jnp.where(hit, 1.0, sel)
        x = jnp.where(hit, NEG, x)
    w = jnp.concatenate(w_rows, axis=0)
    w = w / (jnp.sum(w, axis=0, keepdims=True) + 1e-20) * ROUTED_SCALE
    eidx_ref[...] = jnp.concatenate(idx_rows, axis=0)
    wts_ref[...] = w

    upper = (lax.broadcasted_iota(I32, (tm, tm), 0) < lax.broadcasted_iota(I32, (tm, tm), 1)).astype(BF16)
    base = carry_scr[...][:, 0:1]
    excl = _dot(sel.astype(BF16), upper) + base
    rnk_ref[...] = jnp.concatenate(
        [jnp.sum(jnp.where(ei == ix, excl, 0.0), axis=0, keepdims=True) for ix in idx_rows],
        axis=0).astype(I32)
    carry_scr[...] = carry_scr[...] + jnp.sum(sel, axis=1, keepdims=True)
    cnt_ref[...] = carry_scr[...]


def _route(logits_t, bias, tm):
    E, T = logits_t.shape
    K = TOP_K
    tok = lambda: pl.BlockSpec((K, tm), lambda i: (0, i))
    return pl.pallas_call(
        functools.partial(_route_kernel, tm=tm),
        grid=(T // tm,),
        in_specs=[pl.BlockSpec((E, tm), lambda i: (0, i)),
                  pl.BlockSpec((E, 1), lambda i: (0, 0))],
        out_specs=[tok(), tok(), tok(), pl.BlockSpec((E, LANES), lambda i: (0, 0))],
        out_shape=[jax.ShapeDtypeStruct((K, T), I32), jax.ShapeDtypeStruct((K, T), F32),
                   jax.ShapeDtypeStruct((K, T), I32), jax.ShapeDtypeStruct((E, LANES), F32)],
        scratch_shapes=[pltpu.VMEM((E, LANES), F32)],
        compiler_params=_cparams(("arbitrary",)),
        name="route",
    )(logits_t, bias)


def _slots_kernel(pst_ref, eidx_ref, rnk_ref, dest_ref):
    eidx = eidx_ref[...]
    dest = rnk_ref[...]
    for e in range(N_EXPERTS):
        dest = dest + jnp.where(eidx == e, pst_ref[e], 0)
    dest_ref[...] = dest


def _slots(eidx, rnk, pstart, tm):
    K, T = eidx.shape
    tok = lambda: pl.BlockSpec((K, tm), lambda i: (0, i))
    return pl.pallas_call(
        _slots_kernel,
        grid=(T // tm,),
        in_specs=[pl.BlockSpec(memory_space=pltpu.SMEM), tok(), tok()],
        out_specs=tok(),
        out_shape=jax.ShapeDtypeStruct((K, T), I32),
        compiler_params=_cparams(("parallel",)),
        name="slots",
    )(pstart, eidx, rnk)


def _ffn_kernel(te_ref, nu_ref, tok_ref, dst_ref, h_hbm, wg_ref, wu_ref, wd_ref, ys_hbm,
                gbuf, obuf, wg_b, wu_b, wd_b, gsem, osem, *, tm):
    s = pl.program_id(0)
    nu = nu_ref[0]
    tile_rows = tm * PACK_ROWS

    def gather(slot):
        for r in range(tm):
            pltpu.make_async_copy(
                h_hbm.at[pl.ds(pl.multiple_of(tok_ref[0, 0, r] * PACK_ROWS, PACK_ROWS), PACK_ROWS), :],
                gbuf.at[slot, pl.ds(r * PACK_ROWS, PACK_ROWS), :], gsem.at[slot]).start()

    def scatter(slot):
        for r in range(tm):
            pltpu.make_async_copy(
                obuf.at[slot, pl.ds(r * PACK_ROWS, PACK_ROWS), :],
                ys_hbm.at[pl.ds(pl.multiple_of(dst_ref[0, 0, r] * PACK_ROWS, PACK_ROWS), PACK_ROWS), :],
                osem.at[slot]).start()

    def wait_gather(slot):
        pltpu.make_async_copy(h_hbm.at[pl.ds(0, tile_rows), :], gbuf.at[slot], gsem.at[slot]).wait()

    def wait_scatter(slot):
        pltpu.make_async_copy(obuf.at[slot], ys_hbm.at[pl.ds(0, tile_rows), :], osem.at[slot]).wait()

    def compute(slot):
        parts = []
        for c in range(PACK_ROWS):
            lo, hi = _unpack_rows(gbuf.at[slot], c, tm)
            parts += [lo.astype(BF16), hi.astype(BF16)]
        a = jnp.concatenate(parts, axis=1)
        hg = _dot(a, wg_b[...])
        hu = _dot(a, wu_b[...])
        hid = (hg * _sigmoid(hg) * hu).astype(BF16)
        _pack_rows(_dot(hid, wd_b[...]), obuf.at[slot], tm)

    cur = jnp.maximum(s - 1, 0)
    @pl.when((s >= 1) & (s <= nu) & ((s == 1) | (te_ref[cur] != te_ref[jnp.maximum(s - 2, 0)])))
    def _():
        wg_b[...] = wg_ref[0].astype(BF16)
        wu_b[...] = wu_ref[0].astype(BF16)
        wd_b[...] = wd_ref[0].astype(BF16)

    @pl.when(s == 0)
    def _():
        gather(0)

    @pl.when(s == 1)
    def _():
        wait_gather(0)
        gather(1)
        compute(0)

    for slot in range(2):
        parity = s % 2 == slot

        @pl.when(parity & (s >= 3) & (s <= nu + 1))
        def _():
            wait_scatter(1 - slot)

        @pl.when(parity & (s >= 2) & (s <= nu))
        def _():
            wait_gather(1 - slot)
            gather(slot)
            scatter(slot)
            compute(1 - slot)

        @pl.when(parity & (s >= 2) & (s == nu + 1))
        def _():
            wait_gather(1 - slot)
            scatter(slot)
            wait_scatter(slot)


def _ffn(tile_e, n_used, row_tok, row_dst, h2p, wg, wu, wd, tm):
    nt = row_tok.shape[0]
    D, I = wg.shape[1], wg.shape[2]
    cur = lambda s: jnp.maximum(s - 1, 0)
    wspec = lambda shape: pl.BlockSpec(shape, lambda s, te, nu: (te[cur(s)], 0, 0))
    grid_spec = pltpu.PrefetchScalarGridSpec(
        num_scalar_prefetch=2,
        grid=(nt + 1,),
        in_specs=[
            pl.BlockSpec((1, 1, tm), lambda s, te, nu: (jnp.minimum(s, nt - 1), 0, 0),
                         memory_space=pltpu.SMEM),
            pl.BlockSpec((1, 1, tm), lambda s, te, nu: (jnp.maximum(s - 2, 0), 0, 0),
                         memory_space=pltpu.SMEM),
            pl.BlockSpec(memory_space=pl.ANY),
            wspec((1, D, I)), wspec((1, D, I)), wspec((1, I, D)),
        ],
        out_specs=pl.BlockSpec(memory_space=pl.ANY),
        scratch_shapes=[pltpu.VMEM((2, tm * PACK_ROWS, PACK_W), U32),
                        pltpu.VMEM((2, tm * PACK_ROWS, PACK_W), U32),
                        pltpu.VMEM((D, I), BF16), pltpu.VMEM((D, I), BF16), pltpu.VMEM((I, D), BF16),
                        pltpu.SemaphoreType.DMA((2,)), pltpu.SemaphoreType.DMA((2,))],
    )
    return pl.pallas_call(
        functools.partial(_ffn_kernel, tm=tm),
        grid_spec=grid_spec,
        out_shape=jax.ShapeDtypeStruct((nt * tm * PACK_ROWS, PACK_W), U32),
        compiler_params=_cparams(("arbitrary",)),
        name="expert_ffn",
    )(tile_e, n_used, row_tok, row_dst, h2p, wg, wu, wd)


def _combine_kernel(ys_ref, wt_ref, x1_ref, h2_ref, wsg_ref, wsu_ref, wsd_ref, o_ref, *, tm):
    h2 = h2_ref[...]
    hg = _dot(h2, wsg_ref[...])
    hu = _dot(h2, wsu_ref[...])
    base = x1_ref[...] + _dot((hg * _sigmoid(hg) * hu).astype(BF16), wsd_ref[...])
    wt = wt_ref[...]
    stride = TOP_K * PACK_ROWS
    for c in range(PACK_ROWS):
        acc_lo = base[:, c * 2 * LANES:c * 2 * LANES + LANES]
        acc_hi = base[:, c * 2 * LANES + LANES:(c + 1) * 2 * LANES]
        for k in range(TOP_K):
            u = ys_ref[pl.ds(k * PACK_ROWS + c, tm, stride=stride), :]
            wk = wt[:, k:k + 1]
            acc_lo = acc_lo + wk * pltpu.bitcast(u << 16, F32)
            acc_hi = acc_hi + wk * pltpu.bitcast(u & jnp.uint32(0xFFFF0000), F32)
        o_ref[:, c * 2 * LANES:c * 2 * LANES + LANES] = acc_lo
        o_ref[:, c * 2 * LANES + LANES:(c + 1) * 2 * LANES] = acc_hi


def _combine(ys, wts_t, x1, h2, wsg, wsu, wsd, tm):
    T, D = x1.shape
    K = wts_t.shape[1]
    I = wsg.shape[1]
    full = lambda shape: pl.BlockSpec(shape, lambda i: (0, 0))
    return pl.pallas_call(
        functools.partial(_combine_kernel, tm=tm),
        grid=(T // tm,),
        in_specs=[pl.BlockSpec((tm * K * PACK_ROWS, PACK_W), lambda i: (i, 0)),
                  pl.BlockSpec((tm, K), lambda i: (i, 0)),
                  pl.BlockSpec((tm, D), lambda i: (i, 0)),
                  pl.BlockSpec((tm, D), lambda i: (i, 0)),
                  full((D, I)), full((D, I)), full((I, D))],
        out_specs=pl.BlockSpec((tm, D), lambda i: (i, 0)),
        out_shape=jax.ShapeDtypeStruct((T, D), F32),
        compiler_params=_cparams(("parallel",)),
        name="combine",
    )(ys, wts_t, x1, h2, wsg, wsu, wsd)


def _ple_kernel(xf_ref, xt_ref, p_ref, wp_ref, gpost_ref, gin_ref, wg_ref, o_ref,
                hn_scr, ple_scr, *, tn):
    j = pl.program_id(1)

    @pl.when(j == 0)
    def _():
        hn_scr[...] = _rms(xf_ref[...], gin_ref[...]).astype(BF16)
        ple_scr[...] = _rms(_dot(p_ref[...].astype(BF16), wp_ref[...]), gpost_ref[...])

    cols = pl.ds(pl.multiple_of(j * tn, tn), tn)
    gate = _sigmoid(_dot(hn_scr[...], wg_ref[...]))
    o_ref[...] = xt_ref[...] + gate * ple_scr[:, cols]


def _ple(x2, p2, wp, gpost, gin, wg, tm, tn):
    T, D = x2.shape
    Pd = p2.shape[1]
    return pl.pallas_call(
        functools.partial(_ple_kernel, tn=tn),
        grid=(T // tm, D // tn),
        in_specs=[
            pl.BlockSpec((tm, D), lambda i, j: (i, 0)),
            pl.BlockSpec((tm, tn), lambda i, j: (i, j)),
            pl.BlockSpec((tm, Pd), lambda i, j: (i, 0)),
            pl.BlockSpec((Pd, D), lambda i, j: (0, 0)),
            pl.BlockSpec((1, D), lambda i, j: (0, 0)),
            pl.BlockSpec((1, D), lambda i, j: (0, 0)),
            pl.BlockSpec((D, tn), lambda i, j: (0, j)),
        ],
        out_specs=pl.BlockSpec((tm, tn), lambda i, j: (i, j)),
        out_shape=jax.ShapeDtypeStruct((T, D), F32),
        scratch_shapes=[pltpu.VMEM((tm, D), BF16), pltpu.VMEM((tm, D), F32)],
        compiler_params=_cparams(("parallel", "arbitrary")),
        name="ple",
    )(x2, x2, p2, wp, gpost, gin, wg)


def _pad_cols(w, n):
    return jnp.pad(w, ((0, 0), (0, n - w.shape[1])))


def _pad_rows(w, n):
    return jnp.pad(w, ((0, n - w.shape[0]), (0, 0)))


def _layout_w_in(w):
    W = RWKV_WIDTH
    c = 3 * W
    segs = [w[:, 0:c],
            _pad_cols(w[:, c:c + DECAY_LORA], LORA_PAD),
            _pad_cols(w[:, c + DECAY_LORA:c + DECAY_LORA + AAA_LORA], LORA_PAD),
            _pad_cols(w[:, c + DECAY_LORA + AAA_LORA:c + DECAY_LORA + AAA_LORA + GATE_LORA], GATE_LORA_PAD)]
    c += DECAY_LORA + AAA_LORA + GATE_LORA
    mla_cols = Q_LORA + KV_LORA + QK_ROPE
    segs.append(_pad_cols(w[:, c:c + mla_cols], MLA_PAD))
    c += mla_cols
    segs.append(w[:, c:])
    out = jnp.concatenate(segs, axis=1).astype(BF16)
    assert out.shape[1] == IN_PAD
    return out


def _layout_mu(mu):
    W = RWKV_WIDTH
    c = 3 * W
    segs = [mu[0:c],
            jnp.pad(mu[c:c + DECAY_LORA], (0, LORA_PAD - DECAY_LORA)),
            jnp.pad(mu[c + DECAY_LORA:c + DECAY_LORA + AAA_LORA], (0, LORA_PAD - AAA_LORA)),
            jnp.pad(mu[c + DECAY_LORA + AAA_LORA:], (0, GATE_LORA_PAD - GATE_LORA))]
    return jnp.concatenate(segs).reshape(1, RWKV_PAD)


def _moe_tables(eidx, rnk, counts, tmf, tm_slots):
    K, T = eidx.shape
    E = N_EXPERTS
    A = K * T
    n_tiles = A // tmf + E
    P = n_tiles * tmf
    pcounts = (counts + tmf - 1) // tmf * tmf
    pends = jnp.cumsum(pcounts)
    pstart = (pends - pcounts).astype(I32)
    tile_start = jnp.arange(n_tiles, dtype=I32) * tmf
    tile_e = jnp.minimum(jnp.sum((pends[None, :] <= tile_start[:, None]).astype(I32), axis=1), E - 1)
    n_used = (pends[E - 1:] // tmf).astype(I32)
    dest = _slots(eidx, rnk, pstart, tm_slots)
    pads = pcounts - counts
    padcum = jnp.cumsum(pads)
    q = jnp.arange(P - A, dtype=I32)
    seg = jnp.sum((padcum[None, :] <= q[:, None]).astype(I32), axis=1)
    seg_c = jnp.minimum(seg, E - 1)
    pad_row = jnp.where(seg < E,
                        pstart[seg_c] + counts[seg_c] + q - (padcum - pads)[seg_c],
                        pends[E - 1] + q - padcum[E - 1])
    slot_id = (jnp.arange(T, dtype=I32)[None, :] * K + jnp.arange(K, dtype=I32)[:, None]).reshape(-1)
    rows = jnp.concatenate([dest.reshape(-1), pad_row.astype(I32)])
    slots = jnp.concatenate([slot_id, A + q])
    row_dst = lax.sort((rows, slots), num_keys=1)[1]
    row_tok = jnp.where(row_dst < A, row_dst // K, 0)
    return (row_tok.reshape(n_tiles, 1, tmf), row_dst.reshape(n_tiles, 1, tmf), tile_e, n_used)


def _layer(x, p, positions, g_mix, w_in, mu_rwkv, w0, w2, a0, a2, g2, k_k, k_a, r_k,
           gn_w, gn_b, w_a_up, g_qa, g_kva, w_uq, w_ukv, g_qn, g_kn, w_b_up, w_o,
           g_ffn, w_router, router_bias, w_exp_gate, w_exp_up, w_exp_down,
           w_sh_gate, w_sh_up, w_sh_down, w_ple, g_ple_post, g_ple_in, w_ple_gate,
           *, tiles):
    B, S, D = x.shape
    T = B * S
    W = RWKV_WIDTH
    x2 = x.reshape(T, D)
    row = lambda v: v.reshape(1, -1).astype(F32)

    proj = _inproj(x2, row(g_mix), _layout_w_in(w_in), tiles["tm_in"], tiles["tn_in"])

    head_blk = (jnp.arange(W)[:, None] // RWKV_HEAD == jnp.arange(W)[None, :] // RWKV_HEAD).astype(BF16)
    r, k, v, kk, a, lw, g = _rwkv_prep(
        proj, _layout_mu(mu_rwkv), row(w0), _pad_rows(w2, LORA_PAD).astype(BF16), row(a0),
        _pad_rows(a2, LORA_PAD).astype(BF16), _pad_rows(g2, GATE_LORA_PAD).astype(BF16),
        row(k_k), row(k_a), head_blk, tiles["tm_prep"], S)
    b3 = lambda t: t.reshape(B, S, W)
    o_a = _rwkv_scan(b3(r), b3(k), b3(v), b3(kk), b3(a), b3(lw), b3(g),
                     row(gn_w), row(gn_b), row(r_k), tiles["rows_scan"]).reshape(T, W)

    half = QK_ROPE // 2
    inv_freq = ROPE_THETA ** (-jnp.arange(half, dtype=F32) / half)
    invf = jnp.concatenate([inv_freq, inv_freq, jnp.zeros((LANES - QK_ROPE,), F32)]).reshape(1, LANES)
    wuq = jnp.pad(w_uq.reshape(Q_LORA, MLA_HEADS, QK_HEAD),
                  ((0, 0), (0, 0), (0, QK_PAD - QK_HEAD))).reshape(Q_LORA, MLA_HEADS * QK_PAD)
    pad_g = lambda gv: jnp.pad(gv, (0, QK_PAD - QK_HEAD)).reshape(1, QK_PAD)
    q, kx, vx = _mla_prep(proj, positions.reshape(T, 1).astype(I32), invf, row(g_qa), row(g_kva),
                          wuq.astype(BF16), w_ukv.astype(BF16), pad_g(g_qn), pad_g(g_kn),
                          tiles["tm_mla"])
    o_b = _flash(q.reshape(B, S, -1), kx.reshape(B, S, -1), vx, tiles["tq"]).reshape(T, MLA_WIDTH)

    merged = _merge(o_a, o_b, w_a_up.astype(BF16), w_b_up.astype(BF16), proj,
                    tiles["tm_merge"], tiles["tn_merge"])

    x1, h2, h2p, logits_t = _wo(merged, x2, w_o.astype(BF16), row(g_ffn),
                                jnp.transpose(w_router).astype(F32), tiles["tm_wo"])
    eidx, wts, rnk, cnt = _route(logits_t, router_bias.reshape(N_EXPERTS, 1).astype(F32),
                                 tiles["tm_route"])
    tmf = tiles["tm_ffn"]
    counts = cnt[:, 0].astype(I32)
    row_tok, row_dst, tile_e, n_used = _moe_tables(eidx, rnk, counts, tmf, tiles["tm_slots"])
    ys = _ffn(tile_e, n_used, row_tok, row_dst, h2p, w_exp_gate, w_exp_up, w_exp_down, tmf)
    x3 = _combine(ys, jnp.transpose(wts), x1, h2,
                  w_sh_gate.astype(BF16), w_sh_up.astype(BF16), w_sh_down.astype(BF16),
                  tiles["tm_comb"])

    out = _ple(x3, p.reshape(T, PLE_DIM), w_ple.astype(BF16), row(g_ple_post), row(g_ple_in),
               w_ple_gate.astype(BF16), tiles["tm_ple"], tiles["tn_ple"])
    return out.reshape(B, S, D)


TILES = dict(tm_in=1024, tn_in=512, tm_prep=256, rows_scan=512, tm_mla=512, tq=512,
             tm_merge=512, tn_merge=512, tm_wo=512, tm_route=512, tm_ffn=256,
             tm_slots=2048, tm_comb=128, tm_ple=512, tn_ple=1024)


def kernel(x, p, positions, g_mix, w_in, mu_rwkv, w0, w2, a0, a2, g2, k_k, k_a, r_k, gn_w, gn_b, w_a_up, g_qa, g_kva, w_uq, w_ukv, g_qn, g_kn, w_b_up, w_o, g_ffn, w_router, router_bias, w_exp_gate, w_exp_up, w_exp_down, w_sh_gate, w_sh_up, w_sh_down, w_ple, g_ple_post, g_ple_in, w_ple_gate):
    args = (g_mix, w_in, mu_rwkv, w0, w2, a0, a2, g2, k_k, k_a, r_k, gn_w, gn_b, w_a_up,
            g_qa, g_kva, w_uq, w_ukv, g_qn, g_kn, w_b_up, w_o, g_ffn, w_router, router_bias,
            w_exp_gate, w_exp_up, w_exp_down, w_sh_gate, w_sh_up, w_sh_down, w_ple,
            g_ple_post, g_ple_in, w_ple_gate)
    assert all(t.shape[0] == 1 for t in args), "single-layer stack expected"
    return _layer(x, p[0], positions, *[t[0] for t in args], tiles=TILES)
```

```python
import functools

import jax
import jax.numpy as jnp
from jax import lax
from jax.experimental import pallas as pl
from jax.experimental.pallas import tpu as pltpu

F32 = jnp.float32
BF16 = jnp.bfloat16
I32 = jnp.int32
U32 = jnp.uint32

D_MODEL = 2048
PLE_DIM = 256
NORM_EPS = 1e-6
RWKV_HEADS = 16
RWKV_HEAD = 64
RWKV_WIDTH = RWKV_HEADS * RWKV_HEAD
DECAY_LORA = 64
AAA_LORA = 64
GATE_LORA = 160
GN_EPS = 64e-5
MLA_HEADS = 8
Q_LORA = 512
KV_LORA = 512
QK_NOPE = 128
QK_ROPE = 64
QK_HEAD = QK_NOPE + QK_ROPE
V_HEAD = 128
MLA_WIDTH = MLA_HEADS * V_HEAD
ROPE_THETA = 10000.0
N_EXPERTS = 64
TOP_K = 8
N_GROUPS = 8
GROUP_SIZE = N_EXPERTS // N_GROUPS
TOPK_GROUPS = 4
MOE_INTER = 512
ROUTED_SCALE = 2.5

LANES = 128
QK_PAD = 2 * LANES
PACK_ROWS = 8
PACK_W = D_MODEL // 2 // PACK_ROWS
assert PACK_W == LANES

LORA_PAD = 128
GATE_LORA_PAD = 256
RWKV_PAD = 3 * RWKV_WIDTH + 2 * LORA_PAD + GATE_LORA_PAD
MLA_PAD = 1536
COL_RWKV = 0
COL_MLA = RWKV_PAD
COL_GA = COL_MLA + MLA_PAD
COL_GB = COL_GA + D_MODEL
IN_PAD = COL_GB + D_MODEL

VMEM_LIMIT = 56 * 1024 * 1024
CHUNK = 64
WO_ROWS = 128


def _cparams(sem):
    return pltpu.CompilerParams(dimension_semantics=sem, vmem_limit_bytes=VMEM_LIMIT)


def _dot(a, b):
    return jnp.dot(a, b, preferred_element_type=F32)


def _dot_nt(a, b):
    return lax.dot_general(a, b, (((1,), (1,)), ((), ())), preferred_element_type=F32)


def _dot_tn(a, b):
    return lax.dot_general(a, b, (((0,), (0,)), ((), ())), preferred_element_type=F32)


def _split_bf16(x):
    hi = x.astype(BF16)
    lo = (x - hi.astype(F32)).astype(BF16)
    return hi, lo


def _rms(x, g):
    ms = jnp.mean(x * x, axis=-1, keepdims=True)
    return x * lax.rsqrt(ms + NORM_EPS) * g


def _sigmoid(x):
    return 1.0 / (1.0 + jnp.exp(-x))


def _inproj_kernel(x_ref, g_ref, w_ref, o_ref, h_scr):
    @pl.when(pl.program_id(1) == 0)
    def _():
        h_scr[...] = _rms(x_ref[...], g_ref[...]).astype(BF16)

    o_ref[...] = _dot(h_scr[...], w_ref[...]).astype(o_ref.dtype)


def _inproj(x2, g, w, tm, tn):
    T, D = x2.shape
    N = w.shape[1]
    return pl.pallas_call(
        _inproj_kernel,
        grid=(T // tm, N // tn),
        in_specs=[
            pl.BlockSpec((tm, D), lambda i, j: (i, 0)),
            pl.BlockSpec((1, D), lambda i, j: (0, 0)),
            pl.BlockSpec((D, tn), lambda i, j: (0, j)),
        ],
        out_specs=pl.BlockSpec((tm, tn), lambda i, j: (i, j)),
        out_shape=jax.ShapeDtypeStruct((T, N), BF16),
        scratch_shapes=[pltpu.VMEM((tm, D), BF16)],
        compiler_params=_cparams(("parallel", "arbitrary")),
        name="inproj",
    )(x2, g, w)


HALO = 16


def _rwkv_prep_kernel(cur_ref, halo_ref, mu_ref, w0_ref, w2_ref, a0_ref, a2_ref, g2_ref,
                      kk_ref, ka_ref, bd_ref,
                      r_out, k_out, v_out, kk_out, a_out, lw_out, g_out, *, tm, seq):
    i = pl.program_id(0)
    cur = cur_ref[...].astype(F32)
    first = (i * tm) % seq == 0
    last_prev = halo_ref[HALO - 1:HALO, :].astype(F32)
    last_prev = jnp.where(first, 0.0, last_prev)
    row = lax.broadcasted_iota(I32, cur.shape, 0)
    prev = jnp.where(row == 0, last_prev, pltpu.roll(cur, 1, 0))
    u = cur + (prev - cur) * mu_ref[...]
    W = RWKV_WIDTH
    r = u[:, 0:W]
    k = u[:, W:2 * W]
    v = u[:, 2 * W:3 * W]
    dw = u[:, 3 * W:3 * W + LORA_PAD]
    da = u[:, 3 * W + LORA_PAD:3 * W + 2 * LORA_PAD]
    dg = u[:, 3 * W + 2 * LORA_PAD:3 * W + 2 * LORA_PAD + GATE_LORA_PAD]
    z = w0_ref[...] + _dot(jnp.tanh(dw).astype(BF16), w2_ref[...])
    w_log = -(jnp.maximum(-z, 0.0) + jnp.log(1.0 + jnp.exp(-jnp.abs(z)))) - 0.5
    lw_out[...] = -jnp.exp(w_log)
    a = _sigmoid(a0_ref[...] + _dot(da.astype(BF16), a2_ref[...]))
    g_out[...] = _dot(_sigmoid(dg).astype(BF16), g2_ref[...]).astype(g_out.dtype)
    kk = k * kk_ref[...]
    hi, lo = _split_bf16(kk * kk)
    ssq = _dot(hi, bd_ref[...]) + _dot(lo, bd_ref[...])
    kk = kk / jnp.maximum(jnp.sqrt(ssq), 1e-12)
    r_out[...] = r.astype(r_out.dtype)
    k_out[...] = (k * (1.0 + (a - 1.0) * ka_ref[...])).astype(k_out.dtype)
    v_out[...] = v.astype(v_out.dtype)
    kk_out[...] = kk.astype(kk_out.dtype)
    a_out[...] = a.astype(a_out.dtype)


def _rwkv_prep(proj, mu, w0, w2, a0, a2, g2, k_k, k_a, bd, tm, seq):
    T = proj.shape[0]
    W = RWKV_WIDTH
    full = lambda shape: pl.BlockSpec(shape, lambda i: (0, 0))
    out = lambda: pl.BlockSpec((tm, W), lambda i: (i, 0))
    return pl.pallas_call(
        functools.partial(_rwkv_prep_kernel, tm=tm, seq=seq),
        grid=(T // tm,),
        in_specs=[
            pl.BlockSpec((tm, RWKV_PAD), lambda i: (i, 0)),
            pl.BlockSpec((HALO, RWKV_PAD), lambda i: (jnp.maximum(i * (tm // HALO) - 1, 0), 0)),
            full((1, RWKV_PAD)), full((1, W)), full((LORA_PAD, W)), full((1, W)),
            full((LORA_PAD, W)), full((GATE_LORA_PAD, W)), full((1, W)), full((1, W)),
            full((W, W)),
        ],
        out_specs=[out() for _ in range(7)],
        out_shape=[jax.ShapeDtypeStruct((T, W), dt)
                   for dt in (BF16, BF16, BF16, BF16, BF16, F32, BF16)],
        compiler_params=_cparams(("parallel",)),
        name="rwkv_prep",
    )(proj, proj, mu, w0, w2, a0, a2, g2, k_k, k_a, bd)


def _rwkv_scan_kernel(r_ref, k_ref, v_ref, kk_ref, a_ref, lw_ref, g_ref,
                      gnw_ref, gnb_ref, rk_ref, o_ref, s_scr, *, rows):
    L = CHUNK
    L2 = 2 * L

    @pl.when(pl.program_id(2) == 0)
    def _():
        s_scr[...] = jnp.zeros_like(s_scr)

    lane = lax.broadcasted_iota(I32, (L, LANES), 1)
    m_a = (lane < RWKV_HEAD).astype(F32)
    m_b = 1.0 - m_a
    ri = lax.broadcasted_iota(I32, (L2, L2), 0)
    ci = lax.broadcasted_iota(I32, (L2, L2), 1)
    strict = ri > ci
    incl = ri >= ci
    eye = (ri == ci).astype(F32)
    tri = (lax.broadcasted_iota(I32, (L, L), 0) >= lax.broadcasted_iota(I32, (L, L), 1)).astype(BF16)
    hr = lax.broadcasted_iota(I32, (LANES, LANES), 0) // RWKV_HEAD
    hc = lax.broadcasted_iota(I32, (LANES, LANES), 1) // RWKV_HEAD
    head_ones = (hr == hc).astype(BF16)
    gnw = gnw_ref[...]
    gnb = gnb_ref[...]
    rk = rk_ref[...]

    def stack(x):
        return jnp.concatenate([x * m_a, x * m_b], axis=0)

    chunks = range(rows // L)
    rows_of = lambda c: slice(c * L, (c + 1) * L)
    each = lambda f, *lists: [f(*vals) for vals in zip(*lists)]

    def operands(c):
        sl = rows_of(c)
        r = r_ref[0, sl, :].astype(F32)
        k = k_ref[0, sl, :].astype(F32)
        v = v_ref[0, sl, :].astype(F32)
        kk = kk_ref[0, sl, :].astype(F32)
        a = a_ref[0, sl, :].astype(F32)
        lw = lw_ref[0, sl, :]
        hi, lo = _split_bf16(lw)
        cum = _dot(tri, hi) + _dot(tri, lo)
        c_end = cum[L - 1:L, :]
        p_inv = jnp.exp(-cum)
        p_end = jnp.exp(c_end - cum)
        b = kk * a
        return dict(
            x_a=stack(-(kk * jnp.exp(cum - lw))).astype(BF16),
            x_r=stack(r * jnp.exp(cum)),
            x_bk=jnp.concatenate([stack(b * p_inv), stack(k * p_inv)], axis=0).astype(BF16),
            v_st=stack(v).astype(BF16),
            z_hat=jnp.concatenate([stack(b * p_end), stack(k * p_end)], axis=0).astype(BF16),
            p_row=jnp.exp(c_end),
            rkr=(r * k * rk).astype(BF16), v=v)

    ops = [operands(c) for c in chunks]
    big = [_dot_nt(jnp.concatenate([o["x_a"], o["x_r"].astype(BF16)], axis=0), o["x_bk"])
           for o in ops]
    n_pow = [jnp.where(strict, m[0:L2, 0:L2], 0.0) for m in big]
    a_ak = [jnp.where(strict, m[0:L2, L2:2 * L2], 0.0).astype(BF16) for m in big]
    a_r = [jnp.concatenate([jnp.where(incl, m[L2:2 * L2, 0:L2], 0.0),
                            jnp.where(incl, m[L2:2 * L2, L2:2 * L2], 0.0)], axis=1).astype(BF16)
           for m in big]

    t_inv = [eye + n for n in n_pow]
    for _ in range(5):
        n_pow = each(lambda n: _dot(n.astype(BF16), n.astype(BF16)), n_pow)
        t_inv = each(lambda t, n: t + _dot(t.astype(BF16), n.astype(BF16)), t_inv, n_pow)

    akv = each(lambda m, o: _dot(m, o["v_st"]).astype(BF16), a_ak, ops)
    w = each(lambda t, o, x: _dot(t.astype(BF16), jnp.concatenate([o["x_a"], x], axis=1)),
             t_inv, ops, akv)
    w1 = [m[:, 0:LANES].astype(BF16) for m in w]
    w2v = each(lambda m, o: jnp.concatenate([m[:, LANES:].astype(BF16), o["v_st"]], axis=0), w, ops)
    g_mat = each(lambda o, m, x: (o["x_r"] + _dot(m[:, 0:L2], x)).astype(BF16), ops, a_r, w1)
    y0 = each(_dot, a_r, w2v)
    m_mat = each(lambda x, o: _dot_tn(x, o["z_hat"][0:L2, :]).astype(BF16), w1, ops)
    c2 = each(lambda x, o: _dot_tn(x, o["z_hat"]), w2v, ops)
    bonus_v = [_dot(o["rkr"], head_ones) * o["v"] for o in ops]

    s = s_scr[...]
    y = []
    for c in chunks:
        s_b = s.astype(BF16)
        y_st = _dot_nt(g_mat[c], s_b) + y0[c]
        s = s * ops[c]["p_row"] + _dot(s_b, m_mat[c]) + c2[c]
        y.append(y_st[0:L, :] + y_st[L:L2, :])
    s_scr[...] = s

    mean = [_dot(t.astype(BF16), head_ones) * (1.0 / RWKV_HEAD) for t in y]
    yc = each(lambda t, m: t - m, y, mean)
    var = [_dot((t * t).astype(BF16), head_ones) * (1.0 / RWKV_HEAD) for t in yc]
    for c in chunks:
        yn = yc[c] * lax.rsqrt(var[c] + GN_EPS) * gnw + gnb
        sl = rows_of(c)
        o_ref[0, sl, :] = ((yn + bonus_v[c]) * g_ref[0, sl, :].astype(F32)).astype(o_ref.dtype)


def _rwkv_scan(r, k, v, kk, a, lw, g, gn_w, gn_b, r_k, rows):
    B, S, W = r.shape
    npair = W // LANES
    seq = lambda: pl.BlockSpec((1, rows, LANES), lambda b, p, c: (b, c, p))
    par = lambda: pl.BlockSpec((1, LANES), lambda b, p, c: (0, p))
    return pl.pallas_call(
        functools.partial(_rwkv_scan_kernel, rows=rows),
        grid=(B, npair, S // rows),
        in_specs=[seq() for _ in range(7)] + [par(), par(), par()],
        out_specs=seq(),
        out_shape=jax.ShapeDtypeStruct((B, S, W), BF16),
        scratch_shapes=[pltpu.VMEM((LANES, LANES), F32)],
        compiler_params=_cparams(("parallel", "parallel", "arbitrary")),
        name="rwkv_scan",
    )(r, k, v, kk, a, lw, g, gn_w, gn_b, r_k)


def _rope128(x, cos, sin_signed):
    lane = lax.broadcasted_iota(I32, x.shape, 1)
    half = QK_ROPE // 2
    partner = jnp.where(lane < half, pltpu.roll(x, LANES - half, 1), pltpu.roll(x, half, 1))
    return x * cos + partner * sin_signed


def _mla_prep_kernel(cq_ref, ckv_ref, kr_ref, pos_ref, invf_ref, gqa_ref, gkva_ref,
                     wuq_ref, wukv_ref, gqn_ref, gkn_ref,
                     q_out, k_out, v_out, cqn_scr, ckvn_scr, cos_scr, sin_scr):
    @pl.when(pl.program_id(1) == 0)
    def _():
        cqn_scr[...] = _rms(cq_ref[...].astype(F32), gqa_ref[...]).astype(BF16)
        ckvn_scr[...] = _rms(ckv_ref[...].astype(F32), gkva_ref[...]).astype(BF16)
        ang = pos_ref[...].astype(F32) * invf_ref[...]
        lane = lax.broadcasted_iota(I32, ang.shape, 1)
        cos_scr[...] = jnp.cos(ang)
        sin_scr[...] = jnp.where(lane < QK_ROPE // 2, -1.0, 1.0) * jnp.sin(ang)

    cos = cos_scr[...]
    sin = sin_scr[...]
    inv_n = 1.0 / QK_HEAD
    q = _dot(cqn_scr[...], wuq_ref[...])
    rs = lax.rsqrt(jnp.sum(q * q, axis=-1, keepdims=True) * inv_n + NORM_EPS)
    q = q * rs * gqn_ref[...] * (QK_HEAD ** -0.5 * LOG2E)
    q_out[...] = jnp.concatenate(
        [q[:, 0:LANES], _rope128(q[:, LANES:], cos, sin)], axis=1).astype(q_out.dtype)

    kv = _dot(ckvn_scr[...], wukv_ref[...])
    k_nope = kv[:, 0:QK_NOPE]
    kr = kr_ref[...].astype(F32)
    ssq = (jnp.sum(k_nope * k_nope, axis=-1, keepdims=True)
           + jnp.sum(kr * kr, axis=-1, keepdims=True))
    rs = lax.rsqrt(ssq * inv_n + NORM_EPS)
    gkn = gkn_ref[...]
    k_out[...] = jnp.concatenate(
        [k_nope * rs * gkn[:, 0:LANES], _rope128(kr * rs * gkn[:, LANES:], cos, sin)],
        axis=1).astype(k_out.dtype)
    v_out[...] = jnp.transpose(kv[:, QK_NOPE:]).astype(v_out.dtype)


def _mla_prep(proj, pos, invf, g_qa, g_kva, wuq, wukv, gqn, gkn, tm):
    T = proj.shape[0]
    H = MLA_HEADS
    c0 = COL_MLA // Q_LORA
    ckr = (COL_MLA + Q_LORA + KV_LORA) // LANES
    full = lambda shape: pl.BlockSpec(shape, lambda i, h: (0, 0))
    return pl.pallas_call(
        _mla_prep_kernel,
        grid=(T // tm, H),
        in_specs=[
            pl.BlockSpec((tm, Q_LORA), lambda i, h: (i, c0)),
            pl.BlockSpec((tm, KV_LORA), lambda i, h: (i, c0 + 1)),
            pl.BlockSpec((tm, LANES), lambda i, h: (i, ckr)),
            pl.BlockSpec((tm, 1), lambda i, h: (i, 0)),
            full((1, LANES)), full((1, Q_LORA)), full((1, KV_LORA)),
            pl.BlockSpec((Q_LORA, QK_PAD), lambda i, h: (0, h)),
            pl.BlockSpec((KV_LORA, QK_NOPE + V_HEAD), lambda i, h: (0, h)),
            full((1, QK_PAD)), full((1, QK_PAD)),
        ],
        out_specs=[
            pl.BlockSpec((tm, QK_PAD), lambda i, h: (i, h)),
            pl.BlockSpec((tm, QK_PAD), lambda i, h: (i, h)),
            pl.BlockSpec((V_HEAD, tm), lambda i, h: (h, i)),
        ],
        out_shape=[jax.ShapeDtypeStruct((T, H * QK_PAD), BF16),
                   jax.ShapeDtypeStruct((T, H * QK_PAD), BF16),
                   jax.ShapeDtypeStruct((H * V_HEAD, T), BF16)],
        scratch_shapes=[pltpu.VMEM((tm, Q_LORA), BF16), pltpu.VMEM((tm, KV_LORA), BF16),
                        pltpu.VMEM((tm, LANES), F32), pltpu.VMEM((tm, LANES), F32)],
        compiler_params=_cparams(("parallel", "arbitrary")),
        name="mla_prep",
    )(proj, proj, proj, pos, invf, g_qa, g_kva, wuq, wukv, gqn, gkn)


MASK_VALUE = -1e30
LOG2E = 1.4426950408889634


def _flash_kernel(q_ref, k_ref, vt_ref, o_ref, s_scr, p_scr, alpha_scr, m_scr, l_scr, acc_scr,
                  *, tq):
    i = pl.program_id(2)
    q = q_ref[0]
    m_scr[...] = jnp.full_like(m_scr, MASK_VALUE)
    l_scr[...] = jnp.zeros_like(l_scr)
    acc_scr[...] = jnp.zeros_like(acc_scr)

    q_half = tq // 2

    def scores(j, slot, half):
        ks = pl.ds(pl.multiple_of(j * tq, tq), tq)
        cols = slice(half * q_half, (half + 1) * q_half)
        s_scr[slot, :, cols] = _dot_nt(k_ref[0, ks, :], q[cols, :])

    def softmax(slot, masked, strips):
        for c in strips:
            cols = slice(c * LANES, (c + 1) * LANES)
            s = s_scr[slot, :, cols]
            if masked:
                kpos = lax.broadcasted_iota(I32, s.shape, 0)
                qpos = lax.broadcasted_iota(I32, s.shape, 1) + c * LANES
                s = jnp.where(kpos <= qpos, s, MASK_VALUE)
            m_old = m_scr[:, cols]
            m_new = jnp.maximum(m_old, jnp.max(s, axis=0, keepdims=True))
            p = jnp.exp2(s - m_new)
            alpha = jnp.exp2(m_old - m_new)
            l_scr[:, cols] = alpha * l_scr[:, cols] + jnp.sum(p, axis=0, keepdims=True)
            m_scr[:, cols] = m_new
            p_scr[slot, :, cols] = p.astype(BF16)
            alpha_scr[slot, :, cols] = alpha

    def weighted_values(j, slot):
        vt = vt_ref[:, pl.ds(pl.multiple_of(j * tq, tq), tq)]
        acc_scr[...] = alpha_scr[slot] * acc_scr[...] + _dot(vt, p_scr[slot])

    scores(0, 0, 0)
    scores(0, 0, 1)
    p_scr[1] = jnp.zeros((tq, tq), BF16)
    alpha_scr[1] = jnp.ones((1, tq), F32)
    n_strips = tq // LANES
    first, second = range(0, n_strips // 2), range(n_strips // 2, n_strips)

    def step(j, slot):
        weighted_values(jnp.maximum(j - 1, 0), 1 - slot)
        scores(j + 1, 1 - slot, 0)
        softmax(slot, False, first)
        scores(j + 1, 1 - slot, 1)
        softmax(slot, False, second)

    def pair(jp, carry):
        step(2 * jp, 0)
        step(2 * jp + 1, 1)
        return carry

    def finish(slot):
        weighted_values(jnp.maximum(i - 1, 0), 1 - slot)
        softmax(slot, True, range(n_strips))
        weighted_values(i, slot)
        o_ref[0] = jnp.transpose(acc_scr[...] / l_scr[...]).astype(o_ref.dtype)

    lax.fori_loop(0, i // 2, pair, 0)

    @pl.when(i % 2 == 0)
    def _():
        finish(0)

    @pl.when(i % 2 == 1)
    def _():
        step(i - 1, 0)
        finish(1)


def _flash(q, k, vt, tq):
    B, S, _ = q.shape
    H = MLA_HEADS
    return pl.pallas_call(
        functools.partial(_flash_kernel, tq=tq),
        grid=(B, H, S // tq),
        in_specs=[
            pl.BlockSpec((1, tq, QK_PAD), lambda b, h, i: (b, i, h)),
            pl.BlockSpec((1, S, QK_PAD), lambda b, h, i: (b, 0, h)),
            pl.BlockSpec((V_HEAD, S), lambda b, h, i: (h, b)),
        ],
        out_specs=pl.BlockSpec((1, tq, V_HEAD), lambda b, h, i: (b, i, h)),
        out_shape=jax.ShapeDtypeStruct((B, S, H * V_HEAD), BF16),
        scratch_shapes=[pltpu.VMEM((2, tq, tq), F32), pltpu.VMEM((2, tq, tq), BF16),
                        pltpu.VMEM((2, 1, tq), F32),
                        pltpu.VMEM((1, tq), F32), pltpu.VMEM((1, tq), F32),
                        pltpu.VMEM((V_HEAD, tq), F32)],
        compiler_params=_cparams(("parallel", "parallel", "arbitrary")),
        name="flash_attn",
    )(q, k, vt)


def _merge_kernel(oa_ref, ob_ref, wa_ref, wb_ref, ga_ref, gb_ref, o_ref):
    ya = _dot(oa_ref[...], wa_ref[...])
    yb = _dot(ob_ref[...], wb_ref[...])
    ga = _sigmoid(ga_ref[...].astype(F32))
    gb = _sigmoid(gb_ref[...].astype(F32))
    o_ref[...] = (ga * ya + gb * yb).astype(o_ref.dtype)


def _merge(oa, ob, wa, wb, proj, tm, tn):
    T, K = oa.shape
    N = wa.shape[1]
    ca = COL_GA // tn
    cb = COL_GB // tn
    return pl.pallas_call(
        _merge_kernel,
        grid=(T // tm, N // tn),
        in_specs=[
            pl.BlockSpec((tm, K), lambda i, j: (i, 0)),
            pl.BlockSpec((tm, K), lambda i, j: (i, 0)),
            pl.BlockSpec((K, tn), lambda i, j: (0, j)),
            pl.BlockSpec((K, tn), lambda i, j: (0, j)),
            pl.BlockSpec((tm, tn), lambda i, j: (i, ca + j)),
            pl.BlockSpec((tm, tn), lambda i, j: (i, cb + j)),
        ],
        out_specs=pl.BlockSpec((tm, tn), lambda i, j: (i, j)),
        out_shape=jax.ShapeDtypeStruct((T, N), BF16),
        compiler_params=_cparams(("parallel", "parallel")),
        name="merge",
    )(oa, ob, wa, wb, proj, proj)


def _pack_rows(x, out_ref, n, row0=0):
    for c in range(PACK_ROWS):
        lo = x[:, c * 2 * LANES:c * 2 * LANES + LANES]
        hi = x[:, c * 2 * LANES + LANES:(c + 1) * 2 * LANES]
        lo_b = pltpu.bitcast(lo.astype(BF16).astype(F32), U32)
        hi_b = pltpu.bitcast(hi.astype(BF16).astype(F32), U32)
        out_ref[pl.ds(row0 * PACK_ROWS + c, n, stride=PACK_ROWS), :] = hi_b | (lo_b >> 16)


def _unpack_rows(ref, c, tm, lead=None):
    idx = (pl.ds(c, tm, stride=PACK_ROWS), slice(None))
    if lead is not None:
        idx = (lead,) + idx
    u = ref[idx]
    lo = pltpu.bitcast(u << 16, F32)
    hi = pltpu.bitcast(u & jnp.uint32(0xFFFF0000), F32)
    return lo, hi


def _wo_kernel(m_ref, x_ref, wo_ref, g_ref, wr_ref, x1_ref, h2_ref, h2p_ref, lg_ref, *, tm):
    hw = WO_ROWS
    groups = [slice(h * hw, (h + 1) * hw) for h in range(tm // hw)]
    wo = wo_ref[...]
    x1 = [x_ref[r, :] + _dot(m_ref[r, :], wo) for r in groups]
    h2 = [_rms(v, g_ref[...]) for v in x1]
    w_hi, w_lo = _split_bf16(wr_ref[...])
    for h, r in enumerate(groups):
        x1_ref[r, :] = x1[h]
        h2_ref[r, :] = h2[h].astype(BF16)
        _pack_rows(h2[h], h2p_ref, hw, row0=h * hw)
        h_hi, h_lo = _split_bf16(h2[h])
        lg_ref[:, r] = _dot_nt(w_hi, h_hi) + _dot_nt(w_hi, h_lo) + _dot_nt(w_lo, h_hi)


def _wo(merged, x2, wo, g, wr_t, tm):
    T, D = x2.shape
    E = wr_t.shape[0]
    return pl.pallas_call(
        functools.partial(_wo_kernel, tm=tm),
        grid=(T // tm,),
        in_specs=[
            pl.BlockSpec((tm, D), lambda i: (i, 0)),
            pl.BlockSpec((tm, D), lambda i: (i, 0)),
            pl.BlockSpec((D, D), lambda i: (0, 0), pipeline_mode=pl.Buffered(1)),
            pl.BlockSpec((1, D), lambda i: (0, 0)),
            pl.BlockSpec((E, D), lambda i: (0, 0)),
        ],
        out_specs=[
            pl.BlockSpec((tm, D), lambda i: (i, 0)),
            pl.BlockSpec((tm, D), lambda i: (i, 0)),
            pl.BlockSpec((tm * PACK_ROWS, PACK_W), lambda i: (i, 0)),
            pl.BlockSpec((E, tm), lambda i: (0, i)),
        ],
        out_shape=[jax.ShapeDtypeStruct((T, D), F32),
                   jax.ShapeDtypeStruct((T, D), BF16),
                   jax.ShapeDtypeStruct((T * PACK_ROWS, PACK_W), U32),
                   jax.ShapeDtypeStruct((E, T), F32)],
        compiler_params=_cparams(("parallel",)),
        name="wo_norm_router",
    )(merged, x2, wo, g, wr_t)


def _route_kernel(lg_ref, bias_ref, eidx_ref, wts_ref, rnk_ref, cnt_ref, carry_scr, *, tm):
    E = N_EXPERTS
    NEG = -jnp.inf

    @pl.when(pl.program_id(0) == 0)
    def _():
        carry_scr[...] = jnp.zeros_like(carry_scr)

    scores = _sigmoid(lg_ref[...])
    choice = scores + bias_ref[...]
    c3 = choice.reshape(N_GROUPS, GROUP_SIZE, tm)
    sub = lax.broadcasted_iota(I32, c3.shape, 1)
    m1 = jnp.max(c3, axis=1, keepdims=True)
    i1 = jnp.min(jnp.where(c3 == m1, sub, GROUP_SIZE), axis=1, keepdims=True)
    m2 = jnp.max(jnp.where(sub == i1, NEG, c3), axis=1, keepdims=True)
    gs = (m1 + m2).reshape(N_GROUPS, tm)
    gi = lax.broadcasted_iota(I32, gs.shape, 0)
    gsel = jnp.zeros(gs.shape, F32)
    for _ in range(TOPK_GROUPS):
        mx = jnp.max(gs, axis=0, keepdims=True)
        ix = jnp.min(jnp.where(gs == mx, gi, N_GROUPS), axis=0, keepdims=True)
        hit = gi == ix
        gsel = jnp.where(hit, 1.0, gsel)
        gs = jnp.where(hit, NEG, gs)
    emask = jnp.broadcast_to(gsel.reshape(N_GROUPS, 1, tm), (N_GROUPS, GROUP_SIZE, tm)).reshape(E, tm)
    x = jnp.where(emask > 0.5, choice, NEG)
    ei = lax.broadcasted_iota(I32, x.shape, 0)
    sel = jnp.zeros(x.shape, F32)
    idx_rows, w_rows = [], []
    for _ in range(TOP_K):
        mx = jnp.max(x, axis=0, keepdims=True)
        ix = jnp.min(jnp.where(x == mx, ei, E), axis=0, keepdims=True)
        hit = ei == ix
        w_rows.append(jnp.sum(jnp.where(hit, scores, 0.0), axis=0, keepdims=True))
        idx_rows.append(ix)
        sel = jnp.where(hit, 1.0, sel)
        x = jnp.where(hit, NEG, x)
    w = jnp.concatenate(w_rows, axis=0)
    w = w / (jnp.sum(w, axis=0, keepdims=True) + 1e-20) * ROUTED_SCALE
    eidx_ref[...] = jnp.concatenate(idx_rows, axis=0)
    wts_ref[...] = w

    upper = (lax.broadcasted_iota(I32, (tm, tm), 0) < lax.broadcasted_iota(I32, (tm, tm), 1)).astype(BF16)
    base = carry_scr[...][:, 0:1]
    excl = _dot(sel.astype(BF16), upper) + base
    rnk_ref[...] = jnp.concatenate(
        [jnp.sum(jnp.where(ei == ix, excl, 0.0), axis=0, keepdims=True) for ix in idx_rows],
        axis=0).astype(I32)
    carry_scr[...] = carry_scr[...] + jnp.sum(sel, axis=1, keepdims=True)
    cnt_ref[...] = carry_scr[...]


def _route(logits_t, bias, tm):
    E, T = logits_t.shape
    K = TOP_K
    tok = lambda: pl.BlockSpec((K, tm), lambda i: (0, i))
    return pl.pallas_call(
        functools.partial(_route_kernel, tm=tm),
        grid=(T // tm,),
        in_specs=[pl.BlockSpec((E, tm), lambda i: (0, i)),
                  pl.BlockSpec((E, 1), lambda i: (0, 0))],
        out_specs=[tok(), tok(), tok(), pl.BlockSpec((E, LANES), lambda i: (0, 0))],
        out_shape=[jax.ShapeDtypeStruct((K, T), I32), jax.ShapeDtypeStruct((K, T), F32),
                   jax.ShapeDtypeStruct((K, T), I32), jax.ShapeDtypeStruct((E, LANES), F32)],
        scratch_shapes=[pltpu.VMEM((E, LANES), F32)],
        compiler_params=_cparams(("arbitrary",)),
        name="route",
    )(logits_t, bias)


def _slots_kernel(pst_ref, eidx_ref, rnk_ref, dest_ref):
    eidx = eidx_ref[...]
    dest = rnk_ref[...]
    for e in range(N_EXPERTS):
        dest = dest + jnp.where(eidx == e, pst_ref[e], 0)
    dest_ref[...] = dest


def _slots(eidx, rnk, pstart, tm):
    K, T = eidx.shape
    tok = lambda: pl.BlockSpec((K, tm), lambda i: (0, i))
    return pl.pallas_call(
        _slots_kernel,
        grid=(T // tm,),
        in_specs=[pl.BlockSpec(memory_space=pltpu.SMEM), tok(), tok()],
        out_specs=tok(),
        out_shape=jax.ShapeDtypeStruct((K, T), I32),
        compiler_params=_cparams(("parallel",)),
        name="slots",
    )(pstart, eidx, rnk)


def _ffn_kernel(te_ref, nu_ref, tok_ref, dst_ref, h_hbm, wg_ref, wu_ref, wd_ref, ys_hbm,
                gbuf, obuf, wg_b, wu_b, wd_b, gsem, osem, *, tm):
    s = pl.program_id(0)
    nu = nu_ref[0]
    tile_rows = tm * PACK_ROWS

    def gather(slot):
        for r in range(tm):
            pltpu.make_async_copy(
                h_hbm.at[pl.ds(pl.multiple_of(tok_ref[0, 0, r] * PACK_ROWS, PACK_ROWS), PACK_ROWS), :],
                gbuf.at[slot, pl.ds(r * PACK_ROWS, PACK_ROWS), :], gsem.at[slot]).start()

    def scatter(slot):
        for r in range(tm):
            pltpu.make_async_copy(
                obuf.at[slot, pl.ds(r * PACK_ROWS, PACK_ROWS), :],
                ys_hbm.at[pl.ds(pl.multiple_of(dst_ref[0, 0, r] * PACK_ROWS, PACK_ROWS), PACK_ROWS), :],
                osem.at[slot]).start()

    def wait_gather(slot):
        pltpu.make_async_copy(h_hbm.at[pl.ds(0, tile_rows), :], gbuf.at[slot], gsem.at[slot]).wait()

    def wait_scatter(slot):
        pltpu.make_async_copy(obuf.at[slot], ys_hbm.at[pl.ds(0, tile_rows), :], osem.at[slot]).wait()

    def compute(slot):
        parts = []
        for c in range(PACK_ROWS):
            lo, hi = _unpack_rows(gbuf.at[slot], c, tm)
            parts += [lo.astype(BF16), hi.astype(BF16)]
        a = jnp.concatenate(parts, axis=1)
        hg = _dot(a, wg_b[...])
        hu = _dot(a, wu_b[...])
        hid = (hg * _sigmoid(hg) * hu).astype(BF16)
        _pack_rows(_dot(hid, wd_b[...]), obuf.at[slot], tm)

    cur = jnp.maximum(s - 1, 0)
    @pl.when((s >= 1) & (s <= nu) & ((s == 1) | (te_ref[cur] != te_ref[jnp.maximum(s - 2, 0)])))
    def _():
        wg_b[...] = wg_ref[0].astype(BF16)
        wu_b[...] = wu_ref[0].astype(BF16)
        wd_b[...] = wd_ref[0].astype(BF16)

    @pl.when(s == 0)
    def _():
        gather(0)

    @pl.when(s == 1)
    def _():
        wait_gather(0)
        gather(1)
        compute(0)

    for slot in range(2):
        parity = s % 2 == slot

        @pl.when(parity & (s >= 3) & (s <= nu + 1))
        def _():
            wait_scatter(1 - slot)

        @pl.when(parity & (s >= 2) & (s <= nu))
        def _():
            wait_gather(1 - slot)
            gather(slot)
            scatter(slot)
            compute(1 - slot)

        @pl.when(parity & (s >= 2) & (s == nu + 1))
        def _():
            wait_gather(1 - slot)
            scatter(slot)
            wait_scatter(slot)


def _ffn(tile_e, n_used, row_tok, row_dst, h2p, wg, wu, wd, tm):
    nt = row_tok.shape[0]
    D, I = wg.shape[1], wg.shape[2]
    cur = lambda s: jnp.maximum(s - 1, 0)
    wspec = lambda shape: pl.BlockSpec(shape, lambda s, te, nu: (te[cur(s)], 0, 0))
    grid_spec = pltpu.PrefetchScalarGridSpec(
        num_scalar_prefetch=2,
        grid=(nt + 1,),
        in_specs=[
            pl.BlockSpec((1, 1, tm), lambda s, te, nu: (jnp.minimum(s, nt - 1), 0, 0),
                         memory_space=pltpu.SMEM),
            pl.BlockSpec((1, 1, tm), lambda s, te, nu: (jnp.maximum(s - 2, 0), 0, 0),
                         memory_space=pltpu.SMEM),
            pl.BlockSpec(memory_space=pl.ANY),
            wspec((1, D, I)), wspec((1, D, I)), wspec((1, I, D)),
        ],
        out_specs=pl.BlockSpec(memory_space=pl.ANY),
        scratch_shapes=[pltpu.VMEM((2, tm * PACK_ROWS, PACK_W), U32),
                        pltpu.VMEM((2, tm * PACK_ROWS, PACK_W), U32),
                        pltpu.VMEM((D, I), BF16), pltpu.VMEM((D, I), BF16), pltpu.VMEM((I, D), BF16),
                        pltpu.SemaphoreType.DMA((2,)), pltpu.SemaphoreType.DMA((2,))],
    )
    return pl.pallas_call(
        functools.partial(_ffn_kernel, tm=tm),
        grid_spec=grid_spec,
        out_shape=jax.ShapeDtypeStruct((nt * tm * PACK_ROWS, PACK_W), U32),
        compiler_params=_cparams(("arbitrary",)),
        name="expert_ffn",
    )(tile_e, n_used, row_tok, row_dst, h2p, wg, wu, wd)


def _combine_kernel(*refs, tm):
    ys_refs = refs[:TOP_K]
    wt_ref, x1_ref, h2_ref, wsg_ref, wsu_ref, wsd_ref, o_ref = refs[TOP_K:]
    h2 = h2_ref[...]
    hg = _dot(h2, wsg_ref[...])
    hu = _dot(h2, wsu_ref[...])
    base = x1_ref[...] + _dot((hg * _sigmoid(hg) * hu).astype(BF16), wsd_ref[...])
    wt = wt_ref[...]
    for c in range(PACK_ROWS):
        acc_lo = base[:, c * 2 * LANES:c * 2 * LANES + LANES]
        acc_hi = base[:, c * 2 * LANES + LANES:(c + 1) * 2 * LANES]
        for k in range(TOP_K):
            lo, hi = _unpack_rows(ys_refs[k], c, tm)
            wk = wt[:, k:k + 1]
            acc_lo = acc_lo + wk * lo
            acc_hi = acc_hi + wk * hi
        o_ref[:, c * 2 * LANES:c * 2 * LANES + LANES] = acc_lo
        o_ref[:, c * 2 * LANES + LANES:(c + 1) * 2 * LANES] = acc_hi


def _combine(ys, wts_t, x1, h2, wsg, wsu, wsd, tm):
    T, D = x1.shape
    K = wts_t.shape[1]
    I = wsg.shape[1]
    full = lambda shape: pl.BlockSpec(shape, lambda i: (0, 0))
    return pl.pallas_call(
        functools.partial(_combine_kernel, tm=tm),
        grid=(T // tm,),
        in_specs=[pl.BlockSpec((tm * PACK_ROWS, PACK_W), lambda i, k=k: (k * (T // tm) + i, 0))
                  for k in range(K)] + [
                  pl.BlockSpec((tm, K), lambda i: (i, 0)),
                  pl.BlockSpec((tm, D), lambda i: (i, 0)),
                  pl.BlockSpec((tm, D), lambda i: (i, 0)),
                  full((D, I)), full((D, I)), full((I, D))],
        out_specs=pl.BlockSpec((tm, D), lambda i: (i, 0)),
        out_shape=jax.ShapeDtypeStruct((T, D), F32),
        compiler_params=_cparams(("parallel",)),
        name="combine",
    )(*([ys] * K), wts_t, x1, h2, wsg, wsu, wsd)


def _ple_kernel(xf_ref, xt_ref, p_ref, wp_ref, gpost_ref, gin_ref, wg_ref, o_ref,
                hn_scr, ple_scr, *, tn):
    j = pl.program_id(1)

    @pl.when(j == 0)
    def _():
        hn_scr[...] = _rms(xf_ref[...], gin_ref[...]).astype(BF16)
        ple_scr[...] = _rms(_dot(p_ref[...].astype(BF16), wp_ref[...]), gpost_ref[...])

    cols = pl.ds(pl.multiple_of(j * tn, tn), tn)
    gate = _sigmoid(_dot(hn_scr[...], wg_ref[...]))
    o_ref[...] = xt_ref[...] + gate * ple_scr[:, cols]


def _ple(x2, p2, wp, gpost, gin, wg, tm, tn):
    T, D = x2.shape
    Pd = p2.shape[1]
    return pl.pallas_call(
        functools.partial(_ple_kernel, tn=tn),
        grid=(T // tm, D // tn),
        in_specs=[
            pl.BlockSpec((tm, D), lambda i, j: (i, 0)),
            pl.BlockSpec((tm, tn), lambda i, j: (i, j)),
            pl.BlockSpec((tm, Pd), lambda i, j: (i, 0)),
            pl.BlockSpec((Pd, D), lambda i, j: (0, 0)),
            pl.BlockSpec((1, D), lambda i, j: (0, 0)),
            pl.BlockSpec((1, D), lambda i, j: (0, 0)),
            pl.BlockSpec((D, tn), lambda i, j: (0, j)),
        ],
        out_specs=pl.BlockSpec((tm, tn), lambda i, j: (i, j)),
        out_shape=jax.ShapeDtypeStruct((T, D), F32),
        scratch_shapes=[pltpu.VMEM((tm, D), BF16), pltpu.VMEM((tm, D), F32)],
        compiler_params=_cparams(("parallel", "arbitrary")),
        name="ple",
    )(x2, x2, p2, wp, gpost, gin, wg)


def _pad_cols(w, n):
    return jnp.pad(w, ((0, 0), (0, n - w.shape[1])))


def _pad_rows(w, n):
    return jnp.pad(w, ((0, n - w.shape[0]), (0, 0)))


def _layout_w_in(w):
    W = RWKV_WIDTH
    c = 3 * W
    segs = [w[:, 0:c],
            _pad_cols(w[:, c:c + DECAY_LORA], LORA_PAD),
            _pad_cols(w[:, c + DECAY_LORA:c + DECAY_LORA + AAA_LORA], LORA_PAD),
            _pad_cols(w[:, c + DECAY_LORA + AAA_LORA:c + DECAY_LORA + AAA_LORA + GATE_LORA], GATE_LORA_PAD)]
    c += DECAY_LORA + AAA_LORA + GATE_LORA
    mla_cols = Q_LORA + KV_LORA + QK_ROPE
    segs.append(_pad_cols(w[:, c:c + mla_cols], MLA_PAD))
    c += mla_cols
    segs.append(w[:, c:])
    out = jnp.concatenate(segs, axis=1).astype(BF16)
    assert out.shape[1] == IN_PAD
    return out


def _layout_mu(mu):
    W = RWKV_WIDTH
    c = 3 * W
    segs = [mu[0:c],
            jnp.pad(mu[c:c + DECAY_LORA], (0, LORA_PAD - DECAY_LORA)),
            jnp.pad(mu[c + DECAY_LORA:c + DECAY_LORA + AAA_LORA], (0, LORA_PAD - AAA_LORA)),
            jnp.pad(mu[c + DECAY_LORA + AAA_LORA:], (0, GATE_LORA_PAD - GATE_LORA))]
    return jnp.concatenate(segs).reshape(1, RWKV_PAD)


def _moe_tables(eidx, rnk, counts, tmf, tm_slots):
    K, T = eidx.shape
    E = N_EXPERTS
    A = K * T
    n_tiles = A // tmf + E
    P = n_tiles * tmf
    pcounts = (counts + tmf - 1) // tmf * tmf
    pends = jnp.cumsum(pcounts)
    pstart = (pends - pcounts).astype(I32)
    tile_start = jnp.arange(n_tiles, dtype=I32) * tmf
    tile_e = jnp.minimum(jnp.sum((pends[None, :] <= tile_start[:, None]).astype(I32), axis=1), E - 1)
    n_used = (pends[E - 1:] // tmf).astype(I32)
    dest = _slots(eidx, rnk, pstart, tm_slots)
    j = jnp.arange(tmf, dtype=I32)[None, :]
    tail = j < (pcounts - counts)[:, None]
    spare = jnp.cumsum(jnp.logical_not(tail).reshape(-1).astype(I32)) - 1
    pad_row = jnp.where(tail.reshape(-1), ((pstart + counts)[:, None] + j).reshape(-1),
                        pends[E - 1] + spare)
    q = jnp.arange(P - A, dtype=I32)
    slot_id = jnp.arange(A, dtype=I32)
    rows = jnp.concatenate([dest.reshape(-1), pad_row.astype(I32)])
    slots = jnp.concatenate([slot_id, A + q])
    row_dst = lax.sort((rows, slots), num_keys=1)[1]
    row_tok = jnp.where(row_dst < A, row_dst % T, 0)
    return (row_tok.reshape(n_tiles, 1, tmf), row_dst.reshape(n_tiles, 1, tmf), tile_e, n_used)


def _layer(x, p, positions, g_mix, w_in, mu_rwkv, w0, w2, a0, a2, g2, k_k, k_a, r_k,
           gn_w, gn_b, w_a_up, g_qa, g_kva, w_uq, w_ukv, g_qn, g_kn, w_b_up, w_o,
           g_ffn, w_router, router_bias, w_exp_gate, w_exp_up, w_exp_down,
           w_sh_gate, w_sh_up, w_sh_down, w_ple, g_ple_post, g_ple_in, w_ple_gate,
           *, tiles):
    B, S, D = x.shape
    T = B * S
    W = RWKV_WIDTH
    x2 = x.reshape(T, D)
    row = lambda v: v.reshape(1, -1).astype(F32)

    proj = _inproj(x2, row(g_mix), _layout_w_in(w_in), tiles["tm_in"], tiles["tn_in"])

    head_blk = (jnp.arange(W)[:, None] // RWKV_HEAD == jnp.arange(W)[None, :] // RWKV_HEAD).astype(BF16)
    r, k, v, kk, a, lw, g = _rwkv_prep(
        proj, _layout_mu(mu_rwkv), row(w0), _pad_rows(w2, LORA_PAD).astype(BF16), row(a0),
        _pad_rows(a2, LORA_PAD).astype(BF16), _pad_rows(g2, GATE_LORA_PAD).astype(BF16),
        row(k_k), row(k_a), head_blk, tiles["tm_prep"], S)
    b3 = lambda t: t.reshape(B, S, W)
    o_a = _rwkv_scan(b3(r), b3(k), b3(v), b3(kk), b3(a), b3(lw), b3(g),
                     row(gn_w), row(gn_b), row(r_k), tiles["rows_scan"]).reshape(T, W)

    half = QK_ROPE // 2
    inv_freq = ROPE_THETA ** (-jnp.arange(half, dtype=F32) / half)
    invf = jnp.concatenate([inv_freq, inv_freq, jnp.zeros((LANES - QK_ROPE,), F32)]).reshape(1, LANES)
    wuq = jnp.pad(w_uq.reshape(Q_LORA, MLA_HEADS, QK_HEAD),
                  ((0, 0), (0, 0), (0, QK_PAD - QK_HEAD))).reshape(Q_LORA, MLA_HEADS * QK_PAD)
    pad_g = lambda gv: jnp.pad(gv, (0, QK_PAD - QK_HEAD)).reshape(1, QK_PAD)
    q, kx, vx = _mla_prep(proj, positions.reshape(T, 1).astype(I32), invf, row(g_qa), row(g_kva),
                          wuq.astype(BF16), w_ukv.astype(BF16), pad_g(g_qn), pad_g(g_kn),
                          tiles["tm_mla"])
    o_b = _flash(q.reshape(B, S, -1), kx.reshape(B, S, -1), vx, tiles["tq"]).reshape(T, MLA_WIDTH)

    merged = _merge(o_a, o_b, w_a_up.astype(BF16), w_b_up.astype(BF16), proj,
                    tiles["tm_merge"], tiles["tn_merge"])

    x1, h2, h2p, logits_t = _wo(merged, x2, w_o.astype(BF16), row(g_ffn),
                                jnp.transpose(w_router).astype(F32), tiles["tm_wo"])
    eidx, wts, rnk, cnt = _route(logits_t, router_bias.reshape(N_EXPERTS, 1).astype(F32),
                                 tiles["tm_route"])
    tmf = tiles["tm_ffn"]
    counts = cnt[:, 0].astype(I32)
    row_tok, row_dst, tile_e, n_used = _moe_tables(eidx, rnk, counts, tmf, tiles["tm_slots"])
    ys = _ffn(tile_e, n_used, row_tok, row_dst, h2p, w_exp_gate, w_exp_up, w_exp_down, tmf)
    x3 = _combine(ys, jnp.transpose(wts), x1, h2,
                  w_sh_gate.astype(BF16), w_sh_up.astype(BF16), w_sh_down.astype(BF16),
                  tiles["tm_comb"])

    out = _ple(x3, p.reshape(T, PLE_DIM), w_ple.astype(BF16), row(g_ple_post), row(g_ple_in),
               w_ple_gate.astype(BF16), tiles["tm_ple"], tiles["tn_ple"])
    return out.reshape(B, S, D)


TILES = dict(tm_in=1024, tn_in=512, tm_prep=256, rows_scan=512, tm_mla=512, tq=512,
             tm_merge=512, tn_merge=512, tm_wo=512, tm_route=512, tm_ffn=256,
             tm_slots=2048, tm_comb=128, tm_ple=512, tn_ple=1024)


def kernel(x, p, positions, g_mix, w_in, mu_rwkv, w0, w2, a0, a2, g2, k_k, k_a, r_k, gn_w, gn_b, w_a_up, g_qa, g_kva, w_uq, w_ukv, g_qn, g_kn, w_b_up, w_o, g_ffn, w_router, router_bias, w_exp_gate, w_exp_up, w_exp_down, w_sh_gate, w_sh_up, w_sh_down, w_ple, g_ple_post, g_ple_in, w_ple_gate):
    args = (g_mix, w_in, mu_rwkv, w0, w2, a0, a2, g2, k_k, k_a, r_k, gn_w, gn_b, w_a_up,
            g_qa, g_kva, w_uq, w_ukv, g_qn, g_kn, w_b_up, w_o, g_ffn, w_router, router_bias,
            w_exp_gate, w_exp_up, w_exp_down, w_sh_gate, w_sh_up, w_sh_down, w_ple,
            g_ple_post, g_ple_in, w_ple_gate)
    assert all(t.shape[0] == 1 for t in args), "single-layer stack expected"
    return _layer(x, p[0], positions, *[t[0] for t in args], tiles=TILES)
```

```python
import functools

import jax
import jax.numpy as jnp
from jax import lax
from jax.experimental import pallas as pl
from jax.experimental.pallas import tpu as pltpu

F32 = jnp.float32
BF16 = jnp.bfloat16
I32 = jnp.int32
U32 = jnp.uint32

D_MODEL = 2048
PLE_DIM = 256
NORM_EPS = 1e-6
RWKV_HEADS = 16
RWKV_HEAD = 64
RWKV_WIDTH = RWKV_HEADS * RWKV_HEAD
DECAY_LORA = 64
AAA_LORA = 64
GATE_LORA = 160
GN_EPS = 64e-5
MLA_HEADS = 8
Q_LORA = 512
KV_LORA = 512
QK_NOPE = 128
QK_ROPE = 64
QK_HEAD = QK_NOPE + QK_ROPE
V_HEAD = 128
MLA_WIDTH = MLA_HEADS * V_HEAD
ROPE_THETA = 10000.0
N_EXPERTS = 64
TOP_K = 8
N_GROUPS = 8
GROUP_SIZE = N_EXPERTS // N_GROUPS
TOPK_GROUPS = 4
MOE_INTER = 512
ROUTED_SCALE = 2.5

LANES = 128
QK_PAD = 2 * LANES
PACK_ROWS = 8
PACK_W = D_MODEL // 2 // PACK_ROWS
assert PACK_W == LANES

LORA_PAD = 128
GATE_LORA_PAD = 256
RWKV_PAD = 3 * RWKV_WIDTH + 2 * LORA_PAD + GATE_LORA_PAD
MLA_PAD = 1536
COL_RWKV = 0
COL_MLA = RWKV_PAD
COL_GA = COL_MLA + MLA_PAD
COL_GB = COL_GA + D_MODEL
IN_PAD = COL_GB + D_MODEL

VMEM_LIMIT = 56 * 1024 * 1024
CHUNK = 64
WO_ROWS = 128
SCAN_LANE_BLOCKS = 2


def _cparams(sem):
    return pltpu.CompilerParams(dimension_semantics=sem, vmem_limit_bytes=VMEM_LIMIT)


def _dot(a, b):
    return jnp.dot(a, b, preferred_element_type=F32)


def _dot_nt(a, b):
    return lax.dot_general(a, b, (((1,), (1,)), ((), ())), preferred_element_type=F32)


def _dot_tn(a, b):
    return lax.dot_general(a, b, (((0,), (0,)), ((), ())), preferred_element_type=F32)


def _split_bf16(x):
    hi = x.astype(BF16)
    lo = (x - hi.astype(F32)).astype(BF16)
    return hi, lo


def _rms(x, g):
    ms = jnp.mean(x * x, axis=-1, keepdims=True)
    return x * lax.rsqrt(ms + NORM_EPS) * g


def _sigmoid(x):
    return 1.0 / (1.0 + jnp.exp(-x))


def _inproj_kernel(x_ref, g_ref, w_ref, o_ref, h_scr):
    @pl.when(pl.program_id(1) == 0)
    def _():
        h_scr[...] = _rms(x_ref[...], g_ref[...]).astype(BF16)

    o_ref[...] = _dot(h_scr[...], w_ref[...]).astype(o_ref.dtype)


def _inproj(x2, g, w, tm, tn):
    T, D = x2.shape
    N = w.shape[1]
    return pl.pallas_call(
        _inproj_kernel,
        grid=(T // tm, N // tn),
        in_specs=[
            pl.BlockSpec((tm, D), lambda i, j: (i, 0)),
            pl.BlockSpec((1, D), lambda i, j: (0, 0)),
            pl.BlockSpec((D, tn), lambda i, j: (0, j)),
        ],
        out_specs=pl.BlockSpec((tm, tn), lambda i, j: (i, j)),
        out_shape=jax.ShapeDtypeStruct((T, N), BF16),
        scratch_shapes=[pltpu.VMEM((tm, D), BF16)],
        compiler_params=_cparams(("parallel", "arbitrary")),
        name="inproj",
    )(x2, g, w)


HALO = 16


def _rwkv_prep_kernel(cur_ref, halo_ref, mu_ref, w0_ref, w2_ref, a0_ref, a2_ref, g2_ref,
                      kk_ref, ka_ref, bd_ref,
                      r_out, k_out, v_out, kk_out, a_out, lw_out, g_out, *, tm, seq):
    i = pl.program_id(0)
    cur = cur_ref[...].astype(F32)
    first = (i * tm) % seq == 0
    last_prev = halo_ref[HALO - 1:HALO, :].astype(F32)
    last_prev = jnp.where(first, 0.0, last_prev)
    row = lax.broadcasted_iota(I32, cur.shape, 0)
    prev = jnp.where(row == 0, last_prev, pltpu.roll(cur, 1, 0))
    u = cur + (prev - cur) * mu_ref[...]
    W = RWKV_WIDTH
    r = u[:, 0:W]
    k = u[:, W:2 * W]
    v = u[:, 2 * W:3 * W]
    dw = u[:, 3 * W:3 * W + LORA_PAD]
    da = u[:, 3 * W + LORA_PAD:3 * W + 2 * LORA_PAD]
    dg = u[:, 3 * W + 2 * LORA_PAD:3 * W + 2 * LORA_PAD + GATE_LORA_PAD]
    z = w0_ref[...] + _dot(jnp.tanh(dw).astype(BF16), w2_ref[...])
    w_log = -(jnp.maximum(-z, 0.0) + jnp.log(1.0 + jnp.exp(-jnp.abs(z)))) - 0.5
    lw_out[...] = -jnp.exp(w_log)
    a = _sigmoid(a0_ref[...] + _dot(da.astype(BF16), a2_ref[...]))
    g_out[...] = _dot(_sigmoid(dg).astype(BF16), g2_ref[...]).astype(g_out.dtype)
    kk = k * kk_ref[...]
    hi, lo = _split_bf16(kk * kk)
    ssq = _dot(hi, bd_ref[...]) + _dot(lo, bd_ref[...])
    kk = kk / jnp.maximum(jnp.sqrt(ssq), 1e-12)
    r_out[...] = r.astype(r_out.dtype)
    k_out[...] = (k * (1.0 + (a - 1.0) * ka_ref[...])).astype(k_out.dtype)
    v_out[...] = v.astype(v_out.dtype)
    kk_out[...] = kk.astype(kk_out.dtype)
    a_out[...] = a.astype(a_out.dtype)


def _rwkv_prep(proj, mu, w0, w2, a0, a2, g2, k_k, k_a, bd, tm, seq):
    T = proj.shape[0]
    W = RWKV_WIDTH
    full = lambda shape: pl.BlockSpec(shape, lambda i: (0, 0))
    out = lambda: pl.BlockSpec((tm, W), lambda i: (i, 0))
    return pl.pallas_call(
        functools.partial(_rwkv_prep_kernel, tm=tm, seq=seq),
        grid=(T // tm,),
        in_specs=[
            pl.BlockSpec((tm, RWKV_PAD), lambda i: (i, 0)),
            pl.BlockSpec((HALO, RWKV_PAD), lambda i: (jnp.maximum(i * (tm // HALO) - 1, 0), 0)),
            full((1, RWKV_PAD)), full((1, W)), full((LORA_PAD, W)), full((1, W)),
            full((LORA_PAD, W)), full((GATE_LORA_PAD, W)), full((1, W)), full((1, W)),
            full((W, W)),
        ],
        out_specs=[out() for _ in range(7)],
        out_shape=[jax.ShapeDtypeStruct((T, W), dt)
                   for dt in (BF16, BF16, BF16, BF16, BF16, F32, BF16)],
        compiler_params=_cparams(("parallel",)),
        name="rwkv_prep",
    )(proj, proj, mu, w0, w2, a0, a2, g2, k_k, k_a, bd)


def _rwkv_scan_kernel(r_ref, k_ref, v_ref, kk_ref, a_ref, lw_ref, g_ref,
                      gnw_ref, gnb_ref, rk_ref, o_ref, s_scr, *, rows):
    L = CHUNK
    L2 = 2 * L

    @pl.when(pl.program_id(2) == 0)
    def _():
        s_scr[...] = jnp.zeros_like(s_scr)

    lane = lax.broadcasted_iota(I32, (L, LANES), 1)
    m_a = (lane < RWKV_HEAD).astype(F32)
    m_b = 1.0 - m_a
    ri = lax.broadcasted_iota(I32, (L2, L2), 0)
    ci = lax.broadcasted_iota(I32, (L2, L2), 1)
    strict = ri > ci
    incl = ri >= ci
    eye = (ri == ci).astype(F32)
    tri = (lax.broadcasted_iota(I32, (L, L), 0) >= lax.broadcasted_iota(I32, (L, L), 1)).astype(BF16)
    hr = lax.broadcasted_iota(I32, (LANES, LANES), 0) // RWKV_HEAD
    hc = lax.broadcasted_iota(I32, (LANES, LANES), 1) // RWKV_HEAD
    head_ones = (hr == hc).astype(BF16)
    nb = SCAN_LANE_BLOCKS

    def stack(x):
        return jnp.concatenate([x * m_a, x * m_b], axis=0)

    chunks = range(rows // L)
    rows_of = lambda c: slice(c * L, (c + 1) * L)
    lanes_of = lambda h: slice(h * LANES, (h + 1) * LANES)
    each = lambda f, *lists: [f(*vals) for vals in zip(*lists)]

    def operands(c, h):
        sl, ln = rows_of(c), lanes_of(h)
        r = r_ref[0, sl, ln].astype(F32)
        k = k_ref[0, sl, ln].astype(F32)
        v = v_ref[0, sl, ln].astype(F32)
        kk = kk_ref[0, sl, ln].astype(F32)
        a = a_ref[0, sl, ln].astype(F32)
        lw = lw_ref[0, sl, ln]
        hi, lo = _split_bf16(lw)
        cum = _dot(tri, hi) + _dot(tri, lo)
        c_end = cum[L - 1:L, :]
        p_inv = jnp.exp(-cum)
        p_end = jnp.exp(c_end - cum)
        b = kk * a
        return dict(
            x_a=stack(-(kk * jnp.exp(cum - lw))).astype(BF16),
            x_r=stack(r * jnp.exp(cum)),
            x_bk=jnp.concatenate([stack(b * p_inv), stack(k * p_inv)], axis=0).astype(BF16),
            v_st=stack(v).astype(BF16),
            z_hat=jnp.concatenate([stack(b * p_end), stack(k * p_end)], axis=0).astype(BF16),
            p_row=jnp.exp(c_end),
            rkr=(r * k * rk_ref[:, ln]).astype(BF16), v=v)

    ops = [operands(c, h) for c in chunks for h in range(nb)]
    big = [_dot_nt(jnp.concatenate([o["x_a"], o["x_r"].astype(BF16)], axis=0), o["x_bk"])
           for o in ops]
    n_pow = [jnp.where(strict, m[0:L2, 0:L2], 0.0) for m in big]
    a_ak = [jnp.where(strict, m[0:L2, L2:2 * L2], 0.0).astype(BF16) for m in big]
    a_r = [jnp.concatenate([jnp.where(incl, m[L2:2 * L2, 0:L2], 0.0),
                            jnp.where(incl, m[L2:2 * L2, L2:2 * L2], 0.0)], axis=1).astype(BF16)
           for m in big]

    t_inv = [eye + n for n in n_pow]
    for _ in range(5):
        n_pow = each(lambda n: _dot(n.astype(BF16), n.astype(BF16)), n_pow)
        t_inv = each(lambda t, n: t + _dot(t.astype(BF16), n.astype(BF16)), t_inv, n_pow)

    akv = each(lambda m, o: _dot(m, o["v_st"]).astype(BF16), a_ak, ops)
    w = each(lambda t, o, x: _dot(t.astype(BF16), jnp.concatenate([o["x_a"], x], axis=1)),
             t_inv, ops, akv)
    w1 = [m[:, 0:LANES].astype(BF16) for m in w]
    w2v = each(lambda m, o: jnp.concatenate([m[:, LANES:].astype(BF16), o["v_st"]], axis=0), w, ops)
    g_mat = each(lambda o, m, x: (o["x_r"] + _dot(m[:, 0:L2], x)).astype(BF16), ops, a_r, w1)
    y0 = each(_dot, a_r, w2v)
    m_mat = each(lambda x, o: _dot_tn(x, o["z_hat"][0:L2, :]).astype(BF16), w1, ops)
    c2 = each(lambda x, o: _dot_tn(x, o["z_hat"]), w2v, ops)
    bonus_v = [_dot(o["rkr"], head_ones) * o["v"] for o in ops]

    s = [s_scr[h] for h in range(nb)]
    y = []
    for i in range(len(ops)):
        h = i % nb
        s_b = s[h].astype(BF16)
        y_st = _dot_nt(g_mat[i], s_b) + y0[i]
        s[h] = s[h] * ops[i]["p_row"] + _dot(s_b, m_mat[i]) + c2[i]
        y.append(y_st[0:L, :] + y_st[L:L2, :])
    for h in range(nb):
        s_scr[h] = s[h]

    mean = [_dot(t.astype(BF16), head_ones) * (1.0 / RWKV_HEAD) for t in y]
    yc = each(lambda t, m: t - m, y, mean)
    var = [_dot((t * t).astype(BF16), head_ones) * (1.0 / RWKV_HEAD) for t in yc]
    for i in range(len(ops)):
        sl, ln = rows_of(i // nb), lanes_of(i % nb)
        yn = yc[i] * lax.rsqrt(var[i] + GN_EPS) * gnw_ref[:, ln] + gnb_ref[:, ln]
        o_ref[0, sl, ln] = ((yn + bonus_v[i]) * g_ref[0, sl, ln].astype(F32)).astype(o_ref.dtype)


def _rwkv_scan(r, k, v, kk, a, lw, g, gn_w, gn_b, r_k, rows):
    B, S, W = r.shape
    wb = SCAN_LANE_BLOCKS * LANES
    npair = W // wb
    seq = lambda: pl.BlockSpec((1, rows, wb), lambda b, p, c: (b, c, p))
    par = lambda: pl.BlockSpec((1, wb), lambda b, p, c: (0, p))
    return pl.pallas_call(
        functools.partial(_rwkv_scan_kernel, rows=rows),
        grid=(B, npair, S // rows),
        in_specs=[seq() for _ in range(7)] + [par(), par(), par()],
        out_specs=seq(),
        out_shape=jax.ShapeDtypeStruct((B, S, W), BF16),
        scratch_shapes=[pltpu.VMEM((SCAN_LANE_BLOCKS, LANES, LANES), F32)],
        compiler_params=_cparams(("parallel", "parallel", "arbitrary")),
        name="rwkv_scan",
    )(r, k, v, kk, a, lw, g, gn_w, gn_b, r_k)


def _rope128(x, cos, sin_signed):
    lane = lax.broadcasted_iota(I32, x.shape, 1)
    half = QK_ROPE // 2
    partner = jnp.where(lane < half, pltpu.roll(x, LANES - half, 1), pltpu.roll(x, half, 1))
    return x * cos + partner * sin_signed


def _mla_prep_kernel(cq_ref, ckv_ref, kr_ref, pos_ref, invf_ref, gqa_ref, gkva_ref,
                     wuq_ref, wukv_ref, gqn_ref, gkn_ref,
                     q_out, k_out, v_out, cqn_scr, ckvn_scr, cos_scr, sin_scr):
    @pl.when(pl.program_id(1) == 0)
    def _():
        cqn_scr[...] = _rms(cq_ref[...].astype(F32), gqa_ref[...]).astype(BF16)
        ckvn_scr[...] = _rms(ckv_ref[...].astype(F32), gkva_ref[...]).astype(BF16)
        ang = pos_ref[...].astype(F32) * invf_ref[...]
        lane = lax.broadcasted_iota(I32, ang.shape, 1)
        cos_scr[...] = jnp.cos(ang)
        sin_scr[...] = jnp.where(lane < QK_ROPE // 2, -1.0, 1.0) * jnp.sin(ang)

    cos = cos_scr[...]
    sin = sin_scr[...]
    inv_n = 1.0 / QK_HEAD
    q = _dot(cqn_scr[...], wuq_ref[...])
    rs = lax.rsqrt(jnp.sum(q * q, axis=-1, keepdims=True) * inv_n + NORM_EPS)
    q = q * rs * gqn_ref[...] * (QK_HEAD ** -0.5 * LOG2E)
    q_out[...] = jnp.concatenate(
        [q[:, 0:LANES], _rope128(q[:, LANES:], cos, sin)], axis=1).astype(q_out.dtype)

    kv = _dot(ckvn_scr[...], wukv_ref[...])
    k_nope = kv[:, 0:QK_NOPE]
    kr = kr_ref[...].astype(F32)
    ssq = (jnp.sum(k_nope * k_nope, axis=-1, keepdims=True)
           + jnp.sum(kr * kr, axis=-1, keepdims=True))
    rs = lax.rsqrt(ssq * inv_n + NORM_EPS)
    gkn = gkn_ref[...]
    k_out[...] = jnp.concatenate(
        [k_nope * rs * gkn[:, 0:LANES], _rope128(kr * rs * gkn[:, LANES:], cos, sin)],
        axis=1).astype(k_out.dtype)
    v_out[...] = jnp.transpose(kv[:, QK_NOPE:]).astype(v_out.dtype)


def _mla_prep(proj, pos, invf, g_qa, g_kva, wuq, wukv, gqn, gkn, tm):
    T = proj.shape[0]
    H = MLA_HEADS
    c0 = COL_MLA // Q_LORA
    ckr = (COL_MLA + Q_LORA + KV_LORA) // LANES
    full = lambda shape: pl.BlockSpec(shape, lambda i, h: (0, 0))
    return pl.pallas_call(
        _mla_prep_kernel,
        grid=(T // tm, H),
        in_specs=[
            pl.BlockSpec((tm, Q_LORA), lambda i, h: (i, c0)),
            pl.BlockSpec((tm, KV_LORA), lambda i, h: (i, c0 + 1)),
            pl.BlockSpec((tm, LANES), lambda i, h: (i, ckr)),
            pl.BlockSpec((tm, 1), lambda i, h: (i, 0)),
            full((1, LANES)), full((1, Q_LORA)), full((1, KV_LORA)),
            pl.BlockSpec((Q_LORA, QK_PAD), lambda i, h: (0, h)),
            pl.BlockSpec((KV_LORA, QK_NOPE + V_HEAD), lambda i, h: (0, h)),
            full((1, QK_PAD)), full((1, QK_PAD)),
        ],
        out_specs=[
            pl.BlockSpec((tm, QK_PAD), lambda i, h: (i, h)),
            pl.BlockSpec((tm, QK_PAD), lambda i, h: (i, h)),
            pl.BlockSpec((V_HEAD, tm), lambda i, h: (h, i)),
        ],
        out_shape=[jax.ShapeDtypeStruct((T, H * QK_PAD), BF16),
                   jax.ShapeDtypeStruct((T, H * QK_PAD), BF16),
                   jax.ShapeDtypeStruct((H * V_HEAD, T), BF16)],
        scratch_shapes=[pltpu.VMEM((tm, Q_LORA), BF16), pltpu.VMEM((tm, KV_LORA), BF16),
                        pltpu.VMEM((tm, LANES), F32), pltpu.VMEM((tm, LANES), F32)],
        compiler_params=_cparams(("parallel", "arbitrary")),
        name="mla_prep",
    )(proj, proj, proj, pos, invf, g_qa, g_kva, wuq, wukv, gqn, gkn)


MASK_VALUE = -1e30
LOG2E = 1.4426950408889634


def _flash_kernel(q_ref, k_ref, vt_ref, o_ref, s_scr, p_scr, alpha_scr, m_scr, l_scr, acc_scr,
                  *, tq):
    i = pl.program_id(2)
    q = q_ref[0]
    m_scr[...] = jnp.full_like(m_scr, MASK_VALUE)
    l_scr[...] = jnp.zeros_like(l_scr)
    acc_scr[...] = jnp.zeros_like(acc_scr)

    q_half = tq // 2

    def scores(j, slot, half):
        ks = pl.ds(pl.multiple_of(j * tq, tq), tq)
        cols = slice(half * q_half, (half + 1) * q_half)
        s_scr[slot, :, cols] = _dot_nt(k_ref[0, ks, :], q[cols, :])

    def softmax(slot, masked, strips):
        for c in strips:
            cols = slice(c * LANES, (c + 1) * LANES)
            s = s_scr[slot, :, cols]
            if masked:
                kpos = lax.broadcasted_iota(I32, s.shape, 0)
                qpos = lax.broadcasted_iota(I32, s.shape, 1) + c * LANES
                s = jnp.where(kpos <= qpos, s, MASK_VALUE)
            m_old = m_scr[:, cols]
            m_new = jnp.maximum(m_old, jnp.max(s, axis=0, keepdims=True))
            p = jnp.exp2(s - m_new)
            alpha = jnp.exp2(m_old - m_new)
            l_scr[:, cols] = alpha * l_scr[:, cols] + jnp.sum(p, axis=0, keepdims=True)
            m_scr[:, cols] = m_new
            p_scr[slot, :, cols] = p.astype(BF16)
            alpha_scr[slot, :, cols] = alpha

    def weighted_values(j, slot):
        vt = vt_ref[:, pl.ds(pl.multiple_of(j * tq, tq), tq)]
        acc_scr[...] = alpha_scr[slot] * acc_scr[...] + _dot(vt, p_scr[slot])

    scores(0, 0, 0)
    scores(0, 0, 1)
    p_scr[1] = jnp.zeros((tq, tq), BF16)
    alpha_scr[1] = jnp.ones((1, tq), F32)
    n_strips = tq // LANES
    first, second = range(0, n_strips // 2), range(n_strips // 2, n_strips)

    def step(j, slot):
        weighted_values(jnp.maximum(j - 1, 0), 1 - slot)
        scores(j + 1, 1 - slot, 0)
        softmax(slot, False, first)
        scores(j + 1, 1 - slot, 1)
        softmax(slot, False, second)

    def pair(jp, carry):
        step(2 * jp, 0)
        step(2 * jp + 1, 1)
        return carry

    def finish(slot):
        weighted_values(jnp.maximum(i - 1, 0), 1 - slot)
        softmax(slot, True, range(n_strips))
        weighted_values(i, slot)
        o_ref[0] = jnp.transpose(acc_scr[...] / l_scr[...]).astype(o_ref.dtype)

    lax.fori_loop(0, i // 2, pair, 0)

    @pl.when(i % 2 == 0)
    def _():
        finish(0)

    @pl.when(i % 2 == 1)
    def _():
        step(i - 1, 0)
        finish(1)


def _flash(q, k, vt, tq):
    B, S, _ = q.shape
    H = MLA_HEADS
    return pl.pallas_call(
        functools.partial(_flash_kernel, tq=tq),
        grid=(B, H, S // tq),
        in_specs=[
            pl.BlockSpec((1, tq, QK_PAD), lambda b, h, i: (b, i, h)),
            pl.BlockSpec((1, S, QK_PAD), lambda b, h, i: (b, 0, h)),
            pl.BlockSpec((V_HEAD, S), lambda b, h, i: (h, b)),
        ],
        out_specs=pl.BlockSpec((1, tq, V_HEAD), lambda b, h, i: (b, i, h)),
        out_shape=jax.ShapeDtypeStruct((B, S, H * V_HEAD), BF16),
        scratch_shapes=[pltpu.VMEM((2, tq, tq), F32), pltpu.VMEM((2, tq, tq), BF16),
                        pltpu.VMEM((2, 1, tq), F32),
                        pltpu.VMEM((1, tq), F32), pltpu.VMEM((1, tq), F32),
                        pltpu.VMEM((V_HEAD, tq), F32)],
        compiler_params=_cparams(("parallel", "parallel", "arbitrary")),
        name="flash_attn",
    )(q, k, vt)


def _merge_kernel(oa_ref, ob_ref, wa_ref, wb_ref, ga_ref, gb_ref, o_ref):
    ya = _dot(oa_ref[...], wa_ref[...])
    yb = _dot(ob_ref[...], wb_ref[...])
    ga = _sigmoid(ga_ref[...].astype(F32))
    gb = _sigmoid(gb_ref[...].astype(F32))
    o_ref[...] = (ga * ya + gb * yb).astype(o_ref.dtype)


def _merge(oa, ob, wa, wb, proj, tm, tn):
    T, K = oa.shape
    N = wa.shape[1]
    ca = COL_GA // tn
    cb = COL_GB // tn
    return pl.pallas_call(
        _merge_kernel,
        grid=(T // tm, N // tn),
        in_specs=[
            pl.BlockSpec((tm, K), lambda i, j: (i, 0)),
            pl.BlockSpec((tm, K), lambda i, j: (i, 0)),
            pl.BlockSpec((K, tn), lambda i, j: (0, j)),
            pl.BlockSpec((K, tn), lambda i, j: (0, j)),
            pl.BlockSpec((tm, tn), lambda i, j: (i, ca + j)),
            pl.BlockSpec((tm, tn), lambda i, j: (i, cb + j)),
        ],
        out_specs=pl.BlockSpec((tm, tn), lambda i, j: (i, j)),
        out_shape=jax.ShapeDtypeStruct((T, N), BF16),
        compiler_params=_cparams(("parallel", "parallel")),
        name="merge",
    )(oa, ob, wa, wb, proj, proj)


def _pack_rows(x, out_ref, n, row0=0):
    for c in range(PACK_ROWS):
        lo = x[:, c * 2 * LANES:c * 2 * LANES + LANES]
        hi = x[:, c * 2 * LANES + LANES:(c + 1) * 2 * LANES]
        lo_b = pltpu.bitcast(lo.astype(BF16).astype(F32), U32)
        hi_b = pltpu.bitcast(hi.astype(BF16).astype(F32), U32)
        out_ref[pl.ds(row0 * PACK_ROWS + c, n, stride=PACK_ROWS), :] = hi_b | (lo_b >> 16)


def _unpack_rows(ref, c, tm, lead=None):
    idx = (pl.ds(c, tm, stride=PACK_ROWS), slice(None))
    if lead is not None:
        idx = (lead,) + idx
    u = ref[idx]
    lo = pltpu.bitcast(u << 16, F32)
    hi = pltpu.bitcast(u & jnp.uint32(0xFFFF0000), F32)
    return lo, hi


def _wo_kernel(m_ref, x_ref, wo_ref, g_ref, wr_ref, x1_ref, h2_ref, h2p_ref, lg_ref, *, tm):
    hw = WO_ROWS
    groups = [slice(h * hw, (h + 1) * hw) for h in range(tm // hw)]
    wo = wo_ref[...]
    x1 = [x_ref[r, :] + _dot(m_ref[r, :], wo) for r in groups]
    h2 = [_rms(v, g_ref[...]) for v in x1]
    w_hi, w_lo = _split_bf16(wr_ref[...])
    for h, r in enumerate(groups):
        x1_ref[r, :] = x1[h]
        h2_ref[r, :] = h2[h].astype(BF16)
        _pack_rows(h2[h], h2p_ref, hw, row0=h * hw)
        h_hi, h_lo = _split_bf16(h2[h])
        lg_ref[:, r] = _dot_nt(w_hi, h_hi) + _dot_nt(w_hi, h_lo) + _dot_nt(w_lo, h_hi)


def _wo(merged, x2, wo, g, wr_t, tm):
    T, D = x2.shape
    E = wr_t.shape[0]
    return pl.pallas_call(
        functools.partial(_wo_kernel, tm=tm),
        grid=(T // tm,),
        in_specs=[
            pl.BlockSpec((tm, D), lambda i: (i, 0)),
            pl.BlockSpec((tm, D), lambda i: (i, 0)),
            pl.BlockSpec((D, D), lambda i: (0, 0), pipeline_mode=pl.Buffered(1)),
            pl.BlockSpec((1, D), lambda i: (0, 0)),
            pl.BlockSpec((E, D), lambda i: (0, 0)),
        ],
        out_specs=[
            pl.BlockSpec((tm, D), lambda i: (i, 0)),
            pl.BlockSpec((tm, D), lambda i: (i, 0)),
            pl.BlockSpec((tm * PACK_ROWS, PACK_W), lambda i: (i, 0)),
            pl.BlockSpec((E, tm), lambda i: (0, i)),
        ],
        out_shape=[jax.ShapeDtypeStruct((T, D), F32),
                   jax.ShapeDtypeStruct((T, D), BF16),
                   jax.ShapeDtypeStruct((T * PACK_ROWS, PACK_W), U32),
                   jax.ShapeDtypeStruct((E, T), F32)],
        compiler_params=_cparams(("parallel",)),
        name="wo_norm_router",
    )(merged, x2, wo, g, wr_t)


def _route_kernel(lg_ref, bias_ref, eidx_ref, wts_ref, rnk_ref, cnt_ref, carry_scr, *, tm):
    E = N_EXPERTS
    NEG = -jnp.inf

    @pl.when(pl.program_id(0) == 0)
    def _():
        carry_scr[...] = jnp.zeros_like(carry_scr)

    scores = _sigmoid(lg_ref[...])
    choice = scores + bias_ref[...]
    c3 = choice.reshape(N_GROUPS, GROUP_SIZE, tm)
    sub = lax.broadcasted_iota(I32, c3.shape, 1)
    m1 = jnp.max(c3, axis=1, keepdims=True)
    i1 = jnp.min(jnp.where(c3 == m1, sub, GROUP_SIZE), axis=1, keepdims=True)
    m2 = jnp.max(jnp.where(sub == i1, NEG, c3), axis=1, keepdims=True)
    gs = (m1 + m2).reshape(N_GROUPS, tm)
    gi = lax.broadcasted_iota(I32, gs.shape, 0)
    gsel = jnp.zeros(gs.shape, F32)
    for _ in range(TOPK_GROUPS):
        mx = jnp.max(gs, axis=0, keepdims=True)
        ix = jnp.min(jnp.where(gs == mx, gi, N_GROUPS), axis=0, keepdims=True)
        hit = gi == ix
        gsel = jnp.where(hit, 1.0, gsel)
        gs = jnp.where(hit, NEG, gs)
    emask = jnp.broadcast_to(gsel.reshape(N_GROUPS, 1, tm), (N_GROUPS, GROUP_SIZE, tm)).reshape(E, tm)
    x = jnp.where(emask > 0.5, choice, NEG)
    ei = lax.broadcasted_iota(I32, x.shape, 0)
    sel = jnp.zeros(x.shape, F32)
    idx_rows, w_rows = [], []
    for _ in range(TOP_K):
        mx = jnp.max(x, axis=0, keepdims=True)
        ix = jnp.min(jnp.where(x == mx, ei, E), axis=0, keepdims=True)
        hit = ei == ix
        w_rows.append(jnp.sum(jnp.where(hit, scores, 0.0), axis=0, keepdims=True))
        idx_rows.append(ix)
        sel = jnp.where(hit, 1.0, sel)
        x = jnp.where(hit, NEG, x)
    w = jnp.concatenate(w_rows, axis=0)
    w = w / (jnp.sum(w, axis=0, keepdims=True) + 1e-20) * ROUTED_SCALE
    eidx_ref[...] = jnp.concatenate(idx_rows, axis=0)
    wts_ref[...] = w

    upper = (lax.broadcasted_iota(I32, (tm, tm), 0) < lax.broadcasted_iota(I32, (tm, tm), 1)).astype(BF16)
    base = carry_scr[...][:, 0:1]
    excl = _dot(sel.astype(BF16), upper) + base
    rnk_ref[...] = jnp.concatenate(
        [jnp.sum(jnp.where(ei == ix, excl, 0.0), axis=0, keepdims=True) for ix in idx_rows],
        axis=0).astype(I32)
    carry_scr[...] = carry_scr[...] + jnp.sum(sel, axis=1, keepdims=True)
    cnt_ref[...] = carry_scr[...]


def _route(logits_t, bias, tm):
    E, T = logits_t.shape
    K = TOP_K
    tok = lambda: pl.BlockSpec((K, tm), lambda i: (0, i))
    return pl.pallas_call(
        functools.partial(_route_kernel, tm=tm),
        grid=(T // tm,),
        in_specs=[pl.BlockSpec((E, tm), lambda i: (0, i)),
                  pl.BlockSpec((E, 1), lambda i: (0, 0))],
        out_specs=[tok(), tok(), tok(), pl.BlockSpec((E, LANES), lambda i: (0, 0))],
        out_shape=[jax.ShapeDtypeStruct((K, T), I32), jax.ShapeDtypeStruct((K, T), F32),
                   jax.ShapeDtypeStruct((K, T), I32), jax.ShapeDtypeStruct((E, LANES), F32)],
        scratch_shapes=[pltpu.VMEM((E, LANES), F32)],
        compiler_params=_cparams(("arbitrary",)),
        name="route",
    )(logits_t, bias)


def _slots_kernel(pst_ref, eidx_ref, rnk_ref, dest_ref):
    eidx = eidx_ref[...]
    dest = rnk_ref[...]
    for e in range(N_EXPERTS):
        dest = dest + jnp.where(eidx == e, pst_ref[e], 0)
    dest_ref[...] = dest


def _slots(eidx, rnk, pstart, tm):
    K, T = eidx.shape
    tok = lambda: pl.BlockSpec((K, tm), lambda i: (0, i))
    return pl.pallas_call(
        _slots_kernel,
        grid=(T // tm,),
        in_specs=[pl.BlockSpec(memory_space=pltpu.SMEM), tok(), tok()],
        out_specs=tok(),
        out_shape=jax.ShapeDtypeStruct((K, T), I32),
        compiler_params=_cparams(("parallel",)),
        name="slots",
    )(pstart, eidx, rnk)


def _ffn_kernel(te_ref, nu_ref, tok_ref, dst_ref, h_hbm, wg_ref, wu_ref, wd_ref, ys_hbm,
                gbuf, obuf, wg_b, wu_b, wd_b, gsem, osem, *, tm):
    s = pl.program_id(0)
    nu = nu_ref[0]
    tile_rows = tm * PACK_ROWS

    def gather(slot):
        for r in range(tm):
            pltpu.make_async_copy(
                h_hbm.at[pl.ds(pl.multiple_of(tok_ref[0, 0, r] * PACK_ROWS, PACK_ROWS), PACK_ROWS), :],
                gbuf.at[slot, pl.ds(r * PACK_ROWS, PACK_ROWS), :], gsem.at[slot]).start()

    def scatter(slot):
        for r in range(tm):
            pltpu.make_async_copy(
                obuf.at[slot, pl.ds(r * PACK_ROWS, PACK_ROWS), :],
                ys_hbm.at[pl.ds(pl.multiple_of(dst_ref[0, 0, r] * PACK_ROWS, PACK_ROWS), PACK_ROWS), :],
                osem.at[slot]).start()

    def wait_gather(slot):
        pltpu.make_async_copy(h_hbm.at[pl.ds(0, tile_rows), :], gbuf.at[slot], gsem.at[slot]).wait()

    def wait_scatter(slot):
        pltpu.make_async_copy(obuf.at[slot], ys_hbm.at[pl.ds(0, tile_rows), :], osem.at[slot]).wait()

    def compute(slot):
        parts = []
        for c in range(PACK_ROWS):
            lo, hi = _unpack_rows(gbuf.at[slot], c, tm)
            parts += [lo.astype(BF16), hi.astype(BF16)]
        a = jnp.concatenate(parts, axis=1)
        hg = _dot(a, wg_b[...])
        hu = _dot(a, wu_b[...])
        hid = (hg * _sigmoid(hg) * hu).astype(BF16)
        _pack_rows(_dot(hid, wd_b[...]), obuf.at[slot], tm)

    cur = jnp.maximum(s - 1, 0)
    @pl.when((s >= 1) & (s <= nu) & ((s == 1) | (te_ref[cur] != te_ref[jnp.maximum(s - 2, 0)])))
    def _():
        wg_b[...] = wg_ref[0].astype(BF16)
        wu_b[...] = wu_ref[0].astype(BF16)
        wd_b[...] = wd_ref[0].astype(BF16)

    @pl.when(s == 0)
    def _():
        gather(0)

    @pl.when(s == 1)
    def _():
        wait_gather(0)
        gather(1)
        compute(0)

    for slot in range(2):
        parity = s % 2 == slot

        @pl.when(parity & (s >= 3) & (s <= nu + 1))
        def _():
            wait_scatter(1 - slot)

        @pl.when(parity & (s >= 2) & (s <= nu))
        def _():
            wait_gather(1 - slot)
            gather(slot)
            scatter(slot)
            compute(1 - slot)

        @pl.when(parity & (s >= 2) & (s == nu + 1))
        def _():
            wait_gather(1 - slot)
            scatter(slot)
            wait_scatter(slot)


def _ffn(tile_e, n_used, row_tok, row_dst, h2p, wg, wu, wd, tm):
    nt = row_tok.shape[0]
    D, I = wg.shape[1], wg.shape[2]
    cur = lambda s: jnp.maximum(s - 1, 0)
    wspec = lambda shape: pl.BlockSpec(shape, lambda s, te, nu: (te[cur(s)], 0, 0))
    grid_spec = pltpu.PrefetchScalarGridSpec(
        num_scalar_prefetch=2,
        grid=(nt + 1,),
        in_specs=[
            pl.BlockSpec((1, 1, tm), lambda s, te, nu: (jnp.minimum(s, nt - 1), 0, 0),
                         memory_space=pltpu.SMEM),
            pl.BlockSpec((1, 1, tm), lambda s, te, nu: (jnp.maximum(s - 2, 0), 0, 0),
                         memory_space=pltpu.SMEM),
            pl.BlockSpec(memory_space=pl.ANY),
            wspec((1, D, I)), wspec((1, D, I)), wspec((1, I, D)),
        ],
        out_specs=pl.BlockSpec(memory_space=pl.ANY),
        scratch_shapes=[pltpu.VMEM((2, tm * PACK_ROWS, PACK_W), U32),
                        pltpu.VMEM((2, tm * PACK_ROWS, PACK_W), U32),
                        pltpu.VMEM((D, I), BF16), pltpu.VMEM((D, I), BF16), pltpu.VMEM((I, D), BF16),
                        pltpu.SemaphoreType.DMA((2,)), pltpu.SemaphoreType.DMA((2,))],
    )
    return pl.pallas_call(
        functools.partial(_ffn_kernel, tm=tm),
        grid_spec=grid_spec,
        out_shape=jax.ShapeDtypeStruct((nt * tm * PACK_ROWS, PACK_W), U32),
        compiler_params=_cparams(("arbitrary",)),
        name="expert_ffn",
    )(tile_e, n_used, row_tok, row_dst, h2p, wg, wu, wd)


def _combine_kernel(*refs, tm):
    ys_refs = refs[:TOP_K]
    wt_ref, x1_ref, h2_ref, wsg_ref, wsu_ref, wsd_ref, o_ref = refs[TOP_K:]
    h2 = h2_ref[...]
    hg = _dot(h2, wsg_ref[...])
    hu = _dot(h2, wsu_ref[...])
    base = x1_ref[...] + _dot((hg * _sigmoid(hg) * hu).astype(BF16), wsd_ref[...])
    wt = wt_ref[...]
    for c in range(PACK_ROWS):
        acc_lo = base[:, c * 2 * LANES:c * 2 * LANES + LANES]
        acc_hi = base[:, c * 2 * LANES + LANES:(c + 1) * 2 * LANES]
        for k in range(TOP_K):
            lo, hi = _unpack_rows(ys_refs[k], c, tm)
            wk = wt[:, k:k + 1]
            acc_lo = acc_lo + wk * lo
            acc_hi = acc_hi + wk * hi
        o_ref[:, c * 2 * LANES:c * 2 * LANES + LANES] = acc_lo
        o_ref[:, c * 2 * LANES + LANES:(c + 1) * 2 * LANES] = acc_hi


def _combine(ys, wts_t, x1, h2, wsg, wsu, wsd, tm):
    T, D = x1.shape
    K = wts_t.shape[1]
    I = wsg.shape[1]
    full = lambda shape: pl.BlockSpec(shape, lambda i: (0, 0))
    return pl.pallas_call(
        functools.partial(_combine_kernel, tm=tm),
        grid=(T // tm,),
        in_specs=[pl.BlockSpec((tm * PACK_ROWS, PACK_W), lambda i, k=k: (k * (T // tm) + i, 0))
                  for k in range(K)] + [
                  pl.BlockSpec((tm, K), lambda i: (i, 0)),
                  pl.BlockSpec((tm, D), lambda i: (i, 0)),
                  pl.BlockSpec((tm, D), lambda i: (i, 0)),
                  full((D, I)), full((D, I)), full((I, D))],
        out_specs=pl.BlockSpec((tm, D), lambda i: (i, 0)),
        out_shape=jax.ShapeDtypeStruct((T, D), F32),
        compiler_params=_cparams(("parallel",)),
        name="combine",
    )(*([ys] * K), wts_t, x1, h2, wsg, wsu, wsd)


def _ple_kernel(xf_ref, xt_ref, p_ref, wp_ref, gpost_ref, gin_ref, wg_ref, o_ref,
                hn_scr, ple_scr, *, tn):
    j = pl.program_id(1)

    @pl.when(j == 0)
    def _():
        hn_scr[...] = _rms(xf_ref[...], gin_ref[...]).astype(BF16)
        ple_scr[...] = _rms(_dot(p_ref[...].astype(BF16), wp_ref[...]), gpost_ref[...])

    cols = pl.ds(pl.multiple_of(j * tn, tn), tn)
    gate = _sigmoid(_dot(hn_scr[...], wg_ref[...]))
    o_ref[...] = xt_ref[...] + gate * ple_scr[:, cols]


def _ple(x2, p2, wp, gpost, gin, wg, tm, tn):
    T, D = x2.shape
    Pd = p2.shape[1]
    return pl.pallas_call(
        functools.partial(_ple_kernel, tn=tn),
        grid=(T // tm, D // tn),
        in_specs=[
            pl.BlockSpec((tm, D), lambda i, j: (i, 0)),
            pl.BlockSpec((tm, tn), lambda i, j: (i, j)),
            pl.BlockSpec((tm, Pd), lambda i, j: (i, 0)),
            pl.BlockSpec((Pd, D), lambda i, j: (0, 0)),
            pl.BlockSpec((1, D), lambda i, j: (0, 0)),
            pl.BlockSpec((1, D), lambda i, j: (0, 0)),
            pl.BlockSpec((D, tn), lambda i, j: (0, j)),
        ],
        out_specs=pl.BlockSpec((tm, tn), lambda i, j: (i, j)),
        out_shape=jax.ShapeDtypeStruct((T, D), F32),
        scratch_shapes=[pltpu.VMEM((tm, D), BF16), pltpu.VMEM((tm, D), F32)],
        compiler_params=_cparams(("parallel", "arbitrary")),
        name="ple",
    )(x2, x2, p2, wp, gpost, gin, wg)


def _pad_cols(w, n):
    return jnp.pad(w, ((0, 0), (0, n - w.shape[1])))


def _pad_rows(w, n):
    return jnp.pad(w, ((0, n - w.shape[0]), (0, 0)))


def _layout_w_in(w):
    W = RWKV_WIDTH
    c = 3 * W
    segs = [w[:, 0:c],
            _pad_cols(w[:, c:c + DECAY_LORA], LORA_PAD),
            _pad_cols(w[:, c + DECAY_LORA:c + DECAY_LORA + AAA_LORA], LORA_PAD),
            _pad_cols(w[:, c + DECAY_LORA + AAA_LORA:c + DECAY_LORA + AAA_LORA + GATE_LORA], GATE_LORA_PAD)]
    c += DECAY_LORA + AAA_LORA + GATE_LORA
    mla_cols = Q_LORA + KV_LORA + QK_ROPE
    segs.append(_pad_cols(w[:, c:c + mla_cols], MLA_PAD))
    c += mla_cols
    segs.append(w[:, c:])
    out = jnp.concatenate(segs, axis=1).astype(BF16)
    assert out.shape[1] == IN_PAD
    return out


def _layout_mu(mu):
    W = RWKV_WIDTH
    c = 3 * W
    segs = [mu[0:c],
            jnp.pad(mu[c:c + DECAY_LORA], (0, LORA_PAD - DECAY_LORA)),
            jnp.pad(mu[c + DECAY_LORA:c + DECAY_LORA + AAA_LORA], (0, LORA_PAD - AAA_LORA)),
            jnp.pad(mu[c + DECAY_LORA + AAA_LORA:], (0, GATE_LORA_PAD - GATE_LORA))]
    return jnp.concatenate(segs).reshape(1, RWKV_PAD)


def _moe_tables(eidx, rnk, counts, tmf, tm_slots):
    K, T = eidx.shape
    E = N_EXPERTS
    A = K * T
    n_tiles = A // tmf + E
    P = n_tiles * tmf
    pcounts = (counts + tmf - 1) // tmf * tmf
    pends = jnp.cumsum(pcounts)
    pstart = (pends - pcounts).astype(I32)
    tile_start = jnp.arange(n_tiles, dtype=I32) * tmf
    tile_e = jnp.minimum(jnp.sum((pends[None, :] <= tile_start[:, None]).astype(I32), axis=1), E - 1)
    n_used = (pends[E - 1:] // tmf).astype(I32)
    dest = _slots(eidx, rnk, pstart, tm_slots)
    j = jnp.arange(tmf, dtype=I32)[None, :]
    tail = j < (pcounts - counts)[:, None]
    spare = jnp.cumsum(jnp.logical_not(tail).reshape(-1).astype(I32)) - 1
    pad_row = jnp.where(tail.reshape(-1), ((pstart + counts)[:, None] + j).reshape(-1),
                        pends[E - 1] + spare)
    q = jnp.arange(P - A, dtype=I32)
    slot_id = jnp.arange(A, dtype=I32)
    rows = jnp.concatenate([dest.reshape(-1), pad_row.astype(I32)])
    slots = jnp.concatenate([slot_id, A + q])
    row_dst = lax.sort((rows, slots), num_keys=1)[1]
    row_tok = jnp.where(row_dst < A, row_dst % T, 0)
    return (row_tok.reshape(n_tiles, 1, tmf), row_dst.reshape(n_tiles, 1, tmf), tile_e, n_used)


def _layer(x, p, positions, g_mix, w_in, mu_rwkv, w0, w2, a0, a2, g2, k_k, k_a, r_k,
           gn_w, gn_b, w_a_up, g_qa, g_kva, w_uq, w_ukv, g_qn, g_kn, w_b_up, w_o,
           g_ffn, w_router, router_bias, w_exp_gate, w_exp_up, w_exp_down,
           w_sh_gate, w_sh_up, w_sh_down, w_ple, g_ple_post, g_ple_in, w_ple_gate,
           *, tiles):
    B, S, D = x.shape
    T = B * S
    W = RWKV_WIDTH
    x2 = x.reshape(T, D)
    row = lambda v: v.reshape(1, -1).astype(F32)

    proj = _inproj(x2, row(g_mix), _layout_w_in(w_in), tiles["tm_in"], tiles["tn_in"])

    head_blk = (jnp.arange(W)[:, None] // RWKV_HEAD == jnp.arange(W)[None, :] // RWKV_HEAD).astype(BF16)
    r, k, v, kk, a, lw, g = _rwkv_prep(
        proj, _layout_mu(mu_rwkv), row(w0), _pad_rows(w2, LORA_PAD).astype(BF16), row(a0),
        _pad_rows(a2, LORA_PAD).astype(BF16), _pad_rows(g2, GATE_LORA_PAD).astype(BF16),
        row(k_k), row(k_a), head_blk, tiles["tm_prep"], S)
    b3 = lambda t: t.reshape(B, S, W)
    o_a = _rwkv_scan(b3(r), b3(k), b3(v), b3(kk), b3(a), b3(lw), b3(g),
                     row(gn_w), row(gn_b), row(r_k), tiles["rows_scan"]).reshape(T, W)

    half = QK_ROPE // 2
    inv_freq = ROPE_THETA ** (-jnp.arange(half, dtype=F32) / half)
    invf = jnp.concatenate([inv_freq, inv_freq, jnp.zeros((LANES - QK_ROPE,), F32)]).reshape(1, LANES)
    wuq = jnp.pad(w_uq.reshape(Q_LORA, MLA_HEADS, QK_HEAD),
                  ((0, 0), (0, 0), (0, QK_PAD - QK_HEAD))).reshape(Q_LORA, MLA_HEADS * QK_PAD)
    pad_g = lambda gv: jnp.pad(gv, (0, QK_PAD - QK_HEAD)).reshape(1, QK_PAD)
    q, kx, vx = _mla_prep(proj, positions.reshape(T, 1).astype(I32), invf, row(g_qa), row(g_kva),
                          wuq.astype(BF16), w_ukv.astype(BF16), pad_g(g_qn), pad_g(g_kn),
                          tiles["tm_mla"])
    o_b = _flash(q.reshape(B, S, -1), kx.reshape(B, S, -1), vx, tiles["tq"]).reshape(T, MLA_WIDTH)

    merged = _merge(o_a, o_b, w_a_up.astype(BF16), w_b_up.astype(BF16), proj,
                    tiles["tm_merge"], tiles["tn_merge"])

    x1, h2, h2p, logits_t = _wo(merged, x2, w_o.astype(BF16), row(g_ffn),
                                jnp.transpose(w_router).astype(F32), tiles["tm_wo"])
    eidx, wts, rnk, cnt = _route(logits_t, router_bias.reshape(N_EXPERTS, 1).astype(F32),
                                 tiles["tm_route"])
    tmf = tiles["tm_ffn"]
    counts = cnt[:, 0].astype(I32)
    row_tok, row_dst, tile_e, n_used = _moe_tables(eidx, rnk, counts, tmf, tiles["tm_slots"])
    ys = _ffn(tile_e, n_used, row_tok, row_dst, h2p, w_exp_gate, w_exp_up, w_exp_down, tmf)
    x3 = _combine(ys, jnp.transpose(wts), x1, h2,
                  w_sh_gate.astype(BF16), w_sh_up.astype(BF16), w_sh_down.astype(BF16),
                  tiles["tm_comb"])

    out = _ple(x3, p.reshape(T, PLE_DIM), w_ple.astype(BF16), row(g_ple_post), row(g_ple_in),
               w_ple_gate.astype(BF16), tiles["tm_ple"], tiles["tn_ple"])
    return out.reshape(B, S, D)


TILES = dict(tm_in=1024, tn_in=512, tm_prep=256, rows_scan=512, tm_mla=1024, tq=512,
             tm_merge=1024, tn_merge=512, tm_wo=512, tm_route=512, tm_ffn=256,
             tm_slots=2048, tm_comb=128, tm_ple=512, tn_ple=1024)


def kernel(x, p, positions, g_mix, w_in, mu_rwkv, w0, w2, a0, a2, g2, k_k, k_a, r_k, gn_w, gn_b, w_a_up, g_qa, g_kva, w_uq, w_ukv, g_qn, g_kn, w_b_up, w_o, g_ffn, w_router, router_bias, w_exp_gate, w_exp_up, w_exp_down, w_sh_gate, w_sh_up, w_sh_down, w_ple, g_ple_post, g_ple_in, w_ple_gate):
    args = (g_mix, w_in, mu_rwkv, w0, w2, a0, a2, g2, k_k, k_a, r_k, gn_w, gn_b, w_a_up,
            g_qa, g_kva, w_uq, w_ukv, g_qn, g_kn, w_b_up, w_o, g_ffn, w_router, router_bias,
            w_exp_gate, w_exp_up, w_exp_down, w_sh_gate, w_sh_up, w_sh_down, w_ple,
            g_ple_post, g_ple_in, w_ple_gate)
    assert all(t.shape[0] == 1 for t in args), "single-layer stack expected"
    return _layer(x, p[0], positions, *[t[0] for t in args], tiles=TILES)
```

```python
import functools

import jax
import jax.numpy as jnp
from jax import lax
from jax.experimental import pallas as pl
from jax.experimental.pallas import tpu as pltpu

F32 = jnp.float32
BF16 = jnp.bfloat16
I32 = jnp.int32
U32 = jnp.uint32

D_MODEL = 2048
PLE_DIM = 256
NORM_EPS = 1e-6
RWKV_HEADS = 16
RWKV_HEAD = 64
RWKV_WIDTH = RWKV_HEADS * RWKV_HEAD
DECAY_LORA = 64
AAA_LORA = 64
GATE_LORA = 160
GN_EPS = 64e-5
MLA_HEADS = 8
Q_LORA = 512
KV_LORA = 512
QK_NOPE = 128
QK_ROPE = 64
QK_HEAD = QK_NOPE + QK_ROPE
V_HEAD = 128
MLA_WIDTH = MLA_HEADS * V_HEAD
ROPE_THETA = 10000.0
N_EXPERTS = 64
TOP_K = 8
N_GROUPS = 8
GROUP_SIZE = N_EXPERTS // N_GROUPS
TOPK_GROUPS = 4
MOE_INTER = 512
ROUTED_SCALE = 2.5

LANES = 128
QK_PAD = 2 * LANES
PACK_ROWS = 8
PACK_W = D_MODEL // 2 // PACK_ROWS
assert PACK_W == LANES

LORA_PAD = 128
GATE_LORA_PAD = 256
RWKV_PAD = 3 * RWKV_WIDTH + 2 * LORA_PAD + GATE_LORA_PAD
MLA_PAD = 1536
COL_RWKV = 0
COL_MLA = RWKV_PAD
COL_GA = COL_MLA + MLA_PAD
COL_GB = COL_GA + D_MODEL
IN_PAD = COL_GB + D_MODEL

VMEM_LIMIT = 56 * 1024 * 1024
CHUNK = 64
WO_ROWS = 128
SCAN_LANE_BLOCKS = 8


def _cparams(sem):
    return pltpu.CompilerParams(dimension_semantics=sem, vmem_limit_bytes=VMEM_LIMIT)


def _dot(a, b):
    return jnp.dot(a, b, preferred_element_type=F32)


def _dot_nt(a, b):
    return lax.dot_general(a, b, (((1,), (1,)), ((), ())), preferred_element_type=F32)


def _dot_tn(a, b):
    return lax.dot_general(a, b, (((0,), (0,)), ((), ())), preferred_element_type=F32)


def _split_bf16(x):
    hi = x.astype(BF16)
    lo = (x - hi.astype(F32)).astype(BF16)
    return hi, lo


def _rms(x, g):
    ms = jnp.mean(x * x, axis=-1, keepdims=True)
    return x * lax.rsqrt(ms + NORM_EPS) * g


def _sigmoid(x):
    return 1.0 / (1.0 + jnp.exp(-x))


def _inproj_kernel(x_ref, g_ref, w_ref, o_ref, h_scr):
    @pl.when(pl.program_id(1) == 0)
    def _():
        h_scr[...] = _rms(x_ref[...], g_ref[...]).astype(BF16)

    o_ref[...] = _dot(h_scr[...], w_ref[...]).astype(o_ref.dtype)


def _inproj(x2, g, w, tm, tn):
    T, D = x2.shape
    N = w.shape[1]
    return pl.pallas_call(
        _inproj_kernel,
        grid=(T // tm, N // tn),
        in_specs=[
            pl.BlockSpec((tm, D), lambda i, j: (i, 0)),
            pl.BlockSpec((1, D), lambda i, j: (0, 0)),
            pl.BlockSpec((D, tn), lambda i, j: (0, j)),
        ],
        out_specs=pl.BlockSpec((tm, tn), lambda i, j: (i, j)),
        out_shape=jax.ShapeDtypeStruct((T, N), BF16),
        scratch_shapes=[pltpu.VMEM((tm, D), BF16)],
        compiler_params=_cparams(("parallel", "arbitrary")),
        name="inproj",
    )(x2, g, w)


HALO = 16


def _rwkv_prep_kernel(cur_ref, halo_ref, mu_ref, w0_ref, w2_ref, a0_ref, a2_ref, g2_ref,
                      kk_ref, ka_ref, bd_ref,
                      r_out, k_out, v_out, kk_out, a_out, lw_out, g_out, *, tm, seq):
    i = pl.program_id(0)
    cur = cur_ref[...].astype(F32)
    first = (i * tm) % seq == 0
    last_prev = halo_ref[HALO - 1:HALO, :].astype(F32)
    last_prev = jnp.where(first, 0.0, last_prev)
    row = lax.broadcasted_iota(I32, cur.shape, 0)
    prev = jnp.where(row == 0, last_prev, pltpu.roll(cur, 1, 0))
    u = cur + (prev - cur) * mu_ref[...]
    W = RWKV_WIDTH
    r = u[:, 0:W]
    k = u[:, W:2 * W]
    v = u[:, 2 * W:3 * W]
    dw = u[:, 3 * W:3 * W + LORA_PAD]
    da = u[:, 3 * W + LORA_PAD:3 * W + 2 * LORA_PAD]
    dg = u[:, 3 * W + 2 * LORA_PAD:3 * W + 2 * LORA_PAD + GATE_LORA_PAD]
    z = w0_ref[...] + _dot(jnp.tanh(dw).astype(BF16), w2_ref[...])
    w_log = -(jnp.maximum(-z, 0.0) + jnp.log(1.0 + jnp.exp(-jnp.abs(z)))) - 0.5
    lw_out[...] = -jnp.exp(w_log)
    a = _sigmoid(a0_ref[...] + _dot(da.astype(BF16), a2_ref[...]))
    g_out[...] = _dot(_sigmoid(dg).astype(BF16), g2_ref[...]).astype(g_out.dtype)
    kk = k * kk_ref[...]
    hi, lo = _split_bf16(kk * kk)
    ssq = _dot(hi, bd_ref[...]) + _dot(lo, bd_ref[...])
    kk = kk / jnp.maximum(jnp.sqrt(ssq), 1e-12)
    r_out[...] = r.astype(r_out.dtype)
    k_out[...] = (k * (1.0 + (a - 1.0) * ka_ref[...])).astype(k_out.dtype)
    v_out[...] = v.astype(v_out.dtype)
    kk_out[...] = kk.astype(kk_out.dtype)
    a_out[...] = a.astype(a_out.dtype)


def _rwkv_prep(proj, mu, w0, w2, a0, a2, g2, k_k, k_a, bd, tm, seq):
    T = proj.shape[0]
    W = RWKV_WIDTH
    full = lambda shape: pl.BlockSpec(shape, lambda i: (0, 0))
    out = lambda: pl.BlockSpec((tm, W), lambda i: (i, 0))
    return pl.pallas_call(
        functools.partial(_rwkv_prep_kernel, tm=tm, seq=seq),
        grid=(T // tm,),
        in_specs=[
            pl.BlockSpec((tm, RWKV_PAD), lambda i: (i, 0)),
            pl.BlockSpec((HALO, RWKV_PAD), lambda i: (jnp.maximum(i * (tm // HALO) - 1, 0), 0)),
            full((1, RWKV_PAD)), full((1, W)), full((LORA_PAD, W)), full((1, W)),
            full((LORA_PAD, W)), full((GATE_LORA_PAD, W)), full((1, W)), full((1, W)),
            full((W, W)),
        ],
        out_specs=[out() for _ in range(7)],
        out_shape=[jax.ShapeDtypeStruct((T, W), dt)
                   for dt in (BF16, BF16, BF16, BF16, BF16, F32, BF16)],
        compiler_params=_cparams(("parallel",)),
        name="rwkv_prep",
    )(proj, proj, mu, w0, w2, a0, a2, g2, k_k, k_a, bd)


def _rwkv_scan_kernel(r_ref, k_ref, v_ref, kk_ref, a_ref, lw_ref, g_ref,
                      gnw_ref, gnb_ref, rk_ref, o_ref, s_scr, *, rows):
    L = CHUNK
    L2 = 2 * L

    @pl.when(pl.program_id(2) == 0)
    def _():
        s_scr[...] = jnp.zeros_like(s_scr)

    lane = lax.broadcasted_iota(I32, (L, LANES), 1)
    m_a = (lane < RWKV_HEAD).astype(F32)
    m_b = 1.0 - m_a
    ri = lax.broadcasted_iota(I32, (L2, L2), 0)
    ci = lax.broadcasted_iota(I32, (L2, L2), 1)
    strict = ri > ci
    incl = ri >= ci
    eye = (ri == ci).astype(F32)
    tri = (lax.broadcasted_iota(I32, (L, L), 0) >= lax.broadcasted_iota(I32, (L, L), 1)).astype(BF16)
    hr = lax.broadcasted_iota(I32, (LANES, LANES), 0) // RWKV_HEAD
    hc = lax.broadcasted_iota(I32, (LANES, LANES), 1) // RWKV_HEAD
    head_ones = (hr == hc).astype(BF16)
    nb = SCAN_LANE_BLOCKS

    def stack(x):
        return jnp.concatenate([x * m_a, x * m_b], axis=0)

    chunks = range(rows // L)
    rows_of = lambda c: slice(c * L, (c + 1) * L)
    lanes_of = lambda h: slice(h * LANES, (h + 1) * LANES)
    each = lambda f, *lists: [f(*vals) for vals in zip(*lists)]

    def operands(c, h):
        sl, ln = rows_of(c), lanes_of(h)
        r = r_ref[0, sl, ln].astype(F32)
        k = k_ref[0, sl, ln].astype(F32)
        v = v_ref[0, sl, ln].astype(F32)
        kk = kk_ref[0, sl, ln].astype(F32)
        a = a_ref[0, sl, ln].astype(F32)
        lw = lw_ref[0, sl, ln]
        hi, lo = _split_bf16(lw)
        cum = _dot(tri, hi) + _dot(tri, lo)
        c_end = cum[L - 1:L, :]
        p_inv = jnp.exp(-cum)
        p_end = jnp.exp(c_end - cum)
        b = kk * a
        return dict(
            x_a=stack(-(kk * jnp.exp(cum - lw))).astype(BF16),
            x_r=stack(r * jnp.exp(cum)),
            x_bk=jnp.concatenate([stack(b * p_inv), stack(k * p_inv)], axis=0).astype(BF16),
            v_st=stack(v).astype(BF16),
            z_hat=jnp.concatenate([stack(b * p_end), stack(k * p_end)], axis=0).astype(BF16),
            p_row=jnp.exp(c_end),
            rkr=(r * k * rk_ref[:, ln]).astype(BF16), v=v)

    ops = [operands(c, h) for c in chunks for h in range(nb)]
    big = [_dot_nt(jnp.concatenate([o["x_a"], o["x_r"].astype(BF16)], axis=0), o["x_bk"])
           for o in ops]
    n_pow = [jnp.where(strict, m[0:L2, 0:L2], 0.0) for m in big]
    a_ak = [jnp.where(strict, m[0:L2, L2:2 * L2], 0.0).astype(BF16) for m in big]
    a_r = [jnp.concatenate([jnp.where(incl, m[L2:2 * L2, 0:L2], 0.0),
                            jnp.where(incl, m[L2:2 * L2, L2:2 * L2], 0.0)], axis=1).astype(BF16)
           for m in big]

    t_inv = [eye + n for n in n_pow]
    for _ in range(5):
        n_pow = each(lambda n: _dot(n.astype(BF16), n.astype(BF16)), n_pow)
        t_inv = each(lambda t, n: t + _dot(t.astype(BF16), n.astype(BF16)), t_inv, n_pow)

    akv = each(lambda m, o: _dot(m, o["v_st"]).astype(BF16), a_ak, ops)
    w = each(lambda t, o, x: _dot(t.astype(BF16), jnp.concatenate([o["x_a"], x], axis=1)),
             t_inv, ops, akv)
    w1 = [m[:, 0:LANES].astype(BF16) for m in w]
    w2v = each(lambda m, o: jnp.concatenate([m[:, LANES:].astype(BF16), o["v_st"]], axis=0), w, ops)
    g_mat = each(lambda o, m, x: (o["x_r"] + _dot(m[:, 0:L2], x)).astype(BF16), ops, a_r, w1)
    y0 = each(_dot, a_r, w2v)
    m_mat = each(lambda x, o: _dot_tn(x, o["z_hat"][0:L2, :]).astype(BF16), w1, ops)
    c2 = each(lambda x, o: _dot_tn(x, o["z_hat"]), w2v, ops)
    bonus_v = [_dot(o["rkr"], head_ones) * o["v"] for o in ops]

    s = [s_scr[h] for h in range(nb)]
    y = []
    for i in range(len(ops)):
        h = i % nb
        s_b = s[h].astype(BF16)
        y_st = _dot_nt(g_mat[i], s_b) + y0[i]
        s[h] = s[h] * ops[i]["p_row"] + _dot(s_b, m_mat[i]) + c2[i]
        y.append(y_st[0:L, :] + y_st[L:L2, :])
    for h in range(nb):
        s_scr[h] = s[h]

    mean = [_dot(t.astype(BF16), head_ones) * (1.0 / RWKV_HEAD) for t in y]
    yc = each(lambda t, m: t - m, y, mean)
    var = [_dot((t * t).astype(BF16), head_ones) * (1.0 / RWKV_HEAD) for t in yc]
    for i in range(len(ops)):
        sl, ln = rows_of(i // nb), lanes_of(i % nb)
        yn = yc[i] * lax.rsqrt(var[i] + GN_EPS) * gnw_ref[:, ln] + gnb_ref[:, ln]
        o_ref[0, sl, ln] = ((yn + bonus_v[i]) * g_ref[0, sl, ln].astype(F32)).astype(o_ref.dtype)


def _rwkv_scan(r, k, v, kk, a, lw, g, gn_w, gn_b, r_k, rows):
    B, S, W = r.shape
    wb = SCAN_LANE_BLOCKS * LANES
    npair = W // wb
    seq = lambda: pl.BlockSpec((1, rows, wb), lambda b, p, c: (b, c, p))
    par = lambda: pl.BlockSpec((1, wb), lambda b, p, c: (0, p))
    return pl.pallas_call(
        functools.partial(_rwkv_scan_kernel, rows=rows),
        grid=(B, npair, S // rows),
        in_specs=[seq() for _ in range(7)] + [par(), par(), par()],
        out_specs=seq(),
        out_shape=jax.ShapeDtypeStruct((B, S, W), BF16),
        scratch_shapes=[pltpu.VMEM((SCAN_LANE_BLOCKS, LANES, LANES), F32)],
        compiler_params=_cparams(("parallel", "parallel", "arbitrary")),
        name="rwkv_scan",
    )(r, k, v, kk, a, lw, g, gn_w, gn_b, r_k)


def _rope128(x, cos, sin_signed):
    lane = lax.broadcasted_iota(I32, x.shape, 1)
    half = QK_ROPE // 2
    partner = jnp.where(lane < half, pltpu.roll(x, LANES - half, 1), pltpu.roll(x, half, 1))
    return x * cos + partner * sin_signed


def _mla_prep_kernel(cq_ref, ckv_ref, kr_ref, pos_ref, invf_ref, gqa_ref, gkva_ref,
                     wuq_ref, wk_ref, wvt_ref, gqn_ref, gkn_ref,
                     q_out, k_out, v_out, cqn_scr, ckvn_scr, cos_scr, sin_scr):
    @pl.when(pl.program_id(1) == 0)
    def _():
        cqn_scr[...] = _rms(cq_ref[...].astype(F32), gqa_ref[...]).astype(BF16)
        ckvn_scr[...] = _rms(ckv_ref[...].astype(F32), gkva_ref[...]).astype(BF16)
        ang = pos_ref[...].astype(F32) * invf_ref[...]
        lane = lax.broadcasted_iota(I32, ang.shape, 1)
        cos_scr[...] = jnp.cos(ang)
        sin_scr[...] = jnp.where(lane < QK_ROPE // 2, -1.0, 1.0) * jnp.sin(ang)

    cos = cos_scr[...]
    sin = sin_scr[...]
    inv_n = 1.0 / QK_HEAD
    q = _dot(cqn_scr[...], wuq_ref[...])
    rs = lax.rsqrt(jnp.sum(q * q, axis=-1, keepdims=True) * inv_n + NORM_EPS)
    q = q * rs * gqn_ref[...] * (QK_HEAD ** -0.5 * LOG2E)
    q_out[...] = jnp.concatenate(
        [q[:, 0:LANES], _rope128(q[:, LANES:], cos, sin)], axis=1).astype(q_out.dtype)

    ckvn = ckvn_scr[...]
    k_nope = _dot(ckvn, wk_ref[...])
    kr = kr_ref[...].astype(F32)
    ssq = (jnp.sum(k_nope * k_nope, axis=-1, keepdims=True)
           + jnp.sum(kr * kr, axis=-1, keepdims=True))
    rs = lax.rsqrt(ssq * inv_n + NORM_EPS)
    gkn = gkn_ref[...]
    k_out[...] = jnp.concatenate(
        [k_nope * rs * gkn[:, 0:LANES], _rope128(kr * rs * gkn[:, LANES:], cos, sin)],
        axis=1).astype(k_out.dtype)
    v_out[...] = _dot_nt(wvt_ref[...], ckvn).astype(v_out.dtype)


def _mla_prep(proj, pos, invf, g_qa, g_kva, wuq, wk, wvt, gqn, gkn, tm):
    T = proj.shape[0]
    H = MLA_HEADS
    c0 = COL_MLA // Q_LORA
    ckr = (COL_MLA + Q_LORA + KV_LORA) // LANES
    full = lambda shape: pl.BlockSpec(shape, lambda i, h: (0, 0))
    return pl.pallas_call(
        _mla_prep_kernel,
        grid=(T // tm, H),
        in_specs=[
            pl.BlockSpec((tm, Q_LORA), lambda i, h: (i, c0)),
            pl.BlockSpec((tm, KV_LORA), lambda i, h: (i, c0 + 1)),
            pl.BlockSpec((tm, LANES), lambda i, h: (i, ckr)),
            pl.BlockSpec((tm, 1), lambda i, h: (i, 0)),
            full((1, LANES)), full((1, Q_LORA)), full((1, KV_LORA)),
            pl.BlockSpec((Q_LORA, QK_PAD), lambda i, h: (0, h)),
            pl.BlockSpec((KV_LORA, QK_NOPE), lambda i, h: (0, h)),
            pl.BlockSpec((V_HEAD, KV_LORA), lambda i, h: (h, 0)),
            full((1, QK_PAD)), full((1, QK_PAD)),
        ],
        out_specs=[
            pl.BlockSpec((tm, QK_PAD), lambda i, h: (i, h)),
            pl.BlockSpec((tm, QK_PAD), lambda i, h: (i, h)),
            pl.BlockSpec((V_HEAD, tm), lambda i, h: (h, i)),
        ],
        out_shape=[jax.ShapeDtypeStruct((T, H * QK_PAD), BF16),
                   jax.ShapeDtypeStruct((T, H * QK_PAD), BF16),
                   jax.ShapeDtypeStruct((H * V_HEAD, T), BF16)],
        scratch_shapes=[pltpu.VMEM((tm, Q_LORA), BF16), pltpu.VMEM((tm, KV_LORA), BF16),
                        pltpu.VMEM((tm, LANES), F32), pltpu.VMEM((tm, LANES), F32)],
        compiler_params=_cparams(("parallel", "arbitrary")),
        name="mla_prep",
    )(proj, proj, proj, pos, invf, g_qa, g_kva, wuq, wk, wvt, gqn, gkn)


MASK_VALUE = -1e30
LOG2E = 1.4426950408889634


def _flash_kernel(q_ref, k_ref, vt_ref, o_ref, s_scr, p_scr, alpha_scr, m_scr, l_scr, acc_scr,
                  *, tq):
    i = pl.program_id(2)
    q = q_ref[0]
    m_scr[...] = jnp.full_like(m_scr, MASK_VALUE)
    l_scr[...] = jnp.zeros_like(l_scr)
    acc_scr[...] = jnp.zeros_like(acc_scr)

    q_half = tq // 2

    def scores(j, slot, half):
        ks = pl.ds(pl.multiple_of(j * tq, tq), tq)
        cols = slice(half * q_half, (half + 1) * q_half)
        s_scr[slot, :, cols] = _dot_nt(k_ref[0, ks, :], q[cols, :])

    def softmax(slot, masked, strips):
        for c in strips:
            cols = slice(c * LANES, (c + 1) * LANES)
            s = s_scr[slot, :, cols]
            if masked:
                kpos = lax.broadcasted_iota(I32, s.shape, 0)
                qpos = lax.broadcasted_iota(I32, s.shape, 1) + c * LANES
                s = jnp.where(kpos <= qpos, s, MASK_VALUE)
            m_old = m_scr[:, cols]
            m_new = jnp.maximum(m_old, jnp.max(s, axis=0, keepdims=True))
            p = jnp.exp2(s - m_new)
            alpha = jnp.exp2(m_old - m_new)
            l_scr[:, cols] = alpha * l_scr[:, cols] + jnp.sum(p, axis=0, keepdims=True)
            m_scr[:, cols] = m_new
            p_scr[slot, :, cols] = p.astype(BF16)
            alpha_scr[slot, :, cols] = alpha

    def weighted_values(j, slot):
        vt = vt_ref[:, pl.ds(pl.multiple_of(j * tq, tq), tq)]
        acc_scr[...] = alpha_scr[slot] * acc_scr[...] + _dot(vt, p_scr[slot])

    scores(0, 0, 0)
    scores(0, 0, 1)
    p_scr[1] = jnp.zeros((tq, tq), BF16)
    alpha_scr[1] = jnp.ones((1, tq), F32)
    n_strips = tq // LANES
    first, second = range(0, n_strips // 2), range(n_strips // 2, n_strips)

    def step(j, slot):
        weighted_values(jnp.maximum(j - 1, 0), 1 - slot)
        scores(j + 1, 1 - slot, 0)
        softmax(slot, False, first)
        scores(j + 1, 1 - slot, 1)
        softmax(slot, False, second)

    def pair(jp, carry):
        step(2 * jp, 0)
        step(2 * jp + 1, 1)
        return carry

    def finish(slot):
        weighted_values(jnp.maximum(i - 1, 0), 1 - slot)
        softmax(slot, True, range(n_strips))
        weighted_values(i, slot)
        o_ref[0] = jnp.transpose(acc_scr[...] / l_scr[...]).astype(o_ref.dtype)

    lax.fori_loop(0, i // 2, pair, 0)

    @pl.when(i % 2 == 0)
    def _():
        finish(0)

    @pl.when(i % 2 == 1)
    def _():
        step(i - 1, 0)
        finish(1)


def _flash(q, k, vt, tq):
    B, S, _ = q.shape
    H = MLA_HEADS
    return pl.pallas_call(
        functools.partial(_flash_kernel, tq=tq),
        grid=(B, H, S // tq),
        in_specs=[
            pl.BlockSpec((1, tq, QK_PAD), lambda b, h, i: (b, i, h)),
            pl.BlockSpec((1, S, QK_PAD), lambda b, h, i: (b, 0, h)),
            pl.BlockSpec((V_HEAD, S), lambda b, h, i: (h, b)),
        ],
        out_specs=pl.BlockSpec((1, tq, V_HEAD), lambda b, h, i: (b, i, h)),
        out_shape=jax.ShapeDtypeStruct((B, S, H * V_HEAD), BF16),
        scratch_shapes=[pltpu.VMEM((2, tq, tq), F32), pltpu.VMEM((2, tq, tq), BF16),
                        pltpu.VMEM((2, 1, tq), F32),
                        pltpu.VMEM((1, tq), F32), pltpu.VMEM((1, tq), F32),
                        pltpu.VMEM((V_HEAD, tq), F32)],
        compiler_params=_cparams(("parallel", "parallel", "arbitrary")),
        name="flash_attn",
    )(q, k, vt)


def _merge_kernel(oa_ref, ob_ref, wa_ref, wb_ref, ga_ref, gb_ref, o_ref):
    ya = _dot(oa_ref[...], wa_ref[...])
    yb = _dot(ob_ref[...], wb_ref[...])
    ga = _sigmoid(ga_ref[...].astype(F32))
    gb = _sigmoid(gb_ref[...].astype(F32))
    o_ref[...] = (ga * ya + gb * yb).astype(o_ref.dtype)


def _merge(oa, ob, wa, wb, proj, tm, tn):
    T, K = oa.shape
    N = wa.shape[1]
    ca = COL_GA // tn
    cb = COL_GB // tn
    return pl.pallas_call(
        _merge_kernel,
        grid=(T // tm, N // tn),
        in_specs=[
            pl.BlockSpec((tm, K), lambda i, j: (i, 0)),
            pl.BlockSpec((tm, K), lambda i, j: (i, 0)),
            pl.BlockSpec((K, tn), lambda i, j: (0, j)),
            pl.BlockSpec((K, tn), lambda i, j: (0, j)),
            pl.BlockSpec((tm, tn), lambda i, j: (i, ca + j)),
            pl.BlockSpec((tm, tn), lambda i, j: (i, cb + j)),
        ],
        out_specs=pl.BlockSpec((tm, tn), lambda i, j: (i, j)),
        out_shape=jax.ShapeDtypeStruct((T, N), BF16),
        compiler_params=_cparams(("parallel", "parallel")),
        name="merge",
    )(oa, ob, wa, wb, proj, proj)


def _pack_rows(x, out_ref, n, row0=0):
    for c in range(PACK_ROWS):
        lo = x[:, c * 2 * LANES:c * 2 * LANES + LANES]
        hi = x[:, c * 2 * LANES + LANES:(c + 1) * 2 * LANES]
        lo_b = pltpu.bitcast(lo.astype(BF16).astype(F32), U32)
        hi_b = pltpu.bitcast(hi.astype(BF16).astype(F32), U32)
        out_ref[pl.ds(row0 * PACK_ROWS + c, n, stride=PACK_ROWS), :] = hi_b | (lo_b >> 16)


def _unpack_rows(ref, c, tm, lead=None):
    idx = (pl.ds(c, tm, stride=PACK_ROWS), slice(None))
    if lead is not None:
        idx = (lead,) + idx
    u = ref[idx]
    lo = pltpu.bitcast(u << 16, F32)
    hi = pltpu.bitcast(u & jnp.uint32(0xFFFF0000), F32)
    return lo, hi


def _wo_kernel(m_ref, x_ref, wo_ref, g_ref, wr_ref, x1_ref, h2_ref, h2p_ref, lg_ref, *, tm):
    hw = WO_ROWS
    groups = [slice(h * hw, (h + 1) * hw) for h in range(tm // hw)]
    wo = wo_ref[...]
    x1 = [x_ref[r, :] + _dot(m_ref[r, :], wo) for r in groups]
    h2 = [_rms(v, g_ref[...]) for v in x1]
    w_hi, w_lo = _split_bf16(wr_ref[...])
    for h, r in enumerate(groups):
        x1_ref[r, :] = x1[h]
        h2_ref[r, :] = h2[h].astype(BF16)
        _pack_rows(h2[h], h2p_ref, hw, row0=h * hw)
        h_hi, h_lo = _split_bf16(h2[h])
        lg_ref[:, r] = _dot_nt(w_hi, h_hi) + _dot_nt(w_hi, h_lo) + _dot_nt(w_lo, h_hi)


def _wo(merged, x2, wo, g, wr_t, tm):
    T, D = x2.shape
    E = wr_t.shape[0]
    return pl.pallas_call(
        functools.partial(_wo_kernel, tm=tm),
        grid=(T // tm,),
        in_specs=[
            pl.BlockSpec((tm, D), lambda i: (i, 0)),
            pl.BlockSpec((tm, D), lambda i: (i, 0)),
            pl.BlockSpec((D, D), lambda i: (0, 0), pipeline_mode=pl.Buffered(1)),
            pl.BlockSpec((1, D), lambda i: (0, 0)),
            pl.BlockSpec((E, D), lambda i: (0, 0)),
        ],
        out_specs=[
            pl.BlockSpec((tm, D), lambda i: (i, 0)),
            pl.BlockSpec((tm, D), lambda i: (i, 0)),
            pl.BlockSpec((tm * PACK_ROWS, PACK_W), lambda i: (i, 0)),
            pl.BlockSpec((E, tm), lambda i: (0, i)),
        ],
        out_shape=[jax.ShapeDtypeStruct((T, D), F32),
                   jax.ShapeDtypeStruct((T, D), BF16),
                   jax.ShapeDtypeStruct((T * PACK_ROWS, PACK_W), U32),
                   jax.ShapeDtypeStruct((E, T), F32)],
        compiler_params=_cparams(("parallel",)),
        name="wo_norm_router",
    )(merged, x2, wo, g, wr_t)


def _route_kernel(lg_ref, bias_ref, eidx_ref, wts_ref, rnk_ref, cnt_ref, carry_scr, *, tm):
    E = N_EXPERTS
    NEG = -jnp.inf

    @pl.when(pl.program_id(0) == 0)
    def _():
        carry_scr[...] = jnp.zeros_like(carry_scr)

    scores = _sigmoid(lg_ref[...])
    choice = scores + bias_ref[...]
    c3 = choice.reshape(N_GROUPS, GROUP_SIZE, tm)
    sub = lax.broadcasted_iota(I32, c3.shape, 1)
    m1 = jnp.max(c3, axis=1, keepdims=True)
    i1 = jnp.min(jnp.where(c3 == m1, sub, GROUP_SIZE), axis=1, keepdims=True)
    m2 = jnp.max(jnp.where(sub == i1, NEG, c3), axis=1, keepdims=True)
    gs = (m1 + m2).reshape(N_GROUPS, tm)
    gi = lax.broadcasted_iota(I32, gs.shape, 0)
    gsel = jnp.zeros(gs.shape, F32)
    for _ in range(TOPK_GROUPS):
        mx = jnp.max(gs, axis=0, keepdims=True)
        ix = jnp.min(jnp.where(gs == mx, gi, N_GROUPS), axis=0, keepdims=True)
        hit = gi == ix
        gsel = jnp.where(hit, 1.0, gsel)
        gs = jnp.where(hit, NEG, gs)
    emask = jnp.broadcast_to(gsel.reshape(N_GROUPS, 1, tm), (N_GROUPS, GROUP_SIZE, tm)).reshape(E, tm)
    x = jnp.where(emask > 0.5, choice, NEG)
    ei = lax.broadcasted_iota(I32, x.shape, 0)
    sel = jnp.zeros(x.shape, F32)
    idx_rows, w_rows = [], []
    for _ in range(TOP_K):
        mx = jnp.max(x, axis=0, keepdims=True)
        ix = jnp.min(jnp.where(x == mx, ei, E), axis=0, keepdims=True)
        hit = ei == ix
        w_rows.append(jnp.sum(jnp.where(hit, scores, 0.0), axis=0, keepdims=True))
        idx_rows.append(ix)
        sel = jnp.where(hit, 1.0, sel)
        x = jnp.where(hit, NEG, x)
    w = jnp.concatenate(w_rows, axis=0)
    w = w / (jnp.sum(w, axis=0, keepdims=True) + 1e-20) * ROUTED_SCALE
    eidx_ref[...] = jnp.concatenate(idx_rows, axis=0)
    wts_ref[...] = w

    upper = (lax.broadcasted_iota(I32, (tm, tm), 0) < lax.broadcasted_iota(I32, (tm, tm), 1)).astype(BF16)
    base = carry_scr[...][:, 0:1]
    excl = _dot(sel.astype(BF16), upper) + base
    rnk_ref[...] = jnp.concatenate(
        [jnp.sum(jnp.where(ei == ix, excl, 0.0), axis=0, keepdims=True) for ix in idx_rows],
        axis=0).astype(I32)
    carry_scr[...] = carry_scr[...] + jnp.sum(sel, axis=1, keepdims=True)
    cnt_ref[...] = carry_scr[...]


def _route(logits_t, bias, tm):
    E, T = logits_t.shape
    K = TOP_K
    tok = lambda: pl.BlockSpec((K, tm), lambda i: (0, i))
    return pl.pallas_call(
        functools.partial(_route_kernel, tm=tm),
        grid=(T // tm,),
        in_specs=[pl.BlockSpec((E, tm), lambda i: (0, i)),
                  pl.BlockSpec((E, 1), lambda i: (0, 0))],
        out_specs=[tok(), tok(), tok(), pl.BlockSpec((E, LANES), lambda i: (0, 0))],
        out_shape=[jax.ShapeDtypeStruct((K, T), I32), jax.ShapeDtypeStruct((K, T), F32),
                   jax.ShapeDtypeStruct((K, T), I32), jax.ShapeDtypeStruct((E, LANES), F32)],
        scratch_shapes=[pltpu.VMEM((E, LANES), F32)],
        compiler_params=_cparams(("arbitrary",)),
        name="route",
    )(logits_t, bias)


def _slots_kernel(pst_ref, eidx_ref, rnk_ref, dest_ref):
    eidx = eidx_ref[...]
    dest = rnk_ref[...]
    for e in range(N_EXPERTS):
        dest = dest + jnp.where(eidx == e, pst_ref[e], 0)
    dest_ref[...] = dest


def _slots(eidx, rnk, pstart, tm):
    K, T = eidx.shape
    tok = lambda: pl.BlockSpec((K, tm), lambda i: (0, i))
    return pl.pallas_call(
        _slots_kernel,
        grid=(T // tm,),
        in_specs=[pl.BlockSpec(memory_space=pltpu.SMEM), tok(), tok()],
        out_specs=tok(),
        out_shape=jax.ShapeDtypeStruct((K, T), I32),
        compiler_params=_cparams(("parallel",)),
        name="slots",
    )(pstart, eidx, rnk)


def _ffn_kernel(te_ref, nu_ref, tok_ref, dst_ref, h_hbm, wg_ref, wu_ref, wd_ref, ys_hbm,
                gbuf, obuf, wg_b, wu_b, wd_b, gsem, osem, *, tm):
    s = pl.program_id(0)
    nu = nu_ref[0]
    tile_rows = tm * PACK_ROWS

    def gather(slot):
        for r in range(tm):
            pltpu.make_async_copy(
                h_hbm.at[pl.ds(pl.multiple_of(tok_ref[0, 0, r] * PACK_ROWS, PACK_ROWS), PACK_ROWS), :],
                gbuf.at[slot, pl.ds(r * PACK_ROWS, PACK_ROWS), :], gsem.at[slot]).start()

    def scatter(slot):
        for r in range(tm):
            pltpu.make_async_copy(
                obuf.at[slot, pl.ds(r * PACK_ROWS, PACK_ROWS), :],
                ys_hbm.at[pl.ds(pl.multiple_of(dst_ref[0, 0, r] * PACK_ROWS, PACK_ROWS), PACK_ROWS), :],
                osem.at[slot]).start()

    def wait_gather(slot):
        pltpu.make_async_copy(h_hbm.at[pl.ds(0, tile_rows), :], gbuf.at[slot], gsem.at[slot]).wait()

    def wait_scatter(slot):
        pltpu.make_async_copy(obuf.at[slot], ys_hbm.at[pl.ds(0, tile_rows), :], osem.at[slot]).wait()

    def compute(slot):
        parts = []
        for c in range(PACK_ROWS):
            lo, hi = _unpack_rows(gbuf.at[slot], c, tm)
            parts += [lo.astype(BF16), hi.astype(BF16)]
        a = jnp.concatenate(parts, axis=1)
        hg = _dot(a, wg_b[...])
        hu = _dot(a, wu_b[...])
        hid = (hg * _sigmoid(hg) * hu).astype(BF16)
        _pack_rows(_dot(hid, wd_b[...]), obuf.at[slot], tm)

    cur = jnp.maximum(s - 1, 0)
    @pl.when((s >= 1) & (s <= nu) & ((s == 1) | (te_ref[cur] != te_ref[jnp.maximum(s - 2, 0)])))
    def _():
        wg_b[...] = wg_ref[0].astype(BF16)
        wu_b[...] = wu_ref[0].astype(BF16)
        wd_b[...] = wd_ref[0].astype(BF16)

    @pl.when(s == 0)
    def _():
        gather(0)

    @pl.when(s == 1)
    def _():
        wait_gather(0)
        gather(1)
        compute(0)

    for slot in range(2):
        parity = s % 2 == slot

        @pl.when(parity & (s >= 3) & (s <= nu + 1))
        def _():
            wait_scatter(1 - slot)

        @pl.when(parity & (s >= 2) & (s <= nu))
        def _():
            wait_gather(1 - slot)
            gather(slot)
            scatter(slot)
            compute(1 - slot)

        @pl.when(parity & (s >= 2) & (s == nu + 1))
        def _():
            wait_gather(1 - slot)
            scatter(slot)
            wait_scatter(slot)


def _ffn(tile_e, n_used, row_tok, row_dst, h2p, wg, wu, wd, tm):
    nt = row_tok.shape[0]
    D, I = wg.shape[1], wg.shape[2]
    cur = lambda s: jnp.maximum(s - 1, 0)
    wspec = lambda shape: pl.BlockSpec(shape, lambda s, te, nu: (te[cur(s)], 0, 0))
    grid_spec = pltpu.PrefetchScalarGridSpec(
        num_scalar_prefetch=2,
        grid=(nt + 1,),
        in_specs=[
            pl.BlockSpec((1, 1, tm), lambda s, te, nu: (jnp.minimum(s, nt - 1), 0, 0),
                         memory_space=pltpu.SMEM),
            pl.BlockSpec((1, 1, tm), lambda s, te, nu: (jnp.maximum(s - 2, 0), 0, 0),
                         memory_space=pltpu.SMEM),
            pl.BlockSpec(memory_space=pl.ANY),
            wspec((1, D, I)), wspec((1, D, I)), wspec((1, I, D)),
        ],
        out_specs=pl.BlockSpec(memory_space=pl.ANY),
        scratch_shapes=[pltpu.VMEM((2, tm * PACK_ROWS, PACK_W), U32),
                        pltpu.VMEM((2, tm * PACK_ROWS, PACK_W), U32),
                        pltpu.VMEM((D, I), BF16), pltpu.VMEM((D, I), BF16), pltpu.VMEM((I, D), BF16),
                        pltpu.SemaphoreType.DMA((2,)), pltpu.SemaphoreType.DMA((2,))],
    )
    return pl.pallas_call(
        functools.partial(_ffn_kernel, tm=tm),
        grid_spec=grid_spec,
        out_shape=jax.ShapeDtypeStruct((nt * tm * PACK_ROWS, PACK_W), U32),
        compiler_params=_cparams(("arbitrary",)),
        name="expert_ffn",
    )(tile_e, n_used, row_tok, row_dst, h2p, wg, wu, wd)


def _combine_kernel(*refs, tm):
    ys_refs = refs[:TOP_K]
    wt_ref, x1_ref, h2_ref, wsg_ref, wsu_ref, wsd_ref, o_ref = refs[TOP_K:]
    h2 = h2_ref[...]
    hg = _dot(h2, wsg_ref[...])
    hu = _dot(h2, wsu_ref[...])
    base = x1_ref[...] + _dot((hg * _sigmoid(hg) * hu).astype(BF16), wsd_ref[...])
    wt = wt_ref[...]
    for c in range(PACK_ROWS):
        acc_lo = base[:, c * 2 * LANES:c * 2 * LANES + LANES]
        acc_hi = base[:, c * 2 * LANES + LANES:(c + 1) * 2 * LANES]
        for k in range(TOP_K):
            lo, hi = _unpack_rows(ys_refs[k], c, tm)
            wk = wt[:, k:k + 1]
            acc_lo = acc_lo + wk * lo
            acc_hi = acc_hi + wk * hi
        o_ref[:, c * 2 * LANES:c * 2 * LANES + LANES] = acc_lo
        o_ref[:, c * 2 * LANES + LANES:(c + 1) * 2 * LANES] = acc_hi


def _combine(ys, wts_t, x1, h2, wsg, wsu, wsd, tm):
    T, D = x1.shape
    K = wts_t.shape[1]
    I = wsg.shape[1]
    full = lambda shape: pl.BlockSpec(shape, lambda i: (0, 0))
    return pl.pallas_call(
        functools.partial(_combine_kernel, tm=tm),
        grid=(T // tm,),
        in_specs=[pl.BlockSpec((tm * PACK_ROWS, PACK_W), lambda i, k=k: (k * (T // tm) + i, 0))
                  for k in range(K)] + [
                  pl.BlockSpec((tm, K), lambda i: (i, 0)),
                  pl.BlockSpec((tm, D), lambda i: (i, 0)),
                  pl.BlockSpec((tm, D), lambda i: (i, 0)),
                  full((D, I)), full((D, I)), full((I, D))],
        out_specs=pl.BlockSpec((tm, D), lambda i: (i, 0)),
        out_shape=jax.ShapeDtypeStruct((T, D), F32),
        compiler_params=_cparams(("parallel",)),
        name="combine",
    )(*([ys] * K), wts_t, x1, h2, wsg, wsu, wsd)


def _ple_kernel(xf_ref, xt_ref, p_ref, wp_ref, gpost_ref, gin_ref, wg_ref, o_ref,
                hn_scr, ple_scr, *, tn):
    j = pl.program_id(1)

    @pl.when(j == 0)
    def _():
        hn_scr[...] = _rms(xf_ref[...], gin_ref[...]).astype(BF16)
        ple_scr[...] = _rms(_dot(p_ref[...].astype(BF16), wp_ref[...]), gpost_ref[...])

    cols = pl.ds(pl.multiple_of(j * tn, tn), tn)
    gate = _sigmoid(_dot(hn_scr[...], wg_ref[...]))
    o_ref[...] = xt_ref[...] + gate * ple_scr[:, cols]


def _ple(x2, p2, wp, gpost, gin, wg, tm, tn):
    T, D = x2.shape
    Pd = p2.shape[1]
    return pl.pallas_call(
        functools.partial(_ple_kernel, tn=tn),
        grid=(T // tm, D // tn),
        in_specs=[
            pl.BlockSpec((tm, D), lambda i, j: (i, 0)),
            pl.BlockSpec((tm, tn), lambda i, j: (i, j)),
            pl.BlockSpec((tm, Pd), lambda i, j: (i, 0)),
            pl.BlockSpec((Pd, D), lambda i, j: (0, 0)),
            pl.BlockSpec((1, D), lambda i, j: (0, 0)),
            pl.BlockSpec((1, D), lambda i, j: (0, 0)),
            pl.BlockSpec((D, tn), lambda i, j: (0, j)),
        ],
        out_specs=pl.BlockSpec((tm, tn), lambda i, j: (i, j)),
        out_shape=jax.ShapeDtypeStruct((T, D), F32),
        scratch_shapes=[pltpu.VMEM((tm, D), BF16), pltpu.VMEM((tm, D), F32)],
        compiler_params=_cparams(("parallel", "arbitrary")),
        name="ple",
    )(x2, x2, p2, wp, gpost, gin, wg)


def _pad_cols(w, n):
    return jnp.pad(w, ((0, 0), (0, n - w.shape[1])))


def _pad_rows(w, n):
    return jnp.pad(w, ((0, n - w.shape[0]), (0, 0)))


def _layout_w_in(w):
    W = RWKV_WIDTH
    c = 3 * W
    segs = [w[:, 0:c],
            _pad_cols(w[:, c:c + DECAY_LORA], LORA_PAD),
            _pad_cols(w[:, c + DECAY_LORA:c + DECAY_LORA + AAA_LORA], LORA_PAD),
            _pad_cols(w[:, c + DECAY_LORA + AAA_LORA:c + DECAY_LORA + AAA_LORA + GATE_LORA], GATE_LORA_PAD)]
    c += DECAY_LORA + AAA_LORA + GATE_LORA
    mla_cols = Q_LORA + KV_LORA + QK_ROPE
    segs.append(_pad_cols(w[:, c:c + mla_cols], MLA_PAD))
    c += mla_cols
    segs.append(w[:, c:])
    out = jnp.concatenate(segs, axis=1).astype(BF16)
    assert out.shape[1] == IN_PAD
    return out


def _layout_mu(mu):
    W = RWKV_WIDTH
    c = 3 * W
    segs = [mu[0:c],
            jnp.pad(mu[c:c + DECAY_LORA], (0, LORA_PAD - DECAY_LORA)),
            jnp.pad(mu[c + DECAY_LORA:c + DECAY_LORA + AAA_LORA], (0, LORA_PAD - AAA_LORA)),
            jnp.pad(mu[c + DECAY_LORA + AAA_LORA:], (0, GATE_LORA_PAD - GATE_LORA))]
    return jnp.concatenate(segs).reshape(1, RWKV_PAD)


def _moe_tables(eidx, rnk, counts, tmf, tm_slots):
    K, T = eidx.shape
    E = N_EXPERTS
    A = K * T
    n_tiles = A // tmf + E
    P = n_tiles * tmf
    pcounts = (counts + tmf - 1) // tmf * tmf
    pends = jnp.cumsum(pcounts)
    pstart = (pends - pcounts).astype(I32)
    tile_start = jnp.arange(n_tiles, dtype=I32) * tmf
    tile_e = jnp.minimum(jnp.sum((pends[None, :] <= tile_start[:, None]).astype(I32), axis=1), E - 1)
    n_used = (pends[E - 1:] // tmf).astype(I32)
    dest = _slots(eidx, rnk, pstart, tm_slots)
    j = jnp.arange(tmf, dtype=I32)[None, :]
    tail = j < (pcounts - counts)[:, None]
    spare = jnp.cumsum(jnp.logical_not(tail).reshape(-1).astype(I32)) - 1
    pad_row = jnp.where(tail.reshape(-1), ((pstart + counts)[:, None] + j).reshape(-1),
                        pends[E - 1] + spare)
    q = jnp.arange(P - A, dtype=I32)
    slot_id = jnp.arange(A, dtype=I32)
    rows = jnp.concatenate([dest.reshape(-1), pad_row.astype(I32)])
    slots = jnp.concatenate([slot_id, A + q])
    row_dst = lax.sort((rows, slots), num_keys=1)[1]
    row_tok = jnp.where(row_dst < A, row_dst % T, 0)
    return (row_tok.reshape(n_tiles, 1, tmf), row_dst.reshape(n_tiles, 1, tmf), tile_e, n_used)


def _layer(x, p, positions, g_mix, w_in, mu_rwkv, w0, w2, a0, a2, g2, k_k, k_a, r_k,
           gn_w, gn_b, w_a_up, g_qa, g_kva, w_uq, w_ukv, g_qn, g_kn, w_b_up, w_o,
           g_ffn, w_router, router_bias, w_exp_gate, w_exp_up, w_exp_down,
           w_sh_gate, w_sh_up, w_sh_down, w_ple, g_ple_post, g_ple_in, w_ple_gate,
           *, tiles):
    B, S, D = x.shape
    T = B * S
    W = RWKV_WIDTH
    x2 = x.reshape(T, D)
    row = lambda v: v.reshape(1, -1).astype(F32)

    proj = _inproj(x2, row(g_mix), _layout_w_in(w_in), tiles["tm_in"], tiles["tn_in"])

    head_blk = (jnp.arange(W)[:, None] // RWKV_HEAD == jnp.arange(W)[None, :] // RWKV_HEAD).astype(BF16)
    r, k, v, kk, a, lw, g = _rwkv_prep(
        proj, _layout_mu(mu_rwkv), row(w0), _pad_rows(w2, LORA_PAD).astype(BF16), row(a0),
        _pad_rows(a2, LORA_PAD).astype(BF16), _pad_rows(g2, GATE_LORA_PAD).astype(BF16),
        row(k_k), row(k_a), head_blk, tiles["tm_prep"], S)
    b3 = lambda t: t.reshape(B, S, W)
    o_a = _rwkv_scan(b3(r), b3(k), b3(v), b3(kk), b3(a), b3(lw), b3(g),
                     row(gn_w), row(gn_b), row(r_k), tiles["rows_scan"]).reshape(T, W)

    half = QK_ROPE // 2
    inv_freq = ROPE_THETA ** (-jnp.arange(half, dtype=F32) / half)
    invf = jnp.concatenate([inv_freq, inv_freq, jnp.zeros((LANES - QK_ROPE,), F32)]).reshape(1, LANES)
    wuq = jnp.pad(w_uq.reshape(Q_LORA, MLA_HEADS, QK_HEAD),
                  ((0, 0), (0, 0), (0, QK_PAD - QK_HEAD))).reshape(Q_LORA, MLA_HEADS * QK_PAD)
    pad_g = lambda gv: jnp.pad(gv, (0, QK_PAD - QK_HEAD)).reshape(1, QK_PAD)
    wkv = w_ukv.reshape(KV_LORA, MLA_HEADS, QK_NOPE + V_HEAD)
    wk = wkv[:, :, :QK_NOPE].reshape(KV_LORA, MLA_HEADS * QK_NOPE)
    wvt = jnp.transpose(wkv[:, :, QK_NOPE:].reshape(KV_LORA, MLA_HEADS * V_HEAD))
    q, kx, vx = _mla_prep(proj, positions.reshape(T, 1).astype(I32), invf, row(g_qa), row(g_kva),
                          wuq.astype(BF16), wk.astype(BF16), wvt.astype(BF16), pad_g(g_qn), pad_g(g_kn),
                          tiles["tm_mla"])
    o_b = _flash(q.reshape(B, S, -1), kx.reshape(B, S, -1), vx, tiles["tq"]).reshape(T, MLA_WIDTH)

    merged = _merge(o_a, o_b, w_a_up.astype(BF16), w_b_up.astype(BF16), proj,
                    tiles["tm_merge"], tiles["tn_merge"])

    x1, h2, h2p, logits_t = _wo(merged, x2, w_o.astype(BF16), row(g_ffn),
                                jnp.transpose(w_router).astype(F32), tiles["tm_wo"])
    eidx, wts, rnk, cnt = _route(logits_t, router_bias.reshape(N_EXPERTS, 1).astype(F32),
                                 tiles["tm_route"])
    tmf = tiles["tm_ffn"]
    counts = cnt[:, 0].astype(I32)
    row_tok, row_dst, tile_e, n_used = _moe_tables(eidx, rnk, counts, tmf, tiles["tm_slots"])
    ys = _ffn(tile_e, n_used, row_tok, row_dst, h2p, w_exp_gate, w_exp_up, w_exp_down, tmf)
    x3 = _combine(ys, jnp.transpose(wts), x1, h2,
                  w_sh_gate.astype(BF16), w_sh_up.astype(BF16), w_sh_down.astype(BF16),
                  tiles["tm_comb"])

    out = _ple(x3, p.reshape(T, PLE_DIM), w_ple.astype(BF16), row(g_ple_post), row(g_ple_in),
               w_ple_gate.astype(BF16), tiles["tm_ple"], tiles["tn_ple"])
    return out.reshape(B, S, D)


TILES = dict(tm_in=1024, tn_in=512, tm_prep=256, rows_scan=128, tm_mla=1024, tq=512,
             tm_merge=1024, tn_merge=512, tm_wo=512, tm_route=512, tm_ffn=256,
             tm_slots=2048, tm_comb=128, tm_ple=512, tn_ple=1024)


def kernel(x, p, positions, g_mix, w_in, mu_rwkv, w0, w2, a0, a2, g2, k_k, k_a, r_k, gn_w, gn_b, w_a_up, g_qa, g_kva, w_uq, w_ukv, g_qn, g_kn, w_b_up, w_o, g_ffn, w_router, router_bias, w_exp_gate, w_exp_up, w_exp_down, w_sh_gate, w_sh_up, w_sh_down, w_ple, g_ple_post, g_ple_in, w_ple_gate):
    args = (g_mix, w_in, mu_rwkv, w0, w2, a0, a2, g2, k_k, k_a, r_k, gn_w, gn_b, w_a_up,
            g_qa, g_kva, w_uq, w_ukv, g_qn, g_kn, w_b_up, w_o, g_ffn, w_router, router_bias,
            w_exp_gate, w_exp_up, w_exp_down, w_sh_gate, w_sh_up, w_sh_down, w_ple,
            g_ple_post, g_ple_in, w_ple_gate)
    assert all(t.shape[0] == 1 for t in args), "single-layer stack expected"
    return _layer(x, p[0], positions, *[t[0] for t in args], tiles=TILES)
```

```python
import functools

import jax
import jax.numpy as jnp
from jax import lax
from jax.experimental import pallas as pl
from jax.experimental.pallas import tpu as pltpu

F32 = jnp.float32
BF16 = jnp.bfloat16
I32 = jnp.int32
U32 = jnp.uint32

D_MODEL = 2048
PLE_DIM = 256
NORM_EPS = 1e-6
RWKV_HEADS = 16
RWKV_HEAD = 64
RWKV_WIDTH = RWKV_HEADS * RWKV_HEAD
DECAY_LORA = 64
AAA_LORA = 64
GATE_LORA = 160
GN_EPS = 64e-5
MLA_HEADS = 8
Q_LORA = 512
KV_LORA = 512
QK_NOPE = 128
QK_ROPE = 64
QK_HEAD = QK_NOPE + QK_ROPE
V_HEAD = 128
MLA_WIDTH = MLA_HEADS * V_HEAD
ROPE_THETA = 10000.0
N_EXPERTS = 64
TOP_K = 8
N_GROUPS = 8
GROUP_SIZE = N_EXPERTS // N_GROUPS
TOPK_GROUPS = 4
MOE_INTER = 512
ROUTED_SCALE = 2.5

LANES = 128
QK_PAD = 2 * LANES
PACK_ROWS = 8
PACK_W = D_MODEL // 2 // PACK_ROWS
assert PACK_W == LANES

LORA_PAD = 128
GATE_LORA_PAD = 256
RWKV_PAD = 3 * RWKV_WIDTH + 2 * LORA_PAD + GATE_LORA_PAD
MLA_PAD = 1536
COL_RWKV = 0
COL_MLA = RWKV_PAD
COL_GA = COL_MLA + MLA_PAD
COL_GB = COL_GA + D_MODEL
IN_PAD = COL_GB + D_MODEL

VMEM_LIMIT = 56 * 1024 * 1024
CHUNK = 64
WO_ROWS = 128
SCAN_LANE_BLOCKS = 8


def _cparams(sem):
    return pltpu.CompilerParams(dimension_semantics=sem, vmem_limit_bytes=VMEM_LIMIT)


def _dot(a, b):
    return jnp.dot(a, b, preferred_element_type=F32)


def _dot_nt(a, b):
    return lax.dot_general(a, b, (((1,), (1,)), ((), ())), preferred_element_type=F32)


def _dot_tn(a, b):
    return lax.dot_general(a, b, (((0,), (0,)), ((), ())), preferred_element_type=F32)


def _split_bf16(x):
    hi = x.astype(BF16)
    lo = (x - hi.astype(F32)).astype(BF16)
    return hi, lo


def _rms(x, g):
    ms = jnp.mean(x * x, axis=-1, keepdims=True)
    return x * lax.rsqrt(ms + NORM_EPS) * g


def _sigmoid(x):
    return 1.0 / (1.0 + jnp.exp(-x))


def _inproj_kernel(x_ref, g_ref, w_ref, o_ref, h_scr):
    @pl.when(pl.program_id(1) == 0)
    def _():
        h_scr[...] = _rms(x_ref[...], g_ref[...]).astype(BF16)

    o_ref[...] = _dot(h_scr[...], w_ref[...]).astype(o_ref.dtype)


def _inproj(x2, g, w, tm, tn):
    T, D = x2.shape
    N = w.shape[1]
    return pl.pallas_call(
        _inproj_kernel,
        grid=(T // tm, N // tn),
        in_specs=[
            pl.BlockSpec((tm, D), lambda i, j: (i, 0)),
            pl.BlockSpec((1, D), lambda i, j: (0, 0)),
            pl.BlockSpec((D, tn), lambda i, j: (0, j)),
        ],
        out_specs=pl.BlockSpec((tm, tn), lambda i, j: (i, j)),
        out_shape=jax.ShapeDtypeStruct((T, N), BF16),
        scratch_shapes=[pltpu.VMEM((tm, D), BF16)],
        compiler_params=_cparams(("parallel", "arbitrary")),
        name="inproj",
    )(x2, g, w)


HALO = 16


def _rwkv_prep_kernel(cur_ref, halo_ref, mu_ref, w0_ref, w2_ref, a0_ref, a2_ref, g2_ref,
                      kk_ref, ka_ref, bd_ref,
                      r_out, k_out, v_out, kk_out, a_out, lw_out, g_out, *, tm, seq):
    i = pl.program_id(0)
    cur = cur_ref[...].astype(F32)
    first = (i * tm) % seq == 0
    last_prev = halo_ref[HALO - 1:HALO, :].astype(F32)
    last_prev = jnp.where(first, 0.0, last_prev)
    row = lax.broadcasted_iota(I32, cur.shape, 0)
    prev = jnp.where(row == 0, last_prev, pltpu.roll(cur, 1, 0))
    u = cur + (prev - cur) * mu_ref[...]
    W = RWKV_WIDTH
    r = u[:, 0:W]
    k = u[:, W:2 * W]
    v = u[:, 2 * W:3 * W]
    dw = u[:, 3 * W:3 * W + LORA_PAD]
    da = u[:, 3 * W + LORA_PAD:3 * W + 2 * LORA_PAD]
    dg = u[:, 3 * W + 2 * LORA_PAD:3 * W + 2 * LORA_PAD + GATE_LORA_PAD]
    z = w0_ref[...] + _dot(jnp.tanh(dw).astype(BF16), w2_ref[...])
    w_log = -(jnp.maximum(-z, 0.0) + jnp.log(1.0 + jnp.exp(-jnp.abs(z)))) - 0.5
    lw_out[...] = -jnp.exp(w_log)
    a = _sigmoid(a0_ref[...] + _dot(da.astype(BF16), a2_ref[...]))
    g_out[...] = _dot(_sigmoid(dg).astype(BF16), g2_ref[...]).astype(g_out.dtype)
    kk = k * kk_ref[...]
    hi, lo = _split_bf16(kk * kk)
    ssq = _dot(hi, bd_ref[...]) + _dot(lo, bd_ref[...])
    kk = kk / jnp.maximum(jnp.sqrt(ssq), 1e-12)
    r_out[...] = r.astype(r_out.dtype)
    k_out[...] = (k * (1.0 + (a - 1.0) * ka_ref[...])).astype(k_out.dtype)
    v_out[...] = v.astype(v_out.dtype)
    kk_out[...] = kk.astype(kk_out.dtype)
    a_out[...] = a.astype(a_out.dtype)


def _rwkv_prep(proj, mu, w0, w2, a0, a2, g2, k_k, k_a, bd, tm, seq):
    T = proj.shape[0]
    W = RWKV_WIDTH
    full = lambda shape: pl.BlockSpec(shape, lambda i: (0, 0))
    out = lambda: pl.BlockSpec((tm, W), lambda i: (i, 0))
    return pl.pallas_call(
        functools.partial(_rwkv_prep_kernel, tm=tm, seq=seq),
        grid=(T // tm,),
        in_specs=[
            pl.BlockSpec((tm, RWKV_PAD), lambda i: (i, 0)),
            pl.BlockSpec((HALO, RWKV_PAD), lambda i: (jnp.maximum(i * (tm // HALO) - 1, 0), 0)),
            full((1, RWKV_PAD)), full((1, W)), full((LORA_PAD, W)), full((1, W)),
            full((LORA_PAD, W)), full((GATE_LORA_PAD, W)), full((1, W)), full((1, W)),
            full((W, W)),
        ],
        out_specs=[out() for _ in range(7)],
        out_shape=[jax.ShapeDtypeStruct((T, W), dt)
                   for dt in (BF16, BF16, BF16, BF16, BF16, F32, BF16)],
        compiler_params=_cparams(("parallel",)),
        name="rwkv_prep",
    )(proj, proj, mu, w0, w2, a0, a2, g2, k_k, k_a, bd)


def _rwkv_scan_kernel(r_ref, k_ref, v_ref, kk_ref, a_ref, lw_ref, g_ref,
                      gnw_ref, gnb_ref, rk_ref, o_ref, s_scr, *, rows):
    L = CHUNK
    L2 = 2 * L

    @pl.when(pl.program_id(2) == 0)
    def _():
        s_scr[...] = jnp.zeros_like(s_scr)

    lane = lax.broadcasted_iota(I32, (L, LANES), 1)
    m_a = (lane < RWKV_HEAD).astype(F32)
    m_b = 1.0 - m_a
    ri = lax.broadcasted_iota(I32, (L2, L2), 0)
    ci = lax.broadcasted_iota(I32, (L2, L2), 1)
    strict = ri > ci
    incl = ri >= ci
    eye = (ri == ci).astype(F32)
    tri = (lax.broadcasted_iota(I32, (L, L), 0) >= lax.broadcasted_iota(I32, (L, L), 1)).astype(BF16)
    hr = lax.broadcasted_iota(I32, (LANES, LANES), 0) // RWKV_HEAD
    hc = lax.broadcasted_iota(I32, (LANES, LANES), 1) // RWKV_HEAD
    head_ones = (hr == hc).astype(BF16)
    nb = SCAN_LANE_BLOCKS

    def stack(x):
        return jnp.concatenate([x * m_a, x * m_b], axis=0)

    chunks = range(rows // L)
    rows_of = lambda c: slice(c * L, (c + 1) * L)
    lanes_of = lambda h: slice(h * LANES, (h + 1) * LANES)
    each = lambda f, *lists: [f(*vals) for vals in zip(*lists)]

    def operands(c, h):
        sl, ln = rows_of(c), lanes_of(h)
        r = r_ref[0, sl, ln].astype(F32)
        k = k_ref[0, sl, ln].astype(F32)
        v = v_ref[0, sl, ln].astype(F32)
        kk = kk_ref[0, sl, ln].astype(F32)
        a = a_ref[0, sl, ln].astype(F32)
        lw = lw_ref[0, sl, ln]
        hi, lo = _split_bf16(lw)
        cum = _dot(tri, hi) + _dot(tri, lo)
        c_end = cum[L - 1:L, :]
        p_inv = jnp.exp(-cum)
        p_end = jnp.exp(c_end - cum)
        b = kk * a
        return dict(
            x_a=stack(-(kk * jnp.exp(cum - lw))).astype(BF16),
            x_r=stack(r * jnp.exp(cum)),
            x_bk=jnp.concatenate([stack(b * p_inv), stack(k * p_inv)], axis=0).astype(BF16),
            v_st=stack(v).astype(BF16),
            z_hat=jnp.concatenate([stack(b * p_end), stack(k * p_end)], axis=0).astype(BF16),
            p_row=jnp.exp(c_end),
            rkr=(r * k * rk_ref[:, ln]).astype(BF16), v=v)

    ops = [operands(c, h) for c in chunks for h in range(nb)]
    big = [_dot_nt(jnp.concatenate([o["x_a"], o["x_r"].astype(BF16)], axis=0), o["x_bk"])
           for o in ops]
    n_pow = [jnp.where(strict, m[0:L2, 0:L2], 0.0) for m in big]
    a_ak = [jnp.where(strict, m[0:L2, L2:2 * L2], 0.0).astype(BF16) for m in big]
    a_r = [jnp.concatenate([jnp.where(incl, m[L2:2 * L2, 0:L2], 0.0),
                            jnp.where(incl, m[L2:2 * L2, L2:2 * L2], 0.0)], axis=1).astype(BF16)
           for m in big]

    t_inv = [eye + n for n in n_pow]
    for _ in range(5):
        n_pow = each(lambda n: _dot(n.astype(BF16), n.astype(BF16)), n_pow)
        t_inv = each(lambda t, n: t + _dot(t.astype(BF16), n.astype(BF16)), t_inv, n_pow)

    akv = each(lambda m, o: _dot(m, o["v_st"]).astype(BF16), a_ak, ops)
    w = each(lambda t, o, x: _dot(t.astype(BF16), jnp.concatenate([o["x_a"], x], axis=1)),
             t_inv, ops, akv)
    w1 = [m[:, 0:LANES].astype(BF16) for m in w]
    w2v = each(lambda m, o: jnp.concatenate([m[:, LANES:].astype(BF16), o["v_st"]], axis=0), w, ops)
    g_mat = each(lambda o, m, x: (o["x_r"] + _dot(m[:, 0:L2], x)).astype(BF16), ops, a_r, w1)
    y0 = each(_dot, a_r, w2v)
    m_mat = each(lambda x, o: _dot_tn(x, o["z_hat"][0:L2, :]).astype(BF16), w1, ops)
    c2 = each(lambda x, o: _dot_tn(x, o["z_hat"]), w2v, ops)
    bonus_v = [_dot(o["rkr"], head_ones) * o["v"] for o in ops]

    s = [s_scr[h] for h in range(nb)]
    y = []
    for i in range(len(ops)):
        h = i % nb
        s_b = s[h].astype(BF16)
        y_st = _dot_nt(g_mat[i], s_b) + y0[i]
        s[h] = s[h] * ops[i]["p_row"] + _dot(s_b, m_mat[i]) + c2[i]
        y.append(y_st[0:L, :] + y_st[L:L2, :])
    for h in range(nb):
        s_scr[h] = s[h]

    mean = [_dot(t.astype(BF16), head_ones) * (1.0 / RWKV_HEAD) for t in y]
    yc = each(lambda t, m: t - m, y, mean)
    var = [_dot((t * t).astype(BF16), head_ones) * (1.0 / RWKV_HEAD) for t in yc]
    for i in range(len(ops)):
        sl, ln = rows_of(i // nb), lanes_of(i % nb)
        yn = yc[i] * lax.rsqrt(var[i] + GN_EPS) * gnw_ref[:, ln] + gnb_ref[:, ln]
        o_ref[0, sl, ln] = ((yn + bonus_v[i]) * g_ref[0, sl, ln].astype(F32)).astype(o_ref.dtype)


def _rwkv_scan(r, k, v, kk, a, lw, g, gn_w, gn_b, r_k, rows):
    B, S, W = r.shape
    wb = SCAN_LANE_BLOCKS * LANES
    npair = W // wb
    seq = lambda: pl.BlockSpec((1, rows, wb), lambda b, p, c: (b, c, p))
    par = lambda: pl.BlockSpec((1, wb), lambda b, p, c: (0, p))
    return pl.pallas_call(
        functools.partial(_rwkv_scan_kernel, rows=rows),
        grid=(B, npair, S // rows),
        in_specs=[seq() for _ in range(7)] + [par(), par(), par()],
        out_specs=seq(),
        out_shape=jax.ShapeDtypeStruct((B, S, W), BF16),
        scratch_shapes=[pltpu.VMEM((SCAN_LANE_BLOCKS, LANES, LANES), F32)],
        compiler_params=_cparams(("parallel", "parallel", "arbitrary")),
        name="rwkv_scan",
    )(r, k, v, kk, a, lw, g, gn_w, gn_b, r_k)


def _rope128(x, cos, sin_signed):
    lane = lax.broadcasted_iota(I32, x.shape, 1)
    half = QK_ROPE // 2
    partner = jnp.where(lane < half, pltpu.roll(x, LANES - half, 1), pltpu.roll(x, half, 1))
    return x * cos + partner * sin_signed


def _mla_prep_kernel(cq_ref, ckv_ref, kr_ref, pos_ref, invf_ref, gqa_ref, gkva_ref,
                     wuq_ref, wukv_ref, gqn_ref, gkn_ref,
                     q_out, k_out, v_out, cqn_scr, ckvn_scr, cos_scr, sin_scr):
    @pl.when(pl.program_id(1) == 0)
    def _():
        cqn_scr[...] = _rms(cq_ref[...].astype(F32), gqa_ref[...]).astype(BF16)
        ckvn_scr[...] = _rms(ckv_ref[...].astype(F32), gkva_ref[...]).astype(BF16)
        ang = pos_ref[...].astype(F32) * invf_ref[...]
        lane = lax.broadcasted_iota(I32, ang.shape, 1)
        cos_scr[...] = jnp.cos(ang)
        sin_scr[...] = jnp.where(lane < QK_ROPE // 2, -1.0, 1.0) * jnp.sin(ang)

    cos = cos_scr[...]
    sin = sin_scr[...]
    inv_n = 1.0 / QK_HEAD
    q = _dot(cqn_scr[...], wuq_ref[...])
    rs = lax.rsqrt(jnp.sum(q * q, axis=-1, keepdims=True) * inv_n + NORM_EPS)
    q = q * rs * gqn_ref[...] * (QK_HEAD ** -0.5 * LOG2E)
    q_out[...] = jnp.concatenate(
        [q[:, 0:LANES], _rope128(q[:, LANES:], cos, sin)], axis=1).astype(q_out.dtype)

    kv = _dot(ckvn_scr[...], wukv_ref[...])
    k_nope = kv[:, 0:QK_NOPE]
    kr = kr_ref[...].astype(F32)
    ssq = (jnp.sum(k_nope * k_nope, axis=-1, keepdims=True)
           + jnp.sum(kr * kr, axis=-1, keepdims=True))
    rs = lax.rsqrt(ssq * inv_n + NORM_EPS)
    gkn = gkn_ref[...]
    k_out[...] = jnp.concatenate(
        [k_nope * rs * gkn[:, 0:LANES], _rope128(kr * rs * gkn[:, LANES:], cos, sin)],
        axis=1).astype(k_out.dtype)
    v_out[...] = jnp.transpose(kv[:, QK_NOPE:].astype(v_out.dtype))


def _mla_prep(proj, pos, invf, g_qa, g_kva, wuq, wukv, gqn, gkn, tm):
    T = proj.shape[0]
    H = MLA_HEADS
    c0 = COL_MLA // Q_LORA
    ckr = (COL_MLA + Q_LORA + KV_LORA) // LANES
    full = lambda shape: pl.BlockSpec(shape, lambda i, h: (0, 0))
    return pl.pallas_call(
        _mla_prep_kernel,
        grid=(T // tm, H),
        in_specs=[
            pl.BlockSpec((tm, Q_LORA), lambda i, h: (i, c0)),
            pl.BlockSpec((tm, KV_LORA), lambda i, h: (i, c0 + 1)),
            pl.BlockSpec((tm, LANES), lambda i, h: (i, ckr)),
            pl.BlockSpec((tm, 1), lambda i, h: (i, 0)),
            full((1, LANES)), full((1, Q_LORA)), full((1, KV_LORA)),
            pl.BlockSpec((Q_LORA, QK_PAD), lambda i, h: (0, h)),
            pl.BlockSpec((KV_LORA, QK_NOPE + V_HEAD), lambda i, h: (0, h)),
            full((1, QK_PAD)), full((1, QK_PAD)),
        ],
        out_specs=[
            pl.BlockSpec((tm, QK_PAD), lambda i, h: (i, h)),
            pl.BlockSpec((tm, QK_PAD), lambda i, h: (i, h)),
            pl.BlockSpec((V_HEAD, tm), lambda i, h: (h, i)),
        ],
        out_shape=[jax.ShapeDtypeStruct((T, H * QK_PAD), BF16),
                   jax.ShapeDtypeStruct((T, H * QK_PAD), BF16),
                   jax.ShapeDtypeStruct((H * V_HEAD, T), BF16)],
        scratch_shapes=[pltpu.VMEM((tm, Q_LORA), BF16), pltpu.VMEM((tm, KV_LORA), BF16),
                        pltpu.VMEM((tm, LANES), F32), pltpu.VMEM((tm, LANES), F32)],
        compiler_params=_cparams(("parallel", "arbitrary")),
        name="mla_prep",
    )(proj, proj, proj, pos, invf, g_qa, g_kva, wuq, wukv, gqn, gkn)


MASK_VALUE = -1e30
LOG2E = 1.4426950408889634


def _flash_kernel(q_ref, k_ref, vt_ref, o_ref, s_scr, p_scr, alpha_scr, m_scr, l_scr, acc_scr,
                  *, tq):
    i = pl.program_id(2)
    q = q_ref[0]
    m_scr[...] = jnp.full_like(m_scr, MASK_VALUE)
    l_scr[...] = jnp.zeros_like(l_scr)
    acc_scr[...] = jnp.zeros_like(acc_scr)

    q_half = tq // 2

    def scores(j, slot, half):
        ks = pl.ds(pl.multiple_of(j * tq, tq), tq)
        cols = slice(half * q_half, (half + 1) * q_half)
        s_scr[slot, :, cols] = _dot_nt(k_ref[0, ks, :], q[cols, :])

    def softmax(slot, masked, strips):
        for c in strips:
            cols = slice(c * LANES, (c + 1) * LANES)
            s = s_scr[slot, :, cols]
            if masked:
                kpos = lax.broadcasted_iota(I32, s.shape, 0)
                qpos = lax.broadcasted_iota(I32, s.shape, 1) + c * LANES
                s = jnp.where(kpos <= qpos, s, MASK_VALUE)
            m_old = m_scr[:, cols]
            m_new = jnp.maximum(m_old, jnp.max(s, axis=0, keepdims=True))
            p = jnp.exp2(s - m_new)
            alpha = jnp.exp2(m_old - m_new)
            l_scr[:, cols] = alpha * l_scr[:, cols] + jnp.sum(p, axis=0, keepdims=True)
            m_scr[:, cols] = m_new
            p_scr[slot, :, cols] = p.astype(BF16)
            alpha_scr[slot, :, cols] = alpha

    def weighted_values(j, slot):
        vt = vt_ref[:, pl.ds(pl.multiple_of(j * tq, tq), tq)]
        acc_scr[...] = alpha_scr[slot] * acc_scr[...] + _dot(vt, p_scr[slot])

    scores(0, 0, 0)
    scores(0, 0, 1)
    p_scr[1] = jnp.zeros((tq, tq), BF16)
    alpha_scr[1] = jnp.ones((1, tq), F32)
    n_strips = tq // LANES
    first, second = range(0, n_strips // 2), range(n_strips // 2, n_strips)

    def step(j, slot):
        weighted_values(jnp.maximum(j - 1, 0), 1 - slot)
        scores(j + 1, 1 - slot, 0)
        softmax(slot, False, first)
        scores(j + 1, 1 - slot, 1)
        softmax(slot, False, second)

    def pair(jp, carry):
        step(2 * jp, 0)
        step(2 * jp + 1, 1)
        return carry

    def finish(slot):
        weighted_values(jnp.maximum(i - 1, 0), 1 - slot)
        softmax(slot, True, range(n_strips))
        weighted_values(i, slot)
        o_ref[0] = jnp.transpose(acc_scr[...] / l_scr[...]).astype(o_ref.dtype)

    lax.fori_loop(0, i // 2, pair, 0)

    @pl.when(i % 2 == 0)
    def _():
        finish(0)

    @pl.when(i % 2 == 1)
    def _():
        step(i - 1, 0)
        finish(1)


def _flash(q, k, vt, tq):
    B, S, _ = q.shape
    H = MLA_HEADS
    return pl.pallas_call(
        functools.partial(_flash_kernel, tq=tq),
        grid=(B, H, S // tq),
        in_specs=[
            pl.BlockSpec((1, tq, QK_PAD), lambda b, h, i: (b, i, h)),
            pl.BlockSpec((1, S, QK_PAD), lambda b, h, i: (b, 0, h)),
            pl.BlockSpec((V_HEAD, S), lambda b, h, i: (h, b)),
        ],
        out_specs=pl.BlockSpec((1, tq, V_HEAD), lambda b, h, i: (b, i, h)),
        out_shape=jax.ShapeDtypeStruct((B, S, H * V_HEAD), BF16),
        scratch_shapes=[pltpu.VMEM((2, tq, tq), F32), pltpu.VMEM((2, tq, tq), BF16),
                        pltpu.VMEM((2, 1, tq), F32),
                        pltpu.VMEM((1, tq), F32), pltpu.VMEM((1, tq), F32),
                        pltpu.VMEM((V_HEAD, tq), F32)],
        compiler_params=_cparams(("parallel", "parallel", "arbitrary")),
        name="flash_attn",
    )(q, k, vt)


def _merge_kernel(oa_ref, ob_ref, wa_ref, wb_ref, ga_ref, gb_ref, o_ref):
    ya = _dot(oa_ref[...], wa_ref[...])
    yb = _dot(ob_ref[...], wb_ref[...])
    ga = _sigmoid(ga_ref[...].astype(F32))
    gb = _sigmoid(gb_ref[...].astype(F32))
    o_ref[...] = (ga * ya + gb * yb).astype(o_ref.dtype)


def _merge(oa, ob, wa, wb, proj, tm, tn):
    T, K = oa.shape
    N = wa.shape[1]
    ca = COL_GA // tn
    cb = COL_GB // tn
    return pl.pallas_call(
        _merge_kernel,
        grid=(T // tm, N // tn),
        in_specs=[
            pl.BlockSpec((tm, K), lambda i, j: (i, 0)),
            pl.BlockSpec((tm, K), lambda i, j: (i, 0)),
            pl.BlockSpec((K, tn), lambda i, j: (0, j)),
            pl.BlockSpec((K, tn), lambda i, j: (0, j)),
            pl.BlockSpec((tm, tn), lambda i, j: (i, ca + j)),
            pl.BlockSpec((tm, tn), lambda i, j: (i, cb + j)),
        ],
        out_specs=pl.BlockSpec((tm, tn), lambda i, j: (i, j)),
        out_shape=jax.ShapeDtypeStruct((T, N), BF16),
        compiler_params=_cparams(("parallel", "parallel")),
        name="merge",
    )(oa, ob, wa, wb, proj, proj)


def _pack_rows(x, out_ref, n, row0=0):
    for c in range(PACK_ROWS):
        lo = x[:, c * 2 * LANES:c * 2 * LANES + LANES]
        hi = x[:, c * 2 * LANES + LANES:(c + 1) * 2 * LANES]
        lo_b = pltpu.bitcast(lo.astype(BF16).astype(F32), U32)
        hi_b = pltpu.bitcast(hi.astype(BF16).astype(F32), U32)
        out_ref[pl.ds(row0 * PACK_ROWS + c, n, stride=PACK_ROWS), :] = hi_b | (lo_b >> 16)


def _unpack_rows(ref, c, tm, lead=None):
    idx = (pl.ds(c, tm, stride=PACK_ROWS), slice(None))
    if lead is not None:
        idx = (lead,) + idx
    u = ref[idx]
    lo = pltpu.bitcast(u << 16, F32)
    hi = pltpu.bitcast(u & jnp.uint32(0xFFFF0000), F32)
    return lo, hi


def _wo_kernel(m_ref, x_ref, wo_ref, g_ref, wr_ref, x1_ref, h2_ref, h2p_ref, lg_ref, *, tm):
    hw = WO_ROWS
    groups = [slice(h * hw, (h + 1) * hw) for h in range(tm // hw)]
    wo = wo_ref[...]
    x1 = [x_ref[r, :] + _dot(m_ref[r, :], wo) for r in groups]
    h2 = [_rms(v, g_ref[...]) for v in x1]
    w_hi, w_lo = _split_bf16(wr_ref[...])
    for h, r in enumerate(groups):
        x1_ref[r, :] = x1[h]
        h2_ref[r, :] = h2[h].astype(BF16)
        _pack_rows(h2[h], h2p_ref, hw, row0=h * hw)
        h_hi, h_lo = _split_bf16(h2[h])
        lg_ref[:, r] = _dot_nt(w_hi, h_hi) + _dot_nt(w_hi, h_lo) + _dot_nt(w_lo, h_hi)


def _wo(merged, x2, wo, g, wr_t, tm):
    T, D = x2.shape
    E = wr_t.shape[0]
    return pl.pallas_call(
        functools.partial(_wo_kernel, tm=tm),
        grid=(T // tm,),
        in_specs=[
            pl.BlockSpec((tm, D), lambda i: (i, 0)),
            pl.BlockSpec((tm, D), lambda i: (i, 0)),
            pl.BlockSpec((D, D), lambda i: (0, 0), pipeline_mode=pl.Buffered(1)),
            pl.BlockSpec((1, D), lambda i: (0, 0)),
            pl.BlockSpec((E, D), lambda i: (0, 0)),
        ],
        out_specs=[
            pl.BlockSpec((tm, D), lambda i: (i, 0)),
            pl.BlockSpec((tm, D), lambda i: (i, 0)),
            pl.BlockSpec((tm * PACK_ROWS, PACK_W), lambda i: (i, 0)),
            pl.BlockSpec((E, tm), lambda i: (0, i)),
        ],
        out_shape=[jax.ShapeDtypeStruct((T, D), F32),
                   jax.ShapeDtypeStruct((T, D), BF16),
                   jax.ShapeDtypeStruct((T * PACK_ROWS, PACK_W), U32),
                   jax.ShapeDtypeStruct((E, T), F32)],
        compiler_params=_cparams(("parallel",)),
        name="wo_norm_router",
    )(merged, x2, wo, g, wr_t)


def _route_kernel(lg_ref, bias_ref, eidx_ref, wts_ref, rnk_ref, cnt_ref, carry_scr, *, tm):
    E = N_EXPERTS
    NEG = -jnp.inf

    @pl.when(pl.program_id(0) == 0)
    def _():
        carry_scr[...] = jnp.zeros_like(carry_scr)

    scores = _sigmoid(lg_ref[...])
    choice = scores + bias_ref[...]
    c3 = choice.reshape(N_GROUPS, GROUP_SIZE, tm)
    sub = lax.broadcasted_iota(I32, c3.shape, 1)
    m1 = jnp.max(c3, axis=1, keepdims=True)
    i1 = jnp.min(jnp.where(c3 == m1, sub, GROUP_SIZE), axis=1, keepdims=True)
    m2 = jnp.max(jnp.where(sub == i1, NEG, c3), axis=1, keepdims=True)
    gs = (m1 + m2).reshape(N_GROUPS, tm)
    gi = lax.broadcasted_iota(I32, gs.shape, 0)
    gsel = jnp.zeros(gs.shape, F32)
    for _ in range(TOPK_GROUPS):
        mx = jnp.max(gs, axis=0, keepdims=True)
        ix = jnp.min(jnp.where(gs == mx, gi, N_GROUPS), axis=0, keepdims=True)
        hit = gi == ix
        gsel = jnp.where(hit, 1.0, gsel)
        gs = jnp.where(hit, NEG, gs)
    emask = jnp.broadcast_to(gsel.reshape(N_GROUPS, 1, tm), (N_GROUPS, GROUP_SIZE, tm)).reshape(E, tm)
    x = jnp.where(emask > 0.5, choice, NEG)
    ei = lax.broadcasted_iota(I32, x.shape, 0)
    sel = jnp.zeros(x.shape, F32)
    idx_rows, w_rows = [], []
    for _ in range(TOP_K):
        mx = jnp.max(x, axis=0, keepdims=True)
        ix = jnp.min(jnp.where(x == mx, ei, E), axis=0, keepdims=True)
        hit = ei == ix
        w_rows.append(jnp.sum(jnp.where(hit, scores, 0.0), axis=0, keepdims=True))
        idx_rows.append(ix)
        sel = jnp.where(hit, 1.0, sel)
        x = jnp.where(hit, NEG, x)
    w = jnp.concatenate(w_rows, axis=0)
    w = w / (jnp.sum(w, axis=0, keepdims=True) + 1e-20) * ROUTED_SCALE
    eidx_ref[...] = jnp.concatenate(idx_rows, axis=0)
    wts_ref[...] = w

    upper = (lax.broadcasted_iota(I32, (tm, tm), 0) < lax.broadcasted_iota(I32, (tm, tm), 1)).astype(BF16)
    base = carry_scr[...][:, 0:1]
    excl = _dot(sel.astype(BF16), upper) + base
    rnk_ref[...] = jnp.concatenate(
        [jnp.sum(jnp.where(ei == ix, excl, 0.0), axis=0, keepdims=True) for ix in idx_rows],
        axis=0).astype(I32)
    carry_scr[...] = carry_scr[...] + jnp.sum(sel, axis=1, keepdims=True)
    cnt_ref[...] = carry_scr[...]


def _route(logits_t, bias, tm):
    E, T = logits_t.shape
    K = TOP_K
    tok = lambda: pl.BlockSpec((K, tm), lambda i: (0, i))
    return pl.pallas_call(
        functools.partial(_route_kernel, tm=tm),
        grid=(T // tm,),
        in_specs=[pl.BlockSpec((E, tm), lambda i: (0, i)),
                  pl.BlockSpec((E, 1), lambda i: (0, 0))],
        out_specs=[tok(), tok(), tok(), pl.BlockSpec((E, LANES), lambda i: (0, 0))],
        out_shape=[jax.ShapeDtypeStruct((K, T), I32), jax.ShapeDtypeStruct((K, T), F32),
                   jax.ShapeDtypeStruct((K, T), I32), jax.ShapeDtypeStruct((E, LANES), F32)],
        scratch_shapes=[pltpu.VMEM((E, LANES), F32)],
        compiler_params=_cparams(("arbitrary",)),
        name="route",
    )(logits_t, bias)


def _slots_kernel(pst_ref, eidx_ref, rnk_ref, dest_ref):
    eidx = eidx_ref[...]
    dest = rnk_ref[...]
    for e in range(N_EXPERTS):
        dest = dest + jnp.where(eidx == e, pst_ref[e], 0)
    dest_ref[...] = dest


def _slots(eidx, rnk, pstart, tm):
    K, T = eidx.shape
    tok = lambda: pl.BlockSpec((K, tm), lambda i: (0, i))
    return pl.pallas_call(
        _slots_kernel,
        grid=(T // tm,),
        in_specs=[pl.BlockSpec(memory_space=pltpu.SMEM), tok(), tok()],
        out_specs=tok(),
        out_shape=jax.ShapeDtypeStruct((K, T), I32),
        compiler_params=_cparams(("parallel",)),
        name="slots",
    )(pstart, eidx, rnk)


def _ffn_kernel(te_ref, nu_ref, tok_ref, dst_ref, h_hbm, wg_ref, wu_ref, wd_ref, ys_hbm,
                gbuf, obuf, wg_b, wu_b, wd_b, gsem, osem, *, tm):
    s = pl.program_id(0)
    nu = nu_ref[0]
    tile_rows = tm * PACK_ROWS

    def gather(slot):
        for r in range(tm):
            pltpu.make_async_copy(
                h_hbm.at[pl.ds(pl.multiple_of(tok_ref[0, 0, r] * PACK_ROWS, PACK_ROWS), PACK_ROWS), :],
                gbuf.at[slot, pl.ds(r * PACK_ROWS, PACK_ROWS), :], gsem.at[slot]).start()

    def scatter(slot):
        for r in range(tm):
            pltpu.make_async_copy(
                obuf.at[slot, pl.ds(r * PACK_ROWS, PACK_ROWS), :],
                ys_hbm.at[pl.ds(pl.multiple_of(dst_ref[0, 0, r] * PACK_ROWS, PACK_ROWS), PACK_ROWS), :],
                osem.at[slot]).start()

    def wait_gather(slot):
        pltpu.make_async_copy(h_hbm.at[pl.ds(0, tile_rows), :], gbuf.at[slot], gsem.at[slot]).wait()

    def wait_scatter(slot):
        pltpu.make_async_copy(obuf.at[slot], ys_hbm.at[pl.ds(0, tile_rows), :], osem.at[slot]).wait()

    def compute(slot):
        parts = []
        for c in range(PACK_ROWS):
            lo, hi = _unpack_rows(gbuf.at[slot], c, tm)
            parts += [lo.astype(BF16), hi.astype(BF16)]
        a = jnp.concatenate(parts, axis=1)
        hg = _dot(a, wg_b[...])
        hu = _dot(a, wu_b[...])
        hid = (hg * _sigmoid(hg) * hu).astype(BF16)
        _pack_rows(_dot(hid, wd_b[...]), obuf.at[slot], tm)

    cur = jnp.maximum(s - 1, 0)
    @pl.when((s >= 1) & (s <= nu) & ((s == 1) | (te_ref[cur] != te_ref[jnp.maximum(s - 2, 0)])))
    def _():
        wg_b[...] = wg_ref[0].astype(BF16)
        wu_b[...] = wu_ref[0].astype(BF16)
        wd_b[...] = wd_ref[0].astype(BF16)

    @pl.when(s == 0)
    def _():
        gather(0)

    @pl.when(s == 1)
    def _():
        wait_gather(0)
        gather(1)
        compute(0)

    for slot in range(2):
        parity = s % 2 == slot

        @pl.when(parity & (s >= 3) & (s <= nu + 1))
        def _():
            wait_scatter(1 - slot)

        @pl.when(parity & (s >= 2) & (s <= nu))
        def _():
            wait_gather(1 - slot)
            gather(slot)
            scatter(slot)
            compute(1 - slot)

        @pl.when(parity & (s >= 2) & (s == nu + 1))
        def _():
            wait_gather(1 - slot)
            scatter(slot)
            wait_scatter(slot)


def _ffn(tile_e, n_used, row_tok, row_dst, h2p, wg, wu, wd, tm):
    nt = row_tok.shape[0]
    D, I = wg.shape[1], wg.shape[2]
    cur = lambda s: jnp.maximum(s - 1, 0)
    wspec = lambda shape: pl.BlockSpec(shape, lambda s, te, nu: (te[cur(s)], 0, 0))
    grid_spec = pltpu.PrefetchScalarGridSpec(
        num_scalar_prefetch=2,
        grid=(nt + 1,),
        in_specs=[
            pl.BlockSpec((1, 1, tm), lambda s, te, nu: (jnp.minimum(s, nt - 1), 0, 0),
                         memory_space=pltpu.SMEM),
            pl.BlockSpec((1, 1, tm), lambda s, te, nu: (jnp.maximum(s - 2, 0), 0, 0),
                         memory_space=pltpu.SMEM),
            pl.BlockSpec(memory_space=pl.ANY),
            wspec((1, D, I)), wspec((1, D, I)), wspec((1, I, D)),
        ],
        out_specs=pl.BlockSpec(memory_space=pl.ANY),
        scratch_shapes=[pltpu.VMEM((2, tm * PACK_ROWS, PACK_W), U32),
                        pltpu.VMEM((2, tm * PACK_ROWS, PACK_W), U32),
                        pltpu.VMEM((D, I), BF16), pltpu.VMEM((D, I), BF16), pltpu.VMEM((I, D), BF16),
                        pltpu.SemaphoreType.DMA((2,)), pltpu.SemaphoreType.DMA((2,))],
    )
    return pl.pallas_call(
        functools.partial(_ffn_kernel, tm=tm),
        grid_spec=grid_spec,
        out_shape=jax.ShapeDtypeStruct((nt * tm * PACK_ROWS, PACK_W), U32),
        compiler_params=_cparams(("arbitrary",)),
        name="expert_ffn",
    )(tile_e, n_used, row_tok, row_dst, h2p, wg, wu, wd)


def _combine_kernel(*refs, tm):
    ys_refs = refs[:TOP_K]
    wt_ref, x1_ref, h2_ref, wsg_ref, wsu_ref, wsd_ref, o_ref = refs[TOP_K:]
    h2 = h2_ref[...]
    hg = _dot(h2, wsg_ref[...])
    hu = _dot(h2, wsu_ref[...])
    base = x1_ref[...] + _dot((hg * _sigmoid(hg) * hu).astype(BF16), wsd_ref[...])
    wt = wt_ref[...]
    for c in range(PACK_ROWS):
        acc_lo = base[:, c * 2 * LANES:c * 2 * LANES + LANES]
        acc_hi = base[:, c * 2 * LANES + LANES:(c + 1) * 2 * LANES]
        for k in range(TOP_K):
            lo, hi = _unpack_rows(ys_refs[k], c, tm)
            wk = wt[:, k:k + 1]
            acc_lo = acc_lo + wk * lo
            acc_hi = acc_hi + wk * hi
        o_ref[:, c * 2 * LANES:c * 2 * LANES + LANES] = acc_lo
        o_ref[:, c * 2 * LANES + LANES:(c + 1) * 2 * LANES] = acc_hi


def _combine(ys, wts_t, x1, h2, wsg, wsu, wsd, tm):
    T, D = x1.shape
    K = wts_t.shape[1]
    I = wsg.shape[1]
    full = lambda shape: pl.BlockSpec(shape, lambda i: (0, 0))
    return pl.pallas_call(
        functools.partial(_combine_kernel, tm=tm),
        grid=(T // tm,),
        in_specs=[pl.BlockSpec((tm * PACK_ROWS, PACK_W), lambda i, k=k: (k * (T // tm) + i, 0))
                  for k in range(K)] + [
                  pl.BlockSpec((tm, K), lambda i: (i, 0)),
                  pl.BlockSpec((tm, D), lambda i: (i, 0)),
                  pl.BlockSpec((tm, D), lambda i: (i, 0)),
                  full((D, I)), full((D, I)), full((I, D))],
        out_specs=pl.BlockSpec((tm, D), lambda i: (i, 0)),
        out_shape=jax.ShapeDtypeStruct((T, D), F32),
        compiler_params=_cparams(("parallel",)),
        name="combine",
    )(*([ys] * K), wts_t, x1, h2, wsg, wsu, wsd)


def _ple_kernel(x_ref, p_ref, wp_ref, gpost_ref, gin_ref, wg_ref, o_ref, *, tm):
    hw = WO_ROWS
    groups = [slice(h * hw, (h + 1) * hw) for h in range(tm // hw)]
    wg = wg_ref[...]
    gates, ples = [], []
    for r in groups:
        hn = _rms(x_ref[r, :], gin_ref[...]).astype(BF16)
        gates.append(_dot(hn, wg))
        ples.append(_rms(_dot(p_ref[r, :].astype(BF16), wp_ref[...]), gpost_ref[...]))
    for r, gate, ple in zip(groups, gates, ples):
        o_ref[r, :] = x_ref[r, :] + _sigmoid(gate) * ple


def _ple(x2, p2, wp, gpost, gin, wg, tm):
    T, D = x2.shape
    Pd = p2.shape[1]
    full = lambda shape: pl.BlockSpec(shape, lambda i: (0, 0))
    return pl.pallas_call(
        functools.partial(_ple_kernel, tm=tm),
        grid=(T // tm,),
        in_specs=[
            pl.BlockSpec((tm, D), lambda i: (i, 0)),
            pl.BlockSpec((tm, Pd), lambda i: (i, 0)),
            full((Pd, D)), full((1, D)), full((1, D)),
            pl.BlockSpec((D, D), lambda i: (0, 0), pipeline_mode=pl.Buffered(1)),
        ],
        out_specs=pl.BlockSpec((tm, D), lambda i: (i, 0)),
        out_shape=jax.ShapeDtypeStruct((T, D), F32),
        compiler_params=_cparams(("parallel",)),
        name="ple",
    )(x2, p2, wp, gpost, gin, wg)


def _pad_cols(w, n):
    return jnp.pad(w, ((0, 0), (0, n - w.shape[1])))


def _pad_rows(w, n):
    return jnp.pad(w, ((0, n - w.shape[0]), (0, 0)))


def _layout_w_in(w):
    W = RWKV_WIDTH
    c = 3 * W
    segs = [w[:, 0:c],
            _pad_cols(w[:, c:c + DECAY_LORA], LORA_PAD),
            _pad_cols(w[:, c + DECAY_LORA:c + DECAY_LORA + AAA_LORA], LORA_PAD),
            _pad_cols(w[:, c + DECAY_LORA + AAA_LORA:c + DECAY_LORA + AAA_LORA + GATE_LORA], GATE_LORA_PAD)]
    c += DECAY_LORA + AAA_LORA + GATE_LORA
    mla_cols = Q_LORA + KV_LORA + QK_ROPE
    segs.append(_pad_cols(w[:, c:c + mla_cols], MLA_PAD))
    c += mla_cols
    segs.append(w[:, c:])
    out = jnp.concatenate(segs, axis=1).astype(BF16)
    assert out.shape[1] == IN_PAD
    return out


def _layout_mu(mu):
    W = RWKV_WIDTH
    c = 3 * W
    segs = [mu[0:c],
            jnp.pad(mu[c:c + DECAY_LORA], (0, LORA_PAD - DECAY_LORA)),
            jnp.pad(mu[c + DECAY_LORA:c + DECAY_LORA + AAA_LORA], (0, LORA_PAD - AAA_LORA)),
            jnp.pad(mu[c + DECAY_LORA + AAA_LORA:], (0, GATE_LORA_PAD - GATE_LORA))]
    return jnp.concatenate(segs).reshape(1, RWKV_PAD)


def _moe_tables(eidx, rnk, counts, tmf, tm_slots):
    K, T = eidx.shape
    E = N_EXPERTS
    A = K * T
    n_tiles = A // tmf + E
    P = n_tiles * tmf
    pcounts = (counts + tmf - 1) // tmf * tmf
    pends = jnp.cumsum(pcounts)
    pstart = (pends - pcounts).astype(I32)
    tile_start = jnp.arange(n_tiles, dtype=I32) * tmf
    tile_e = jnp.minimum(jnp.sum((pends[None, :] <= tile_start[:, None]).astype(I32), axis=1), E - 1)
    n_used = (pends[E - 1:] // tmf).astype(I32)
    dest = _slots(eidx, rnk, pstart, tm_slots)
    j = jnp.arange(tmf, dtype=I32)[None, :]
    tail = j < (pcounts - counts)[:, None]
    spare = jnp.cumsum(jnp.logical_not(tail).reshape(-1).astype(I32)) - 1
    pad_row = jnp.where(tail.reshape(-1), ((pstart + counts)[:, None] + j).reshape(-1),
                        pends[E - 1] + spare)
    q = jnp.arange(P - A, dtype=I32)
    slot_id = jnp.arange(A, dtype=I32)
    rows = jnp.concatenate([dest.reshape(-1), pad_row.astype(I32)])
    slots = jnp.concatenate([slot_id, A + q])
    row_dst = lax.sort((rows, slots), num_keys=1)[1]
    row_tok = jnp.where(row_dst < A, row_dst % T, 0)
    return (row_tok.reshape(n_tiles, 1, tmf), row_dst.reshape(n_tiles, 1, tmf), tile_e, n_used)


def _layer(x, p, positions, g_mix, w_in, mu_rwkv, w0, w2, a0, a2, g2, k_k, k_a, r_k,
           gn_w, gn_b, w_a_up, g_qa, g_kva, w_uq, w_ukv, g_qn, g_kn, w_b_up, w_o,
           g_ffn, w_router, router_bias, w_exp_gate, w_exp_up, w_exp_down,
           w_sh_gate, w_sh_up, w_sh_down, w_ple, g_ple_post, g_ple_in, w_ple_gate,
           *, tiles):
    B, S, D = x.shape
    T = B * S
    W = RWKV_WIDTH
    x2 = x.reshape(T, D)
    row = lambda v: v.reshape(1, -1).astype(F32)

    proj = _inproj(x2, row(g_mix), _layout_w_in(w_in), tiles["tm_in"], tiles["tn_in"])

    head_blk = (jnp.arange(W)[:, None] // RWKV_HEAD == jnp.arange(W)[None, :] // RWKV_HEAD).astype(BF16)
    r, k, v, kk, a, lw, g = _rwkv_prep(
        proj, _layout_mu(mu_rwkv), row(w0), _pad_rows(w2, LORA_PAD).astype(BF16), row(a0),
        _pad_rows(a2, LORA_PAD).astype(BF16), _pad_rows(g2, GATE_LORA_PAD).astype(BF16),
        row(k_k), row(k_a), head_blk, tiles["tm_prep"], S)
    b3 = lambda t: t.reshape(B, S, W)
    o_a = _rwkv_scan(b3(r), b3(k), b3(v), b3(kk), b3(a), b3(lw), b3(g),
                     row(gn_w), row(gn_b), row(r_k), tiles["rows_scan"]).reshape(T, W)

    half = QK_ROPE // 2
    inv_freq = ROPE_THETA ** (-jnp.arange(half, dtype=F32) / half)
    invf = jnp.concatenate([inv_freq, inv_freq, jnp.zeros((LANES - QK_ROPE,), F32)]).reshape(1, LANES)
    wuq = jnp.pad(w_uq.reshape(Q_LORA, MLA_HEADS, QK_HEAD),
                  ((0, 0), (0, 0), (0, QK_PAD - QK_HEAD))).reshape(Q_LORA, MLA_HEADS * QK_PAD)
    pad_g = lambda gv: jnp.pad(gv, (0, QK_PAD - QK_HEAD)).reshape(1, QK_PAD)
    q, kx, vx = _mla_prep(proj, positions.reshape(T, 1).astype(I32), invf, row(g_qa), row(g_kva),
                          wuq.astype(BF16), w_ukv.astype(BF16), pad_g(g_qn), pad_g(g_kn),
                          tiles["tm_mla"])
    o_b = _flash(q.reshape(B, S, -1), kx.reshape(B, S, -1), vx, tiles["tq"]).reshape(T, MLA_WIDTH)

    merged = _merge(o_a, o_b, w_a_up.astype(BF16), w_b_up.astype(BF16), proj,
                    tiles["tm_merge"], tiles["tn_merge"])

    x1, h2, h2p, logits_t = _wo(merged, x2, w_o.astype(BF16), row(g_ffn),
                                jnp.transpose(w_router).astype(F32), tiles["tm_wo"])
    eidx, wts, rnk, cnt = _route(logits_t, router_bias.reshape(N_EXPERTS, 1).astype(F32),
                                 tiles["tm_route"])
    tmf = tiles["tm_ffn"]
    counts = cnt[:, 0].astype(I32)
    row_tok, row_dst, tile_e, n_used = _moe_tables(eidx, rnk, counts, tmf, tiles["tm_slots"])
    ys = _ffn(tile_e, n_used, row_tok, row_dst, h2p, w_exp_gate, w_exp_up, w_exp_down, tmf)
    x3 = _combine(ys, jnp.transpose(wts), x1, h2,
                  w_sh_gate.astype(BF16), w_sh_up.astype(BF16), w_sh_down.astype(BF16),
                  tiles["tm_comb"])

    out = _ple(x3, p.reshape(T, PLE_DIM), w_ple.astype(BF16), row(g_ple_post), row(g_ple_in),
               w_ple_gate.astype(BF16), tiles["tm_ple"])
    return out.reshape(B, S, D)


TILES = dict(tm_in=1024, tn_in=512, tm_prep=256, rows_scan=128, tm_mla=1024, tq=512,
             tm_merge=1024, tn_merge=512, tm_wo=512, tm_route=512, tm_ffn=256,
             tm_slots=2048, tm_comb=128, tm_ple=512)


def kernel(x, p, positions, g_mix, w_in, mu_rwkv, w0, w2, a0, a2, g2, k_k, k_a, r_k, gn_w, gn_b, w_a_up, g_qa, g_kva, w_uq, w_ukv, g_qn, g_kn, w_b_up, w_o, g_ffn, w_router, router_bias, w_exp_gate, w_exp_up, w_exp_down, w_sh_gate, w_sh_up, w_sh_down, w_ple, g_ple_post, g_ple_in, w_ple_gate):
    args = (g_mix, w_in, mu_rwkv, w0, w2, a0, a2, g2, k_k, k_a, r_k, gn_w, gn_b, w_a_up,
            g_qa, g_kva, w_uq, w_ukv, g_qn, g_kn, w_b_up, w_o, g_ffn, w_router, router_bias,
            w_exp_gate, w_exp_up, w_exp_down, w_sh_gate, w_sh_up, w_sh_down, w_ple,
            g_ple_post, g_ple_in, w_ple_gate)
    assert all(t.shape[0] == 1 for t in args), "single-layer stack expected"
    return _layer(x, p[0], positions, *[t[0] for t in args], tiles=TILES)
```

```python
import functools

import jax
import jax.numpy as jnp
from jax import lax
from jax.experimental import pallas as pl
from jax.experimental.pallas import tpu as pltpu

F32 = jnp.float32
BF16 = jnp.bfloat16
I32 = jnp.int32
U32 = jnp.uint32

D_MODEL = 2048
PLE_DIM = 256
NORM_EPS = 1e-6
RWKV_HEADS = 16
RWKV_HEAD = 64
RWKV_WIDTH = RWKV_HEADS * RWKV_HEAD
DECAY_LORA = 64
AAA_LORA = 64
GATE_LORA = 160
GN_EPS = 64e-5
MLA_HEADS = 8
Q_LORA = 512
KV_LORA = 512
QK_NOPE = 128
QK_ROPE = 64
QK_HEAD = QK_NOPE + QK_ROPE
V_HEAD = 128
MLA_WIDTH = MLA_HEADS * V_HEAD
ROPE_THETA = 10000.0
N_EXPERTS = 64
TOP_K = 8
N_GROUPS = 8
GROUP_SIZE = N_EXPERTS // N_GROUPS
TOPK_GROUPS = 4
MOE_INTER = 512
ROUTED_SCALE = 2.5

LANES = 128
QK_PAD = 2 * LANES
PACK_ROWS = 8
PACK_W = D_MODEL // 2 // PACK_ROWS
assert PACK_W == LANES

LORA_PAD = 128
GATE_LORA_PAD = 256
RWKV_PAD = 3 * RWKV_WIDTH + 2 * LORA_PAD + GATE_LORA_PAD
MLA_PAD = 1536
COL_RWKV = 0
COL_MLA = RWKV_PAD
COL_GA = COL_MLA + MLA_PAD
COL_GB = COL_GA + D_MODEL
IN_PAD = COL_GB + D_MODEL

VMEM_LIMIT = 56 * 1024 * 1024
CHUNK = 64
WO_ROWS = 128
SCAN_LANE_BLOCKS = 8


def _cparams(sem):
    return pltpu.CompilerParams(dimension_semantics=sem, vmem_limit_bytes=VMEM_LIMIT)


def _dot(a, b):
    return jnp.dot(a, b, preferred_element_type=F32)


def _dot_nt(a, b):
    return lax.dot_general(a, b, (((1,), (1,)), ((), ())), preferred_element_type=F32)


def _dot_tn(a, b):
    return lax.dot_general(a, b, (((0,), (0,)), ((), ())), preferred_element_type=F32)


def _split_bf16(x):
    hi = x.astype(BF16)
    lo = (x - hi.astype(F32)).astype(BF16)
    return hi, lo


def _rms(x, g):
    ms = jnp.mean(x * x, axis=-1, keepdims=True)
    return x * lax.rsqrt(ms + NORM_EPS) * g


def _sigmoid(x):
    return 1.0 / (1.0 + jnp.exp(-x))


def _inproj_kernel(x_ref, g_ref, w_ref, o_ref, h_scr):
    @pl.when(pl.program_id(1) == 0)
    def _():
        h_scr[...] = _rms(x_ref[...], g_ref[...]).astype(BF16)

    o_ref[...] = _dot(h_scr[...], w_ref[...]).astype(o_ref.dtype)


def _inproj(x2, g, w, tm, tn):
    T, D = x2.shape
    N = w.shape[1]
    return pl.pallas_call(
        _inproj_kernel,
        grid=(T // tm, N // tn),
        in_specs=[
            pl.BlockSpec((tm, D), lambda i, j: (i, 0)),
            pl.BlockSpec((1, D), lambda i, j: (0, 0)),
            pl.BlockSpec((D, tn), lambda i, j: (0, j)),
        ],
        out_specs=pl.BlockSpec((tm, tn), lambda i, j: (i, j)),
        out_shape=jax.ShapeDtypeStruct((T, N), BF16),
        scratch_shapes=[pltpu.VMEM((tm, D), BF16)],
        compiler_params=_cparams(("parallel", "arbitrary")),
        name="inproj",
    )(x2, g, w)


HALO = 16


def _rwkv_prep_kernel(cur_ref, halo_ref, mu_ref, w0_ref, w2_ref, a0_ref, a2_ref, g2_ref,
                      kk_ref, ka_ref, bd_ref,
                      r_out, k_out, v_out, kk_out, a_out, lw_out, g_out, *, tm, seq):
    i = pl.program_id(0)
    cur = cur_ref[...].astype(F32)
    first = (i * tm) % seq == 0
    last_prev = halo_ref[HALO - 1:HALO, :].astype(F32)
    last_prev = jnp.where(first, 0.0, last_prev)
    row = lax.broadcasted_iota(I32, cur.shape, 0)
    prev = jnp.where(row == 0, last_prev, pltpu.roll(cur, 1, 0))
    u = cur + (prev - cur) * mu_ref[...]
    W = RWKV_WIDTH
    r = u[:, 0:W]
    k = u[:, W:2 * W]
    v = u[:, 2 * W:3 * W]
    dw = u[:, 3 * W:3 * W + LORA_PAD]
    da = u[:, 3 * W + LORA_PAD:3 * W + 2 * LORA_PAD]
    dg = u[:, 3 * W + 2 * LORA_PAD:3 * W + 2 * LORA_PAD + GATE_LORA_PAD]
    z = w0_ref[...] + _dot(jnp.tanh(dw).astype(BF16), w2_ref[...])
    w_log = -(jnp.maximum(-z, 0.0) + jnp.log(1.0 + jnp.exp(-jnp.abs(z)))) - 0.5
    lw_out[...] = -jnp.exp(w_log)
    a = _sigmoid(a0_ref[...] + _dot(da.astype(BF16), a2_ref[...]))
    g_out[...] = _dot(_sigmoid(dg).astype(BF16), g2_ref[...]).astype(g_out.dtype)
    kk = k * kk_ref[...]
    hi, lo = _split_bf16(kk * kk)
    ssq = _dot(hi, bd_ref[...]) + _dot(lo, bd_ref[...])
    kk = kk / jnp.maximum(jnp.sqrt(ssq), 1e-12)
    r_out[...] = r.astype(r_out.dtype)
    k_out[...] = (k * (1.0 + (a - 1.0) * ka_ref[...])).astype(k_out.dtype)
    v_out[...] = v.astype(v_out.dtype)
    kk_out[...] = kk.astype(kk_out.dtype)
    a_out[...] = a.astype(a_out.dtype)


def _rwkv_prep(proj, mu, w0, w2, a0, a2, g2, k_k, k_a, bd, tm, seq):
    T = proj.shape[0]
    W = RWKV_WIDTH
    full = lambda shape: pl.BlockSpec(shape, lambda i: (0, 0))
    out = lambda: pl.BlockSpec((tm, W), lambda i: (i, 0))
    return pl.pallas_call(
        functools.partial(_rwkv_prep_kernel, tm=tm, seq=seq),
        grid=(T // tm,),
        in_specs=[
            pl.BlockSpec((tm, RWKV_PAD), lambda i: (i, 0)),
            pl.BlockSpec((HALO, RWKV_PAD), lambda i: (jnp.maximum(i * (tm // HALO) - 1, 0), 0)),
            full((1, RWKV_PAD)), full((1, W)), full((LORA_PAD, W)), full((1, W)),
            full((LORA_PAD, W)), full((GATE_LORA_PAD, W)), full((1, W)), full((1, W)),
            full((W, W)),
        ],
        out_specs=[out() for _ in range(7)],
        out_shape=[jax.ShapeDtypeStruct((T, W), dt)
                   for dt in (BF16, BF16, BF16, BF16, BF16, F32, BF16)],
        compiler_params=_cparams(("parallel",)),
        name="rwkv_prep",
    )(proj, proj, mu, w0, w2, a0, a2, g2, k_k, k_a, bd)


def _rwkv_scan_kernel(r_ref, k_ref, v_ref, kk_ref, a_ref, lw_ref, g_ref,
                      gnw_ref, gnb_ref, rk_ref, o_ref, s_scr, *, rows):
    L = CHUNK
    L2 = 2 * L

    @pl.when(pl.program_id(2) == 0)
    def _():
        s_scr[...] = jnp.zeros_like(s_scr)

    lane = lax.broadcasted_iota(I32, (L, LANES), 1)
    m_a = (lane < RWKV_HEAD).astype(F32)
    m_b = 1.0 - m_a
    ri = lax.broadcasted_iota(I32, (L2, L2), 0)
    ci = lax.broadcasted_iota(I32, (L2, L2), 1)
    strict = ri > ci
    incl = ri >= ci
    eye = (ri == ci).astype(F32)
    tri = (lax.broadcasted_iota(I32, (L, L), 0) >= lax.broadcasted_iota(I32, (L, L), 1)).astype(BF16)
    hr = lax.broadcasted_iota(I32, (LANES, LANES), 0) // RWKV_HEAD
    hc = lax.broadcasted_iota(I32, (LANES, LANES), 1) // RWKV_HEAD
    head_ones = (hr == hc).astype(BF16)
    nb = SCAN_LANE_BLOCKS

    def stack(x):
        return jnp.concatenate([x * m_a, x * m_b], axis=0)

    chunks = range(rows // L)
    rows_of = lambda c: slice(c * L, (c + 1) * L)
    lanes_of = lambda h: slice(h * LANES, (h + 1) * LANES)
    each = lambda f, *lists: [f(*vals) for vals in zip(*lists)]

    def operands(c, h):
        sl, ln = rows_of(c), lanes_of(h)
        r = r_ref[0, sl, ln].astype(F32)
        k = k_ref[0, sl, ln].astype(F32)
        v = v_ref[0, sl, ln].astype(F32)
        kk = kk_ref[0, sl, ln].astype(F32)
        a = a_ref[0, sl, ln].astype(F32)
        lw = lw_ref[0, sl, ln]
        hi, lo = _split_bf16(lw)
        cum = _dot(tri, hi) + _dot(tri, lo)
        c_end = cum[L - 1:L, :]
        p_inv = jnp.exp(-cum)
        p_end = jnp.exp(c_end - cum)
        b = kk * a
        return dict(
            x_a=stack(-(kk * jnp.exp(cum - lw))).astype(BF16),
            x_r=stack(r * jnp.exp(cum)),
            x_bk=jnp.concatenate([stack(b * p_inv), stack(k * p_inv)], axis=0).astype(BF16),
            v_st=stack(v).astype(BF16),
            z_hat=jnp.concatenate([stack(b * p_end), stack(k * p_end)], axis=0).astype(BF16),
            p_row=jnp.exp(c_end),
            rkr=(r * k * rk_ref[:, ln]).astype(BF16), v=v)

    ops = [operands(c, h) for c in chunks for h in range(nb)]
    big = [_dot_nt(jnp.concatenate([o["x_a"], o["x_r"].astype(BF16)], axis=0), o["x_bk"])
           for o in ops]
    n_pow = [jnp.where(strict, m[0:L2, 0:L2], 0.0) for m in big]
    a_ak = [jnp.where(strict, m[0:L2, L2:2 * L2], 0.0).astype(BF16) for m in big]
    a_r = [jnp.concatenate([jnp.where(incl, m[L2:2 * L2, 0:L2], 0.0),
                            jnp.where(incl, m[L2:2 * L2, L2:2 * L2], 0.0)], axis=1).astype(BF16)
           for m in big]

    t_inv = [eye + n for n in n_pow]
    for _ in range(5):
        n_pow = each(lambda n: _dot(n.astype(BF16), n.astype(BF16)), n_pow)
        t_inv = each(lambda t, n: t + _dot(t.astype(BF16), n.astype(BF16)), t_inv, n_pow)

    akv = each(lambda m, o: _dot(m, o["v_st"]).astype(BF16), a_ak, ops)
    w = each(lambda t, o, x: _dot(t.astype(BF16), jnp.concatenate([o["x_a"], x], axis=1)),
             t_inv, ops, akv)
    w1 = [m[:, 0:LANES].astype(BF16) for m in w]
    w2v = each(lambda m, o: jnp.concatenate([m[:, LANES:].astype(BF16), o["v_st"]], axis=0), w, ops)
    g_mat = each(lambda o, m, x: (o["x_r"] + _dot(m[:, 0:L2], x)).astype(BF16), ops, a_r, w1)
    y0 = each(_dot, a_r, w2v)
    m_mat = each(lambda x, o: _dot_tn(x, o["z_hat"][0:L2, :]).astype(BF16), w1, ops)
    c2 = each(lambda x, o: _dot_tn(x, o["z_hat"]), w2v, ops)
    bonus_v = [_dot(o["rkr"], head_ones) * o["v"] for o in ops]

    s = [s_scr[h] for h in range(nb)]
    y = []
    for i in range(len(ops)):
        h = i % nb
        s_b = s[h].astype(BF16)
        y_st = _dot_nt(g_mat[i], s_b) + y0[i]
        s[h] = s[h] * ops[i]["p_row"] + _dot(s_b, m_mat[i]) + c2[i]
        y.append(y_st[0:L, :] + y_st[L:L2, :])
    for h in range(nb):
        s_scr[h] = s[h]

    mean = [_dot(t.astype(BF16), head_ones) * (1.0 / RWKV_HEAD) for t in y]
    yc = each(lambda t, m: t - m, y, mean)
    var = [_dot((t * t).astype(BF16), head_ones) * (1.0 / RWKV_HEAD) for t in yc]
    for i in range(len(ops)):
        sl, ln = rows_of(i // nb), lanes_of(i % nb)
        yn = yc[i] * lax.rsqrt(var[i] + GN_EPS) * gnw_ref[:, ln] + gnb_ref[:, ln]
        o_ref[0, sl, ln] = ((yn + bonus_v[i]) * g_ref[0, sl, ln].astype(F32)).astype(o_ref.dtype)


def _rwkv_scan(r, k, v, kk, a, lw, g, gn_w, gn_b, r_k, rows):
    B, S, W = r.shape
    wb = SCAN_LANE_BLOCKS * LANES
    npair = W // wb
    seq = lambda: pl.BlockSpec((1, rows, wb), lambda b, p, c: (b, c, p))
    par = lambda: pl.BlockSpec((1, wb), lambda b, p, c: (0, p))
    return pl.pallas_call(
        functools.partial(_rwkv_scan_kernel, rows=rows),
        grid=(B, npair, S // rows),
        in_specs=[seq() for _ in range(7)] + [par(), par(), par()],
        out_specs=seq(),
        out_shape=jax.ShapeDtypeStruct((B, S, W), BF16),
        scratch_shapes=[pltpu.VMEM((SCAN_LANE_BLOCKS, LANES, LANES), F32)],
        compiler_params=_cparams(("parallel", "parallel", "arbitrary")),
        name="rwkv_scan",
    )(r, k, v, kk, a, lw, g, gn_w, gn_b, r_k)


def _rope128(x, cos, sin_signed):
    lane = lax.broadcasted_iota(I32, x.shape, 1)
    half = QK_ROPE // 2
    partner = jnp.where(lane < half, pltpu.roll(x, LANES - half, 1), pltpu.roll(x, half, 1))
    return x * cos + partner * sin_signed


def _mla_prep_kernel(cq_ref, ckv_ref, kr_ref, pos_ref, invf_ref, gqa_ref, gkva_ref,
                     wuq_ref, wukv_ref, gqn_ref, gkn_ref,
                     q_out, k_out, v_out, cqn_scr, ckvn_scr, cos_scr, sin_scr):
    @pl.when(pl.program_id(1) == 0)
    def _():
        cqn_scr[...] = _rms(cq_ref[...].astype(F32), gqa_ref[...]).astype(BF16)
        ckvn_scr[...] = _rms(ckv_ref[...].astype(F32), gkva_ref[...]).astype(BF16)
        ang = pos_ref[...].astype(F32) * invf_ref[...]
        lane = lax.broadcasted_iota(I32, ang.shape, 1)
        cos_scr[...] = jnp.cos(ang)
        sin_scr[...] = jnp.where(lane < QK_ROPE // 2, -1.0, 1.0) * jnp.sin(ang)

    cos = cos_scr[...]
    sin = sin_scr[...]
    inv_n = 1.0 / QK_HEAD
    q = _dot(cqn_scr[...], wuq_ref[...])
    rs = lax.rsqrt(jnp.sum(q * q, axis=-1, keepdims=True) * inv_n + NORM_EPS)
    q = q * rs * gqn_ref[...] * (QK_HEAD ** -0.5 * LOG2E)
    q_out[...] = jnp.concatenate(
        [q[:, 0:LANES], _rope128(q[:, LANES:], cos, sin)], axis=1).astype(q_out.dtype)

    kv = _dot(ckvn_scr[...], wukv_ref[...])
    k_nope = kv[:, 0:QK_NOPE]
    kr = kr_ref[...].astype(F32)
    ssq = (jnp.sum(k_nope * k_nope, axis=-1, keepdims=True)
           + jnp.sum(kr * kr, axis=-1, keepdims=True))
    rs = lax.rsqrt(ssq * inv_n + NORM_EPS)
    gkn = gkn_ref[...]
    k_out[...] = jnp.concatenate(
        [k_nope * rs * gkn[:, 0:LANES], _rope128(kr * rs * gkn[:, LANES:], cos, sin)],
        axis=1).astype(k_out.dtype)
    v_out[...] = jnp.transpose(kv[:, QK_NOPE:].astype(v_out.dtype))


def _mla_prep(proj, pos, invf, g_qa, g_kva, wuq, wukv, gqn, gkn, tm):
    T = proj.shape[0]
    H = MLA_HEADS
    c0 = COL_MLA // Q_LORA
    ckr = (COL_MLA + Q_LORA + KV_LORA) // LANES
    full = lambda shape: pl.BlockSpec(shape, lambda i, h: (0, 0))
    return pl.pallas_call(
        _mla_prep_kernel,
        grid=(T // tm, H),
        in_specs=[
            pl.BlockSpec((tm, Q_LORA), lambda i, h: (i, c0)),
            pl.BlockSpec((tm, KV_LORA), lambda i, h: (i, c0 + 1)),
            pl.BlockSpec((tm, LANES), lambda i, h: (i, ckr)),
            pl.BlockSpec((tm, 1), lambda i, h: (i, 0)),
            full((1, LANES)), full((1, Q_LORA)), full((1, KV_LORA)),
            pl.BlockSpec((Q_LORA, QK_PAD), lambda i, h: (0, h)),
            pl.BlockSpec((KV_LORA, QK_NOPE + V_HEAD), lambda i, h: (0, h)),
            full((1, QK_PAD)), full((1, QK_PAD)),
        ],
        out_specs=[
            pl.BlockSpec((tm, QK_PAD), lambda i, h: (i, h)),
            pl.BlockSpec((tm, QK_PAD), lambda i, h: (i, h)),
            pl.BlockSpec((V_HEAD, tm), lambda i, h: (h, i)),
        ],
        out_shape=[jax.ShapeDtypeStruct((T, H * QK_PAD), BF16),
                   jax.ShapeDtypeStruct((T, H * QK_PAD), BF16),
                   jax.ShapeDtypeStruct((H * V_HEAD, T), BF16)],
        scratch_shapes=[pltpu.VMEM((tm, Q_LORA), BF16), pltpu.VMEM((tm, KV_LORA), BF16),
                        pltpu.VMEM((tm, LANES), F32), pltpu.VMEM((tm, LANES), F32)],
        compiler_params=_cparams(("parallel", "arbitrary")),
        name="mla_prep",
    )(proj, proj, proj, pos, invf, g_qa, g_kva, wuq, wukv, gqn, gkn)


MASK_VALUE = -1e30
LOG2E = 1.4426950408889634


def _flash_kernel(q_ref, k_ref, vt_ref, o_ref, s_scr, p_scr, alpha_scr, m_scr, l_scr, acc_scr,
                  *, tq):
    i = pl.program_id(2)
    q = q_ref[0]
    m_scr[...] = jnp.full_like(m_scr, MASK_VALUE)
    l_scr[...] = jnp.zeros_like(l_scr)
    acc_scr[...] = jnp.zeros_like(acc_scr)

    q_half = tq // 2

    def scores(j, slot, half):
        ks = pl.ds(pl.multiple_of(j * tq, tq), tq)
        cols = slice(half * q_half, (half + 1) * q_half)
        s_scr[slot, :, cols] = _dot_nt(k_ref[0, ks, :], q[cols, :])

    def softmax(slot, masked, strips):
        for c in strips:
            cols = slice(c * LANES, (c + 1) * LANES)
            s = s_scr[slot, :, cols]
            if masked:
                kpos = lax.broadcasted_iota(I32, s.shape, 0)
                qpos = lax.broadcasted_iota(I32, s.shape, 1) + c * LANES
                s = jnp.where(kpos <= qpos, s, MASK_VALUE)
            m_old = m_scr[:, cols]
            m_new = jnp.maximum(m_old, jnp.max(s, axis=0, keepdims=True))
            p = jnp.exp2(s - m_new)
            alpha = jnp.exp2(m_old - m_new)
            l_scr[:, cols] = alpha * l_scr[:, cols] + jnp.sum(p, axis=0, keepdims=True)
            m_scr[:, cols] = m_new
            p_scr[slot, :, cols] = p.astype(BF16)
            alpha_scr[slot, :, cols] = alpha

    def weighted_values(j, slot):
        vt = vt_ref[:, pl.ds(pl.multiple_of(j * tq, tq), tq)]
        acc_scr[...] = alpha_scr[slot] * acc_scr[...] + _dot(vt, p_scr[slot])

    scores(0, 0, 0)
    scores(0, 0, 1)
    p_scr[1] = jnp.zeros((tq, tq), BF16)
    alpha_scr[1] = jnp.ones((1, tq), F32)
    n_strips = tq // LANES
    first, second = range(0, n_strips // 2), range(n_strips // 2, n_strips)

    def step(j, slot):
        weighted_values(jnp.maximum(j - 1, 0), 1 - slot)
        scores(j + 1, 1 - slot, 0)
        softmax(slot, False, first)
        scores(j + 1, 1 - slot, 1)
        softmax(slot, False, second)

    def pair(jp, carry):
        step(2 * jp, 0)
        step(2 * jp + 1, 1)
        return carry

    def finish(slot):
        weighted_values(jnp.maximum(i - 1, 0), 1 - slot)
        softmax(slot, True, range(n_strips))
        weighted_values(i, slot)
        o_ref[0] = jnp.transpose(acc_scr[...] / l_scr[...]).astype(o_ref.dtype)

    lax.fori_loop(0, i // 2, pair, 0)

    @pl.when(i % 2 == 0)
    def _():
        finish(0)

    @pl.when(i % 2 == 1)
    def _():
        step(i - 1, 0)
        finish(1)


def _flash(q, k, vt, tq):
    B, S, _ = q.shape
    H = MLA_HEADS
    return pl.pallas_call(
        functools.partial(_flash_kernel, tq=tq),
        grid=(B, H, S // tq),
        in_specs=[
            pl.BlockSpec((1, tq, QK_PAD), lambda b, h, i: (b, i, h)),
            pl.BlockSpec((1, S, QK_PAD), lambda b, h, i: (b, 0, h)),
            pl.BlockSpec((V_HEAD, S), lambda b, h, i: (h, b)),
        ],
        out_specs=pl.BlockSpec((1, tq, V_HEAD), lambda b, h, i: (b, i, h)),
        out_shape=jax.ShapeDtypeStruct((B, S, H * V_HEAD), BF16),
        scratch_shapes=[pltpu.VMEM((2, tq, tq), F32), pltpu.VMEM((2, tq, tq), BF16),
                        pltpu.VMEM((2, 1, tq), F32),
                        pltpu.VMEM((1, tq), F32), pltpu.VMEM((1, tq), F32),
                        pltpu.VMEM((V_HEAD, tq), F32)],
        compiler_params=_cparams(("parallel", "parallel", "arbitrary")),
        name="flash_attn",
    )(q, k, vt)


def _merge_kernel(oa_ref, ob_ref, wa_ref, wb_ref, ga_ref, gb_ref, o_ref):
    ya = _dot(oa_ref[...], wa_ref[...])
    yb = _dot(ob_ref[...], wb_ref[...])
    ga = _sigmoid(ga_ref[...].astype(F32))
    gb = _sigmoid(gb_ref[...].astype(F32))
    o_ref[...] = (ga * ya + gb * yb).astype(o_ref.dtype)


def _merge(oa, ob, wa, wb, proj, tm, tn):
    T, K = oa.shape
    N = wa.shape[1]
    ca = COL_GA // tn
    cb = COL_GB // tn
    return pl.pallas_call(
        _merge_kernel,
        grid=(T // tm, N // tn),
        in_specs=[
            pl.BlockSpec((tm, K), lambda i, j: (i, 0)),
            pl.BlockSpec((tm, K), lambda i, j: (i, 0)),
            pl.BlockSpec((K, tn), lambda i, j: (0, j)),
            pl.BlockSpec((K, tn), lambda i, j: (0, j)),
            pl.BlockSpec((tm, tn), lambda i, j: (i, ca + j)),
            pl.BlockSpec((tm, tn), lambda i, j: (i, cb + j)),
        ],
        out_specs=pl.BlockSpec((tm, tn), lambda i, j: (i, j)),
        out_shape=jax.ShapeDtypeStruct((T, N), BF16),
        compiler_params=_cparams(("parallel", "parallel")),
        name="merge",
    )(oa, ob, wa, wb, proj, proj)


def _pack_rows(x, out_ref, n, row0=0):
    for c in range(PACK_ROWS):
        lo = x[:, c * 2 * LANES:c * 2 * LANES + LANES]
        hi = x[:, c * 2 * LANES + LANES:(c + 1) * 2 * LANES]
        lo_b = pltpu.bitcast(lo.astype(BF16).astype(F32), U32)
        hi_b = pltpu.bitcast(hi.astype(BF16).astype(F32), U32)
        out_ref[pl.ds(row0 * PACK_ROWS + c, n, stride=PACK_ROWS), :] = hi_b | (lo_b >> 16)


def _unpack_rows(ref, c, tm, lead=None):
    idx = (pl.ds(c, tm, stride=PACK_ROWS), slice(None))
    if lead is not None:
        idx = (lead,) + idx
    u = ref[idx]
    lo = pltpu.bitcast(u << 16, F32)
    hi = pltpu.bitcast(u & jnp.uint32(0xFFFF0000), F32)
    return lo, hi


def _wo_kernel(m_ref, x_ref, wo_ref, g_ref, wr_ref, x1_ref, h2_ref, h2p_ref, lg_ref, *, tm):
    hw = WO_ROWS
    groups = [slice(h * hw, (h + 1) * hw) for h in range(tm // hw)]
    wo = wo_ref[...]
    x1 = [x_ref[r, :] + _dot(m_ref[r, :], wo) for r in groups]
    h2 = [_rms(v, g_ref[...]) for v in x1]
    w_hi, w_lo = _split_bf16(wr_ref[...])
    for h, r in enumerate(groups):
        x1_ref[r, :] = x1[h]
        h2_ref[r, :] = h2[h].astype(BF16)
        _pack_rows(h2[h], h2p_ref, hw, row0=h * hw)
        h_hi, h_lo = _split_bf16(h2[h])
        lg_ref[:, r] = _dot_nt(w_hi, h_hi) + _dot_nt(w_hi, h_lo) + _dot_nt(w_lo, h_hi)


def _wo(merged, x2, wo, g, wr_t, tm):
    T, D = x2.shape
    E = wr_t.shape[0]
    return pl.pallas_call(
        functools.partial(_wo_kernel, tm=tm),
        grid=(T // tm,),
        in_specs=[
            pl.BlockSpec((tm, D), lambda i: (i, 0)),
            pl.BlockSpec((tm, D), lambda i: (i, 0)),
            pl.BlockSpec((D, D), lambda i: (0, 0), pipeline_mode=pl.Buffered(1)),
            pl.BlockSpec((1, D), lambda i: (0, 0)),
            pl.BlockSpec((E, D), lambda i: (0, 0)),
        ],
        out_specs=[
            pl.BlockSpec((tm, D), lambda i: (i, 0)),
            pl.BlockSpec((tm, D), lambda i: (i, 0)),
            pl.BlockSpec((tm * PACK_ROWS, PACK_W), lambda i: (i, 0)),
            pl.BlockSpec((E, tm), lambda i: (0, i)),
        ],
        out_shape=[jax.ShapeDtypeStruct((T, D), F32),
                   jax.ShapeDtypeStruct((T, D), BF16),
                   jax.ShapeDtypeStruct((T * PACK_ROWS, PACK_W), U32),
                   jax.ShapeDtypeStruct((E, T), F32)],
        compiler_params=_cparams(("parallel",)),
        name="wo_norm_router",
    )(merged, x2, wo, g, wr_t)


def _route_kernel(lg_ref, bias_ref, eidx_ref, wts_ref, rnk_ref, cnt_ref, carry_scr, *, tm):
    E = N_EXPERTS
    NEG = -jnp.inf

    @pl.when(pl.program_id(0) == 0)
    def _():
        carry_scr[...] = jnp.zeros_like(carry_scr)

    scores = _sigmoid(lg_ref[...])
    choice = scores + bias_ref[...]
    c3 = choice.reshape(N_GROUPS, GROUP_SIZE, tm)
    sub = lax.broadcasted_iota(I32, c3.shape, 1)
    m1 = jnp.max(c3, axis=1, keepdims=True)
    i1 = jnp.min(jnp.where(c3 == m1, sub, GROUP_SIZE), axis=1, keepdims=True)
    m2 = jnp.max(jnp.where(sub == i1, NEG, c3), axis=1, keepdims=True)
    gs = (m1 + m2).reshape(N_GROUPS, tm)
    gi = lax.broadcasted_iota(I32, gs.shape, 0)
    gsel = jnp.zeros(gs.shape, F32)
    for _ in range(TOPK_GROUPS):
        mx = jnp.max(gs, axis=0, keepdims=True)
        ix = jnp.min(jnp.where(gs == mx, gi, N_GROUPS), axis=0, keepdims=True)
        hit = gi == ix
        gsel = jnp.where(hit, 1.0, gsel)
        gs = jnp.where(hit, NEG, gs)
    emask = jnp.broadcast_to(gsel.reshape(N_GROUPS, 1, tm), (N_GROUPS, GROUP_SIZE, tm)).reshape(E, tm)
    x = jnp.where(emask > 0.5, choice, NEG)
    ei = lax.broadcasted_iota(I32, x.shape, 0)
    sel = jnp.zeros(x.shape, F32)
    idx_rows, w_rows = [], []
    for _ in range(TOP_K):
        mx = jnp.max(x, axis=0, keepdims=True)
        ix = jnp.min(jnp.where(x == mx, ei, E), axis=0, keepdims=True)
        hit = ei == ix
        w_rows.append(jnp.sum(jnp.where(hit, scores, 0.0), axis=0, keepdims=True))
        idx_rows.append(ix)
        sel = jnp.where(hit, 1.0, sel)
        x = jnp.where(hit, NEG, x)
    w = jnp.concatenate(w_rows, axis=0)
    w = w / (jnp.sum(w, axis=0, keepdims=True) + 1e-20) * ROUTED_SCALE
    eidx_ref[...] = jnp.concatenate(idx_rows, axis=0)
    wts_ref[...] = w

    upper = (lax.broadcasted_iota(I32, (tm, tm), 0) < lax.broadcasted_iota(I32, (tm, tm), 1)).astype(BF16)
    base = carry_scr[...][:, 0:1]
    excl = _dot(sel.astype(BF16), upper) + base
    rnk_ref[...] = jnp.concatenate(
        [jnp.sum(jnp.where(ei == ix, excl, 0.0), axis=0, keepdims=True) for ix in idx_rows],
        axis=0).astype(I32)
    carry_scr[...] = carry_scr[...] + jnp.sum(sel, axis=1, keepdims=True)
    cnt_ref[...] = carry_scr[...]


def _route(logits_t, bias, tm):
    E, T = logits_t.shape
    K = TOP_K
    tok = lambda: pl.BlockSpec((K, tm), lambda i: (0, i))
    return pl.pallas_call(
        functools.partial(_route_kernel, tm=tm),
        grid=(T // tm,),
        in_specs=[pl.BlockSpec((E, tm), lambda i: (0, i)),
                  pl.BlockSpec((E, 1), lambda i: (0, 0))],
        out_specs=[tok(), tok(), tok(), pl.BlockSpec((E, LANES), lambda i: (0, 0))],
        out_shape=[jax.ShapeDtypeStruct((K, T), I32), jax.ShapeDtypeStruct((K, T), F32),
                   jax.ShapeDtypeStruct((K, T), I32), jax.ShapeDtypeStruct((E, LANES), F32)],
        scratch_shapes=[pltpu.VMEM((E, LANES), F32)],
        compiler_params=_cparams(("arbitrary",)),
        name="route",
    )(logits_t, bias)


def _slots_kernel(pst_ref, eidx_ref, rnk_ref, dest_ref):
    eidx = eidx_ref[...]
    dest = rnk_ref[...]
    for e in range(N_EXPERTS):
        dest = dest + jnp.where(eidx == e, pst_ref[e], 0)
    dest_ref[...] = dest


def _slots(eidx, rnk, pstart, tm):
    K, T = eidx.shape
    tok = lambda: pl.BlockSpec((K, tm), lambda i: (0, i))
    return pl.pallas_call(
        _slots_kernel,
        grid=(T // tm,),
        in_specs=[pl.BlockSpec(memory_space=pltpu.SMEM), tok(), tok()],
        out_specs=tok(),
        out_shape=jax.ShapeDtypeStruct((K, T), I32),
        compiler_params=_cparams(("parallel",)),
        name="slots",
    )(pstart, eidx, rnk)


def _ffn_kernel(te_ref, nu_ref, tok_ref, dst_ref, h_hbm, wg_ref, wu_ref, wd_ref, ys_hbm,
                gbuf, obuf, wg_b, wu_b, wd_b, gsem, osem, *, tm):
    s = pl.program_id(0)
    nu = nu_ref[0]
    tile_rows = tm * PACK_ROWS

    def gather(slot):
        for r in range(tm):
            pltpu.make_async_copy(
                h_hbm.at[pl.ds(pl.multiple_of(tok_ref[0, 0, r] * PACK_ROWS, PACK_ROWS), PACK_ROWS), :],
                gbuf.at[slot, pl.ds(r * PACK_ROWS, PACK_ROWS), :], gsem.at[slot]).start()

    def scatter(slot):
        for r in range(tm):
            pltpu.make_async_copy(
                obuf.at[slot, pl.ds(r * PACK_ROWS, PACK_ROWS), :],
                ys_hbm.at[pl.ds(pl.multiple_of(dst_ref[0, 0, r] * PACK_ROWS, PACK_ROWS), PACK_ROWS), :],
                osem.at[slot]).start()

    def wait_gather(slot):
        pltpu.make_async_copy(h_hbm.at[pl.ds(0, tile_rows), :], gbuf.at[slot], gsem.at[slot]).wait()

    def wait_scatter(slot):
        pltpu.make_async_copy(obuf.at[slot], ys_hbm.at[pl.ds(0, tile_rows), :], osem.at[slot]).wait()

    def compute(slot, reuses_obuf):
        parts = []
        for c in range(PACK_ROWS):
            lo, hi = _unpack_rows(gbuf.at[slot], c, tm)
            parts += [lo.astype(BF16), hi.astype(BF16)]
        a = jnp.concatenate(parts, axis=1)
        hg = _dot(a, wg_b[...])
        hu = _dot(a, wu_b[...])
        hid = (hg * _sigmoid(hg) * hu).astype(BF16)
        y = _dot(hid, wd_b[...])
        if reuses_obuf:
            wait_scatter(slot)
        _pack_rows(y, obuf.at[slot], tm)

    cur = jnp.maximum(s - 1, 0)
    @pl.when((s >= 1) & (s <= nu) & ((s == 1) | (te_ref[cur] != te_ref[jnp.maximum(s - 2, 0)])))
    def _():
        wg_b[...] = wg_ref[0].astype(BF16)
        wu_b[...] = wu_ref[0].astype(BF16)
        wd_b[...] = wd_ref[0].astype(BF16)

    @pl.when(s == 0)
    def _():
        gather(0)

    @pl.when(s == 1)
    def _():
        wait_gather(0)
        gather(1)
        compute(0, False)

    @pl.when((s == 2) & (s <= nu))
    def _():
        wait_gather(1)
        gather(0)
        scatter(0)
        compute(1, False)

    for slot in range(2):
        parity = s % 2 == slot

        @pl.when(parity & (s >= 3) & (s <= nu))
        def _():
            wait_gather(1 - slot)
            gather(slot)
            scatter(slot)
            compute(1 - slot, True)

        @pl.when(parity & (s >= 3) & (s == nu + 1))
        def _():
            wait_scatter(1 - slot)

        @pl.when(parity & (s >= 2) & (s == nu + 1))
        def _():
            wait_gather(1 - slot)
            scatter(slot)
            wait_scatter(slot)


def _ffn(tile_e, n_used, row_tok, row_dst, h2p, wg, wu, wd, tm):
    nt = row_tok.shape[0]
    D, I = wg.shape[1], wg.shape[2]
    cur = lambda s: jnp.maximum(s - 1, 0)
    wspec = lambda shape: pl.BlockSpec(shape, lambda s, te, nu: (te[cur(s)], 0, 0))
    grid_spec = pltpu.PrefetchScalarGridSpec(
        num_scalar_prefetch=2,
        grid=(nt + 1,),
        in_specs=[
            pl.BlockSpec((1, 1, tm), lambda s, te, nu: (jnp.minimum(s, nt - 1), 0, 0),
                         memory_space=pltpu.SMEM),
            pl.BlockSpec((1, 1, tm), lambda s, te, nu: (jnp.maximum(s - 2, 0), 0, 0),
                         memory_space=pltpu.SMEM),
            pl.BlockSpec(memory_space=pl.ANY),
            wspec((1, D, I)), wspec((1, D, I)), wspec((1, I, D)),
        ],
        out_specs=pl.BlockSpec(memory_space=pl.ANY),
        scratch_shapes=[pltpu.VMEM((2, tm * PACK_ROWS, PACK_W), U32),
                        pltpu.VMEM((2, tm * PACK_ROWS, PACK_W), U32),
                        pltpu.VMEM((D, I), BF16), pltpu.VMEM((D, I), BF16), pltpu.VMEM((I, D), BF16),
                        pltpu.SemaphoreType.DMA((2,)), pltpu.SemaphoreType.DMA((2,))],
    )
    return pl.pallas_call(
        functools.partial(_ffn_kernel, tm=tm),
        grid_spec=grid_spec,
        out_shape=jax.ShapeDtypeStruct((nt * tm * PACK_ROWS, PACK_W), U32),
        compiler_params=_cparams(("arbitrary",)),
        name="expert_ffn",
    )(tile_e, n_used, row_tok, row_dst, h2p, wg, wu, wd)


def _combine_kernel(*refs, tm):
    ys_refs = refs[:TOP_K]
    wt_ref, x1_ref, h2_ref, wsg_ref, wsu_ref, wsd_ref, o_ref = refs[TOP_K:]
    h2 = h2_ref[...]
    hg = _dot(h2, wsg_ref[...])
    hu = _dot(h2, wsu_ref[...])
    base = x1_ref[...] + _dot((hg * _sigmoid(hg) * hu).astype(BF16), wsd_ref[...])
    wt = wt_ref[...]
    for c in range(PACK_ROWS):
        acc_lo = base[:, c * 2 * LANES:c * 2 * LANES + LANES]
        acc_hi = base[:, c * 2 * LANES + LANES:(c + 1) * 2 * LANES]
        for k in range(TOP_K):
            lo, hi = _unpack_rows(ys_refs[k], c, tm)
            wk = wt[:, k:k + 1]
            acc_lo = acc_lo + wk * lo
            acc_hi = acc_hi + wk * hi
        o_ref[:, c * 2 * LANES:c * 2 * LANES + LANES] = acc_lo
        o_ref[:, c * 2 * LANES + LANES:(c + 1) * 2 * LANES] = acc_hi


def _combine(ys, wts_t, x1, h2, wsg, wsu, wsd, tm):
    T, D = x1.shape
    K = wts_t.shape[1]
    I = wsg.shape[1]
    full = lambda shape: pl.BlockSpec(shape, lambda i: (0, 0))
    return pl.pallas_call(
        functools.partial(_combine_kernel, tm=tm),
        grid=(T // tm,),
        in_specs=[pl.BlockSpec((tm * PACK_ROWS, PACK_W), lambda i, k=k: (k * (T // tm) + i, 0))
                  for k in range(K)] + [
                  pl.BlockSpec((tm, K), lambda i: (i, 0)),
                  pl.BlockSpec((tm, D), lambda i: (i, 0)),
                  pl.BlockSpec((tm, D), lambda i: (i, 0)),
                  full((D, I)), full((D, I)), full((I, D))],
        out_specs=pl.BlockSpec((tm, D), lambda i: (i, 0)),
        out_shape=jax.ShapeDtypeStruct((T, D), F32),
        compiler_params=_cparams(("parallel",)),
        name="combine",
    )(*([ys] * K), wts_t, x1, h2, wsg, wsu, wsd)


def _ple_kernel(x_ref, p_ref, wp_ref, gpost_ref, gin_ref, wg_ref, o_ref, *, tm):
    hw = WO_ROWS
    groups = [slice(h * hw, (h + 1) * hw) for h in range(tm // hw)]
    wg = wg_ref[...]
    gates, ples = [], []
    for r in groups:
        hn = _rms(x_ref[r, :], gin_ref[...]).astype(BF16)
        gates.append(_dot(hn, wg))
        ples.append(_rms(_dot(p_ref[r, :].astype(BF16), wp_ref[...]), gpost_ref[...]))
    for r, gate, ple in zip(groups, gates, ples):
        o_ref[r, :] = x_ref[r, :] + _sigmoid(gate) * ple


def _ple(x2, p2, wp, gpost, gin, wg, tm):
    T, D = x2.shape
    Pd = p2.shape[1]
    full = lambda shape: pl.BlockSpec(shape, lambda i: (0, 0))
    return pl.pallas_call(
        functools.partial(_ple_kernel, tm=tm),
        grid=(T // tm,),
        in_specs=[
            pl.BlockSpec((tm, D), lambda i: (i, 0)),
            pl.BlockSpec((tm, Pd), lambda i: (i, 0)),
            full((Pd, D)), full((1, D)), full((1, D)),
            pl.BlockSpec((D, D), lambda i: (0, 0), pipeline_mode=pl.Buffered(1)),
        ],
        out_specs=pl.BlockSpec((tm, D), lambda i: (i, 0)),
        out_shape=jax.ShapeDtypeStruct((T, D), F32),
        compiler_params=_cparams(("parallel",)),
        name="ple",
    )(x2, p2, wp, gpost, gin, wg)


def _pad_cols(w, n):
    return jnp.pad(w, ((0, 0), (0, n - w.shape[1])))


def _pad_rows(w, n):
    return jnp.pad(w, ((0, n - w.shape[0]), (0, 0)))


def _layout_w_in(w):
    W = RWKV_WIDTH
    c = 3 * W
    segs = [w[:, 0:c],
            _pad_cols(w[:, c:c + DECAY_LORA], LORA_PAD),
            _pad_cols(w[:, c + DECAY_LORA:c + DECAY_LORA + AAA_LORA], LORA_PAD),
            _pad_cols(w[:, c + DECAY_LORA + AAA_LORA:c + DECAY_LORA + AAA_LORA + GATE_LORA], GATE_LORA_PAD)]
    c += DECAY_LORA + AAA_LORA + GATE_LORA
    mla_cols = Q_LORA + KV_LORA + QK_ROPE
    segs.append(_pad_cols(w[:, c:c + mla_cols], MLA_PAD))
    c += mla_cols
    segs.append(w[:, c:])
    out = jnp.concatenate(segs, axis=1).astype(BF16)
    assert out.shape[1] == IN_PAD
    return out


def _layout_mu(mu):
    W = RWKV_WIDTH
    c = 3 * W
    segs = [mu[0:c],
            jnp.pad(mu[c:c + DECAY_LORA], (0, LORA_PAD - DECAY_LORA)),
            jnp.pad(mu[c + DECAY_LORA:c + DECAY_LORA + AAA_LORA], (0, LORA_PAD - AAA_LORA)),
            jnp.pad(mu[c + DECAY_LORA + AAA_LORA:], (0, GATE_LORA_PAD - GATE_LORA))]
    return jnp.concatenate(segs).reshape(1, RWKV_PAD)


def _moe_tables(eidx, rnk, counts, tmf, tm_slots):
    K, T = eidx.shape
    E = N_EXPERTS
    A = K * T
    n_tiles = A // tmf + E
    P = n_tiles * tmf
    pcounts = (counts + tmf - 1) // tmf * tmf
    pends = jnp.cumsum(pcounts)
    pstart = (pends - pcounts).astype(I32)
    tile_start = jnp.arange(n_tiles, dtype=I32) * tmf
    tile_e = jnp.minimum(jnp.sum((pends[None, :] <= tile_start[:, None]).astype(I32), axis=1), E - 1)
    n_used = (pends[E - 1:] // tmf).astype(I32)
    dest = _slots(eidx, rnk, pstart, tm_slots)
    j = jnp.arange(tmf, dtype=I32)[None, :]
    tail = j < (pcounts - counts)[:, None]
    spare = jnp.cumsum(jnp.logical_not(tail).reshape(-1).astype(I32)) - 1
    pad_row = jnp.where(tail.reshape(-1), ((pstart + counts)[:, None] + j).reshape(-1),
                        pends[E - 1] + spare)
    q = jnp.arange(P - A, dtype=I32)
    slot_id = jnp.arange(A, dtype=I32)
    rows = jnp.concatenate([dest.reshape(-1), pad_row.astype(I32)])
    slots = jnp.concatenate([slot_id, A + q])
    row_dst = lax.sort((rows, slots), num_keys=1)[1]
    row_tok = jnp.where(row_dst < A, row_dst % T, 0)
    return (row_tok.reshape(n_tiles, 1, tmf), row_dst.reshape(n_tiles, 1, tmf), tile_e, n_used)


def _layer(x, p, positions, g_mix, w_in, mu_rwkv, w0, w2, a0, a2, g2, k_k, k_a, r_k,
           gn_w, gn_b, w_a_up, g_qa, g_kva, w_uq, w_ukv, g_qn, g_kn, w_b_up, w_o,
           g_ffn, w_router, router_bias, w_exp_gate, w_exp_up, w_exp_down,
           w_sh_gate, w_sh_up, w_sh_down, w_ple, g_ple_post, g_ple_in, w_ple_gate,
           *, tiles):
    B, S, D = x.shape
    T = B * S
    W = RWKV_WIDTH
    x2 = x.reshape(T, D)
    row = lambda v: v.reshape(1, -1).astype(F32)

    proj = _inproj(x2, row(g_mix), _layout_w_in(w_in), tiles["tm_in"], tiles["tn_in"])

    head_blk = (jnp.arange(W)[:, None] // RWKV_HEAD == jnp.arange(W)[None, :] // RWKV_HEAD).astype(BF16)
    r, k, v, kk, a, lw, g = _rwkv_prep(
        proj, _layout_mu(mu_rwkv), row(w0), _pad_rows(w2, LORA_PAD).astype(BF16), row(a0),
        _pad_rows(a2, LORA_PAD).astype(BF16), _pad_rows(g2, GATE_LORA_PAD).astype(BF16),
        row(k_k), row(k_a), head_blk, tiles["tm_prep"], S)
    b3 = lambda t: t.reshape(B, S, W)
    o_a = _rwkv_scan(b3(r), b3(k), b3(v), b3(kk), b3(a), b3(lw), b3(g),
                     row(gn_w), row(gn_b), row(r_k), tiles["rows_scan"]).reshape(T, W)

    half = QK_ROPE // 2
    inv_freq = ROPE_THETA ** (-jnp.arange(half, dtype=F32) / half)
    invf = jnp.concatenate([inv_freq, inv_freq, jnp.zeros((LANES - QK_ROPE,), F32)]).reshape(1, LANES)
    wuq = jnp.pad(w_uq.reshape(Q_LORA, MLA_HEADS, QK_HEAD),
                  ((0, 0), (0, 0), (0, QK_PAD - QK_HEAD))).reshape(Q_LORA, MLA_HEADS * QK_PAD)
    pad_g = lambda gv: jnp.pad(gv, (0, QK_PAD - QK_HEAD)).reshape(1, QK_PAD)
    q, kx, vx = _mla_prep(proj, positions.reshape(T, 1).astype(I32), invf, row(g_qa), row(g_kva),
                          wuq.astype(BF16), w_ukv.astype(BF16), pad_g(g_qn), pad_g(g_kn),
                          tiles["tm_mla"])
    o_b = _flash(q.reshape(B, S, -1), kx.reshape(B, S, -1), vx, tiles["tq"]).reshape(T, MLA_WIDTH)

    merged = _merge(o_a, o_b, w_a_up.astype(BF16), w_b_up.astype(BF16), proj,
                    tiles["tm_merge"], tiles["tn_merge"])

    x1, h2, h2p, logits_t = _wo(merged, x2, w_o.astype(BF16), row(g_ffn),
                                jnp.transpose(w_router).astype(F32), tiles["tm_wo"])
    eidx, wts, rnk, cnt = _route(logits_t, router_bias.reshape(N_EXPERTS, 1).astype(F32),
                                 tiles["tm_route"])
    tmf = tiles["tm_ffn"]
    counts = cnt[:, 0].astype(I32)
    row_tok, row_dst, tile_e, n_used = _moe_tables(eidx, rnk, counts, tmf, tiles["tm_slots"])
    ys = _ffn(tile_e, n_used, row_tok, row_dst, h2p, w_exp_gate, w_exp_up, w_exp_down, tmf)
    x3 = _combine(ys, jnp.transpose(wts), x1, h2,
                  w_sh_gate.astype(BF16), w_sh_up.astype(BF16), w_sh_down.astype(BF16),
                  tiles["tm_comb"])

    out = _ple(x3, p.reshape(T, PLE_DIM), w_ple.astype(BF16), row(g_ple_post), row(g_ple_in),
               w_ple_gate.astype(BF16), tiles["tm_ple"])
    return out.reshape(B, S, D)


TILES = dict(tm_in=1024, tn_in=512, tm_prep=256, rows_scan=128, tm_mla=1024, tq=512,
             tm_merge=1024, tn_merge=512, tm_wo=512, tm_route=512, tm_ffn=256,
             tm_slots=2048, tm_comb=128, tm_ple=512)


def kernel(x, p, positions, g_mix, w_in, mu_rwkv, w0, w2, a0, a2, g2, k_k, k_a, r_k, gn_w, gn_b, w_a_up, g_qa, g_kva, w_uq, w_ukv, g_qn, g_kn, w_b_up, w_o, g_ffn, w_router, router_bias, w_exp_gate, w_exp_up, w_exp_down, w_sh_gate, w_sh_up, w_sh_down, w_ple, g_ple_post, g_ple_in, w_ple_gate):
    args = (g_mix, w_in, mu_rwkv, w0, w2, a0, a2, g2, k_k, k_a, r_k, gn_w, gn_b, w_a_up,
            g_qa, g_kva, w_uq, w_ukv, g_qn, g_kn, w_b_up, w_o, g_ffn, w_router, router_bias,
            w_exp_gate, w_exp_up, w_exp_down, w_sh_gate, w_sh_up, w_sh_down, w_ple,
            g_ple_post, g_ple_in, w_ple_gate)
    assert all(t.shape[0] == 1 for t in args), "single-layer stack expected"
    return _layer(x, p[0], positions, *[t[0] for t in args], tiles=TILES)
```

```python
import functools

import jax
import jax.numpy as jnp
from jax import lax
from jax.experimental import pallas as pl
from jax.experimental.pallas import tpu as pltpu

F32 = jnp.float32
BF16 = jnp.bfloat16
I32 = jnp.int32
U32 = jnp.uint32

D_MODEL = 2048
PLE_DIM = 256
NORM_EPS = 1e-6
RWKV_HEADS = 16
RWKV_HEAD = 64
RWKV_WIDTH = RWKV_HEADS * RWKV_HEAD
DECAY_LORA = 64
AAA_LORA = 64
GATE_LORA = 160
GN_EPS = 64e-5
MLA_HEADS = 8
Q_LORA = 512
KV_LORA = 512
QK_NOPE = 128
QK_ROPE = 64
QK_HEAD = QK_NOPE + QK_ROPE
V_HEAD = 128
MLA_WIDTH = MLA_HEADS * V_HEAD
ROPE_THETA = 10000.0
N_EXPERTS = 64
TOP_K = 8
N_GROUPS = 8
GROUP_SIZE = N_EXPERTS // N_GROUPS
TOPK_GROUPS = 4
MOE_INTER = 512
ROUTED_SCALE = 2.5

LANES = 128
QK_PAD = 2 * LANES
PACK_ROWS = 8
PACK_W = D_MODEL // 2 // PACK_ROWS
assert PACK_W == LANES

LORA_PAD = 128
GATE_LORA_PAD = 256
RWKV_PAD = 3 * RWKV_WIDTH + 2 * LORA_PAD + GATE_LORA_PAD
MLA_PAD = 1536
COL_RWKV = 0
COL_MLA = RWKV_PAD
COL_GA = COL_MLA + MLA_PAD
COL_GB = COL_GA + D_MODEL
IN_PAD = COL_GB + D_MODEL

VMEM_LIMIT = 56 * 1024 * 1024
CHUNK = 64
WO_ROWS = 128
SCAN_LANE_BLOCKS = 8


def _cparams(sem):
    return pltpu.CompilerParams(dimension_semantics=sem, vmem_limit_bytes=VMEM_LIMIT)


def _dot(a, b):
    return jnp.dot(a, b, preferred_element_type=F32)


def _dot_nt(a, b):
    return lax.dot_general(a, b, (((1,), (1,)), ((), ())), preferred_element_type=F32)


def _dot_tn(a, b):
    return lax.dot_general(a, b, (((0,), (0,)), ((), ())), preferred_element_type=F32)


def _split_bf16(x):
    hi = x.astype(BF16)
    lo = (x - hi.astype(F32)).astype(BF16)
    return hi, lo


def _rms(x, g):
    ms = jnp.mean(x * x, axis=-1, keepdims=True)
    return x * lax.rsqrt(ms + NORM_EPS) * g


def _sigmoid(x):
    return 1.0 / (1.0 + jnp.exp(-x))


def _inproj_kernel(x_ref, g_ref, w_ref, o_ref, h_scr):
    @pl.when(pl.program_id(1) == 0)
    def _():
        h_scr[...] = _rms(x_ref[...], g_ref[...]).astype(BF16)

    o_ref[...] = _dot(h_scr[...], w_ref[...]).astype(o_ref.dtype)


def _inproj(x2, g, w, tm, tn):
    T, D = x2.shape
    N = w.shape[1]
    return pl.pallas_call(
        _inproj_kernel,
        grid=(T // tm, N // tn),
        in_specs=[
            pl.BlockSpec((tm, D), lambda i, j: (i, 0)),
            pl.BlockSpec((1, D), lambda i, j: (0, 0)),
            pl.BlockSpec((D, tn), lambda i, j: (0, j)),
        ],
        out_specs=pl.BlockSpec((tm, tn), lambda i, j: (i, j)),
        out_shape=jax.ShapeDtypeStruct((T, N), BF16),
        scratch_shapes=[pltpu.VMEM((tm, D), BF16)],
        compiler_params=_cparams(("parallel", "arbitrary")),
        name="inproj",
    )(x2, g, w)


HALO = 16


def _rwkv_prep_kernel(cur_ref, halo_ref, mu_ref, w0_ref, w2_ref, a0_ref, a2_ref, g2_ref,
                      kk_ref, ka_ref, bd_ref,
                      r_out, k_out, v_out, kk_out, a_out, lw_out, g_out, *, tm, seq):
    i = pl.program_id(0)
    cur = cur_ref[...].astype(F32)
    first = (i * tm) % seq == 0
    last_prev = halo_ref[HALO - 1:HALO, :].astype(F32)
    last_prev = jnp.where(first, 0.0, last_prev)
    row = lax.broadcasted_iota(I32, cur.shape, 0)
    prev = jnp.where(row == 0, last_prev, pltpu.roll(cur, 1, 0))
    u = cur + (prev - cur) * mu_ref[...]
    W = RWKV_WIDTH
    r = u[:, 0:W]
    k = u[:, W:2 * W]
    v = u[:, 2 * W:3 * W]
    dw = u[:, 3 * W:3 * W + LORA_PAD]
    da = u[:, 3 * W + LORA_PAD:3 * W + 2 * LORA_PAD]
    dg = u[:, 3 * W + 2 * LORA_PAD:3 * W + 2 * LORA_PAD + GATE_LORA_PAD]
    z = w0_ref[...] + _dot(jnp.tanh(dw).astype(BF16), w2_ref[...])
    w_log = -(jnp.maximum(-z, 0.0) + jnp.log(1.0 + jnp.exp(-jnp.abs(z)))) - 0.5
    lw_out[...] = -jnp.exp(w_log)
    a = _sigmoid(a0_ref[...] + _dot(da.astype(BF16), a2_ref[...]))
    g_out[...] = _dot(_sigmoid(dg).astype(BF16), g2_ref[...]).astype(g_out.dtype)
    kk = k * kk_ref[...]
    hi, lo = _split_bf16(kk * kk)
    ssq = _dot(hi, bd_ref[...]) + _dot(lo, bd_ref[...])
    kk = kk / jnp.maximum(jnp.sqrt(ssq), 1e-12)
    r_out[...] = r.astype(r_out.dtype)
    k_out[...] = (k * (1.0 + (a - 1.0) * ka_ref[...])).astype(k_out.dtype)
    v_out[...] = v.astype(v_out.dtype)
    kk_out[...] = kk.astype(kk_out.dtype)
    a_out[...] = a.astype(a_out.dtype)


def _rwkv_prep(proj, mu, w0, w2, a0, a2, g2, k_k, k_a, bd, tm, seq):
    T = proj.shape[0]
    W = RWKV_WIDTH
    full = lambda shape: pl.BlockSpec(shape, lambda i: (0, 0))
    out = lambda: pl.BlockSpec((tm, W), lambda i: (i, 0))
    return pl.pallas_call(
        functools.partial(_rwkv_prep_kernel, tm=tm, seq=seq),
        grid=(T // tm,),
        in_specs=[
            pl.BlockSpec((tm, RWKV_PAD), lambda i: (i, 0)),
            pl.BlockSpec((HALO, RWKV_PAD), lambda i: (jnp.maximum(i * (tm // HALO) - 1, 0), 0)),
            full((1, RWKV_PAD)), full((1, W)), full((LORA_PAD, W)), full((1, W)),
            full((LORA_PAD, W)), full((GATE_LORA_PAD, W)), full((1, W)), full((1, W)),
            full((W, W)),
        ],
        out_specs=[out() for _ in range(7)],
        out_shape=[jax.ShapeDtypeStruct((T, W), dt)
                   for dt in (BF16, BF16, BF16, BF16, BF16, F32, BF16)],
        compiler_params=_cparams(("parallel",)),
        name="rwkv_prep",
    )(proj, proj, mu, w0, w2, a0, a2, g2, k_k, k_a, bd)


def _rwkv_scan_kernel(r_ref, k_ref, v_ref, kk_ref, a_ref, lw_ref, g_ref,
                      gnw_ref, gnb_ref, rk_ref, o_ref, s_scr, *, rows):
    L = CHUNK
    L2 = 2 * L

    @pl.when(pl.program_id(2) == 0)
    def _():
        s_scr[...] = jnp.zeros_like(s_scr)

    lane = lax.broadcasted_iota(I32, (L, LANES), 1)
    m_a = (lane < RWKV_HEAD).astype(F32)
    m_b = 1.0 - m_a
    ri = lax.broadcasted_iota(I32, (L2, L2), 0)
    ci = lax.broadcasted_iota(I32, (L2, L2), 1)
    strict = ri > ci
    incl = ri >= ci
    eye = (ri == ci).astype(F32)
    tri = (lax.broadcasted_iota(I32, (L, L), 0) >= lax.broadcasted_iota(I32, (L, L), 1)).astype(BF16)
    hr = lax.broadcasted_iota(I32, (LANES, LANES), 0) // RWKV_HEAD
    hc = lax.broadcasted_iota(I32, (LANES, LANES), 1) // RWKV_HEAD
    head_ones = (hr == hc).astype(BF16)
    nb = SCAN_LANE_BLOCKS

    def stack(x):
        return jnp.concatenate([x * m_a, x * m_b], axis=0)

    chunks = range(rows // L)
    rows_of = lambda c: slice(c * L, (c + 1) * L)
    lanes_of = lambda h: slice(h * LANES, (h + 1) * LANES)
    each = lambda f, *lists: [f(*vals) for vals in zip(*lists)]

    def operands(c, h):
        sl, ln = rows_of(c), lanes_of(h)
        r = r_ref[0, sl, ln].astype(F32)
        k = k_ref[0, sl, ln].astype(F32)
        v = v_ref[0, sl, ln].astype(F32)
        kk = kk_ref[0, sl, ln].astype(F32)
        a = a_ref[0, sl, ln].astype(F32)
        lw = lw_ref[0, sl, ln]
        hi, lo = _split_bf16(lw)
        cum = _dot(tri, hi) + _dot(tri, lo)
        c_end = cum[L - 1:L, :]
        p_inv = jnp.exp(-cum)
        p_end = jnp.exp(c_end - cum)
        b = kk * a
        return dict(
            x_a=stack(-(kk * jnp.exp(cum - lw))).astype(BF16),
            x_r=stack(r * jnp.exp(cum)),
            x_bk=jnp.concatenate([stack(b * p_inv), stack(k * p_inv)], axis=0).astype(BF16),
            v_st=stack(v).astype(BF16),
            z_hat=jnp.concatenate([stack(b * p_end), stack(k * p_end)], axis=0).astype(BF16),
            p_row=jnp.exp(c_end),
            rkr=(r * k * rk_ref[:, ln]).astype(BF16), v=v)

    ops = [operands(c, h) for c in chunks for h in range(nb)]
    big = [_dot_nt(jnp.concatenate([o["x_a"], o["x_r"].astype(BF16)], axis=0), o["x_bk"])
           for o in ops]
    n_pow = [jnp.where(strict, m[0:L2, 0:L2], 0.0) for m in big]
    a_ak = [jnp.where(strict, m[0:L2, L2:2 * L2], 0.0).astype(BF16) for m in big]
    a_r = [jnp.concatenate([jnp.where(incl, m[L2:2 * L2, 0:L2], 0.0),
                            jnp.where(incl, m[L2:2 * L2, L2:2 * L2], 0.0)], axis=1).astype(BF16)
           for m in big]

    t_inv = [eye + n for n in n_pow]
    for _ in range(5):
        n_pow = each(lambda n: _dot(n.astype(BF16), n.astype(BF16)), n_pow)
        t_inv = each(lambda t, n: t + _dot(t.astype(BF16), n.astype(BF16)), t_inv, n_pow)

    akv = each(lambda m, o: _dot(m, o["v_st"]).astype(BF16), a_ak, ops)
    w = each(lambda t, o, x: _dot(t.astype(BF16), jnp.concatenate([o["x_a"], x], axis=1)),
             t_inv, ops, akv)
    w1 = [m[:, 0:LANES].astype(BF16) for m in w]
    w2v = each(lambda m, o: jnp.concatenate([m[:, LANES:].astype(BF16), o["v_st"]], axis=0), w, ops)
    g_mat = each(lambda o, m, x: (o["x_r"] + _dot(m[:, 0:L2], x)).astype(BF16), ops, a_r, w1)
    y0 = each(_dot, a_r, w2v)
    m_mat = each(lambda x, o: _dot_tn(x, o["z_hat"][0:L2, :]).astype(BF16), w1, ops)
    c2 = each(lambda x, o: _dot_tn(x, o["z_hat"]), w2v, ops)
    bonus_v = [_dot(o["rkr"], head_ones) * o["v"] for o in ops]

    s = [s_scr[h] for h in range(nb)]
    y = []
    for i in range(len(ops)):
        h = i % nb
        s_b = s[h].astype(BF16)
        y_st = _dot_nt(g_mat[i], s_b) + y0[i]
        s[h] = s[h] * ops[i]["p_row"] + _dot(s_b, m_mat[i]) + c2[i]
        y.append(y_st[0:L, :] + y_st[L:L2, :])
    for h in range(nb):
        s_scr[h] = s[h]

    mean = [_dot(t.astype(BF16), head_ones) * (1.0 / RWKV_HEAD) for t in y]
    yc = each(lambda t, m: t - m, y, mean)
    var = [_dot((t * t).astype(BF16), head_ones) * (1.0 / RWKV_HEAD) for t in yc]
    for i in range(len(ops)):
        sl, ln = rows_of(i // nb), lanes_of(i % nb)
        yn = yc[i] * lax.rsqrt(var[i] + GN_EPS) * gnw_ref[:, ln] + gnb_ref[:, ln]
        o_ref[0, sl, ln] = ((yn + bonus_v[i]) * g_ref[0, sl, ln].astype(F32)).astype(o_ref.dtype)


def _rwkv_scan(r, k, v, kk, a, lw, g, gn_w, gn_b, r_k, rows):
    B, S, W = r.shape
    wb = SCAN_LANE_BLOCKS * LANES
    npair = W // wb
    seq = lambda: pl.BlockSpec((1, rows, wb), lambda b, p, c: (b, c, p))
    par = lambda: pl.BlockSpec((1, wb), lambda b, p, c: (0, p))
    return pl.pallas_call(
        functools.partial(_rwkv_scan_kernel, rows=rows),
        grid=(B, npair, S // rows),
        in_specs=[seq() for _ in range(7)] + [par(), par(), par()],
        out_specs=seq(),
        out_shape=jax.ShapeDtypeStruct((B, S, W), BF16),
        scratch_shapes=[pltpu.VMEM((SCAN_LANE_BLOCKS, LANES, LANES), F32)],
        compiler_params=_cparams(("parallel", "parallel", "arbitrary")),
        name="rwkv_scan",
    )(r, k, v, kk, a, lw, g, gn_w, gn_b, r_k)


def _rope128(x, cos, sin_signed):
    lane = lax.broadcasted_iota(I32, x.shape, 1)
    half = QK_ROPE // 2
    partner = jnp.where(lane < half, pltpu.roll(x, LANES - half, 1), pltpu.roll(x, half, 1))
    return x * cos + partner * sin_signed


def _mla_prep_kernel(cq_ref, ckv_ref, kr_ref, pos_ref, invf_ref, gqa_ref, gkva_ref,
                     wuq_ref, wukv_ref, gqn_ref, gkn_ref,
                     q_out, k_out, v_out, cqn_scr, ckvn_scr, cos_scr, sin_scr):
    @pl.when(pl.program_id(1) == 0)
    def _():
        cqn_scr[...] = _rms(cq_ref[...].astype(F32), gqa_ref[...]).astype(BF16)
        ckvn_scr[...] = _rms(ckv_ref[...].astype(F32), gkva_ref[...]).astype(BF16)
        ang = pos_ref[...].astype(F32) * invf_ref[...]
        lane = lax.broadcasted_iota(I32, ang.shape, 1)
        cos_scr[...] = jnp.cos(ang)
        sin_scr[...] = jnp.where(lane < QK_ROPE // 2, -1.0, 1.0) * jnp.sin(ang)

    cos = cos_scr[...]
    sin = sin_scr[...]
    inv_n = 1.0 / QK_HEAD
    q = _dot(cqn_scr[...], wuq_ref[...])
    rs = lax.rsqrt(jnp.sum(q * q, axis=-1, keepdims=True) * inv_n + NORM_EPS)
    q = q * rs * gqn_ref[...] * (QK_HEAD ** -0.5 * LOG2E)
    q_out[...] = jnp.concatenate(
        [q[:, 0:LANES], _rope128(q[:, LANES:], cos, sin)], axis=1).astype(q_out.dtype)

    kv = _dot(ckvn_scr[...], wukv_ref[...])
    k_nope = kv[:, 0:QK_NOPE]
    kr = kr_ref[...].astype(F32)
    ssq = (jnp.sum(k_nope * k_nope, axis=-1, keepdims=True)
           + jnp.sum(kr * kr, axis=-1, keepdims=True))
    rs = lax.rsqrt(ssq * inv_n + NORM_EPS)
    gkn = gkn_ref[...]
    k_out[...] = jnp.concatenate(
        [k_nope * rs * gkn[:, 0:LANES], _rope128(kr * rs * gkn[:, LANES:], cos, sin)],
        axis=1).astype(k_out.dtype)
    v_out[...] = jnp.transpose(kv[:, QK_NOPE:].astype(v_out.dtype))


def _mla_prep(proj, pos, invf, g_qa, g_kva, wuq, wukv, gqn, gkn, tm):
    T = proj.shape[0]
    H = MLA_HEADS
    c0 = COL_MLA // Q_LORA
    ckr = (COL_MLA + Q_LORA + KV_LORA) // LANES
    full = lambda shape: pl.BlockSpec(shape, lambda i, h: (0, 0))
    return pl.pallas_call(
        _mla_prep_kernel,
        grid=(T // tm, H),
        in_specs=[
            pl.BlockSpec((tm, Q_LORA), lambda i, h: (i, c0)),
            pl.BlockSpec((tm, KV_LORA), lambda i, h: (i, c0 + 1)),
            pl.BlockSpec((tm, LANES), lambda i, h: (i, ckr)),
            pl.BlockSpec((tm, 1), lambda i, h: (i, 0)),
            full((1, LANES)), full((1, Q_LORA)), full((1, KV_LORA)),
            pl.BlockSpec((Q_LORA, QK_PAD), lambda i, h: (0, h)),
            pl.BlockSpec((KV_LORA, QK_NOPE + V_HEAD), lambda i, h: (0, h)),
            full((1, QK_PAD)), full((1, QK_PAD)),
        ],
        out_specs=[
            pl.BlockSpec((tm, QK_PAD), lambda i, h: (i, h)),
            pl.BlockSpec((tm, QK_PAD), lambda i, h: (i, h)),
            pl.BlockSpec((V_HEAD, tm), lambda i, h: (h, i)),
        ],
        out_shape=[jax.ShapeDtypeStruct((T, H * QK_PAD), BF16),
                   jax.ShapeDtypeStruct((T, H * QK_PAD), BF16),
                   jax.ShapeDtypeStruct((H * V_HEAD, T), BF16)],
        scratch_shapes=[pltpu.VMEM((tm, Q_LORA), BF16), pltpu.VMEM((tm, KV_LORA), BF16),
                        pltpu.VMEM((tm, LANES), F32), pltpu.VMEM((tm, LANES), F32)],
        compiler_params=_cparams(("parallel", "arbitrary")),
        name="mla_prep",
    )(proj, proj, proj, pos, invf, g_qa, g_kva, wuq, wukv, gqn, gkn)


MASK_VALUE = -1e30
LOG2E = 1.4426950408889634


def _flash_kernel(q_ref, k_ref, vt_ref, o_ref, s_scr, p_scr, alpha_scr, m_scr, l_scr, acc_scr,
                  *, tq):
    i = pl.program_id(2)
    q = q_ref[0]
    m_scr[...] = jnp.full_like(m_scr, MASK_VALUE)
    l_scr[...] = jnp.zeros_like(l_scr)
    acc_scr[...] = jnp.zeros_like(acc_scr)

    q_half = tq // 2

    def scores(j, slot, half):
        ks = pl.ds(pl.multiple_of(j * tq, tq), tq)
        cols = slice(half * q_half, (half + 1) * q_half)
        s_scr[slot, :, cols] = _dot_nt(k_ref[0, ks, :], q[cols, :])

    def softmax(slot, masked, strips):
        for c in strips:
            cols = slice(c * LANES, (c + 1) * LANES)
            s = s_scr[slot, :, cols]
            if masked:
                kpos = lax.broadcasted_iota(I32, s.shape, 0)
                qpos = lax.broadcasted_iota(I32, s.shape, 1) + c * LANES
                s = jnp.where(kpos <= qpos, s, MASK_VALUE)
            m_old = m_scr[:, cols]
            m_new = jnp.maximum(m_old, jnp.max(s, axis=0, keepdims=True))
            p = jnp.exp2(s - m_new)
            alpha = jnp.exp2(m_old - m_new)
            l_scr[:, cols] = alpha * l_scr[:, cols] + jnp.sum(p, axis=0, keepdims=True)
            m_scr[:, cols] = m_new
            p_scr[slot, :, cols] = p.astype(BF16)
            alpha_scr[slot, :, cols] = alpha

    def weighted_values(j, slot):
        vt = vt_ref[:, pl.ds(pl.multiple_of(j * tq, tq), tq)]
        acc_scr[...] = alpha_scr[slot] * acc_scr[...] + _dot(vt, p_scr[slot])

    scores(0, 0, 0)
    scores(0, 0, 1)
    p_scr[1] = jnp.zeros((tq, tq), BF16)
    alpha_scr[1] = jnp.ones((1, tq), F32)
    n_strips = tq // LANES
    first, second = range(0, n_strips // 2), range(n_strips // 2, n_strips)

    def step(j, slot):
        weighted_values(jnp.maximum(j - 1, 0), 1 - slot)
        scores(j + 1, 1 - slot, 0)
        softmax(slot, False, first)
        scores(j + 1, 1 - slot, 1)
        softmax(slot, False, second)

    def pair(jp, carry):
        step(2 * jp, 0)
        step(2 * jp + 1, 1)
        return carry

    def finish(slot):
        weighted_values(jnp.maximum(i - 1, 0), 1 - slot)
        softmax(slot, True, range(n_strips))
        weighted_values(i, slot)
        o_ref[0] = jnp.transpose(acc_scr[...] / l_scr[...]).astype(o_ref.dtype)

    lax.fori_loop(0, i // 2, pair, 0)

    @pl.when(i % 2 == 0)
    def _():
        finish(0)

    @pl.when(i % 2 == 1)
    def _():
        step(i - 1, 0)
        finish(1)


def _flash(q, k, vt, tq):
    B, S, _ = q.shape
    H = MLA_HEADS
    return pl.pallas_call(
        functools.partial(_flash_kernel, tq=tq),
        grid=(B, H, S // tq),
        in_specs=[
            pl.BlockSpec((1, tq, QK_PAD), lambda b, h, i: (b, i, h)),
            pl.BlockSpec((1, S, QK_PAD), lambda b, h, i: (b, 0, h)),
            pl.BlockSpec((V_HEAD, S), lambda b, h, i: (h, b)),
        ],
        out_specs=pl.BlockSpec((1, tq, V_HEAD), lambda b, h, i: (b, i, h)),
        out_shape=jax.ShapeDtypeStruct((B, S, H * V_HEAD), BF16),
        scratch_shapes=[pltpu.VMEM((2, tq, tq), F32), pltpu.VMEM((2, tq, tq), BF16),
                        pltpu.VMEM((2, 1, tq), F32),
                        pltpu.VMEM((1, tq), F32), pltpu.VMEM((1, tq), F32),
                        pltpu.VMEM((V_HEAD, tq), F32)],
        compiler_params=_cparams(("parallel", "parallel", "arbitrary")),
        name="flash_attn",
    )(q, k, vt)


def _merge_kernel(oa_ref, ob_ref, wa_ref, wb_ref, ga_ref, gb_ref, o_ref):
    ya = _dot(oa_ref[...], wa_ref[...])
    yb = _dot(ob_ref[...], wb_ref[...])
    ga = _sigmoid(ga_ref[...].astype(F32))
    gb = _sigmoid(gb_ref[...].astype(F32))
    o_ref[...] = (ga * ya + gb * yb).astype(o_ref.dtype)


def _merge(oa, ob, wa, wb, proj, tm, tn):
    T, K = oa.shape
    N = wa.shape[1]
    ca = COL_GA // tn
    cb = COL_GB // tn
    return pl.pallas_call(
        _merge_kernel,
        grid=(T // tm, N // tn),
        in_specs=[
            pl.BlockSpec((tm, K), lambda i, j: (i, 0)),
            pl.BlockSpec((tm, K), lambda i, j: (i, 0)),
            pl.BlockSpec((K, tn), lambda i, j: (0, j)),
            pl.BlockSpec((K, tn), lambda i, j: (0, j)),
            pl.BlockSpec((tm, tn), lambda i, j: (i, ca + j)),
            pl.BlockSpec((tm, tn), lambda i, j: (i, cb + j)),
        ],
        out_specs=pl.BlockSpec((tm, tn), lambda i, j: (i, j)),
        out_shape=jax.ShapeDtypeStruct((T, N), BF16),
        compiler_params=_cparams(("parallel", "parallel")),
        name="merge",
    )(oa, ob, wa, wb, proj, proj)


def _pack_rows(x, out_ref, n, row0=0):
    for c in range(PACK_ROWS):
        lo = x[:, c * 2 * LANES:c * 2 * LANES + LANES]
        hi = x[:, c * 2 * LANES + LANES:(c + 1) * 2 * LANES]
        lo_b = pltpu.bitcast(lo.astype(BF16).astype(F32), U32)
        hi_b = pltpu.bitcast(hi.astype(BF16).astype(F32), U32)
        out_ref[pl.ds(row0 * PACK_ROWS + c, n, stride=PACK_ROWS), :] = hi_b | (lo_b >> 16)


def _unpack_rows(ref, c, tm, lead=None):
    idx = (pl.ds(c, tm, stride=PACK_ROWS), slice(None))
    if lead is not None:
        idx = (lead,) + idx
    u = ref[idx]
    lo = pltpu.bitcast(u << 16, F32)
    hi = pltpu.bitcast(u & jnp.uint32(0xFFFF0000), F32)
    return lo, hi


def _wo_kernel(m_ref, x_ref, wo_ref, g_ref, wr_ref, x1_ref, h2_ref, h2p_ref, lg_ref, *, tm):
    hw = WO_ROWS
    groups = [slice(h * hw, (h + 1) * hw) for h in range(tm // hw)]
    wo = wo_ref[...]
    x1 = [x_ref[r, :] + _dot(m_ref[r, :], wo) for r in groups]
    h2 = [_rms(v, g_ref[...]) for v in x1]
    w_hi, w_lo = _split_bf16(wr_ref[...])
    for h, r in enumerate(groups):
        x1_ref[r, :] = x1[h]
        h2_ref[r, :] = h2[h].astype(BF16)
        _pack_rows(h2[h], h2p_ref, hw, row0=h * hw)
        h_hi, h_lo = _split_bf16(h2[h])
        lg_ref[:, r] = _dot_nt(w_hi, h_hi) + _dot_nt(w_hi, h_lo) + _dot_nt(w_lo, h_hi)


def _wo(merged, x2, wo, g, wr_t, tm):
    T, D = x2.shape
    E = wr_t.shape[0]
    return pl.pallas_call(
        functools.partial(_wo_kernel, tm=tm),
        grid=(T // tm,),
        in_specs=[
            pl.BlockSpec((tm, D), lambda i: (i, 0)),
            pl.BlockSpec((tm, D), lambda i: (i, 0)),
            pl.BlockSpec((D, D), lambda i: (0, 0), pipeline_mode=pl.Buffered(1)),
            pl.BlockSpec((1, D), lambda i: (0, 0)),
            pl.BlockSpec((E, D), lambda i: (0, 0)),
        ],
        out_specs=[
            pl.BlockSpec((tm, D), lambda i: (i, 0)),
            pl.BlockSpec((tm, D), lambda i: (i, 0)),
            pl.BlockSpec((tm * PACK_ROWS, PACK_W), lambda i: (i, 0)),
            pl.BlockSpec((E, tm), lambda i: (0, i)),
        ],
        out_shape=[jax.ShapeDtypeStruct((T, D), F32),
                   jax.ShapeDtypeStruct((T, D), BF16),
                   jax.ShapeDtypeStruct((T * PACK_ROWS, PACK_W), U32),
                   jax.ShapeDtypeStruct((E, T), F32)],
        compiler_params=_cparams(("parallel",)),
        name="wo_norm_router",
    )(merged, x2, wo, g, wr_t)


def _route_kernel(lg_ref, bias_ref, eidx_ref, wts_ref, rnk_ref, cnt_ref, carry_scr, *, tm):
    E = N_EXPERTS
    NEG = -jnp.inf

    @pl.when(pl.program_id(0) == 0)
    def _():
        carry_scr[...] = jnp.zeros_like(carry_scr)

    scores = _sigmoid(lg_ref[...])
    choice = scores + bias_ref[...]
    c3 = choice.reshape(N_GROUPS, GROUP_SIZE, tm)
    sub = lax.broadcasted_iota(I32, c3.shape, 1)
    m1 = jnp.max(c3, axis=1, keepdims=True)
    i1 = jnp.min(jnp.where(c3 == m1, sub, GROUP_SIZE), axis=1, keepdims=True)
    m2 = jnp.max(jnp.where(sub == i1, NEG, c3), axis=1, keepdims=True)
    gs = (m1 + m2).reshape(N_GROUPS, tm)
    gi = lax.broadcasted_iota(I32, gs.shape, 0)
    gsel = jnp.zeros(gs.shape, F32)
    for _ in range(TOPK_GROUPS):
        mx = jnp.max(gs, axis=0, keepdims=True)
        ix = jnp.min(jnp.where(gs == mx, gi, N_GROUPS), axis=0, keepdims=True)
        hit = gi == ix
        gsel = jnp.where(hit, 1.0, gsel)
        gs = jnp.where(hit, NEG, gs)
    emask = jnp.broadcast_to(gsel.reshape(N_GROUPS, 1, tm), (N_GROUPS, GROUP_SIZE, tm)).reshape(E, tm)
    x = jnp.where(emask > 0.5, choice, NEG)
    ei = lax.broadcasted_iota(I32, x.shape, 0)
    sel = jnp.zeros(x.shape, F32)
    idx_rows, w_rows = [], []
    for _ in range(TOP_K):
        mx = jnp.max(x, axis=0, keepdims=True)
        ix = jnp.min(jnp.where(x == mx, ei, E), axis=0, keepdims=True)
        hit = ei == ix
        w_rows.append(jnp.sum(jnp.where(hit, scores, 0.0), axis=0, keepdims=True))
        idx_rows.append(ix)
        sel = jnp.where(hit, 1.0, sel)
        x = jnp.where(hit, NEG, x)
    w = jnp.concatenate(w_rows, axis=0)
    w = w / (jnp.sum(w, axis=0, keepdims=True) + 1e-20) * ROUTED_SCALE
    eidx_ref[...] = jnp.concatenate(idx_rows, axis=0)
    wts_ref[...] = w

    upper = (lax.broadcasted_iota(I32, (tm, tm), 0) < lax.broadcasted_iota(I32, (tm, tm), 1)).astype(BF16)
    base = carry_scr[...][:, 0:1]
    excl = _dot(sel.astype(BF16), upper) + base
    rnk_ref[...] = jnp.concatenate(
        [jnp.sum(jnp.where(ei == ix, excl, 0.0), axis=0, keepdims=True) for ix in idx_rows],
        axis=0).astype(I32)
    carry_scr[...] = carry_scr[...] + jnp.sum(sel, axis=1, keepdims=True)
    cnt_ref[...] = carry_scr[...]


def _route(logits_t, bias, tm):
    E, T = logits_t.shape
    K = TOP_K
    tok = lambda: pl.BlockSpec((K, tm), lambda i: (0, i))
    return pl.pallas_call(
        functools.partial(_route_kernel, tm=tm),
        grid=(T // tm,),
        in_specs=[pl.BlockSpec((E, tm), lambda i: (0, i)),
                  pl.BlockSpec((E, 1), lambda i: (0, 0))],
        out_specs=[tok(), tok(), tok(), pl.BlockSpec((E, LANES), lambda i: (0, 0))],
        out_shape=[jax.ShapeDtypeStruct((K, T), I32), jax.ShapeDtypeStruct((K, T), F32),
                   jax.ShapeDtypeStruct((K, T), I32), jax.ShapeDtypeStruct((E, LANES), F32)],
        scratch_shapes=[pltpu.VMEM((E, LANES), F32)],
        compiler_params=_cparams(("arbitrary",)),
        name="route",
    )(logits_t, bias)


def _slots_kernel(pst_ref, eidx_ref, rnk_ref, dest_ref):
    eidx = eidx_ref[...]
    dest = rnk_ref[...]
    for e in range(N_EXPERTS):
        dest = dest + jnp.where(eidx == e, pst_ref[e], 0)
    dest_ref[...] = dest


def _slots(eidx, rnk, pstart, tm):
    K, T = eidx.shape
    tok = lambda: pl.BlockSpec((K, tm), lambda i: (0, i))
    return pl.pallas_call(
        _slots_kernel,
        grid=(T // tm,),
        in_specs=[pl.BlockSpec(memory_space=pltpu.SMEM), tok(), tok()],
        out_specs=tok(),
        out_shape=jax.ShapeDtypeStruct((K, T), I32),
        compiler_params=_cparams(("parallel",)),
        name="slots",
    )(pstart, eidx, rnk)


def _ffn_kernel(te_ref, nu_ref, tok_ref, dst_ref, h_hbm, wg_ref, wu_ref, wd_ref, ys_hbm,
                gbuf, obuf, wg_b, wu_b, wd_b, gsem, osem, *, tm):
    s = pl.program_id(0)
    nu = nu_ref[0]
    tile_rows = tm * PACK_ROWS

    def gather(slot):
        for r in range(tm):
            pltpu.make_async_copy(
                h_hbm.at[pl.ds(pl.multiple_of(tok_ref[0, 0, r] * PACK_ROWS, PACK_ROWS), PACK_ROWS), :],
                gbuf.at[slot, pl.ds(r * PACK_ROWS, PACK_ROWS), :], gsem.at[slot]).start()

    def scatter(slot):
        for r in range(tm):
            pltpu.make_async_copy(
                obuf.at[slot, pl.ds(r * PACK_ROWS, PACK_ROWS), :],
                ys_hbm.at[pl.ds(pl.multiple_of(dst_ref[0, 0, r] * PACK_ROWS, PACK_ROWS), PACK_ROWS), :],
                osem.at[slot]).start()

    def wait_gather(slot):
        pltpu.make_async_copy(h_hbm.at[pl.ds(0, tile_rows), :], gbuf.at[slot], gsem.at[slot]).wait()

    def wait_scatter(slot):
        pltpu.make_async_copy(obuf.at[slot], ys_hbm.at[pl.ds(0, tile_rows), :], osem.at[slot]).wait()

    def compute(slot):
        parts = []
        for c in range(PACK_ROWS):
            lo, hi = _unpack_rows(gbuf.at[slot], c, tm)
            parts += [lo.astype(BF16), hi.astype(BF16)]
        a = jnp.concatenate(parts, axis=1)
        hg = _dot(a, wg_b[...])
        hu = _dot(a, wu_b[...])
        hid = (hg * _sigmoid(hg) * hu).astype(BF16)
        _pack_rows(_dot(hid, wd_b[...]), obuf.at[slot], tm)

    cur = jnp.maximum(s - 1, 0)
    @pl.when((s >= 1) & (s <= nu) & ((s == 1) | (te_ref[cur] != te_ref[jnp.maximum(s - 2, 0)])))
    def _():
        wg_b[...] = wg_ref[0].astype(BF16)
        wu_b[...] = wu_ref[0].astype(BF16)
        wd_b[...] = wd_ref[0].astype(BF16)

    @pl.when(s == 0)
    def _():
        gather(0)

    @pl.when(s == 1)
    def _():
        wait_gather(0)
        gather(1)
        compute(0)

    for slot in range(2):
        parity = s % 2 == slot

        @pl.when(parity & (s >= 3) & (s <= nu + 1))
        def _():
            wait_scatter(1 - slot)

        @pl.when(parity & (s >= 2) & (s <= nu))
        def _():
            wait_gather(1 - slot)
            gather(slot)
            scatter(slot)
            compute(1 - slot)

        @pl.when(parity & (s >= 2) & (s == nu + 1))
        def _():
            wait_gather(1 - slot)
            scatter(slot)
            wait_scatter(slot)


def _ffn(tile_e, n_used, row_tok, row_dst, h2p, wg, wu, wd, tm):
    nt = row_tok.shape[0]
    D, I = wg.shape[1], wg.shape[2]
    cur = lambda s: jnp.maximum(s - 1, 0)
    wspec = lambda shape: pl.BlockSpec(shape, lambda s, te, nu: (te[cur(s)], 0, 0))
    grid_spec = pltpu.PrefetchScalarGridSpec(
        num_scalar_prefetch=2,
        grid=(nt + 1,),
        in_specs=[
            pl.BlockSpec((1, 1, tm), lambda s, te, nu: (jnp.minimum(s, nt - 1), 0, 0),
                         memory_space=pltpu.SMEM),
            pl.BlockSpec((1, 1, tm), lambda s, te, nu: (jnp.maximum(s - 2, 0), 0, 0),
                         memory_space=pltpu.SMEM),
            pl.BlockSpec(memory_space=pl.ANY),
            wspec((1, D, I)), wspec((1, D, I)), wspec((1, I, D)),
        ],
        out_specs=pl.BlockSpec(memory_space=pl.ANY),
        scratch_shapes=[pltpu.VMEM((2, tm * PACK_ROWS, PACK_W), U32),
                        pltpu.VMEM((2, tm * PACK_ROWS, PACK_W), U32),
                        pltpu.VMEM((D, I), BF16), pltpu.VMEM((D, I), BF16), pltpu.VMEM((I, D), BF16),
                        pltpu.SemaphoreType.DMA((2,)), pltpu.SemaphoreType.DMA((2,))],
    )
    return pl.pallas_call(
        functools.partial(_ffn_kernel, tm=tm),
        grid_spec=grid_spec,
        out_shape=jax.ShapeDtypeStruct((nt * tm * PACK_ROWS, PACK_W), U32),
        compiler_params=_cparams(("arbitrary",)),
        name="expert_ffn",
    )(tile_e, n_used, row_tok, row_dst, h2p, wg, wu, wd)


def _combine_kernel(*refs, tm):
    ys_refs = refs[:TOP_K]
    wt_ref, x1_ref, h2_ref, wsg_ref, wsu_ref, wsd_ref, o_ref = refs[TOP_K:]
    h2 = h2_ref[...]
    hg = _dot(h2, wsg_ref[...])
    hu = _dot(h2, wsu_ref[...])
    base = x1_ref[...] + _dot((hg * _sigmoid(hg) * hu).astype(BF16), wsd_ref[...])
    wt = wt_ref[...]
    for c in range(PACK_ROWS):
        acc_lo = base[:, c * 2 * LANES:c * 2 * LANES + LANES]
        acc_hi = base[:, c * 2 * LANES + LANES:(c + 1) * 2 * LANES]
        for k in range(TOP_K):
            lo, hi = _unpack_rows(ys_refs[k], c, tm)
            wk = wt[:, k:k + 1]
            acc_lo = acc_lo + wk * lo
            acc_hi = acc_hi + wk * hi
        o_ref[:, c * 2 * LANES:c * 2 * LANES + LANES] = acc_lo
        o_ref[:, c * 2 * LANES + LANES:(c + 1) * 2 * LANES] = acc_hi


def _combine(ys, wts_t, x1, h2, wsg, wsu, wsd, tm):
    T, D = x1.shape
    K = wts_t.shape[1]
    I = wsg.shape[1]
    full = lambda shape: pl.BlockSpec(shape, lambda i: (0, 0))
    return pl.pallas_call(
        functools.partial(_combine_kernel, tm=tm),
        grid=(T // tm,),
        in_specs=[pl.BlockSpec((tm * PACK_ROWS, PACK_W), lambda i, k=k: (k * (T // tm) + i, 0))
                  for k in range(K)] + [
                  pl.BlockSpec((tm, K), lambda i: (i, 0)),
                  pl.BlockSpec((tm, D), lambda i: (i, 0)),
                  pl.BlockSpec((tm, D), lambda i: (i, 0)),
                  full((D, I)), full((D, I)), full((I, D))],
        out_specs=pl.BlockSpec((tm, D), lambda i: (i, 0)),
        out_shape=jax.ShapeDtypeStruct((T, D), F32),
        compiler_params=_cparams(("parallel",)),
        name="combine",
    )(*([ys] * K), wts_t, x1, h2, wsg, wsu, wsd)


def _ple_kernel(x_ref, p_ref, wp_ref, gpost_ref, gin_ref, wg_ref, o_ref, *, tm):
    hw = WO_ROWS
    groups = [slice(h * hw, (h + 1) * hw) for h in range(tm // hw)]
    wg = wg_ref[...]
    gates, ples = [], []
    for r in groups:
        hn = _rms(x_ref[r, :], gin_ref[...]).astype(BF16)
        gates.append(_dot(hn, wg))
        ples.append(_rms(_dot(p_ref[r, :].astype(BF16), wp_ref[...]), gpost_ref[...]))
    for r, gate, ple in zip(groups, gates, ples):
        o_ref[r, :] = x_ref[r, :] + _sigmoid(gate) * ple


def _ple(x2, p2, wp, gpost, gin, wg, tm):
    T, D = x2.shape
    Pd = p2.shape[1]
    full = lambda shape: pl.BlockSpec(shape, lambda i: (0, 0))
    return pl.pallas_call(
        functools.partial(_ple_kernel, tm=tm),
        grid=(T // tm,),
        in_specs=[
            pl.BlockSpec((tm, D), lambda i: (i, 0)),
            pl.BlockSpec((tm, Pd), lambda i: (i, 0)),
            full((Pd, D)), full((1, D)), full((1, D)),
            pl.BlockSpec((D, D), lambda i: (0, 0), pipeline_mode=pl.Buffered(1)),
        ],
        out_specs=pl.BlockSpec((tm, D), lambda i: (i, 0)),
        out_shape=jax.ShapeDtypeStruct((T, D), F32),
        compiler_params=_cparams(("parallel",)),
        name="ple",
    )(x2, p2, wp, gpost, gin, wg)


def _pad_cols(w, n):
    return jnp.pad(w, ((0, 0), (0, n - w.shape[1])))


def _pad_rows(w, n):
    return jnp.pad(w, ((0, n - w.shape[0]), (0, 0)))


def _layout_w_in(w):
    W = RWKV_WIDTH
    c = 3 * W
    segs = [w[:, 0:c],
            _pad_cols(w[:, c:c + DECAY_LORA], LORA_PAD),
            _pad_cols(w[:, c + DECAY_LORA:c + DECAY_LORA + AAA_LORA], LORA_PAD),
            _pad_cols(w[:, c + DECAY_LORA + AAA_LORA:c + DECAY_LORA + AAA_LORA + GATE_LORA], GATE_LORA_PAD)]
    c += DECAY_LORA + AAA_LORA + GATE_LORA
    mla_cols = Q_LORA + KV_LORA + QK_ROPE
    segs.append(_pad_cols(w[:, c:c + mla_cols], MLA_PAD))
    c += mla_cols
    segs.append(w[:, c:])
    out = jnp.concatenate(segs, axis=1).astype(BF16)
    assert out.shape[1] == IN_PAD
    return out


def _layout_mu(mu):
    W = RWKV_WIDTH
    c = 3 * W
    segs = [mu[0:c],
            jnp.pad(mu[c:c + DECAY_LORA], (0, LORA_PAD - DECAY_LORA)),
            jnp.pad(mu[c + DECAY_LORA:c + DECAY_LORA + AAA_LORA], (0, LORA_PAD - AAA_LORA)),
            jnp.pad(mu[c + DECAY_LORA + AAA_LORA:], (0, GATE_LORA_PAD - GATE_LORA))]
    return jnp.concatenate(segs).reshape(1, RWKV_PAD)


def _moe_tables(eidx, rnk, counts, tmf, tm_slots):
    K, T = eidx.shape
    E = N_EXPERTS
    A = K * T
    n_tiles = A // tmf + E
    P = n_tiles * tmf
    pcounts = (counts + tmf - 1) // tmf * tmf
    pends = jnp.cumsum(pcounts)
    pstart = (pends - pcounts).astype(I32)
    tile_start = jnp.arange(n_tiles, dtype=I32) * tmf
    tile_e = jnp.minimum(jnp.sum((pends[None, :] <= tile_start[:, None]).astype(I32), axis=1), E - 1)
    n_used = (pends[E - 1:] // tmf).astype(I32)
    dest = _slots(eidx, rnk, pstart, tm_slots)
    j = jnp.arange(tmf, dtype=I32)[None, :]
    tail = j < (pcounts - counts)[:, None]
    spare = jnp.cumsum(jnp.logical_not(tail).reshape(-1).astype(I32)) - 1
    pad_row = jnp.where(tail.reshape(-1), ((pstart + counts)[:, None] + j).reshape(-1),
                        pends[E - 1] + spare)
    q = jnp.arange(P - A, dtype=I32)
    slot_id = jnp.arange(A, dtype=I32)
    rows = jnp.concatenate([dest.reshape(-1), pad_row.astype(I32)])
    slots = jnp.concatenate([slot_id, A + q])
    row_dst = lax.sort((rows, slots), num_keys=1)[1]
    row_tok = jnp.where(row_dst < A, row_dst % T, 0)
    return (row_tok.reshape(n_tiles, 1, tmf), row_dst.reshape(n_tiles, 1, tmf), tile_e, n_used)


def _layer(x, p, positions, g_mix, w_in, mu_rwkv, w0, w2, a0, a2, g2, k_k, k_a, r_k,
           gn_w, gn_b, w_a_up, g_qa, g_kva, w_uq, w_ukv, g_qn, g_kn, w_b_up, w_o,
           g_ffn, w_router, router_bias, w_exp_gate, w_exp_up, w_exp_down,
           w_sh_gate, w_sh_up, w_sh_down, w_ple, g_ple_post, g_ple_in, w_ple_gate,
           *, tiles):
    B, S, D = x.shape
    T = B * S
    W = RWKV_WIDTH
    x2 = x.reshape(T, D)
    row = lambda v: v.reshape(1, -1).astype(F32)

    proj = _inproj(x2, row(g_mix), _layout_w_in(w_in), tiles["tm_in"], tiles["tn_in"])

    head_blk = (jnp.arange(W)[:, None] // RWKV_HEAD == jnp.arange(W)[None, :] // RWKV_HEAD).astype(BF16)
    r, k, v, kk, a, lw, g = _rwkv_prep(
        proj, _layout_mu(mu_rwkv), row(w0), _pad_rows(w2, LORA_PAD).astype(BF16), row(a0),
        _pad_rows(a2, LORA_PAD).astype(BF16), _pad_rows(g2, GATE_LORA_PAD).astype(BF16),
        row(k_k), row(k_a), head_blk, tiles["tm_prep"], S)
    b3 = lambda t: t.reshape(B, S, W)
    o_a = _rwkv_scan(b3(r), b3(k), b3(v), b3(kk), b3(a), b3(lw), b3(g),
                     row(gn_w), row(gn_b), row(r_k), tiles["rows_scan"]).reshape(T, W)

    half = QK_ROPE // 2
    inv_freq = ROPE_THETA ** (-jnp.arange(half, dtype=F32) / half)
    invf = jnp.concatenate([inv_freq, inv_freq, jnp.zeros((LANES - QK_ROPE,), F32)]).reshape(1, LANES)
    wuq = jnp.pad(w_uq.reshape(Q_LORA, MLA_HEADS, QK_HEAD),
                  ((0, 0), (0, 0), (0, QK_PAD - QK_HEAD))).reshape(Q_LORA, MLA_HEADS * QK_PAD)
    pad_g = lambda gv: jnp.pad(gv, (0, QK_PAD - QK_HEAD)).reshape(1, QK_PAD)
    q, kx, vx = _mla_prep(proj, positions.reshape(T, 1).astype(I32), invf, row(g_qa), row(g_kva),
                          wuq.astype(BF16), w_ukv.astype(BF16), pad_g(g_qn), pad_g(g_kn),
                          tiles["tm_mla"])
    o_b = _flash(q.reshape(B, S, -1), kx.reshape(B, S, -1), vx, tiles["tq"]).reshape(T, MLA_WIDTH)

    merged = _merge(o_a, o_b, w_a_up.astype(BF16), w_b_up.astype(BF16), proj,
                    tiles["tm_merge"], tiles["tn_merge"])

    x1, h2, h2p, logits_t = _wo(merged, x2, w_o.astype(BF16), row(g_ffn),
                                jnp.transpose(w_router).astype(F32), tiles["tm_wo"])
    eidx, wts, rnk, cnt = _route(logits_t, router_bias.reshape(N_EXPERTS, 1).astype(F32),
                                 tiles["tm_route"])
    tmf = tiles["tm_ffn"]
    counts = cnt[:, 0].astype(I32)
    row_tok, row_dst, tile_e, n_used = _moe_tables(eidx, rnk, counts, tmf, tiles["tm_slots"])
    ys = _ffn(tile_e, n_used, row_tok, row_dst, h2p, w_exp_gate, w_exp_up, w_exp_down, tmf)
    x3 = _combine(ys, jnp.transpose(wts), x1, h2,
                  w_sh_gate.astype(BF16), w_sh_up.astype(BF16), w_sh_down.astype(BF16),
                  tiles["tm_comb"])

    out = _ple(x3, p.reshape(T, PLE_DIM), w_ple.astype(BF16), row(g_ple_post), row(g_ple_in),
               w_ple_gate.astype(BF16), tiles["tm_ple"])
    return out.reshape(B, S, D)


TILES = dict(tm_in=1024, tn_in=512, tm_prep=256, rows_scan=128, tm_mla=1024, tq=1024,
             tm_merge=1024, tn_merge=512, tm_wo=512, tm_route=512, tm_ffn=256,
             tm_slots=2048, tm_comb=128, tm_ple=512)


def kernel(x, p, positions, g_mix, w_in, mu_rwkv, w0, w2, a0, a2, g2, k_k, k_a, r_k, gn_w, gn_b, w_a_up, g_qa, g_kva, w_uq, w_ukv, g_qn, g_kn, w_b_up, w_o, g_ffn, w_router, router_bias, w_exp_gate, w_exp_up, w_exp_down, w_sh_gate, w_sh_up, w_sh_down, w_ple, g_ple_post, g_ple_in, w_ple_gate):
    args = (g_mix, w_in, mu_rwkv, w0, w2, a0, a2, g2, k_k, k_a, r_k, gn_w, gn_b, w_a_up,
            g_qa, g_kva, w_uq, w_ukv, g_qn, g_kn, w_b_up, w_o, g_ffn, w_router, router_bias,
            w_exp_gate, w_exp_up, w_exp_down, w_sh_gate, w_sh_up, w_sh_down, w_ple,
            g_ple_post, g_ple_in, w_ple_gate)
    assert all(t.shape[0] == 1 for t in args), "single-layer stack expected"
    return _layer(x, p[0], positions, *[t[0] for t in args], tiles=TILES)
```

```python
import functools

import jax
import jax.numpy as jnp
from jax import lax
from jax.experimental import pallas as pl
from jax.experimental.pallas import tpu as pltpu

F32 = jnp.float32
BF16 = jnp.bfloat16
I32 = jnp.int32
U32 = jnp.uint32

D_MODEL = 2048
PLE_DIM = 256
NORM_EPS = 1e-6
RWKV_HEADS = 16
RWKV_HEAD = 64
RWKV_WIDTH = RWKV_HEADS * RWKV_HEAD
DECAY_LORA = 64
AAA_LORA = 64
GATE_LORA = 160
GN_EPS = 64e-5
MLA_HEADS = 8
Q_LORA = 512
KV_LORA = 512
QK_NOPE = 128
QK_ROPE = 64
QK_HEAD = QK_NOPE + QK_ROPE
V_HEAD = 128
MLA_WIDTH = MLA_HEADS * V_HEAD
ROPE_THETA = 10000.0
N_EXPERTS = 64
TOP_K = 8
N_GROUPS = 8
GROUP_SIZE = N_EXPERTS // N_GROUPS
TOPK_GROUPS = 4
MOE_INTER = 512
ROUTED_SCALE = 2.5

LANES = 128
QK_PAD = 2 * LANES
PACK_ROWS = 8
PACK_W = D_MODEL // 2 // PACK_ROWS
assert PACK_W == LANES

LORA_PAD = 128
GATE_LORA_PAD = 256
RWKV_PAD = 3 * RWKV_WIDTH + 2 * LORA_PAD + GATE_LORA_PAD
MLA_PAD = 1536
COL_RWKV = 0
COL_MLA = RWKV_PAD
COL_GA = COL_MLA + MLA_PAD
COL_GB = COL_GA + D_MODEL
IN_PAD = COL_GB + D_MODEL

VMEM_LIMIT = 56 * 1024 * 1024
CHUNK = 64
WO_ROWS = 128
SCAN_LANE_BLOCKS = 8


def _cparams(sem):
    return pltpu.CompilerParams(dimension_semantics=sem, vmem_limit_bytes=VMEM_LIMIT)


def _dot(a, b):
    return jnp.dot(a, b, preferred_element_type=F32)


def _dot_nt(a, b):
    return lax.dot_general(a, b, (((1,), (1,)), ((), ())), preferred_element_type=F32)


def _dot_tn(a, b):
    return lax.dot_general(a, b, (((0,), (0,)), ((), ())), preferred_element_type=F32)


def _split_bf16(x):
    hi = x.astype(BF16)
    lo = (x - hi.astype(F32)).astype(BF16)
    return hi, lo


def _rms(x, g):
    ms = jnp.mean(x * x, axis=-1, keepdims=True)
    return x * lax.rsqrt(ms + NORM_EPS) * g


def _sigmoid(x):
    return 1.0 / (1.0 + jnp.exp(-x))


def _inproj_kernel(x_ref, g_ref, w_ref, o_ref, h_scr):
    @pl.when(pl.program_id(1) == 0)
    def _():
        h_scr[...] = _rms(x_ref[...], g_ref[...]).astype(BF16)

    o_ref[...] = _dot(h_scr[...], w_ref[...]).astype(o_ref.dtype)


def _inproj(x2, g, w, tm, tn):
    T, D = x2.shape
    N = w.shape[1]
    return pl.pallas_call(
        _inproj_kernel,
        grid=(T // tm, N // tn),
        in_specs=[
            pl.BlockSpec((tm, D), lambda i, j: (i, 0)),
            pl.BlockSpec((1, D), lambda i, j: (0, 0)),
            pl.BlockSpec((D, tn), lambda i, j: (0, j)),
        ],
        out_specs=pl.BlockSpec((tm, tn), lambda i, j: (i, j)),
        out_shape=jax.ShapeDtypeStruct((T, N), BF16),
        scratch_shapes=[pltpu.VMEM((tm, D), BF16)],
        compiler_params=_cparams(("parallel", "arbitrary")),
        name="inproj",
    )(x2, g, w)


HALO = 16


def _rwkv_prep_kernel(cur_ref, halo_ref, mu_ref, w0_ref, w2_ref, a0_ref, a2_ref, g2_ref,
                      kk_ref, ka_ref, bd_ref,
                      r_out, k_out, v_out, kk_out, a_out, lw_out, g_out, *, tm, seq):
    i = pl.program_id(0)
    cur = cur_ref[...].astype(F32)
    first = (i * tm) % seq == 0
    last_prev = halo_ref[HALO - 1:HALO, :].astype(F32)
    last_prev = jnp.where(first, 0.0, last_prev)
    row = lax.broadcasted_iota(I32, cur.shape, 0)
    prev = jnp.where(row == 0, last_prev, pltpu.roll(cur, 1, 0))
    u = cur + (prev - cur) * mu_ref[...]
    W = RWKV_WIDTH
    r = u[:, 0:W]
    k = u[:, W:2 * W]
    v = u[:, 2 * W:3 * W]
    dw = u[:, 3 * W:3 * W + LORA_PAD]
    da = u[:, 3 * W + LORA_PAD:3 * W + 2 * LORA_PAD]
    dg = u[:, 3 * W + 2 * LORA_PAD:3 * W + 2 * LORA_PAD + GATE_LORA_PAD]
    z = w0_ref[...] + _dot(jnp.tanh(dw).astype(BF16), w2_ref[...])
    w_log = -(jnp.maximum(-z, 0.0) + jnp.log(1.0 + jnp.exp(-jnp.abs(z)))) - 0.5
    lw_out[...] = -jnp.exp(w_log)
    a = _sigmoid(a0_ref[...] + _dot(da.astype(BF16), a2_ref[...]))
    g_out[...] = _dot(_sigmoid(dg).astype(BF16), g2_ref[...]).astype(g_out.dtype)
    kk = k * kk_ref[...]
    hi, lo = _split_bf16(kk * kk)
    ssq = _dot(hi, bd_ref[...]) + _dot(lo, bd_ref[...])
    kk = kk / jnp.maximum(jnp.sqrt(ssq), 1e-12)
    r_out[...] = r.astype(r_out.dtype)
    k_out[...] = (k * (1.0 + (a - 1.0) * ka_ref[...])).astype(k_out.dtype)
    v_out[...] = v.astype(v_out.dtype)
    kk_out[...] = kk.astype(kk_out.dtype)
    a_out[...] = a.astype(a_out.dtype)


def _rwkv_prep(proj, mu, w0, w2, a0, a2, g2, k_k, k_a, bd, tm, seq):
    T = proj.shape[0]
    W = RWKV_WIDTH
    full = lambda shape: pl.BlockSpec(shape, lambda i: (0, 0))
    out = lambda: pl.BlockSpec((tm, W), lambda i: (i, 0))
    return pl.pallas_call(
        functools.partial(_rwkv_prep_kernel, tm=tm, seq=seq),
        grid=(T // tm,),
        in_specs=[
            pl.BlockSpec((tm, RWKV_PAD), lambda i: (i, 0)),
            pl.BlockSpec((HALO, RWKV_PAD), lambda i: (jnp.maximum(i * (tm // HALO) - 1, 0), 0)),
            full((1, RWKV_PAD)), full((1, W)), full((LORA_PAD, W)), full((1, W)),
            full((LORA_PAD, W)), full((GATE_LORA_PAD, W)), full((1, W)), full((1, W)),
            full((W, W)),
        ],
        out_specs=[out() for _ in range(7)],
        out_shape=[jax.ShapeDtypeStruct((T, W), dt)
                   for dt in (BF16, BF16, BF16, BF16, BF16, F32, BF16)],
        compiler_params=_cparams(("parallel",)),
        name="rwkv_prep",
    )(proj, proj, mu, w0, w2, a0, a2, g2, k_k, k_a, bd)


def _rwkv_scan_kernel(r_ref, k_ref, v_ref, kk_ref, a_ref, lw_ref, g_ref,
                      gnw_ref, gnb_ref, rk_ref, o_ref, s_scr, *, rows):
    L = CHUNK
    L2 = 2 * L

    @pl.when(pl.program_id(2) == 0)
    def _():
        s_scr[...] = jnp.zeros_like(s_scr)

    lane = lax.broadcasted_iota(I32, (L, LANES), 1)
    m_a = (lane < RWKV_HEAD).astype(F32)
    m_b = 1.0 - m_a
    ri = lax.broadcasted_iota(I32, (L2, L2), 0)
    ci = lax.broadcasted_iota(I32, (L2, L2), 1)
    strict = ri > ci
    incl = ri >= ci
    eye = (ri == ci).astype(F32)
    tri = (lax.broadcasted_iota(I32, (L, L), 0) >= lax.broadcasted_iota(I32, (L, L), 1)).astype(BF16)
    hr = lax.broadcasted_iota(I32, (LANES, LANES), 0) // RWKV_HEAD
    hc = lax.broadcasted_iota(I32, (LANES, LANES), 1) // RWKV_HEAD
    head_ones = (hr == hc).astype(BF16)
    nb = SCAN_LANE_BLOCKS

    def stack(x):
        return jnp.concatenate([x * m_a, x * m_b], axis=0)

    chunks = range(rows // L)
    rows_of = lambda c: slice(c * L, (c + 1) * L)
    lanes_of = lambda h: slice(h * LANES, (h + 1) * LANES)
    each = lambda f, *lists: [f(*vals) for vals in zip(*lists)]

    def operands(c, h):
        sl, ln = rows_of(c), lanes_of(h)
        r = r_ref[0, sl, ln].astype(F32)
        k = k_ref[0, sl, ln].astype(F32)
        v = v_ref[0, sl, ln].astype(F32)
        kk = kk_ref[0, sl, ln].astype(F32)
        a = a_ref[0, sl, ln].astype(F32)
        lw = lw_ref[0, sl, ln]
        hi, lo = _split_bf16(lw)
        cum = _dot(tri, hi) + _dot(tri, lo)
        c_end = cum[L - 1:L, :]
        p_inv = jnp.exp(-cum)
        p_end = jnp.exp(c_end - cum)
        b = kk * a
        return dict(
            x_a=stack(-(kk * jnp.exp(cum - lw))).astype(BF16),
            x_r=stack(r * jnp.exp(cum)),
            x_bk=jnp.concatenate([stack(b * p_inv), stack(k * p_inv)], axis=0).astype(BF16),
            v_st=stack(v).astype(BF16),
            z_hat=jnp.concatenate([stack(b * p_end), stack(k * p_end)], axis=0).astype(BF16),
            p_row=jnp.exp(c_end),
            rkr=(r * k * rk_ref[:, ln]).astype(BF16), v=v)

    ops = [operands(c, h) for c in chunks for h in range(nb)]
    big = [_dot_nt(jnp.concatenate([o["x_a"], o["x_r"].astype(BF16)], axis=0), o["x_bk"])
           for o in ops]
    n_pow = [jnp.where(strict, m[0:L2, 0:L2], 0.0) for m in big]
    a_ak = [jnp.where(strict, m[0:L2, L2:2 * L2], 0.0).astype(BF16) for m in big]
    a_r = [jnp.concatenate([jnp.where(incl, m[L2:2 * L2, 0:L2], 0.0),
                            jnp.where(incl, m[L2:2 * L2, L2:2 * L2], 0.0)], axis=1).astype(BF16)
           for m in big]

    t_inv = [eye + n for n in n_pow]
    for _ in range(5):
        n_pow = each(lambda n: _dot(n.astype(BF16), n.astype(BF16)), n_pow)
        t_inv = each(lambda t, n: t + _dot(t.astype(BF16), n.astype(BF16)), t_inv, n_pow)

    akv = each(lambda m, o: _dot(m, o["v_st"]).astype(BF16), a_ak, ops)
    w = each(lambda t, o, x: _dot(t.astype(BF16), jnp.concatenate([o["x_a"], x], axis=1)),
             t_inv, ops, akv)
    w1 = [m[:, 0:LANES].astype(BF16) for m in w]
    w2v = each(lambda m, o: jnp.concatenate([m[:, LANES:].astype(BF16), o["v_st"]], axis=0), w, ops)
    g_mat = each(lambda o, m, x: (o["x_r"] + _dot(m[:, 0:L2], x)).astype(BF16), ops, a_r, w1)
    y0 = each(_dot, a_r, w2v)
    m_mat = each(lambda x, o: _dot_tn(x, o["z_hat"][0:L2, :]).astype(BF16), w1, ops)
    c2 = each(lambda x, o: _dot_tn(x, o["z_hat"]), w2v, ops)
    bonus_v = [_dot(o["rkr"], head_ones) * o["v"] for o in ops]

    s = [s_scr[h] for h in range(nb)]
    y = []
    for i in range(len(ops)):
        h = i % nb
        s_b = s[h].astype(BF16)
        y_st = _dot_nt(g_mat[i], s_b) + y0[i]
        s[h] = s[h] * ops[i]["p_row"] + _dot(s_b, m_mat[i]) + c2[i]
        y.append(y_st[0:L, :] + y_st[L:L2, :])
    for h in range(nb):
        s_scr[h] = s[h]

    mean = [_dot(t.astype(BF16), head_ones) * (1.0 / RWKV_HEAD) for t in y]
    yc = each(lambda t, m: t - m, y, mean)
    var = [_dot((t * t).astype(BF16), head_ones) * (1.0 / RWKV_HEAD) for t in yc]
    for i in range(len(ops)):
        sl, ln = rows_of(i // nb), lanes_of(i % nb)
        yn = yc[i] * lax.rsqrt(var[i] + GN_EPS) * gnw_ref[:, ln] + gnb_ref[:, ln]
        o_ref[0, sl, ln] = ((yn + bonus_v[i]) * g_ref[0, sl, ln].astype(F32)).astype(o_ref.dtype)


def _rwkv_scan(r, k, v, kk, a, lw, g, gn_w, gn_b, r_k, rows):
    B, S, W = r.shape
    wb = SCAN_LANE_BLOCKS * LANES
    npair = W // wb
    seq = lambda: pl.BlockSpec((1, rows, wb), lambda b, p, c: (b, c, p))
    par = lambda: pl.BlockSpec((1, wb), lambda b, p, c: (0, p))
    return pl.pallas_call(
        functools.partial(_rwkv_scan_kernel, rows=rows),
        grid=(B, npair, S // rows),
        in_specs=[seq() for _ in range(7)] + [par(), par(), par()],
        out_specs=seq(),
        out_shape=jax.ShapeDtypeStruct((B, S, W), BF16),
        scratch_shapes=[pltpu.VMEM((SCAN_LANE_BLOCKS, LANES, LANES), F32)],
        compiler_params=_cparams(("parallel", "parallel", "arbitrary")),
        name="rwkv_scan",
    )(r, k, v, kk, a, lw, g, gn_w, gn_b, r_k)


def _rope128(x, cos, sin_signed):
    lane = lax.broadcasted_iota(I32, x.shape, 1)
    half = QK_ROPE // 2
    partner = jnp.where(lane < half, pltpu.roll(x, LANES - half, 1), pltpu.roll(x, half, 1))
    return x * cos + partner * sin_signed


def _mla_prep_kernel(cq_ref, ckv_ref, kr_ref, pos_ref, invf_ref, gqa_ref, gkva_ref,
                     wuq_ref, wukv_ref, gqn_ref, gkn_ref,
                     q_out, k_out, v_out, cqn_scr, ckvn_scr, cos_scr, sin_scr):
    @pl.when(pl.program_id(1) == 0)
    def _():
        cqn_scr[...] = _rms(cq_ref[...].astype(F32), gqa_ref[...]).astype(BF16)
        ckvn_scr[...] = _rms(ckv_ref[...].astype(F32), gkva_ref[...]).astype(BF16)
        ang = pos_ref[...].astype(F32) * invf_ref[...]
        lane = lax.broadcasted_iota(I32, ang.shape, 1)
        cos_scr[...] = jnp.cos(ang)
        sin_scr[...] = jnp.where(lane < QK_ROPE // 2, -1.0, 1.0) * jnp.sin(ang)

    cos = cos_scr[...]
    sin = sin_scr[...]
    inv_n = 1.0 / QK_HEAD
    q = _dot(cqn_scr[...], wuq_ref[...])
    rs = lax.rsqrt(jnp.sum(q * q, axis=-1, keepdims=True) * inv_n + NORM_EPS)
    q = q * rs * gqn_ref[...] * (QK_HEAD ** -0.5 * LOG2E)
    q_out[...] = jnp.concatenate(
        [q[:, 0:LANES], _rope128(q[:, LANES:], cos, sin)], axis=1).astype(q_out.dtype)

    kv = _dot(ckvn_scr[...], wukv_ref[...])
    k_nope = kv[:, 0:QK_NOPE]
    kr = kr_ref[...].astype(F32)
    ssq = (jnp.sum(k_nope * k_nope, axis=-1, keepdims=True)
           + jnp.sum(kr * kr, axis=-1, keepdims=True))
    rs = lax.rsqrt(ssq * inv_n + NORM_EPS)
    gkn = gkn_ref[...]
    k_out[...] = jnp.concatenate(
        [k_nope * rs * gkn[:, 0:LANES], _rope128(kr * rs * gkn[:, LANES:], cos, sin)],
        axis=1).astype(k_out.dtype)
    v_out[...] = jnp.transpose(kv[:, QK_NOPE:].astype(v_out.dtype))


def _mla_prep(proj, pos, invf, g_qa, g_kva, wuq, wukv, gqn, gkn, tm):
    T = proj.shape[0]
    H = MLA_HEADS
    c0 = COL_MLA // Q_LORA
    ckr = (COL_MLA + Q_LORA + KV_LORA) // LANES
    full = lambda shape: pl.BlockSpec(shape, lambda i, h: (0, 0))
    return pl.pallas_call(
        _mla_prep_kernel,
        grid=(T // tm, H),
        in_specs=[
            pl.BlockSpec((tm, Q_LORA), lambda i, h: (i, c0)),
            pl.BlockSpec((tm, KV_LORA), lambda i, h: (i, c0 + 1)),
            pl.BlockSpec((tm, LANES), lambda i, h: (i, ckr)),
            pl.BlockSpec((tm, 1), lambda i, h: (i, 0)),
            full((1, LANES)), full((1, Q_LORA)), full((1, KV_LORA)),
            pl.BlockSpec((Q_LORA, QK_PAD), lambda i, h: (0, h)),
            pl.BlockSpec((KV_LORA, QK_NOPE + V_HEAD), lambda i, h: (0, h)),
            full((1, QK_PAD)), full((1, QK_PAD)),
        ],
        out_specs=[
            pl.BlockSpec((tm, QK_PAD), lambda i, h: (i, h)),
            pl.BlockSpec((tm, QK_PAD), lambda i, h: (i, h)),
            pl.BlockSpec((V_HEAD, tm), lambda i, h: (h, i)),
        ],
        out_shape=[jax.ShapeDtypeStruct((T, H * QK_PAD), BF16),
                   jax.ShapeDtypeStruct((T, H * QK_PAD), BF16),
                   jax.ShapeDtypeStruct((H * V_HEAD, T), BF16)],
        scratch_shapes=[pltpu.VMEM((tm, Q_LORA), BF16), pltpu.VMEM((tm, KV_LORA), BF16),
                        pltpu.VMEM((tm, LANES), F32), pltpu.VMEM((tm, LANES), F32)],
        compiler_params=_cparams(("parallel", "arbitrary")),
        name="mla_prep",
    )(proj, proj, proj, pos, invf, g_qa, g_kva, wuq, wukv, gqn, gkn)


MASK_VALUE = -1e30
LOG2E = 1.4426950408889634


def _flash_kernel(q_ref, k_ref, vt_ref, o_ref, s_scr, p_scr, alpha_scr, m_scr, l_scr, acc_scr,
                  *, tq):
    i = pl.program_id(2)
    q = q_ref[0]
    m_scr[...] = jnp.full_like(m_scr, MASK_VALUE)
    l_scr[...] = jnp.zeros_like(l_scr)
    acc_scr[...] = jnp.zeros_like(acc_scr)

    q_half = tq // 2

    def scores(j, slot, half):
        ks = pl.ds(pl.multiple_of(j * tq, tq), tq)
        cols = slice(half * q_half, (half + 1) * q_half)
        s_scr[slot, :, cols] = _dot_nt(k_ref[0, ks, :], q[cols, :])

    def softmax(slot, masked, strips):
        for c in strips:
            cols = slice(c * LANES, (c + 1) * LANES)
            s = s_scr[slot, :, cols]
            if masked:
                kpos = lax.broadcasted_iota(I32, s.shape, 0)
                qpos = lax.broadcasted_iota(I32, s.shape, 1) + c * LANES
                s = jnp.where(kpos <= qpos, s, MASK_VALUE)
            m_old = m_scr[:, cols]
            m_new = jnp.maximum(m_old, jnp.max(s, axis=0, keepdims=True))
            p = jnp.exp2(s - m_new)
            alpha = jnp.exp2(m_old - m_new)
            l_scr[:, cols] = alpha * l_scr[:, cols] + jnp.sum(p, axis=0, keepdims=True)
            m_scr[:, cols] = m_new
            p_scr[slot, :, cols] = p.astype(BF16)
            alpha_scr[slot, :, cols] = alpha

    def weighted_values(j, slot):
        vt = vt_ref[:, pl.ds(pl.multiple_of(j * tq, tq), tq)]
        acc_scr[...] = alpha_scr[slot] * acc_scr[...] + _dot(vt, p_scr[slot])

    scores(0, 0, 0)
    scores(0, 0, 1)
    p_scr[1] = jnp.zeros((tq, tq), BF16)
    alpha_scr[1] = jnp.ones((1, tq), F32)
    n_strips = tq // LANES
    first, second = range(0, n_strips // 2), range(n_strips // 2, n_strips)

    def step(j, slot):
        weighted_values(jnp.maximum(j - 1, 0), 1 - slot)
        scores(j + 1, 1 - slot, 0)
        softmax(slot, False, first)
        scores(j + 1, 1 - slot, 1)
        softmax(slot, False, second)

    def pair(jp, carry):
        step(2 * jp, 0)
        step(2 * jp + 1, 1)
        return carry

    def finish(slot):
        weighted_values(jnp.maximum(i - 1, 0), 1 - slot)
        softmax(slot, True, range(n_strips))
        weighted_values(i, slot)
        o_ref[0] = jnp.transpose(acc_scr[...] / l_scr[...]).astype(o_ref.dtype)

    lax.fori_loop(0, i // 2, pair, 0)

    @pl.when(i % 2 == 0)
    def _():
        finish(0)

    @pl.when(i % 2 == 1)
    def _():
        step(i - 1, 0)
        finish(1)


def _flash(q, k, vt, tq):
    B, S, _ = q.shape
    H = MLA_HEADS
    return pl.pallas_call(
        functools.partial(_flash_kernel, tq=tq),
        grid=(B, H, S // tq),
        in_specs=[
            pl.BlockSpec((1, tq, QK_PAD), lambda b, h, i: (b, i, h)),
            pl.BlockSpec((1, S, QK_PAD), lambda b, h, i: (b, 0, h)),
            pl.BlockSpec((V_HEAD, S), lambda b, h, i: (h, b)),
        ],
        out_specs=pl.BlockSpec((1, tq, V_HEAD), lambda b, h, i: (b, i, h)),
        out_shape=jax.ShapeDtypeStruct((B, S, H * V_HEAD), BF16),
        scratch_shapes=[pltpu.VMEM((2, tq, tq), F32), pltpu.VMEM((2, tq, tq), BF16),
                        pltpu.VMEM((2, 1, tq), F32),
                        pltpu.VMEM((1, tq), F32), pltpu.VMEM((1, tq), F32),
                        pltpu.VMEM((V_HEAD, tq), F32)],
        compiler_params=_cparams(("parallel", "parallel", "arbitrary")),
        name="flash_attn",
    )(q, k, vt)


def _merge_kernel(oa_ref, ob_ref, wa_ref, wb_ref, ga_ref, gb_ref, o_ref):
    ya = _dot(oa_ref[...], wa_ref[...])
    yb = _dot(ob_ref[...], wb_ref[...])
    ga = _sigmoid(ga_ref[...].astype(F32))
    gb = _sigmoid(gb_ref[...].astype(F32))
    o_ref[...] = (ga * ya + gb * yb).astype(o_ref.dtype)


def _merge(oa, ob, wa, wb, proj, tm, tn):
    T, K = oa.shape
    N = wa.shape[1]
    ca = COL_GA // tn
    cb = COL_GB // tn
    return pl.pallas_call(
        _merge_kernel,
        grid=(T // tm, N // tn),
        in_specs=[
            pl.BlockSpec((tm, K), lambda i, j: (i, 0)),
            pl.BlockSpec((tm, K), lambda i, j: (i, 0)),
            pl.BlockSpec((K, tn), lambda i, j: (0, j)),
            pl.BlockSpec((K, tn), lambda i, j: (0, j)),
            pl.BlockSpec((tm, tn), lambda i, j: (i, ca + j)),
            pl.BlockSpec((tm, tn), lambda i, j: (i, cb + j)),
        ],
        out_specs=pl.BlockSpec((tm, tn), lambda i, j: (i, j)),
        out_shape=jax.ShapeDtypeStruct((T, N), BF16),
        compiler_params=_cparams(("parallel", "parallel")),
        name="merge",
    )(oa, ob, wa, wb, proj, proj)


def _pack_rows(x, out_ref, n, row0=0):
    for c in range(PACK_ROWS):
        lo = x[:, c * 2 * LANES:c * 2 * LANES + LANES]
        hi = x[:, c * 2 * LANES + LANES:(c + 1) * 2 * LANES]
        lo_b = pltpu.bitcast(lo.astype(BF16).astype(F32), U32)
        hi_b = pltpu.bitcast(hi.astype(BF16).astype(F32), U32)
        out_ref[pl.ds(row0 * PACK_ROWS + c, n, stride=PACK_ROWS), :] = hi_b | (lo_b >> 16)


def _unpack_rows(ref, c, tm, lead=None):
    idx = (pl.ds(c, tm, stride=PACK_ROWS), slice(None))
    if lead is not None:
        idx = (lead,) + idx
    u = ref[idx]
    lo = pltpu.bitcast(u << 16, F32)
    hi = pltpu.bitcast(u & jnp.uint32(0xFFFF0000), F32)
    return lo, hi


def _wo_kernel(m_ref, x_ref, wo_ref, g_ref, wr_ref, x1_ref, h2_ref, h2p_ref, lg_ref, *, tm):
    hw = WO_ROWS
    groups = [slice(h * hw, (h + 1) * hw) for h in range(tm // hw)]
    wo = wo_ref[...]
    x1 = [x_ref[r, :] + _dot(m_ref[r, :], wo) for r in groups]
    h2 = [_rms(v, g_ref[...]) for v in x1]
    w_hi, w_lo = _split_bf16(wr_ref[...])
    for h, r in enumerate(groups):
        x1_ref[r, :] = x1[h]
        h2_ref[r, :] = h2[h].astype(BF16)
        _pack_rows(h2[h], h2p_ref, hw, row0=h * hw)
        h_hi, h_lo = _split_bf16(h2[h])
        lg_ref[:, r] = _dot_nt(w_hi, h_hi) + _dot_nt(w_hi, h_lo) + _dot_nt(w_lo, h_hi)


def _wo(merged, x2, wo, g, wr_t, tm):
    T, D = x2.shape
    E = wr_t.shape[0]
    return pl.pallas_call(
        functools.partial(_wo_kernel, tm=tm),
        grid=(T // tm,),
        in_specs=[
            pl.BlockSpec((tm, D), lambda i: (i, 0)),
            pl.BlockSpec((tm, D), lambda i: (i, 0)),
            pl.BlockSpec((D, D), lambda i: (0, 0), pipeline_mode=pl.Buffered(1)),
            pl.BlockSpec((1, D), lambda i: (0, 0)),
            pl.BlockSpec((E, D), lambda i: (0, 0)),
        ],
        out_specs=[
            pl.BlockSpec((tm, D), lambda i: (i, 0)),
            pl.BlockSpec((tm, D), lambda i: (i, 0)),
            pl.BlockSpec((tm * PACK_ROWS, PACK_W), lambda i: (i, 0)),
            pl.BlockSpec((E, tm), lambda i: (0, i)),
        ],
        out_shape=[jax.ShapeDtypeStruct((T, D), F32),
                   jax.ShapeDtypeStruct((T, D), BF16),
                   jax.ShapeDtypeStruct((T * PACK_ROWS, PACK_W), U32),
                   jax.ShapeDtypeStruct((E, T), F32)],
        compiler_params=_cparams(("parallel",)),
        name="wo_norm_router",
    )(merged, x2, wo, g, wr_t)


def _route_kernel(lg_ref, bias_ref, eidx_ref, wts_ref, rnk_ref, cnt_ref, carry_scr, *, tm):
    E = N_EXPERTS
    NEG = -jnp.inf

    @pl.when(pl.program_id(0) == 0)
    def _():
        carry_scr[...] = jnp.zeros_like(carry_scr)

    scores = _sigmoid(lg_ref[...])
    choice = scores + bias_ref[...]
    c3 = choice.reshape(N_GROUPS, GROUP_SIZE, tm)
    sub = lax.broadcasted_iota(I32, c3.shape, 1)
    m1 = jnp.max(c3, axis=1, keepdims=True)
    i1 = jnp.min(jnp.where(c3 == m1, sub, GROUP_SIZE), axis=1, keepdims=True)
    m2 = jnp.max(jnp.where(sub == i1, NEG, c3), axis=1, keepdims=True)
    gs = (m1 + m2).reshape(N_GROUPS, tm)
    gi = lax.broadcasted_iota(I32, gs.shape, 0)
    gsel = jnp.zeros(gs.shape, F32)
    for _ in range(TOPK_GROUPS):
        mx = jnp.max(gs, axis=0, keepdims=True)
        ix = jnp.min(jnp.where(gs == mx, gi, N_GROUPS), axis=0, keepdims=True)
        hit = gi == ix
        gsel = jnp.where(hit, 1.0, gsel)
        gs = jnp.where(hit, NEG, gs)
    emask = jnp.broadcast_to(gsel.reshape(N_GROUPS, 1, tm), (N_GROUPS, GROUP_SIZE, tm)).reshape(E, tm)
    x = jnp.where(emask > 0.5, choice, NEG)
    ei = lax.broadcasted_iota(I32, x.shape, 0)
    sel = jnp.zeros(x.shape, F32)
    idx_rows, w_rows = [], []
    for _ in range(TOP_K):
        mx = jnp.max(x, axis=0, keepdims=True)
        ix = jnp.min(jnp.where(x == mx, ei, E), axis=0, keepdims=True)
        hit = ei == ix
        w_rows.append(jnp.sum(jnp.where(hit, scores, 0.0), axis=0, keepdims=True))
        idx_rows.append(ix)
        sel = jnp.where(hit, 1.0, sel)
        x = jnp.where(hit, NEG, x)
    w = jnp.concatenate(w_rows, axis=0)
    w = w / (jnp.sum(w, axis=0, keepdims=True) + 1e-20) * ROUTED_SCALE
    eidx_ref[...] = jnp.concatenate(idx_rows, axis=0)
    wts_ref[...] = w

    upper = (lax.broadcasted_iota(I32, (tm, tm), 0) < lax.broadcasted_iota(I32, (tm, tm), 1)).astype(BF16)
    base = carry_scr[...][:, 0:1]
    excl = _dot(sel.astype(BF16), upper) + base
    rnk_ref[...] = jnp.concatenate(
        [jnp.sum(jnp.where(ei == ix, excl, 0.0), axis=0, keepdims=True) for ix in idx_rows],
        axis=0).astype(I32)
    carry_scr[...] = carry_scr[...] + jnp.sum(sel, axis=1, keepdims=True)
    cnt_ref[...] = carry_scr[...]


def _route(logits_t, bias, tm):
    E, T = logits_t.shape
    K = TOP_K
    tok = lambda: pl.BlockSpec((K, tm), lambda i: (0, i))
    return pl.pallas_call(
        functools.partial(_route_kernel, tm=tm),
        grid=(T // tm,),
        in_specs=[pl.BlockSpec((E, tm), lambda i: (0, i)),
                  pl.BlockSpec((E, 1), lambda i: (0, 0))],
        out_specs=[tok(), tok(), tok(), pl.BlockSpec((E, LANES), lambda i: (0, 0))],
        out_shape=[jax.ShapeDtypeStruct((K, T), I32), jax.ShapeDtypeStruct((K, T), F32),
                   jax.ShapeDtypeStruct((K, T), I32), jax.ShapeDtypeStruct((E, LANES), F32)],
        scratch_shapes=[pltpu.VMEM((E, LANES), F32)],
        compiler_params=_cparams(("arbitrary",)),
        name="route",
    )(logits_t, bias)


def _slots_kernel(pst_ref, eidx_ref, rnk_ref, dest_ref):
    eidx = eidx_ref[...]
    dest = rnk_ref[...]
    for e in range(N_EXPERTS):
        dest = dest + jnp.where(eidx == e, pst_ref[e], 0)
    dest_ref[...] = dest


def _slots(eidx, rnk, pstart, tm):
    K, T = eidx.shape
    tok = lambda: pl.BlockSpec((K, tm), lambda i: (0, i))
    return pl.pallas_call(
        _slots_kernel,
        grid=(T // tm,),
        in_specs=[pl.BlockSpec(memory_space=pltpu.SMEM), tok(), tok()],
        out_specs=tok(),
        out_shape=jax.ShapeDtypeStruct((K, T), I32),
        compiler_params=_cparams(("parallel",)),
        name="slots",
    )(pstart, eidx, rnk)


def _ffn_kernel(te_ref, nu_ref, tok_ref, dst_ref, h_hbm, wg_ref, wu_ref, wd_ref, ys_hbm,
                gbuf, obuf, wg_b, wu_b, wd_b, gsem, osem, *, tm):
    s = pl.program_id(0)
    nu = nu_ref[0]
    tile_rows = tm * PACK_ROWS

    def gather(slot):
        for r in range(tm):
            pltpu.make_async_copy(
                h_hbm.at[pl.ds(pl.multiple_of(tok_ref[0, 0, r] * PACK_ROWS, PACK_ROWS), PACK_ROWS), :],
                gbuf.at[slot, pl.ds(r * PACK_ROWS, PACK_ROWS), :], gsem.at[slot]).start()

    def scatter(slot):
        for r in range(tm):
            pltpu.make_async_copy(
                obuf.at[slot, pl.ds(r * PACK_ROWS, PACK_ROWS), :],
                ys_hbm.at[pl.ds(pl.multiple_of(dst_ref[0, 0, r] * PACK_ROWS, PACK_ROWS), PACK_ROWS), :],
                osem.at[slot]).start()

    def wait_gather(slot):
        pltpu.make_async_copy(h_hbm.at[pl.ds(0, tile_rows), :], gbuf.at[slot], gsem.at[slot]).wait()

    def wait_scatter(slot):
        pltpu.make_async_copy(obuf.at[slot], ys_hbm.at[pl.ds(0, tile_rows), :], osem.at[slot]).wait()

    def compute(slot):
        parts = []
        for c in range(PACK_ROWS):
            lo, hi = _unpack_rows(gbuf.at[slot], c, tm)
            parts += [lo.astype(BF16), hi.astype(BF16)]
        a = jnp.concatenate(parts, axis=1)
        hg = _dot(a, wg_b[...])
        hu = _dot(a, wu_b[...])
        hid = (hg * _sigmoid(hg) * hu).astype(BF16)
        _pack_rows(_dot(hid, wd_b[...]), obuf.at[slot], tm)

    cur = jnp.maximum(s - 1, 0)
    @pl.when((s >= 1) & (s <= nu) & ((s == 1) | (te_ref[cur] != te_ref[jnp.maximum(s - 2, 0)])))
    def _():
        wg_b[...] = wg_ref[0].astype(BF16)
        wu_b[...] = wu_ref[0].astype(BF16)
        wd_b[...] = wd_ref[0].astype(BF16)

    @pl.when(s == 0)
    def _():
        gather(0)

    @pl.when(s == 1)
    def _():
        wait_gather(0)
        gather(1)
        compute(0)

    for slot in range(2):
        parity = s % 2 == slot

        @pl.when(parity & (s >= 3) & (s <= nu + 1))
        def _():
            wait_scatter(1 - slot)

        @pl.when(parity & (s >= 2) & (s <= nu))
        def _():
            wait_gather(1 - slot)
            gather(slot)
            scatter(slot)
            compute(1 - slot)

        @pl.when(parity & (s >= 2) & (s == nu + 1))
        def _():
            wait_gather(1 - slot)
            scatter(slot)
            wait_scatter(slot)


def _ffn(tile_e, n_used, row_tok, row_dst, h2p, wg, wu, wd, tm):
    nt = row_tok.shape[0]
    D, I = wg.shape[1], wg.shape[2]
    cur = lambda s: jnp.maximum(s - 1, 0)
    wspec = lambda shape: pl.BlockSpec(shape, lambda s, te, nu: (te[cur(s)], 0, 0))
    grid_spec = pltpu.PrefetchScalarGridSpec(
        num_scalar_prefetch=2,
        grid=(nt + 1,),
        in_specs=[
            pl.BlockSpec((1, 1, tm), lambda s, te, nu: (jnp.minimum(s, nt - 1), 0, 0),
                         memory_space=pltpu.SMEM),
            pl.BlockSpec((1, 1, tm), lambda s, te, nu: (jnp.maximum(s - 2, 0), 0, 0),
                         memory_space=pltpu.SMEM),
            pl.BlockSpec(memory_space=pl.ANY),
            wspec((1, D, I)), wspec((1, D, I)), wspec((1, I, D)),
        ],
        out_specs=pl.BlockSpec(memory_space=pl.ANY),
        scratch_shapes=[pltpu.VMEM((2, tm * PACK_ROWS, PACK_W), U32),
                        pltpu.VMEM((2, tm * PACK_ROWS, PACK_W), U32),
                        pltpu.VMEM((D, I), BF16), pltpu.VMEM((D, I), BF16), pltpu.VMEM((I, D), BF16),
                        pltpu.SemaphoreType.DMA((2,)), pltpu.SemaphoreType.DMA((2,))],
    )
    return pl.pallas_call(
        functools.partial(_ffn_kernel, tm=tm),
        grid_spec=grid_spec,
        out_shape=jax.ShapeDtypeStruct((nt * tm * PACK_ROWS, PACK_W), U32),
        compiler_params=_cparams(("arbitrary",)),
        name="expert_ffn",
    )(tile_e, n_used, row_tok, row_dst, h2p, wg, wu, wd)


def _combine_kernel(*refs, tm):
    ys_refs = refs[:TOP_K]
    wt_ref, x1_ref, h2_ref, wsg_ref, wsu_ref, wsd_ref, o_ref = refs[TOP_K:]
    h2 = h2_ref[...]
    hg = _dot(h2, wsg_ref[...])
    hu = _dot(h2, wsu_ref[...])
    base = x1_ref[...] + _dot((hg * _sigmoid(hg) * hu).astype(BF16), wsd_ref[...])
    wt = wt_ref[...]
    for c in range(PACK_ROWS):
        acc_lo = base[:, c * 2 * LANES:c * 2 * LANES + LANES]
        acc_hi = base[:, c * 2 * LANES + LANES:(c + 1) * 2 * LANES]
        for k in range(TOP_K):
            lo, hi = _unpack_rows(ys_refs[k], c, tm)
            wk = wt[:, k:k + 1]
            acc_lo = acc_lo + wk * lo
            acc_hi = acc_hi + wk * hi
        o_ref[:, c * 2 * LANES:c * 2 * LANES + LANES] = acc_lo
        o_ref[:, c * 2 * LANES + LANES:(c + 1) * 2 * LANES] = acc_hi


def _combine(ys, wts_t, x1, h2, wsg, wsu, wsd, tm):
    T, D = x1.shape
    K = wts_t.shape[1]
    I = wsg.shape[1]
    full = lambda shape: pl.BlockSpec(shape, lambda i: (0, 0))
    return pl.pallas_call(
        functools.partial(_combine_kernel, tm=tm),
        grid=(T // tm,),
        in_specs=[pl.BlockSpec((tm * PACK_ROWS, PACK_W), lambda i, k=k: (k * (T // tm) + i, 0))
                  for k in range(K)] + [
                  pl.BlockSpec((tm, K), lambda i: (i, 0)),
                  pl.BlockSpec((tm, D), lambda i: (i, 0)),
                  pl.BlockSpec((tm, D), lambda i: (i, 0)),
                  full((D, I)), full((D, I)), full((I, D))],
        out_specs=pl.BlockSpec((tm, D), lambda i: (i, 0)),
        out_shape=jax.ShapeDtypeStruct((T, D), F32),
        compiler_params=_cparams(("parallel",)),
        name="combine",
    )(*([ys] * K), wts_t, x1, h2, wsg, wsu, wsd)


def _ple_kernel(x_ref, p_ref, wp_ref, gpost_ref, gin_ref, wg_ref, o_ref, *, tm):
    hw = WO_ROWS
    groups = [slice(h * hw, (h + 1) * hw) for h in range(tm // hw)]
    wg = wg_ref[...]
    gates, ples = [], []
    for r in groups:
        hn = _rms(x_ref[r, :], gin_ref[...]).astype(BF16)
        gates.append(_dot(hn, wg))
        ples.append(_rms(_dot(p_ref[r, :].astype(BF16), wp_ref[...]), gpost_ref[...]))
    for r, gate, ple in zip(groups, gates, ples):
        o_ref[r, :] = x_ref[r, :] + _sigmoid(gate) * ple


def _ple(x2, p2, wp, gpost, gin, wg, tm):
    T, D = x2.shape
    Pd = p2.shape[1]
    full = lambda shape: pl.BlockSpec(shape, lambda i: (0, 0))
    return pl.pallas_call(
        functools.partial(_ple_kernel, tm=tm),
        grid=(T // tm,),
        in_specs=[
            pl.BlockSpec((tm, D), lambda i: (i, 0)),
            pl.BlockSpec((tm, Pd), lambda i: (i, 0)),
            full((Pd, D)), full((1, D)), full((1, D)),
            pl.BlockSpec((D, D), lambda i: (0, 0), pipeline_mode=pl.Buffered(1)),
        ],
        out_specs=pl.BlockSpec((tm, D), lambda i: (i, 0)),
        out_shape=jax.ShapeDtypeStruct((T, D), F32),
        compiler_params=_cparams(("parallel",)),
        name="ple",
    )(x2, p2, wp, gpost, gin, wg)


def _pad_cols(w, n):
    return jnp.pad(w, ((0, 0), (0, n - w.shape[1])))


def _pad_rows(w, n):
    return jnp.pad(w, ((0, n - w.shape[0]), (0, 0)))


def _layout_w_in(w):
    W = RWKV_WIDTH
    c = 3 * W
    segs = [w[:, 0:c],
            _pad_cols(w[:, c:c + DECAY_LORA], LORA_PAD),
            _pad_cols(w[:, c + DECAY_LORA:c + DECAY_LORA + AAA_LORA], LORA_PAD),
            _pad_cols(w[:, c + DECAY_LORA + AAA_LORA:c + DECAY_LORA + AAA_LORA + GATE_LORA], GATE_LORA_PAD)]
    c += DECAY_LORA + AAA_LORA + GATE_LORA
    mla_cols = Q_LORA + KV_LORA + QK_ROPE
    segs.append(_pad_cols(w[:, c:c + mla_cols], MLA_PAD))
    c += mla_cols
    segs.append(w[:, c:])
    out = jnp.concatenate(segs, axis=1).astype(BF16)
    assert out.shape[1] == IN_PAD
    return out


def _layout_mu(mu):
    W = RWKV_WIDTH
    c = 3 * W
    segs = [mu[0:c],
            jnp.pad(mu[c:c + DECAY_LORA], (0, LORA_PAD - DECAY_LORA)),
            jnp.pad(mu[c + DECAY_LORA:c + DECAY_LORA + AAA_LORA], (0, LORA_PAD - AAA_LORA)),
            jnp.pad(mu[c + DECAY_LORA + AAA_LORA:], (0, GATE_LORA_PAD - GATE_LORA))]
    return jnp.concatenate(segs).reshape(1, RWKV_PAD)


def _moe_tables(eidx, rnk, counts, tmf, tm_slots):
    K, T = eidx.shape
    E = N_EXPERTS
    A = K * T
    n_tiles = A // tmf + E
    P = n_tiles * tmf
    pcounts = (counts + tmf - 1) // tmf * tmf
    pends = jnp.cumsum(pcounts)
    pstart = (pends - pcounts).astype(I32)
    tile_start = jnp.arange(n_tiles, dtype=I32) * tmf
    tile_e = jnp.minimum(jnp.sum((pends[None, :] <= tile_start[:, None]).astype(I32), axis=1), E - 1)
    n_used = (pends[E - 1:] // tmf).astype(I32)
    dest = _slots(eidx, rnk, pstart, tm_slots)
    j = jnp.arange(tmf, dtype=I32)[None, :]
    tail = j < (pcounts - counts)[:, None]
    spare = jnp.cumsum(jnp.logical_not(tail).reshape(-1).astype(I32)) - 1
    pad_row = jnp.where(tail.reshape(-1), ((pstart + counts)[:, None] + j).reshape(-1),
                        pends[E - 1] + spare)
    q = jnp.arange(P - A, dtype=I32)
    slot_id = jnp.arange(A, dtype=I32)
    rows = jnp.concatenate([dest.reshape(-1), pad_row.astype(I32)])
    slots = jnp.concatenate([slot_id, A + q])
    row_dst = lax.sort((rows, slots), num_keys=1)[1]
    row_tok = jnp.where(row_dst < A, row_dst % T, 0)
    return (row_tok.reshape(n_tiles, 1, tmf), row_dst.reshape(n_tiles, 1, tmf), tile_e, n_used)


def _layer(x, p, positions, g_mix, w_in, mu_rwkv, w0, w2, a0, a2, g2, k_k, k_a, r_k,
           gn_w, gn_b, w_a_up, g_qa, g_kva, w_uq, w_ukv, g_qn, g_kn, w_b_up, w_o,
           g_ffn, w_router, router_bias, w_exp_gate, w_exp_up, w_exp_down,
           w_sh_gate, w_sh_up, w_sh_down, w_ple, g_ple_post, g_ple_in, w_ple_gate,
           *, tiles):
    B, S, D = x.shape
    T = B * S
    W = RWKV_WIDTH
    x2 = x.reshape(T, D)
    row = lambda v: v.reshape(1, -1).astype(F32)

    proj = _inproj(x2, row(g_mix), _layout_w_in(w_in), tiles["tm_in"], tiles["tn_in"])

    head_blk = (jnp.arange(W)[:, None] // RWKV_HEAD == jnp.arange(W)[None, :] // RWKV_HEAD).astype(BF16)
    r, k, v, kk, a, lw, g = _rwkv_prep(
        proj, _layout_mu(mu_rwkv), row(w0), _pad_rows(w2, LORA_PAD).astype(BF16), row(a0),
        _pad_rows(a2, LORA_PAD).astype(BF16), _pad_rows(g2, GATE_LORA_PAD).astype(BF16),
        row(k_k), row(k_a), head_blk, tiles["tm_prep"], S)
    b3 = lambda t: t.reshape(B, S, W)
    o_a = _rwkv_scan(b3(r), b3(k), b3(v), b3(kk), b3(a), b3(lw), b3(g),
                     row(gn_w), row(gn_b), row(r_k), tiles["rows_scan"]).reshape(T, W)

    half = QK_ROPE // 2
    inv_freq = ROPE_THETA ** (-jnp.arange(half, dtype=F32) / half)
    invf = jnp.concatenate([inv_freq, inv_freq, jnp.zeros((LANES - QK_ROPE,), F32)]).reshape(1, LANES)
    wuq = jnp.pad(w_uq.reshape(Q_LORA, MLA_HEADS, QK_HEAD),
                  ((0, 0), (0, 0), (0, QK_PAD - QK_HEAD))).reshape(Q_LORA, MLA_HEADS * QK_PAD)
    pad_g = lambda gv: jnp.pad(gv, (0, QK_PAD - QK_HEAD)).reshape(1, QK_PAD)
    q, kx, vx = _mla_prep(proj, positions.reshape(T, 1).astype(I32), invf, row(g_qa), row(g_kva),
                          wuq.astype(BF16), w_ukv.astype(BF16), pad_g(g_qn), pad_g(g_kn),
                          tiles["tm_mla"])
    o_b = _flash(q.reshape(B, S, -1), kx.reshape(B, S, -1), vx, tiles["tq"]).reshape(T, MLA_WIDTH)

    merged = _merge(o_a, o_b, w_a_up.astype(BF16), w_b_up.astype(BF16), proj,
                    tiles["tm_merge"], tiles["tn_merge"])

    x1, h2, h2p, logits_t = _wo(merged, x2, w_o.astype(BF16), row(g_ffn),
                                jnp.transpose(w_router).astype(F32), tiles["tm_wo"])
    eidx, wts, rnk, cnt = _route(logits_t, router_bias.reshape(N_EXPERTS, 1).astype(F32),
                                 tiles["tm_route"])
    tmf = tiles["tm_ffn"]
    counts = cnt[:, 0].astype(I32)
    row_tok, row_dst, tile_e, n_used = _moe_tables(eidx, rnk, counts, tmf, tiles["tm_slots"])
    ys = _ffn(tile_e, n_used, row_tok, row_dst, h2p, w_exp_gate, w_exp_up, w_exp_down, tmf)
    x3 = _combine(ys, jnp.transpose(wts), x1, h2,
                  w_sh_gate.astype(BF16), w_sh_up.astype(BF16), w_sh_down.astype(BF16),
                  tiles["tm_comb"])

    out = _ple(x3, p.reshape(T, PLE_DIM), w_ple.astype(BF16), row(g_ple_post), row(g_ple_in),
               w_ple_gate.astype(BF16), tiles["tm_ple"])
    return out.reshape(B, S, D)


TILES = dict(tm_in=1024, tn_in=1024, tm_prep=256, rows_scan=128, tm_mla=1024, tq=512,
             tm_merge=1024, tn_merge=512, tm_wo=512, tm_route=512, tm_ffn=256,
             tm_slots=2048, tm_comb=256, tm_ple=512)


def kernel(x, p, positions, g_mix, w_in, mu_rwkv, w0, w2, a0, a2, g2, k_k, k_a, r_k, gn_w, gn_b, w_a_up, g_qa, g_kva, w_uq, w_ukv, g_qn, g_kn, w_b_up, w_o, g_ffn, w_router, router_bias, w_exp_gate, w_exp_up, w_exp_down, w_sh_gate, w_sh_up, w_sh_down, w_ple, g_ple_post, g_ple_in, w_ple_gate):
    args = (g_mix, w_in, mu_rwkv, w0, w2, a0, a2, g2, k_k, k_a, r_k, gn_w, gn_b, w_a_up,
            g_qa, g_kva, w_uq, w_ukv, g_qn, g_kn, w_b_up, w_o, g_ffn, w_router, router_bias,
            w_exp_gate, w_exp_up, w_exp_down, w_sh_gate, w_sh_up, w_sh_down, w_ple,
            g_ple_post, g_ple_in, w_ple_gate)
    assert all(t.shape[0] == 1 for t in args), "single-layer stack expected"
    return _layer(x, p[0], positions, *[t[0] for t in args], tiles=TILES)
```

```python
import functools

import jax
import jax.numpy as jnp
from jax import lax
from jax.experimental import pallas as pl
from jax.experimental.pallas import tpu as pltpu

F32 = jnp.float32
BF16 = jnp.bfloat16
I32 = jnp.int32
U32 = jnp.uint32

D_MODEL = 2048
PLE_DIM = 256
NORM_EPS = 1e-6
RWKV_HEADS = 16
RWKV_HEAD = 64
RWKV_WIDTH = RWKV_HEADS * RWKV_HEAD
DECAY_LORA = 64
AAA_LORA = 64
GATE_LORA = 160
GN_EPS = 64e-5
MLA_HEADS = 8
Q_LORA = 512
KV_LORA = 512
QK_NOPE = 128
QK_ROPE = 64
QK_HEAD = QK_NOPE + QK_ROPE
V_HEAD = 128
MLA_WIDTH = MLA_HEADS * V_HEAD
ROPE_THETA = 10000.0
N_EXPERTS = 64
TOP_K = 8
N_GROUPS = 8
GROUP_SIZE = N_EXPERTS // N_GROUPS
TOPK_GROUPS = 4
MOE_INTER = 512
ROUTED_SCALE = 2.5

LANES = 128
QK_PAD = 2 * LANES
PACK_ROWS = 8
PACK_W = D_MODEL // 2 // PACK_ROWS
assert PACK_W == LANES

LORA_PAD = 128
GATE_LORA_PAD = 256
RWKV_PAD = 3 * RWKV_WIDTH + 2 * LORA_PAD + GATE_LORA_PAD
MLA_PAD = 1536
COL_RWKV = 0
COL_MLA = RWKV_PAD
COL_GA = COL_MLA + MLA_PAD
COL_GB = COL_GA + D_MODEL
IN_PAD = COL_GB + D_MODEL

VMEM_LIMIT = 56 * 1024 * 1024
CHUNK = 64
WO_ROWS = 128
SCAN_LANE_BLOCKS = 8
MLA_ROWS = 256


def _cparams(sem):
    return pltpu.CompilerParams(dimension_semantics=sem, vmem_limit_bytes=VMEM_LIMIT)


def _dot(a, b):
    return jnp.dot(a, b, preferred_element_type=F32)


def _dot_nt(a, b):
    return lax.dot_general(a, b, (((1,), (1,)), ((), ())), preferred_element_type=F32)


def _dot_tn(a, b):
    return lax.dot_general(a, b, (((0,), (0,)), ((), ())), preferred_element_type=F32)


def _split_bf16(x):
    hi = x.astype(BF16)
    lo = (x - hi.astype(F32)).astype(BF16)
    return hi, lo


def _rms(x, g):
    ms = jnp.mean(x * x, axis=-1, keepdims=True)
    return x * lax.rsqrt(ms + NORM_EPS) * g


def _sigmoid(x):
    return 1.0 / (1.0 + jnp.exp(-x))


def _inproj_kernel(x_ref, g_ref, w_ref, o_ref, h_scr):
    @pl.when(pl.program_id(1) == 0)
    def _():
        h_scr[...] = _rms(x_ref[...], g_ref[...]).astype(BF16)

    o_ref[...] = _dot(h_scr[...], w_ref[...]).astype(o_ref.dtype)


def _inproj(x2, g, w, tm, tn):
    T, D = x2.shape
    N = w.shape[1]
    return pl.pallas_call(
        _inproj_kernel,
        grid=(T // tm, N // tn),
        in_specs=[
            pl.BlockSpec((tm, D), lambda i, j: (i, 0)),
            pl.BlockSpec((1, D), lambda i, j: (0, 0)),
            pl.BlockSpec((D, tn), lambda i, j: (0, j)),
        ],
        out_specs=pl.BlockSpec((tm, tn), lambda i, j: (i, j)),
        out_shape=jax.ShapeDtypeStruct((T, N), BF16),
        scratch_shapes=[pltpu.VMEM((tm, D), BF16)],
        compiler_params=_cparams(("parallel", "arbitrary")),
        name="inproj",
    )(x2, g, w)


HALO = 16


def _rwkv_prep_kernel(cur_ref, halo_ref, mu_ref, w0_ref, w2_ref, a0_ref, a2_ref, g2_ref,
                      kk_ref, ka_ref, bd_ref,
                      r_out, k_out, v_out, kk_out, a_out, lw_out, g_out, *, tm, seq):
    i = pl.program_id(0)
    cur = cur_ref[...].astype(F32)
    first = (i * tm) % seq == 0
    last_prev = halo_ref[HALO - 1:HALO, :].astype(F32)
    last_prev = jnp.where(first, 0.0, last_prev)
    row = lax.broadcasted_iota(I32, cur.shape, 0)
    prev = jnp.where(row == 0, last_prev, pltpu.roll(cur, 1, 0))
    u = cur + (prev - cur) * mu_ref[...]
    W = RWKV_WIDTH
    r = u[:, 0:W]
    k = u[:, W:2 * W]
    v = u[:, 2 * W:3 * W]
    dw = u[:, 3 * W:3 * W + LORA_PAD]
    da = u[:, 3 * W + LORA_PAD:3 * W + 2 * LORA_PAD]
    dg = u[:, 3 * W + 2 * LORA_PAD:3 * W + 2 * LORA_PAD + GATE_LORA_PAD]
    z = w0_ref[...] + _dot(jnp.tanh(dw).astype(BF16), w2_ref[...])
    w_log = -(jnp.maximum(-z, 0.0) + jnp.log(1.0 + jnp.exp(-jnp.abs(z)))) - 0.5
    lw_out[...] = -jnp.exp(w_log)
    a = _sigmoid(a0_ref[...] + _dot(da.astype(BF16), a2_ref[...]))
    g_out[...] = _dot(_sigmoid(dg).astype(BF16), g2_ref[...]).astype(g_out.dtype)
    kk = k * kk_ref[...]
    hi, lo = _split_bf16(kk * kk)
    ssq = _dot(hi, bd_ref[...]) + _dot(lo, bd_ref[...])
    kk = kk / jnp.maximum(jnp.sqrt(ssq), 1e-12)
    r_out[...] = r.astype(r_out.dtype)
    k_out[...] = (k * (1.0 + (a - 1.0) * ka_ref[...])).astype(k_out.dtype)
    v_out[...] = v.astype(v_out.dtype)
    kk_out[...] = kk.astype(kk_out.dtype)
    a_out[...] = a.astype(a_out.dtype)


def _rwkv_prep(proj, mu, w0, w2, a0, a2, g2, k_k, k_a, bd, tm, seq):
    T = proj.shape[0]
    W = RWKV_WIDTH
    full = lambda shape: pl.BlockSpec(shape, lambda i: (0, 0))
    out = lambda: pl.BlockSpec((tm, W), lambda i: (i, 0))
    return pl.pallas_call(
        functools.partial(_rwkv_prep_kernel, tm=tm, seq=seq),
        grid=(T // tm,),
        in_specs=[
            pl.BlockSpec((tm, RWKV_PAD), lambda i: (i, 0)),
            pl.BlockSpec((HALO, RWKV_PAD), lambda i: (jnp.maximum(i * (tm // HALO) - 1, 0), 0)),
            full((1, RWKV_PAD)), full((1, W)), full((LORA_PAD, W)), full((1, W)),
            full((LORA_PAD, W)), full((GATE_LORA_PAD, W)), full((1, W)), full((1, W)),
            full((W, W)),
        ],
        out_specs=[out() for _ in range(7)],
        out_shape=[jax.ShapeDtypeStruct((T, W), dt)
                   for dt in (BF16, BF16, BF16, BF16, BF16, F32, BF16)],
        compiler_params=_cparams(("parallel",)),
        name="rwkv_prep",
    )(proj, proj, mu, w0, w2, a0, a2, g2, k_k, k_a, bd)


def _rwkv_scan_kernel(r_ref, k_ref, v_ref, kk_ref, a_ref, lw_ref, g_ref,
                      gnw_ref, gnb_ref, rk_ref, o_ref, s_scr, *, rows):
    L = CHUNK
    L2 = 2 * L

    @pl.when(pl.program_id(2) == 0)
    def _():
        s_scr[...] = jnp.zeros_like(s_scr)

    lane = lax.broadcasted_iota(I32, (L, LANES), 1)
    m_a = (lane < RWKV_HEAD).astype(F32)
    m_b = 1.0 - m_a
    ri = lax.broadcasted_iota(I32, (L2, L2), 0)
    ci = lax.broadcasted_iota(I32, (L2, L2), 1)
    strict = ri > ci
    incl = ri >= ci
    eye = (ri == ci).astype(F32)
    tri = (lax.broadcasted_iota(I32, (L, L), 0) >= lax.broadcasted_iota(I32, (L, L), 1)).astype(BF16)
    hr = lax.broadcasted_iota(I32, (LANES, LANES), 0) // RWKV_HEAD
    hc = lax.broadcasted_iota(I32, (LANES, LANES), 1) // RWKV_HEAD
    head_ones = (hr == hc).astype(BF16)
    nb = SCAN_LANE_BLOCKS

    def stack(x):
        return jnp.concatenate([x * m_a, x * m_b], axis=0)

    chunks = range(rows // L)
    rows_of = lambda c: slice(c * L, (c + 1) * L)
    lanes_of = lambda h: slice(h * LANES, (h + 1) * LANES)
    each = lambda f, *lists: [f(*vals) for vals in zip(*lists)]

    def operands(c, h):
        sl, ln = rows_of(c), lanes_of(h)
        r = r_ref[0, sl, ln].astype(F32)
        k = k_ref[0, sl, ln].astype(F32)
        v = v_ref[0, sl, ln].astype(F32)
        kk = kk_ref[0, sl, ln].astype(F32)
        a = a_ref[0, sl, ln].astype(F32)
        lw = lw_ref[0, sl, ln]
        hi, lo = _split_bf16(lw)
        cum = _dot(tri, hi) + _dot(tri, lo)
        c_end = cum[L - 1:L, :]
        p_inv = jnp.exp(-cum)
        p_end = jnp.exp(c_end - cum)
        b = kk * a
        return dict(
            x_a=stack(-(kk * jnp.exp(cum - lw))).astype(BF16),
            x_r=stack(r * jnp.exp(cum)),
            x_bk=jnp.concatenate([stack(b * p_inv), stack(k * p_inv)], axis=0).astype(BF16),
            v_st=stack(v).astype(BF16),
            z_hat=jnp.concatenate([stack(b * p_end), stack(k * p_end)], axis=0).astype(BF16),
            p_row=jnp.exp(c_end),
            rkr=(r * k * rk_ref[:, ln]).astype(BF16), v=v)

    ops = [operands(c, h) for c in chunks for h in range(nb)]
    big = [_dot_nt(jnp.concatenate([o["x_a"], o["x_r"].astype(BF16)], axis=0), o["x_bk"])
           for o in ops]
    n_pow = [jnp.where(strict, m[0:L2, 0:L2], 0.0) for m in big]
    a_ak = [jnp.where(strict, m[0:L2, L2:2 * L2], 0.0).astype(BF16) for m in big]
    a_r = [jnp.concatenate([jnp.where(incl, m[L2:2 * L2, 0:L2], 0.0),
                            jnp.where(incl, m[L2:2 * L2, L2:2 * L2], 0.0)], axis=1).astype(BF16)
           for m in big]

    t_inv = [eye + n for n in n_pow]
    for _ in range(5):
        n_pow = each(lambda n: _dot(n.astype(BF16), n.astype(BF16)), n_pow)
        t_inv = each(lambda t, n: t + _dot(t.astype(BF16), n.astype(BF16)), t_inv, n_pow)

    akv = each(lambda m, o: _dot(m, o["v_st"]).astype(BF16), a_ak, ops)
    w = each(lambda t, o, x: _dot(t.astype(BF16), jnp.concatenate([o["x_a"], x], axis=1)),
             t_inv, ops, akv)
    w1 = [m[:, 0:LANES].astype(BF16) for m in w]
    w2v = each(lambda m, o: jnp.concatenate([m[:, LANES:].astype(BF16), o["v_st"]], axis=0), w, ops)
    g_mat = each(lambda o, m, x: (o["x_r"] + _dot(m[:, 0:L2], x)).astype(BF16), ops, a_r, w1)
    y0 = each(_dot, a_r, w2v)
    m_mat = each(lambda x, o: _dot_tn(x, o["z_hat"][0:L2, :]).astype(BF16), w1, ops)
    c2 = each(lambda x, o: _dot_tn(x, o["z_hat"]), w2v, ops)
    bonus_v = [_dot(o["rkr"], head_ones) * o["v"] for o in ops]

    s = [s_scr[h] for h in range(nb)]
    y = []
    for i in range(len(ops)):
        h = i % nb
        s_b = s[h].astype(BF16)
        y_st = _dot_nt(g_mat[i], s_b) + y0[i]
        s[h] = s[h] * ops[i]["p_row"] + _dot(s_b, m_mat[i]) + c2[i]
        y.append(y_st[0:L, :] + y_st[L:L2, :])
    for h in range(nb):
        s_scr[h] = s[h]

    mean = [_dot(t.astype(BF16), head_ones) * (1.0 / RWKV_HEAD) for t in y]
    yc = each(lambda t, m: t - m, y, mean)
    var = [_dot((t * t).astype(BF16), head_ones) * (1.0 / RWKV_HEAD) for t in yc]
    for i in range(len(ops)):
        sl, ln = rows_of(i // nb), lanes_of(i % nb)
        yn = yc[i] * lax.rsqrt(var[i] + GN_EPS) * gnw_ref[:, ln] + gnb_ref[:, ln]
        o_ref[0, sl, ln] = ((yn + bonus_v[i]) * g_ref[0, sl, ln].astype(F32)).astype(o_ref.dtype)


def _rwkv_scan(r, k, v, kk, a, lw, g, gn_w, gn_b, r_k, rows):
    B, S, W = r.shape
    wb = SCAN_LANE_BLOCKS * LANES
    npair = W // wb
    seq = lambda: pl.BlockSpec((1, rows, wb), lambda b, p, c: (b, c, p))
    par = lambda: pl.BlockSpec((1, wb), lambda b, p, c: (0, p))
    return pl.pallas_call(
        functools.partial(_rwkv_scan_kernel, rows=rows),
        grid=(B, npair, S // rows),
        in_specs=[seq() for _ in range(7)] + [par(), par(), par()],
        out_specs=seq(),
        out_shape=jax.ShapeDtypeStruct((B, S, W), BF16),
        scratch_shapes=[pltpu.VMEM((SCAN_LANE_BLOCKS, LANES, LANES), F32)],
        compiler_params=_cparams(("parallel", "parallel", "arbitrary")),
        name="rwkv_scan",
    )(r, k, v, kk, a, lw, g, gn_w, gn_b, r_k)


def _rope128(x, cos, sin_signed):
    lane = lax.broadcasted_iota(I32, x.shape, 1)
    half = QK_ROPE // 2
    partner = jnp.where(lane < half, pltpu.roll(x, LANES - half, 1), pltpu.roll(x, half, 1))
    return x * cos + partner * sin_signed


def _mla_prep_kernel(cq_ref, ckv_ref, kr_ref, pos_ref, invf_ref, gqa_ref, gkva_ref,
                     wuq_ref, wukv_ref, gqn_ref, gkn_ref,
                     q_out, k_out, v_out, cqn_scr, ckvn_scr, cos_scr, sin_scr):
    @pl.when(pl.program_id(1) == 0)
    def _():
        cqn_scr[...] = _rms(cq_ref[...].astype(F32), gqa_ref[...]).astype(BF16)
        ckvn_scr[...] = _rms(ckv_ref[...].astype(F32), gkva_ref[...]).astype(BF16)
        ang = pos_ref[...].astype(F32) * invf_ref[...]
        lane = lax.broadcasted_iota(I32, ang.shape, 1)
        cos_scr[...] = jnp.cos(ang)
        sin_scr[...] = jnp.where(lane < QK_ROPE // 2, -1.0, 1.0) * jnp.sin(ang)

    tm = q_out.shape[0]
    hw = min(MLA_ROWS, tm)
    groups = [slice(g * hw, (g + 1) * hw) for g in range(tm // hw)]
    qs = [_dot(cqn_scr[r, :], wuq_ref[...]) for r in groups]
    kvs = [_dot(ckvn_scr[r, :], wukv_ref[...]) for r in groups]
    inv_n = 1.0 / QK_HEAD
    gkn = gkn_ref[...]
    for r, q, kv in zip(groups, qs, kvs):
        cos = cos_scr[r, :]
        sin = sin_scr[r, :]
        rs = lax.rsqrt(jnp.sum(q * q, axis=-1, keepdims=True) * inv_n + NORM_EPS)
        q = q * rs * gqn_ref[...] * (QK_HEAD ** -0.5 * LOG2E)
        q_out[r, :] = jnp.concatenate(
            [q[:, 0:LANES], _rope128(q[:, LANES:], cos, sin)], axis=1).astype(q_out.dtype)
        k_nope = kv[:, 0:QK_NOPE]
        kr = kr_ref[r, :].astype(F32)
        ssq = (jnp.sum(k_nope * k_nope, axis=-1, keepdims=True)
               + jnp.sum(kr * kr, axis=-1, keepdims=True))
        rs = lax.rsqrt(ssq * inv_n + NORM_EPS)
        k_out[r, :] = jnp.concatenate(
            [k_nope * rs * gkn[:, 0:LANES], _rope128(kr * rs * gkn[:, LANES:], cos, sin)],
            axis=1).astype(k_out.dtype)
        v_out[:, r] = jnp.transpose(kv[:, QK_NOPE:].astype(v_out.dtype))


def _mla_prep(proj, pos, invf, g_qa, g_kva, wuq, wukv, gqn, gkn, tm):
    T = proj.shape[0]
    H = MLA_HEADS
    c0 = COL_MLA // Q_LORA
    ckr = (COL_MLA + Q_LORA + KV_LORA) // LANES
    full = lambda shape: pl.BlockSpec(shape, lambda i, h: (0, 0))
    return pl.pallas_call(
        _mla_prep_kernel,
        grid=(T // tm, H),
        in_specs=[
            pl.BlockSpec((tm, Q_LORA), lambda i, h: (i, c0)),
            pl.BlockSpec((tm, KV_LORA), lambda i, h: (i, c0 + 1)),
            pl.BlockSpec((tm, LANES), lambda i, h: (i, ckr)),
            pl.BlockSpec((tm, 1), lambda i, h: (i, 0)),
            full((1, LANES)), full((1, Q_LORA)), full((1, KV_LORA)),
            pl.BlockSpec((Q_LORA, QK_PAD), lambda i, h: (0, h)),
            pl.BlockSpec((KV_LORA, QK_NOPE + V_HEAD), lambda i, h: (0, h)),
            full((1, QK_PAD)), full((1, QK_PAD)),
        ],
        out_specs=[
            pl.BlockSpec((tm, QK_PAD), lambda i, h: (i, h)),
            pl.BlockSpec((tm, QK_PAD), lambda i, h: (i, h)),
            pl.BlockSpec((V_HEAD, tm), lambda i, h: (h, i)),
        ],
        out_shape=[jax.ShapeDtypeStruct((T, H * QK_PAD), BF16),
                   jax.ShapeDtypeStruct((T, H * QK_PAD), BF16),
                   jax.ShapeDtypeStruct((H * V_HEAD, T), BF16)],
        scratch_shapes=[pltpu.VMEM((tm, Q_LORA), BF16), pltpu.VMEM((tm, KV_LORA), BF16),
                        pltpu.VMEM((tm, LANES), F32), pltpu.VMEM((tm, LANES), F32)],
        compiler_params=_cparams(("parallel", "arbitrary")),
        name="mla_prep",
    )(proj, proj, proj, pos, invf, g_qa, g_kva, wuq, wukv, gqn, gkn)


MASK_VALUE = -1e30
LOG2E = 1.4426950408889634


def _flash_kernel(q_ref, k_ref, vt_ref, o_ref, s_scr, p_scr, alpha_scr, m_scr, l_scr, acc_scr,
                  *, tq):
    i = pl.program_id(2)
    q = q_ref[0]
    m_scr[...] = jnp.full_like(m_scr, MASK_VALUE)
    l_scr[...] = jnp.zeros_like(l_scr)
    acc_scr[...] = jnp.zeros_like(acc_scr)

    q_half = tq // 2

    def scores(j, slot, half):
        ks = pl.ds(pl.multiple_of(j * tq, tq), tq)
        cols = slice(half * q_half, (half + 1) * q_half)
        s_scr[slot, :, cols] = _dot_nt(k_ref[0, ks, :], q[cols, :])

    def softmax(slot, masked, strips):
        for c in strips:
            cols = slice(c * LANES, (c + 1) * LANES)
            s = s_scr[slot, :, cols]
            if masked:
                kpos = lax.broadcasted_iota(I32, s.shape, 0)
                qpos = lax.broadcasted_iota(I32, s.shape, 1) + c * LANES
                s = jnp.where(kpos <= qpos, s, MASK_VALUE)
            m_old = m_scr[:, cols]
            m_new = jnp.maximum(m_old, jnp.max(s, axis=0, keepdims=True))
            p = jnp.exp2(s - m_new)
            alpha = jnp.exp2(m_old - m_new)
            l_scr[:, cols] = alpha * l_scr[:, cols] + jnp.sum(p, axis=0, keepdims=True)
            m_scr[:, cols] = m_new
            p_scr[slot, :, cols] = p.astype(BF16)
            alpha_scr[slot, :, cols] = alpha

    def weighted_values(j, slot):
        vt = vt_ref[:, pl.ds(pl.multiple_of(j * tq, tq), tq)]
        acc_scr[...] = alpha_scr[slot] * acc_scr[...] + _dot(vt, p_scr[slot])

    scores(0, 0, 0)
    scores(0, 0, 1)
    p_scr[1] = jnp.zeros((tq, tq), BF16)
    alpha_scr[1] = jnp.ones((1, tq), F32)
    n_strips = tq // LANES
    first, second = range(0, n_strips // 2), range(n_strips // 2, n_strips)

    def step(j, slot):
        weighted_values(jnp.maximum(j - 1, 0), 1 - slot)
        scores(j + 1, 1 - slot, 0)
        softmax(slot, False, first)
        scores(j + 1, 1 - slot, 1)
        softmax(slot, False, second)

    def pair(jp, carry):
        step(2 * jp, 0)
        step(2 * jp + 1, 1)
        return carry

    def finish(slot):
        weighted_values(jnp.maximum(i - 1, 0), 1 - slot)
        softmax(slot, True, range(n_strips))
        weighted_values(i, slot)
        o_ref[0] = jnp.transpose(acc_scr[...] / l_scr[...]).astype(o_ref.dtype)

    lax.fori_loop(0, i // 2, pair, 0)

    @pl.when(i % 2 == 0)
    def _():
        finish(0)

    @pl.when(i % 2 == 1)
    def _():
        step(i - 1, 0)
        finish(1)


def _flash(q, k, vt, tq):
    B, S, _ = q.shape
    H = MLA_HEADS
    return pl.pallas_call(
        functools.partial(_flash_kernel, tq=tq),
        grid=(B, H, S // tq),
        in_specs=[
            pl.BlockSpec((1, tq, QK_PAD), lambda b, h, i: (b, i, h)),
            pl.BlockSpec((1, S, QK_PAD), lambda b, h, i: (b, 0, h)),
            pl.BlockSpec((V_HEAD, S), lambda b, h, i: (h, b)),
        ],
        out_specs=pl.BlockSpec((1, tq, V_HEAD), lambda b, h, i: (b, i, h)),
        out_shape=jax.ShapeDtypeStruct((B, S, H * V_HEAD), BF16),
        scratch_shapes=[pltpu.VMEM((2, tq, tq), F32), pltpu.VMEM((2, tq, tq), BF16),
                        pltpu.VMEM((2, 1, tq), F32),
                        pltpu.VMEM((1, tq), F32), pltpu.VMEM((1, tq), F32),
                        pltpu.VMEM((V_HEAD, tq), F32)],
        compiler_params=_cparams(("parallel", "parallel", "arbitrary")),
        name="flash_attn",
    )(q, k, vt)


def _merge_kernel(oa_ref, ob_ref, wa_ref, wb_ref, ga_ref, gb_ref, o_ref):
    ya = _dot(oa_ref[...], wa_ref[...])
    yb = _dot(ob_ref[...], wb_ref[...])
    ga = _sigmoid(ga_ref[...].astype(F32))
    gb = _sigmoid(gb_ref[...].astype(F32))
    o_ref[...] = (ga * ya + gb * yb).astype(o_ref.dtype)


def _merge(oa, ob, wa, wb, proj, tm, tn):
    T, K = oa.shape
    N = wa.shape[1]
    ca = COL_GA // tn
    cb = COL_GB // tn
    return pl.pallas_call(
        _merge_kernel,
        grid=(T // tm, N // tn),
        in_specs=[
            pl.BlockSpec((tm, K), lambda i, j: (i, 0)),
            pl.BlockSpec((tm, K), lambda i, j: (i, 0)),
            pl.BlockSpec((K, tn), lambda i, j: (0, j)),
            pl.BlockSpec((K, tn), lambda i, j: (0, j)),
            pl.BlockSpec((tm, tn), lambda i, j: (i, ca + j)),
            pl.BlockSpec((tm, tn), lambda i, j: (i, cb + j)),
        ],
        out_specs=pl.BlockSpec((tm, tn), lambda i, j: (i, j)),
        out_shape=jax.ShapeDtypeStruct((T, N), BF16),
        compiler_params=_cparams(("parallel", "parallel")),
        name="merge",
    )(oa, ob, wa, wb, proj, proj)


def _pack_rows(x, out_ref, n, row0=0):
    for c in range(PACK_ROWS):
        lo = x[:, c * 2 * LANES:c * 2 * LANES + LANES]
        hi = x[:, c * 2 * LANES + LANES:(c + 1) * 2 * LANES]
        lo_b = pltpu.bitcast(lo.astype(BF16).astype(F32), U32)
        hi_b = pltpu.bitcast(hi.astype(BF16).astype(F32), U32)
        out_ref[pl.ds(row0 * PACK_ROWS + c, n, stride=PACK_ROWS), :] = hi_b | (lo_b >> 16)


def _unpack_rows(ref, c, tm, lead=None):
    idx = (pl.ds(c, tm, stride=PACK_ROWS), slice(None))
    if lead is not None:
        idx = (lead,) + idx
    u = ref[idx]
    lo = pltpu.bitcast(u << 16, F32)
    hi = pltpu.bitcast(u & jnp.uint32(0xFFFF0000), F32)
    return lo, hi


def _wo_kernel(m_ref, x_ref, wo_ref, g_ref, wr_ref, x1_ref, h2_ref, h2p_ref, lg_ref, *, tm):
    hw = WO_ROWS
    groups = [slice(h * hw, (h + 1) * hw) for h in range(tm // hw)]
    wo = wo_ref[...]
    x1 = [x_ref[r, :] + _dot(m_ref[r, :], wo) for r in groups]
    h2 = [_rms(v, g_ref[...]) for v in x1]
    w_hi, w_lo = _split_bf16(wr_ref[...])
    for h, r in enumerate(groups):
        x1_ref[r, :] = x1[h]
        h2_ref[r, :] = h2[h].astype(BF16)
        _pack_rows(h2[h], h2p_ref, hw, row0=h * hw)
        h_hi, h_lo = _split_bf16(h2[h])
        lg_ref[:, r] = _dot_nt(w_hi, h_hi) + _dot_nt(w_hi, h_lo) + _dot_nt(w_lo, h_hi)


def _wo(merged, x2, wo, g, wr_t, tm):
    T, D = x2.shape
    E = wr_t.shape[0]
    return pl.pallas_call(
        functools.partial(_wo_kernel, tm=tm),
        grid=(T // tm,),
        in_specs=[
            pl.BlockSpec((tm, D), lambda i: (i, 0)),
            pl.BlockSpec((tm, D), lambda i: (i, 0)),
            pl.BlockSpec((D, D), lambda i: (0, 0), pipeline_mode=pl.Buffered(1)),
            pl.BlockSpec((1, D), lambda i: (0, 0)),
            pl.BlockSpec((E, D), lambda i: (0, 0)),
        ],
        out_specs=[
            pl.BlockSpec((tm, D), lambda i: (i, 0)),
            pl.BlockSpec((tm, D), lambda i: (i, 0)),
            pl.BlockSpec((tm * PACK_ROWS, PACK_W), lambda i: (i, 0)),
            pl.BlockSpec((E, tm), lambda i: (0, i)),
        ],
        out_shape=[jax.ShapeDtypeStruct((T, D), F32),
                   jax.ShapeDtypeStruct((T, D), BF16),
                   jax.ShapeDtypeStruct((T * PACK_ROWS, PACK_W), U32),
                   jax.ShapeDtypeStruct((E, T), F32)],
        compiler_params=_cparams(("parallel",)),
        name="wo_norm_router",
    )(merged, x2, wo, g, wr_t)


def _route_kernel(lg_ref, bias_ref, eidx_ref, wts_ref, rnk_ref, cnt_ref, carry_scr, *, tm):
    E = N_EXPERTS
    NEG = -jnp.inf

    @pl.when(pl.program_id(0) == 0)
    def _():
        carry_scr[...] = jnp.zeros_like(carry_scr)

    scores = _sigmoid(lg_ref[...])
    choice = scores + bias_ref[...]
    c3 = choice.reshape(N_GROUPS, GROUP_SIZE, tm)
    sub = lax.broadcasted_iota(I32, c3.shape, 1)
    m1 = jnp.max(c3, axis=1, keepdims=True)
    i1 = jnp.min(jnp.where(c3 == m1, sub, GROUP_SIZE), axis=1, keepdims=True)
    m2 = jnp.max(jnp.where(sub == i1, NEG, c3), axis=1, keepdims=True)
    gs = (m1 + m2).reshape(N_GROUPS, tm)
    gi = lax.broadcasted_iota(I32, gs.shape, 0)
    gsel = jnp.zeros(gs.shape, F32)
    for _ in range(TOPK_GROUPS):
        mx = jnp.max(gs, axis=0, keepdims=True)
        ix = jnp.min(jnp.where(gs == mx, gi, N_GROUPS), axis=0, keepdims=True)
        hit = gi == ix
        gsel = jnp.where(hit, 1.0, gsel)
        gs = jnp.where(hit, NEG, gs)
    emask = jnp.broadcast_to(gsel.reshape(N_GROUPS, 1, tm), (N_GROUPS, GROUP_SIZE, tm)).reshape(E, tm)
    x = jnp.where(emask > 0.5, choice, NEG)
    ei = lax.broadcasted_iota(I32, x.shape, 0)
    sel = jnp.zeros(x.shape, F32)
    idx_rows, w_rows = [], []
    for _ in range(TOP_K):
        mx = jnp.max(x, axis=0, keepdims=True)
        ix = jnp.min(jnp.where(x == mx, ei, E), axis=0, keepdims=True)
        hit = ei == ix
        w_rows.append(jnp.sum(jnp.where(hit, scores, 0.0), axis=0, keepdims=True))
        idx_rows.append(ix)
        sel = jnp.where(hit, 1.0, sel)
        x = jnp.where(hit, NEG, x)
    w = jnp.concatenate(w_rows, axis=0)
    w = w / (jnp.sum(w, axis=0, keepdims=True) + 1e-20) * ROUTED_SCALE
    eidx_ref[...] = jnp.concatenate(idx_rows, axis=0)
    wts_ref[...] = w

    upper = (lax.broadcasted_iota(I32, (tm, tm), 0) < lax.broadcasted_iota(I32, (tm, tm), 1)).astype(BF16)
    base = carry_scr[...][:, 0:1]
    excl = _dot(sel.astype(BF16), upper) + base
    rnk_ref[...] = jnp.concatenate(
        [jnp.sum(jnp.where(ei == ix, excl, 0.0), axis=0, keepdims=True) for ix in idx_rows],
        axis=0).astype(I32)
    carry_scr[...] = carry_scr[...] + jnp.sum(sel, axis=1, keepdims=True)
    cnt_ref[...] = carry_scr[...]


def _route(logits_t, bias, tm):
    E, T = logits_t.shape
    K = TOP_K
    tok = lambda: pl.BlockSpec((K, tm), lambda i: (0, i))
    return pl.pallas_call(
        functools.partial(_route_kernel, tm=tm),
        grid=(T // tm,),
        in_specs=[pl.BlockSpec((E, tm), lambda i: (0, i)),
                  pl.BlockSpec((E, 1), lambda i: (0, 0))],
        out_specs=[tok(), tok(), tok(), pl.BlockSpec((E, LANES), lambda i: (0, 0))],
        out_shape=[jax.ShapeDtypeStruct((K, T), I32), jax.ShapeDtypeStruct((K, T), F32),
                   jax.ShapeDtypeStruct((K, T), I32), jax.ShapeDtypeStruct((E, LANES), F32)],
        scratch_shapes=[pltpu.VMEM((E, LANES), F32)],
        compiler_params=_cparams(("arbitrary",)),
        name="route",
    )(logits_t, bias)


def _slots_kernel(pst_ref, eidx_ref, rnk_ref, dest_ref):
    eidx = eidx_ref[...]
    dest = rnk_ref[...]
    for e in range(N_EXPERTS):
        dest = dest + jnp.where(eidx == e, pst_ref[e], 0)
    dest_ref[...] = dest


def _slots(eidx, rnk, pstart, tm):
    K, T = eidx.shape
    tok = lambda: pl.BlockSpec((K, tm), lambda i: (0, i))
    return pl.pallas_call(
        _slots_kernel,
        grid=(T // tm,),
        in_specs=[pl.BlockSpec(memory_space=pltpu.SMEM), tok(), tok()],
        out_specs=tok(),
        out_shape=jax.ShapeDtypeStruct((K, T), I32),
        compiler_params=_cparams(("parallel",)),
        name="slots",
    )(pstart, eidx, rnk)


def _ffn_kernel(te_ref, nu_ref, tok_ref, dst_ref, h_hbm, wg_ref, wu_ref, wd_ref, ys_hbm,
                gbuf, obuf, wg_b, wu_b, wd_b, gsem, osem, *, tm):
    s = pl.program_id(0)
    nu = nu_ref[0]
    tile_rows = tm * PACK_ROWS

    def gather(slot):
        for r in range(tm):
            pltpu.make_async_copy(
                h_hbm.at[pl.ds(pl.multiple_of(tok_ref[0, 0, r] * PACK_ROWS, PACK_ROWS), PACK_ROWS), :],
                gbuf.at[slot, pl.ds(r * PACK_ROWS, PACK_ROWS), :], gsem.at[slot]).start()

    def scatter(slot):
        for r in range(tm):
            pltpu.make_async_copy(
                obuf.at[slot, pl.ds(r * PACK_ROWS, PACK_ROWS), :],
                ys_hbm.at[pl.ds(pl.multiple_of(dst_ref[0, 0, r] * PACK_ROWS, PACK_ROWS), PACK_ROWS), :],
                osem.at[slot]).start()

    def wait_gather(slot):
        pltpu.make_async_copy(h_hbm.at[pl.ds(0, tile_rows), :], gbuf.at[slot], gsem.at[slot]).wait()

    def wait_scatter(slot):
        pltpu.make_async_copy(obuf.at[slot], ys_hbm.at[pl.ds(0, tile_rows), :], osem.at[slot]).wait()

    def compute(slot):
        parts = []
        for c in range(PACK_ROWS):
            lo, hi = _unpack_rows(gbuf.at[slot], c, tm)
            parts += [lo.astype(BF16), hi.astype(BF16)]
        a = jnp.concatenate(parts, axis=1)
        hg = _dot(a, wg_b[...])
        hu = _dot(a, wu_b[...])
        hid = (hg * _sigmoid(hg) * hu).astype(BF16)
        _pack_rows(_dot(hid, wd_b[...]), obuf.at[slot], tm)

    cur = jnp.maximum(s - 1, 0)
    @pl.when((s >= 1) & (s <= nu) & ((s == 1) | (te_ref[cur] != te_ref[jnp.maximum(s - 2, 0)])))
    def _():
        wg_b[...] = wg_ref[0].astype(BF16)
        wu_b[...] = wu_ref[0].astype(BF16)
        wd_b[...] = wd_ref[0].astype(BF16)

    @pl.when(s == 0)
    def _():
        gather(0)

    @pl.when(s == 1)
    def _():
        wait_gather(0)
        gather(1)
        compute(0)

    for slot in range(2):
        parity = s % 2 == slot

        @pl.when(parity & (s >= 3) & (s <= nu + 1))
        def _():
            wait_scatter(1 - slot)

        @pl.when(parity & (s >= 2) & (s <= nu))
        def _():
            wait_gather(1 - slot)
            gather(slot)
            scatter(slot)
            compute(1 - slot)

        @pl.when(parity & (s >= 2) & (s == nu + 1))
        def _():
            wait_gather(1 - slot)
            scatter(slot)
            wait_scatter(slot)


def _ffn(tile_e, n_used, row_tok, row_dst, h2p, wg, wu, wd, tm):
    nt = row_tok.shape[0]
    D, I = wg.shape[1], wg.shape[2]
    cur = lambda s: jnp.maximum(s - 1, 0)
    wspec = lambda shape: pl.BlockSpec(shape, lambda s, te, nu: (te[cur(s)], 0, 0))
    grid_spec = pltpu.PrefetchScalarGridSpec(
        num_scalar_prefetch=2,
        grid=(nt + 1,),
        in_specs=[
            pl.BlockSpec((1, 1, tm), lambda s, te, nu: (jnp.minimum(s, nt - 1), 0, 0),
                         memory_space=pltpu.SMEM),
            pl.BlockSpec((1, 1, tm), lambda s, te, nu: (jnp.maximum(s - 2, 0), 0, 0),
                         memory_space=pltpu.SMEM),
            pl.BlockSpec(memory_space=pl.ANY),
            wspec((1, D, I)), wspec((1, D, I)), wspec((1, I, D)),
        ],
        out_specs=pl.BlockSpec(memory_space=pl.ANY),
        scratch_shapes=[pltpu.VMEM((2, tm * PACK_ROWS, PACK_W), U32),
                        pltpu.VMEM((2, tm * PACK_ROWS, PACK_W), U32),
                        pltpu.VMEM((D, I), BF16), pltpu.VMEM((D, I), BF16), pltpu.VMEM((I, D), BF16),
                        pltpu.SemaphoreType.DMA((2,)), pltpu.SemaphoreType.DMA((2,))],
    )
    return pl.pallas_call(
        functools.partial(_ffn_kernel, tm=tm),
        grid_spec=grid_spec,
        out_shape=jax.ShapeDtypeStruct((nt * tm * PACK_ROWS, PACK_W), U32),
        compiler_params=_cparams(("arbitrary",)),
        name="expert_ffn",
    )(tile_e, n_used, row_tok, row_dst, h2p, wg, wu, wd)


def _combine_kernel(*refs, tm):
    ys_refs = refs[:TOP_K]
    wt_ref, x1_ref, h2_ref, wsg_ref, wsu_ref, wsd_ref, o_ref = refs[TOP_K:]
    h2 = h2_ref[...]
    hg = _dot(h2, wsg_ref[...])
    hu = _dot(h2, wsu_ref[...])
    base = x1_ref[...] + _dot((hg * _sigmoid(hg) * hu).astype(BF16), wsd_ref[...])
    wt = wt_ref[...]
    for c in range(PACK_ROWS):
        acc_lo = base[:, c * 2 * LANES:c * 2 * LANES + LANES]
        acc_hi = base[:, c * 2 * LANES + LANES:(c + 1) * 2 * LANES]
        for k in range(TOP_K):
            lo, hi = _unpack_rows(ys_refs[k], c, tm)
            wk = wt[:, k:k + 1]
            acc_lo = acc_lo + wk * lo
            acc_hi = acc_hi + wk * hi
        o_ref[:, c * 2 * LANES:c * 2 * LANES + LANES] = acc_lo
        o_ref[:, c * 2 * LANES + LANES:(c + 1) * 2 * LANES] = acc_hi


def _combine(ys, wts_t, x1, h2, wsg, wsu, wsd, tm):
    T, D = x1.shape
    K = wts_t.shape[1]
    I = wsg.shape[1]
    full = lambda shape: pl.BlockSpec(shape, lambda i: (0, 0))
    return pl.pallas_call(
        functools.partial(_combine_kernel, tm=tm),
        grid=(T // tm,),
        in_specs=[pl.BlockSpec((tm * PACK_ROWS, PACK_W), lambda i, k=k: (k * (T // tm) + i, 0))
                  for k in range(K)] + [
                  pl.BlockSpec((tm, K), lambda i: (i, 0)),
                  pl.BlockSpec((tm, D), lambda i: (i, 0)),
                  pl.BlockSpec((tm, D), lambda i: (i, 0)),
                  full((D, I)), full((D, I)), full((I, D))],
        out_specs=pl.BlockSpec((tm, D), lambda i: (i, 0)),
        out_shape=jax.ShapeDtypeStruct((T, D), F32),
        compiler_params=_cparams(("parallel",)),
        name="combine",
    )(*([ys] * K), wts_t, x1, h2, wsg, wsu, wsd)


def _ple_kernel(x_ref, p_ref, wp_ref, gpost_ref, gin_ref, wg_ref, o_ref, *, tm):
    hw = WO_ROWS
    groups = [slice(h * hw, (h + 1) * hw) for h in range(tm // hw)]
    wg = wg_ref[...]
    gates, ples = [], []
    for r in groups:
        hn = _rms(x_ref[r, :], gin_ref[...]).astype(BF16)
        gates.append(_dot(hn, wg))
        ples.append(_rms(_dot(p_ref[r, :].astype(BF16), wp_ref[...]), gpost_ref[...]))
    for r, gate, ple in zip(groups, gates, ples):
        o_ref[r, :] = x_ref[r, :] + _sigmoid(gate) * ple


def _ple(x2, p2, wp, gpost, gin, wg, tm):
    T, D = x2.shape
    Pd = p2.shape[1]
    full = lambda shape: pl.BlockSpec(shape, lambda i: (0, 0))
    return pl.pallas_call(
        functools.partial(_ple_kernel, tm=tm),
        grid=(T // tm,),
        in_specs=[
            pl.BlockSpec((tm, D), lambda i: (i, 0)),
            pl.BlockSpec((tm, Pd), lambda i: (i, 0)),
            full((Pd, D)), full((1, D)), full((1, D)),
            pl.BlockSpec((D, D), lambda i: (0, 0), pipeline_mode=pl.Buffered(1)),
        ],
        out_specs=pl.BlockSpec((tm, D), lambda i: (i, 0)),
        out_shape=jax.ShapeDtypeStruct((T, D), F32),
        compiler_params=_cparams(("parallel",)),
        name="ple",
    )(x2, p2, wp, gpost, gin, wg)


def _pad_cols(w, n):
    return jnp.pad(w, ((0, 0), (0, n - w.shape[1])))


def _pad_rows(w, n):
    return jnp.pad(w, ((0, n - w.shape[0]), (0, 0)))


def _layout_w_in(w):
    W = RWKV_WIDTH
    c = 3 * W
    segs = [w[:, 0:c],
            _pad_cols(w[:, c:c + DECAY_LORA], LORA_PAD),
            _pad_cols(w[:, c + DECAY_LORA:c + DECAY_LORA + AAA_LORA], LORA_PAD),
            _pad_cols(w[:, c + DECAY_LORA + AAA_LORA:c + DECAY_LORA + AAA_LORA + GATE_LORA], GATE_LORA_PAD)]
    c += DECAY_LORA + AAA_LORA + GATE_LORA
    mla_cols = Q_LORA + KV_LORA + QK_ROPE
    segs.append(_pad_cols(w[:, c:c + mla_cols], MLA_PAD))
    c += mla_cols
    segs.append(w[:, c:])
    out = jnp.concatenate(segs, axis=1).astype(BF16)
    assert out.shape[1] == IN_PAD
    return out


def _layout_mu(mu):
    W = RWKV_WIDTH
    c = 3 * W
    segs = [mu[0:c],
            jnp.pad(mu[c:c + DECAY_LORA], (0, LORA_PAD - DECAY_LORA)),
            jnp.pad(mu[c + DECAY_LORA:c + DECAY_LORA + AAA_LORA], (0, LORA_PAD - AAA_LORA)),
            jnp.pad(mu[c + DECAY_LORA + AAA_LORA:], (0, GATE_LORA_PAD - GATE_LORA))]
    return jnp.concatenate(segs).reshape(1, RWKV_PAD)


def _moe_tables(eidx, rnk, counts, tmf, tm_slots):
    K, T = eidx.shape
    E = N_EXPERTS
    A = K * T
    n_tiles = A // tmf + E
    P = n_tiles * tmf
    pcounts = (counts + tmf - 1) // tmf * tmf
    pends = jnp.cumsum(pcounts)
    pstart = (pends - pcounts).astype(I32)
    tile_start = jnp.arange(n_tiles, dtype=I32) * tmf
    tile_e = jnp.minimum(jnp.sum((pends[None, :] <= tile_start[:, None]).astype(I32), axis=1), E - 1)
    n_used = (pends[E - 1:] // tmf).astype(I32)
    dest = _slots(eidx, rnk, pstart, tm_slots)
    j = jnp.arange(tmf, dtype=I32)[None, :]
    tail = j < (pcounts - counts)[:, None]
    spare = jnp.cumsum(jnp.logical_not(tail).reshape(-1).astype(I32)) - 1
    pad_row = jnp.where(tail.reshape(-1), ((pstart + counts)[:, None] + j).reshape(-1),
                        pends[E - 1] + spare)
    q = jnp.arange(P - A, dtype=I32)
    slot_id = jnp.arange(A, dtype=I32)
    rows = jnp.concatenate([dest.reshape(-1), pad_row.astype(I32)])
    slots = jnp.concatenate([slot_id, A + q])
    row_dst = lax.sort((rows, slots), num_keys=1)[1]
    row_tok = jnp.where(row_dst < A, row_dst % T, 0)
    return (row_tok.reshape(n_tiles, 1, tmf), row_dst.reshape(n_tiles, 1, tmf), tile_e, n_used)


def _layer(x, p, positions, g_mix, w_in, mu_rwkv, w0, w2, a0, a2, g2, k_k, k_a, r_k,
           gn_w, gn_b, w_a_up, g_qa, g_kva, w_uq, w_ukv, g_qn, g_kn, w_b_up, w_o,
           g_ffn, w_router, router_bias, w_exp_gate, w_exp_up, w_exp_down,
           w_sh_gate, w_sh_up, w_sh_down, w_ple, g_ple_post, g_ple_in, w_ple_gate,
           *, tiles):
    B, S, D = x.shape
    T = B * S
    W = RWKV_WIDTH
    x2 = x.reshape(T, D)
    row = lambda v: v.reshape(1, -1).astype(F32)

    proj = _inproj(x2, row(g_mix), _layout_w_in(w_in), tiles["tm_in"], tiles["tn_in"])

    head_blk = (jnp.arange(W)[:, None] // RWKV_HEAD == jnp.arange(W)[None, :] // RWKV_HEAD).astype(BF16)
    r, k, v, kk, a, lw, g = _rwkv_prep(
        proj, _layout_mu(mu_rwkv), row(w0), _pad_rows(w2, LORA_PAD).astype(BF16), row(a0),
        _pad_rows(a2, LORA_PAD).astype(BF16), _pad_rows(g2, GATE_LORA_PAD).astype(BF16),
        row(k_k), row(k_a), head_blk, tiles["tm_prep"], S)
    b3 = lambda t: t.reshape(B, S, W)
    o_a = _rwkv_scan(b3(r), b3(k), b3(v), b3(kk), b3(a), b3(lw), b3(g),
                     row(gn_w), row(gn_b), row(r_k), tiles["rows_scan"]).reshape(T, W)

    half = QK_ROPE // 2
    inv_freq = ROPE_THETA ** (-jnp.arange(half, dtype=F32) / half)
    invf = jnp.concatenate([inv_freq, inv_freq, jnp.zeros((LANES - QK_ROPE,), F32)]).reshape(1, LANES)
    wuq = jnp.pad(w_uq.reshape(Q_LORA, MLA_HEADS, QK_HEAD),
                  ((0, 0), (0, 0), (0, QK_PAD - QK_HEAD))).reshape(Q_LORA, MLA_HEADS * QK_PAD)
    pad_g = lambda gv: jnp.pad(gv, (0, QK_PAD - QK_HEAD)).reshape(1, QK_PAD)
    q, kx, vx = _mla_prep(proj, positions.reshape(T, 1).astype(I32), invf, row(g_qa), row(g_kva),
                          wuq.astype(BF16), w_ukv.astype(BF16), pad_g(g_qn), pad_g(g_kn),
                          tiles["tm_mla"])
    o_b = _flash(q.reshape(B, S, -1), kx.reshape(B, S, -1), vx, tiles["tq"]).reshape(T, MLA_WIDTH)

    merged = _merge(o_a, o_b, w_a_up.astype(BF16), w_b_up.astype(BF16), proj,
                    tiles["tm_merge"], tiles["tn_merge"])

    x1, h2, h2p, logits_t = _wo(merged, x2, w_o.astype(BF16), row(g_ffn),
                                jnp.transpose(w_router).astype(F32), tiles["tm_wo"])
    eidx, wts, rnk, cnt = _route(logits_t, router_bias.reshape(N_EXPERTS, 1).astype(F32),
                                 tiles["tm_route"])
    tmf = tiles["tm_ffn"]
    counts = cnt[:, 0].astype(I32)
    row_tok, row_dst, tile_e, n_used = _moe_tables(eidx, rnk, counts, tmf, tiles["tm_slots"])
    ys = _ffn(tile_e, n_used, row_tok, row_dst, h2p, w_exp_gate, w_exp_up, w_exp_down, tmf)
    x3 = _combine(ys, jnp.transpose(wts), x1, h2,
                  w_sh_gate.astype(BF16), w_sh_up.astype(BF16), w_sh_down.astype(BF16),
                  tiles["tm_comb"])

    out = _ple(x3, p.reshape(T, PLE_DIM), w_ple.astype(BF16), row(g_ple_post), row(g_ple_in),
               w_ple_gate.astype(BF16), tiles["tm_ple"])
    return out.reshape(B, S, D)


TILES = dict(tm_in=1024, tn_in=1024, tm_prep=256, rows_scan=128, tm_mla=1024, tq=512,
             tm_merge=1024, tn_merge=1024, tm_wo=512, tm_route=512, tm_ffn=256,
             tm_slots=2048, tm_comb=256, tm_ple=512)


def kernel(x, p, positions, g_mix, w_in, mu_rwkv, w0, w2, a0, a2, g2, k_k, k_a, r_k, gn_w, gn_b, w_a_up, g_qa, g_kva, w_uq, w_ukv, g_qn, g_kn, w_b_up, w_o, g_ffn, w_router, router_bias, w_exp_gate, w_exp_up, w_exp_down, w_sh_gate, w_sh_up, w_sh_down, w_ple, g_ple_post, g_ple_in, w_ple_gate):
    args = (g_mix, w_in, mu_rwkv, w0, w2, a0, a2, g2, k_k, k_a, r_k, gn_w, gn_b, w_a_up,
            g_qa, g_kva, w_uq, w_ukv, g_qn, g_kn, w_b_up, w_o, g_ffn, w_router, router_bias,
            w_exp_gate, w_exp_up, w_exp_down, w_sh_gate, w_sh_up, w_sh_down, w_ple,
            g_ple_post, g_ple_in, w_ple_gate)
    assert all(t.shape[0] == 1 for t in args), "single-layer stack expected"
    return _layer(x, p[0], positions, *[t[0] for t in args], tiles=TILES)
```

```python
import functools

import jax
import jax.numpy as jnp
from jax import lax
from jax.experimental import pallas as pl
from jax.experimental.pallas import tpu as pltpu

F32 = jnp.float32
BF16 = jnp.bfloat16
I32 = jnp.int32
U32 = jnp.uint32

D_MODEL = 2048
PLE_DIM = 256
NORM_EPS = 1e-6
RWKV_HEADS = 16
RWKV_HEAD = 64
RWKV_WIDTH = RWKV_HEADS * RWKV_HEAD
DECAY_LORA = 64
AAA_LORA = 64
GATE_LORA = 160
GN_EPS = 64e-5
MLA_HEADS = 8
Q_LORA = 512
KV_LORA = 512
QK_NOPE = 128
QK_ROPE = 64
QK_HEAD = QK_NOPE + QK_ROPE
V_HEAD = 128
MLA_WIDTH = MLA_HEADS * V_HEAD
ROPE_THETA = 10000.0
N_EXPERTS = 64
TOP_K = 8
N_GROUPS = 8
GROUP_SIZE = N_EXPERTS // N_GROUPS
TOPK_GROUPS = 4
MOE_INTER = 512
ROUTED_SCALE = 2.5

LANES = 128
QK_PAD = 2 * LANES
PACK_ROWS = 8
PACK_W = D_MODEL // 2 // PACK_ROWS
assert PACK_W == LANES

LORA_PAD = 128
GATE_LORA_PAD = 256
RWKV_PAD = 3 * RWKV_WIDTH + 2 * LORA_PAD + GATE_LORA_PAD
MLA_PAD = 1536
COL_RWKV = 0
COL_MLA = RWKV_PAD
COL_GA = COL_MLA + MLA_PAD
COL_GB = COL_GA + D_MODEL
IN_PAD = COL_GB + D_MODEL

VMEM_LIMIT = 56 * 1024 * 1024
CHUNK = 64
WO_ROWS = 256
SCAN_LANE_BLOCKS = 8
MLA_ROWS = 128


def _cparams(sem):
    return pltpu.CompilerParams(dimension_semantics=sem, vmem_limit_bytes=VMEM_LIMIT)


def _dot(a, b):
    return jnp.dot(a, b, preferred_element_type=F32)


def _dot_nt(a, b):
    return lax.dot_general(a, b, (((1,), (1,)), ((), ())), preferred_element_type=F32)


def _dot_tn(a, b):
    return lax.dot_general(a, b, (((0,), (0,)), ((), ())), preferred_element_type=F32)


def _split_bf16(x):
    hi = x.astype(BF16)
    lo = (x - hi.astype(F32)).astype(BF16)
    return hi, lo


def _rms(x, g):
    ms = jnp.mean(x * x, axis=-1, keepdims=True)
    return x * lax.rsqrt(ms + NORM_EPS) * g


def _sigmoid(x):
    return 1.0 / (1.0 + jnp.exp(-x))


def _inproj_kernel(x_ref, g_ref, w_ref, o_ref, h_scr):
    @pl.when(pl.program_id(1) == 0)
    def _():
        h_scr[...] = _rms(x_ref[...], g_ref[...]).astype(BF16)

    o_ref[...] = _dot(h_scr[...], w_ref[...]).astype(o_ref.dtype)


def _inproj(x2, g, w, tm, tn):
    T, D = x2.shape
    N = w.shape[1]
    return pl.pallas_call(
        _inproj_kernel,
        grid=(T // tm, N // tn),
        in_specs=[
            pl.BlockSpec((tm, D), lambda i, j: (i, 0)),
            pl.BlockSpec((1, D), lambda i, j: (0, 0)),
            pl.BlockSpec((D, tn), lambda i, j: (0, j)),
        ],
        out_specs=pl.BlockSpec((tm, tn), lambda i, j: (i, j)),
        out_shape=jax.ShapeDtypeStruct((T, N), BF16),
        scratch_shapes=[pltpu.VMEM((tm, D), BF16)],
        compiler_params=_cparams(("parallel", "arbitrary")),
        name="inproj",
    )(x2, g, w)


HALO = 16


def _rwkv_prep_kernel(cur_ref, halo_ref, mu_ref, w0_ref, w2_ref, a0_ref, a2_ref, g2_ref,
                      kk_ref, ka_ref, bd_ref,
                      r_out, k_out, v_out, kk_out, a_out, lw_out, g_out, *, tm, seq):
    i = pl.program_id(0)
    cur = cur_ref[...].astype(F32)
    first = (i * tm) % seq == 0
    last_prev = halo_ref[HALO - 1:HALO, :].astype(F32)
    last_prev = jnp.where(first, 0.0, last_prev)
    row = lax.broadcasted_iota(I32, cur.shape, 0)
    prev = jnp.where(row == 0, last_prev, pltpu.roll(cur, 1, 0))
    u = cur + (prev - cur) * mu_ref[...]
    W = RWKV_WIDTH
    r = u[:, 0:W]
    k = u[:, W:2 * W]
    v = u[:, 2 * W:3 * W]
    dw = u[:, 3 * W:3 * W + LORA_PAD]
    da = u[:, 3 * W + LORA_PAD:3 * W + 2 * LORA_PAD]
    dg = u[:, 3 * W + 2 * LORA_PAD:3 * W + 2 * LORA_PAD + GATE_LORA_PAD]
    z = w0_ref[...] + _dot(jnp.tanh(dw).astype(BF16), w2_ref[...])
    w_log = -(jnp.maximum(-z, 0.0) + jnp.log(1.0 + jnp.exp(-jnp.abs(z)))) - 0.5
    lw_out[...] = -jnp.exp(w_log)
    a = _sigmoid(a0_ref[...] + _dot(da.astype(BF16), a2_ref[...]))
    g_out[...] = _dot(_sigmoid(dg).astype(BF16), g2_ref[...]).astype(g_out.dtype)
    kk = k * kk_ref[...]
    hi, lo = _split_bf16(kk * kk)
    ssq = _dot(hi, bd_ref[...]) + _dot(lo, bd_ref[...])
    kk = kk / jnp.maximum(jnp.sqrt(ssq), 1e-12)
    r_out[...] = r.astype(r_out.dtype)
    k_out[...] = (k * (1.0 + (a - 1.0) * ka_ref[...])).astype(k_out.dtype)
    v_out[...] = v.astype(v_out.dtype)
    kk_out[...] = kk.astype(kk_out.dtype)
    a_out[...] = a.astype(a_out.dtype)


def _rwkv_prep(proj, mu, w0, w2, a0, a2, g2, k_k, k_a, bd, tm, seq):
    T = proj.shape[0]
    W = RWKV_WIDTH
    full = lambda shape: pl.BlockSpec(shape, lambda i: (0, 0))
    out = lambda: pl.BlockSpec((tm, W), lambda i: (i, 0))
    return pl.pallas_call(
        functools.partial(_rwkv_prep_kernel, tm=tm, seq=seq),
        grid=(T // tm,),
        in_specs=[
            pl.BlockSpec((tm, RWKV_PAD), lambda i: (i, 0)),
            pl.BlockSpec((HALO, RWKV_PAD), lambda i: (jnp.maximum(i * (tm // HALO) - 1, 0), 0)),
            full((1, RWKV_PAD)), full((1, W)), full((LORA_PAD, W)), full((1, W)),
            full((LORA_PAD, W)), full((GATE_LORA_PAD, W)), full((1, W)), full((1, W)),
            full((W, W)),
        ],
        out_specs=[out() for _ in range(7)],
        out_shape=[jax.ShapeDtypeStruct((T, W), dt)
                   for dt in (BF16, BF16, BF16, BF16, BF16, F32, BF16)],
        compiler_params=_cparams(("parallel",)),
        name="rwkv_prep",
    )(proj, proj, mu, w0, w2, a0, a2, g2, k_k, k_a, bd)


def _rwkv_scan_kernel(r_ref, k_ref, v_ref, kk_ref, a_ref, lw_ref, g_ref,
                      gnw_ref, gnb_ref, rk_ref, o_ref, s_scr, *, rows):
    L = CHUNK
    L2 = 2 * L

    @pl.when(pl.program_id(2) == 0)
    def _():
        s_scr[...] = jnp.zeros_like(s_scr)

    lane = lax.broadcasted_iota(I32, (L, LANES), 1)
    m_a = (lane < RWKV_HEAD).astype(F32)
    m_b = 1.0 - m_a
    ri = lax.broadcasted_iota(I32, (L2, L2), 0)
    ci = lax.broadcasted_iota(I32, (L2, L2), 1)
    strict = ri > ci
    incl = ri >= ci
    eye = (ri == ci).astype(F32)
    tri = (lax.broadcasted_iota(I32, (L, L), 0) >= lax.broadcasted_iota(I32, (L, L), 1)).astype(BF16)
    hr = lax.broadcasted_iota(I32, (LANES, LANES), 0) // RWKV_HEAD
    hc = lax.broadcasted_iota(I32, (LANES, LANES), 1) // RWKV_HEAD
    head_ones = (hr == hc).astype(BF16)
    nb = SCAN_LANE_BLOCKS

    def stack(x):
        return jnp.concatenate([x * m_a, x * m_b], axis=0)

    chunks = range(rows // L)
    rows_of = lambda c: slice(c * L, (c + 1) * L)
    lanes_of = lambda h: slice(h * LANES, (h + 1) * LANES)
    each = lambda f, *lists: [f(*vals) for vals in zip(*lists)]

    def operands(c, h):
        sl, ln = rows_of(c), lanes_of(h)
        r = r_ref[0, sl, ln].astype(F32)
        k = k_ref[0, sl, ln].astype(F32)
        v = v_ref[0, sl, ln].astype(F32)
        kk = kk_ref[0, sl, ln].astype(F32)
        a = a_ref[0, sl, ln].astype(F32)
        lw = lw_ref[0, sl, ln]
        hi, lo = _split_bf16(lw)
        cum = _dot(tri, hi) + _dot(tri, lo)
        c_end = cum[L - 1:L, :]
        p_inv = jnp.exp(-cum)
        p_end = jnp.exp(c_end - cum)
        b = kk * a
        return dict(
            x_a=stack(-(kk * jnp.exp(cum - lw))).astype(BF16),
            x_r=stack(r * jnp.exp(cum)),
            x_bk=jnp.concatenate([stack(b * p_inv), stack(k * p_inv)], axis=0).astype(BF16),
            v_st=stack(v).astype(BF16),
            z_hat=jnp.concatenate([stack(b * p_end), stack(k * p_end)], axis=0).astype(BF16),
            p_row=jnp.exp(c_end),
            rkr=(r * k * rk_ref[:, ln]).astype(BF16), v=v)

    ops = [operands(c, h) for c in chunks for h in range(nb)]
    big = [_dot_nt(jnp.concatenate([o["x_a"], o["x_r"].astype(BF16)], axis=0), o["x_bk"])
           for o in ops]
    n_pow = [jnp.where(strict, m[0:L2, 0:L2], 0.0) for m in big]
    a_ak = [jnp.where(strict, m[0:L2, L2:2 * L2], 0.0).astype(BF16) for m in big]
    a_r = [jnp.concatenate([jnp.where(incl, m[L2:2 * L2, 0:L2], 0.0),
                            jnp.where(incl, m[L2:2 * L2, L2:2 * L2], 0.0)], axis=1).astype(BF16)
           for m in big]

    t_inv = [eye + n for n in n_pow]
    for _ in range(5):
        n_pow = each(lambda n: _dot(n.astype(BF16), n.astype(BF16)), n_pow)
        t_inv = each(lambda t, n: t + _dot(t.astype(BF16), n.astype(BF16)), t_inv, n_pow)

    akv = each(lambda m, o: _dot(m, o["v_st"]).astype(BF16), a_ak, ops)
    w = each(lambda t, o, x: _dot(t.astype(BF16), jnp.concatenate([o["x_a"], x], axis=1)),
             t_inv, ops, akv)
    w1 = [m[:, 0:LANES].astype(BF16) for m in w]
    w2v = each(lambda m, o: jnp.concatenate([m[:, LANES:].astype(BF16), o["v_st"]], axis=0), w, ops)
    g_mat = each(lambda o, m, x: (o["x_r"] + _dot(m[:, 0:L2], x)).astype(BF16), ops, a_r, w1)
    y0 = each(_dot, a_r, w2v)
    m_mat = each(lambda x, o: _dot_tn(x, o["z_hat"][0:L2, :]).astype(BF16), w1, ops)
    c2 = each(lambda x, o: _dot_tn(x, o["z_hat"]), w2v, ops)
    bonus_v = [_dot(o["rkr"], head_ones) * o["v"] for o in ops]

    s = [s_scr[h] for h in range(nb)]
    y = []
    for i in range(len(ops)):
        h = i % nb
        s_b = s[h].astype(BF16)
        y_st = _dot_nt(g_mat[i], s_b) + y0[i]
        s[h] = s[h] * ops[i]["p_row"] + _dot(s_b, m_mat[i]) + c2[i]
        y.append(y_st[0:L, :] + y_st[L:L2, :])
    for h in range(nb):
        s_scr[h] = s[h]

    mean = [_dot(t.astype(BF16), head_ones) * (1.0 / RWKV_HEAD) for t in y]
    yc = each(lambda t, m: t - m, y, mean)
    var = [_dot((t * t).astype(BF16), head_ones) * (1.0 / RWKV_HEAD) for t in yc]
    for i in range(len(ops)):
        sl, ln = rows_of(i // nb), lanes_of(i % nb)
        yn = yc[i] * lax.rsqrt(var[i] + GN_EPS) * gnw_ref[:, ln] + gnb_ref[:, ln]
        o_ref[0, sl, ln] = ((yn + bonus_v[i]) * g_ref[0, sl, ln].astype(F32)).astype(o_ref.dtype)


def _rwkv_scan(r, k, v, kk, a, lw, g, gn_w, gn_b, r_k, rows):
    B, S, W = r.shape
    wb = SCAN_LANE_BLOCKS * LANES
    npair = W // wb
    seq = lambda: pl.BlockSpec((1, rows, wb), lambda b, p, c: (b, c, p))
    par = lambda: pl.BlockSpec((1, wb), lambda b, p, c: (0, p))
    return pl.pallas_call(
        functools.partial(_rwkv_scan_kernel, rows=rows),
        grid=(B, npair, S // rows),
        in_specs=[seq() for _ in range(7)] + [par(), par(), par()],
        out_specs=seq(),
        out_shape=jax.ShapeDtypeStruct((B, S, W), BF16),
        scratch_shapes=[pltpu.VMEM((SCAN_LANE_BLOCKS, LANES, LANES), F32)],
        compiler_params=_cparams(("parallel", "parallel", "arbitrary")),
        name="rwkv_scan",
    )(r, k, v, kk, a, lw, g, gn_w, gn_b, r_k)


def _rope128(x, cos, sin_signed):
    lane = lax.broadcasted_iota(I32, x.shape, 1)
    half = QK_ROPE // 2
    partner = jnp.where(lane < half, pltpu.roll(x, LANES - half, 1), pltpu.roll(x, half, 1))
    return x * cos + partner * sin_signed


def _mla_prep_kernel(cq_ref, ckv_ref, kr_ref, pos_ref, invf_ref, gqa_ref, gkva_ref,
                     wuq_ref, wukv_ref, gqn_ref, gkn_ref,
                     q_out, k_out, v_out, cqn_scr, ckvn_scr, cos_scr, sin_scr):
    @pl.when(pl.program_id(1) == 0)
    def _():
        cqn_scr[...] = _rms(cq_ref[...].astype(F32), gqa_ref[...]).astype(BF16)
        ckvn_scr[...] = _rms(ckv_ref[...].astype(F32), gkva_ref[...]).astype(BF16)
        ang = pos_ref[...].astype(F32) * invf_ref[...]
        lane = lax.broadcasted_iota(I32, ang.shape, 1)
        cos_scr[...] = jnp.cos(ang)
        sin_scr[...] = jnp.where(lane < QK_ROPE // 2, -1.0, 1.0) * jnp.sin(ang)

    tm = q_out.shape[0]
    hw = min(MLA_ROWS, tm)
    groups = [slice(g * hw, (g + 1) * hw) for g in range(tm // hw)]
    qs = [_dot(cqn_scr[r, :], wuq_ref[...]) for r in groups]
    kvs = [_dot(ckvn_scr[r, :], wukv_ref[...]) for r in groups]
    inv_n = 1.0 / QK_HEAD
    gkn = gkn_ref[...]
    for r, q, kv in zip(groups, qs, kvs):
        cos = cos_scr[r, :]
        sin = sin_scr[r, :]
        rs = lax.rsqrt(jnp.sum(q * q, axis=-1, keepdims=True) * inv_n + NORM_EPS)
        q = q * rs * gqn_ref[...] * (QK_HEAD ** -0.5 * LOG2E)
        q_out[r, :] = jnp.concatenate(
            [q[:, 0:LANES], _rope128(q[:, LANES:], cos, sin)], axis=1).astype(q_out.dtype)
        k_nope = kv[:, 0:QK_NOPE]
        kr = kr_ref[r, :].astype(F32)
        ssq = (jnp.sum(k_nope * k_nope, axis=-1, keepdims=True)
               + jnp.sum(kr * kr, axis=-1, keepdims=True))
        rs = lax.rsqrt(ssq * inv_n + NORM_EPS)
        k_out[r, :] = jnp.concatenate(
            [k_nope * rs * gkn[:, 0:LANES], _rope128(kr * rs * gkn[:, LANES:], cos, sin)],
            axis=1).astype(k_out.dtype)
        v_out[:, r] = jnp.transpose(kv[:, QK_NOPE:].astype(v_out.dtype))


def _mla_prep(proj, pos, invf, g_qa, g_kva, wuq, wukv, gqn, gkn, tm):
    T = proj.shape[0]
    H = MLA_HEADS
    c0 = COL_MLA // Q_LORA
    ckr = (COL_MLA + Q_LORA + KV_LORA) // LANES
    full = lambda shape: pl.BlockSpec(shape, lambda i, h: (0, 0))
    return pl.pallas_call(
        _mla_prep_kernel,
        grid=(T // tm, H),
        in_specs=[
            pl.BlockSpec((tm, Q_LORA), lambda i, h: (i, c0)),
            pl.BlockSpec((tm, KV_LORA), lambda i, h: (i, c0 + 1)),
            pl.BlockSpec((tm, LANES), lambda i, h: (i, ckr)),
            pl.BlockSpec((tm, 1), lambda i, h: (i, 0)),
            full((1, LANES)), full((1, Q_LORA)), full((1, KV_LORA)),
            pl.BlockSpec((Q_LORA, QK_PAD), lambda i, h: (0, h)),
            pl.BlockSpec((KV_LORA, QK_NOPE + V_HEAD), lambda i, h: (0, h)),
            full((1, QK_PAD)), full((1, QK_PAD)),
        ],
        out_specs=[
            pl.BlockSpec((tm, QK_PAD), lambda i, h: (i, h)),
            pl.BlockSpec((tm, QK_PAD), lambda i, h: (i, h)),
            pl.BlockSpec((V_HEAD, tm), lambda i, h: (h, i)),
        ],
        out_shape=[jax.ShapeDtypeStruct((T, H * QK_PAD), BF16),
                   jax.ShapeDtypeStruct((T, H * QK_PAD), BF16),
                   jax.ShapeDtypeStruct((H * V_HEAD, T), BF16)],
        scratch_shapes=[pltpu.VMEM((tm, Q_LORA), BF16), pltpu.VMEM((tm, KV_LORA), BF16),
                        pltpu.VMEM((tm, LANES), F32), pltpu.VMEM((tm, LANES), F32)],
        compiler_params=_cparams(("parallel", "arbitrary")),
        name="mla_prep",
    )(proj, proj, proj, pos, invf, g_qa, g_kva, wuq, wukv, gqn, gkn)


MASK_VALUE = -1e30
LOG2E = 1.4426950408889634


def _flash_kernel(q_ref, k_ref, vt_ref, o_ref, s_scr, p_scr, alpha_scr, m_scr, l_scr, acc_scr,
                  *, tq):
    i = pl.program_id(2)
    q = q_ref[0]
    m_scr[...] = jnp.full_like(m_scr, MASK_VALUE)
    l_scr[...] = jnp.zeros_like(l_scr)
    acc_scr[...] = jnp.zeros_like(acc_scr)

    q_half = tq // 2

    def scores(j, slot, half):
        ks = pl.ds(pl.multiple_of(j * tq, tq), tq)
        cols = slice(half * q_half, (half + 1) * q_half)
        s_scr[slot, :, cols] = _dot_nt(k_ref[0, ks, :], q[cols, :])

    def softmax(slot, masked, strips):
        for c in strips:
            cols = slice(c * LANES, (c + 1) * LANES)
            s = s_scr[slot, :, cols]
            if masked:
                kpos = lax.broadcasted_iota(I32, s.shape, 0)
                qpos = lax.broadcasted_iota(I32, s.shape, 1) + c * LANES
                s = jnp.where(kpos <= qpos, s, MASK_VALUE)
            m_old = m_scr[:, cols]
            m_new = jnp.maximum(m_old, jnp.max(s, axis=0, keepdims=True))
            p = jnp.exp2(s - m_new)
            alpha = jnp.exp2(m_old - m_new)
            l_scr[:, cols] = alpha * l_scr[:, cols] + jnp.sum(p, axis=0, keepdims=True)
            m_scr[:, cols] = m_new
            p_scr[slot, :, cols] = p.astype(BF16)
            alpha_scr[slot, :, cols] = alpha

    def weighted_values(j, slot):
        vt = vt_ref[:, pl.ds(pl.multiple_of(j * tq, tq), tq)]
        acc_scr[...] = alpha_scr[slot] * acc_scr[...] + _dot(vt, p_scr[slot])

    scores(0, 0, 0)
    scores(0, 0, 1)
    p_scr[1] = jnp.zeros((tq, tq), BF16)
    alpha_scr[1] = jnp.ones((1, tq), F32)
    n_strips = tq // LANES
    first, second = range(0, n_strips // 2), range(n_strips // 2, n_strips)

    def step(j, slot):
        weighted_values(jnp.maximum(j - 1, 0), 1 - slot)
        scores(j + 1, 1 - slot, 0)
        softmax(slot, False, first)
        scores(j + 1, 1 - slot, 1)
        softmax(slot, False, second)

    def pair(jp, carry):
        step(2 * jp, 0)
        step(2 * jp + 1, 1)
        return carry

    def finish(slot):
        weighted_values(jnp.maximum(i - 1, 0), 1 - slot)
        softmax(slot, True, range(n_strips))
        weighted_values(i, slot)
        o_ref[0] = jnp.transpose(acc_scr[...] / l_scr[...]).astype(o_ref.dtype)

    lax.fori_loop(0, i // 2, pair, 0)

    @pl.when(i % 2 == 0)
    def _():
        finish(0)

    @pl.when(i % 2 == 1)
    def _():
        step(i - 1, 0)
        finish(1)


def _flash(q, k, vt, tq):
    B, S, _ = q.shape
    H = MLA_HEADS
    return pl.pallas_call(
        functools.partial(_flash_kernel, tq=tq),
        grid=(B, H, S // tq),
        in_specs=[
            pl.BlockSpec((1, tq, QK_PAD), lambda b, h, i: (b, i, h)),
            pl.BlockSpec((1, S, QK_PAD), lambda b, h, i: (b, 0, h)),
            pl.BlockSpec((V_HEAD, S), lambda b, h, i: (h, b)),
        ],
        out_specs=pl.BlockSpec((1, tq, V_HEAD), lambda b, h, i: (b, i, h)),
        out_shape=jax.ShapeDtypeStruct((B, S, H * V_HEAD), BF16),
        scratch_shapes=[pltpu.VMEM((2, tq, tq), F32), pltpu.VMEM((2, tq, tq), BF16),
                        pltpu.VMEM((2, 1, tq), F32),
                        pltpu.VMEM((1, tq), F32), pltpu.VMEM((1, tq), F32),
                        pltpu.VMEM((V_HEAD, tq), F32)],
        compiler_params=_cparams(("parallel", "parallel", "arbitrary")),
        name="flash_attn",
    )(q, k, vt)


def _merge_kernel(oa_ref, ob_ref, wa_ref, wb_ref, ga_ref, gb_ref, o_ref):
    ya = _dot(oa_ref[...], wa_ref[...])
    yb = _dot(ob_ref[...], wb_ref[...])
    ga = _sigmoid(ga_ref[...].astype(F32))
    gb = _sigmoid(gb_ref[...].astype(F32))
    o_ref[...] = (ga * ya + gb * yb).astype(o_ref.dtype)


def _merge(oa, ob, wa, wb, proj, tm, tn):
    T, K = oa.shape
    N = wa.shape[1]
    ca = COL_GA // tn
    cb = COL_GB // tn
    return pl.pallas_call(
        _merge_kernel,
        grid=(T // tm, N // tn),
        in_specs=[
            pl.BlockSpec((tm, K), lambda i, j: (i, 0)),
            pl.BlockSpec((tm, K), lambda i, j: (i, 0)),
            pl.BlockSpec((K, tn), lambda i, j: (0, j)),
            pl.BlockSpec((K, tn), lambda i, j: (0, j)),
            pl.BlockSpec((tm, tn), lambda i, j: (i, ca + j)),
            pl.BlockSpec((tm, tn), lambda i, j: (i, cb + j)),
        ],
        out_specs=pl.BlockSpec((tm, tn), lambda i, j: (i, j)),
        out_shape=jax.ShapeDtypeStruct((T, N), BF16),
        compiler_params=_cparams(("parallel", "parallel")),
        name="merge",
    )(oa, ob, wa, wb, proj, proj)


def _pack_rows(x, out_ref, n, row0=0):
    for c in range(PACK_ROWS):
        lo = x[:, c * 2 * LANES:c * 2 * LANES + LANES]
        hi = x[:, c * 2 * LANES + LANES:(c + 1) * 2 * LANES]
        lo_b = pltpu.bitcast(lo.astype(BF16).astype(F32), U32)
        hi_b = pltpu.bitcast(hi.astype(BF16).astype(F32), U32)
        out_ref[pl.ds(row0 * PACK_ROWS + c, n, stride=PACK_ROWS), :] = hi_b | (lo_b >> 16)


def _unpack_rows(ref, c, tm, lead=None):
    idx = (pl.ds(c, tm, stride=PACK_ROWS), slice(None))
    if lead is not None:
        idx = (lead,) + idx
    u = ref[idx]
    lo = pltpu.bitcast(u << 16, F32)
    hi = pltpu.bitcast(u & jnp.uint32(0xFFFF0000), F32)
    return lo, hi


def _wo_kernel(m_ref, x_ref, wo_ref, g_ref, wr_ref, x1_ref, h2_ref, h2p_ref, lg_ref, *, tm):
    hw = WO_ROWS
    groups = [slice(h * hw, (h + 1) * hw) for h in range(tm // hw)]
    wo = wo_ref[...]
    x1 = [x_ref[r, :] + _dot(m_ref[r, :], wo) for r in groups]
    h2 = [_rms(v, g_ref[...]) for v in x1]
    w_hi, w_lo = _split_bf16(wr_ref[...])
    for h, r in enumerate(groups):
        x1_ref[r, :] = x1[h]
        h2_ref[r, :] = h2[h].astype(BF16)
        _pack_rows(h2[h], h2p_ref, hw, row0=h * hw)
        h_hi, h_lo = _split_bf16(h2[h])
        lg_ref[:, r] = _dot_nt(w_hi, h_hi) + _dot_nt(w_hi, h_lo) + _dot_nt(w_lo, h_hi)


def _wo(merged, x2, wo, g, wr_t, tm):
    T, D = x2.shape
    E = wr_t.shape[0]
    return pl.pallas_call(
        functools.partial(_wo_kernel, tm=tm),
        grid=(T // tm,),
        in_specs=[
            pl.BlockSpec((tm, D), lambda i: (i, 0)),
            pl.BlockSpec((tm, D), lambda i: (i, 0)),
            pl.BlockSpec((D, D), lambda i: (0, 0), pipeline_mode=pl.Buffered(1)),
            pl.BlockSpec((1, D), lambda i: (0, 0)),
            pl.BlockSpec((E, D), lambda i: (0, 0)),
        ],
        out_specs=[
            pl.BlockSpec((tm, D), lambda i: (i, 0)),
            pl.BlockSpec((tm, D), lambda i: (i, 0)),
            pl.BlockSpec((tm * PACK_ROWS, PACK_W), lambda i: (i, 0)),
            pl.BlockSpec((E, tm), lambda i: (0, i)),
        ],
        out_shape=[jax.ShapeDtypeStruct((T, D), F32),
                   jax.ShapeDtypeStruct((T, D), BF16),
                   jax.ShapeDtypeStruct((T * PACK_ROWS, PACK_W), U32),
                   jax.ShapeDtypeStruct((E, T), F32)],
        compiler_params=_cparams(("parallel",)),
        name="wo_norm_router",
    )(merged, x2, wo, g, wr_t)


def _route_kernel(lg_ref, bias_ref, eidx_ref, wts_ref, rnk_ref, cnt_ref, carry_scr, *, tm):
    E = N_EXPERTS
    NEG = -jnp.inf

    @pl.when(pl.program_id(0) == 0)
    def _():
        carry_scr[...] = jnp.zeros_like(carry_scr)

    scores = _sigmoid(lg_ref[...])
    choice = scores + bias_ref[...]
    c3 = choice.reshape(N_GROUPS, GROUP_SIZE, tm)
    sub = lax.broadcasted_iota(I32, c3.shape, 1)
    m1 = jnp.max(c3, axis=1, keepdims=True)
    i1 = jnp.min(jnp.where(c3 == m1, sub, GROUP_SIZE), axis=1, keepdims=True)
    m2 = jnp.max(jnp.where(sub == i1, NEG, c3), axis=1, keepdims=True)
    gs = (m1 + m2).reshape(N_GROUPS, tm)
    gi = lax.broadcasted_iota(I32, gs.shape, 0)
    gsel = jnp.zeros(gs.shape, F32)
    for _ in range(TOPK_GROUPS):
        mx = jnp.max(gs, axis=0, keepdims=True)
        ix = jnp.min(jnp.where(gs == mx, gi, N_GROUPS), axis=0, keepdims=True)
        hit = gi == ix
        gsel = jnp.where(hit, 1.0, gsel)
        gs = jnp.where(hit, NEG, gs)
    emask = jnp.broadcast_to(gsel.reshape(N_GROUPS, 1, tm), (N_GROUPS, GROUP_SIZE, tm)).reshape(E, tm)
    x = jnp.where(emask > 0.5, choice, NEG)
    ei = lax.broadcasted_iota(I32, x.shape, 0)
    sel = jnp.zeros(x.shape, F32)
    idx_rows, w_rows = [], []
    for _ in range(TOP_K):
        mx = jnp.max(x, axis=0, keepdims=True)
        ix = jnp.min(jnp.where(x == mx, ei, E), axis=0, keepdims=True)
        hit = ei == ix
        w_rows.append(jnp.sum(jnp.where(hit, scores, 0.0), axis=0, keepdims=True))
        idx_rows.append(ix)
        sel = jnp.where(hit, 1.0, sel)
        x = jnp.where(hit, NEG, x)
    w = jnp.concatenate(w_rows, axis=0)
    w = w / (jnp.sum(w, axis=0, keepdims=True) + 1e-20) * ROUTED_SCALE
    eidx_ref[...] = jnp.concatenate(idx_rows, axis=0)
    wts_ref[...] = w

    upper = (lax.broadcasted_iota(I32, (tm, tm), 0) < lax.broadcasted_iota(I32, (tm, tm), 1)).astype(BF16)
    base = carry_scr[...][:, 0:1]
    excl = _dot(sel.astype(BF16), upper) + base
    rnk_ref[...] = jnp.concatenate(
        [jnp.sum(jnp.where(ei == ix, excl, 0.0), axis=0, keepdims=True) for ix in idx_rows],
        axis=0).astype(I32)
    carry_scr[...] = carry_scr[...] + jnp.sum(sel, axis=1, keepdims=True)
    cnt_ref[...] = carry_scr[...]


def _route(logits_t, bias, tm):
    E, T = logits_t.shape
    K = TOP_K
    tok = lambda: pl.BlockSpec((K, tm), lambda i: (0, i))
    return pl.pallas_call(
        functools.partial(_route_kernel, tm=tm),
        grid=(T // tm,),
        in_specs=[pl.BlockSpec((E, tm), lambda i: (0, i)),
                  pl.BlockSpec((E, 1), lambda i: (0, 0))],
        out_specs=[tok(), tok(), tok(), pl.BlockSpec((E, LANES), lambda i: (0, 0))],
        out_shape=[jax.ShapeDtypeStruct((K, T), I32), jax.ShapeDtypeStruct((K, T), F32),
                   jax.ShapeDtypeStruct((K, T), I32), jax.ShapeDtypeStruct((E, LANES), F32)],
        scratch_shapes=[pltpu.VMEM((E, LANES), F32)],
        compiler_params=_cparams(("arbitrary",)),
        name="route",
    )(logits_t, bias)


def _slots_kernel(pst_ref, eidx_ref, rnk_ref, dest_ref):
    eidx = eidx_ref[...]
    dest = rnk_ref[...]
    for e in range(N_EXPERTS):
        dest = dest + jnp.where(eidx == e, pst_ref[e], 0)
    dest_ref[...] = dest


def _slots(eidx, rnk, pstart, tm):
    K, T = eidx.shape
    tok = lambda: pl.BlockSpec((K, tm), lambda i: (0, i))
    return pl.pallas_call(
        _slots_kernel,
        grid=(T // tm,),
        in_specs=[pl.BlockSpec(memory_space=pltpu.SMEM), tok(), tok()],
        out_specs=tok(),
        out_shape=jax.ShapeDtypeStruct((K, T), I32),
        compiler_params=_cparams(("parallel",)),
        name="slots",
    )(pstart, eidx, rnk)


def _ffn_kernel(te_ref, nu_ref, tok_ref, dst_ref, h_hbm, wg_ref, wu_ref, wd_ref, ys_hbm,
                gbuf, obuf, wg_b, wu_b, wd_b, gsem, osem, *, tm):
    s = pl.program_id(0)
    nu = nu_ref[0]
    tile_rows = tm * PACK_ROWS

    def gather(slot):
        for r in range(tm):
            pltpu.make_async_copy(
                h_hbm.at[pl.ds(pl.multiple_of(tok_ref[0, 0, r] * PACK_ROWS, PACK_ROWS), PACK_ROWS), :],
                gbuf.at[slot, pl.ds(r * PACK_ROWS, PACK_ROWS), :], gsem.at[slot]).start()

    def scatter(slot):
        for r in range(tm):
            pltpu.make_async_copy(
                obuf.at[slot, pl.ds(r * PACK_ROWS, PACK_ROWS), :],
                ys_hbm.at[pl.ds(pl.multiple_of(dst_ref[0, 0, r] * PACK_ROWS, PACK_ROWS), PACK_ROWS), :],
                osem.at[slot]).start()

    def wait_gather(slot):
        pltpu.make_async_copy(h_hbm.at[pl.ds(0, tile_rows), :], gbuf.at[slot], gsem.at[slot]).wait()

    def wait_scatter(slot):
        pltpu.make_async_copy(obuf.at[slot], ys_hbm.at[pl.ds(0, tile_rows), :], osem.at[slot]).wait()

    def compute(slot):
        parts = []
        for c in range(PACK_ROWS):
            lo, hi = _unpack_rows(gbuf.at[slot], c, tm)
            parts += [lo.astype(BF16), hi.astype(BF16)]
        a = jnp.concatenate(parts, axis=1)
        hg = _dot(a, wg_b[...])
        hu = _dot(a, wu_b[...])
        hid = (hg * _sigmoid(hg) * hu).astype(BF16)
        _pack_rows(_dot(hid, wd_b[...]), obuf.at[slot], tm)

    cur = jnp.maximum(s - 1, 0)
    @pl.when((s >= 1) & (s <= nu) & ((s == 1) | (te_ref[cur] != te_ref[jnp.maximum(s - 2, 0)])))
    def _():
        wg_b[...] = wg_ref[0].astype(BF16)
        wu_b[...] = wu_ref[0].astype(BF16)
        wd_b[...] = wd_ref[0].astype(BF16)

    @pl.when(s == 0)
    def _():
        gather(0)

    @pl.when(s == 1)
    def _():
        wait_gather(0)
        gather(1)
        compute(0)

    for slot in range(2):
        parity = s % 2 == slot

        @pl.when(parity & (s >= 3) & (s <= nu + 1))
        def _():
            wait_scatter(1 - slot)

        @pl.when(parity & (s >= 2) & (s <= nu))
        def _():
            wait_gather(1 - slot)
            gather(slot)
            scatter(slot)
            compute(1 - slot)

        @pl.when(parity & (s >= 2) & (s == nu + 1))
        def _():
            wait_gather(1 - slot)
            scatter(slot)
            wait_scatter(slot)


def _ffn(tile_e, n_used, row_tok, row_dst, h2p, wg, wu, wd, tm):
    nt = row_tok.shape[0]
    D, I = wg.shape[1], wg.shape[2]
    cur = lambda s: jnp.maximum(s - 1, 0)
    wspec = lambda shape: pl.BlockSpec(shape, lambda s, te, nu: (te[cur(s)], 0, 0))
    grid_spec = pltpu.PrefetchScalarGridSpec(
        num_scalar_prefetch=2,
        grid=(nt + 1,),
        in_specs=[
            pl.BlockSpec((1, 1, tm), lambda s, te, nu: (jnp.minimum(s, nt - 1), 0, 0),
                         memory_space=pltpu.SMEM),
            pl.BlockSpec((1, 1, tm), lambda s, te, nu: (jnp.maximum(s - 2, 0), 0, 0),
                         memory_space=pltpu.SMEM),
            pl.BlockSpec(memory_space=pl.ANY),
            wspec((1, D, I)), wspec((1, D, I)), wspec((1, I, D)),
        ],
        out_specs=pl.BlockSpec(memory_space=pl.ANY),
        scratch_shapes=[pltpu.VMEM((2, tm * PACK_ROWS, PACK_W), U32),
                        pltpu.VMEM((2, tm * PACK_ROWS, PACK_W), U32),
                        pltpu.VMEM((D, I), BF16), pltpu.VMEM((D, I), BF16), pltpu.VMEM((I, D), BF16),
                        pltpu.SemaphoreType.DMA((2,)), pltpu.SemaphoreType.DMA((2,))],
    )
    return pl.pallas_call(
        functools.partial(_ffn_kernel, tm=tm),
        grid_spec=grid_spec,
        out_shape=jax.ShapeDtypeStruct((nt * tm * PACK_ROWS, PACK_W), U32),
        compiler_params=_cparams(("arbitrary",)),
        name="expert_ffn",
    )(tile_e, n_used, row_tok, row_dst, h2p, wg, wu, wd)


def _combine_kernel(*refs, tm):
    ys_refs = refs[:TOP_K]
    wt_ref, x1_ref, h2_ref, wsg_ref, wsu_ref, wsd_ref, o_ref = refs[TOP_K:]
    h2 = h2_ref[...]
    hg = _dot(h2, wsg_ref[...])
    hu = _dot(h2, wsu_ref[...])
    base = x1_ref[...] + _dot((hg * _sigmoid(hg) * hu).astype(BF16), wsd_ref[...])
    wt = wt_ref[...]
    for c in range(PACK_ROWS):
        acc_lo = base[:, c * 2 * LANES:c * 2 * LANES + LANES]
        acc_hi = base[:, c * 2 * LANES + LANES:(c + 1) * 2 * LANES]
        for k in range(TOP_K):
            lo, hi = _unpack_rows(ys_refs[k], c, tm)
            wk = wt[:, k:k + 1]
            acc_lo = acc_lo + wk * lo
            acc_hi = acc_hi + wk * hi
        o_ref[:, c * 2 * LANES:c * 2 * LANES + LANES] = acc_lo
        o_ref[:, c * 2 * LANES + LANES:(c + 1) * 2 * LANES] = acc_hi


def _combine(ys, wts_t, x1, h2, wsg, wsu, wsd, tm):
    T, D = x1.shape
    K = wts_t.shape[1]
    I = wsg.shape[1]
    full = lambda shape: pl.BlockSpec(shape, lambda i: (0, 0))
    return pl.pallas_call(
        functools.partial(_combine_kernel, tm=tm),
        grid=(T // tm,),
        in_specs=[pl.BlockSpec((tm * PACK_ROWS, PACK_W), lambda i, k=k: (k * (T // tm) + i, 0))
                  for k in range(K)] + [
                  pl.BlockSpec((tm, K), lambda i: (i, 0)),
                  pl.BlockSpec((tm, D), lambda i: (i, 0)),
                  pl.BlockSpec((tm, D), lambda i: (i, 0)),
                  full((D, I)), full((D, I)), full((I, D))],
        out_specs=pl.BlockSpec((tm, D), lambda i: (i, 0)),
        out_shape=jax.ShapeDtypeStruct((T, D), F32),
        compiler_params=_cparams(("parallel",)),
        name="combine",
    )(*([ys] * K), wts_t, x1, h2, wsg, wsu, wsd)


def _ple_kernel(x_ref, p_ref, wp_ref, gpost_ref, gin_ref, wg_ref, o_ref, *, tm):
    hw = WO_ROWS
    groups = [slice(h * hw, (h + 1) * hw) for h in range(tm // hw)]
    wg = wg_ref[...]
    gates, ples = [], []
    for r in groups:
        hn = _rms(x_ref[r, :], gin_ref[...]).astype(BF16)
        gates.append(_dot(hn, wg))
        ples.append(_rms(_dot(p_ref[r, :].astype(BF16), wp_ref[...]), gpost_ref[...]))
    for r, gate, ple in zip(groups, gates, ples):
        o_ref[r, :] = x_ref[r, :] + _sigmoid(gate) * ple


def _ple(x2, p2, wp, gpost, gin, wg, tm):
    T, D = x2.shape
    Pd = p2.shape[1]
    full = lambda shape: pl.BlockSpec(shape, lambda i: (0, 0))
    return pl.pallas_call(
        functools.partial(_ple_kernel, tm=tm),
        grid=(T // tm,),
        in_specs=[
            pl.BlockSpec((tm, D), lambda i: (i, 0)),
            pl.BlockSpec((tm, Pd), lambda i: (i, 0)),
            full((Pd, D)), full((1, D)), full((1, D)),
            pl.BlockSpec((D, D), lambda i: (0, 0), pipeline_mode=pl.Buffered(1)),
        ],
        out_specs=pl.BlockSpec((tm, D), lambda i: (i, 0)),
        out_shape=jax.ShapeDtypeStruct((T, D), F32),
        compiler_params=_cparams(("parallel",)),
        name="ple",
    )(x2, p2, wp, gpost, gin, wg)


def _pad_cols(w, n):
    return jnp.pad(w, ((0, 0), (0, n - w.shape[1])))


def _pad_rows(w, n):
    return jnp.pad(w, ((0, n - w.shape[0]), (0, 0)))


def _layout_w_in(w):
    W = RWKV_WIDTH
    c = 3 * W
    segs = [w[:, 0:c],
            _pad_cols(w[:, c:c + DECAY_LORA], LORA_PAD),
            _pad_cols(w[:, c + DECAY_LORA:c + DECAY_LORA + AAA_LORA], LORA_PAD),
            _pad_cols(w[:, c + DECAY_LORA + AAA_LORA:c + DECAY_LORA + AAA_LORA + GATE_LORA], GATE_LORA_PAD)]
    c += DECAY_LORA + AAA_LORA + GATE_LORA
    mla_cols = Q_LORA + KV_LORA + QK_ROPE
    segs.append(_pad_cols(w[:, c:c + mla_cols], MLA_PAD))
    c += mla_cols
    segs.append(w[:, c:])
    out = jnp.concatenate(segs, axis=1).astype(BF16)
    assert out.shape[1] == IN_PAD
    return out


def _layout_mu(mu):
    W = RWKV_WIDTH
    c = 3 * W
    segs = [mu[0:c],
            jnp.pad(mu[c:c + DECAY_LORA], (0, LORA_PAD - DECAY_LORA)),
            jnp.pad(mu[c + DECAY_LORA:c + DECAY_LORA + AAA_LORA], (0, LORA_PAD - AAA_LORA)),
            jnp.pad(mu[c + DECAY_LORA + AAA_LORA:], (0, GATE_LORA_PAD - GATE_LORA))]
    return jnp.concatenate(segs).reshape(1, RWKV_PAD)


def _moe_tables(eidx, rnk, counts, tmf, tm_slots):
    K, T = eidx.shape
    E = N_EXPERTS
    A = K * T
    n_tiles = A // tmf + E
    P = n_tiles * tmf
    pcounts = (counts + tmf - 1) // tmf * tmf
    pends = jnp.cumsum(pcounts)
    pstart = (pends - pcounts).astype(I32)
    tile_start = jnp.arange(n_tiles, dtype=I32) * tmf
    tile_e = jnp.minimum(jnp.sum((pends[None, :] <= tile_start[:, None]).astype(I32), axis=1), E - 1)
    n_used = (pends[E - 1:] // tmf).astype(I32)
    dest = _slots(eidx, rnk, pstart, tm_slots)
    j = jnp.arange(tmf, dtype=I32)[None, :]
    tail = j < (pcounts - counts)[:, None]
    spare = jnp.cumsum(jnp.logical_not(tail).reshape(-1).astype(I32)) - 1
    pad_row = jnp.where(tail.reshape(-1), ((pstart + counts)[:, None] + j).reshape(-1),
                        pends[E - 1] + spare)
    q = jnp.arange(P - A, dtype=I32)
    slot_id = jnp.arange(A, dtype=I32)
    rows = jnp.concatenate([dest.reshape(-1), pad_row.astype(I32)])
    slots = jnp.concatenate([slot_id, A + q])
    row_dst = lax.sort((rows, slots), num_keys=1)[1]
    row_tok = jnp.where(row_dst < A, row_dst % T, 0)
    return (row_tok.reshape(n_tiles, 1, tmf), row_dst.reshape(n_tiles, 1, tmf), tile_e, n_used)


def _layer(x, p, positions, g_mix, w_in, mu_rwkv, w0, w2, a0, a2, g2, k_k, k_a, r_k,
           gn_w, gn_b, w_a_up, g_qa, g_kva, w_uq, w_ukv, g_qn, g_kn, w_b_up, w_o,
           g_ffn, w_router, router_bias, w_exp_gate, w_exp_up, w_exp_down,
           w_sh_gate, w_sh_up, w_sh_down, w_ple, g_ple_post, g_ple_in, w_ple_gate,
           *, tiles):
    B, S, D = x.shape
    T = B * S
    W = RWKV_WIDTH
    x2 = x.reshape(T, D)
    row = lambda v: v.reshape(1, -1).astype(F32)

    proj = _inproj(x2, row(g_mix), _layout_w_in(w_in), tiles["tm_in"], tiles["tn_in"])

    head_blk = (jnp.arange(W)[:, None] // RWKV_HEAD == jnp.arange(W)[None, :] // RWKV_HEAD).astype(BF16)
    r, k, v, kk, a, lw, g = _rwkv_prep(
        proj, _layout_mu(mu_rwkv), row(w0), _pad_rows(w2, LORA_PAD).astype(BF16), row(a0),
        _pad_rows(a2, LORA_PAD).astype(BF16), _pad_rows(g2, GATE_LORA_PAD).astype(BF16),
        row(k_k), row(k_a), head_blk, tiles["tm_prep"], S)
    b3 = lambda t: t.reshape(B, S, W)
    o_a = _rwkv_scan(b3(r), b3(k), b3(v), b3(kk), b3(a), b3(lw), b3(g),
                     row(gn_w), row(gn_b), row(r_k), tiles["rows_scan"]).reshape(T, W)

    half = QK_ROPE // 2
    inv_freq = ROPE_THETA ** (-jnp.arange(half, dtype=F32) / half)
    invf = jnp.concatenate([inv_freq, inv_freq, jnp.zeros((LANES - QK_ROPE,), F32)]).reshape(1, LANES)
    wuq = jnp.pad(w_uq.reshape(Q_LORA, MLA_HEADS, QK_HEAD),
                  ((0, 0), (0, 0), (0, QK_PAD - QK_HEAD))).reshape(Q_LORA, MLA_HEADS * QK_PAD)
    pad_g = lambda gv: jnp.pad(gv, (0, QK_PAD - QK_HEAD)).reshape(1, QK_PAD)
    q, kx, vx = _mla_prep(proj, positions.reshape(T, 1).astype(I32), invf, row(g_qa), row(g_kva),
                          wuq.astype(BF16), w_ukv.astype(BF16), pad_g(g_qn), pad_g(g_kn),
                          tiles["tm_mla"])
    o_b = _flash(q.reshape(B, S, -1), kx.reshape(B, S, -1), vx, tiles["tq"]).reshape(T, MLA_WIDTH)

    merged = _merge(o_a, o_b, w_a_up.astype(BF16), w_b_up.astype(BF16), proj,
                    tiles["tm_merge"], tiles["tn_merge"])

    x1, h2, h2p, logits_t = _wo(merged, x2, w_o.astype(BF16), row(g_ffn),
                                jnp.transpose(w_router).astype(F32), tiles["tm_wo"])
    eidx, wts, rnk, cnt = _route(logits_t, router_bias.reshape(N_EXPERTS, 1).astype(F32),
                                 tiles["tm_route"])
    tmf = tiles["tm_ffn"]
    counts = cnt[:, 0].astype(I32)
    row_tok, row_dst, tile_e, n_used = _moe_tables(eidx, rnk, counts, tmf, tiles["tm_slots"])
    ys = _ffn(tile_e, n_used, row_tok, row_dst, h2p, w_exp_gate, w_exp_up, w_exp_down, tmf)
    x3 = _combine(ys, jnp.transpose(wts), x1, h2,
                  w_sh_gate.astype(BF16), w_sh_up.astype(BF16), w_sh_down.astype(BF16),
                  tiles["tm_comb"])

    out = _ple(x3, p.reshape(T, PLE_DIM), w_ple.astype(BF16), row(g_ple_post), row(g_ple_in),
               w_ple_gate.astype(BF16), tiles["tm_ple"])
    return out.reshape(B, S, D)


TILES = dict(tm_in=1024, tn_in=1024, tm_prep=512, rows_scan=256, tm_mla=1024, tq=512,
             tm_merge=1024, tn_merge=1024, tm_wo=512, tm_route=512, tm_ffn=256,
             tm_slots=2048, tm_comb=256, tm_ple=512)


def kernel(x, p, positions, g_mix, w_in, mu_rwkv, w0, w2, a0, a2, g2, k_k, k_a, r_k, gn_w, gn_b, w_a_up, g_qa, g_kva, w_uq, w_ukv, g_qn, g_kn, w_b_up, w_o, g_ffn, w_router, router_bias, w_exp_gate, w_exp_up, w_exp_down, w_sh_gate, w_sh_up, w_sh_down, w_ple, g_ple_post, g_ple_in, w_ple_gate):
    args = (g_mix, w_in, mu_rwkv, w0, w2, a0, a2, g2, k_k, k_a, r_k, gn_w, gn_b, w_a_up,
            g_qa, g_kva, w_uq, w_ukv, g_qn, g_kn, w_b_up, w_o, g_ffn, w_router, router_bias,
            w_exp_gate, w_exp_up, w_exp_down, w_sh_gate, w_sh_up, w_sh_down, w_ple,
            g_ple_post, g_ple_in, w_ple_gate)
    assert all(t.shape[0] == 1 for t in args), "single-layer stack expected"
    return _layer(x, p[0], positions, *[t[0] for t in args], tiles=TILES)
```

```python
import functools

import jax
import jax.numpy as jnp
from jax import lax
from jax.experimental import pallas as pl
from jax.experimental.pallas import tpu as pltpu

F32 = jnp.float32
BF16 = jnp.bfloat16
I32 = jnp.int32
U32 = jnp.uint32

D_MODEL = 2048
PLE_DIM = 256
NORM_EPS = 1e-6
RWKV_HEADS = 16
RWKV_HEAD = 64
RWKV_WIDTH = RWKV_HEADS * RWKV_HEAD
DECAY_LORA = 64
AAA_LORA = 64
GATE_LORA = 160
GN_EPS = 64e-5
MLA_HEADS = 8
Q_LORA = 512
KV_LORA = 512
QK_NOPE = 128
QK_ROPE = 64
QK_HEAD = QK_NOPE + QK_ROPE
V_HEAD = 128
MLA_WIDTH = MLA_HEADS * V_HEAD
ROPE_THETA = 10000.0
N_EXPERTS = 64
TOP_K = 8
N_GROUPS = 8
GROUP_SIZE = N_EXPERTS // N_GROUPS
TOPK_GROUPS = 4
MOE_INTER = 512
ROUTED_SCALE = 2.5

LANES = 128
QK_PAD = 2 * LANES
PACK_ROWS = 8
PACK_W = D_MODEL // 2 // PACK_ROWS
assert PACK_W == LANES

LORA_PAD = 128
GATE_LORA_PAD = 256
RWKV_PAD = 3 * RWKV_WIDTH + 2 * LORA_PAD + GATE_LORA_PAD
MLA_PAD = 1536
COL_RWKV = 0
COL_MLA = RWKV_PAD
COL_GA = COL_MLA + MLA_PAD
COL_GB = COL_GA + D_MODEL
IN_PAD = COL_GB + D_MODEL

VMEM_LIMIT = 56 * 1024 * 1024
CHUNK = 64
WO_ROWS = 128
SCAN_LANE_BLOCKS = 8
MLA_ROWS = 512


def _cparams(sem):
    return pltpu.CompilerParams(dimension_semantics=sem, vmem_limit_bytes=VMEM_LIMIT)


def _dot(a, b):
    return jnp.dot(a, b, preferred_element_type=F32)


def _dot_nt(a, b):
    return lax.dot_general(a, b, (((1,), (1,)), ((), ())), preferred_element_type=F32)


def _dot_tn(a, b):
    return lax.dot_general(a, b, (((0,), (0,)), ((), ())), preferred_element_type=F32)


def _split_bf16(x):
    hi = x.astype(BF16)
    lo = (x - hi.astype(F32)).astype(BF16)
    return hi, lo


def _rms(x, g):
    ms = jnp.mean(x * x, axis=-1, keepdims=True)
    return x * lax.rsqrt(ms + NORM_EPS) * g


def _sigmoid(x):
    return 1.0 / (1.0 + jnp.exp(-x))


def _inproj_kernel(x_ref, g_ref, w_ref, o_ref, h_scr):
    @pl.when(pl.program_id(1) == 0)
    def _():
        h_scr[...] = _rms(x_ref[...], g_ref[...]).astype(BF16)

    o_ref[...] = _dot(h_scr[...], w_ref[...]).astype(o_ref.dtype)


def _inproj(x2, g, w, tm, tn):
    T, D = x2.shape
    N = w.shape[1]
    return pl.pallas_call(
        _inproj_kernel,
        grid=(T // tm, N // tn),
        in_specs=[
            pl.BlockSpec((tm, D), lambda i, j: (i, 0)),
            pl.BlockSpec((1, D), lambda i, j: (0, 0)),
            pl.BlockSpec((D, tn), lambda i, j: (0, j)),
        ],
        out_specs=pl.BlockSpec((tm, tn), lambda i, j: (i, j)),
        out_shape=jax.ShapeDtypeStruct((T, N), BF16),
        scratch_shapes=[pltpu.VMEM((tm, D), BF16)],
        compiler_params=_cparams(("parallel", "arbitrary")),
        name="inproj",
    )(x2, g, w)


HALO = 16


def _rwkv_prep_kernel(cur_ref, halo_ref, mu_ref, w0_ref, w2_ref, a0_ref, a2_ref, g2_ref,
                      kk_ref, ka_ref, bd_ref,
                      r_out, k_out, v_out, kk_out, a_out, lw_out, g_out, *, tm, seq):
    i = pl.program_id(0)
    first = (i * tm) % seq == 0
    W = RWKV_WIDTH

    def shifted(c0, c1):
        cur = cur_ref[:, c0:c1].astype(F32)
        last_prev = jnp.where(first, 0.0, halo_ref[HALO - 1:HALO, c0:c1].astype(F32))
        row = lax.broadcasted_iota(I32, cur.shape, 0)
        prev = jnp.where(row == 0, last_prev, pltpu.roll(cur, 1, 0))
        return cur + (prev - cur) * mu_ref[:, c0:c1]

    r = shifted(0, W)
    r_out[...] = r.astype(r_out.dtype)
    v_out[...] = shifted(2 * W, 3 * W).astype(v_out.dtype)
    lora = shifted(3 * W, RWKV_PAD)
    dw = lora[:, 0:LORA_PAD]
    da = lora[:, LORA_PAD:2 * LORA_PAD]
    dg = lora[:, 2 * LORA_PAD:]
    k = shifted(W, 2 * W)
    z = w0_ref[...] + _dot(jnp.tanh(dw).astype(BF16), w2_ref[...])
    w_log = -(jnp.maximum(-z, 0.0) + jnp.log(1.0 + jnp.exp(-jnp.abs(z)))) - 0.5
    lw_out[...] = -jnp.exp(w_log)
    a = _sigmoid(a0_ref[...] + _dot(da.astype(BF16), a2_ref[...]))
    g_out[...] = _dot(_sigmoid(dg).astype(BF16), g2_ref[...]).astype(g_out.dtype)
    kk = k * kk_ref[...]
    hi, lo = _split_bf16(kk * kk)
    ssq = _dot(hi, bd_ref[...]) + _dot(lo, bd_ref[...])
    kk = kk / jnp.maximum(jnp.sqrt(ssq), 1e-12)
    k_out[...] = (k * (1.0 + (a - 1.0) * ka_ref[...])).astype(k_out.dtype)
    kk_out[...] = kk.astype(kk_out.dtype)
    a_out[...] = a.astype(a_out.dtype)


def _rwkv_prep(proj, mu, w0, w2, a0, a2, g2, k_k, k_a, bd, tm, seq):
    T = proj.shape[0]
    W = RWKV_WIDTH
    full = lambda shape: pl.BlockSpec(shape, lambda i: (0, 0))
    out = lambda: pl.BlockSpec((tm, W), lambda i: (i, 0))
    return pl.pallas_call(
        functools.partial(_rwkv_prep_kernel, tm=tm, seq=seq),
        grid=(T // tm,),
        in_specs=[
            pl.BlockSpec((tm, RWKV_PAD), lambda i: (i, 0)),
            pl.BlockSpec((HALO, RWKV_PAD), lambda i: (jnp.maximum(i * (tm // HALO) - 1, 0), 0)),
            full((1, RWKV_PAD)), full((1, W)), full((LORA_PAD, W)), full((1, W)),
            full((LORA_PAD, W)), full((GATE_LORA_PAD, W)), full((1, W)), full((1, W)),
            full((W, W)),
        ],
        out_specs=[out() for _ in range(7)],
        out_shape=[jax.ShapeDtypeStruct((T, W), dt)
                   for dt in (BF16, BF16, BF16, BF16, BF16, F32, BF16)],
        compiler_params=_cparams(("parallel",)),
        name="rwkv_prep",
    )(proj, proj, mu, w0, w2, a0, a2, g2, k_k, k_a, bd)


def _rwkv_scan_kernel(r_ref, k_ref, v_ref, kk_ref, a_ref, lw_ref, g_ref,
                      gnw_ref, gnb_ref, rk_ref, o_ref, s_scr, *, rows):
    L = CHUNK
    L2 = 2 * L

    @pl.when(pl.program_id(2) == 0)
    def _():
        s_scr[...] = jnp.zeros_like(s_scr)

    lane = lax.broadcasted_iota(I32, (L, LANES), 1)
    m_a = (lane < RWKV_HEAD).astype(F32)
    m_b = 1.0 - m_a
    ri = lax.broadcasted_iota(I32, (L2, L2), 0)
    ci = lax.broadcasted_iota(I32, (L2, L2), 1)
    strict = ri > ci
    incl = ri >= ci
    eye = (ri == ci).astype(F32)
    tri = (lax.broadcasted_iota(I32, (L, L), 0) >= lax.broadcasted_iota(I32, (L, L), 1)).astype(BF16)
    hr = lax.broadcasted_iota(I32, (LANES, LANES), 0) // RWKV_HEAD
    hc = lax.broadcasted_iota(I32, (LANES, LANES), 1) // RWKV_HEAD
    head_ones = (hr == hc).astype(BF16)
    nb = SCAN_LANE_BLOCKS

    def stack(x):
        return jnp.concatenate([x * m_a, x * m_b], axis=0)

    chunks = range(rows // L)
    rows_of = lambda c: slice(c * L, (c + 1) * L)
    lanes_of = lambda h: slice(h * LANES, (h + 1) * LANES)
    each = lambda f, *lists: [f(*vals) for vals in zip(*lists)]

    def operands(c, h):
        sl, ln = rows_of(c), lanes_of(h)
        r = r_ref[0, sl, ln].astype(F32)
        k = k_ref[0, sl, ln].astype(F32)
        v = v_ref[0, sl, ln].astype(F32)
        kk = kk_ref[0, sl, ln].astype(F32)
        a = a_ref[0, sl, ln].astype(F32)
        lw = lw_ref[0, sl, ln]
        hi, lo = _split_bf16(lw)
        cum = _dot(tri, hi) + _dot(tri, lo)
        c_end = cum[L - 1:L, :]
        p_inv = jnp.exp(-cum)
        p_end = jnp.exp(c_end - cum)
        b = kk * a
        return dict(
            x_a=stack(-(kk * jnp.exp(cum - lw))).astype(BF16),
            x_r=stack(r * jnp.exp(cum)),
            x_bk=jnp.concatenate([stack(b * p_inv), stack(k * p_inv)], axis=0).astype(BF16),
            v_st=stack(v).astype(BF16),
            z_hat=jnp.concatenate([stack(b * p_end), stack(k * p_end)], axis=0).astype(BF16),
            p_row=jnp.exp(c_end),
            rkr=(r * k * rk_ref[:, ln]).astype(BF16), v=v)

    ops = [operands(c, h) for c in chunks for h in range(nb)]
    big = [_dot_nt(jnp.concatenate([o["x_a"], o["x_r"].astype(BF16)], axis=0), o["x_bk"])
           for o in ops]
    n_pow = [jnp.where(strict, m[0:L2, 0:L2], 0.0) for m in big]
    a_ak = [jnp.where(strict, m[0:L2, L2:2 * L2], 0.0).astype(BF16) for m in big]
    a_r = [jnp.concatenate([jnp.where(incl, m[L2:2 * L2, 0:L2], 0.0),
                            jnp.where(incl, m[L2:2 * L2, L2:2 * L2], 0.0)], axis=1).astype(BF16)
           for m in big]

    t_inv = [eye + n for n in n_pow]
    for _ in range(5):
        n_pow = each(lambda n: _dot(n.astype(BF16), n.astype(BF16)), n_pow)
        t_inv = each(lambda t, n: t + _dot(t.astype(BF16), n.astype(BF16)), t_inv, n_pow)

    akv = each(lambda m, o: _dot(m, o["v_st"]).astype(BF16), a_ak, ops)
    w = each(lambda t, o, x: _dot(t.astype(BF16), jnp.concatenate([o["x_a"], x], axis=1)),
             t_inv, ops, akv)
    w1 = [m[:, 0:LANES].astype(BF16) for m in w]
    w2v = each(lambda m, o: jnp.concatenate([m[:, LANES:].astype(BF16), o["v_st"]], axis=0), w, ops)
    g_mat = each(lambda o, m, x: (o["x_r"] + _dot(m[:, 0:L2], x)).astype(BF16), ops, a_r, w1)
    y0 = each(_dot, a_r, w2v)
    m_mat = each(lambda x, o: _dot_tn(x, o["z_hat"][0:L2, :]).astype(BF16), w1, ops)
    c2 = each(lambda x, o: _dot_tn(x, o["z_hat"]), w2v, ops)
    bonus_v = [_dot(o["rkr"], head_ones) * o["v"] for o in ops]

    s = [s_scr[h] for h in range(nb)]
    y = []
    for i in range(len(ops)):
        h = i % nb
        s_b = s[h].astype(BF16)
        y_st = _dot_nt(g_mat[i], s_b) + y0[i]
        s[h] = s[h] * ops[i]["p_row"] + _dot(s_b, m_mat[i]) + c2[i]
        y.append(y_st[0:L, :] + y_st[L:L2, :])
    for h in range(nb):
        s_scr[h] = s[h]

    mean = [_dot(t.astype(BF16), head_ones) * (1.0 / RWKV_HEAD) for t in y]
    yc = each(lambda t, m: t - m, y, mean)
    var = [_dot((t * t).astype(BF16), head_ones) * (1.0 / RWKV_HEAD) for t in yc]
    for i in range(len(ops)):
        sl, ln = rows_of(i // nb), lanes_of(i % nb)
        yn = yc[i] * lax.rsqrt(var[i] + GN_EPS) * gnw_ref[:, ln] + gnb_ref[:, ln]
        o_ref[0, sl, ln] = ((yn + bonus_v[i]) * g_ref[0, sl, ln].astype(F32)).astype(o_ref.dtype)


def _rwkv_scan(r, k, v, kk, a, lw, g, gn_w, gn_b, r_k, rows):
    B, S, W = r.shape
    wb = SCAN_LANE_BLOCKS * LANES
    npair = W // wb
    seq = lambda: pl.BlockSpec((1, rows, wb), lambda b, p, c: (b, c, p))
    par = lambda: pl.BlockSpec((1, wb), lambda b, p, c: (0, p))
    return pl.pallas_call(
        functools.partial(_rwkv_scan_kernel, rows=rows),
        grid=(B, npair, S // rows),
        in_specs=[seq() for _ in range(7)] + [par(), par(), par()],
        out_specs=seq(),
        out_shape=jax.ShapeDtypeStruct((B, S, W), BF16),
        scratch_shapes=[pltpu.VMEM((SCAN_LANE_BLOCKS, LANES, LANES), F32)],
        compiler_params=_cparams(("parallel", "parallel", "arbitrary")),
        name="rwkv_scan",
    )(r, k, v, kk, a, lw, g, gn_w, gn_b, r_k)


def _rope128(x, cos, sin_signed):
    lane = lax.broadcasted_iota(I32, x.shape, 1)
    half = QK_ROPE // 2
    partner = jnp.where(lane < half, pltpu.roll(x, LANES - half, 1), pltpu.roll(x, half, 1))
    return x * cos + partner * sin_signed


def _mla_prep_kernel(cq_ref, ckv_ref, kr_ref, pos_ref, invf_ref, gqa_ref, gkva_ref,
                     wuq_ref, wukv_ref, gqn_ref, gkn_ref,
                     q_out, k_out, v_out, cqn_scr, ckvn_scr, cos_scr, sin_scr):
    @pl.when(pl.program_id(1) == 0)
    def _():
        cqn_scr[...] = _rms(cq_ref[...].astype(F32), gqa_ref[...]).astype(BF16)
        ckvn_scr[...] = _rms(ckv_ref[...].astype(F32), gkva_ref[...]).astype(BF16)
        ang = pos_ref[...].astype(F32) * invf_ref[...]
        lane = lax.broadcasted_iota(I32, ang.shape, 1)
        cos_scr[...] = jnp.cos(ang)
        sin_scr[...] = jnp.where(lane < QK_ROPE // 2, -1.0, 1.0) * jnp.sin(ang)

    tm = q_out.shape[0]
    hw = min(MLA_ROWS, tm)
    groups = [slice(g * hw, (g + 1) * hw) for g in range(tm // hw)]
    qs = [_dot(cqn_scr[r, :], wuq_ref[...]) for r in groups]
    kvs = [_dot(ckvn_scr[r, :], wukv_ref[...]) for r in groups]
    inv_n = 1.0 / QK_HEAD
    gkn = gkn_ref[...]
    for r, q, kv in zip(groups, qs, kvs):
        cos = cos_scr[r, :]
        sin = sin_scr[r, :]
        rs = lax.rsqrt(jnp.sum(q * q, axis=-1, keepdims=True) * inv_n + NORM_EPS)
        q = q * rs * gqn_ref[...] * (QK_HEAD ** -0.5 * LOG2E)
        q_out[r, :] = jnp.concatenate(
            [q[:, 0:LANES], _rope128(q[:, LANES:], cos, sin)], axis=1).astype(q_out.dtype)
        k_nope = kv[:, 0:QK_NOPE]
        kr = kr_ref[r, :].astype(F32)
        ssq = (jnp.sum(k_nope * k_nope, axis=-1, keepdims=True)
               + jnp.sum(kr * kr, axis=-1, keepdims=True))
        rs = lax.rsqrt(ssq * inv_n + NORM_EPS)
        k_out[r, :] = jnp.concatenate(
            [k_nope * rs * gkn[:, 0:LANES], _rope128(kr * rs * gkn[:, LANES:], cos, sin)],
            axis=1).astype(k_out.dtype)
        v_out[:, r] = jnp.transpose(kv[:, QK_NOPE:].astype(v_out.dtype))


def _mla_prep(proj, pos, invf, g_qa, g_kva, wuq, wukv, gqn, gkn, tm):
    T = proj.shape[0]
    H = MLA_HEADS
    c0 = COL_MLA // Q_LORA
    ckr = (COL_MLA + Q_LORA + KV_LORA) // LANES
    full = lambda shape: pl.BlockSpec(shape, lambda i, h: (0, 0))
    return pl.pallas_call(
        _mla_prep_kernel,
        grid=(T // tm, H),
        in_specs=[
            pl.BlockSpec((tm, Q_LORA), lambda i, h: (i, c0)),
            pl.BlockSpec((tm, KV_LORA), lambda i, h: (i, c0 + 1)),
            pl.BlockSpec((tm, LANES), lambda i, h: (i, ckr)),
            pl.BlockSpec((tm, 1), lambda i, h: (i, 0)),
            full((1, LANES)), full((1, Q_LORA)), full((1, KV_LORA)),
            pl.BlockSpec((Q_LORA, QK_PAD), lambda i, h: (0, h)),
            pl.BlockSpec((KV_LORA, QK_NOPE + V_HEAD), lambda i, h: (0, h)),
            full((1, QK_PAD)), full((1, QK_PAD)),
        ],
        out_specs=[
            pl.BlockSpec((tm, QK_PAD), lambda i, h: (i, h)),
            pl.BlockSpec((tm, QK_PAD), lambda i, h: (i, h)),
            pl.BlockSpec((V_HEAD, tm), lambda i, h: (h, i)),
        ],
        out_shape=[jax.ShapeDtypeStruct((T, H * QK_PAD), BF16),
                   jax.ShapeDtypeStruct((T, H * QK_PAD), BF16),
                   jax.ShapeDtypeStruct((H * V_HEAD, T), BF16)],
        scratch_shapes=[pltpu.VMEM((tm, Q_LORA), BF16), pltpu.VMEM((tm, KV_LORA), BF16),
                        pltpu.VMEM((tm, LANES), F32), pltpu.VMEM((tm, LANES), F32)],
        compiler_params=_cparams(("parallel", "arbitrary")),
        name="mla_prep",
    )(proj, proj, proj, pos, invf, g_qa, g_kva, wuq, wukv, gqn, gkn)


MASK_VALUE = -1e30
LOG2E = 1.4426950408889634


def _flash_kernel(q_ref, k_ref, vt_ref, o_ref, s_scr, p_scr, alpha_scr, m_scr, l_scr, acc_scr,
                  *, tq):
    i = pl.program_id(2)
    q = q_ref[0]
    m_scr[...] = jnp.full_like(m_scr, MASK_VALUE)
    l_scr[...] = jnp.zeros_like(l_scr)
    acc_scr[...] = jnp.zeros_like(acc_scr)

    q_half = tq // 2

    def scores(j, slot, half):
        ks = pl.ds(pl.multiple_of(j * tq, tq), tq)
        cols = slice(half * q_half, (half + 1) * q_half)
        s_scr[slot, :, cols] = _dot_nt(k_ref[0, ks, :], q[cols, :])

    def softmax(slot, masked, strips):
        for c in strips:
            cols = slice(c * LANES, (c + 1) * LANES)
            s = s_scr[slot, :, cols]
            if masked:
                kpos = lax.broadcasted_iota(I32, s.shape, 0)
                qpos = lax.broadcasted_iota(I32, s.shape, 1) + c * LANES
                s = jnp.where(kpos <= qpos, s, MASK_VALUE)
            m_old = m_scr[:, cols]
            m_new = jnp.maximum(m_old, jnp.max(s, axis=0, keepdims=True))
            p = jnp.exp2(s - m_new)
            alpha = jnp.exp2(m_old - m_new)
            l_scr[:, cols] = alpha * l_scr[:, cols] + jnp.sum(p, axis=0, keepdims=True)
            m_scr[:, cols] = m_new
            p_scr[slot, :, cols] = p.astype(BF16)
            alpha_scr[slot, :, cols] = alpha

    def weighted_values(j, slot):
        vt = vt_ref[:, pl.ds(pl.multiple_of(j * tq, tq), tq)]
        acc_scr[...] = alpha_scr[slot] * acc_scr[...] + _dot(vt, p_scr[slot])

    scores(0, 0, 0)
    scores(0, 0, 1)
    p_scr[1] = jnp.zeros((tq, tq), BF16)
    alpha_scr[1] = jnp.ones((1, tq), F32)
    n_strips = tq // LANES
    first, second = range(0, n_strips // 2), range(n_strips // 2, n_strips)

    def step(j, slot):
        weighted_values(jnp.maximum(j - 1, 0), 1 - slot)
        scores(j + 1, 1 - slot, 0)
        softmax(slot, False, first)
        scores(j + 1, 1 - slot, 1)
        softmax(slot, False, second)

    def pair(jp, carry):
        step(2 * jp, 0)
        step(2 * jp + 1, 1)
        return carry

    def finish(slot):
        weighted_values(jnp.maximum(i - 1, 0), 1 - slot)
        softmax(slot, True, range(n_strips))
        weighted_values(i, slot)
        o_ref[0] = jnp.transpose(acc_scr[...] / l_scr[...]).astype(o_ref.dtype)

    lax.fori_loop(0, i // 2, pair, 0)

    @pl.when(i % 2 == 0)
    def _():
        finish(0)

    @pl.when(i % 2 == 1)
    def _():
        step(i - 1, 0)
        finish(1)


def _flash(q, k, vt, tq):
    B, S, _ = q.shape
    H = MLA_HEADS
    return pl.pallas_call(
        functools.partial(_flash_kernel, tq=tq),
        grid=(B, H, S // tq),
        in_specs=[
            pl.BlockSpec((1, tq, QK_PAD), lambda b, h, i: (b, i, h)),
            pl.BlockSpec((1, S, QK_PAD), lambda b, h, i: (b, 0, h)),
            pl.BlockSpec((V_HEAD, S), lambda b, h, i: (h, b)),
        ],
        out_specs=pl.BlockSpec((1, tq, V_HEAD), lambda b, h, i: (b, i, h)),
        out_shape=jax.ShapeDtypeStruct((B, S, H * V_HEAD), BF16),
        scratch_shapes=[pltpu.VMEM((2, tq, tq), F32), pltpu.VMEM((2, tq, tq), BF16),
                        pltpu.VMEM((2, 1, tq), F32),
                        pltpu.VMEM((1, tq), F32), pltpu.VMEM((1, tq), F32),
                        pltpu.VMEM((V_HEAD, tq), F32)],
        compiler_params=_cparams(("parallel", "parallel", "arbitrary")),
        name="flash_attn",
    )(q, k, vt)


def _merge_kernel(oa_ref, ob_ref, wa_ref, wb_ref, ga_ref, gb_ref, o_ref):
    ya = _dot(oa_ref[...], wa_ref[...])
    yb = _dot(ob_ref[...], wb_ref[...])
    ga = _sigmoid(ga_ref[...].astype(F32))
    gb = _sigmoid(gb_ref[...].astype(F32))
    o_ref[...] = (ga * ya + gb * yb).astype(o_ref.dtype)


def _merge(oa, ob, wa, wb, proj, tm, tn):
    T, K = oa.shape
    N = wa.shape[1]
    ca = COL_GA // tn
    cb = COL_GB // tn
    return pl.pallas_call(
        _merge_kernel,
        grid=(T // tm, N // tn),
        in_specs=[
            pl.BlockSpec((tm, K), lambda i, j: (i, 0)),
            pl.BlockSpec((tm, K), lambda i, j: (i, 0)),
            pl.BlockSpec((K, tn), lambda i, j: (0, j)),
            pl.BlockSpec((K, tn), lambda i, j: (0, j)),
            pl.BlockSpec((tm, tn), lambda i, j: (i, ca + j)),
            pl.BlockSpec((tm, tn), lambda i, j: (i, cb + j)),
        ],
        out_specs=pl.BlockSpec((tm, tn), lambda i, j: (i, j)),
        out_shape=jax.ShapeDtypeStruct((T, N), BF16),
        compiler_params=_cparams(("parallel", "parallel")),
        name="merge",
    )(oa, ob, wa, wb, proj, proj)


def _pack_rows(x, out_ref, n, row0=0):
    for c in range(PACK_ROWS):
        lo = x[:, c * 2 * LANES:c * 2 * LANES + LANES]
        hi = x[:, c * 2 * LANES + LANES:(c + 1) * 2 * LANES]
        lo_b = pltpu.bitcast(lo.astype(BF16).astype(F32), U32)
        hi_b = pltpu.bitcast(hi.astype(BF16).astype(F32), U32)
        out_ref[pl.ds(row0 * PACK_ROWS + c, n, stride=PACK_ROWS), :] = hi_b | (lo_b >> 16)


def _unpack_rows(ref, c, tm, lead=None):
    idx = (pl.ds(c, tm, stride=PACK_ROWS), slice(None))
    if lead is not None:
        idx = (lead,) + idx
    u = ref[idx]
    lo = pltpu.bitcast(u << 16, F32)
    hi = pltpu.bitcast(u & jnp.uint32(0xFFFF0000), F32)
    return lo, hi


def _wo_kernel(m_ref, x_ref, wo_ref, g_ref, wr_ref, x1_ref, h2_ref, h2p_ref, lg_ref, *, tm):
    hw = WO_ROWS
    groups = [slice(h * hw, (h + 1) * hw) for h in range(tm // hw)]
    wo = wo_ref[...]
    x1 = [x_ref[r, :] + _dot(m_ref[r, :], wo) for r in groups]
    h2 = [_rms(v, g_ref[...]) for v in x1]
    w_hi, w_lo = _split_bf16(wr_ref[...])
    for h, r in enumerate(groups):
        x1_ref[r, :] = x1[h]
        h2_ref[r, :] = h2[h].astype(BF16)
        _pack_rows(h2[h], h2p_ref, hw, row0=h * hw)
        h_hi, h_lo = _split_bf16(h2[h])
        lg_ref[:, r] = _dot_nt(w_hi, h_hi) + _dot_nt(w_hi, h_lo) + _dot_nt(w_lo, h_hi)


def _wo(merged, x2, wo, g, wr_t, tm):
    T, D = x2.shape
    E = wr_t.shape[0]
    return pl.pallas_call(
        functools.partial(_wo_kernel, tm=tm),
        grid=(T // tm,),
        in_specs=[
            pl.BlockSpec((tm, D), lambda i: (i, 0)),
            pl.BlockSpec((tm, D), lambda i: (i, 0)),
            pl.BlockSpec((D, D), lambda i: (0, 0), pipeline_mode=pl.Buffered(1)),
            pl.BlockSpec((1, D), lambda i: (0, 0)),
            pl.BlockSpec((E, D), lambda i: (0, 0)),
        ],
        out_specs=[
            pl.BlockSpec((tm, D), lambda i: (i, 0)),
            pl.BlockSpec((tm, D), lambda i: (i, 0)),
            pl.BlockSpec((tm * PACK_ROWS, PACK_W), lambda i: (i, 0)),
            pl.BlockSpec((E, tm), lambda i: (0, i)),
        ],
        out_shape=[jax.ShapeDtypeStruct((T, D), F32),
                   jax.ShapeDtypeStruct((T, D), BF16),
                   jax.ShapeDtypeStruct((T * PACK_ROWS, PACK_W), U32),
                   jax.ShapeDtypeStruct((E, T), F32)],
        compiler_params=_cparams(("parallel",)),
        name="wo_norm_router",
    )(merged, x2, wo, g, wr_t)


def _route_kernel(lg_ref, bias_ref, eidx_ref, wts_ref, rnk_ref, cnt_ref, carry_scr, *, tm):
    E = N_EXPERTS
    NEG = -jnp.inf

    @pl.when(pl.program_id(0) == 0)
    def _():
        carry_scr[...] = jnp.zeros_like(carry_scr)

    scores = _sigmoid(lg_ref[...])
    choice = scores + bias_ref[...]
    c3 = choice.reshape(N_GROUPS, GROUP_SIZE, tm)
    sub = lax.broadcasted_iota(I32, c3.shape, 1)
    m1 = jnp.max(c3, axis=1, keepdims=True)
    i1 = jnp.min(jnp.where(c3 == m1, sub, GROUP_SIZE), axis=1, keepdims=True)
    m2 = jnp.max(jnp.where(sub == i1, NEG, c3), axis=1, keepdims=True)
    gs = (m1 + m2).reshape(N_GROUPS, tm)
    gi = lax.broadcasted_iota(I32, gs.shape, 0)
    gsel = jnp.zeros(gs.shape, F32)
    for _ in range(TOPK_GROUPS):
        mx = jnp.max(gs, axis=0, keepdims=True)
        ix = jnp.min(jnp.where(gs == mx, gi, N_GROUPS), axis=0, keepdims=True)
        hit = gi == ix
        gsel = jnp.where(hit, 1.0, gsel)
        gs = jnp.where(hit, NEG, gs)
    emask = jnp.broadcast_to(gsel.reshape(N_GROUPS, 1, tm), (N_GROUPS, GROUP_SIZE, tm)).reshape(E, tm)
    x = jnp.where(emask > 0.5, choice, NEG)
    ei = lax.broadcasted_iota(I32, x.shape, 0)
    sel = jnp.zeros(x.shape, F32)
    idx_rows, w_rows = [], []
    for _ in range(TOP_K):
        mx = jnp.max(x, axis=0, keepdims=True)
        ix = jnp.min(jnp.where(x == mx, ei, E), axis=0, keepdims=True)
        hit = ei == ix
        w_rows.append(jnp.sum(jnp.where(hit, scores, 0.0), axis=0, keepdims=True))
        idx_rows.append(ix)
        sel = jnp.where(hit, 1.0, sel)
        x = jnp.where(hit, NEG, x)
    w = jnp.concatenate(w_rows, axis=0)
    w = w / (jnp.sum(w, axis=0, keepdims=True) + 1e-20) * ROUTED_SCALE
    eidx_ref[...] = jnp.concatenate(idx_rows, axis=0)
    wts_ref[...] = w

    upper = (lax.broadcasted_iota(I32, (tm, tm), 0) < lax.broadcasted_iota(I32, (tm, tm), 1)).astype(BF16)
    base = carry_scr[...][:, 0:1]
    excl = _dot(sel.astype(BF16), upper) + base
    rnk_ref[...] = jnp.concatenate(
        [jnp.sum(jnp.where(ei == ix, excl, 0.0), axis=0, keepdims=True) for ix in idx_rows],
        axis=0).astype(I32)
    carry_scr[...] = carry_scr[...] + jnp.sum(sel, axis=1, keepdims=True)
    cnt_ref[...] = carry_scr[...]


def _route(logits_t, bias, tm):
    E, T = logits_t.shape
    K = TOP_K
    tok = lambda: pl.BlockSpec((K, tm), lambda i: (0, i))
    return pl.pallas_call(
        functools.partial(_route_kernel, tm=tm),
        grid=(T // tm,),
        in_specs=[pl.BlockSpec((E, tm), lambda i: (0, i)),
                  pl.BlockSpec((E, 1), lambda i: (0, 0))],
        out_specs=[tok(), tok(), tok(), pl.BlockSpec((E, LANES), lambda i: (0, 0))],
        out_shape=[jax.ShapeDtypeStruct((K, T), I32), jax.ShapeDtypeStruct((K, T), F32),
                   jax.ShapeDtypeStruct((K, T), I32), jax.ShapeDtypeStruct((E, LANES), F32)],
        scratch_shapes=[pltpu.VMEM((E, LANES), F32)],
        compiler_params=_cparams(("arbitrary",)),
        name="route",
    )(logits_t, bias)


def _slots_kernel(pst_ref, eidx_ref, rnk_ref, dest_ref):
    eidx = eidx_ref[...]
    dest = rnk_ref[...]
    for e in range(N_EXPERTS):
        dest = dest + jnp.where(eidx == e, pst_ref[e], 0)
    dest_ref[...] = dest


def _slots(eidx, rnk, pstart, tm):
    K, T = eidx.shape
    tok = lambda: pl.BlockSpec((K, tm), lambda i: (0, i))
    return pl.pallas_call(
        _slots_kernel,
        grid=(T // tm,),
        in_specs=[pl.BlockSpec(memory_space=pltpu.SMEM), tok(), tok()],
        out_specs=tok(),
        out_shape=jax.ShapeDtypeStruct((K, T), I32),
        compiler_params=_cparams(("parallel",)),
        name="slots",
    )(pstart, eidx, rnk)


def _ffn_kernel(te_ref, nu_ref, tok_ref, dst_ref, h_hbm, wg_ref, wu_ref, wd_ref, ys_hbm,
                gbuf, obuf, wg_b, wu_b, wd_b, gsem, osem, *, tm):
    s = pl.program_id(0)
    nu = nu_ref[0]
    tile_rows = tm * PACK_ROWS

    def gather(slot):
        for r in range(tm):
            pltpu.make_async_copy(
                h_hbm.at[pl.ds(pl.multiple_of(tok_ref[0, 0, r] * PACK_ROWS, PACK_ROWS), PACK_ROWS), :],
                gbuf.at[slot, pl.ds(r * PACK_ROWS, PACK_ROWS), :], gsem.at[slot]).start()

    def scatter(slot):
        for r in range(tm):
            pltpu.make_async_copy(
                obuf.at[slot, pl.ds(r * PACK_ROWS, PACK_ROWS), :],
                ys_hbm.at[pl.ds(pl.multiple_of(dst_ref[0, 0, r] * PACK_ROWS, PACK_ROWS), PACK_ROWS), :],
                osem.at[slot]).start()

    def wait_gather(slot):
        pltpu.make_async_copy(h_hbm.at[pl.ds(0, tile_rows), :], gbuf.at[slot], gsem.at[slot]).wait()

    def wait_scatter(slot):
        pltpu.make_async_copy(obuf.at[slot], ys_hbm.at[pl.ds(0, tile_rows), :], osem.at[slot]).wait()

    def compute(slot):
        parts = []
        for c in range(PACK_ROWS):
            lo, hi = _unpack_rows(gbuf.at[slot], c, tm)
            parts += [lo.astype(BF16), hi.astype(BF16)]
        a = jnp.concatenate(parts, axis=1)
        hg = _dot(a, wg_b[...])
        hu = _dot(a, wu_b[...])
        hid = (hg * _sigmoid(hg) * hu).astype(BF16)
        _pack_rows(_dot(hid, wd_b[...]), obuf.at[slot], tm)

    cur = jnp.maximum(s - 1, 0)
    @pl.when((s >= 1) & (s <= nu) & ((s == 1) | (te_ref[cur] != te_ref[jnp.maximum(s - 2, 0)])))
    def _():
        wg_b[...] = wg_ref[0].astype(BF16)
        wu_b[...] = wu_ref[0].astype(BF16)
        wd_b[...] = wd_ref[0].astype(BF16)

    @pl.when(s == 0)
    def _():
        gather(0)

    @pl.when(s == 1)
    def _():
        wait_gather(0)
        gather(1)
        compute(0)

    for slot in range(2):
        parity = s % 2 == slot

        @pl.when(parity & (s >= 3) & (s <= nu + 1))
        def _():
            wait_scatter(1 - slot)

        @pl.when(parity & (s >= 2) & (s <= nu))
        def _():
            wait_gather(1 - slot)
            gather(slot)
            scatter(slot)
            compute(1 - slot)

        @pl.when(parity & (s >= 2) & (s == nu + 1))
        def _():
            wait_gather(1 - slot)
            scatter(slot)
            wait_scatter(slot)


def _ffn(tile_e, n_used, row_tok, row_dst, h2p, wg, wu, wd, tm):
    nt = row_tok.shape[0]
    D, I = wg.shape[1], wg.shape[2]
    cur = lambda s: jnp.maximum(s - 1, 0)
    wspec = lambda shape: pl.BlockSpec(shape, lambda s, te, nu: (te[cur(s)], 0, 0))
    grid_spec = pltpu.PrefetchScalarGridSpec(
        num_scalar_prefetch=2,
        grid=(nt + 1,),
        in_specs=[
            pl.BlockSpec((1, 1, tm), lambda s, te, nu: (jnp.minimum(s, nt - 1), 0, 0),
                         memory_space=pltpu.SMEM),
            pl.BlockSpec((1, 1, tm), lambda s, te, nu: (jnp.maximum(s - 2, 0), 0, 0),
                         memory_space=pltpu.SMEM),
            pl.BlockSpec(memory_space=pl.ANY),
            wspec((1, D, I)), wspec((1, D, I)), wspec((1, I, D)),
        ],
        out_specs=pl.BlockSpec(memory_space=pl.ANY),
        scratch_shapes=[pltpu.VMEM((2, tm * PACK_ROWS, PACK_W), U32),
                        pltpu.VMEM((2, tm * PACK_ROWS, PACK_W), U32),
                        pltpu.VMEM((D, I), BF16), pltpu.VMEM((D, I), BF16), pltpu.VMEM((I, D), BF16),
                        pltpu.SemaphoreType.DMA((2,)), pltpu.SemaphoreType.DMA((2,))],
    )
    return pl.pallas_call(
        functools.partial(_ffn_kernel, tm=tm),
        grid_spec=grid_spec,
        out_shape=jax.ShapeDtypeStruct((nt * tm * PACK_ROWS, PACK_W), U32),
        compiler_params=_cparams(("arbitrary",)),
        name="expert_ffn",
    )(tile_e, n_used, row_tok, row_dst, h2p, wg, wu, wd)


def _combine_kernel(*refs, tm):
    ys_refs = refs[:TOP_K]
    wt_ref, x1_ref, h2_ref, wsg_ref, wsu_ref, wsd_ref, o_ref = refs[TOP_K:]
    h2 = h2_ref[...]
    hg = _dot(h2, wsg_ref[...])
    hu = _dot(h2, wsu_ref[...])
    base = x1_ref[...] + _dot((hg * _sigmoid(hg) * hu).astype(BF16), wsd_ref[...])
    wt = wt_ref[...]
    for c in range(PACK_ROWS):
        acc_lo = base[:, c * 2 * LANES:c * 2 * LANES + LANES]
        acc_hi = base[:, c * 2 * LANES + LANES:(c + 1) * 2 * LANES]
        for k in range(TOP_K):
            lo, hi = _unpack_rows(ys_refs[k], c, tm)
            wk = wt[:, k:k + 1]
            acc_lo = acc_lo + wk * lo
            acc_hi = acc_hi + wk * hi
        o_ref[:, c * 2 * LANES:c * 2 * LANES + LANES] = acc_lo
        o_ref[:, c * 2 * LANES + LANES:(c + 1) * 2 * LANES] = acc_hi


def _combine(ys, wts_t, x1, h2, wsg, wsu, wsd, tm):
    T, D = x1.shape
    K = wts_t.shape[1]
    I = wsg.shape[1]
    full = lambda shape: pl.BlockSpec(shape, lambda i: (0, 0))
    return pl.pallas_call(
        functools.partial(_combine_kernel, tm=tm),
        grid=(T // tm,),
        in_specs=[pl.BlockSpec((tm * PACK_ROWS, PACK_W), lambda i, k=k: (k * (T // tm) + i, 0))
                  for k in range(K)] + [
                  pl.BlockSpec((tm, K), lambda i: (i, 0)),
                  pl.BlockSpec((tm, D), lambda i: (i, 0)),
                  pl.BlockSpec((tm, D), lambda i: (i, 0)),
                  full((D, I)), full((D, I)), full((I, D))],
        out_specs=pl.BlockSpec((tm, D), lambda i: (i, 0)),
        out_shape=jax.ShapeDtypeStruct((T, D), F32),
        compiler_params=_cparams(("parallel",)),
        name="combine",
    )(*([ys] * K), wts_t, x1, h2, wsg, wsu, wsd)


def _ple_kernel(x_ref, p_ref, wp_ref, gpost_ref, gin_ref, wg_ref, o_ref, *, tm):
    hw = WO_ROWS
    groups = [slice(h * hw, (h + 1) * hw) for h in range(tm // hw)]
    wg = wg_ref[...]
    gates, ples = [], []
    for r in groups:
        hn = _rms(x_ref[r, :], gin_ref[...]).astype(BF16)
        gates.append(_dot(hn, wg))
        ples.append(_rms(_dot(p_ref[r, :].astype(BF16), wp_ref[...]), gpost_ref[...]))
    for r, gate, ple in zip(groups, gates, ples):
        o_ref[r, :] = x_ref[r, :] + _sigmoid(gate) * ple


def _ple(x2, p2, wp, gpost, gin, wg, tm):
    T, D = x2.shape
    Pd = p2.shape[1]
    full = lambda shape: pl.BlockSpec(shape, lambda i: (0, 0))
    return pl.pallas_call(
        functools.partial(_ple_kernel, tm=tm),
        grid=(T // tm,),
        in_specs=[
            pl.BlockSpec((tm, D), lambda i: (i, 0)),
            pl.BlockSpec((tm, Pd), lambda i: (i, 0)),
            full((Pd, D)), full((1, D)), full((1, D)),
            pl.BlockSpec((D, D), lambda i: (0, 0), pipeline_mode=pl.Buffered(1)),
        ],
        out_specs=pl.BlockSpec((tm, D), lambda i: (i, 0)),
        out_shape=jax.ShapeDtypeStruct((T, D), F32),
        compiler_params=_cparams(("parallel",)),
        name="ple",
    )(x2, p2, wp, gpost, gin, wg)


def _pad_cols(w, n):
    return jnp.pad(w, ((0, 0), (0, n - w.shape[1])))


def _pad_rows(w, n):
    return jnp.pad(w, ((0, n - w.shape[0]), (0, 0)))


def _layout_w_in(w):
    W = RWKV_WIDTH
    c = 3 * W
    segs = [w[:, 0:c],
            _pad_cols(w[:, c:c + DECAY_LORA], LORA_PAD),
            _pad_cols(w[:, c + DECAY_LORA:c + DECAY_LORA + AAA_LORA], LORA_PAD),
            _pad_cols(w[:, c + DECAY_LORA + AAA_LORA:c + DECAY_LORA + AAA_LORA + GATE_LORA], GATE_LORA_PAD)]
    c += DECAY_LORA + AAA_LORA + GATE_LORA
    mla_cols = Q_LORA + KV_LORA + QK_ROPE
    segs.append(_pad_cols(w[:, c:c + mla_cols], MLA_PAD))
    c += mla_cols
    segs.append(w[:, c:])
    out = jnp.concatenate(segs, axis=1).astype(BF16)
    assert out.shape[1] == IN_PAD
    return out


def _layout_mu(mu):
    W = RWKV_WIDTH
    c = 3 * W
    segs = [mu[0:c],
            jnp.pad(mu[c:c + DECAY_LORA], (0, LORA_PAD - DECAY_LORA)),
            jnp.pad(mu[c + DECAY_LORA:c + DECAY_LORA + AAA_LORA], (0, LORA_PAD - AAA_LORA)),
            jnp.pad(mu[c + DECAY_LORA + AAA_LORA:], (0, GATE_LORA_PAD - GATE_LORA))]
    return jnp.concatenate(segs).reshape(1, RWKV_PAD)


def _moe_tables(eidx, rnk, counts, tmf, tm_slots):
    K, T = eidx.shape
    E = N_EXPERTS
    A = K * T
    n_tiles = A // tmf + E
    P = n_tiles * tmf
    pcounts = (counts + tmf - 1) // tmf * tmf
    pends = jnp.cumsum(pcounts)
    pstart = (pends - pcounts).astype(I32)
    tile_start = jnp.arange(n_tiles, dtype=I32) * tmf
    tile_e = jnp.minimum(jnp.sum((pends[None, :] <= tile_start[:, None]).astype(I32), axis=1), E - 1)
    n_used = (pends[E - 1:] // tmf).astype(I32)
    dest = _slots(eidx, rnk, pstart, tm_slots)
    j = jnp.arange(tmf, dtype=I32)[None, :]
    tail = j < (pcounts - counts)[:, None]
    spare = jnp.cumsum(jnp.logical_not(tail).reshape(-1).astype(I32)) - 1
    pad_row = jnp.where(tail.reshape(-1), ((pstart + counts)[:, None] + j).reshape(-1),
                        pends[E - 1] + spare)
    q = jnp.arange(P - A, dtype=I32)
    slot_id = jnp.arange(A, dtype=I32)
    rows = jnp.concatenate([dest.reshape(-1), pad_row.astype(I32)])
    slots = jnp.concatenate([slot_id, A + q])
    row_dst = lax.sort((rows, slots), num_keys=1)[1]
    row_tok = jnp.where(row_dst < A, row_dst % T, 0)
    return (row_tok.reshape(n_tiles, 1, tmf), row_dst.reshape(n_tiles, 1, tmf), tile_e, n_used)


def _layer(x, p, positions, g_mix, w_in, mu_rwkv, w0, w2, a0, a2, g2, k_k, k_a, r_k,
           gn_w, gn_b, w_a_up, g_qa, g_kva, w_uq, w_ukv, g_qn, g_kn, w_b_up, w_o,
           g_ffn, w_router, router_bias, w_exp_gate, w_exp_up, w_exp_down,
           w_sh_gate, w_sh_up, w_sh_down, w_ple, g_ple_post, g_ple_in, w_ple_gate,
           *, tiles):
    B, S, D = x.shape
    T = B * S
    W = RWKV_WIDTH
    x2 = x.reshape(T, D)
    row = lambda v: v.reshape(1, -1).astype(F32)

    proj = _inproj(x2, row(g_mix), _layout_w_in(w_in), tiles["tm_in"], tiles["tn_in"])

    head_blk = (jnp.arange(W)[:, None] // RWKV_HEAD == jnp.arange(W)[None, :] // RWKV_HEAD).astype(BF16)
    r, k, v, kk, a, lw, g = _rwkv_prep(
        proj, _layout_mu(mu_rwkv), row(w0), _pad_rows(w2, LORA_PAD).astype(BF16), row(a0),
        _pad_rows(a2, LORA_PAD).astype(BF16), _pad_rows(g2, GATE_LORA_PAD).astype(BF16),
        row(k_k), row(k_a), head_blk, tiles["tm_prep"], S)
    b3 = lambda t: t.reshape(B, S, W)
    o_a = _rwkv_scan(b3(r), b3(k), b3(v), b3(kk), b3(a), b3(lw), b3(g),
                     row(gn_w), row(gn_b), row(r_k), tiles["rows_scan"]).reshape(T, W)

    half = QK_ROPE // 2
    inv_freq = ROPE_THETA ** (-jnp.arange(half, dtype=F32) / half)
    invf = jnp.concatenate([inv_freq, inv_freq, jnp.zeros((LANES - QK_ROPE,), F32)]).reshape(1, LANES)
    wuq = jnp.pad(w_uq.reshape(Q_LORA, MLA_HEADS, QK_HEAD),
                  ((0, 0), (0, 0), (0, QK_PAD - QK_HEAD))).reshape(Q_LORA, MLA_HEADS * QK_PAD)
    pad_g = lambda gv: jnp.pad(gv, (0, QK_PAD - QK_HEAD)).reshape(1, QK_PAD)
    q, kx, vx = _mla_prep(proj, positions.reshape(T, 1).astype(I32), invf, row(g_qa), row(g_kva),
                          wuq.astype(BF16), w_ukv.astype(BF16), pad_g(g_qn), pad_g(g_kn),
                          tiles["tm_mla"])
    o_b = _flash(q.reshape(B, S, -1), kx.reshape(B, S, -1), vx, tiles["tq"]).reshape(T, MLA_WIDTH)

    merged = _merge(o_a, o_b, w_a_up.astype(BF16), w_b_up.astype(BF16), proj,
                    tiles["tm_merge"], tiles["tn_merge"])

    x1, h2, h2p, logits_t = _wo(merged, x2, w_o.astype(BF16), row(g_ffn),
                                jnp.transpose(w_router).astype(F32), tiles["tm_wo"])
    eidx, wts, rnk, cnt = _route(logits_t, router_bias.reshape(N_EXPERTS, 1).astype(F32),
                                 tiles["tm_route"])
    tmf = tiles["tm_ffn"]
    counts = cnt[:, 0].astype(I32)
    row_tok, row_dst, tile_e, n_used = _moe_tables(eidx, rnk, counts, tmf, tiles["tm_slots"])
    ys = _ffn(tile_e, n_used, row_tok, row_dst, h2p, w_exp_gate, w_exp_up, w_exp_down, tmf)
    x3 = _combine(ys, jnp.transpose(wts), x1, h2,
                  w_sh_gate.astype(BF16), w_sh_up.astype(BF16), w_sh_down.astype(BF16),
                  tiles["tm_comb"])

    out = _ple(x3, p.reshape(T, PLE_DIM), w_ple.astype(BF16), row(g_ple_post), row(g_ple_in),
               w_ple_gate.astype(BF16), tiles["tm_ple"])
    return out.reshape(B, S, D)


TILES = dict(tm_in=1024, tn_in=1024, tm_prep=512, rows_scan=128, tm_mla=1024, tq=512,
             tm_merge=1024, tn_merge=1024, tm_wo=512, tm_route=512, tm_ffn=256,
             tm_slots=2048, tm_comb=256, tm_ple=512)


def kernel(x, p, positions, g_mix, w_in, mu_rwkv, w0, w2, a0, a2, g2, k_k, k_a, r_k, gn_w, gn_b, w_a_up, g_qa, g_kva, w_uq, w_ukv, g_qn, g_kn, w_b_up, w_o, g_ffn, w_router, router_bias, w_exp_gate, w_exp_up, w_exp_down, w_sh_gate, w_sh_up, w_sh_down, w_ple, g_ple_post, g_ple_in, w_ple_gate):
    args = (g_mix, w_in, mu_rwkv, w0, w2, a0, a2, g2, k_k, k_a, r_k, gn_w, gn_b, w_a_up,
            g_qa, g_kva, w_uq, w_ukv, g_qn, g_kn, w_b_up, w_o, g_ffn, w_router, router_bias,
            w_exp_gate, w_exp_up, w_exp_down, w_sh_gate, w_sh_up, w_sh_down, w_ple,
            g_ple_post, g_ple_in, w_ple_gate)
    assert all(t.shape[0] == 1 for t in args), "single-layer stack expected"
    return _layer(x, p[0], positions, *[t[0] for t in args], tiles=TILES)
```

```python
import functools

import jax
import jax.numpy as jnp
from jax import lax
from jax.experimental import pallas as pl
from jax.experimental.pallas import tpu as pltpu

F32 = jnp.float32
BF16 = jnp.bfloat16
I32 = jnp.int32
U32 = jnp.uint32

D_MODEL = 2048
PLE_DIM = 256
NORM_EPS = 1e-6
RWKV_HEADS = 16
RWKV_HEAD = 64
RWKV_WIDTH = RWKV_HEADS * RWKV_HEAD
DECAY_LORA = 64
AAA_LORA = 64
GATE_LORA = 160
GN_EPS = 64e-5
MLA_HEADS = 8
Q_LORA = 512
KV_LORA = 512
QK_NOPE = 128
QK_ROPE = 64
QK_HEAD = QK_NOPE + QK_ROPE
V_HEAD = 128
MLA_WIDTH = MLA_HEADS * V_HEAD
ROPE_THETA = 10000.0
N_EXPERTS = 64
TOP_K = 8
N_GROUPS = 8
GROUP_SIZE = N_EXPERTS // N_GROUPS
TOPK_GROUPS = 4
MOE_INTER = 512
ROUTED_SCALE = 2.5

LANES = 128
QK_PAD = 2 * LANES
PACK_ROWS = 8
PACK_W = D_MODEL // 2 // PACK_ROWS
assert PACK_W == LANES

LORA_PAD = 128
GATE_LORA_PAD = 256
RWKV_PAD = 3 * RWKV_WIDTH + 2 * LORA_PAD + GATE_LORA_PAD
MLA_PAD = 1536
COL_RWKV = 0
COL_MLA = RWKV_PAD
COL_GA = COL_MLA + MLA_PAD
COL_GB = COL_GA + D_MODEL
IN_PAD = COL_GB + D_MODEL

VMEM_LIMIT = 56 * 1024 * 1024
CHUNK = 64
WO_ROWS = 128
SCAN_LANE_BLOCKS = 8
MLA_ROWS = 256


def _cparams(sem):
    return pltpu.CompilerParams(dimension_semantics=sem, vmem_limit_bytes=VMEM_LIMIT)


def _dot(a, b):
    return jnp.dot(a, b, preferred_element_type=F32)


def _dot_nt(a, b):
    return lax.dot_general(a, b, (((1,), (1,)), ((), ())), preferred_element_type=F32)


def _dot_tn(a, b):
    return lax.dot_general(a, b, (((0,), (0,)), ((), ())), preferred_element_type=F32)


def _split_bf16(x):
    hi = x.astype(BF16)
    lo = (x - hi.astype(F32)).astype(BF16)
    return hi, lo


def _rms(x, g):
    ms = jnp.mean(x * x, axis=-1, keepdims=True)
    return x * lax.rsqrt(ms + NORM_EPS) * g


def _sigmoid(x):
    return 1.0 / (1.0 + jnp.exp(-x))


def _inproj_kernel(x_ref, g_ref, w_ref, o_ref, h_scr):
    @pl.when(pl.program_id(1) == 0)
    def _():
        h_scr[...] = _rms(x_ref[...], g_ref[...]).astype(BF16)

    o_ref[...] = _dot(h_scr[...], w_ref[...]).astype(o_ref.dtype)


def _inproj(x2, g, w, tm, tn):
    T, D = x2.shape
    N = w.shape[1]
    return pl.pallas_call(
        _inproj_kernel,
        grid=(T // tm, N // tn),
        in_specs=[
            pl.BlockSpec((tm, D), lambda i, j: (i, 0)),
            pl.BlockSpec((1, D), lambda i, j: (0, 0)),
            pl.BlockSpec((D, tn), lambda i, j: (0, j)),
        ],
        out_specs=pl.BlockSpec((tm, tn), lambda i, j: (i, j)),
        out_shape=jax.ShapeDtypeStruct((T, N), BF16),
        scratch_shapes=[pltpu.VMEM((tm, D), BF16)],
        compiler_params=_cparams(("parallel", "arbitrary")),
        name="inproj",
    )(x2, g, w)


HALO = 16


def _rwkv_prep_kernel(cur_ref, halo_ref, mu_ref, w0_ref, w2_ref, a0_ref, a2_ref, g2_ref,
                      kk_ref, ka_ref, bd_ref,
                      r_out, k_out, v_out, kk_out, a_out, lw_out, g_out, *, tm, seq):
    i = pl.program_id(0)
    first = (i * tm) % seq == 0
    W = RWKV_WIDTH

    def shifted(c0, c1):
        cur = cur_ref[:, c0:c1].astype(F32)
        last_prev = jnp.where(first, 0.0, halo_ref[HALO - 1:HALO, c0:c1].astype(F32))
        row = lax.broadcasted_iota(I32, cur.shape, 0)
        prev = jnp.where(row == 0, last_prev, pltpu.roll(cur, 1, 0))
        return cur + (prev - cur) * mu_ref[:, c0:c1]

    r = shifted(0, W)
    r_out[...] = r.astype(r_out.dtype)
    v_out[...] = shifted(2 * W, 3 * W).astype(v_out.dtype)
    lora = shifted(3 * W, RWKV_PAD)
    dw = lora[:, 0:LORA_PAD]
    da = lora[:, LORA_PAD:2 * LORA_PAD]
    dg = lora[:, 2 * LORA_PAD:]
    k = shifted(W, 2 * W)
    z = w0_ref[...] + _dot(jnp.tanh(dw).astype(BF16), w2_ref[...])
    w_log = -(jnp.maximum(-z, 0.0) + jnp.log(1.0 + jnp.exp(-jnp.abs(z)))) - 0.5
    lw_out[...] = -jnp.exp(w_log)
    a = _sigmoid(a0_ref[...] + _dot(da.astype(BF16), a2_ref[...]))
    g_out[...] = _dot(_sigmoid(dg).astype(BF16), g2_ref[...]).astype(g_out.dtype)
    kk = k * kk_ref[...]
    hi, lo = _split_bf16(kk * kk)
    ssq = _dot(hi, bd_ref[...]) + _dot(lo, bd_ref[...])
    kk = kk / jnp.maximum(jnp.sqrt(ssq), 1e-12)
    k_out[...] = (k * (1.0 + (a - 1.0) * ka_ref[...])).astype(k_out.dtype)
    kk_out[...] = kk.astype(kk_out.dtype)
    a_out[...] = a.astype(a_out.dtype)


def _rwkv_prep(proj, mu, w0, w2, a0, a2, g2, k_k, k_a, bd, tm, seq):
    T = proj.shape[0]
    W = RWKV_WIDTH
    full = lambda shape: pl.BlockSpec(shape, lambda i: (0, 0))
    out = lambda: pl.BlockSpec((tm, W), lambda i: (i, 0))
    return pl.pallas_call(
        functools.partial(_rwkv_prep_kernel, tm=tm, seq=seq),
        grid=(T // tm,),
        in_specs=[
            pl.BlockSpec((tm, RWKV_PAD), lambda i: (i, 0)),
            pl.BlockSpec((HALO, RWKV_PAD), lambda i: (jnp.maximum(i * (tm // HALO) - 1, 0), 0)),
            full((1, RWKV_PAD)), full((1, W)), full((LORA_PAD, W)), full((1, W)),
            full((LORA_PAD, W)), full((GATE_LORA_PAD, W)), full((1, W)), full((1, W)),
            full((W, W)),
        ],
        out_specs=[out() for _ in range(7)],
        out_shape=[jax.ShapeDtypeStruct((T, W), dt)
                   for dt in (BF16, BF16, BF16, BF16, BF16, F32, BF16)],
        compiler_params=_cparams(("parallel",)),
        name="rwkv_prep",
    )(proj, proj, mu, w0, w2, a0, a2, g2, k_k, k_a, bd)


def _rwkv_scan_kernel(r_ref, k_ref, v_ref, kk_ref, a_ref, lw_ref, g_ref,
                      gnw_ref, gnb_ref, rk_ref, o_ref, s_scr, *, rows):
    L = CHUNK
    L2 = 2 * L

    @pl.when(pl.program_id(2) == 0)
    def _():
        s_scr[...] = jnp.zeros_like(s_scr)

    lane = lax.broadcasted_iota(I32, (L, LANES), 1)
    m_a = (lane < RWKV_HEAD).astype(F32)
    m_b = 1.0 - m_a
    ri = lax.broadcasted_iota(I32, (L2, L2), 0)
    ci = lax.broadcasted_iota(I32, (L2, L2), 1)
    strict = ri > ci
    incl = ri >= ci
    eye = (ri == ci).astype(F32)
    tri = (lax.broadcasted_iota(I32, (L, L), 0) >= lax.broadcasted_iota(I32, (L, L), 1)).astype(BF16)
    hr = lax.broadcasted_iota(I32, (LANES, LANES), 0) // RWKV_HEAD
    hc = lax.broadcasted_iota(I32, (LANES, LANES), 1) // RWKV_HEAD
    head_ones = (hr == hc).astype(BF16)
    nb = SCAN_LANE_BLOCKS

    def stack(x):
        return jnp.concatenate([x * m_a, x * m_b], axis=0)

    chunks = range(rows // L)
    rows_of = lambda c: slice(c * L, (c + 1) * L)
    lanes_of = lambda h: slice(h * LANES, (h + 1) * LANES)
    each = lambda f, *lists: [f(*vals) for vals in zip(*lists)]

    def operands(c, h):
        sl, ln = rows_of(c), lanes_of(h)
        r = r_ref[0, sl, ln].astype(F32)
        k = k_ref[0, sl, ln].astype(F32)
        v = v_ref[0, sl, ln].astype(F32)
        kk = kk_ref[0, sl, ln].astype(F32)
        a = a_ref[0, sl, ln].astype(F32)
        lw = lw_ref[0, sl, ln]
        hi, lo = _split_bf16(lw)
        cum = _dot(tri, hi) + _dot(tri, lo)
        c_end = cum[L - 1:L, :]
        p_inv = jnp.exp(-cum)
        p_end = jnp.exp(c_end - cum)
        b = kk * a
        return dict(
            x_a=stack(-(kk * jnp.exp(cum - lw))).astype(BF16),
            x_r=stack(r * jnp.exp(cum)),
            x_bk=jnp.concatenate([stack(b * p_inv), stack(k * p_inv)], axis=0).astype(BF16),
            v_st=stack(v).astype(BF16),
            z_hat=jnp.concatenate([stack(b * p_end), stack(k * p_end)], axis=0).astype(BF16),
            p_row=jnp.exp(c_end),
            rkr=(r * k * rk_ref[:, ln]).astype(BF16), v=v)

    ops = [operands(c, h) for c in chunks for h in range(nb)]
    big = [_dot_nt(jnp.concatenate([o["x_a"], o["x_r"].astype(BF16)], axis=0), o["x_bk"])
           for o in ops]
    n_pow = [jnp.where(strict, m[0:L2, 0:L2], 0.0) for m in big]
    a_ak = [jnp.where(strict, m[0:L2, L2:2 * L2], 0.0).astype(BF16) for m in big]
    a_r = [jnp.concatenate([jnp.where(incl, m[L2:2 * L2, 0:L2], 0.0),
                            jnp.where(incl, m[L2:2 * L2, L2:2 * L2], 0.0)], axis=1).astype(BF16)
           for m in big]

    t_inv = [eye + n for n in n_pow]
    for _ in range(5):
        n_pow = each(lambda n: _dot(n.astype(BF16), n.astype(BF16)), n_pow)
        t_inv = each(lambda t, n: t + _dot(t.astype(BF16), n.astype(BF16)), t_inv, n_pow)

    akv = each(lambda m, o: _dot(m, o["v_st"]).astype(BF16), a_ak, ops)
    w = each(lambda t, o, x: _dot(t.astype(BF16), jnp.concatenate([o["x_a"], x], axis=1)),
             t_inv, ops, akv)
    w1 = [m[:, 0:LANES].astype(BF16) for m in w]
    w2v = each(lambda m, o: jnp.concatenate([m[:, LANES:].astype(BF16), o["v_st"]], axis=0), w, ops)
    g_mat = each(lambda o, m, x: (o["x_r"] + _dot(m[:, 0:L2], x)).astype(BF16), ops, a_r, w1)
    y0 = each(_dot, a_r, w2v)
    m_mat = each(lambda x, o: _dot_tn(x, o["z_hat"][0:L2, :]).astype(BF16), w1, ops)
    c2 = each(lambda x, o: _dot_tn(x, o["z_hat"]), w2v, ops)
    bonus_v = [_dot(o["rkr"], head_ones) * o["v"] for o in ops]

    s = [s_scr[h] for h in range(nb)]
    y = []
    for i in range(len(ops)):
        h = i % nb
        s_b = s[h].astype(BF16)
        y_st = _dot_nt(g_mat[i], s_b) + y0[i]
        s[h] = s[h] * ops[i]["p_row"] + _dot(s_b, m_mat[i]) + c2[i]
        y.append(y_st[0:L, :] + y_st[L:L2, :])
    for h in range(nb):
        s_scr[h] = s[h]

    mean = [_dot(t.astype(BF16), head_ones) * (1.0 / RWKV_HEAD) for t in y]
    yc = each(lambda t, m: t - m, y, mean)
    var = [_dot((t * t).astype(BF16), head_ones) * (1.0 / RWKV_HEAD) for t in yc]
    for i in range(len(ops)):
        sl, ln = rows_of(i // nb), lanes_of(i % nb)
        yn = yc[i] * lax.rsqrt(var[i] + GN_EPS) * gnw_ref[:, ln] + gnb_ref[:, ln]
        o_ref[0, sl, ln] = ((yn + bonus_v[i]) * g_ref[0, sl, ln].astype(F32)).astype(o_ref.dtype)


def _rwkv_scan(r, k, v, kk, a, lw, g, gn_w, gn_b, r_k, rows):
    B, S, W = r.shape
    wb = SCAN_LANE_BLOCKS * LANES
    npair = W // wb
    seq = lambda: pl.BlockSpec((1, rows, wb), lambda b, p, c: (b, c, p))
    par = lambda: pl.BlockSpec((1, wb), lambda b, p, c: (0, p))
    return pl.pallas_call(
        functools.partial(_rwkv_scan_kernel, rows=rows),
        grid=(B, npair, S // rows),
        in_specs=[seq() for _ in range(7)] + [par(), par(), par()],
        out_specs=seq(),
        out_shape=jax.ShapeDtypeStruct((B, S, W), BF16),
        scratch_shapes=[pltpu.VMEM((SCAN_LANE_BLOCKS, LANES, LANES), F32)],
        compiler_params=_cparams(("parallel", "parallel", "arbitrary")),
        name="rwkv_scan",
    )(r, k, v, kk, a, lw, g, gn_w, gn_b, r_k)


def _rope128(x, cos, sin_signed):
    lane = lax.broadcasted_iota(I32, x.shape, 1)
    half = QK_ROPE // 2
    partner = jnp.where(lane < half, pltpu.roll(x, LANES - half, 1), pltpu.roll(x, half, 1))
    return x * cos + partner * sin_signed


def _mla_prep_kernel(cq_ref, ckv_ref, kr_ref, pos_ref, invf_ref, gqa_ref, gkva_ref,
                     wuq_ref, wukv_ref, gqn_ref, gkn_ref,
                     q_out, k_out, v_out, cqn_scr, ckvn_scr, cos_scr, sin_scr):
    @pl.when(pl.program_id(1) == 0)
    def _():
        cqn_scr[...] = _rms(cq_ref[...].astype(F32), gqa_ref[...]).astype(BF16)
        ckvn_scr[...] = _rms(ckv_ref[...].astype(F32), gkva_ref[...]).astype(BF16)
        ang = pos_ref[...].astype(F32) * invf_ref[...]
        lane = lax.broadcasted_iota(I32, ang.shape, 1)
        cos_scr[...] = jnp.cos(ang)
        sin_scr[...] = jnp.where(lane < QK_ROPE // 2, -1.0, 1.0) * jnp.sin(ang)

    tm = q_out.shape[0]
    hw = min(MLA_ROWS, tm)
    groups = [slice(g * hw, (g + 1) * hw) for g in range(tm // hw)]
    qs = [_dot(cqn_scr[r, :], wuq_ref[...]) for r in groups]
    kvs = [_dot(ckvn_scr[r, :], wukv_ref[...]) for r in groups]
    inv_n = 1.0 / QK_HEAD
    gkn = gkn_ref[...]
    for r, q, kv in zip(groups, qs, kvs):
        cos = cos_scr[r, :]
        sin = sin_scr[r, :]
        rs = lax.rsqrt(jnp.sum(q * q, axis=-1, keepdims=True) * inv_n + NORM_EPS)
        q = q * rs * gqn_ref[...] * (QK_HEAD ** -0.5 * LOG2E)
        q_out[r, :] = jnp.concatenate(
            [q[:, 0:LANES], _rope128(q[:, LANES:], cos, sin)], axis=1).astype(q_out.dtype)
        k_nope = kv[:, 0:QK_NOPE]
        kr = kr_ref[r, :].astype(F32)
        ssq = (jnp.sum(k_nope * k_nope, axis=-1, keepdims=True)
               + jnp.sum(kr * kr, axis=-1, keepdims=True))
        rs = lax.rsqrt(ssq * inv_n + NORM_EPS)
        k_out[r, :] = jnp.concatenate(
            [k_nope * rs * gkn[:, 0:LANES], _rope128(kr * rs * gkn[:, LANES:], cos, sin)],
            axis=1).astype(k_out.dtype)
        v_out[:, r] = jnp.transpose(kv[:, QK_NOPE:].astype(v_out.dtype))


def _mla_prep(proj, pos, invf, g_qa, g_kva, wuq, wukv, gqn, gkn, tm):
    T = proj.shape[0]
    H = MLA_HEADS
    c0 = COL_MLA // Q_LORA
    ckr = (COL_MLA + Q_LORA + KV_LORA) // LANES
    full = lambda shape: pl.BlockSpec(shape, lambda i, h: (0, 0))
    return pl.pallas_call(
        _mla_prep_kernel,
        grid=(T // tm, H),
        in_specs=[
            pl.BlockSpec((tm, Q_LORA), lambda i, h: (i, c0)),
            pl.BlockSpec((tm, KV_LORA), lambda i, h: (i, c0 + 1)),
            pl.BlockSpec((tm, LANES), lambda i, h: (i, ckr)),
            pl.BlockSpec((tm, 1), lambda i, h: (i, 0)),
            full((1, LANES)), full((1, Q_LORA)), full((1, KV_LORA)),
            pl.BlockSpec((Q_LORA, QK_PAD), lambda i, h: (0, h)),
            pl.BlockSpec((KV_LORA, QK_NOPE + V_HEAD), lambda i, h: (0, h)),
            full((1, QK_PAD)), full((1, QK_PAD)),
        ],
        out_specs=[
            pl.BlockSpec((tm, QK_PAD), lambda i, h: (i, h)),
            pl.BlockSpec((tm, QK_PAD), lambda i, h: (i, h)),
            pl.BlockSpec((V_HEAD, tm), lambda i, h: (h, i)),
        ],
        out_shape=[jax.ShapeDtypeStruct((T, H * QK_PAD), BF16),
                   jax.ShapeDtypeStruct((T, H * QK_PAD), BF16),
                   jax.ShapeDtypeStruct((H * V_HEAD, T), BF16)],
        scratch_shapes=[pltpu.VMEM((tm, Q_LORA), BF16), pltpu.VMEM((tm, KV_LORA), BF16),
                        pltpu.VMEM((tm, LANES), F32), pltpu.VMEM((tm, LANES), F32)],
        compiler_params=_cparams(("parallel", "arbitrary")),
        name="mla_prep",
    )(proj, proj, proj, pos, invf, g_qa, g_kva, wuq, wukv, gqn, gkn)


MASK_VALUE = -1e30
LOG2E = 1.4426950408889634


def _flash_kernel(q_ref, k_ref, vt_ref, o_ref, s_scr, p_scr, alpha_scr, m_scr, l_scr, acc_scr,
                  *, tq):
    i = pl.program_id(2)
    q = q_ref[0]
    m_scr[...] = jnp.full_like(m_scr, MASK_VALUE)
    l_scr[...] = jnp.zeros_like(l_scr)
    acc_scr[...] = jnp.zeros_like(acc_scr)

    q_half = tq // 2

    def scores(j, slot, half):
        ks = pl.ds(pl.multiple_of(j * tq, tq), tq)
        cols = slice(half * q_half, (half + 1) * q_half)
        s_scr[slot, :, cols] = _dot_nt(k_ref[0, ks, :], q[cols, :])

    def softmax(slot, masked, strips):
        for c in strips:
            cols = slice(c * LANES, (c + 1) * LANES)
            s = s_scr[slot, :, cols]
            if masked:
                kpos = lax.broadcasted_iota(I32, s.shape, 0)
                qpos = lax.broadcasted_iota(I32, s.shape, 1) + c * LANES
                s = jnp.where(kpos <= qpos, s, MASK_VALUE)
            m_old = m_scr[:, cols]
            m_new = jnp.maximum(m_old, jnp.max(s, axis=0, keepdims=True))
            p = jnp.exp2(s - m_new)
            alpha = jnp.exp2(m_old - m_new)
            l_scr[:, cols] = alpha * l_scr[:, cols] + jnp.sum(p, axis=0, keepdims=True)
            m_scr[:, cols] = m_new
            p_scr[slot, :, cols] = p.astype(BF16)
            alpha_scr[slot, :, cols] = alpha

    def weighted_values(j, slot):
        vt = vt_ref[:, pl.ds(pl.multiple_of(j * tq, tq), tq)]
        acc_scr[...] = alpha_scr[slot] * acc_scr[...] + _dot(vt, p_scr[slot])

    scores(0, 0, 0)
    scores(0, 0, 1)
    p_scr[1] = jnp.zeros((tq, tq), BF16)
    alpha_scr[1] = jnp.ones((1, tq), F32)
    n_strips = tq // LANES
    first, second = range(0, n_strips // 2), range(n_strips // 2, n_strips)

    def step(j, slot):
        weighted_values(jnp.maximum(j - 1, 0), 1 - slot)
        scores(j + 1, 1 - slot, 0)
        softmax(slot, False, first)
        scores(j + 1, 1 - slot, 1)
        softmax(slot, False, second)

    def pair(jp, carry):
        step(2 * jp, 0)
        step(2 * jp + 1, 1)
        return carry

    def finish(slot):
        weighted_values(jnp.maximum(i - 1, 0), 1 - slot)
        softmax(slot, True, range(n_strips))
        weighted_values(i, slot)
        o_ref[0] = jnp.transpose(acc_scr[...] / l_scr[...]).astype(o_ref.dtype)

    lax.fori_loop(0, i // 2, pair, 0)

    @pl.when(i % 2 == 0)
    def _():
        finish(0)

    @pl.when(i % 2 == 1)
    def _():
        step(i - 1, 0)
        finish(1)


def _flash(q, k, vt, tq):
    B, S, _ = q.shape
    H = MLA_HEADS
    return pl.pallas_call(
        functools.partial(_flash_kernel, tq=tq),
        grid=(B, H, S // tq),
        in_specs=[
            pl.BlockSpec((1, tq, QK_PAD), lambda b, h, i: (b, i, h)),
            pl.BlockSpec((1, S, QK_PAD), lambda b, h, i: (b, 0, h)),
            pl.BlockSpec((V_HEAD, S), lambda b, h, i: (h, b)),
        ],
        out_specs=pl.BlockSpec((1, tq, V_HEAD), lambda b, h, i: (b, i, h)),
        out_shape=jax.ShapeDtypeStruct((B, S, H * V_HEAD), BF16),
        scratch_shapes=[pltpu.VMEM((2, tq, tq), F32), pltpu.VMEM((2, tq, tq), BF16),
                        pltpu.VMEM((2, 1, tq), F32),
                        pltpu.VMEM((1, tq), F32), pltpu.VMEM((1, tq), F32),
                        pltpu.VMEM((V_HEAD, tq), F32)],
        compiler_params=_cparams(("parallel", "parallel", "arbitrary")),
        name="flash_attn",
    )(q, k, vt)


def _merge_kernel(oa_ref, ob_ref, wa_ref, wb_ref, ga_ref, gb_ref, o_ref):
    ya = _dot(oa_ref[...], wa_ref[...])
    yb = _dot(ob_ref[...], wb_ref[...])
    ga = _sigmoid(ga_ref[...].astype(F32))
    gb = _sigmoid(gb_ref[...].astype(F32))
    o_ref[...] = (ga * ya + gb * yb).astype(o_ref.dtype)


def _merge(oa, ob, wa, wb, proj, tm, tn):
    T, K = oa.shape
    N = wa.shape[1]
    ca = COL_GA // tn
    cb = COL_GB // tn
    return pl.pallas_call(
        _merge_kernel,
        grid=(T // tm, N // tn),
        in_specs=[
            pl.BlockSpec((tm, K), lambda i, j: (i, 0)),
            pl.BlockSpec((tm, K), lambda i, j: (i, 0)),
            pl.BlockSpec((K, tn), lambda i, j: (0, j)),
            pl.BlockSpec((K, tn), lambda i, j: (0, j)),
            pl.BlockSpec((tm, tn), lambda i, j: (i, ca + j)),
            pl.BlockSpec((tm, tn), lambda i, j: (i, cb + j)),
        ],
        out_specs=pl.BlockSpec((tm, tn), lambda i, j: (i, j)),
        out_shape=jax.ShapeDtypeStruct((T, N), BF16),
        compiler_params=_cparams(("parallel", "parallel")),
        name="merge",
    )(oa, ob, wa, wb, proj, proj)


def _pack_rows(x, out_ref, n, row0=0):
    for c in range(PACK_ROWS):
        lo = x[:, c * 2 * LANES:c * 2 * LANES + LANES]
        hi = x[:, c * 2 * LANES + LANES:(c + 1) * 2 * LANES]
        lo_b = pltpu.bitcast(lo.astype(BF16).astype(F32), U32)
        hi_b = pltpu.bitcast(hi.astype(BF16).astype(F32), U32)
        out_ref[pl.ds(row0 * PACK_ROWS + c, n, stride=PACK_ROWS), :] = hi_b | (lo_b >> 16)


def _unpack_rows(ref, c, tm, lead=None):
    idx = (pl.ds(c, tm, stride=PACK_ROWS), slice(None))
    if lead is not None:
        idx = (lead,) + idx
    u = ref[idx]
    lo = pltpu.bitcast(u << 16, F32)
    hi = pltpu.bitcast(u & jnp.uint32(0xFFFF0000), F32)
    return lo, hi


def _wo_kernel(m_ref, x_ref, wo_ref, g_ref, wr_ref, x1_ref, h2_ref, h2p_ref, lg_ref, *, tm):
    hw = WO_ROWS
    groups = [slice(h * hw, (h + 1) * hw) for h in range(tm // hw)]
    wo = wo_ref[...]
    x1 = [x_ref[r, :] + _dot(m_ref[r, :], wo) for r in groups]
    h2 = [_rms(v, g_ref[...]) for v in x1]
    w_hi, w_lo = _split_bf16(wr_ref[...])
    for h, r in enumerate(groups):
        x1_ref[r, :] = x1[h]
        h2_ref[r, :] = h2[h].astype(BF16)
        _pack_rows(h2[h], h2p_ref, hw, row0=h * hw)
        h_hi, h_lo = _split_bf16(h2[h])
        lg_ref[:, r] = _dot_nt(w_hi, h_hi) + _dot_nt(w_hi, h_lo) + _dot_nt(w_lo, h_hi)


def _wo(merged, x2, wo, g, wr_t, tm):
    T, D = x2.shape
    E = wr_t.shape[0]
    return pl.pallas_call(
        functools.partial(_wo_kernel, tm=tm),
        grid=(T // tm,),
        in_specs=[
            pl.BlockSpec((tm, D), lambda i: (i, 0)),
            pl.BlockSpec((tm, D), lambda i: (i, 0)),
            pl.BlockSpec((D, D), lambda i: (0, 0), pipeline_mode=pl.Buffered(1)),
            pl.BlockSpec((1, D), lambda i: (0, 0)),
            pl.BlockSpec((E, D), lambda i: (0, 0)),
        ],
        out_specs=[
            pl.BlockSpec((tm, D), lambda i: (i, 0)),
            pl.BlockSpec((tm, D), lambda i: (i, 0)),
            pl.BlockSpec((tm * PACK_ROWS, PACK_W), lambda i: (i, 0)),
            pl.BlockSpec((E, tm), lambda i: (0, i)),
        ],
        out_shape=[jax.ShapeDtypeStruct((T, D), F32),
                   jax.ShapeDtypeStruct((T, D), BF16),
                   jax.ShapeDtypeStruct((T * PACK_ROWS, PACK_W), U32),
                   jax.ShapeDtypeStruct((E, T), F32)],
        compiler_params=_cparams(("parallel",)),
        name="wo_norm_router",
    )(merged, x2, wo, g, wr_t)


def _route_kernel(lg_ref, bias_ref, eidx_ref, wts_ref, rnk_ref, cnt_ref, carry_scr, *, tm):
    E = N_EXPERTS
    NEG = -jnp.inf

    @pl.when(pl.program_id(0) == 0)
    def _():
        carry_scr[...] = jnp.zeros_like(carry_scr)

    scores = _sigmoid(lg_ref[...])
    choice = scores + bias_ref[...]
    c3 = choice.reshape(N_GROUPS, GROUP_SIZE, tm)
    sub = lax.broadcasted_iota(I32, c3.shape, 1)
    m1 = jnp.max(c3, axis=1, keepdims=True)
    i1 = jnp.min(jnp.where(c3 == m1, sub, GROUP_SIZE), axis=1, keepdims=True)
    m2 = jnp.max(jnp.where(sub == i1, NEG, c3), axis=1, keepdims=True)
    gs = (m1 + m2).reshape(N_GROUPS, tm)
    gi = lax.broadcasted_iota(I32, gs.shape, 0)
    gsel = jnp.zeros(gs.shape, F32)
    for _ in range(TOPK_GROUPS):
        mx = jnp.max(gs, axis=0, keepdims=True)
        ix = jnp.min(jnp.where(gs == mx, gi, N_GROUPS), axis=0, keepdims=True)
        hit = gi == ix
        gsel = jnp.where(hit, 1.0, gsel)
        gs = jnp.where(hit, NEG, gs)
    emask = jnp.broadcast_to(gsel.reshape(N_GROUPS, 1, tm), (N_GROUPS, GROUP_SIZE, tm)).reshape(E, tm)
    x = jnp.where(emask > 0.5, choice, NEG)
    ei = lax.broadcasted_iota(I32, x.shape, 0)
    sel = jnp.zeros(x.shape, F32)
    idx_rows, w_rows = [], []
    for _ in range(TOP_K):
        mx = jnp.max(x, axis=0, keepdims=True)
        ix = jnp.min(jnp.where(x == mx, ei, E), axis=0, keepdims=True)
        hit = ei == ix
        w_rows.append(jnp.sum(jnp.where(hit, scores, 0.0), axis=0, keepdims=True))
        idx_rows.append(ix)
        sel = jnp.where(hit, 1.0, sel)
        x = jnp.where(hit, NEG, x)
    w = jnp.concatenate(w_rows, axis=0)
    w = w / (jnp.sum(w, axis=0, keepdims=True) + 1e-20) * ROUTED_SCALE
    eidx_ref[...] = jnp.concatenate(idx_rows, axis=0)
    wts_ref[...] = w

    upper = (lax.broadcasted_iota(I32, (tm, tm), 0) < lax.broadcasted_iota(I32, (tm, tm), 1)).astype(BF16)
    base = carry_scr[...][:, 0:1]
    excl = _dot(sel.astype(BF16), upper) + base
    rnk_ref[...] = jnp.concatenate(
        [jnp.sum(jnp.where(ei == ix, excl, 0.0), axis=0, keepdims=True) for ix in idx_rows],
        axis=0).astype(I32)
    carry_scr[...] = carry_scr[...] + jnp.sum(sel, axis=1, keepdims=True)
    cnt_ref[...] = carry_scr[...]


def _route(logits_t, bias, tm):
    E, T = logits_t.shape
    K = TOP_K
    tok = lambda: pl.BlockSpec((K, tm), lambda i: (0, i))
    return pl.pallas_call(
        functools.partial(_route_kernel, tm=tm),
        grid=(T // tm,),
        in_specs=[pl.BlockSpec((E, tm), lambda i: (0, i)),
                  pl.BlockSpec((E, 1), lambda i: (0, 0))],
        out_specs=[tok(), tok(), tok(), pl.BlockSpec((E, LANES), lambda i: (0, 0))],
        out_shape=[jax.ShapeDtypeStruct((K, T), I32), jax.ShapeDtypeStruct((K, T), F32),
                   jax.ShapeDtypeStruct((K, T), I32), jax.ShapeDtypeStruct((E, LANES), F32)],
        scratch_shapes=[pltpu.VMEM((E, LANES), F32)],
        compiler_params=_cparams(("arbitrary",)),
        name="route",
    )(logits_t, bias)


def _slots_kernel(pst_ref, eidx_ref, rnk_ref, dest_ref):
    eidx = eidx_ref[...]
    dest = rnk_ref[...]
    for e in range(N_EXPERTS):
        dest = dest + jnp.where(eidx == e, pst_ref[e], 0)
    dest_ref[...] = dest


def _slots(eidx, rnk, pstart, tm):
    K, T = eidx.shape
    tok = lambda: pl.BlockSpec((K, tm), lambda i: (0, i))
    return pl.pallas_call(
        _slots_kernel,
        grid=(T // tm,),
        in_specs=[pl.BlockSpec(memory_space=pltpu.SMEM), tok(), tok()],
        out_specs=tok(),
        out_shape=jax.ShapeDtypeStruct((K, T), I32),
        compiler_params=_cparams(("parallel",)),
        name="slots",
    )(pstart, eidx, rnk)


def _ffn_kernel(te_ref, nu_ref, tok_ref, dst_ref, h_hbm, wg_ref, wu_ref, wd_ref, ys_hbm,
                gbuf, obuf, wg_b, wu_b, wd_b, gsem, osem, *, tm):
    s = pl.program_id(0)
    nu = nu_ref[0]
    tile_rows = tm * PACK_ROWS

    def gather(slot):
        for r in range(tm):
            pltpu.make_async_copy(
                h_hbm.at[pl.ds(pl.multiple_of(tok_ref[0, 0, r] * PACK_ROWS, PACK_ROWS), PACK_ROWS), :],
                gbuf.at[slot, pl.ds(r * PACK_ROWS, PACK_ROWS), :], gsem.at[slot]).start()

    def scatter(slot):
        for r in range(tm):
            pltpu.make_async_copy(
                obuf.at[slot, pl.ds(r * PACK_ROWS, PACK_ROWS), :],
                ys_hbm.at[pl.ds(pl.multiple_of(dst_ref[0, 0, r] * PACK_ROWS, PACK_ROWS), PACK_ROWS), :],
                osem.at[slot]).start()

    def wait_gather(slot):
        pltpu.make_async_copy(h_hbm.at[pl.ds(0, tile_rows), :], gbuf.at[slot], gsem.at[slot]).wait()

    def wait_scatter(slot):
        pltpu.make_async_copy(obuf.at[slot], ys_hbm.at[pl.ds(0, tile_rows), :], osem.at[slot]).wait()

    def compute(slot):
        parts = []
        for c in range(PACK_ROWS):
            lo, hi = _unpack_rows(gbuf.at[slot], c, tm)
            parts += [lo.astype(BF16), hi.astype(BF16)]
        a = jnp.concatenate(parts, axis=1)
        hg = _dot(a, wg_b[...])
        hu = _dot(a, wu_b[...])
        hid = (hg * _sigmoid(hg) * hu).astype(BF16)
        _pack_rows(_dot(hid, wd_b[...]), obuf.at[slot], tm)

    cur = jnp.maximum(s - 1, 0)
    @pl.when((s >= 1) & (s <= nu) & ((s == 1) | (te_ref[cur] != te_ref[jnp.maximum(s - 2, 0)])))
    def _():
        wg_b[...] = wg_ref[0].astype(BF16)
        wu_b[...] = wu_ref[0].astype(BF16)
        wd_b[...] = wd_ref[0].astype(BF16)

    @pl.when(s == 0)
    def _():
        gather(0)

    @pl.when(s == 1)
    def _():
        wait_gather(0)
        gather(1)
        compute(0)

    for slot in range(2):
        parity = s % 2 == slot

        @pl.when(parity & (s >= 3) & (s <= nu + 1))
        def _():
            wait_scatter(1 - slot)

        @pl.when(parity & (s >= 2) & (s <= nu))
        def _():
            wait_gather(1 - slot)
            gather(slot)
            scatter(slot)
            compute(1 - slot)

        @pl.when(parity & (s >= 2) & (s == nu + 1))
        def _():
            wait_gather(1 - slot)
            scatter(slot)
            wait_scatter(slot)


def _ffn(tile_e, n_used, row_tok, row_dst, h2p, wg, wu, wd, tm):
    nt = row_tok.shape[0]
    D, I = wg.shape[1], wg.shape[2]
    cur = lambda s: jnp.maximum(s - 1, 0)
    wspec = lambda shape: pl.BlockSpec(shape, lambda s, te, nu: (te[cur(s)], 0, 0))
    grid_spec = pltpu.PrefetchScalarGridSpec(
        num_scalar_prefetch=2,
        grid=(nt + 1,),
        in_specs=[
            pl.BlockSpec((1, 1, tm), lambda s, te, nu: (jnp.minimum(s, nt - 1), 0, 0),
                         memory_space=pltpu.SMEM),
            pl.BlockSpec((1, 1, tm), lambda s, te, nu: (jnp.maximum(s - 2, 0), 0, 0),
                         memory_space=pltpu.SMEM),
            pl.BlockSpec(memory_space=pl.ANY),
            wspec((1, D, I)), wspec((1, D, I)), wspec((1, I, D)),
        ],
        out_specs=pl.BlockSpec(memory_space=pl.ANY),
        scratch_shapes=[pltpu.VMEM((2, tm * PACK_ROWS, PACK_W), U32),
                        pltpu.VMEM((2, tm * PACK_ROWS, PACK_W), U32),
                        pltpu.VMEM((D, I), BF16), pltpu.VMEM((D, I), BF16), pltpu.VMEM((I, D), BF16),
                        pltpu.SemaphoreType.DMA((2,)), pltpu.SemaphoreType.DMA((2,))],
    )
    return pl.pallas_call(
        functools.partial(_ffn_kernel, tm=tm),
        grid_spec=grid_spec,
        out_shape=jax.ShapeDtypeStruct((nt * tm * PACK_ROWS, PACK_W), U32),
        compiler_params=_cparams(("arbitrary",)),
        name="expert_ffn",
    )(tile_e, n_used, row_tok, row_dst, h2p, wg, wu, wd)


def _combine_kernel(*refs, tm):
    ys_refs = refs[:TOP_K]
    wt_ref, x1_ref, h2_ref, wsg_ref, wsu_ref, wsd_ref, o_ref = refs[TOP_K:]
    h2 = h2_ref[...]
    hg = _dot(h2, wsg_ref[...])
    hu = _dot(h2, wsu_ref[...])
    base = x1_ref[...] + _dot((hg * _sigmoid(hg) * hu).astype(BF16), wsd_ref[...])
    wt = wt_ref[...]
    for c in range(PACK_ROWS):
        acc_lo = base[:, c * 2 * LANES:c * 2 * LANES + LANES]
        acc_hi = base[:, c * 2 * LANES + LANES:(c + 1) * 2 * LANES]
        for k in range(TOP_K):
            lo, hi = _unpack_rows(ys_refs[k], c, tm)
            wk = wt[:, k:k + 1]
            acc_lo = acc_lo + wk * lo
            acc_hi = acc_hi + wk * hi
        o_ref[:, c * 2 * LANES:c * 2 * LANES + LANES] = acc_lo
        o_ref[:, c * 2 * LANES + LANES:(c + 1) * 2 * LANES] = acc_hi


def _combine(ys, wts_t, x1, h2, wsg, wsu, wsd, tm):
    T, D = x1.shape
    K = wts_t.shape[1]
    I = wsg.shape[1]
    full = lambda shape: pl.BlockSpec(shape, lambda i: (0, 0))
    return pl.pallas_call(
        functools.partial(_combine_kernel, tm=tm),
        grid=(T // tm,),
        in_specs=[pl.BlockSpec((tm * PACK_ROWS, PACK_W), lambda i, k=k: (k * (T // tm) + i, 0))
                  for k in range(K)] + [
                  pl.BlockSpec((tm, K), lambda i: (i, 0)),
                  pl.BlockSpec((tm, D), lambda i: (i, 0)),
                  pl.BlockSpec((tm, D), lambda i: (i, 0)),
                  full((D, I)), full((D, I)), full((I, D))],
        out_specs=pl.BlockSpec((tm, D), lambda i: (i, 0)),
        out_shape=jax.ShapeDtypeStruct((T, D), F32),
        compiler_params=_cparams(("parallel",)),
        name="combine",
    )(*([ys] * K), wts_t, x1, h2, wsg, wsu, wsd)


def _ple_kernel(x_ref, p_ref, wp_ref, gpost_ref, gin_ref, wg_ref, o_ref, *, tm):
    hw = WO_ROWS
    groups = [slice(h * hw, (h + 1) * hw) for h in range(tm // hw)]
    wg = wg_ref[...]
    gates, ples = [], []
    for r in groups:
        hn = _rms(x_ref[r, :], gin_ref[...]).astype(BF16)
        gates.append(_dot(hn, wg))
        ples.append(_rms(_dot(p_ref[r, :].astype(BF16), wp_ref[...]), gpost_ref[...]))
    for r, gate, ple in zip(groups, gates, ples):
        o_ref[r, :] = x_ref[r, :] + _sigmoid(gate) * ple


def _ple(x2, p2, wp, gpost, gin, wg, tm):
    T, D = x2.shape
    Pd = p2.shape[1]
    full = lambda shape: pl.BlockSpec(shape, lambda i: (0, 0))
    return pl.pallas_call(
        functools.partial(_ple_kernel, tm=tm),
        grid=(T // tm,),
        in_specs=[
            pl.BlockSpec((tm, D), lambda i: (i, 0)),
            pl.BlockSpec((tm, Pd), lambda i: (i, 0)),
            full((Pd, D)), full((1, D)), full((1, D)),
            pl.BlockSpec((D, D), lambda i: (0, 0), pipeline_mode=pl.Buffered(1)),
        ],
        out_specs=pl.BlockSpec((tm, D), lambda i: (i, 0)),
        out_shape=jax.ShapeDtypeStruct((T, D), F32),
        compiler_params=_cparams(("parallel",)),
        name="ple",
    )(x2, p2, wp, gpost, gin, wg)


def _pad_cols(w, n):
    return jnp.pad(w, ((0, 0), (0, n - w.shape[1])))


def _pad_rows(w, n):
    return jnp.pad(w, ((0, n - w.shape[0]), (0, 0)))


def _layout_w_in(w):
    W = RWKV_WIDTH
    c = 3 * W
    segs = [w[:, 0:c],
            _pad_cols(w[:, c:c + DECAY_LORA], LORA_PAD),
            _pad_cols(w[:, c + DECAY_LORA:c + DECAY_LORA + AAA_LORA], LORA_PAD),
            _pad_cols(w[:, c + DECAY_LORA + AAA_LORA:c + DECAY_LORA + AAA_LORA + GATE_LORA], GATE_LORA_PAD)]
    c += DECAY_LORA + AAA_LORA + GATE_LORA
    mla_cols = Q_LORA + KV_LORA + QK_ROPE
    segs.append(_pad_cols(w[:, c:c + mla_cols], MLA_PAD))
    c += mla_cols
    segs.append(w[:, c:])
    out = jnp.concatenate(segs, axis=1).astype(BF16)
    assert out.shape[1] == IN_PAD
    return out


def _layout_mu(mu):
    W = RWKV_WIDTH
    c = 3 * W
    segs = [mu[0:c],
            jnp.pad(mu[c:c + DECAY_LORA], (0, LORA_PAD - DECAY_LORA)),
            jnp.pad(mu[c + DECAY_LORA:c + DECAY_LORA + AAA_LORA], (0, LORA_PAD - AAA_LORA)),
            jnp.pad(mu[c + DECAY_LORA + AAA_LORA:], (0, GATE_LORA_PAD - GATE_LORA))]
    return jnp.concatenate(segs).reshape(1, RWKV_PAD)


def _moe_tables(eidx, rnk, counts, tmf, tm_slots):
    K, T = eidx.shape
    E = N_EXPERTS
    A = K * T
    n_tiles = A // tmf + E
    P = n_tiles * tmf
    pcounts = (counts + tmf - 1) // tmf * tmf
    pends = jnp.cumsum(pcounts)
    pstart = (pends - pcounts).astype(I32)
    tile_start = jnp.arange(n_tiles, dtype=I32) * tmf
    tile_e = jnp.minimum(jnp.sum((pends[None, :] <= tile_start[:, None]).astype(I32), axis=1), E - 1)
    n_used = (pends[E - 1:] // tmf).astype(I32)
    dest = _slots(eidx, rnk, pstart, tm_slots)
    j = jnp.arange(tmf, dtype=I32)[None, :]
    tail = j < (pcounts - counts)[:, None]
    spare = jnp.cumsum(jnp.logical_not(tail).reshape(-1).astype(I32)) - 1
    pad_row = jnp.where(tail.reshape(-1), ((pstart + counts)[:, None] + j).reshape(-1),
                        pends[E - 1] + spare)
    q = jnp.arange(P - A, dtype=I32)
    slot_id = jnp.arange(A, dtype=I32)
    rows = jnp.concatenate([dest.reshape(-1), pad_row.astype(I32)])
    slots = jnp.concatenate([slot_id, A + q])
    row_dst = lax.sort((rows, slots), num_keys=1)[1]
    row_tok = jnp.where(row_dst < A, row_dst % T, 0)
    return (row_tok.reshape(n_tiles, 1, tmf), row_dst.reshape(n_tiles, 1, tmf), tile_e, n_used)


def _layer(x, p, positions, g_mix, w_in, mu_rwkv, w0, w2, a0, a2, g2, k_k, k_a, r_k,
           gn_w, gn_b, w_a_up, g_qa, g_kva, w_uq, w_ukv, g_qn, g_kn, w_b_up, w_o,
           g_ffn, w_router, router_bias, w_exp_gate, w_exp_up, w_exp_down,
           w_sh_gate, w_sh_up, w_sh_down, w_ple, g_ple_post, g_ple_in, w_ple_gate,
           *, tiles):
    B, S, D = x.shape
    T = B * S
    W = RWKV_WIDTH
    x2 = x.reshape(T, D)
    row = lambda v: v.reshape(1, -1).astype(F32)

    proj = _inproj(x2, row(g_mix), _layout_w_in(w_in), tiles["tm_in"], tiles["tn_in"])

    head_blk = (jnp.arange(W)[:, None] // RWKV_HEAD == jnp.arange(W)[None, :] // RWKV_HEAD).astype(BF16)
    r, k, v, kk, a, lw, g = _rwkv_prep(
        proj, _layout_mu(mu_rwkv), row(w0), _pad_rows(w2, LORA_PAD).astype(BF16), row(a0),
        _pad_rows(a2, LORA_PAD).astype(BF16), _pad_rows(g2, GATE_LORA_PAD).astype(BF16),
        row(k_k), row(k_a), head_blk, tiles["tm_prep"], S)
    b3 = lambda t: t.reshape(B, S, W)
    o_a = _rwkv_scan(b3(r), b3(k), b3(v), b3(kk), b3(a), b3(lw), b3(g),
                     row(gn_w), row(gn_b), row(r_k), tiles["rows_scan"]).reshape(T, W)

    half = QK_ROPE // 2
    inv_freq = ROPE_THETA ** (-jnp.arange(half, dtype=F32) / half)
    invf = jnp.concatenate([inv_freq, inv_freq, jnp.zeros((LANES - QK_ROPE,), F32)]).reshape(1, LANES)
    wuq = jnp.pad(w_uq.reshape(Q_LORA, MLA_HEADS, QK_HEAD),
                  ((0, 0), (0, 0), (0, QK_PAD - QK_HEAD))).reshape(Q_LORA, MLA_HEADS * QK_PAD)
    pad_g = lambda gv: jnp.pad(gv, (0, QK_PAD - QK_HEAD)).reshape(1, QK_PAD)
    q, kx, vx = _mla_prep(proj, positions.reshape(T, 1).astype(I32), invf, row(g_qa), row(g_kva),
                          wuq.astype(BF16), w_ukv.astype(BF16), pad_g(g_qn), pad_g(g_kn),
                          tiles["tm_mla"])
    o_b = _flash(q.reshape(B, S, -1), kx.reshape(B, S, -1), vx, tiles["tq"]).reshape(T, MLA_WIDTH)

    merged = _merge(o_a, o_b, w_a_up.astype(BF16), w_b_up.astype(BF16), proj,
                    tiles["tm_merge"], tiles["tn_merge"])

    x1, h2, h2p, logits_t = _wo(merged, x2, w_o.astype(BF16), row(g_ffn),
                                jnp.transpose(w_router).astype(F32), tiles["tm_wo"])
    eidx, wts, rnk, cnt = _route(logits_t, router_bias.reshape(N_EXPERTS, 1).astype(F32),
                                 tiles["tm_route"])
    tmf = tiles["tm_ffn"]
    counts = cnt[:, 0].astype(I32)
    row_tok, row_dst, tile_e, n_used = _moe_tables(eidx, rnk, counts, tmf, tiles["tm_slots"])
    ys = _ffn(tile_e, n_used, row_tok, row_dst, h2p, w_exp_gate, w_exp_up, w_exp_down, tmf)
    x3 = _combine(ys, jnp.transpose(wts), x1, h2,
                  w_sh_gate.astype(BF16), w_sh_up.astype(BF16), w_sh_down.astype(BF16),
                  tiles["tm_comb"])

    out = _ple(x3, p.reshape(T, PLE_DIM), w_ple.astype(BF16), row(g_ple_post), row(g_ple_in),
               w_ple_gate.astype(BF16), tiles["tm_ple"])
    return out.reshape(B, S, D)


TILES = dict(tm_in=1024, tn_in=1536, tm_prep=512, rows_scan=128, tm_mla=1024, tq=512,
             tm_merge=1024, tn_merge=1024, tm_wo=512, tm_route=512, tm_ffn=256,
             tm_slots=2048, tm_comb=256, tm_ple=512)


def kernel(x, p, positions, g_mix, w_in, mu_rwkv, w0, w2, a0, a2, g2, k_k, k_a, r_k, gn_w, gn_b, w_a_up, g_qa, g_kva, w_uq, w_ukv, g_qn, g_kn, w_b_up, w_o, g_ffn, w_router, router_bias, w_exp_gate, w_exp_up, w_exp_down, w_sh_gate, w_sh_up, w_sh_down, w_ple, g_ple_post, g_ple_in, w_ple_gate):
    args = (g_mix, w_in, mu_rwkv, w0, w2, a0, a2, g2, k_k, k_a, r_k, gn_w, gn_b, w_a_up,
            g_qa, g_kva, w_uq, w_ukv, g_qn, g_kn, w_b_up, w_o, g_ffn, w_router, router_bias,
            w_exp_gate, w_exp_up, w_exp_down, w_sh_gate, w_sh_up, w_sh_down, w_ple,
            g_ple_post, g_ple_in, w_ple_gate)
    assert all(t.shape[0] == 1 for t in args), "single-layer stack expected"
    return _layer(x, p[0], positions, *[t[0] for t in args], tiles=TILES)
```

```python
import functools

import jax
import jax.numpy as jnp
from jax import lax
from jax.experimental import pallas as pl
from jax.experimental.pallas import tpu as pltpu

F32 = jnp.float32
BF16 = jnp.bfloat16
I32 = jnp.int32
U32 = jnp.uint32

D_MODEL = 2048
PLE_DIM = 256
NORM_EPS = 1e-6
RWKV_HEADS = 16
RWKV_HEAD = 64
RWKV_WIDTH = RWKV_HEADS * RWKV_HEAD
DECAY_LORA = 64
AAA_LORA = 64
GATE_LORA = 160
GN_EPS = 64e-5
MLA_HEADS = 8
Q_LORA = 512
KV_LORA = 512
QK_NOPE = 128
QK_ROPE = 64
QK_HEAD = QK_NOPE + QK_ROPE
V_HEAD = 128
MLA_WIDTH = MLA_HEADS * V_HEAD
ROPE_THETA = 10000.0
N_EXPERTS = 64
TOP_K = 8
N_GROUPS = 8
GROUP_SIZE = N_EXPERTS // N_GROUPS
TOPK_GROUPS = 4
MOE_INTER = 512
ROUTED_SCALE = 2.5

LANES = 128
QK_PAD = 2 * LANES
PACK_ROWS = 8
PACK_W = D_MODEL // 2 // PACK_ROWS
assert PACK_W == LANES

LORA_PAD = 128
GATE_LORA_PAD = 256
RWKV_PAD = 3 * RWKV_WIDTH + 2 * LORA_PAD + GATE_LORA_PAD
MLA_PAD = 1536
COL_RWKV = 0
COL_MLA = RWKV_PAD
COL_GA = COL_MLA + MLA_PAD
COL_GB = COL_GA + D_MODEL
IN_PAD = COL_GB + D_MODEL

VMEM_LIMIT = 56 * 1024 * 1024
CHUNK = 64
WO_ROWS = 128
SCAN_LANE_BLOCKS = 8
MLA_ROWS = 256


def _cparams(sem):
    return pltpu.CompilerParams(dimension_semantics=sem, vmem_limit_bytes=VMEM_LIMIT)


def _dot(a, b):
    return jnp.dot(a, b, preferred_element_type=F32)


def _dot_nt(a, b):
    return lax.dot_general(a, b, (((1,), (1,)), ((), ())), preferred_element_type=F32)


def _dot_tn(a, b):
    return lax.dot_general(a, b, (((0,), (0,)), ((), ())), preferred_element_type=F32)


def _split_bf16(x):
    hi = x.astype(BF16)
    lo = (x - hi.astype(F32)).astype(BF16)
    return hi, lo


def _rms(x, g):
    ms = jnp.mean(x * x, axis=-1, keepdims=True)
    return x * lax.rsqrt(ms + NORM_EPS) * g


def _sigmoid(x):
    return 1.0 / (1.0 + jnp.exp(-x))


def _inproj_kernel(x_ref, g_ref, w_ref, o_ref, h_scr):
    @pl.when(pl.program_id(1) == 0)
    def _():
        h_scr[...] = _rms(x_ref[...], g_ref[...]).astype(BF16)

    o_ref[...] = _dot(h_scr[...], w_ref[...]).astype(o_ref.dtype)


def _inproj(x2, g, w, tm, tn):
    T, D = x2.shape
    N = w.shape[1]
    return pl.pallas_call(
        _inproj_kernel,
        grid=(T // tm, N // tn),
        in_specs=[
            pl.BlockSpec((tm, D), lambda i, j: (i, 0)),
            pl.BlockSpec((1, D), lambda i, j: (0, 0)),
            pl.BlockSpec((D, tn), lambda i, j: (0, j)),
        ],
        out_specs=pl.BlockSpec((tm, tn), lambda i, j: (i, j)),
        out_shape=jax.ShapeDtypeStruct((T, N), BF16),
        scratch_shapes=[pltpu.VMEM((tm, D), BF16)],
        compiler_params=_cparams(("parallel", "arbitrary")),
        name="inproj",
    )(x2, g, w)


HALO = 16


def _rwkv_prep_kernel(cur_ref, halo_ref, mu_ref, w0_ref, w2_ref, a0_ref, a2_ref, g2_ref,
                      kk_ref, ka_ref, bd_ref,
                      r_out, k_out, v_out, kk_out, a_out, lw_out, g_out, *, tm, seq):
    i = pl.program_id(0)
    first = (i * tm) % seq == 0
    W = RWKV_WIDTH

    def shifted(c0, c1):
        cur = cur_ref[:, c0:c1].astype(F32)
        last_prev = jnp.where(first, 0.0, halo_ref[HALO - 1:HALO, c0:c1].astype(F32))
        row = lax.broadcasted_iota(I32, cur.shape, 0)
        prev = jnp.where(row == 0, last_prev, pltpu.roll(cur, 1, 0))
        return cur + (prev - cur) * mu_ref[:, c0:c1]

    r = shifted(0, W)
    r_out[...] = r.astype(r_out.dtype)
    v_out[...] = shifted(2 * W, 3 * W).astype(v_out.dtype)
    lora = shifted(3 * W, RWKV_PAD)
    dw = lora[:, 0:LORA_PAD]
    da = lora[:, LORA_PAD:2 * LORA_PAD]
    dg = lora[:, 2 * LORA_PAD:]
    k = shifted(W, 2 * W)
    z = w0_ref[...] + _dot(jnp.tanh(dw).astype(BF16), w2_ref[...])
    w_log = -(jnp.maximum(-z, 0.0) + jnp.log(1.0 + jnp.exp(-jnp.abs(z)))) - 0.5
    lw_out[...] = -jnp.exp(w_log)
    a = _sigmoid(a0_ref[...] + _dot(da.astype(BF16), a2_ref[...]))
    g_out[...] = _dot(_sigmoid(dg).astype(BF16), g2_ref[...]).astype(g_out.dtype)
    kk = k * kk_ref[...]
    hi, lo = _split_bf16(kk * kk)
    ssq = _dot(hi, bd_ref[...]) + _dot(lo, bd_ref[...])
    kk = kk / jnp.maximum(jnp.sqrt(ssq), 1e-12)
    k_out[...] = (k * (1.0 + (a - 1.0) * ka_ref[...])).astype(k_out.dtype)
    kk_out[...] = kk.astype(kk_out.dtype)
    a_out[...] = a.astype(a_out.dtype)


def _rwkv_prep(proj, mu, w0, w2, a0, a2, g2, k_k, k_a, bd, tm, seq):
    T = proj.shape[0]
    W = RWKV_WIDTH
    full = lambda shape: pl.BlockSpec(shape, lambda i: (0, 0))
    out = lambda: pl.BlockSpec((tm, W), lambda i: (i, 0))
    return pl.pallas_call(
        functools.partial(_rwkv_prep_kernel, tm=tm, seq=seq),
        grid=(T // tm,),
        in_specs=[
            pl.BlockSpec((tm, RWKV_PAD), lambda i: (i, 0)),
            pl.BlockSpec((HALO, RWKV_PAD), lambda i: (jnp.maximum(i * (tm // HALO) - 1, 0), 0)),
            full((1, RWKV_PAD)), full((1, W)), full((LORA_PAD, W)), full((1, W)),
            full((LORA_PAD, W)), full((GATE_LORA_PAD, W)), full((1, W)), full((1, W)),
            full((W, W)),
        ],
        out_specs=[out() for _ in range(7)],
        out_shape=[jax.ShapeDtypeStruct((T, W), dt)
                   for dt in (BF16, BF16, BF16, BF16, BF16, F32, BF16)],
        compiler_params=_cparams(("parallel",)),
        name="rwkv_prep",
    )(proj, proj, mu, w0, w2, a0, a2, g2, k_k, k_a, bd)


def _rwkv_scan_kernel(r_ref, k_ref, v_ref, kk_ref, a_ref, lw_ref, g_ref,
                      gnw_ref, gnb_ref, rk_ref, o_ref, s_scr, *, rows):
    L = CHUNK
    L2 = 2 * L

    @pl.when(pl.program_id(2) == 0)
    def _():
        s_scr[...] = jnp.zeros_like(s_scr)

    lane = lax.broadcasted_iota(I32, (L, LANES), 1)
    m_a = (lane < RWKV_HEAD).astype(F32)
    m_b = 1.0 - m_a
    ri = lax.broadcasted_iota(I32, (L2, L2), 0)
    ci = lax.broadcasted_iota(I32, (L2, L2), 1)
    strict = ri > ci
    incl = ri >= ci
    eye = (ri == ci).astype(F32)
    tri = (lax.broadcasted_iota(I32, (L, L), 0) >= lax.broadcasted_iota(I32, (L, L), 1)).astype(BF16)
    hr = lax.broadcasted_iota(I32, (LANES, LANES), 0) // RWKV_HEAD
    hc = lax.broadcasted_iota(I32, (LANES, LANES), 1) // RWKV_HEAD
    head_ones = (hr == hc).astype(BF16)
    nb = SCAN_LANE_BLOCKS

    def stack(x):
        return jnp.concatenate([x * m_a, x * m_b], axis=0)

    chunks = range(rows // L)
    rows_of = lambda c: slice(c * L, (c + 1) * L)
    lanes_of = lambda h: slice(h * LANES, (h + 1) * LANES)
    each = lambda f, *lists: [f(*vals) for vals in zip(*lists)]

    def operands(c, h):
        sl, ln = rows_of(c), lanes_of(h)
        r = r_ref[0, sl, ln].astype(F32)
        k = k_ref[0, sl, ln].astype(F32)
        v = v_ref[0, sl, ln].astype(F32)
        kk = kk_ref[0, sl, ln].astype(F32)
        a = a_ref[0, sl, ln].astype(F32)
        lw = lw_ref[0, sl, ln]
        hi, lo = _split_bf16(lw)
        cum = _dot(tri, hi) + _dot(tri, lo)
        c_end = cum[L - 1:L, :]
        p_inv = jnp.exp(-cum)
        p_end = jnp.exp(c_end - cum)
        b = kk * a
        return dict(
            x_a=stack(-(kk * jnp.exp(cum - lw))).astype(BF16),
            x_r=stack(r * jnp.exp(cum)),
            x_bk=jnp.concatenate([stack(b * p_inv), stack(k * p_inv)], axis=0).astype(BF16),
            v_st=stack(v).astype(BF16),
            z_hat=jnp.concatenate([stack(b * p_end), stack(k * p_end)], axis=0).astype(BF16),
            p_row=jnp.exp(c_end),
            rkr=(r * k * rk_ref[:, ln]).astype(BF16), v=v)

    ops = [operands(c, h) for c in chunks for h in range(nb)]
    big = [_dot_nt(jnp.concatenate([o["x_a"], o["x_r"].astype(BF16)], axis=0), o["x_bk"])
           for o in ops]
    n_pow = [jnp.where(strict, m[0:L2, 0:L2], 0.0) for m in big]
    a_ak = [jnp.where(strict, m[0:L2, L2:2 * L2], 0.0).astype(BF16) for m in big]
    a_r = [jnp.concatenate([jnp.where(incl, m[L2:2 * L2, 0:L2], 0.0),
                            jnp.where(incl, m[L2:2 * L2, L2:2 * L2], 0.0)], axis=1).astype(BF16)
           for m in big]

    t_inv = [eye + n for n in n_pow]
    for _ in range(5):
        n_pow = each(lambda n: _dot(n.astype(BF16), n.astype(BF16)), n_pow)
        t_inv = each(lambda t, n: t + _dot(t.astype(BF16), n.astype(BF16)), t_inv, n_pow)

    akv = each(lambda m, o: _dot(m, o["v_st"]).astype(BF16), a_ak, ops)
    w = each(lambda t, o, x: _dot(t.astype(BF16), jnp.concatenate([o["x_a"], x], axis=1)),
             t_inv, ops, akv)
    w1 = [m[:, 0:LANES].astype(BF16) for m in w]
    w2v = each(lambda m, o: jnp.concatenate([m[:, LANES:].astype(BF16), o["v_st"]], axis=0), w, ops)
    g_mat = each(lambda o, m, x: (o["x_r"] + _dot(m[:, 0:L2], x)).astype(BF16), ops, a_r, w1)
    y0 = each(_dot, a_r, w2v)
    m_mat = each(lambda x, o: _dot_tn(x, o["z_hat"][0:L2, :]).astype(BF16), w1, ops)
    c2 = each(lambda x, o: _dot_tn(x, o["z_hat"]), w2v, ops)
    bonus_v = [_dot(o["rkr"], head_ones) * o["v"] for o in ops]

    s = [s_scr[h] for h in range(nb)]
    y = []
    for i in range(len(ops)):
        h = i % nb
        s_b = s[h].astype(BF16)
        y_st = _dot_nt(g_mat[i], s_b) + y0[i]
        s[h] = s[h] * ops[i]["p_row"] + _dot(s_b, m_mat[i]) + c2[i]
        y.append(y_st[0:L, :] + y_st[L:L2, :])
    for h in range(nb):
        s_scr[h] = s[h]

    mean = [_dot(t.astype(BF16), head_ones) * (1.0 / RWKV_HEAD) for t in y]
    yc = each(lambda t, m: t - m, y, mean)
    var = [_dot((t * t).astype(BF16), head_ones) * (1.0 / RWKV_HEAD) for t in yc]
    for i in range(len(ops)):
        sl, ln = rows_of(i // nb), lanes_of(i % nb)
        yn = yc[i] * lax.rsqrt(var[i] + GN_EPS) * gnw_ref[:, ln] + gnb_ref[:, ln]
        o_ref[0, sl, ln] = ((yn + bonus_v[i]) * g_ref[0, sl, ln].astype(F32)).astype(o_ref.dtype)


def _rwkv_scan(r, k, v, kk, a, lw, g, gn_w, gn_b, r_k, rows):
    B, S, W = r.shape
    wb = SCAN_LANE_BLOCKS * LANES
    npair = W // wb
    seq = lambda: pl.BlockSpec((1, rows, wb), lambda b, p, c: (b, c, p))
    par = lambda: pl.BlockSpec((1, wb), lambda b, p, c: (0, p))
    return pl.pallas_call(
        functools.partial(_rwkv_scan_kernel, rows=rows),
        grid=(B, npair, S // rows),
        in_specs=[seq() for _ in range(7)] + [par(), par(), par()],
        out_specs=seq(),
        out_shape=jax.ShapeDtypeStruct((B, S, W), BF16),
        scratch_shapes=[pltpu.VMEM((SCAN_LANE_BLOCKS, LANES, LANES), F32)],
        compiler_params=_cparams(("parallel", "parallel", "arbitrary")),
        name="rwkv_scan",
    )(r, k, v, kk, a, lw, g, gn_w, gn_b, r_k)


def _rope128(x, cos, sin_signed):
    lane = lax.broadcasted_iota(I32, x.shape, 1)
    half = QK_ROPE // 2
    partner = jnp.where(lane < half, pltpu.roll(x, LANES - half, 1), pltpu.roll(x, half, 1))
    return x * cos + partner * sin_signed


def _mla_prep_kernel(cq_ref, ckv_ref, kr_ref, pos_ref, invf_ref, gqa_ref, gkva_ref,
                     wuq_ref, wukv_ref, gqn_ref, gkn_ref,
                     q_out, k_out, v_out, cqn_scr, ckvn_scr, cos_scr, sin_scr):
    @pl.when(pl.program_id(1) == 0)
    def _():
        cqn_scr[...] = _rms(cq_ref[...].astype(F32), gqa_ref[...]).astype(BF16)
        ckvn_scr[...] = _rms(ckv_ref[...].astype(F32), gkva_ref[...]).astype(BF16)
        ang = pos_ref[...].astype(F32) * invf_ref[...]
        lane = lax.broadcasted_iota(I32, ang.shape, 1)
        cos_scr[...] = jnp.cos(ang)
        sin_scr[...] = jnp.where(lane < QK_ROPE // 2, -1.0, 1.0) * jnp.sin(ang)

    tm = q_out.shape[0]
    hw = min(MLA_ROWS, tm)
    groups = [slice(g * hw, (g + 1) * hw) for g in range(tm // hw)]
    qs = [_dot(cqn_scr[r, :], wuq_ref[...]) for r in groups]
    kvs = [_dot(ckvn_scr[r, :], wukv_ref[...]) for r in groups]
    inv_n = 1.0 / QK_HEAD
    gkn = gkn_ref[...]
    for r, q, kv in zip(groups, qs, kvs):
        cos = cos_scr[r, :]
        sin = sin_scr[r, :]
        rs = lax.rsqrt(jnp.sum(q * q, axis=-1, keepdims=True) * inv_n + NORM_EPS)
        q = q * rs * gqn_ref[...] * (QK_HEAD ** -0.5 * LOG2E)
        q_out[r, :] = jnp.concatenate(
            [q[:, 0:LANES], _rope128(q[:, LANES:], cos, sin)], axis=1).astype(q_out.dtype)
        k_nope = kv[:, 0:QK_NOPE]
        kr = kr_ref[r, :].astype(F32)
        ssq = (jnp.sum(k_nope * k_nope, axis=-1, keepdims=True)
               + jnp.sum(kr * kr, axis=-1, keepdims=True))
        rs = lax.rsqrt(ssq * inv_n + NORM_EPS)
        k_out[r, :] = jnp.concatenate(
            [k_nope * rs * gkn[:, 0:LANES], _rope128(kr * rs * gkn[:, LANES:], cos, sin)],
            axis=1).astype(k_out.dtype)
        v_out[:, r] = jnp.transpose(kv[:, QK_NOPE:].astype(v_out.dtype))


def _mla_prep(proj, pos, invf, g_qa, g_kva, wuq, wukv, gqn, gkn, tm):
    T = proj.shape[0]
    H = MLA_HEADS
    c0 = COL_MLA // Q_LORA
    ckr = (COL_MLA + Q_LORA + KV_LORA) // LANES
    full = lambda shape: pl.BlockSpec(shape, lambda i, h: (0, 0))
    return pl.pallas_call(
        _mla_prep_kernel,
        grid=(T // tm, H),
        in_specs=[
            pl.BlockSpec((tm, Q_LORA), lambda i, h: (i, c0)),
            pl.BlockSpec((tm, KV_LORA), lambda i, h: (i, c0 + 1)),
            pl.BlockSpec((tm, LANES), lambda i, h: (i, ckr)),
            pl.BlockSpec((tm, 1), lambda i, h: (i, 0)),
            full((1, LANES)), full((1, Q_LORA)), full((1, KV_LORA)),
            pl.BlockSpec((Q_LORA, QK_PAD), lambda i, h: (0, h)),
            pl.BlockSpec((KV_LORA, QK_NOPE + V_HEAD), lambda i, h: (0, h)),
            full((1, QK_PAD)), full((1, QK_PAD)),
        ],
        out_specs=[
            pl.BlockSpec((tm, QK_PAD), lambda i, h: (i, h)),
            pl.BlockSpec((tm, QK_PAD), lambda i, h: (i, h)),
            pl.BlockSpec((V_HEAD, tm), lambda i, h: (h, i)),
        ],
        out_shape=[jax.ShapeDtypeStruct((T, H * QK_PAD), BF16),
                   jax.ShapeDtypeStruct((T, H * QK_PAD), BF16),
                   jax.ShapeDtypeStruct((H * V_HEAD, T), BF16)],
        scratch_shapes=[pltpu.VMEM((tm, Q_LORA), BF16), pltpu.VMEM((tm, KV_LORA), BF16),
                        pltpu.VMEM((tm, LANES), F32), pltpu.VMEM((tm, LANES), F32)],
        compiler_params=_cparams(("parallel", "arbitrary")),
        name="mla_prep",
    )(proj, proj, proj, pos, invf, g_qa, g_kva, wuq, wukv, gqn, gkn)


MASK_VALUE = -1e30
LOG2E = 1.4426950408889634


def _flash_kernel(q_ref, k_ref, vt_ref, o_ref, s_scr, p_scr, alpha_scr, m_scr, l_scr, acc_scr,
                  *, tq):
    i = pl.program_id(2)
    q = q_ref[0]
    m_scr[...] = jnp.full_like(m_scr, MASK_VALUE)
    l_scr[...] = jnp.zeros_like(l_scr)
    acc_scr[...] = jnp.zeros_like(acc_scr)

    q_half = tq // 2

    def scores(j, slot, half):
        ks = pl.ds(pl.multiple_of(j * tq, tq), tq)
        cols = slice(half * q_half, (half + 1) * q_half)
        s_scr[slot, :, cols] = _dot_nt(k_ref[0, ks, :], q[cols, :])

    def softmax(slot, masked, strips):
        for c in strips:
            cols = slice(c * LANES, (c + 1) * LANES)
            s = s_scr[slot, :, cols]
            if masked:
                kpos = lax.broadcasted_iota(I32, s.shape, 0)
                qpos = lax.broadcasted_iota(I32, s.shape, 1) + c * LANES
                s = jnp.where(kpos <= qpos, s, MASK_VALUE)
            m_old = m_scr[:, cols]
            m_new = jnp.maximum(m_old, jnp.max(s, axis=0, keepdims=True))
            p = jnp.exp2(s - m_new)
            alpha = jnp.exp2(m_old - m_new)
            l_scr[:, cols] = alpha * l_scr[:, cols] + jnp.sum(p, axis=0, keepdims=True)
            m_scr[:, cols] = m_new
            p_scr[slot, :, cols] = p.astype(BF16)
            alpha_scr[slot, :, cols] = alpha

    def weighted_values(j, slot):
        vt = vt_ref[:, pl.ds(pl.multiple_of(j * tq, tq), tq)]
        acc_scr[...] = alpha_scr[slot] * acc_scr[...] + _dot(vt, p_scr[slot])

    scores(0, 0, 0)
    scores(0, 0, 1)
    p_scr[1] = jnp.zeros((tq, tq), BF16)
    alpha_scr[1] = jnp.ones((1, tq), F32)
    n_strips = tq // LANES
    first, second = range(0, n_strips // 2), range(n_strips // 2, n_strips)

    def step(j, slot):
        weighted_values(jnp.maximum(j - 1, 0), 1 - slot)
        scores(j + 1, 1 - slot, 0)
        softmax(slot, False, first)
        scores(j + 1, 1 - slot, 1)
        softmax(slot, False, second)

    def pair(jp, carry):
        step(2 * jp, 0)
        step(2 * jp + 1, 1)
        return carry

    def finish(slot):
        weighted_values(jnp.maximum(i - 1, 0), 1 - slot)
        softmax(slot, True, range(n_strips))
        weighted_values(i, slot)
        o_ref[0] = jnp.transpose(acc_scr[...] / l_scr[...]).astype(o_ref.dtype)

    lax.fori_loop(0, i // 2, pair, 0)

    @pl.when(i % 2 == 0)
    def _():
        finish(0)

    @pl.when(i % 2 == 1)
    def _():
        step(i - 1, 0)
        finish(1)


def _flash(q, k, vt, tq):
    B, S, _ = q.shape
    H = MLA_HEADS
    return pl.pallas_call(
        functools.partial(_flash_kernel, tq=tq),
        grid=(B, H, S // tq),
        in_specs=[
            pl.BlockSpec((1, tq, QK_PAD), lambda b, h, i: (b, i, h)),
            pl.BlockSpec((1, S, QK_PAD), lambda b, h, i: (b, 0, h)),
            pl.BlockSpec((V_HEAD, S), lambda b, h, i: (h, b)),
        ],
        out_specs=pl.BlockSpec((1, tq, V_HEAD), lambda b, h, i: (b, i, h)),
        out_shape=jax.ShapeDtypeStruct((B, S, H * V_HEAD), BF16),
        scratch_shapes=[pltpu.VMEM((2, tq, tq), F32), pltpu.VMEM((2, tq, tq), BF16),
                        pltpu.VMEM((2, 1, tq), F32),
                        pltpu.VMEM((1, tq), F32), pltpu.VMEM((1, tq), F32),
                        pltpu.VMEM((V_HEAD, tq), F32)],
        compiler_params=_cparams(("parallel", "parallel", "arbitrary")),
        name="flash_attn",
    )(q, k, vt)


def _merge_kernel(oa_ref, ob_ref, wa_ref, wb_ref, ga_ref, gb_ref, o_ref):
    ya = _dot(oa_ref[...], wa_ref[...])
    yb = _dot(ob_ref[...], wb_ref[...])
    ga = _sigmoid(ga_ref[...].astype(F32))
    gb = _sigmoid(gb_ref[...].astype(F32))
    o_ref[...] = (ga * ya + gb * yb).astype(o_ref.dtype)


def _merge(oa, ob, wa, wb, proj, tm, tn):
    T, K = oa.shape
    N = wa.shape[1]
    ca = COL_GA // tn
    cb = COL_GB // tn
    return pl.pallas_call(
        _merge_kernel,
        grid=(T // tm, N // tn),
        in_specs=[
            pl.BlockSpec((tm, K), lambda i, j: (i, 0)),
            pl.BlockSpec((tm, K), lambda i, j: (i, 0)),
            pl.BlockSpec((K, tn), lambda i, j: (0, j)),
            pl.BlockSpec((K, tn), lambda i, j: (0, j)),
            pl.BlockSpec((tm, tn), lambda i, j: (i, ca + j)),
            pl.BlockSpec((tm, tn), lambda i, j: (i, cb + j)),
        ],
        out_specs=pl.BlockSpec((tm, tn), lambda i, j: (i, j)),
        out_shape=jax.ShapeDtypeStruct((T, N), BF16),
        compiler_params=_cparams(("parallel", "parallel")),
        name="merge",
    )(oa, ob, wa, wb, proj, proj)


def _pack_rows(x, out_ref, n, row0=0):
    for c in range(PACK_ROWS):
        lo = x[:, c * 2 * LANES:c * 2 * LANES + LANES]
        hi = x[:, c * 2 * LANES + LANES:(c + 1) * 2 * LANES]
        lo_b = pltpu.bitcast(lo.astype(BF16).astype(F32), U32)
        hi_b = pltpu.bitcast(hi.astype(BF16).astype(F32), U32)
        out_ref[pl.ds(row0 * PACK_ROWS + c, n, stride=PACK_ROWS), :] = hi_b | (lo_b >> 16)


def _unpack_rows(ref, c, tm, lead=None):
    idx = (pl.ds(c, tm, stride=PACK_ROWS), slice(None))
    if lead is not None:
        idx = (lead,) + idx
    u = ref[idx]
    lo = pltpu.bitcast(u << 16, F32)
    hi = pltpu.bitcast(u & jnp.uint32(0xFFFF0000), F32)
    return lo, hi


def _wo_kernel(m_ref, x_ref, wo_ref, g_ref, wr_ref, x1_ref, h2_ref, h2p_ref, lg_ref, *, tm):
    hw = WO_ROWS
    groups = [slice(h * hw, (h + 1) * hw) for h in range(tm // hw)]
    wo = wo_ref[...]
    x1 = [x_ref[r, :] + _dot(m_ref[r, :], wo) for r in groups]
    h2 = [_rms(v, g_ref[...]) for v in x1]
    w_hi, w_lo = _split_bf16(wr_ref[...])
    for h, r in enumerate(groups):
        x1_ref[r, :] = x1[h]
        h2_ref[r, :] = h2[h].astype(BF16)
        _pack_rows(h2[h], h2p_ref, hw, row0=h * hw)
        h_hi, h_lo = _split_bf16(h2[h])
        lg_ref[:, r] = _dot_nt(w_hi, h_hi) + _dot_nt(w_hi, h_lo) + _dot_nt(w_lo, h_hi)


def _wo(merged, x2, wo, g, wr_t, tm):
    T, D = x2.shape
    E = wr_t.shape[0]
    return pl.pallas_call(
        functools.partial(_wo_kernel, tm=tm),
        grid=(T // tm,),
        in_specs=[
            pl.BlockSpec((tm, D), lambda i: (i, 0)),
            pl.BlockSpec((tm, D), lambda i: (i, 0)),
            pl.BlockSpec((D, D), lambda i: (0, 0), pipeline_mode=pl.Buffered(1)),
            pl.BlockSpec((1, D), lambda i: (0, 0)),
            pl.BlockSpec((E, D), lambda i: (0, 0)),
        ],
        out_specs=[
            pl.BlockSpec((tm, D), lambda i: (i, 0)),
            pl.BlockSpec((tm, D), lambda i: (i, 0)),
            pl.BlockSpec((tm * PACK_ROWS, PACK_W), lambda i: (i, 0)),
            pl.BlockSpec((E, tm), lambda i: (0, i)),
        ],
        out_shape=[jax.ShapeDtypeStruct((T, D), F32),
                   jax.ShapeDtypeStruct((T, D), BF16),
                   jax.ShapeDtypeStruct((T * PACK_ROWS, PACK_W), U32),
                   jax.ShapeDtypeStruct((E, T), F32)],
        compiler_params=_cparams(("parallel",)),
        name="wo_norm_router",
    )(merged, x2, wo, g, wr_t)


def _route_kernel(lg_ref, bias_ref, eidx_ref, wts_ref, rnk_ref, cnt_ref, carry_scr, *, tm):
    E = N_EXPERTS
    NEG = -jnp.inf

    @pl.when(pl.program_id(0) == 0)
    def _():
        carry_scr[...] = jnp.zeros_like(carry_scr)

    scores = _sigmoid(lg_ref[...])
    choice = scores + bias_ref[...]
    c3 = choice.reshape(N_GROUPS, GROUP_SIZE, tm)
    sub = lax.broadcasted_iota(I32, c3.shape, 1)
    m1 = jnp.max(c3, axis=1, keepdims=True)
    i1 = jnp.min(jnp.where(c3 == m1, sub, GROUP_SIZE), axis=1, keepdims=True)
    m2 = jnp.max(jnp.where(sub == i1, NEG, c3), axis=1, keepdims=True)
    gs = (m1 + m2).reshape(N_GROUPS, tm)
    gi = lax.broadcasted_iota(I32, gs.shape, 0)
    gsel = jnp.zeros(gs.shape, F32)
    for _ in range(TOPK_GROUPS):
        mx = jnp.max(gs, axis=0, keepdims=True)
        ix = jnp.min(jnp.where(gs == mx, gi, N_GROUPS), axis=0, keepdims=True)
        hit = gi == ix
        gsel = jnp.where(hit, 1.0, gsel)
        gs = jnp.where(hit, NEG, gs)
    emask = jnp.broadcast_to(gsel.reshape(N_GROUPS, 1, tm), (N_GROUPS, GROUP_SIZE, tm)).reshape(E, tm)
    x = jnp.where(emask > 0.5, choice, NEG)
    ei = lax.broadcasted_iota(I32, x.shape, 0)
    sel = jnp.zeros(x.shape, F32)
    idx_rows, w_rows = [], []
    for _ in range(TOP_K):
        mx = jnp.max(x, axis=0, keepdims=True)
        ix = jnp.min(jnp.where(x == mx, ei, E), axis=0, keepdims=True)
        hit = ei == ix
        w_rows.append(jnp.sum(jnp.where(hit, scores, 0.0), axis=0, keepdims=True))
        idx_rows.append(ix)
        sel = jnp.where(hit, 1.0, sel)
        x = jnp.where(hit, NEG, x)
    w = jnp.concatenate(w_rows, axis=0)
    w = w / (jnp.sum(w, axis=0, keepdims=True) + 1e-20) * ROUTED_SCALE
    eidx_ref[...] = jnp.concatenate(idx_rows, axis=0)
    wts_ref[...] = w

    upper = (lax.broadcasted_iota(I32, (tm, tm), 0) < lax.broadcasted_iota(I32, (tm, tm), 1)).astype(BF16)
    base = carry_scr[...][:, 0:1]
    excl = _dot(sel.astype(BF16), upper) + base
    rnk_ref[...] = jnp.concatenate(
        [jnp.sum(jnp.where(ei == ix, excl, 0.0), axis=0, keepdims=True) for ix in idx_rows],
        axis=0).astype(I32)
    carry_scr[...] = carry_scr[...] + jnp.sum(sel, axis=1, keepdims=True)
    cnt_ref[...] = carry_scr[...]


def _route(logits_t, bias, tm):
    E, T = logits_t.shape
    K = TOP_K
    tok = lambda: pl.BlockSpec((K, tm), lambda i: (0, i))
    return pl.pallas_call(
        functools.partial(_route_kernel, tm=tm),
        grid=(T // tm,),
        in_specs=[pl.BlockSpec((E, tm), lambda i: (0, i)),
                  pl.BlockSpec((E, 1), lambda i: (0, 0))],
        out_specs=[tok(), tok(), tok(), pl.BlockSpec((E, LANES), lambda i: (0, 0))],
        out_shape=[jax.ShapeDtypeStruct((K, T), I32), jax.ShapeDtypeStruct((K, T), F32),
                   jax.ShapeDtypeStruct((K, T), I32), jax.ShapeDtypeStruct((E, LANES), F32)],
        scratch_shapes=[pltpu.VMEM((E, LANES), F32)],
        compiler_params=_cparams(("arbitrary",)),
        name="route",
    )(logits_t, bias)


def _slots_kernel(pst_ref, eidx_ref, rnk_ref, dest_ref):
    eidx = eidx_ref[...]
    dest = rnk_ref[...]
    for e in range(N_EXPERTS):
        dest = dest + jnp.where(eidx == e, pst_ref[e], 0)
    dest_ref[...] = dest


def _slots(eidx, rnk, pstart, tm):
    K, T = eidx.shape
    tok = lambda: pl.BlockSpec((K, tm), lambda i: (0, i))
    return pl.pallas_call(
        _slots_kernel,
        grid=(T // tm,),
        in_specs=[pl.BlockSpec(memory_space=pltpu.SMEM), tok(), tok()],
        out_specs=tok(),
        out_shape=jax.ShapeDtypeStruct((K, T), I32),
        compiler_params=_cparams(("parallel",)),
        name="slots",
    )(pstart, eidx, rnk)


def _ffn_kernel(te_ref, nu_ref, tok_ref, dst_ref, h_hbm, wg_ref, wu_ref, wd_ref, ys_hbm,
                gbuf, obuf, wg_b, wu_b, wd_b, gsem, osem, *, tm):
    s = pl.program_id(0)
    nu = nu_ref[0]
    tile_rows = tm * PACK_ROWS

    def gather(slot):
        for r in range(tm):
            pltpu.make_async_copy(
                h_hbm.at[pl.ds(pl.multiple_of(tok_ref[0, 0, r] * PACK_ROWS, PACK_ROWS), PACK_ROWS), :],
                gbuf.at[slot, pl.ds(r * PACK_ROWS, PACK_ROWS), :], gsem.at[slot]).start()

    def scatter(slot):
        for r in range(tm):
            pltpu.make_async_copy(
                obuf.at[slot, pl.ds(r * PACK_ROWS, PACK_ROWS), :],
                ys_hbm.at[pl.ds(pl.multiple_of(dst_ref[0, 0, r] * PACK_ROWS, PACK_ROWS), PACK_ROWS), :],
                osem.at[slot]).start()

    def wait_gather(slot):
        pltpu.make_async_copy(h_hbm.at[pl.ds(0, tile_rows), :], gbuf.at[slot], gsem.at[slot]).wait()

    def wait_scatter(slot):
        pltpu.make_async_copy(obuf.at[slot], ys_hbm.at[pl.ds(0, tile_rows), :], osem.at[slot]).wait()

    def compute(slot):
        parts = []
        for c in range(PACK_ROWS):
            lo, hi = _unpack_rows(gbuf.at[slot], c, tm)
            parts += [lo.astype(BF16), hi.astype(BF16)]
        a = jnp.concatenate(parts, axis=1)
        hg = _dot(a, wg_b[...])
        hu = _dot(a, wu_b[...])
        hid = (hg * _sigmoid(hg) * hu).astype(BF16)
        _pack_rows(_dot(hid, wd_b[...]), obuf.at[slot], tm)

    cur = jnp.maximum(s - 1, 0)
    @pl.when((s >= 1) & (s <= nu) & ((s == 1) | (te_ref[cur] != te_ref[jnp.maximum(s - 2, 0)])))
    def _():
        wg_b[...] = wg_ref[0].astype(BF16)
        wu_b[...] = wu_ref[0].astype(BF16)
        wd_b[...] = wd_ref[0].astype(BF16)

    @pl.when(s == 0)
    def _():
        gather(0)

    @pl.when(s == 1)
    def _():
        wait_gather(0)
        gather(1)
        compute(0)

    for slot in range(2):
        parity = s % 2 == slot

        @pl.when(parity & (s >= 3) & (s <= nu + 1))
        def _():
            wait_scatter(1 - slot)

        @pl.when(parity & (s >= 2) & (s <= nu))
        def _():
            wait_gather(1 - slot)
            gather(slot)
            scatter(slot)
            compute(1 - slot)

        @pl.when(parity & (s >= 2) & (s == nu + 1))
        def _():
            wait_gather(1 - slot)
            scatter(slot)
            wait_scatter(slot)


def _ffn(tile_e, n_used, row_tok, row_dst, h2p, wg, wu, wd, tm):
    nt = row_tok.shape[0]
    D, I = wg.shape[1], wg.shape[2]
    cur = lambda s: jnp.maximum(s - 1, 0)
    wspec = lambda shape: pl.BlockSpec(shape, lambda s, te, nu: (te[cur(s)], 0, 0))
    grid_spec = pltpu.PrefetchScalarGridSpec(
        num_scalar_prefetch=2,
        grid=(nt + 1,),
        in_specs=[
            pl.BlockSpec((1, 1, tm), lambda s, te, nu: (jnp.minimum(s, nt - 1), 0, 0),
                         memory_space=pltpu.SMEM),
            pl.BlockSpec((1, 1, tm), lambda s, te, nu: (jnp.maximum(s - 2, 0), 0, 0),
                         memory_space=pltpu.SMEM),
            pl.BlockSpec(memory_space=pl.ANY),
            wspec((1, D, I)), wspec((1, D, I)), wspec((1, I, D)),
        ],
        out_specs=pl.BlockSpec(memory_space=pl.ANY),
        scratch_shapes=[pltpu.VMEM((2, tm * PACK_ROWS, PACK_W), U32),
                        pltpu.VMEM((2, tm * PACK_ROWS, PACK_W), U32),
                        pltpu.VMEM((D, I), BF16), pltpu.VMEM((D, I), BF16), pltpu.VMEM((I, D), BF16),
                        pltpu.SemaphoreType.DMA((2,)), pltpu.SemaphoreType.DMA((2,))],
    )
    return pl.pallas_call(
        functools.partial(_ffn_kernel, tm=tm),
        grid_spec=grid_spec,
        out_shape=jax.ShapeDtypeStruct((nt * tm * PACK_ROWS, PACK_W), U32),
        compiler_params=_cparams(("arbitrary",)),
        name="expert_ffn",
    )(tile_e, n_used, row_tok, row_dst, h2p, wg, wu, wd)


def _combine_kernel(*refs, tm):
    ys_refs = refs[:TOP_K]
    wt_ref, x1_ref, h2_ref, wsg_ref, wsu_ref, wsd_ref, o_ref = refs[TOP_K:]
    h2 = h2_ref[...]
    hg = _dot(h2, wsg_ref[...])
    hu = _dot(h2, wsu_ref[...])
    base = x1_ref[...] + _dot((hg * _sigmoid(hg) * hu).astype(BF16), wsd_ref[...])
    wt = wt_ref[...]
    for c in range(PACK_ROWS):
        acc_lo = base[:, c * 2 * LANES:c * 2 * LANES + LANES]
        acc_hi = base[:, c * 2 * LANES + LANES:(c + 1) * 2 * LANES]
        for k in range(TOP_K):
            lo, hi = _unpack_rows(ys_refs[k], c, tm)
            wk = wt[:, k:k + 1]
            acc_lo = acc_lo + wk * lo
            acc_hi = acc_hi + wk * hi
        o_ref[:, c * 2 * LANES:c * 2 * LANES + LANES] = acc_lo
        o_ref[:, c * 2 * LANES + LANES:(c + 1) * 2 * LANES] = acc_hi


def _combine(ys, wts_t, x1, h2, wsg, wsu, wsd, tm):
    T, D = x1.shape
    K = wts_t.shape[1]
    I = wsg.shape[1]
    full = lambda shape: pl.BlockSpec(shape, lambda i: (0, 0))
    return pl.pallas_call(
        functools.partial(_combine_kernel, tm=tm),
        grid=(T // tm,),
        in_specs=[pl.BlockSpec((tm * PACK_ROWS, PACK_W), lambda i, k=k: (k * (T // tm) + i, 0))
                  for k in range(K)] + [
                  pl.BlockSpec((tm, K), lambda i: (i, 0)),
                  pl.BlockSpec((tm, D), lambda i: (i, 0)),
                  pl.BlockSpec((tm, D), lambda i: (i, 0)),
                  full((D, I)), full((D, I)), full((I, D))],
        out_specs=pl.BlockSpec((tm, D), lambda i: (i, 0)),
        out_shape=jax.ShapeDtypeStruct((T, D), F32),
        compiler_params=_cparams(("parallel",)),
        name="combine",
    )(*([ys] * K), wts_t, x1, h2, wsg, wsu, wsd)


def _ple_kernel(x_ref, p_ref, wp_ref, gpost_ref, gin_ref, wg_ref, o_ref, *, tm):
    hw = WO_ROWS
    groups = [slice(h * hw, (h + 1) * hw) for h in range(tm // hw)]
    wg = wg_ref[...]
    gates, ples = [], []
    for r in groups:
        hn = _rms(x_ref[r, :], gin_ref[...]).astype(BF16)
        gates.append(_dot(hn, wg))
        ples.append(_rms(_dot(p_ref[r, :].astype(BF16), wp_ref[...]), gpost_ref[...]))
    for r, gate, ple in zip(groups, gates, ples):
        o_ref[r, :] = x_ref[r, :] + _sigmoid(gate) * ple


def _ple(x2, p2, wp, gpost, gin, wg, tm):
    T, D = x2.shape
    Pd = p2.shape[1]
    full = lambda shape: pl.BlockSpec(shape, lambda i: (0, 0))
    return pl.pallas_call(
        functools.partial(_ple_kernel, tm=tm),
        grid=(T // tm,),
        in_specs=[
            pl.BlockSpec((tm, D), lambda i: (i, 0)),
            pl.BlockSpec((tm, Pd), lambda i: (i, 0)),
            full((Pd, D)), full((1, D)), full((1, D)),
            pl.BlockSpec((D, D), lambda i: (0, 0), pipeline_mode=pl.Buffered(1)),
        ],
        out_specs=pl.BlockSpec((tm, D), lambda i: (i, 0)),
        out_shape=jax.ShapeDtypeStruct((T, D), F32),
        compiler_params=_cparams(("parallel",)),
        name="ple",
    )(x2, p2, wp, gpost, gin, wg)


def _pad_cols(w, n):
    return jnp.pad(w, ((0, 0), (0, n - w.shape[1])))


def _pad_rows(w, n):
    return jnp.pad(w, ((0, n - w.shape[0]), (0, 0)))


def _layout_w_in(w):
    W = RWKV_WIDTH
    c = 3 * W
    segs = [w[:, 0:c],
            _pad_cols(w[:, c:c + DECAY_LORA], LORA_PAD),
            _pad_cols(w[:, c + DECAY_LORA:c + DECAY_LORA + AAA_LORA], LORA_PAD),
            _pad_cols(w[:, c + DECAY_LORA + AAA_LORA:c + DECAY_LORA + AAA_LORA + GATE_LORA], GATE_LORA_PAD)]
    c += DECAY_LORA + AAA_LORA + GATE_LORA
    mla_cols = Q_LORA + KV_LORA + QK_ROPE
    segs.append(_pad_cols(w[:, c:c + mla_cols], MLA_PAD))
    c += mla_cols
    segs.append(w[:, c:])
    out = jnp.concatenate(segs, axis=1).astype(BF16)
    assert out.shape[1] == IN_PAD
    return out


def _layout_mu(mu):
    W = RWKV_WIDTH
    c = 3 * W
    segs = [mu[0:c],
            jnp.pad(mu[c:c + DECAY_LORA], (0, LORA_PAD - DECAY_LORA)),
            jnp.pad(mu[c + DECAY_LORA:c + DECAY_LORA + AAA_LORA], (0, LORA_PAD - AAA_LORA)),
            jnp.pad(mu[c + DECAY_LORA + AAA_LORA:], (0, GATE_LORA_PAD - GATE_LORA))]
    return jnp.concatenate(segs).reshape(1, RWKV_PAD)


def _moe_tables(eidx, rnk, counts, tmf, tm_slots):
    K, T = eidx.shape
    E = N_EXPERTS
    A = K * T
    n_tiles = A // tmf + E
    P = n_tiles * tmf
    pcounts = (counts + tmf - 1) // tmf * tmf
    pends = jnp.cumsum(pcounts)
    pstart = (pends - pcounts).astype(I32)
    tile_start = jnp.arange(n_tiles, dtype=I32) * tmf
    tile_e = jnp.minimum(jnp.sum((pends[None, :] <= tile_start[:, None]).astype(I32), axis=1), E - 1)
    n_used = (pends[E - 1:] // tmf).astype(I32)
    dest = _slots(eidx, rnk, pstart, tm_slots)
    j = jnp.arange(tmf, dtype=I32)[None, :]
    tail = j < (pcounts - counts)[:, None]
    spare = jnp.cumsum(jnp.logical_not(tail).reshape(-1).astype(I32)) - 1
    pad_row = jnp.where(tail.reshape(-1), ((pstart + counts)[:, None] + j).reshape(-1),
                        pends[E - 1] + spare)
    q = jnp.arange(P - A, dtype=I32)
    slot_id = jnp.arange(A, dtype=I32)
    rows = jnp.concatenate([dest.reshape(-1), pad_row.astype(I32)])
    slots = jnp.concatenate([slot_id, A + q])
    row_dst = lax.sort((rows, slots), num_keys=1)[1]
    row_tok = jnp.where(row_dst < A, row_dst % T, 0)
    return (row_tok.reshape(n_tiles, 1, tmf), row_dst.reshape(n_tiles, 1, tmf), tile_e, n_used)


def _layer(x, p, positions, g_mix, w_in, mu_rwkv, w0, w2, a0, a2, g2, k_k, k_a, r_k,
           gn_w, gn_b, w_a_up, g_qa, g_kva, w_uq, w_ukv, g_qn, g_kn, w_b_up, w_o,
           g_ffn, w_router, router_bias, w_exp_gate, w_exp_up, w_exp_down,
           w_sh_gate, w_sh_up, w_sh_down, w_ple, g_ple_post, g_ple_in, w_ple_gate,
           *, tiles):
    B, S, D = x.shape
    T = B * S
    W = RWKV_WIDTH
    x2 = x.reshape(T, D)
    row = lambda v: v.reshape(1, -1).astype(F32)

    proj = _inproj(x2, row(g_mix), _layout_w_in(w_in), tiles["tm_in"], tiles["tn_in"])

    head_blk = (jnp.arange(W)[:, None] // RWKV_HEAD == jnp.arange(W)[None, :] // RWKV_HEAD).astype(BF16)
    r, k, v, kk, a, lw, g = _rwkv_prep(
        proj, _layout_mu(mu_rwkv), row(w0), _pad_rows(w2, LORA_PAD).astype(BF16), row(a0),
        _pad_rows(a2, LORA_PAD).astype(BF16), _pad_rows(g2, GATE_LORA_PAD).astype(BF16),
        row(k_k), row(k_a), head_blk, tiles["tm_prep"], S)
    b3 = lambda t: t.reshape(B, S, W)
    o_a = _rwkv_scan(b3(r), b3(k), b3(v), b3(kk), b3(a), b3(lw), b3(g),
                     row(gn_w), row(gn_b), row(r_k), tiles["rows_scan"]).reshape(T, W)

    half = QK_ROPE // 2
    inv_freq = ROPE_THETA ** (-jnp.arange(half, dtype=F32) / half)
    invf = jnp.concatenate([inv_freq, inv_freq, jnp.zeros((LANES - QK_ROPE,), F32)]).reshape(1, LANES)
    wuq = jnp.pad(w_uq.reshape(Q_LORA, MLA_HEADS, QK_HEAD),
                  ((0, 0), (0, 0), (0, QK_PAD - QK_HEAD))).reshape(Q_LORA, MLA_HEADS * QK_PAD)
    pad_g = lambda gv: jnp.pad(gv, (0, QK_PAD - QK_HEAD)).reshape(1, QK_PAD)
    q, kx, vx = _mla_prep(proj, positions.reshape(T, 1).astype(I32), invf, row(g_qa), row(g_kva),
                          wuq.astype(BF16), w_ukv.astype(BF16), pad_g(g_qn), pad_g(g_kn),
                          tiles["tm_mla"])
    o_b = _flash(q.reshape(B, S, -1), kx.reshape(B, S, -1), vx, tiles["tq"]).reshape(T, MLA_WIDTH)

    merged = _merge(o_a, o_b, w_a_up.astype(BF16), w_b_up.astype(BF16), proj,
                    tiles["tm_merge"], tiles["tn_merge"])

    x1, h2, h2p, logits_t = _wo(merged, x2, w_o.astype(BF16), row(g_ffn),
                                jnp.transpose(w_router).astype(F32), tiles["tm_wo"])
    eidx, wts, rnk, cnt = _route(logits_t, router_bias.reshape(N_EXPERTS, 1).astype(F32),
                                 tiles["tm_route"])
    tmf = tiles["tm_ffn"]
    counts = cnt[:, 0].astype(I32)
    row_tok, row_dst, tile_e, n_used = _moe_tables(eidx, rnk, counts, tmf, tiles["tm_slots"])
    ys = _ffn(tile_e, n_used, row_tok, row_dst, h2p, w_exp_gate, w_exp_up, w_exp_down, tmf)
    x3 = _combine(ys, jnp.transpose(wts), x1, h2,
                  w_sh_gate.astype(BF16), w_sh_up.astype(BF16), w_sh_down.astype(BF16),
                  tiles["tm_comb"])

    out = _ple(x3, p.reshape(T, PLE_DIM), w_ple.astype(BF16), row(g_ple_post), row(g_ple_in),
               w_ple_gate.astype(BF16), tiles["tm_ple"])
    return out.reshape(B, S, D)


TILES = dict(tm_in=1024, tn_in=1536, tm_prep=512, rows_scan=128, tm_mla=1024, tq=512,
             tm_merge=1024, tn_merge=1024, tm_wo=512, tm_route=1024, tm_ffn=256,
             tm_slots=2048, tm_comb=256, tm_ple=1024)


def kernel(x, p, positions, g_mix, w_in, mu_rwkv, w0, w2, a0, a2, g2, k_k, k_a, r_k, gn_w, gn_b, w_a_up, g_qa, g_kva, w_uq, w_ukv, g_qn, g_kn, w_b_up, w_o, g_ffn, w_router, router_bias, w_exp_gate, w_exp_up, w_exp_down, w_sh_gate, w_sh_up, w_sh_down, w_ple, g_ple_post, g_ple_in, w_ple_gate):
    args = (g_mix, w_in, mu_rwkv, w0, w2, a0, a2, g2, k_k, k_a, r_k, gn_w, gn_b, w_a_up,
            g_qa, g_kva, w_uq, w_ukv, g_qn, g_kn, w_b_up, w_o, g_ffn, w_router, router_bias,
            w_exp_gate, w_exp_up, w_exp_down, w_sh_gate, w_sh_up, w_sh_down, w_ple,
            g_ple_post, g_ple_in, w_ple_gate)
    assert all(t.shape[0] == 1 for t in args), "single-layer stack expected"
    return _layer(x, p[0], positions, *[t[0] for t in args], tiles=TILES)
```

```python
import functools

import jax
import jax.numpy as jnp
from jax import lax
from jax.experimental import pallas as pl
from jax.experimental.pallas import tpu as pltpu

F32 = jnp.float32
BF16 = jnp.bfloat16
I32 = jnp.int32
U32 = jnp.uint32

D_MODEL = 2048
PLE_DIM = 256
NORM_EPS = 1e-6
RWKV_HEADS = 16
RWKV_HEAD = 64
RWKV_WIDTH = RWKV_HEADS * RWKV_HEAD
DECAY_LORA = 64
AAA_LORA = 64
GATE_LORA = 160
GN_EPS = 64e-5
MLA_HEADS = 8
Q_LORA = 512
KV_LORA = 512
QK_NOPE = 128
QK_ROPE = 64
QK_HEAD = QK_NOPE + QK_ROPE
V_HEAD = 128
MLA_WIDTH = MLA_HEADS * V_HEAD
ROPE_THETA = 10000.0
N_EXPERTS = 64
TOP_K = 8
N_GROUPS = 8
GROUP_SIZE = N_EXPERTS // N_GROUPS
TOPK_GROUPS = 4
MOE_INTER = 512
ROUTED_SCALE = 2.5

LANES = 128
QK_PAD = 2 * LANES
PACK_ROWS = 8
PACK_W = D_MODEL // 2 // PACK_ROWS
assert PACK_W == LANES

LORA_PAD = 128
GATE_LORA_PAD = 256
RWKV_PAD = 3 * RWKV_WIDTH + 2 * LORA_PAD + GATE_LORA_PAD
MLA_PAD = 1536
COL_RWKV = 0
COL_MLA = RWKV_PAD
COL_GA = COL_MLA + MLA_PAD
COL_GB = COL_GA + D_MODEL
IN_PAD = COL_GB + D_MODEL

VMEM_LIMIT = 56 * 1024 * 1024
CHUNK = 64
WO_ROWS = 128
SCAN_LANE_BLOCKS = 8
MLA_ROWS = 256


def _cparams(sem):
    return pltpu.CompilerParams(dimension_semantics=sem, vmem_limit_bytes=VMEM_LIMIT)


def _dot(a, b):
    return jnp.dot(a, b, preferred_element_type=F32)


def _dot_nt(a, b):
    return lax.dot_general(a, b, (((1,), (1,)), ((), ())), preferred_element_type=F32)


def _dot_tn(a, b):
    return lax.dot_general(a, b, (((0,), (0,)), ((), ())), preferred_element_type=F32)


def _split_bf16(x):
    hi = x.astype(BF16)
    lo = (x - hi.astype(F32)).astype(BF16)
    return hi, lo


def _rms(x, g):
    ms = jnp.mean(x * x, axis=-1, keepdims=True)
    return x * lax.rsqrt(ms + NORM_EPS) * g


def _sigmoid(x):
    return 1.0 / (1.0 + jnp.exp(-x))


def _inproj_kernel(x_ref, g_ref, w_ref, o_ref, h_scr):
    @pl.when(pl.program_id(1) == 0)
    def _():
        h_scr[...] = _rms(x_ref[...], g_ref[...]).astype(BF16)

    o_ref[...] = _dot(h_scr[...], w_ref[...]).astype(o_ref.dtype)


def _inproj(x2, g, w, tm, tn):
    T, D = x2.shape
    N = w.shape[1]
    return pl.pallas_call(
        _inproj_kernel,
        grid=(T // tm, N // tn),
        in_specs=[
            pl.BlockSpec((tm, D), lambda i, j: (i, 0)),
            pl.BlockSpec((1, D), lambda i, j: (0, 0)),
            pl.BlockSpec((D, tn), lambda i, j: (0, j)),
        ],
        out_specs=pl.BlockSpec((tm, tn), lambda i, j: (i, j)),
        out_shape=jax.ShapeDtypeStruct((T, N), BF16),
        scratch_shapes=[pltpu.VMEM((tm, D), BF16)],
        compiler_params=_cparams(("parallel", "arbitrary")),
        name="inproj",
    )(x2, g, w)


HALO = 16


def _rwkv_prep_kernel(cur_ref, halo_ref, mu_ref, w0_ref, w2_ref, a0_ref, a2_ref, g2_ref,
                      kk_ref, ka_ref, bd_ref,
                      r_out, k_out, v_out, kk_out, a_out, lw_out, g_out, *, tm, seq):
    i = pl.program_id(0)
    first = (i * tm) % seq == 0
    W = RWKV_WIDTH

    def shifted(c0, c1):
        cur = cur_ref[:, c0:c1].astype(F32)
        last_prev = jnp.where(first, 0.0, halo_ref[HALO - 1:HALO, c0:c1].astype(F32))
        row = lax.broadcasted_iota(I32, cur.shape, 0)
        prev = jnp.where(row == 0, last_prev, pltpu.roll(cur, 1, 0))
        return cur + (prev - cur) * mu_ref[:, c0:c1]

    r = shifted(0, W)
    r_out[...] = r.astype(r_out.dtype)
    v_out[...] = shifted(2 * W, 3 * W).astype(v_out.dtype)
    lora = shifted(3 * W, RWKV_PAD)
    dw = lora[:, 0:LORA_PAD]
    da = lora[:, LORA_PAD:2 * LORA_PAD]
    dg = lora[:, 2 * LORA_PAD:]
    k = shifted(W, 2 * W)
    z = w0_ref[...] + _dot(jnp.tanh(dw).astype(BF16), w2_ref[...])
    w_log = -(jnp.maximum(-z, 0.0) + jnp.log(1.0 + jnp.exp(-jnp.abs(z)))) - 0.5
    lw_out[...] = -jnp.exp(w_log)
    a = _sigmoid(a0_ref[...] + _dot(da.astype(BF16), a2_ref[...]))
    g_out[...] = _dot(_sigmoid(dg).astype(BF16), g2_ref[...]).astype(g_out.dtype)
    kk = k * kk_ref[...]
    hi, lo = _split_bf16(kk * kk)
    ssq = _dot(hi, bd_ref[...]) + _dot(lo, bd_ref[...])
    kk = kk / jnp.maximum(jnp.sqrt(ssq), 1e-12)
    k_out[...] = (k * (1.0 + (a - 1.0) * ka_ref[...])).astype(k_out.dtype)
    kk_out[...] = kk.astype(kk_out.dtype)
    a_out[...] = a.astype(a_out.dtype)


def _rwkv_prep(proj, mu, w0, w2, a0, a2, g2, k_k, k_a, bd, tm, seq):
    T = proj.shape[0]
    W = RWKV_WIDTH
    full = lambda shape: pl.BlockSpec(shape, lambda i: (0, 0))
    out = lambda: pl.BlockSpec((tm, W), lambda i: (i, 0))
    return pl.pallas_call(
        functools.partial(_rwkv_prep_kernel, tm=tm, seq=seq),
        grid=(T // tm,),
        in_specs=[
            pl.BlockSpec((tm, RWKV_PAD), lambda i: (i, 0)),
            pl.BlockSpec((HALO, RWKV_PAD), lambda i: (jnp.maximum(i * (tm // HALO) - 1, 0), 0)),
            full((1, RWKV_PAD)), full((1, W)), full((LORA_PAD, W)), full((1, W)),
            full((LORA_PAD, W)), full((GATE_LORA_PAD, W)), full((1, W)), full((1, W)),
            full((W, W)),
        ],
        out_specs=[out() for _ in range(7)],
        out_shape=[jax.ShapeDtypeStruct((T, W), dt)
                   for dt in (BF16, BF16, BF16, BF16, BF16, F32, BF16)],
        compiler_params=_cparams(("parallel",)),
        name="rwkv_prep",
    )(proj, proj, mu, w0, w2, a0, a2, g2, k_k, k_a, bd)


def _rwkv_scan_kernel(r_ref, k_ref, v_ref, kk_ref, a_ref, lw_ref, g_ref,
                      gnw_ref, gnb_ref, rk_ref, o_ref, s_scr, *, rows):
    L = CHUNK
    L2 = 2 * L

    @pl.when(pl.program_id(2) == 0)
    def _():
        s_scr[...] = jnp.zeros_like(s_scr)

    lane = lax.broadcasted_iota(I32, (L, LANES), 1)
    m_a = (lane < RWKV_HEAD).astype(F32)
    m_b = 1.0 - m_a
    ri = lax.broadcasted_iota(I32, (L2, L2), 0)
    ci = lax.broadcasted_iota(I32, (L2, L2), 1)
    strict = ri > ci
    incl = ri >= ci
    eye = (ri == ci).astype(F32)
    tri = (lax.broadcasted_iota(I32, (L, L), 0) >= lax.broadcasted_iota(I32, (L, L), 1)).astype(BF16)
    hr = lax.broadcasted_iota(I32, (LANES, LANES), 0) // RWKV_HEAD
    hc = lax.broadcasted_iota(I32, (LANES, LANES), 1) // RWKV_HEAD
    head_ones = (hr == hc).astype(BF16)
    nb = SCAN_LANE_BLOCKS

    def stack(x):
        return jnp.concatenate([x * m_a, x * m_b], axis=0)

    chunks = range(rows // L)
    rows_of = lambda c: slice(c * L, (c + 1) * L)
    lanes_of = lambda h: slice(h * LANES, (h + 1) * LANES)
    each = lambda f, *lists: [f(*vals) for vals in zip(*lists)]

    def operands(c, h):
        sl, ln = rows_of(c), lanes_of(h)
        r = r_ref[0, sl, ln].astype(F32)
        k = k_ref[0, sl, ln].astype(F32)
        v = v_ref[0, sl, ln].astype(F32)
        kk = kk_ref[0, sl, ln].astype(F32)
        a = a_ref[0, sl, ln].astype(F32)
        lw = lw_ref[0, sl, ln]
        hi, lo = _split_bf16(lw)
        cum = _dot(tri, hi) + _dot(tri, lo)
        c_end = cum[L - 1:L, :]
        p_inv = jnp.exp(-cum)
        p_end = jnp.exp(c_end - cum)
        b = kk * a
        return dict(
            x_a=stack(-(kk * jnp.exp(cum - lw))).astype(BF16),
            x_r=stack(r * jnp.exp(cum)),
            x_bk=jnp.concatenate([stack(b * p_inv), stack(k * p_inv)], axis=0).astype(BF16),
            v_st=stack(v).astype(BF16),
            z_hat=jnp.concatenate([stack(b * p_end), stack(k * p_end)], axis=0).astype(BF16),
            p_row=jnp.exp(c_end),
            rkr=(r * k * rk_ref[:, ln]).astype(BF16), v=v)

    ops = [operands(c, h) for c in chunks for h in range(nb)]
    big = [_dot_nt(jnp.concatenate([o["x_a"], o["x_r"].astype(BF16)], axis=0), o["x_bk"])
           for o in ops]
    n_pow = [jnp.where(strict, m[0:L2, 0:L2], 0.0) for m in big]
    a_ak = [jnp.where(strict, m[0:L2, L2:2 * L2], 0.0).astype(BF16) for m in big]
    a_r = [jnp.concatenate([jnp.where(incl, m[L2:2 * L2, 0:L2], 0.0),
                            jnp.where(incl, m[L2:2 * L2, L2:2 * L2], 0.0)], axis=1).astype(BF16)
           for m in big]

    t_inv = [eye + n for n in n_pow]
    for _ in range(5):
        n_pow = each(lambda n: _dot(n.astype(BF16), n.astype(BF16)), n_pow)
        t_inv = each(lambda t, n: t + _dot(t.astype(BF16), n.astype(BF16)), t_inv, n_pow)

    akv = each(lambda m, o: _dot(m, o["v_st"]).astype(BF16), a_ak, ops)
    w = each(lambda t, o, x: _dot(t.astype(BF16), jnp.concatenate([o["x_a"], x], axis=1)),
             t_inv, ops, akv)
    w1 = [m[:, 0:LANES].astype(BF16) for m in w]
    w2v = each(lambda m, o: jnp.concatenate([m[:, LANES:].astype(BF16), o["v_st"]], axis=0), w, ops)
    g_mat = each(lambda o, m, x: (o["x_r"] + _dot(m[:, 0:L2], x)).astype(BF16), ops, a_r, w1)
    y0 = each(_dot, a_r, w2v)
    m_mat = each(lambda x, o: _dot_tn(x, o["z_hat"][0:L2, :]).astype(BF16), w1, ops)
    c2 = each(lambda x, o: _dot_tn(x, o["z_hat"]), w2v, ops)
    bonus_v = [_dot(o["rkr"], head_ones) * o["v"] for o in ops]

    s = [s_scr[h] for h in range(nb)]
    y = []
    for i in range(len(ops)):
        h = i % nb
        s_b = s[h].astype(BF16)
        y_st = _dot_nt(g_mat[i], s_b) + y0[i]
        s[h] = s[h] * ops[i]["p_row"] + _dot(s_b, m_mat[i]) + c2[i]
        y.append(y_st[0:L, :] + y_st[L:L2, :])
    for h in range(nb):
        s_scr[h] = s[h]

    mean = [_dot(t.astype(BF16), head_ones) * (1.0 / RWKV_HEAD) for t in y]
    yc = each(lambda t, m: t - m, y, mean)
    var = [_dot((t * t).astype(BF16), head_ones) * (1.0 / RWKV_HEAD) for t in yc]
    for i in range(len(ops)):
        sl, ln = rows_of(i // nb), lanes_of(i % nb)
        yn = yc[i] * lax.rsqrt(var[i] + GN_EPS) * gnw_ref[:, ln] + gnb_ref[:, ln]
        o_ref[0, sl, ln] = ((yn + bonus_v[i]) * g_ref[0, sl, ln].astype(F32)).astype(o_ref.dtype)


def _rwkv_scan(r, k, v, kk, a, lw, g, gn_w, gn_b, r_k, rows):
    B, S, W = r.shape
    wb = SCAN_LANE_BLOCKS * LANES
    npair = W // wb
    seq = lambda: pl.BlockSpec((1, rows, wb), lambda b, p, c: (b, c, p))
    par = lambda: pl.BlockSpec((1, wb), lambda b, p, c: (0, p))
    return pl.pallas_call(
        functools.partial(_rwkv_scan_kernel, rows=rows),
        grid=(B, npair, S // rows),
        in_specs=[seq() for _ in range(7)] + [par(), par(), par()],
        out_specs=seq(),
        out_shape=jax.ShapeDtypeStruct((B, S, W), BF16),
        scratch_shapes=[pltpu.VMEM((SCAN_LANE_BLOCKS, LANES, LANES), F32)],
        compiler_params=_cparams(("parallel", "parallel", "arbitrary")),
        name="rwkv_scan",
    )(r, k, v, kk, a, lw, g, gn_w, gn_b, r_k)


def _rope128(x, cos, sin_signed):
    lane = lax.broadcasted_iota(I32, x.shape, 1)
    half = QK_ROPE // 2
    partner = jnp.where(lane < half, pltpu.roll(x, LANES - half, 1), pltpu.roll(x, half, 1))
    return x * cos + partner * sin_signed


def _mla_prep_kernel(cq_ref, ckv_ref, kr_ref, pos_ref, invf_ref, gqa_ref, gkva_ref,
                     wuq_ref, wukv_ref, gqn_ref, gkn_ref,
                     q_out, k_out, v_out, cqn_scr, ckvn_scr, cos_scr, sin_scr):
    @pl.when(pl.program_id(1) == 0)
    def _():
        cqn_scr[...] = _rms(cq_ref[...].astype(F32), gqa_ref[...]).astype(BF16)
        ckvn_scr[...] = _rms(ckv_ref[...].astype(F32), gkva_ref[...]).astype(BF16)
        ang = pos_ref[...].astype(F32) * invf_ref[...]
        lane = lax.broadcasted_iota(I32, ang.shape, 1)
        cos_scr[...] = jnp.cos(ang)
        sin_scr[...] = jnp.where(lane < QK_ROPE // 2, -1.0, 1.0) * jnp.sin(ang)

    tm = q_out.shape[0]
    hw = min(MLA_ROWS, tm)
    groups = [slice(g * hw, (g + 1) * hw) for g in range(tm // hw)]
    qs = [_dot(cqn_scr[r, :], wuq_ref[...]) for r in groups]
    kvs = [_dot(ckvn_scr[r, :], wukv_ref[...]) for r in groups]
    inv_n = 1.0 / QK_HEAD
    gkn = gkn_ref[...]
    for r, q, kv in zip(groups, qs, kvs):
        cos = cos_scr[r, :]
        sin = sin_scr[r, :]
        rs = lax.rsqrt(jnp.sum(q * q, axis=-1, keepdims=True) * inv_n + NORM_EPS)
        q = q * rs * gqn_ref[...] * (QK_HEAD ** -0.5 * LOG2E)
        q_out[r, :] = jnp.concatenate(
            [q[:, 0:LANES], _rope128(q[:, LANES:], cos, sin)], axis=1).astype(q_out.dtype)
        k_nope = kv[:, 0:QK_NOPE]
        kr = kr_ref[r, :].astype(F32)
        ssq = (jnp.sum(k_nope * k_nope, axis=-1, keepdims=True)
               + jnp.sum(kr * kr, axis=-1, keepdims=True))
        rs = lax.rsqrt(ssq * inv_n + NORM_EPS)
        k_out[r, :] = jnp.concatenate(
            [k_nope * rs * gkn[:, 0:LANES], _rope128(kr * rs * gkn[:, LANES:], cos, sin)],
            axis=1).astype(k_out.dtype)
        v_out[:, r] = jnp.transpose(kv[:, QK_NOPE:].astype(v_out.dtype))


def _mla_prep(proj, pos, invf, g_qa, g_kva, wuq, wukv, gqn, gkn, tm):
    T = proj.shape[0]
    H = MLA_HEADS
    c0 = COL_MLA // Q_LORA
    ckr = (COL_MLA + Q_LORA + KV_LORA) // LANES
    full = lambda shape: pl.BlockSpec(shape, lambda i, h: (0, 0))
    return pl.pallas_call(
        _mla_prep_kernel,
        grid=(T // tm, H),
        in_specs=[
            pl.BlockSpec((tm, Q_LORA), lambda i, h: (i, c0)),
            pl.BlockSpec((tm, KV_LORA), lambda i, h: (i, c0 + 1)),
            pl.BlockSpec((tm, LANES), lambda i, h: (i, ckr)),
            pl.BlockSpec((tm, 1), lambda i, h: (i, 0)),
            full((1, LANES)), full((1, Q_LORA)), full((1, KV_LORA)),
            pl.BlockSpec((Q_LORA, QK_PAD), lambda i, h: (0, h)),
            pl.BlockSpec((KV_LORA, QK_NOPE + V_HEAD), lambda i, h: (0, h)),
            full((1, QK_PAD)), full((1, QK_PAD)),
        ],
        out_specs=[
            pl.BlockSpec((tm, QK_PAD), lambda i, h: (i, h)),
            pl.BlockSpec((tm, QK_PAD), lambda i, h: (i, h)),
            pl.BlockSpec((V_HEAD, tm), lambda i, h: (h, i)),
        ],
        out_shape=[jax.ShapeDtypeStruct((T, H * QK_PAD), BF16),
                   jax.ShapeDtypeStruct((T, H * QK_PAD), BF16),
                   jax.ShapeDtypeStruct((H * V_HEAD, T), BF16)],
        scratch_shapes=[pltpu.VMEM((tm, Q_LORA), BF16), pltpu.VMEM((tm, KV_LORA), BF16),
                        pltpu.VMEM((tm, LANES), F32), pltpu.VMEM((tm, LANES), F32)],
        compiler_params=_cparams(("parallel", "arbitrary")),
        name="mla_prep",
    )(proj, proj, proj, pos, invf, g_qa, g_kva, wuq, wukv, gqn, gkn)


MASK_VALUE = -1e30
LOG2E = 1.4426950408889634


def _flash_kernel(q_ref, k_ref, vt_ref, o_ref, s_scr, p_scr, alpha_scr, m_scr, l_scr, acc_scr,
                  *, tq):
    i = pl.program_id(2)
    q = q_ref[0]
    m_scr[...] = jnp.full_like(m_scr, MASK_VALUE)
    l_scr[...] = jnp.zeros_like(l_scr)
    acc_scr[...] = jnp.zeros_like(acc_scr)

    q_half = tq // 2

    def scores(j, slot, half):
        ks = pl.ds(pl.multiple_of(j * tq, tq), tq)
        cols = slice(half * q_half, (half + 1) * q_half)
        s_scr[slot, :, cols] = _dot_nt(k_ref[0, ks, :], q[cols, :])

    def softmax(slot, masked, strips):
        for c in strips:
            cols = slice(c * LANES, (c + 1) * LANES)
            s = s_scr[slot, :, cols]
            if masked:
                kpos = lax.broadcasted_iota(I32, s.shape, 0)
                qpos = lax.broadcasted_iota(I32, s.shape, 1) + c * LANES
                s = jnp.where(kpos <= qpos, s, MASK_VALUE)
            m_old = m_scr[:, cols]
            m_new = jnp.maximum(m_old, jnp.max(s, axis=0, keepdims=True))
            p = jnp.exp2(s - m_new)
            alpha = jnp.exp2(m_old - m_new)
            l_scr[:, cols] = alpha * l_scr[:, cols] + jnp.sum(p, axis=0, keepdims=True)
            m_scr[:, cols] = m_new
            p_scr[slot, :, cols] = p.astype(BF16)
            alpha_scr[slot, :, cols] = alpha

    def weighted_values(j, slot):
        vt = vt_ref[:, pl.ds(pl.multiple_of(j * tq, tq), tq)]
        acc_scr[...] = alpha_scr[slot] * acc_scr[...] + _dot(vt, p_scr[slot])

    scores(0, 0, 0)
    scores(0, 0, 1)
    p_scr[1] = jnp.zeros((tq, tq), BF16)
    alpha_scr[1] = jnp.ones((1, tq), F32)
    n_strips = tq // LANES
    first, second = range(0, n_strips // 2), range(n_strips // 2, n_strips)

    def step(j, slot):
        weighted_values(jnp.maximum(j - 1, 0), 1 - slot)
        scores(j + 1, 1 - slot, 0)
        softmax(slot, False, first)
        scores(j + 1, 1 - slot, 1)
        softmax(slot, False, second)

    def pair(jp, carry):
        step(2 * jp, 0)
        step(2 * jp + 1, 1)
        return carry

    def finish(slot):
        weighted_values(jnp.maximum(i - 1, 0), 1 - slot)
        softmax(slot, True, range(n_strips))
        weighted_values(i, slot)
        o_ref[0] = jnp.transpose(acc_scr[...] / l_scr[...]).astype(o_ref.dtype)

    lax.fori_loop(0, i // 2, pair, 0)

    @pl.when(i % 2 == 0)
    def _():
        finish(0)

    @pl.when(i % 2 == 1)
    def _():
        step(i - 1, 0)
        finish(1)


def _flash(q, k, vt, tq):
    B, S, _ = q.shape
    H = MLA_HEADS
    return pl.pallas_call(
        functools.partial(_flash_kernel, tq=tq),
        grid=(B, H, S // tq),
        in_specs=[
            pl.BlockSpec((1, tq, QK_PAD), lambda b, h, i: (b, i, h)),
            pl.BlockSpec((1, S, QK_PAD), lambda b, h, i: (b, 0, h)),
            pl.BlockSpec((V_HEAD, S), lambda b, h, i: (h, b)),
        ],
        out_specs=pl.BlockSpec((1, tq, V_HEAD), lambda b, h, i: (b, i, h)),
        out_shape=jax.ShapeDtypeStruct((B, S, H * V_HEAD), BF16),
        scratch_shapes=[pltpu.VMEM((2, tq, tq), F32), pltpu.VMEM((2, tq, tq), BF16),
                        pltpu.VMEM((2, 1, tq), F32),
                        pltpu.VMEM((1, tq), F32), pltpu.VMEM((1, tq), F32),
                        pltpu.VMEM((V_HEAD, tq), F32)],
        compiler_params=_cparams(("parallel", "parallel", "arbitrary")),
        name="flash_attn",
    )(q, k, vt)


def _merge_kernel(oa_ref, ob_ref, wa_ref, wb_ref, ga_ref, gb_ref, o_ref):
    ya = _dot(oa_ref[...], wa_ref[...])
    yb = _dot(ob_ref[...], wb_ref[...])
    ga = _sigmoid(ga_ref[...].astype(F32))
    gb = _sigmoid(gb_ref[...].astype(F32))
    o_ref[...] = (ga * ya + gb * yb).astype(o_ref.dtype)


def _merge(oa, ob, wa, wb, proj, tm, tn):
    T, K = oa.shape
    N = wa.shape[1]
    ca = COL_GA // tn
    cb = COL_GB // tn
    return pl.pallas_call(
        _merge_kernel,
        grid=(T // tm, N // tn),
        in_specs=[
            pl.BlockSpec((tm, K), lambda i, j: (i, 0)),
            pl.BlockSpec((tm, K), lambda i, j: (i, 0)),
            pl.BlockSpec((K, tn), lambda i, j: (0, j)),
            pl.BlockSpec((K, tn), lambda i, j: (0, j)),
            pl.BlockSpec((tm, tn), lambda i, j: (i, ca + j)),
            pl.BlockSpec((tm, tn), lambda i, j: (i, cb + j)),
        ],
        out_specs=pl.BlockSpec((tm, tn), lambda i, j: (i, j)),
        out_shape=jax.ShapeDtypeStruct((T, N), BF16),
        compiler_params=_cparams(("parallel", "parallel")),
        name="merge",
    )(oa, ob, wa, wb, proj, proj)


def _pack_rows(x, out_ref, n, row0=0):
    for c in range(PACK_ROWS):
        lo = x[:, c * 2 * LANES:c * 2 * LANES + LANES]
        hi = x[:, c * 2 * LANES + LANES:(c + 1) * 2 * LANES]
        lo_b = pltpu.bitcast(lo.astype(BF16).astype(F32), U32)
        hi_b = pltpu.bitcast(hi.astype(BF16).astype(F32), U32)
        out_ref[pl.ds(row0 * PACK_ROWS + c, n, stride=PACK_ROWS), :] = hi_b | (lo_b >> 16)


def _unpack_rows(ref, c, tm, lead=None):
    idx = (pl.ds(c, tm, stride=PACK_ROWS), slice(None))
    if lead is not None:
        idx = (lead,) + idx
    u = ref[idx]
    lo = pltpu.bitcast(u << 16, F32)
    hi = pltpu.bitcast(u & jnp.uint32(0xFFFF0000), F32)
    return lo, hi


def _wo_kernel(m_ref, x_ref, wo_ref, g_ref, wr_ref, x1_ref, h2_ref, h2p_ref, lg_ref, *, tm):
    hw = WO_ROWS
    groups = [slice(h * hw, (h + 1) * hw) for h in range(tm // hw)]
    wo = wo_ref[...]
    x1 = [x_ref[r, :] + _dot(m_ref[r, :], wo) for r in groups]
    h2 = [_rms(v, g_ref[...]) for v in x1]
    w_hi, w_lo = _split_bf16(wr_ref[...])
    for h, r in enumerate(groups):
        x1_ref[r, :] = x1[h]
        h2_ref[r, :] = h2[h].astype(BF16)
        _pack_rows(h2[h], h2p_ref, hw, row0=h * hw)
        h_hi, h_lo = _split_bf16(h2[h])
        lg_ref[:, r] = _dot_nt(w_hi, h_hi) + _dot_nt(w_hi, h_lo) + _dot_nt(w_lo, h_hi)


def _wo(merged, x2, wo, g, wr_t, tm):
    T, D = x2.shape
    E = wr_t.shape[0]
    return pl.pallas_call(
        functools.partial(_wo_kernel, tm=tm),
        grid=(T // tm,),
        in_specs=[
            pl.BlockSpec((tm, D), lambda i: (i, 0)),
            pl.BlockSpec((tm, D), lambda i: (i, 0)),
            pl.BlockSpec((D, D), lambda i: (0, 0), pipeline_mode=pl.Buffered(1)),
            pl.BlockSpec((1, D), lambda i: (0, 0)),
            pl.BlockSpec((E, D), lambda i: (0, 0)),
        ],
        out_specs=[
            pl.BlockSpec((tm, D), lambda i: (i, 0)),
            pl.BlockSpec((tm, D), lambda i: (i, 0)),
            pl.BlockSpec((tm * PACK_ROWS, PACK_W), lambda i: (i, 0)),
            pl.BlockSpec((E, tm), lambda i: (0, i)),
        ],
        out_shape=[jax.ShapeDtypeStruct((T, D), F32),
                   jax.ShapeDtypeStruct((T, D), BF16),
                   jax.ShapeDtypeStruct((T * PACK_ROWS, PACK_W), U32),
                   jax.ShapeDtypeStruct((E, T), F32)],
        compiler_params=_cparams(("parallel",)),
        name="wo_norm_router",
    )(merged, x2, wo, g, wr_t)


def _route_kernel(lg_ref, bias_ref, eidx_ref, wts_ref, rnk_ref, cnt_ref, carry_scr, *, tm):
    E = N_EXPERTS
    NEG = -jnp.inf

    @pl.when(pl.program_id(0) == 0)
    def _():
        carry_scr[...] = jnp.zeros_like(carry_scr)

    scores = _sigmoid(lg_ref[...])
    choice = scores + bias_ref[...]
    c3 = choice.reshape(N_GROUPS, GROUP_SIZE, tm)
    sub = lax.broadcasted_iota(I32, c3.shape, 1)
    m1 = jnp.max(c3, axis=1, keepdims=True)
    i1 = jnp.min(jnp.where(c3 == m1, sub, GROUP_SIZE), axis=1, keepdims=True)
    m2 = jnp.max(jnp.where(sub == i1, NEG, c3), axis=1, keepdims=True)
    gs = (m1 + m2).reshape(N_GROUPS, tm)
    gi = lax.broadcasted_iota(I32, gs.shape, 0)
    gsel = jnp.zeros(gs.shape, F32)
    for _ in range(TOPK_GROUPS):
        mx = jnp.max(gs, axis=0, keepdims=True)
        ix = jnp.min(jnp.where(gs == mx, gi, N_GROUPS), axis=0, keepdims=True)
        hit = gi == ix
        gsel = jnp.where(hit, 1.0, gsel)
        gs = jnp.where(hit, NEG, gs)
    emask = jnp.broadcast_to(gsel.reshape(N_GROUPS, 1, tm), (N_GROUPS, GROUP_SIZE, tm)).reshape(E, tm)
    x = jnp.where(emask > 0.5, choice, NEG)
    ei = lax.broadcasted_iota(I32, x.shape, 0)
    sel = jnp.zeros(x.shape, F32)
    idx_rows, w_rows = [], []
    for _ in range(TOP_K):
        mx = jnp.max(x, axis=0, keepdims=True)
        ix = jnp.min(jnp.where(x == mx, ei, E), axis=0, keepdims=True)
        hit = ei == ix
        w_rows.append(jnp.sum(jnp.where(hit, scores, 0.0), axis=0, keepdims=True))
        idx_rows.append(ix)
        sel = jnp.where(hit, 1.0, sel)
        x = jnp.where(hit, NEG, x)
    w = jnp.concatenate(w_rows, axis=0)
    w = w / (jnp.sum(w, axis=0, keepdims=True) + 1e-20) * ROUTED_SCALE
    eidx_ref[...] = jnp.concatenate(idx_rows, axis=0)
    wts_ref[...] = w

    upper = (lax.broadcasted_iota(I32, (tm, tm), 0) < lax.broadcasted_iota(I32, (tm, tm), 1)).astype(BF16)
    base = carry_scr[...][:, 0:1]
    excl = _dot(sel.astype(BF16), upper) + base
    rnk_ref[...] = jnp.concatenate(
        [jnp.sum(jnp.where(ei == ix, excl, 0.0), axis=0, keepdims=True) for ix in idx_rows],
        axis=0).astype(I32)
    carry_scr[...] = carry_scr[...] + jnp.sum(sel, axis=1, keepdims=True)
    cnt_ref[...] = carry_scr[...]


def _route(logits_t, bias, tm):
    E, T = logits_t.shape
    K = TOP_K
    tok = lambda: pl.BlockSpec((K, tm), lambda i: (0, i))
    return pl.pallas_call(
        functools.partial(_route_kernel, tm=tm),
        grid=(T // tm,),
        in_specs=[pl.BlockSpec((E, tm), lambda i: (0, i)),
                  pl.BlockSpec((E, 1), lambda i: (0, 0))],
        out_specs=[tok(), tok(), tok(), pl.BlockSpec((E, LANES), lambda i: (0, 0))],
        out_shape=[jax.ShapeDtypeStruct((K, T), I32), jax.ShapeDtypeStruct((K, T), F32),
                   jax.ShapeDtypeStruct((K, T), I32), jax.ShapeDtypeStruct((E, LANES), F32)],
        scratch_shapes=[pltpu.VMEM((E, LANES), F32)],
        compiler_params=_cparams(("arbitrary",)),
        name="route",
    )(logits_t, bias)


def _slots_kernel(pst_ref, eidx_ref, rnk_ref, dest_ref):
    eidx = eidx_ref[...]
    dest = rnk_ref[...]
    for e in range(N_EXPERTS):
        dest = dest + jnp.where(eidx == e, pst_ref[e], 0)
    dest_ref[...] = dest


def _slots(eidx, rnk, pstart, tm):
    K, T = eidx.shape
    tok = lambda: pl.BlockSpec((K, tm), lambda i: (0, i))
    return pl.pallas_call(
        _slots_kernel,
        grid=(T // tm,),
        in_specs=[pl.BlockSpec(memory_space=pltpu.SMEM), tok(), tok()],
        out_specs=tok(),
        out_shape=jax.ShapeDtypeStruct((K, T), I32),
        compiler_params=_cparams(("parallel",)),
        name="slots",
    )(pstart, eidx, rnk)


def _ffn_kernel(te_ref, nu_ref, tok_ref, dst_ref, h_hbm, wg_ref, wu_ref, wd_ref, ys_hbm,
                gbuf, obuf, wg_b, wu_b, wd_b, gsem, osem, *, tm):
    s = pl.program_id(0)
    nu = nu_ref[0]
    tile_rows = tm * PACK_ROWS

    def gather(slot):
        for r in range(tm):
            pltpu.make_async_copy(
                h_hbm.at[pl.ds(pl.multiple_of(tok_ref[0, 0, r] * PACK_ROWS, PACK_ROWS), PACK_ROWS), :],
                gbuf.at[slot, pl.ds(r * PACK_ROWS, PACK_ROWS), :], gsem.at[slot]).start()

    def scatter(slot):
        for r in range(tm):
            pltpu.make_async_copy(
                obuf.at[slot, pl.ds(r * PACK_ROWS, PACK_ROWS), :],
                ys_hbm.at[pl.ds(pl.multiple_of(dst_ref[0, 0, r] * PACK_ROWS, PACK_ROWS), PACK_ROWS), :],
                osem.at[slot]).start()

    def wait_gather(slot):
        pltpu.make_async_copy(h_hbm.at[pl.ds(0, tile_rows), :], gbuf.at[slot], gsem.at[slot]).wait()

    def wait_scatter(slot):
        pltpu.make_async_copy(obuf.at[slot], ys_hbm.at[pl.ds(0, tile_rows), :], osem.at[slot]).wait()

    def compute(slot):
        parts = []
        for c in range(PACK_ROWS):
            lo, hi = _unpack_rows(gbuf.at[slot], c, tm)
            parts += [lo.astype(BF16), hi.astype(BF16)]
        a = jnp.concatenate(parts, axis=1)
        hg = _dot(a, wg_b[...])
        hu = _dot(a, wu_b[...])
        hid = (hg * _sigmoid(hg) * hu).astype(BF16)
        _pack_rows(_dot(hid, wd_b[...]), obuf.at[slot], tm)

    cur = jnp.maximum(s - 1, 0)
    @pl.when((s >= 1) & (s <= nu) & ((s == 1) | (te_ref[cur] != te_ref[jnp.maximum(s - 2, 0)])))
    def _():
        wg_b[...] = wg_ref[0].astype(BF16)
        wu_b[...] = wu_ref[0].astype(BF16)
        wd_b[...] = wd_ref[0].astype(BF16)

    @pl.when(s == 0)
    def _():
        gather(0)

    @pl.when(s == 1)
    def _():
        wait_gather(0)
        gather(1)
        compute(0)

    for slot in range(2):
        parity = s % 2 == slot

        @pl.when(parity & (s >= 3) & (s <= nu + 1))
        def _():
            wait_scatter(1 - slot)

        @pl.when(parity & (s >= 2) & (s <= nu))
        def _():
            wait_gather(1 - slot)
            gather(slot)
            scatter(slot)
            compute(1 - slot)

        @pl.when(parity & (s >= 2) & (s == nu + 1))
        def _():
            wait_gather(1 - slot)
            scatter(slot)
            wait_scatter(slot)


def _ffn(tile_e, n_used, row_tok, row_dst, h2p, wg, wu, wd, tm):
    nt = row_tok.shape[0]
    D, I = wg.shape[1], wg.shape[2]
    cur = lambda s: jnp.maximum(s - 1, 0)
    wspec = lambda shape: pl.BlockSpec(shape, lambda s, te, nu: (te[cur(s)], 0, 0))
    grid_spec = pltpu.PrefetchScalarGridSpec(
        num_scalar_prefetch=2,
        grid=(nt + 1,),
        in_specs=[
            pl.BlockSpec((1, 1, tm), lambda s, te, nu: (jnp.minimum(s, nt - 1), 0, 0),
                         memory_space=pltpu.SMEM),
            pl.BlockSpec((1, 1, tm), lambda s, te, nu: (jnp.maximum(s - 2, 0), 0, 0),
                         memory_space=pltpu.SMEM),
            pl.BlockSpec(memory_space=pl.ANY),
            wspec((1, D, I)), wspec((1, D, I)), wspec((1, I, D)),
        ],
        out_specs=pl.BlockSpec(memory_space=pl.ANY),
        scratch_shapes=[pltpu.VMEM((2, tm * PACK_ROWS, PACK_W), U32),
                        pltpu.VMEM((2, tm * PACK_ROWS, PACK_W), U32),
                        pltpu.VMEM((D, I), BF16), pltpu.VMEM((D, I), BF16), pltpu.VMEM((I, D), BF16),
                        pltpu.SemaphoreType.DMA((2,)), pltpu.SemaphoreType.DMA((2,))],
    )
    return pl.pallas_call(
        functools.partial(_ffn_kernel, tm=tm),
        grid_spec=grid_spec,
        out_shape=jax.ShapeDtypeStruct((nt * tm * PACK_ROWS, PACK_W), U32),
        compiler_params=_cparams(("arbitrary",)),
        name="expert_ffn",
    )(tile_e, n_used, row_tok, row_dst, h2p, wg, wu, wd)


def _combine_kernel(*refs, tm):
    ys_refs = refs[:TOP_K]
    wt_ref, x1_ref, h2_ref, wsg_ref, wsu_ref, wsd_ref, o_ref = refs[TOP_K:]
    h2 = h2_ref[...]
    hg = _dot(h2, wsg_ref[...])
    hu = _dot(h2, wsu_ref[...])
    base = x1_ref[...] + _dot((hg * _sigmoid(hg) * hu).astype(BF16), wsd_ref[...])
    wt = wt_ref[...]
    for c in range(PACK_ROWS):
        acc_lo = base[:, c * 2 * LANES:c * 2 * LANES + LANES]
        acc_hi = base[:, c * 2 * LANES + LANES:(c + 1) * 2 * LANES]
        for k in range(TOP_K):
            lo, hi = _unpack_rows(ys_refs[k], c, tm)
            wk = wt[:, k:k + 1]
            acc_lo = acc_lo + wk * lo
            acc_hi = acc_hi + wk * hi
        o_ref[:, c * 2 * LANES:c * 2 * LANES + LANES] = acc_lo
        o_ref[:, c * 2 * LANES + LANES:(c + 1) * 2 * LANES] = acc_hi


def _combine(ys, wts_t, x1, h2, wsg, wsu, wsd, tm):
    T, D = x1.shape
    K = wts_t.shape[1]
    I = wsg.shape[1]
    full = lambda shape: pl.BlockSpec(shape, lambda i: (0, 0))
    return pl.pallas_call(
        functools.partial(_combine_kernel, tm=tm),
        grid=(T // tm,),
        in_specs=[pl.BlockSpec((tm * PACK_ROWS, PACK_W), lambda i, k=k: (k * (T // tm) + i, 0))
                  for k in range(K)] + [
                  pl.BlockSpec((tm, K), lambda i: (i, 0)),
                  pl.BlockSpec((tm, D), lambda i: (i, 0)),
                  pl.BlockSpec((tm, D), lambda i: (i, 0)),
                  full((D, I)), full((D, I)), full((I, D))],
        out_specs=pl.BlockSpec((tm, D), lambda i: (i, 0)),
        out_shape=jax.ShapeDtypeStruct((T, D), F32),
        compiler_params=_cparams(("parallel",)),
        name="combine",
    )(*([ys] * K), wts_t, x1, h2, wsg, wsu, wsd)


def _ple_kernel(x_ref, p_ref, wp_ref, gpost_ref, gin_ref, wg_ref, o_ref, *, tm):
    hw = WO_ROWS
    groups = [slice(h * hw, (h + 1) * hw) for h in range(tm // hw)]
    wg = wg_ref[...]
    gates, ples = [], []
    for r in groups:
        hn = _rms(x_ref[r, :], gin_ref[...]).astype(BF16)
        gates.append(_dot(hn, wg))
        ples.append(_rms(_dot(p_ref[r, :].astype(BF16), wp_ref[...]), gpost_ref[...]))
    for r, gate, ple in zip(groups, gates, ples):
        o_ref[r, :] = x_ref[r, :] + _sigmoid(gate) * ple


def _ple(x2, p2, wp, gpost, gin, wg, tm):
    T, D = x2.shape
    Pd = p2.shape[1]
    full = lambda shape: pl.BlockSpec(shape, lambda i: (0, 0))
    return pl.pallas_call(
        functools.partial(_ple_kernel, tm=tm),
        grid=(T // tm,),
        in_specs=[
            pl.BlockSpec((tm, D), lambda i: (i, 0)),
            pl.BlockSpec((tm, Pd), lambda i: (i, 0)),
            full((Pd, D)), full((1, D)), full((1, D)),
            pl.BlockSpec((D, D), lambda i: (0, 0), pipeline_mode=pl.Buffered(1)),
        ],
        out_specs=pl.BlockSpec((tm, D), lambda i: (i, 0)),
        out_shape=jax.ShapeDtypeStruct((T, D), F32),
        compiler_params=_cparams(("parallel",)),
        name="ple",
    )(x2, p2, wp, gpost, gin, wg)


def _pad_cols(w, n):
    return jnp.pad(w, ((0, 0), (0, n - w.shape[1])))


def _pad_rows(w, n):
    return jnp.pad(w, ((0, n - w.shape[0]), (0, 0)))


def _layout_w_in(w):
    W = RWKV_WIDTH
    c = 3 * W
    segs = [w[:, 0:c],
            _pad_cols(w[:, c:c + DECAY_LORA], LORA_PAD),
            _pad_cols(w[:, c + DECAY_LORA:c + DECAY_LORA + AAA_LORA], LORA_PAD),
            _pad_cols(w[:, c + DECAY_LORA + AAA_LORA:c + DECAY_LORA + AAA_LORA + GATE_LORA], GATE_LORA_PAD)]
    c += DECAY_LORA + AAA_LORA + GATE_LORA
    mla_cols = Q_LORA + KV_LORA + QK_ROPE
    segs.append(_pad_cols(w[:, c:c + mla_cols], MLA_PAD))
    c += mla_cols
    segs.append(w[:, c:])
    out = jnp.concatenate(segs, axis=1).astype(BF16)
    assert out.shape[1] == IN_PAD
    return out


def _layout_mu(mu):
    W = RWKV_WIDTH
    c = 3 * W
    segs = [mu[0:c],
            jnp.pad(mu[c:c + DECAY_LORA], (0, LORA_PAD - DECAY_LORA)),
            jnp.pad(mu[c + DECAY_LORA:c + DECAY_LORA + AAA_LORA], (0, LORA_PAD - AAA_LORA)),
            jnp.pad(mu[c + DECAY_LORA + AAA_LORA:], (0, GATE_LORA_PAD - GATE_LORA))]
    return jnp.concatenate(segs).reshape(1, RWKV_PAD)


def _moe_tables(eidx, rnk, counts, tmf, tm_slots):
    K, T = eidx.shape
    E = N_EXPERTS
    A = K * T
    n_tiles = A // tmf + E
    P = n_tiles * tmf
    pcounts = (counts + tmf - 1) // tmf * tmf
    pends = jnp.cumsum(pcounts)
    pstart = (pends - pcounts).astype(I32)
    tile_start = jnp.arange(n_tiles, dtype=I32) * tmf
    tile_e = jnp.minimum(jnp.sum((pends[None, :] <= tile_start[:, None]).astype(I32), axis=1), E - 1)
    n_used = (pends[E - 1:] // tmf).astype(I32)
    dest = _slots(eidx, rnk, pstart, tm_slots)
    j = jnp.arange(tmf, dtype=I32)[None, :]
    tail = j < (pcounts - counts)[:, None]
    spare = jnp.cumsum(jnp.logical_not(tail).reshape(-1).astype(I32)) - 1
    pad_row = jnp.where(tail.reshape(-1), ((pstart + counts)[:, None] + j).reshape(-1),
                        pends[E - 1] + spare)
    q = jnp.arange(P - A, dtype=I32)
    slot_id = jnp.arange(A, dtype=I32)
    rows = jnp.concatenate([dest.reshape(-1), pad_row.astype(I32)])
    slots = jnp.concatenate([slot_id, A + q])
    row_dst = lax.sort((rows, slots), num_keys=1)[1]
    row_tok = jnp.where(row_dst < A, row_dst % T, 0)
    return (row_tok.reshape(n_tiles, 1, tmf), row_dst.reshape(n_tiles, 1, tmf), tile_e, n_used)


def _layer(x, p, positions, g_mix, w_in, mu_rwkv, w0, w2, a0, a2, g2, k_k, k_a, r_k,
           gn_w, gn_b, w_a_up, g_qa, g_kva, w_uq, w_ukv, g_qn, g_kn, w_b_up, w_o,
           g_ffn, w_router, router_bias, w_exp_gate, w_exp_up, w_exp_down,
           w_sh_gate, w_sh_up, w_sh_down, w_ple, g_ple_post, g_ple_in, w_ple_gate,
           *, tiles):
    B, S, D = x.shape
    T = B * S
    W = RWKV_WIDTH
    x2 = x.reshape(T, D)
    row = lambda v: v.reshape(1, -1).astype(F32)

    proj = _inproj(x2, row(g_mix), _layout_w_in(w_in), tiles["tm_in"], tiles["tn_in"])

    head_blk = (jnp.arange(W)[:, None] // RWKV_HEAD == jnp.arange(W)[None, :] // RWKV_HEAD).astype(BF16)
    r, k, v, kk, a, lw, g = _rwkv_prep(
        proj, _layout_mu(mu_rwkv), row(w0), _pad_rows(w2, LORA_PAD).astype(BF16), row(a0),
        _pad_rows(a2, LORA_PAD).astype(BF16), _pad_rows(g2, GATE_LORA_PAD).astype(BF16),
        row(k_k), row(k_a), head_blk, tiles["tm_prep"], S)
    b3 = lambda t: t.reshape(B, S, W)
    o_a = _rwkv_scan(b3(r), b3(k), b3(v), b3(kk), b3(a), b3(lw), b3(g),
                     row(gn_w), row(gn_b), row(r_k), tiles["rows_scan"]).reshape(T, W)

    half = QK_ROPE // 2
    inv_freq = ROPE_THETA ** (-jnp.arange(half, dtype=F32) / half)
    invf = jnp.concatenate([inv_freq, inv_freq, jnp.zeros((LANES - QK_ROPE,), F32)]).reshape(1, LANES)
    wuq = jnp.pad(w_uq.reshape(Q_LORA, MLA_HEADS, QK_HEAD),
                  ((0, 0), (0, 0), (0, QK_PAD - QK_HEAD))).reshape(Q_LORA, MLA_HEADS * QK_PAD)
    pad_g = lambda gv: jnp.pad(gv, (0, QK_PAD - QK_HEAD)).reshape(1, QK_PAD)
    q, kx, vx = _mla_prep(proj, positions.reshape(T, 1).astype(I32), invf, row(g_qa), row(g_kva),
                          wuq.astype(BF16), w_ukv.astype(BF16), pad_g(g_qn), pad_g(g_kn),
                          tiles["tm_mla"])
    o_b = _flash(q.reshape(B, S, -1), kx.reshape(B, S, -1), vx, tiles["tq"]).reshape(T, MLA_WIDTH)

    merged = _merge(o_a, o_b, w_a_up.astype(BF16), w_b_up.astype(BF16), proj,
                    tiles["tm_merge"], tiles["tn_merge"])

    x1, h2, h2p, logits_t = _wo(merged, x2, w_o.astype(BF16), row(g_ffn),
                                jnp.transpose(w_router).astype(F32), tiles["tm_wo"])
    eidx, wts, rnk, cnt = _route(logits_t, router_bias.reshape(N_EXPERTS, 1).astype(F32),
                                 tiles["tm_route"])
    tmf = tiles["tm_ffn"]
    counts = cnt[:, 0].astype(I32)
    row_tok, row_dst, tile_e, n_used = _moe_tables(eidx, rnk, counts, tmf, tiles["tm_slots"])
    ys = _ffn(tile_e, n_used, row_tok, row_dst, h2p, w_exp_gate, w_exp_up, w_exp_down, tmf)
    x3 = _combine(ys, jnp.transpose(wts), x1, h2,
                  w_sh_gate.astype(BF16), w_sh_up.astype(BF16), w_sh_down.astype(BF16),
                  tiles["tm_comb"])

    out = _ple(x3, p.reshape(T, PLE_DIM), w_ple.astype(BF16), row(g_ple_post), row(g_ple_in),
               w_ple_gate.astype(BF16), tiles["tm_ple"])
    return out.reshape(B, S, D)


TILES = dict(tm_in=1024, tn_in=1536, tm_prep=512, rows_scan=128, tm_mla=2048, tq=512,
             tm_merge=1024, tn_merge=1024, tm_wo=512, tm_route=1024, tm_ffn=256,
             tm_slots=2048, tm_comb=256, tm_ple=1024)


def kernel(x, p, positions, g_mix, w_in, mu_rwkv, w0, w2, a0, a2, g2, k_k, k_a, r_k, gn_w, gn_b, w_a_up, g_qa, g_kva, w_uq, w_ukv, g_qn, g_kn, w_b_up, w_o, g_ffn, w_router, router_bias, w_exp_gate, w_exp_up, w_exp_down, w_sh_gate, w_sh_up, w_sh_down, w_ple, g_ple_post, g_ple_in, w_ple_gate):
    args = (g_mix, w_in, mu_rwkv, w0, w2, a0, a2, g2, k_k, k_a, r_k, gn_w, gn_b, w_a_up,
            g_qa, g_kva, w_uq, w_ukv, g_qn, g_kn, w_b_up, w_o, g_ffn, w_router, router_bias,
            w_exp_gate, w_exp_up, w_exp_down, w_sh_gate, w_sh_up, w_sh_down, w_ple,
            g_ple_post, g_ple_in, w_ple_gate)
    assert all(t.shape[0] == 1 for t in args), "single-layer stack expected"
    return _layer(x, p[0], positions, *[t[0] for t in args], tiles=TILES)
```

```python
import functools

import jax
import jax.numpy as jnp
from jax import lax
from jax.experimental import pallas as pl
from jax.experimental.pallas import tpu as pltpu

F32 = jnp.float32
BF16 = jnp.bfloat16
I32 = jnp.int32
U32 = jnp.uint32

D_MODEL = 2048
PLE_DIM = 256
NORM_EPS = 1e-6
RWKV_HEADS = 16
RWKV_HEAD = 64
RWKV_WIDTH = RWKV_HEADS * RWKV_HEAD
DECAY_LORA = 64
AAA_LORA = 64
GATE_LORA = 160
GN_EPS = 64e-5
MLA_HEADS = 8
Q_LORA = 512
KV_LORA = 512
QK_NOPE = 128
QK_ROPE = 64
QK_HEAD = QK_NOPE + QK_ROPE
V_HEAD = 128
MLA_WIDTH = MLA_HEADS * V_HEAD
ROPE_THETA = 10000.0
N_EXPERTS = 64
TOP_K = 8
N_GROUPS = 8
GROUP_SIZE = N_EXPERTS // N_GROUPS
TOPK_GROUPS = 4
MOE_INTER = 512
ROUTED_SCALE = 2.5

LANES = 128
QK_PAD = 2 * LANES
PACK_ROWS = 8
PACK_W = D_MODEL // 2 // PACK_ROWS
assert PACK_W == LANES

LORA_PAD = 128
GATE_LORA_PAD = 256
RWKV_PAD = 3 * RWKV_WIDTH + 2 * LORA_PAD + GATE_LORA_PAD
MLA_PAD = 1536
COL_RWKV = 0
COL_MLA = RWKV_PAD
COL_GA = COL_MLA + MLA_PAD
COL_GB = COL_GA + D_MODEL
IN_PAD = COL_GB + D_MODEL

VMEM_LIMIT = 56 * 1024 * 1024
CHUNK = 64
WO_ROWS = 128
SCAN_LANE_BLOCKS = 8
MLA_ROWS = 256


def _cparams(sem):
    return pltpu.CompilerParams(dimension_semantics=sem, vmem_limit_bytes=VMEM_LIMIT)


def _dot(a, b):
    return jnp.dot(a, b, preferred_element_type=F32)


def _dot_nt(a, b):
    return lax.dot_general(a, b, (((1,), (1,)), ((), ())), preferred_element_type=F32)


def _dot_tn(a, b):
    return lax.dot_general(a, b, (((0,), (0,)), ((), ())), preferred_element_type=F32)


def _split_bf16(x):
    hi = x.astype(BF16)
    lo = (x - hi.astype(F32)).astype(BF16)
    return hi, lo


def _rms(x, g):
    ms = jnp.mean(x * x, axis=-1, keepdims=True)
    return x * lax.rsqrt(ms + NORM_EPS) * g


def _sigmoid(x):
    return 1.0 / (1.0 + jnp.exp(-x))


def _inproj_kernel(x_ref, g_ref, w_ref, o_ref, h_scr):
    @pl.when(pl.program_id(1) == 0)
    def _():
        h_scr[...] = _rms(x_ref[...], g_ref[...]).astype(BF16)

    o_ref[...] = _dot(h_scr[...], w_ref[...]).astype(o_ref.dtype)


def _inproj(x2, g, w, tm, tn):
    T, D = x2.shape
    N = w.shape[1]
    return pl.pallas_call(
        _inproj_kernel,
        grid=(T // tm, N // tn),
        in_specs=[
            pl.BlockSpec((tm, D), lambda i, j: (i, 0)),
            pl.BlockSpec((1, D), lambda i, j: (0, 0)),
            pl.BlockSpec((D, tn), lambda i, j: (0, j)),
        ],
        out_specs=pl.BlockSpec((tm, tn), lambda i, j: (i, j)),
        out_shape=jax.ShapeDtypeStruct((T, N), BF16),
        scratch_shapes=[pltpu.VMEM((tm, D), BF16)],
        compiler_params=_cparams(("parallel", "arbitrary")),
        name="inproj",
    )(x2, g, w)


HALO = 16


def _rwkv_prep_kernel(cur_ref, halo_ref, mu_ref, w0_ref, w2_ref, a0_ref, a2_ref, g2_ref,
                      kk_ref, ka_ref, bd_ref,
                      r_out, k_out, v_out, kk_out, a_out, lw_out, g_out, *, tm, seq):
    i = pl.program_id(0)
    first = (i * tm) % seq == 0
    W = RWKV_WIDTH

    def shifted(c0, c1):
        cur = cur_ref[:, c0:c1].astype(F32)
        last_prev = jnp.where(first, 0.0, halo_ref[HALO - 1:HALO, c0:c1].astype(F32))
        row = lax.broadcasted_iota(I32, cur.shape, 0)
        prev = jnp.where(row == 0, last_prev, pltpu.roll(cur, 1, 0))
        return cur + (prev - cur) * mu_ref[:, c0:c1]

    r = shifted(0, W)
    r_out[...] = r.astype(r_out.dtype)
    v_out[...] = shifted(2 * W, 3 * W).astype(v_out.dtype)
    lora = shifted(3 * W, RWKV_PAD)
    dw = lora[:, 0:LORA_PAD]
    da = lora[:, LORA_PAD:2 * LORA_PAD]
    dg = lora[:, 2 * LORA_PAD:]
    k = shifted(W, 2 * W)
    z = w0_ref[...] + _dot(jnp.tanh(dw).astype(BF16), w2_ref[...])
    w_log = -(jnp.maximum(-z, 0.0) + jnp.log(1.0 + jnp.exp(-jnp.abs(z)))) - 0.5
    lw_out[...] = -jnp.exp(w_log)
    a = _sigmoid(a0_ref[...] + _dot(da.astype(BF16), a2_ref[...]))
    g_out[...] = _dot(_sigmoid(dg).astype(BF16), g2_ref[...]).astype(g_out.dtype)
    kk = k * kk_ref[...]
    hi, lo = _split_bf16(kk * kk)
    ssq = _dot(hi, bd_ref[...]) + _dot(lo, bd_ref[...])
    kk = kk / jnp.maximum(jnp.sqrt(ssq), 1e-12)
    k_out[...] = (k * (1.0 + (a - 1.0) * ka_ref[...])).astype(k_out.dtype)
    kk_out[...] = kk.astype(kk_out.dtype)
    a_out[...] = a.astype(a_out.dtype)


def _rwkv_prep(proj, mu, w0, w2, a0, a2, g2, k_k, k_a, bd, tm, seq):
    T = proj.shape[0]
    W = RWKV_WIDTH
    full = lambda shape: pl.BlockSpec(shape, lambda i: (0, 0))
    out = lambda: pl.BlockSpec((tm, W), lambda i: (i, 0))
    return pl.pallas_call(
        functools.partial(_rwkv_prep_kernel, tm=tm, seq=seq),
        grid=(T // tm,),
        in_specs=[
            pl.BlockSpec((tm, RWKV_PAD), lambda i: (i, 0)),
            pl.BlockSpec((HALO, RWKV_PAD), lambda i: (jnp.maximum(i * (tm // HALO) - 1, 0), 0)),
            full((1, RWKV_PAD)), full((1, W)), full((LORA_PAD, W)), full((1, W)),
            full((LORA_PAD, W)), full((GATE_LORA_PAD, W)), full((1, W)), full((1, W)),
            full((W, W)),
        ],
        out_specs=[out() for _ in range(7)],
        out_shape=[jax.ShapeDtypeStruct((T, W), dt)
                   for dt in (BF16, BF16, BF16, BF16, BF16, F32, BF16)],
        compiler_params=_cparams(("parallel",)),
        name="rwkv_prep",
    )(proj, proj, mu, w0, w2, a0, a2, g2, k_k, k_a, bd)


def _rwkv_scan_kernel(r_ref, k_ref, v_ref, kk_ref, a_ref, lw_ref, g_ref,
                      gnw_ref, gnb_ref, rk_ref, o_ref, s_scr, *, rows):
    L = CHUNK
    L2 = 2 * L

    @pl.when(pl.program_id(2) == 0)
    def _():
        s_scr[...] = jnp.zeros_like(s_scr)

    lane = lax.broadcasted_iota(I32, (L, LANES), 1)
    m_a = (lane < RWKV_HEAD).astype(F32)
    m_b = 1.0 - m_a
    ri = lax.broadcasted_iota(I32, (L2, L2), 0)
    ci = lax.broadcasted_iota(I32, (L2, L2), 1)
    strict = ri > ci
    incl = ri >= ci
    eye = (ri == ci).astype(F32)
    tri = (lax.broadcasted_iota(I32, (L, L), 0) >= lax.broadcasted_iota(I32, (L, L), 1)).astype(BF16)
    hr = lax.broadcasted_iota(I32, (LANES, LANES), 0) // RWKV_HEAD
    hc = lax.broadcasted_iota(I32, (LANES, LANES), 1) // RWKV_HEAD
    head_ones = (hr == hc).astype(BF16)
    nb = SCAN_LANE_BLOCKS

    def stack(x):
        return jnp.concatenate([x * m_a, x * m_b], axis=0)

    chunks = range(rows // L)
    rows_of = lambda c: slice(c * L, (c + 1) * L)
    lanes_of = lambda h: slice(h * LANES, (h + 1) * LANES)
    each = lambda f, *lists: [f(*vals) for vals in zip(*lists)]

    def operands(c, h):
        sl, ln = rows_of(c), lanes_of(h)
        r = r_ref[0, sl, ln].astype(F32)
        k = k_ref[0, sl, ln].astype(F32)
        v = v_ref[0, sl, ln].astype(F32)
        kk = kk_ref[0, sl, ln].astype(F32)
        a = a_ref[0, sl, ln].astype(F32)
        lw = lw_ref[0, sl, ln]
        hi, lo = _split_bf16(lw)
        cum = _dot(tri, hi) + _dot(tri, lo)
        c_end = cum[L - 1:L, :]
        p_inv = jnp.exp(-cum)
        p_end = jnp.exp(c_end - cum)
        b = kk * a
        return dict(
            x_a=stack(-(kk * jnp.exp(cum - lw))).astype(BF16),
            x_r=stack(r * jnp.exp(cum)),
            x_bk=jnp.concatenate([stack(b * p_inv), stack(k * p_inv)], axis=0).astype(BF16),
            v_st=stack(v).astype(BF16),
            z_hat=jnp.concatenate([stack(b * p_end), stack(k * p_end)], axis=0).astype(BF16),
            p_row=jnp.exp(c_end),
            rkr=(r * k * rk_ref[:, ln]).astype(BF16), v=v)

    ops = [operands(c, h) for c in chunks for h in range(nb)]
    big = [_dot_nt(jnp.concatenate([o["x_a"], o["x_r"].astype(BF16)], axis=0), o["x_bk"])
           for o in ops]
    n_pow = [jnp.where(strict, m[0:L2, 0:L2], 0.0) for m in big]
    a_ak = [jnp.where(strict, m[0:L2, L2:2 * L2], 0.0).astype(BF16) for m in big]
    a_r = [jnp.concatenate([jnp.where(incl, m[L2:2 * L2, 0:L2], 0.0),
                            jnp.where(incl, m[L2:2 * L2, L2:2 * L2], 0.0)], axis=1).astype(BF16)
           for m in big]

    t_inv = [eye + n for n in n_pow]
    for _ in range(5):
        n_pow = each(lambda n: _dot(n.astype(BF16), n.astype(BF16)), n_pow)
        t_inv = each(lambda t, n: t + _dot(t.astype(BF16), n.astype(BF16)), t_inv, n_pow)

    akv = each(lambda m, o: _dot(m, o["v_st"]).astype(BF16), a_ak, ops)
    w = each(lambda t, o, x: _dot(t.astype(BF16), jnp.concatenate([o["x_a"], x], axis=1)),
             t_inv, ops, akv)
    w1 = [m[:, 0:LANES].astype(BF16) for m in w]
    w2v = each(lambda m, o: jnp.concatenate([m[:, LANES:].astype(BF16), o["v_st"]], axis=0), w, ops)
    g_mat = each(lambda o, m, x: (o["x_r"] + _dot(m[:, 0:L2], x)).astype(BF16), ops, a_r, w1)
    y0 = each(_dot, a_r, w2v)
    m_mat = each(lambda x, o: _dot_tn(x, o["z_hat"][0:L2, :]).astype(BF16), w1, ops)
    c2 = each(lambda x, o: _dot_tn(x, o["z_hat"]), w2v, ops)
    bonus_v = [_dot(o["rkr"], head_ones) * o["v"] for o in ops]

    s = [s_scr[h] for h in range(nb)]
    y = []
    for i in range(len(ops)):
        h = i % nb
        s_b = s[h].astype(BF16)
        y_st = _dot_nt(g_mat[i], s_b) + y0[i]
        s[h] = s[h] * ops[i]["p_row"] + _dot(s_b, m_mat[i]) + c2[i]
        y.append(y_st[0:L, :] + y_st[L:L2, :])
    for h in range(nb):
        s_scr[h] = s[h]

    mean = [_dot(t.astype(BF16), head_ones) * (1.0 / RWKV_HEAD) for t in y]
    yc = each(lambda t, m: t - m, y, mean)
    var = [_dot((t * t).astype(BF16), head_ones) * (1.0 / RWKV_HEAD) for t in yc]
    for i in range(len(ops)):
        sl, ln = rows_of(i // nb), lanes_of(i % nb)
        yn = yc[i] * lax.rsqrt(var[i] + GN_EPS) * gnw_ref[:, ln] + gnb_ref[:, ln]
        o_ref[0, sl, ln] = ((yn + bonus_v[i]) * g_ref[0, sl, ln].astype(F32)).astype(o_ref.dtype)


def _rwkv_scan(r, k, v, kk, a, lw, g, gn_w, gn_b, r_k, rows):
    B, S, W = r.shape
    wb = SCAN_LANE_BLOCKS * LANES
    npair = W // wb
    seq = lambda: pl.BlockSpec((1, rows, wb), lambda b, p, c: (b, c, p))
    par = lambda: pl.BlockSpec((1, wb), lambda b, p, c: (0, p))
    return pl.pallas_call(
        functools.partial(_rwkv_scan_kernel, rows=rows),
        grid=(B, npair, S // rows),
        in_specs=[seq() for _ in range(7)] + [par(), par(), par()],
        out_specs=seq(),
        out_shape=jax.ShapeDtypeStruct((B, S, W), BF16),
        scratch_shapes=[pltpu.VMEM((SCAN_LANE_BLOCKS, LANES, LANES), F32)],
        compiler_params=_cparams(("parallel", "parallel", "arbitrary")),
        name="rwkv_scan",
    )(r, k, v, kk, a, lw, g, gn_w, gn_b, r_k)


def _rope128(x, cos, sin_signed):
    lane = lax.broadcasted_iota(I32, x.shape, 1)
    half = QK_ROPE // 2
    partner = jnp.where(lane < half, pltpu.roll(x, LANES - half, 1), pltpu.roll(x, half, 1))
    return x * cos + partner * sin_signed


def _mla_prep_kernel(cq_ref, ckv_ref, kr_ref, pos_ref, invf_ref, gqa_ref, gkva_ref,
                     wuq_ref, wukv_ref, gqn_ref, gkn_ref,
                     q_out, k_out, v_out, cqn_scr, ckvn_scr, cos_scr, sin_scr):
    @pl.when(pl.program_id(1) == 0)
    def _():
        cqn_scr[...] = _rms(cq_ref[...].astype(F32), gqa_ref[...]).astype(BF16)
        ckvn_scr[...] = _rms(ckv_ref[...].astype(F32), gkva_ref[...]).astype(BF16)
        ang = pos_ref[...].astype(F32) * invf_ref[...]
        lane = lax.broadcasted_iota(I32, ang.shape, 1)
        cos_scr[...] = jnp.cos(ang)
        sin_scr[...] = jnp.where(lane < QK_ROPE // 2, -1.0, 1.0) * jnp.sin(ang)

    tm = q_out.shape[0]
    hw = min(MLA_ROWS, tm)
    groups = [slice(g * hw, (g + 1) * hw) for g in range(tm // hw)]
    qs = [_dot(cqn_scr[r, :], wuq_ref[...]) for r in groups]
    kvs = [_dot(ckvn_scr[r, :], wukv_ref[...]) for r in groups]
    inv_n = 1.0 / QK_HEAD
    gkn = gkn_ref[...]
    for r, q, kv in zip(groups, qs, kvs):
        cos = cos_scr[r, :]
        sin = sin_scr[r, :]
        rs = lax.rsqrt(jnp.sum(q * q, axis=-1, keepdims=True) * inv_n + NORM_EPS)
        q = q * rs * gqn_ref[...] * (QK_HEAD ** -0.5 * LOG2E)
        q_out[r, :] = jnp.concatenate(
            [q[:, 0:LANES], _rope128(q[:, LANES:], cos, sin)], axis=1).astype(q_out.dtype)
        k_nope = kv[:, 0:QK_NOPE]
        kr = kr_ref[r, :].astype(F32)
        ssq = (jnp.sum(k_nope * k_nope, axis=-1, keepdims=True)
               + jnp.sum(kr * kr, axis=-1, keepdims=True))
        rs = lax.rsqrt(ssq * inv_n + NORM_EPS)
        k_out[r, :] = jnp.concatenate(
            [k_nope * rs * gkn[:, 0:LANES], _rope128(kr * rs * gkn[:, LANES:], cos, sin)],
            axis=1).astype(k_out.dtype)
        v_out[:, r] = jnp.transpose(kv[:, QK_NOPE:].astype(v_out.dtype))


def _mla_prep(proj, pos, invf, g_qa, g_kva, wuq, wukv, gqn, gkn, tm):
    T = proj.shape[0]
    H = MLA_HEADS
    c0 = COL_MLA // Q_LORA
    ckr = (COL_MLA + Q_LORA + KV_LORA) // LANES
    full = lambda shape: pl.BlockSpec(shape, lambda i, h: (0, 0))
    return pl.pallas_call(
        _mla_prep_kernel,
        grid=(T // tm, H),
        in_specs=[
            pl.BlockSpec((tm, Q_LORA), lambda i, h: (i, c0)),
            pl.BlockSpec((tm, KV_LORA), lambda i, h: (i, c0 + 1)),
            pl.BlockSpec((tm, LANES), lambda i, h: (i, ckr)),
            pl.BlockSpec((tm, 1), lambda i, h: (i, 0)),
            full((1, LANES)), full((1, Q_LORA)), full((1, KV_LORA)),
            pl.BlockSpec((Q_LORA, QK_PAD), lambda i, h: (0, h)),
            pl.BlockSpec((KV_LORA, QK_NOPE + V_HEAD), lambda i, h: (0, h)),
            full((1, QK_PAD)), full((1, QK_PAD)),
        ],
        out_specs=[
            pl.BlockSpec((tm, QK_PAD), lambda i, h: (i, h)),
            pl.BlockSpec((tm, QK_PAD), lambda i, h: (i, h)),
            pl.BlockSpec((V_HEAD, tm), lambda i, h: (h, i)),
        ],
        out_shape=[jax.ShapeDtypeStruct((T, H * QK_PAD), BF16),
                   jax.ShapeDtypeStruct((T, H * QK_PAD), BF16),
                   jax.ShapeDtypeStruct((H * V_HEAD, T), BF16)],
        scratch_shapes=[pltpu.VMEM((tm, Q_LORA), BF16), pltpu.VMEM((tm, KV_LORA), BF16),
                        pltpu.VMEM((tm, LANES), F32), pltpu.VMEM((tm, LANES), F32)],
        compiler_params=_cparams(("parallel", "arbitrary")),
        name="mla_prep",
    )(proj, proj, proj, pos, invf, g_qa, g_kva, wuq, wukv, gqn, gkn)


MASK_VALUE = -1e30
LOG2E = 1.4426950408889634


def _flash_kernel(q_ref, k_ref, vt_ref, o_ref, s_scr, p_scr, alpha_scr, m_scr, l_scr, acc_scr,
                  *, tq):
    i = pl.program_id(2)
    q = q_ref[0]
    m_scr[...] = jnp.full_like(m_scr, MASK_VALUE)
    l_scr[...] = jnp.zeros_like(l_scr)
    acc_scr[...] = jnp.zeros_like(acc_scr)

    q_half = tq // 2

    def scores(j, slot, half):
        ks = pl.ds(pl.multiple_of(j * tq, tq), tq)
        cols = slice(half * q_half, (half + 1) * q_half)
        s_scr[slot, :, cols] = _dot_nt(k_ref[0, ks, :], q[cols, :])

    def softmax(slot, masked, strips):
        for c in strips:
            cols = slice(c * LANES, (c + 1) * LANES)
            s = s_scr[slot, :, cols]
            if masked:
                kpos = lax.broadcasted_iota(I32, s.shape, 0)
                qpos = lax.broadcasted_iota(I32, s.shape, 1) + c * LANES
                s = jnp.where(kpos <= qpos, s, MASK_VALUE)
            m_old = m_scr[:, cols]
            m_new = jnp.maximum(m_old, jnp.max(s, axis=0, keepdims=True))
            p = jnp.exp2(s - m_new)
            alpha = jnp.exp2(m_old - m_new)
            l_scr[:, cols] = alpha * l_scr[:, cols] + jnp.sum(p, axis=0, keepdims=True)
            m_scr[:, cols] = m_new
            p_scr[slot, :, cols] = p.astype(BF16)
            alpha_scr[slot, :, cols] = alpha

    def weighted_values(j, slot):
        vt = vt_ref[:, pl.ds(pl.multiple_of(j * tq, tq), tq)]
        acc_scr[...] = alpha_scr[slot] * acc_scr[...] + _dot(vt, p_scr[slot])

    scores(0, 0, 0)
    scores(0, 0, 1)
    p_scr[1] = jnp.zeros((tq, tq), BF16)
    alpha_scr[1] = jnp.ones((1, tq), F32)
    n_strips = tq // LANES
    first, second = range(0, n_strips // 2), range(n_strips // 2, n_strips)

    def step(j, slot):
        weighted_values(jnp.maximum(j - 1, 0), 1 - slot)
        scores(j + 1, 1 - slot, 0)
        softmax(slot, False, first)
        scores(j + 1, 1 - slot, 1)
        softmax(slot, False, second)

    def pair(jp, carry):
        step(2 * jp, 0)
        step(2 * jp + 1, 1)
        return carry

    def finish(slot):
        weighted_values(jnp.maximum(i - 1, 0), 1 - slot)
        softmax(slot, True, range(n_strips))
        weighted_values(i, slot)
        o_ref[0] = jnp.transpose(acc_scr[...] / l_scr[...]).astype(o_ref.dtype)

    lax.fori_loop(0, i // 2, pair, 0)

    @pl.when(i % 2 == 0)
    def _():
        finish(0)

    @pl.when(i % 2 == 1)
    def _():
        step(i - 1, 0)
        finish(1)


def _flash(q, k, vt, tq):
    B, S, _ = q.shape
    H = MLA_HEADS
    return pl.pallas_call(
        functools.partial(_flash_kernel, tq=tq),
        grid=(B, H, S // tq),
        in_specs=[
            pl.BlockSpec((1, tq, QK_PAD), lambda b, h, i: (b, i, h)),
            pl.BlockSpec((1, S, QK_PAD), lambda b, h, i: (b, 0, h)),
            pl.BlockSpec((V_HEAD, S), lambda b, h, i: (h, b)),
        ],
        out_specs=pl.BlockSpec((1, tq, V_HEAD), lambda b, h, i: (b, i, h)),
        out_shape=jax.ShapeDtypeStruct((B, S, H * V_HEAD), BF16),
        scratch_shapes=[pltpu.VMEM((2, tq, tq), F32), pltpu.VMEM((2, tq, tq), BF16),
                        pltpu.VMEM((2, 1, tq), F32),
                        pltpu.VMEM((1, tq), F32), pltpu.VMEM((1, tq), F32),
                        pltpu.VMEM((V_HEAD, tq), F32)],
        compiler_params=_cparams(("parallel", "parallel", "arbitrary")),
        name="flash_attn",
    )(q, k, vt)


def _merge_kernel(oa_ref, ob_ref, wa_ref, wb_ref, ga_ref, gb_ref, o_ref):
    ya = _dot(oa_ref[...], wa_ref[...])
    yb = _dot(ob_ref[...], wb_ref[...])
    ga = _sigmoid(ga_ref[...].astype(F32))
    gb = _sigmoid(gb_ref[...].astype(F32))
    o_ref[...] = (ga * ya + gb * yb).astype(o_ref.dtype)


def _merge(oa, ob, wa, wb, proj, tm, tn):
    T, K = oa.shape
    N = wa.shape[1]
    ca = COL_GA // tn
    cb = COL_GB // tn
    return pl.pallas_call(
        _merge_kernel,
        grid=(T // tm, N // tn),
        in_specs=[
            pl.BlockSpec((tm, K), lambda i, j: (i, 0)),
            pl.BlockSpec((tm, K), lambda i, j: (i, 0)),
            pl.BlockSpec((K, tn), lambda i, j: (0, j)),
            pl.BlockSpec((K, tn), lambda i, j: (0, j)),
            pl.BlockSpec((tm, tn), lambda i, j: (i, ca + j)),
            pl.BlockSpec((tm, tn), lambda i, j: (i, cb + j)),
        ],
        out_specs=pl.BlockSpec((tm, tn), lambda i, j: (i, j)),
        out_shape=jax.ShapeDtypeStruct((T, N), BF16),
        compiler_params=_cparams(("parallel", "parallel")),
        name="merge",
    )(oa, ob, wa, wb, proj, proj)


def _pack_rows(x, out_ref, n, row0=0):
    for c in range(PACK_ROWS):
        lo = x[:, c * 2 * LANES:c * 2 * LANES + LANES]
        hi = x[:, c * 2 * LANES + LANES:(c + 1) * 2 * LANES]
        lo_b = pltpu.bitcast(lo.astype(BF16).astype(F32), U32)
        hi_b = pltpu.bitcast(hi.astype(BF16).astype(F32), U32)
        out_ref[pl.ds(row0 * PACK_ROWS + c, n, stride=PACK_ROWS), :] = hi_b | (lo_b >> 16)


def _unpack_rows(ref, c, tm, lead=None):
    idx = (pl.ds(c, tm, stride=PACK_ROWS), slice(None))
    if lead is not None:
        idx = (lead,) + idx
    u = ref[idx]
    lo = pltpu.bitcast(u << 16, F32)
    hi = pltpu.bitcast(u & jnp.uint32(0xFFFF0000), F32)
    return lo, hi


def _wo_kernel(m_ref, x_ref, wo_ref, g_ref, wr_ref, x1_ref, h2_ref, h2p_ref, lg_ref, *, tm):
    hw = WO_ROWS
    groups = [slice(h * hw, (h + 1) * hw) for h in range(tm // hw)]
    wo = wo_ref[...]
    x1 = [x_ref[r, :] + _dot(m_ref[r, :], wo) for r in groups]
    h2 = [_rms(v, g_ref[...]) for v in x1]
    w_hi, w_lo = _split_bf16(wr_ref[...])
    for h, r in enumerate(groups):
        x1_ref[r, :] = x1[h]
        h2_ref[r, :] = h2[h].astype(BF16)
        _pack_rows(h2[h], h2p_ref, hw, row0=h * hw)
        h_hi, h_lo = _split_bf16(h2[h])
        lg_ref[:, r] = _dot_nt(w_hi, h_hi) + _dot_nt(w_hi, h_lo) + _dot_nt(w_lo, h_hi)


def _wo(merged, x2, wo, g, wr_t, tm):
    T, D = x2.shape
    E = wr_t.shape[0]
    return pl.pallas_call(
        functools.partial(_wo_kernel, tm=tm),
        grid=(T // tm,),
        in_specs=[
            pl.BlockSpec((tm, D), lambda i: (i, 0)),
            pl.BlockSpec((tm, D), lambda i: (i, 0)),
            pl.BlockSpec((D, D), lambda i: (0, 0), pipeline_mode=pl.Buffered(1)),
            pl.BlockSpec((1, D), lambda i: (0, 0)),
            pl.BlockSpec((E, D), lambda i: (0, 0)),
        ],
        out_specs=[
            pl.BlockSpec((tm, D), lambda i: (i, 0)),
            pl.BlockSpec((tm, D), lambda i: (i, 0)),
            pl.BlockSpec((tm * PACK_ROWS, PACK_W), lambda i: (i, 0)),
            pl.BlockSpec((E, tm), lambda i: (0, i)),
        ],
        out_shape=[jax.ShapeDtypeStruct((T, D), F32),
                   jax.ShapeDtypeStruct((T, D), BF16),
                   jax.ShapeDtypeStruct((T * PACK_ROWS, PACK_W), U32),
                   jax.ShapeDtypeStruct((E, T), F32)],
        compiler_params=_cparams(("parallel",)),
        name="wo_norm_router",
    )(merged, x2, wo, g, wr_t)


def _route_kernel(lg_ref, bias_ref, eidx_ref, wts_ref, rnk_ref, cnt_ref, carry_scr, *, tm):
    E = N_EXPERTS
    NEG = -jnp.inf

    @pl.when(pl.program_id(0) == 0)
    def _():
        carry_scr[...] = jnp.zeros_like(carry_scr)

    scores = _sigmoid(lg_ref[...])
    choice = scores + bias_ref[...]
    c3 = choice.reshape(N_GROUPS, GROUP_SIZE, tm)
    sub = lax.broadcasted_iota(I32, c3.shape, 1)
    m1 = jnp.max(c3, axis=1, keepdims=True)
    i1 = jnp.min(jnp.where(c3 == m1, sub, GROUP_SIZE), axis=1, keepdims=True)
    m2 = jnp.max(jnp.where(sub == i1, NEG, c3), axis=1, keepdims=True)
    gs = (m1 + m2).reshape(N_GROUPS, tm)
    gi = lax.broadcasted_iota(I32, gs.shape, 0)
    gsel = jnp.zeros(gs.shape, F32)
    for _ in range(TOPK_GROUPS):
        mx = jnp.max(gs, axis=0, keepdims=True)
        ix = jnp.min(jnp.where(gs == mx, gi, N_GROUPS), axis=0, keepdims=True)
        hit = gi == ix
        gsel = jnp.where(hit, 1.0, gsel)
        gs = jnp.where(hit, NEG, gs)
    emask = jnp.broadcast_to(gsel.reshape(N_GROUPS, 1, tm), (N_GROUPS, GROUP_SIZE, tm)).reshape(E, tm)
    x = jnp.where(emask > 0.5, choice, NEG)
    ei = lax.broadcasted_iota(I32, x.shape, 0)
    sel = jnp.zeros(x.shape, F32)
    idx_rows, w_rows = [], []
    for _ in range(TOP_K):
        mx = jnp.max(x, axis=0, keepdims=True)
        ix = jnp.min(jnp.where(x == mx, ei, E), axis=0, keepdims=True)
        hit = ei == ix
        w_rows.append(jnp.sum(jnp.where(hit, scores, 0.0), axis=0, keepdims=True))
        idx_rows.append(ix)
        sel = jnp.where(hit, 1.0, sel)
        x = jnp.where(hit, NEG, x)
    w = jnp.concatenate(w_rows, axis=0)
    w = w / (jnp.sum(w, axis=0, keepdims=True) + 1e-20) * ROUTED_SCALE
    eidx_ref[...] = jnp.concatenate(idx_rows, axis=0)
    wts_ref[...] = w

    upper = (lax.broadcasted_iota(I32, (tm, tm), 0) < lax.broadcasted_iota(I32, (tm, tm), 1)).astype(BF16)
    base = carry_scr[...][:, 0:1]
    excl = _dot(sel.astype(BF16), upper) + base
    rnk_ref[...] = jnp.concatenate(
        [jnp.sum(jnp.where(ei == ix, excl, 0.0), axis=0, keepdims=True) for ix in idx_rows],
        axis=0).astype(I32)
    carry_scr[...] = carry_scr[...] + jnp.sum(sel, axis=1, keepdims=True)
    cnt_ref[...] = carry_scr[...]


def _route(logits_t, bias, tm):
    E, T = logits_t.shape
    K = TOP_K
    tok = lambda: pl.BlockSpec((K, tm), lambda i: (0, i))
    return pl.pallas_call(
        functools.partial(_route_kernel, tm=tm),
        grid=(T // tm,),
        in_specs=[pl.BlockSpec((E, tm), lambda i: (0, i)),
                  pl.BlockSpec((E, 1), lambda i: (0, 0))],
        out_specs=[tok(), tok(), tok(), pl.BlockSpec((E, LANES), lambda i: (0, 0))],
        out_shape=[jax.ShapeDtypeStruct((K, T), I32), jax.ShapeDtypeStruct((K, T), F32),
                   jax.ShapeDtypeStruct((K, T), I32), jax.ShapeDtypeStruct((E, LANES), F32)],
        scratch_shapes=[pltpu.VMEM((E, LANES), F32)],
        compiler_params=_cparams(("arbitrary",)),
        name="route",
    )(logits_t, bias)


def _slots_kernel(pst_ref, eidx_ref, rnk_ref, dest_ref):
    eidx = eidx_ref[...]
    dest = rnk_ref[...]
    for e in range(N_EXPERTS):
        dest = dest + jnp.where(eidx == e, pst_ref[e], 0)
    dest_ref[...] = dest


def _slots(eidx, rnk, pstart, tm):
    K, T = eidx.shape
    tok = lambda: pl.BlockSpec((K, tm), lambda i: (0, i))
    return pl.pallas_call(
        _slots_kernel,
        grid=(T // tm,),
        in_specs=[pl.BlockSpec(memory_space=pltpu.SMEM), tok(), tok()],
        out_specs=tok(),
        out_shape=jax.ShapeDtypeStruct((K, T), I32),
        compiler_params=_cparams(("parallel",)),
        name="slots",
    )(pstart, eidx, rnk)


def _ffn_kernel(te_ref, nu_ref, tok_ref, dst_ref, h_hbm, wg_ref, wu_ref, wd_ref, ys_hbm,
                gbuf, obuf, wg_b, wu_b, wd_b, gsem, osem, *, tm):
    s = pl.program_id(0)
    nu = nu_ref[0]
    tile_rows = tm * PACK_ROWS

    def gather(slot):
        for r in range(tm):
            pltpu.make_async_copy(
                h_hbm.at[pl.ds(pl.multiple_of(tok_ref[0, 0, r] * PACK_ROWS, PACK_ROWS), PACK_ROWS), :],
                gbuf.at[slot, pl.ds(r * PACK_ROWS, PACK_ROWS), :], gsem.at[slot]
            ).start(priority=r % 2)

    def scatter(slot):
        for r in range(tm):
            pltpu.make_async_copy(
                obuf.at[slot, pl.ds(r * PACK_ROWS, PACK_ROWS), :],
                ys_hbm.at[pl.ds(pl.multiple_of(dst_ref[0, 0, r] * PACK_ROWS, PACK_ROWS), PACK_ROWS), :],
                osem.at[slot]).start(priority=r % 2)

    def wait_gather(slot):
        pltpu.make_async_copy(h_hbm.at[pl.ds(0, tile_rows), :], gbuf.at[slot], gsem.at[slot]).wait()

    def wait_scatter(slot):
        pltpu.make_async_copy(obuf.at[slot], ys_hbm.at[pl.ds(0, tile_rows), :], osem.at[slot]).wait()

    def compute(slot):
        parts = []
        for c in range(PACK_ROWS):
            lo, hi = _unpack_rows(gbuf.at[slot], c, tm)
            parts += [lo.astype(BF16), hi.astype(BF16)]
        a = jnp.concatenate(parts, axis=1)
        hg = _dot(a, wg_b[...])
        hu = _dot(a, wu_b[...])
        hid = (hg * _sigmoid(hg) * hu).astype(BF16)
        _pack_rows(_dot(hid, wd_b[...]), obuf.at[slot], tm)

    cur = jnp.maximum(s - 1, 0)
    @pl.when((s >= 1) & (s <= nu) & ((s == 1) | (te_ref[cur] != te_ref[jnp.maximum(s - 2, 0)])))
    def _():
        wg_b[...] = wg_ref[0].astype(BF16)
        wu_b[...] = wu_ref[0].astype(BF16)
        wd_b[...] = wd_ref[0].astype(BF16)

    @pl.when(s == 0)
    def _():
        gather(0)

    @pl.when(s == 1)
    def _():
        wait_gather(0)
        gather(1)
        compute(0)

    for slot in range(2):
        parity = s % 2 == slot

        @pl.when(parity & (s >= 3) & (s <= nu + 1))
        def _():
            wait_scatter(1 - slot)

        @pl.when(parity & (s >= 2) & (s <= nu))
        def _():
            wait_gather(1 - slot)
            gather(slot)
            scatter(slot)
            compute(1 - slot)

        @pl.when(parity & (s >= 2) & (s == nu + 1))
        def _():
            wait_gather(1 - slot)
            scatter(slot)
            wait_scatter(slot)


def _ffn(tile_e, n_used, row_tok, row_dst, h2p, wg, wu, wd, tm):
    nt = row_tok.shape[0]
    D, I = wg.shape[1], wg.shape[2]
    cur = lambda s: jnp.maximum(s - 1, 0)
    wspec = lambda shape: pl.BlockSpec(shape, lambda s, te, nu: (te[cur(s)], 0, 0))
    grid_spec = pltpu.PrefetchScalarGridSpec(
        num_scalar_prefetch=2,
        grid=(nt + 1,),
        in_specs=[
            pl.BlockSpec((1, 1, tm), lambda s, te, nu: (jnp.minimum(s, nt - 1), 0, 0),
                         memory_space=pltpu.SMEM),
            pl.BlockSpec((1, 1, tm), lambda s, te, nu: (jnp.maximum(s - 2, 0), 0, 0),
                         memory_space=pltpu.SMEM),
            pl.BlockSpec(memory_space=pl.ANY),
            wspec((1, D, I)), wspec((1, D, I)), wspec((1, I, D)),
        ],
        out_specs=pl.BlockSpec(memory_space=pl.ANY),
        scratch_shapes=[pltpu.VMEM((2, tm * PACK_ROWS, PACK_W), U32),
                        pltpu.VMEM((2, tm * PACK_ROWS, PACK_W), U32),
                        pltpu.VMEM((D, I), BF16), pltpu.VMEM((D, I), BF16), pltpu.VMEM((I, D), BF16),
                        pltpu.SemaphoreType.DMA((2,)), pltpu.SemaphoreType.DMA((2,))],
    )
    return pl.pallas_call(
        functools.partial(_ffn_kernel, tm=tm),
        grid_spec=grid_spec,
        out_shape=jax.ShapeDtypeStruct((nt * tm * PACK_ROWS, PACK_W), U32),
        compiler_params=_cparams(("arbitrary",)),
        name="expert_ffn",
    )(tile_e, n_used, row_tok, row_dst, h2p, wg, wu, wd)


def _combine_kernel(*refs, tm):
    ys_refs = refs[:TOP_K]
    wt_ref, x1_ref, h2_ref, wsg_ref, wsu_ref, wsd_ref, o_ref = refs[TOP_K:]
    h2 = h2_ref[...]
    hg = _dot(h2, wsg_ref[...])
    hu = _dot(h2, wsu_ref[...])
    base = x1_ref[...] + _dot((hg * _sigmoid(hg) * hu).astype(BF16), wsd_ref[...])
    wt = wt_ref[...]
    for c in range(PACK_ROWS):
        acc_lo = base[:, c * 2 * LANES:c * 2 * LANES + LANES]
        acc_hi = base[:, c * 2 * LANES + LANES:(c + 1) * 2 * LANES]
        for k in range(TOP_K):
            lo, hi = _unpack_rows(ys_refs[k], c, tm)
            wk = wt[:, k:k + 1]
            acc_lo = acc_lo + wk * lo
            acc_hi = acc_hi + wk * hi
        o_ref[:, c * 2 * LANES:c * 2 * LANES + LANES] = acc_lo
        o_ref[:, c * 2 * LANES + LANES:(c + 1) * 2 * LANES] = acc_hi


def _combine(ys, wts_t, x1, h2, wsg, wsu, wsd, tm):
    T, D = x1.shape
    K = wts_t.shape[1]
    I = wsg.shape[1]
    full = lambda shape: pl.BlockSpec(shape, lambda i: (0, 0))
    return pl.pallas_call(
        functools.partial(_combine_kernel, tm=tm),
        grid=(T // tm,),
        in_specs=[pl.BlockSpec((tm * PACK_ROWS, PACK_W), lambda i, k=k: (k * (T // tm) + i, 0))
                  for k in range(K)] + [
                  pl.BlockSpec((tm, K), lambda i: (i, 0)),
                  pl.BlockSpec((tm, D), lambda i: (i, 0)),
                  pl.BlockSpec((tm, D), lambda i: (i, 0)),
                  full((D, I)), full((D, I)), full((I, D))],
        out_specs=pl.BlockSpec((tm, D), lambda i: (i, 0)),
        out_shape=jax.ShapeDtypeStruct((T, D), F32),
        compiler_params=_cparams(("parallel",)),
        name="combine",
    )(*([ys] * K), wts_t, x1, h2, wsg, wsu, wsd)


def _ple_kernel(x_ref, p_ref, wp_ref, gpost_ref, gin_ref, wg_ref, o_ref, *, tm):
    hw = WO_ROWS
    groups = [slice(h * hw, (h + 1) * hw) for h in range(tm // hw)]
    wg = wg_ref[...]
    gates, ples = [], []
    for r in groups:
        hn = _rms(x_ref[r, :], gin_ref[...]).astype(BF16)
        gates.append(_dot(hn, wg))
        ples.append(_rms(_dot(p_ref[r, :].astype(BF16), wp_ref[...]), gpost_ref[...]))
    for r, gate, ple in zip(groups, gates, ples):
        o_ref[r, :] = x_ref[r, :] + _sigmoid(gate) * ple


def _ple(x2, p2, wp, gpost, gin, wg, tm):
    T, D = x2.shape
    Pd = p2.shape[1]
    full = lambda shape: pl.BlockSpec(shape, lambda i: (0, 0))
    return pl.pallas_call(
        functools.partial(_ple_kernel, tm=tm),
        grid=(T // tm,),
        in_specs=[
            pl.BlockSpec((tm, D), lambda i: (i, 0)),
            pl.BlockSpec((tm, Pd), lambda i: (i, 0)),
            full((Pd, D)), full((1, D)), full((1, D)),
            pl.BlockSpec((D, D), lambda i: (0, 0), pipeline_mode=pl.Buffered(1)),
        ],
        out_specs=pl.BlockSpec((tm, D), lambda i: (i, 0)),
        out_shape=jax.ShapeDtypeStruct((T, D), F32),
        compiler_params=_cparams(("parallel",)),
        name="ple",
    )(x2, p2, wp, gpost, gin, wg)


def _pad_cols(w, n):
    return jnp.pad(w, ((0, 0), (0, n - w.shape[1])))


def _pad_rows(w, n):
    return jnp.pad(w, ((0, n - w.shape[0]), (0, 0)))


def _layout_w_in(w):
    W = RWKV_WIDTH
    c = 3 * W
    segs = [w[:, 0:c],
            _pad_cols(w[:, c:c + DECAY_LORA], LORA_PAD),
            _pad_cols(w[:, c + DECAY_LORA:c + DECAY_LORA + AAA_LORA], LORA_PAD),
            _pad_cols(w[:, c + DECAY_LORA + AAA_LORA:c + DECAY_LORA + AAA_LORA + GATE_LORA], GATE_LORA_PAD)]
    c += DECAY_LORA + AAA_LORA + GATE_LORA
    mla_cols = Q_LORA + KV_LORA + QK_ROPE
    segs.append(_pad_cols(w[:, c:c + mla_cols], MLA_PAD))
    c += mla_cols
    segs.append(w[:, c:])
    out = jnp.concatenate(segs, axis=1).astype(BF16)
    assert out.shape[1] == IN_PAD
    return out


def _layout_mu(mu):
    W = RWKV_WIDTH
    c = 3 * W
    segs = [mu[0:c],
            jnp.pad(mu[c:c + DECAY_LORA], (0, LORA_PAD - DECAY_LORA)),
            jnp.pad(mu[c + DECAY_LORA:c + DECAY_LORA + AAA_LORA], (0, LORA_PAD - AAA_LORA)),
            jnp.pad(mu[c + DECAY_LORA + AAA_LORA:], (0, GATE_LORA_PAD - GATE_LORA))]
    return jnp.concatenate(segs).reshape(1, RWKV_PAD)


def _moe_tables(eidx, rnk, counts, tmf, tm_slots):
    K, T = eidx.shape
    E = N_EXPERTS
    A = K * T
    n_tiles = A // tmf + E
    P = n_tiles * tmf
    pcounts = (counts + tmf - 1) // tmf * tmf
    pends = jnp.cumsum(pcounts)
    pstart = (pends - pcounts).astype(I32)
    tile_start = jnp.arange(n_tiles, dtype=I32) * tmf
    tile_e = jnp.minimum(jnp.sum((pends[None, :] <= tile_start[:, None]).astype(I32), axis=1), E - 1)
    n_used = (pends[E - 1:] // tmf).astype(I32)
    dest = _slots(eidx, rnk, pstart, tm_slots)
    j = jnp.arange(tmf, dtype=I32)[None, :]
    tail = j < (pcounts - counts)[:, None]
    spare = jnp.cumsum(jnp.logical_not(tail).reshape(-1).astype(I32)) - 1
    pad_row = jnp.where(tail.reshape(-1), ((pstart + counts)[:, None] + j).reshape(-1),
                        pends[E - 1] + spare)
    q = jnp.arange(P - A, dtype=I32)
    slot_id = jnp.arange(A, dtype=I32)
    rows = jnp.concatenate([dest.reshape(-1), pad_row.astype(I32)])
    slots = jnp.concatenate([slot_id, A + q])
    row_dst = lax.sort((rows, slots), num_keys=1)[1]
    row_tok = jnp.where(row_dst < A, row_dst % T, 0)
    return (row_tok.reshape(n_tiles, 1, tmf), row_dst.reshape(n_tiles, 1, tmf), tile_e, n_used)


def _layer(x, p, positions, g_mix, w_in, mu_rwkv, w0, w2, a0, a2, g2, k_k, k_a, r_k,
           gn_w, gn_b, w_a_up, g_qa, g_kva, w_uq, w_ukv, g_qn, g_kn, w_b_up, w_o,
           g_ffn, w_router, router_bias, w_exp_gate, w_exp_up, w_exp_down,
           w_sh_gate, w_sh_up, w_sh_down, w_ple, g_ple_post, g_ple_in, w_ple_gate,
           *, tiles):
    B, S, D = x.shape
    T = B * S
    W = RWKV_WIDTH
    x2 = x.reshape(T, D)
    row = lambda v: v.reshape(1, -1).astype(F32)

    proj = _inproj(x2, row(g_mix), _layout_w_in(w_in), tiles["tm_in"], tiles["tn_in"])

    head_blk = (jnp.arange(W)[:, None] // RWKV_HEAD == jnp.arange(W)[None, :] // RWKV_HEAD).astype(BF16)
    r, k, v, kk, a, lw, g = _rwkv_prep(
        proj, _layout_mu(mu_rwkv), row(w0), _pad_rows(w2, LORA_PAD).astype(BF16), row(a0),
        _pad_rows(a2, LORA_PAD).astype(BF16), _pad_rows(g2, GATE_LORA_PAD).astype(BF16),
        row(k_k), row(k_a), head_blk, tiles["tm_prep"], S)
    b3 = lambda t: t.reshape(B, S, W)
    o_a = _rwkv_scan(b3(r), b3(k), b3(v), b3(kk), b3(a), b3(lw), b3(g),
                     row(gn_w), row(gn_b), row(r_k), tiles["rows_scan"]).reshape(T, W)

    half = QK_ROPE // 2
    inv_freq = ROPE_THETA ** (-jnp.arange(half, dtype=F32) / half)
    invf = jnp.concatenate([inv_freq, inv_freq, jnp.zeros((LANES - QK_ROPE,), F32)]).reshape(1, LANES)
    wuq = jnp.pad(w_uq.reshape(Q_LORA, MLA_HEADS, QK_HEAD),
                  ((0, 0), (0, 0), (0, QK_PAD - QK_HEAD))).reshape(Q_LORA, MLA_HEADS * QK_PAD)
    pad_g = lambda gv: jnp.pad(gv, (0, QK_PAD - QK_HEAD)).reshape(1, QK_PAD)
    q, kx, vx = _mla_prep(proj, positions.reshape(T, 1).astype(I32), invf, row(g_qa), row(g_kva),
                          wuq.astype(BF16), w_ukv.astype(BF16), pad_g(g_qn), pad_g(g_kn),
                          tiles["tm_mla"])
    o_b = _flash(q.reshape(B, S, -1), kx.reshape(B, S, -1), vx, tiles["tq"]).reshape(T, MLA_WIDTH)

    merged = _merge(o_a, o_b, w_a_up.astype(BF16), w_b_up.astype(BF16), proj,
                    tiles["tm_merge"], tiles["tn_merge"])

    x1, h2, h2p, logits_t = _wo(merged, x2, w_o.astype(BF16), row(g_ffn),
                                jnp.transpose(w_router).astype(F32), tiles["tm_wo"])
    eidx, wts, rnk, cnt = _route(logits_t, router_bias.reshape(N_EXPERTS, 1).astype(F32),
                                 tiles["tm_route"])
    tmf = tiles["tm_ffn"]
    counts = cnt[:, 0].astype(I32)
    row_tok, row_dst, tile_e, n_used = _moe_tables(eidx, rnk, counts, tmf, tiles["tm_slots"])
    ys = _ffn(tile_e, n_used, row_tok, row_dst, h2p, w_exp_gate, w_exp_up, w_exp_down, tmf)
    x3 = _combine(ys, jnp.transpose(wts), x1, h2,
                  w_sh_gate.astype(BF16), w_sh_up.astype(BF16), w_sh_down.astype(BF16),
                  tiles["tm_comb"])

    out = _ple(x3, p.reshape(T, PLE_DIM), w_ple.astype(BF16), row(g_ple_post), row(g_ple_in),
               w_ple_gate.astype(BF16), tiles["tm_ple"])
    return out.reshape(B, S, D)


TILES = dict(tm_in=1024, tn_in=1536, tm_prep=512, rows_scan=128, tm_mla=2048, tq=512,
             tm_merge=1024, tn_merge=1024, tm_wo=512, tm_route=1024, tm_ffn=256,
             tm_slots=2048, tm_comb=256, tm_ple=1024)


def kernel(x, p, positions, g_mix, w_in, mu_rwkv, w0, w2, a0, a2, g2, k_k, k_a, r_k, gn_w, gn_b, w_a_up, g_qa, g_kva, w_uq, w_ukv, g_qn, g_kn, w_b_up, w_o, g_ffn, w_router, router_bias, w_exp_gate, w_exp_up, w_exp_down, w_sh_gate, w_sh_up, w_sh_down, w_ple, g_ple_post, g_ple_in, w_ple_gate):
    args = (g_mix, w_in, mu_rwkv, w0, w2, a0, a2, g2, k_k, k_a, r_k, gn_w, gn_b, w_a_up,
            g_qa, g_kva, w_uq, w_ukv, g_qn, g_kn, w_b_up, w_o, g_ffn, w_router, router_bias,
            w_exp_gate, w_exp_up, w_exp_down, w_sh_gate, w_sh_up, w_sh_down, w_ple,
            g_ple_post, g_ple_in, w_ple_gate)
    assert all(t.shape[0] == 1 for t in args), "single-layer stack expected"
    return _layer(x, p[0], positions, *[t[0] for t in args], tiles=TILES)
```
